```python
import jax, jax.numpy as jnp
from jax import lax
import numpy as np

D_MODEL = 2048
BATCH = 8
SEQ = 8192
DEPTH = 2

D_MIX = D_MODEL
D_A = D_MIX // 4
D_B = D_MIX // 4
D_C = D_MIX // 4
D_D = D_MIX // 4
GDN_HEAD_DIM = 128
GDN_HEADS = D_A // GDN_HEAD_DIM
GDN_CONV = 4
GDN_CHUNK = 64
GDN_CHUNK_LOG2 = 6
LRU_BLOCKS = 8
LRU_BLOCK_DIM = D_B // LRU_BLOCKS
LRU_CONV = 4
LRU_C = 8.0
SGU_GROUPS = 4
SGU_GROUP_DIM = D_C // SGU_GROUPS
SGU_CHUNK = 128
SCONV_WIDTH = 3
D_FF = (D_MODEL * 11) // 4
FFN_CONV = 3
IN_SIZES = (D_A, D_A, D_A, D_A, GDN_HEADS, GDN_HEADS, D_B, D_B, 2 * D_C, D_D, D_D, D_D)
N_IN = 4 * D_A + 2 * GDN_HEADS + 2 * D_B + 2 * D_C + 3 * D_D
EPS = 1e-6

kernel_name = "hybrid_parallel_head_groups_gdn_rglru_sgu_shortconv"


def _split(t, sizes):
    idx, acc = [], 0
    for s in sizes[:-1]:
        acc += s
        idx.append(acc)
    return jnp.split(t, idx, axis=-1)


def rms_norm(x, w):
    xf = x.astype(jnp.float32)
    y = xf * lax.rsqrt(jnp.mean(xf * xf, axis=-1, keepdims=True) + EPS)
    return (y * w.astype(jnp.float32)).astype(x.dtype)


def layer_norm(x, w, b):
    xf = x.astype(jnp.float32)
    mu = jnp.mean(xf, axis=-1, keepdims=True)
    xc = xf - mu
    y = xc * lax.rsqrt(jnp.mean(xc * xc, axis=-1, keepdims=True) + EPS)
    return y * w.astype(jnp.float32) + b.astype(jnp.float32)


def causal_dwconv(x, w, b=None):
    K, C = w.shape
    y = lax.conv_general_dilated(x, w[:, None, :].astype(x.dtype), window_strides=(1,),
                                 padding=[(K - 1, 0)], dimension_numbers=('NWC', 'WIO', 'NWC'),
                                 feature_group_count=C)
    if b is not None:
        y = y + b.astype(x.dtype)
    return y


def l2norm(t):
    return t * lax.rsqrt(jnp.sum(t * t, axis=-1, keepdims=True) + EPS)


def chunk_gated_delta_rule(q, k, v, g, beta):
    Bn, S, H, Dk = q.shape
    Dv = v.shape[-1]
    nC = S // GDN_CHUNK
    C = GDN_CHUNK
    q = l2norm(q) * (Dk ** -0.5)
    k = l2norm(k)

    def chunks(t):
        return t.reshape(Bn, nC, C, H, -1).transpose(0, 3, 1, 2, 4)

    qc, kc, vc = chunks(q), chunks(k), chunks(v)
    gc = g.reshape(Bn, nC, C, H).transpose(0, 3, 1, 2)
    bc = beta.reshape(Bn, nC, C, H).transpose(0, 3, 1, 2)
    gcum = jnp.cumsum(gc, axis=-1)
    causal = jnp.tril(jnp.ones((C, C), dtype=bool))
    strict = jnp.tril(jnp.ones((C, C), dtype=bool), -1)
    diff = gcum[..., :, None] - gcum[..., None, :]
    decay = jnp.where(causal, jnp.exp(jnp.where(causal, diff, 0.0)), 0.0)
    kb = kc * bc[..., None]
    M = jnp.where(strict, jnp.einsum('bhncd,bhnsd->bhncs', kb, kc) * decay, 0.0)
    N = -M
    T = jnp.eye(C, dtype=jnp.float32) + N
    P = N
    for _ in range(GDN_CHUNK_LOG2 - 1):
        P = jnp.einsum('bhnij,bhnjk->bhnik', P, P)
        T = T + jnp.einsum('bhnij,bhnjk->bhnik', T, P)
    w = jnp.einsum('bhncs,bhnsd->bhncd', T, kb * jnp.exp(gcum)[..., None])
    u = jnp.einsum('bhncs,bhnsd->bhncd', T, vc * bc[..., None])
    attn = jnp.where(causal, jnp.einsum('bhncd,bhnsd->bhncs', qc, kc) * decay, 0.0)
    q_g = qc * jnp.exp(gcum)[..., None]
    k_g = kc * jnp.exp(gcum[..., -1:] - gcum)[..., None]
    g_last = jnp.exp(gcum[..., -1])

    def step(state, inp):
        q_i, a_i, u_i, w_i, k_i, gl_i = inp
        v_new = u_i - jnp.einsum('bhck,bhkv->bhcv', w_i, state)
        o_i = jnp.einsum('bhck,bhkv->bhcv', q_i, state) + jnp.einsum('bhcs,bhsv->bhcv', a_i, v_new)
        state = state * gl_i[..., None, None] + jnp.einsum('bhck,bhcv->bhkv', k_i, v_new)
        return state, o_i

    xs = tuple(jnp.moveaxis(t, 2, 0) for t in (q_g, attn, u, w, k_g, g_last))
    s0 = jnp.zeros((Bn, H, Dk, Dv), jnp.float32)
    _, o = lax.scan(step, s0, xs)
    return o.transpose(1, 0, 3, 2, 4).reshape(Bn, S, H, Dv)


def gdn_mixer(q, k, v, z, b, a, conv_w, a_log, dt_bias, norm_w):
    Bn, S, _ = q.shape
    qkv = jax.nn.silu(causal_dwconv(jnp.concatenate([q, k, v], axis=-1), conv_w))
    q, k, v = jnp.split(qkv, 3, axis=-1)

    def heads(t):
        return t.reshape(Bn, S, GDN_HEADS, GDN_HEAD_DIM).astype(jnp.float32)

    beta = jax.nn.sigmoid(b.astype(jnp.float32))
    g = -jnp.exp(a_log.astype(jnp.float32)) * jax.nn.softplus(a.astype(jnp.float32) + dt_bias.astype(jnp.float32))
    o = chunk_gated_delta_rule(heads(q), heads(k), heads(v), g, beta)
    o = o * lax.rsqrt(jnp.mean(o * o, axis=-1, keepdims=True) + EPS) * norm_w.astype(jnp.float32) * jax.nn.silu(heads(z))
    return o.reshape(Bn, S, D_A).astype(z.dtype)


def lru_scan(a, b):
    def combine(l, r):
        return (l[0] * r[0], r[0] * l[1] + r[1])
    _, h = lax.associative_scan(combine, (a, b), axis=1)
    return h


def rglru_mixer(xb, gate, conv_w, conv_b, wa, ba, wx, bx, lam):
    Bn, S, _ = xb.shape
    xc = causal_dwconv(xb, conv_w, conv_b).astype(jnp.float32)
    xblk = xc.reshape(Bn, S, LRU_BLOCKS, LRU_BLOCK_DIM)
    r = jax.nn.sigmoid(jnp.einsum('bshi,hij->bshj', xblk, wa.astype(jnp.float32)) + ba.astype(jnp.float32)).reshape(Bn, S, D_B)
    i = jax.nn.sigmoid(jnp.einsum('bshi,hij->bshj', xblk, wx.astype(jnp.float32)) + bx.astype(jnp.float32)).reshape(Bn, S, D_B)
    log_a = -LRU_C * r * jax.nn.softplus(-lam.astype(jnp.float32))
    a = jnp.exp(log_a)
    mult = jnp.sqrt(-jnp.expm1(2.0 * log_a))
    h = lru_scan(a, mult * (i * xc))
    return (h * jax.nn.gelu(gate.astype(jnp.float32))).astype(xb.dtype)


def sgu_mixer(uv, ln_w, ln_b, ws, bs):
    Bn, S, _ = uv.shape
    uvf = jax.nn.gelu(uv.astype(jnp.float32))
    u, v = jnp.split(uvf, 2, axis=-1)
    v = layer_norm(v, ln_w, ln_b)
    v = v.reshape(Bn, S // SGU_CHUNK, SGU_CHUNK, SGU_GROUPS, SGU_GROUP_DIM)
    mask = jnp.tril(jnp.ones((SGU_CHUNK, SGU_CHUNK), dtype=bool))
    wsm = jnp.where(mask, ws.astype(jnp.float32), 0.0)
    v = jnp.einsum('gts,bnsgd->bntgd', wsm, v) + bs.astype(jnp.float32).T[:, :, None]
    return (u * v.reshape(Bn, S, D_C)).astype(uv.dtype)


def short_conv_mixer(bg, cg, hh, conv_w):
    return bg * causal_dwconv(cg * hh, conv_w)


def conv_ffn(h, up, conv_w, conv_b, down):
    hid = causal_dwconv(h @ up.astype(h.dtype), conv_w, conv_b)
    gate, val = jnp.split(hid, 2, axis=-1)
    return (jax.nn.gelu(gate) * val) @ down.astype(h.dtype)


def _fwd_setup_inputs(seed: int = 0) -> dict:
    key = jax.random.key(seed)
    ks = iter(jax.random.split(key, 48))
    L = DEPTH
    f32 = jnp.float32

    def nrm(shape, scale):
        return jax.random.normal(next(ks), shape, f32) * scale

    def gain(shape):
        return 1.0 + 0.1 * jax.random.normal(next(ks), shape, f32)

    def unif(shape, lo, hi):
        return jax.random.uniform(next(ks), shape, f32, lo, hi)

    s = unif((L, D_B), 0.9, 0.999) ** (1.0 / LRU_C)
    lru_lambda = jnp.log(s) - jnp.log1p(-s)
    dt = jnp.exp(unif((L, GDN_HEADS), float(np.log(1e-3)), float(np.log(1e-1))))
    gdn_dt_bias = dt + jnp.log(-jnp.expm1(-dt))
    return {
        "x": nrm((BATCH, SEQ, D_MODEL), 1.0),
        "pre_mix_norm": gain((L, D_MODEL)),
        "w_in": nrm((L, D_MODEL, N_IN), D_MODEL ** -0.5),
        "gdn_conv_w": nrm((L, GDN_CONV, 3 * D_A), GDN_CONV ** -0.5),
        "gdn_a_log": jnp.log(unif((L, GDN_HEADS), 1.0, 16.0)),
        "gdn_dt_bias": gdn_dt_bias,
        "gdn_norm_w": gain((L, GDN_HEAD_DIM)),
        "lru_conv_w": nrm((L, LRU_CONV, D_B), LRU_CONV ** -0.5),
        "lru_conv_b": nrm((L, D_B), 0.01),
        "lru_wa": nrm((L, LRU_BLOCKS, LRU_BLOCK_DIM, LRU_BLOCK_DIM), LRU_BLOCK_DIM ** -0.5),
        "lru_ba": nrm((L, LRU_BLOCKS, LRU_BLOCK_DIM), 0.01),
        "lru_wx": nrm((L, LRU_BLOCKS, LRU_BLOCK_DIM, LRU_BLOCK_DIM), LRU_BLOCK_DIM ** -0.5),
        "lru_bx": nrm((L, LRU_BLOCKS, LRU_BLOCK_DIM), 0.01),
        "lru_lambda": lru_lambda,
        "sgu_ln_w": gain((L, D_C)),
        "sgu_ln_b": nrm((L, D_C), 0.01),
        "sgu_ws": nrm((L, SGU_GROUPS, SGU_CHUNK, SGU_CHUNK), SGU_CHUNK ** -0.5),
        "sgu_b": gain((L, SGU_GROUPS, SGU_CHUNK)),
        "sconv_w": nrm((L, SCONV_WIDTH, D_D), SCONV_WIDTH ** -0.5),
        "grp_norm_w": gain((L, 3, D_B)),
        "w_out": nrm((L, D_MIX, D_MODEL), D_MIX ** -0.5),
        "post_mix_norm": gain((L, D_MODEL)),
        "pre_ffn_norm": gain((L, D_MODEL)),
        "ffn_up": nrm((L, D_MODEL, 2 * D_FF), D_MODEL ** -0.5),
        "ffn_conv_w": nrm((L, FFN_CONV, 2 * D_FF), FFN_CONV ** -0.5),
        "ffn_conv_b": nrm((L, 2 * D_FF), 0.01),
        "ffn_down": nrm((L, D_FF, D_MODEL), D_FF ** -0.5),
        "post_ffn_norm": gain((L, D_MODEL)),
    }


def _fwd_reference(x, pre_mix_norm, w_in, gdn_conv_w, gdn_a_log, gdn_dt_bias, gdn_norm_w,
              lru_conv_w, lru_conv_b, lru_wa, lru_ba, lru_wx, lru_bx, lru_lambda,
              sgu_ln_w, sgu_ln_b, sgu_ws, sgu_b, sconv_w, grp_norm_w, w_out,
              post_mix_norm, pre_ffn_norm, ffn_up, ffn_conv_w, ffn_conv_b, ffn_down,
              post_ffn_norm):
    for l in range(DEPTH):
        h = rms_norm(x, pre_mix_norm[l])
        p = h @ w_in[l].astype(h.dtype)
        (q, k, v, z, b_gdn, a_gdn, lru_x, lru_gate, sgu_uv, sc_b, sc_c, sc_h) = _split(p, IN_SIZES)
        y_a = gdn_mixer(q, k, v, z, b_gdn, a_gdn, gdn_conv_w[l], gdn_a_log[l], gdn_dt_bias[l], gdn_norm_w[l])
        y_b = rms_norm(rglru_mixer(lru_x, lru_gate, lru_conv_w[l], lru_conv_b[l], lru_wa[l], lru_ba[l],
                                   lru_wx[l], lru_bx[l], lru_lambda[l]), grp_norm_w[l, 0])
        y_c = rms_norm(sgu_mixer(sgu_uv, sgu_ln_w[l], sgu_ln_b[l], sgu_ws[l], sgu_b[l]), grp_norm_w[l, 1])
        y_d = rms_norm(short_conv_mixer(sc_b, sc_c, sc_h, sconv_w[l]), grp_norm_w[l, 2])
        y = jnp.concatenate([y_a, y_b, y_c, y_d], axis=-1) @ w_out[l].astype(h.dtype)
        x = x + rms_norm(y, post_mix_norm[l])
        h = rms_norm(x, pre_ffn_norm[l])
        y = conv_ffn(h, ffn_up[l], ffn_conv_w[l], ffn_conv_b[l], ffn_down[l])
        x = x + rms_norm(y, post_ffn_norm[l])
    return x


import jax as _jax
import jax.numpy as _jnp

TWIN_FORMAT = 'train_step'
FWD_PARAMS = ['x', 'pre_mix_norm', 'w_in', 'gdn_conv_w', 'gdn_a_log', 'gdn_dt_bias', 'gdn_norm_w', 'lru_conv_w', 'lru_conv_b', 'lru_wa', 'lru_ba', 'lru_wx', 'lru_bx', 'lru_lambda', 'sgu_ln_w', 'sgu_ln_b', 'sgu_ws', 'sgu_b', 'sconv_w', 'grp_norm_w', 'w_out', 'post_mix_norm', 'pre_ffn_norm', 'ffn_up', 'ffn_conv_w', 'ffn_conv_b', 'ffn_down', 'post_ffn_norm']
TWIN_WEIGHTS = ['pre_mix_norm', 'w_in', 'gdn_conv_w', 'gdn_a_log', 'gdn_dt_bias', 'gdn_norm_w', 'lru_conv_w', 'lru_conv_b', 'lru_wa', 'lru_ba', 'lru_wx', 'lru_bx', 'lru_lambda', 'sgu_ln_w', 'sgu_ln_b', 'sgu_ws', 'sgu_b', 'sconv_w', 'grp_norm_w', 'w_out', 'post_mix_norm', 'pre_ffn_norm', 'ffn_up', 'ffn_conv_w', 'ffn_conv_b', 'ffn_down', 'post_ffn_norm']
TWIN_DIFF_INPUT = 'x'
TWIN_INPUTS = ['x', 'pre_mix_norm', 'w_in', 'gdn_conv_w', 'gdn_a_log', 'gdn_dt_bias', 'gdn_norm_w', 'lru_conv_w', 'lru_conv_b', 'lru_wa', 'lru_ba', 'lru_wx', 'lru_bx', 'lru_lambda', 'sgu_ln_w', 'sgu_ln_b', 'sgu_ws', 'sgu_b', 'sconv_w', 'grp_norm_w', 'w_out', 'post_mix_norm', 'pre_ffn_norm', 'ffn_up', 'ffn_conv_w', 'ffn_conv_b', 'ffn_down', 'post_ffn_norm', 'loss_target', 'm_pre_mix_norm', 'm_w_in', 'm_gdn_conv_w', 'm_gdn_a_log', 'm_gdn_dt_bias', 'm_gdn_norm_w', 'm_lru_conv_w', 'm_lru_conv_b', 'm_lru_wa', 'm_lru_ba', 'm_lru_wx', 'm_lru_bx', 'm_lru_lambda', 'm_sgu_ln_w', 'm_sgu_ln_b', 'm_sgu_ws', 'm_sgu_b', 'm_sconv_w', 'm_grp_norm_w', 'm_w_out', 'm_post_mix_norm', 'm_pre_ffn_norm', 'm_ffn_up', 'm_ffn_conv_w', 'm_ffn_conv_b', 'm_ffn_down', 'm_post_ffn_norm', 'v_pre_mix_norm', 'v_w_in', 'v_gdn_conv_w', 'v_gdn_a_log', 'v_gdn_dt_bias', 'v_gdn_norm_w', 'v_lru_conv_w', 'v_lru_conv_b', 'v_lru_wa', 'v_lru_ba', 'v_lru_wx', 'v_lru_bx', 'v_lru_lambda', 'v_sgu_ln_w', 'v_sgu_ln_b', 'v_sgu_ws', 'v_sgu_b', 'v_sconv_w', 'v_grp_norm_w', 'v_w_out', 'v_post_mix_norm', 'v_pre_ffn_norm', 'v_ffn_up', 'v_ffn_conv_w', 'v_ffn_conv_b', 'v_ffn_down', 'v_post_ffn_norm']
TWIN_OUTPUTS = ['loss', 'grad_x', 'grad_pre_mix_norm', 'grad_w_in', 'grad_gdn_conv_w', 'grad_gdn_a_log', 'grad_gdn_dt_bias', 'grad_gdn_norm_w', 'grad_lru_conv_w', 'grad_lru_conv_b', 'grad_lru_wa', 'grad_lru_ba', 'grad_lru_wx', 'grad_lru_bx', 'grad_lru_lambda', 'grad_sgu_ln_w', 'grad_sgu_ln_b', 'grad_sgu_ws', 'grad_sgu_b', 'grad_sconv_w', 'grad_grp_norm_w', 'grad_w_out', 'grad_post_mix_norm', 'grad_pre_ffn_norm', 'grad_ffn_up', 'grad_ffn_conv_w', 'grad_ffn_conv_b', 'grad_ffn_down', 'grad_post_ffn_norm', 'delta_pre_mix_norm', 'delta_w_in', 'delta_gdn_conv_w', 'delta_gdn_a_log', 'delta_gdn_dt_bias', 'delta_gdn_norm_w', 'delta_lru_conv_w', 'delta_lru_conv_b', 'delta_lru_wa', 'delta_lru_ba', 'delta_lru_wx', 'delta_lru_bx', 'delta_lru_lambda', 'delta_sgu_ln_w', 'delta_sgu_ln_b', 'delta_sgu_ws', 'delta_sgu_b', 'delta_sconv_w', 'delta_grp_norm_w', 'delta_w_out', 'delta_post_mix_norm', 'delta_pre_ffn_norm', 'delta_ffn_up', 'delta_ffn_conv_w', 'delta_ffn_conv_b', 'delta_ffn_down', 'delta_post_ffn_norm', 'new_m_pre_mix_norm', 'new_m_w_in', 'new_m_gdn_conv_w', 'new_m_gdn_a_log', 'new_m_gdn_dt_bias', 'new_m_gdn_norm_w', 'new_m_lru_conv_w', 'new_m_lru_conv_b', 'new_m_lru_wa', 'new_m_lru_ba', 'new_m_lru_wx', 'new_m_lru_bx', 'new_m_lru_lambda', 'new_m_sgu_ln_w', 'new_m_sgu_ln_b', 'new_m_sgu_ws', 'new_m_sgu_b', 'new_m_sconv_w', 'new_m_grp_norm_w', 'new_m_w_out', 'new_m_post_mix_norm', 'new_m_pre_ffn_norm', 'new_m_ffn_up', 'new_m_ffn_conv_w', 'new_m_ffn_conv_b', 'new_m_ffn_down', 'new_m_post_ffn_norm', 'new_v_pre_mix_norm', 'new_v_w_in', 'new_v_gdn_conv_w', 'new_v_gdn_a_log', 'new_v_gdn_dt_bias', 'new_v_gdn_norm_w', 'new_v_lru_conv_w', 'new_v_lru_conv_b', 'new_v_lru_wa', 'new_v_lru_ba', 'new_v_lru_wx', 'new_v_lru_bx', 'new_v_lru_lambda', 'new_v_sgu_ln_w', 'new_v_sgu_ln_b', 'new_v_sgu_ws', 'new_v_sgu_b', 'new_v_sconv_w', 'new_v_grp_norm_w', 'new_v_w_out', 'new_v_post_mix_norm', 'new_v_pre_ffn_norm', 'new_v_ffn_up', 'new_v_ffn_conv_w', 'new_v_ffn_conv_b', 'new_v_ffn_down', 'new_v_post_ffn_norm']
TWIN_LEAF_KINDS = {'loss': 'loss', 'grad_x': 'grad_x', 'grad_pre_mix_norm': 'grad_w', 'grad_w_in': 'grad_w', 'grad_gdn_conv_w': 'grad_w', 'grad_gdn_a_log': 'grad_w', 'grad_gdn_dt_bias': 'grad_w', 'grad_gdn_norm_w': 'grad_w', 'grad_lru_conv_w': 'grad_w', 'grad_lru_conv_b': 'grad_w', 'grad_lru_wa': 'grad_w', 'grad_lru_ba': 'grad_w', 'grad_lru_wx': 'grad_w', 'grad_lru_bx': 'grad_w', 'grad_lru_lambda': 'grad_w', 'grad_sgu_ln_w': 'grad_w', 'grad_sgu_ln_b': 'grad_w', 'grad_sgu_ws': 'grad_w', 'grad_sgu_b': 'grad_w', 'grad_sconv_w': 'grad_w', 'grad_grp_norm_w': 'grad_w', 'grad_w_out': 'grad_w', 'grad_post_mix_norm': 'grad_w', 'grad_pre_ffn_norm': 'grad_w', 'grad_ffn_up': 'grad_w', 'grad_ffn_conv_w': 'grad_w', 'grad_ffn_conv_b': 'grad_w', 'grad_ffn_down': 'grad_w', 'grad_post_ffn_norm': 'grad_w', 'delta_pre_mix_norm': 'delta_w', 'delta_w_in': 'delta_w', 'delta_gdn_conv_w': 'delta_w', 'delta_gdn_a_log': 'delta_w', 'delta_gdn_dt_bias': 'delta_w', 'delta_gdn_norm_w': 'delta_w', 'delta_lru_conv_w': 'delta_w', 'delta_lru_conv_b': 'delta_w', 'delta_lru_wa': 'delta_w', 'delta_lru_ba': 'delta_w', 'delta_lru_wx': 'delta_w', 'delta_lru_bx': 'delta_w', 'delta_lru_lambda': 'delta_w', 'delta_sgu_ln_w': 'delta_w', 'delta_sgu_ln_b': 'delta_w', 'delta_sgu_ws': 'delta_w', 'delta_sgu_b': 'delta_w', 'delta_sconv_w': 'delta_w', 'delta_grp_norm_w': 'delta_w', 'delta_w_out': 'delta_w', 'delta_post_mix_norm': 'delta_w', 'delta_pre_ffn_norm': 'delta_w', 'delta_ffn_up': 'delta_w', 'delta_ffn_conv_w': 'delta_w', 'delta_ffn_conv_b': 'delta_w', 'delta_ffn_down': 'delta_w', 'delta_post_ffn_norm': 'delta_w', 'new_m_pre_mix_norm': 'new_m', 'new_m_w_in': 'new_m', 'new_m_gdn_conv_w': 'new_m', 'new_m_gdn_a_log': 'new_m', 'new_m_gdn_dt_bias': 'new_m', 'new_m_gdn_norm_w': 'new_m', 'new_m_lru_conv_w': 'new_m', 'new_m_lru_conv_b': 'new_m', 'new_m_lru_wa': 'new_m', 'new_m_lru_ba': 'new_m', 'new_m_lru_wx': 'new_m', 'new_m_lru_bx': 'new_m', 'new_m_lru_lambda': 'new_m', 'new_m_sgu_ln_w': 'new_m', 'new_m_sgu_ln_b': 'new_m', 'new_m_sgu_ws': 'new_m', 'new_m_sgu_b': 'new_m', 'new_m_sconv_w': 'new_m', 'new_m_grp_norm_w': 'new_m', 'new_m_w_out': 'new_m', 'new_m_post_mix_norm': 'new_m', 'new_m_pre_ffn_norm': 'new_m', 'new_m_ffn_up': 'new_m', 'new_m_ffn_conv_w': 'new_m', 'new_m_ffn_conv_b': 'new_m', 'new_m_ffn_down': 'new_m', 'new_m_post_ffn_norm': 'new_m', 'new_v_pre_mix_norm': 'new_v', 'new_v_w_in': 'new_v', 'new_v_gdn_conv_w': 'new_v', 'new_v_gdn_a_log': 'new_v', 'new_v_gdn_dt_bias': 'new_v', 'new_v_gdn_norm_w': 'new_v', 'new_v_lru_conv_w': 'new_v', 'new_v_lru_conv_b': 'new_v', 'new_v_lru_wa': 'new_v', 'new_v_lru_ba': 'new_v', 'new_v_lru_wx': 'new_v', 'new_v_lru_bx': 'new_v', 'new_v_lru_lambda': 'new_v', 'new_v_sgu_ln_w': 'new_v', 'new_v_sgu_ln_b': 'new_v', 'new_v_sgu_ws': 'new_v', 'new_v_sgu_b': 'new_v', 'new_v_sconv_w': 'new_v', 'new_v_grp_norm_w': 'new_v', 'new_v_w_out': 'new_v', 'new_v_post_mix_norm': 'new_v', 'new_v_pre_ffn_norm': 'new_v', 'new_v_ffn_up': 'new_v', 'new_v_ffn_conv_w': 'new_v', 'new_v_ffn_conv_b': 'new_v', 'new_v_ffn_down': 'new_v', 'new_v_post_ffn_norm': 'new_v'}


def _forward(args):
    return _fwd_reference(*[args[k] for k in FWD_PARAMS])


def _output_shape():
    def fwd():
        inp = _fwd_setup_inputs(0)
        return _fwd_reference(*[inp[k] for k in FWD_PARAMS])
    out = _jax.eval_shape(fwd)
    return out.shape, out.dtype

N_MICROBATCH = 1
ADAM_LR = 0.001
ADAM_B1 = 0.9
ADAM_B2 = 0.999
ADAM_EPS = 1e-08
ADAM_WD = 0.01
ADAM_STEP = 10
PER_EXAMPLE_BATCH_AXIS = {'x': 0, 'loss_target': 0}
SHARED_INPUTS = []
_WEIGHT_DTYPES = {'pre_mix_norm': _jnp.float32, 'w_in': _jnp.float32, 'gdn_conv_w': _jnp.float32, 'gdn_a_log': _jnp.float32, 'gdn_dt_bias': _jnp.float32, 'gdn_norm_w': _jnp.float32, 'lru_conv_w': _jnp.float32, 'lru_conv_b': _jnp.float32, 'lru_wa': _jnp.float32, 'lru_ba': _jnp.float32, 'lru_wx': _jnp.float32, 'lru_bx': _jnp.float32, 'lru_lambda': _jnp.float32, 'sgu_ln_w': _jnp.float32, 'sgu_ln_b': _jnp.float32, 'sgu_ws': _jnp.float32, 'sgu_b': _jnp.float32, 'sconv_w': _jnp.float32, 'grp_norm_w': _jnp.float32, 'w_out': _jnp.float32, 'post_mix_norm': _jnp.float32, 'pre_ffn_norm': _jnp.float32, 'ffn_up': _jnp.float32, 'ffn_conv_w': _jnp.float32, 'ffn_conv_b': _jnp.float32, 'ffn_down': _jnp.float32, 'post_ffn_norm': _jnp.float32}
MOMENT_SCALE = {'pre_mix_norm': 1.180674e+00, 'w_in': 6.924754e-01, 'gdn_conv_w': 5.665532e-01, 'gdn_a_log': 1.216224e+00, 'gdn_dt_bias': 1.185424e+00, 'gdn_norm_w': 2.717327e+00, 'lru_conv_w': 3.028563e+00, 'lru_conv_b': 3.810416e+01, 'lru_wa': 1.271326e+00, 'lru_ba': 9.428839e-01, 'lru_wx': 2.361297e+00, 'lru_bx': 9.031111e-01, 'lru_lambda': 1.529297e+00, 'sgu_ln_w': 3.437210e-01, 'sgu_ln_b': 3.503199e-01, 'sgu_ws': 3.216431e-01, 'sgu_b': 5.788735e-01, 'sconv_w': 6.120047e-01, 'grp_norm_w': 3.172656e+00, 'w_out': 2.855499e+00, 'post_mix_norm': 3.244749e+01, 'pre_ffn_norm': 1.028203e+00, 'ffn_up': 4.392663e-01, 'ffn_conv_w': 5.017865e-01, 'ffn_conv_b': 2.914207e+00, 'ffn_down': 9.791667e-01, 'post_ffn_norm': 3.219350e+01}


def _to_microbatches(a, axis):
    t = _jnp.moveaxis(a, axis, 0)
    t = t.reshape((N_MICROBATCH, t.shape[0] // N_MICROBATCH) + t.shape[1:])
    return _jnp.moveaxis(t, 1, axis + 1)


def setup_inputs(seed: int = 0) -> dict:
    inp = _fwd_setup_inputs(seed)
    key = _jax.random.fold_in(_jax.random.key(seed), 7919)
    shape, _ = _output_shape()
    out = dict(inp)
    out["loss_target"] = _jax.random.normal(_jax.random.fold_in(key, 0), shape, _jnp.float32)
    for i, name in enumerate(TWIN_WEIGHTS):
        w = inp[name].astype(_jnp.float32)
        if MOMENT_SCALE is None:
            s = _jnp.sqrt(_jnp.mean(_jnp.square(w)) + 1e-30)
        else:
            s = MOMENT_SCALE[name]
        km, kv = _jax.random.split(_jax.random.fold_in(key, i + 1))
        out[name] = w
        out["m_" + name] = s * _jax.random.normal(km, w.shape, _jnp.float32)
        out["v_" + name] = (s * s) * _jax.random.uniform(kv, w.shape, _jnp.float32, 0.5, 1.5)
    if N_MICROBATCH > 1:
        for name, axis in PER_EXAMPLE_BATCH_AXIS.items():
            out[name] = _to_microbatches(out[name], axis)
    return {'x': out['x'], 'pre_mix_norm': out['pre_mix_norm'], 'w_in': out['w_in'], 'gdn_conv_w': out['gdn_conv_w'], 'gdn_a_log': out['gdn_a_log'], 'gdn_dt_bias': out['gdn_dt_bias'], 'gdn_norm_w': out['gdn_norm_w'], 'lru_conv_w': out['lru_conv_w'], 'lru_conv_b': out['lru_conv_b'], 'lru_wa': out['lru_wa'], 'lru_ba': out['lru_ba'], 'lru_wx': out['lru_wx'], 'lru_bx': out['lru_bx'], 'lru_lambda': out['lru_lambda'], 'sgu_ln_w': out['sgu_ln_w'], 'sgu_ln_b': out['sgu_ln_b'], 'sgu_ws': out['sgu_ws'], 'sgu_b': out['sgu_b'], 'sconv_w': out['sconv_w'], 'grp_norm_w': out['grp_norm_w'], 'w_out': out['w_out'], 'post_mix_norm': out['post_mix_norm'], 'pre_ffn_norm': out['pre_ffn_norm'], 'ffn_up': out['ffn_up'], 'ffn_conv_w': out['ffn_conv_w'], 'ffn_conv_b': out['ffn_conv_b'], 'ffn_down': out['ffn_down'], 'post_ffn_norm': out['post_ffn_norm'], 'loss_target': out['loss_target'], 'm_pre_mix_norm': out['m_pre_mix_norm'], 'm_w_in': out['m_w_in'], 'm_gdn_conv_w': out['m_gdn_conv_w'], 'm_gdn_a_log': out['m_gdn_a_log'], 'm_gdn_dt_bias': out['m_gdn_dt_bias'], 'm_gdn_norm_w': out['m_gdn_norm_w'], 'm_lru_conv_w': out['m_lru_conv_w'], 'm_lru_conv_b': out['m_lru_conv_b'], 'm_lru_wa': out['m_lru_wa'], 'm_lru_ba': out['m_lru_ba'], 'm_lru_wx': out['m_lru_wx'], 'm_lru_bx': out['m_lru_bx'], 'm_lru_lambda': out['m_lru_lambda'], 'm_sgu_ln_w': out['m_sgu_ln_w'], 'm_sgu_ln_b': out['m_sgu_ln_b'], 'm_sgu_ws': out['m_sgu_ws'], 'm_sgu_b': out['m_sgu_b'], 'm_sconv_w': out['m_sconv_w'], 'm_grp_norm_w': out['m_grp_norm_w'], 'm_w_out': out['m_w_out'], 'm_post_mix_norm': out['m_post_mix_norm'], 'm_pre_ffn_norm': out['m_pre_ffn_norm'], 'm_ffn_up': out['m_ffn_up'], 'm_ffn_conv_w': out['m_ffn_conv_w'], 'm_ffn_conv_b': out['m_ffn_conv_b'], 'm_ffn_down': out['m_ffn_down'], 'm_post_ffn_norm': out['m_post_ffn_norm'], 'v_pre_mix_norm': out['v_pre_mix_norm'], 'v_w_in': out['v_w_in'], 'v_gdn_conv_w': out['v_gdn_conv_w'], 'v_gdn_a_log': out['v_gdn_a_log'], 'v_gdn_dt_bias': out['v_gdn_dt_bias'], 'v_gdn_norm_w': out['v_gdn_norm_w'], 'v_lru_conv_w': out['v_lru_conv_w'], 'v_lru_conv_b': out['v_lru_conv_b'], 'v_lru_wa': out['v_lru_wa'], 'v_lru_ba': out['v_lru_ba'], 'v_lru_wx': out['v_lru_wx'], 'v_lru_bx': out['v_lru_bx'], 'v_lru_lambda': out['v_lru_lambda'], 'v_sgu_ln_w': out['v_sgu_ln_w'], 'v_sgu_ln_b': out['v_sgu_ln_b'], 'v_sgu_ws': out['v_sgu_ws'], 'v_sgu_b': out['v_sgu_b'], 'v_sconv_w': out['v_sconv_w'], 'v_grp_norm_w': out['v_grp_norm_w'], 'v_w_out': out['v_w_out'], 'v_post_mix_norm': out['v_post_mix_norm'], 'v_pre_ffn_norm': out['v_pre_ffn_norm'], 'v_ffn_up': out['v_ffn_up'], 'v_ffn_conv_w': out['v_ffn_conv_w'], 'v_ffn_conv_b': out['v_ffn_conv_b'], 'v_ffn_down': out['v_ffn_down'], 'v_post_ffn_norm': out['v_post_ffn_norm']}


def _loss(weights, diff, rest, loss_target):
    with _jax.named_scope("forward"):
        args = {**rest, TWIN_DIFF_INPUT: diff, **{k: w.astype(_WEIGHT_DTYPES[k]) for k, w in weights.items()}}
        y = _forward(args)
    with _jax.named_scope("loss_head"):
        err = _jnp.square(y.astype(_jnp.float32) - loss_target)
        return 0.5 * _jnp.sum(_jnp.mean(err, axis=-1)) if err.ndim else 0.5 * err


def _adamw(w, g, m, v):
    m = ADAM_B1 * m + (1.0 - ADAM_B1) * g
    v = ADAM_B2 * v + (1.0 - ADAM_B2) * _jnp.square(g)
    m_hat = m / (1.0 - ADAM_B1 ** ADAM_STEP)
    v_hat = v / (1.0 - ADAM_B2 ** ADAM_STEP)
    delta = -ADAM_LR * (m_hat / (_jnp.sqrt(v_hat) + ADAM_EPS) + ADAM_WD * w)
    return delta, m, v


def reference(x, pre_mix_norm, w_in, gdn_conv_w, gdn_a_log, gdn_dt_bias, gdn_norm_w, lru_conv_w, lru_conv_b, lru_wa, lru_ba, lru_wx, lru_bx, lru_lambda, sgu_ln_w, sgu_ln_b, sgu_ws, sgu_b, sconv_w, grp_norm_w, w_out, post_mix_norm, pre_ffn_norm, ffn_up, ffn_conv_w, ffn_conv_b, ffn_down, post_ffn_norm, loss_target, m_pre_mix_norm, m_w_in, m_gdn_conv_w, m_gdn_a_log, m_gdn_dt_bias, m_gdn_norm_w, m_lru_conv_w, m_lru_conv_b, m_lru_wa, m_lru_ba, m_lru_wx, m_lru_bx, m_lru_lambda, m_sgu_ln_w, m_sgu_ln_b, m_sgu_ws, m_sgu_b, m_sconv_w, m_grp_norm_w, m_w_out, m_post_mix_norm, m_pre_ffn_norm, m_ffn_up, m_ffn_conv_w, m_ffn_conv_b, m_ffn_down, m_post_ffn_norm, v_pre_mix_norm, v_w_in, v_gdn_conv_w, v_gdn_a_log, v_gdn_dt_bias, v_gdn_norm_w, v_lru_conv_w, v_lru_conv_b, v_lru_wa, v_lru_ba, v_lru_wx, v_lru_bx, v_lru_lambda, v_sgu_ln_w, v_sgu_ln_b, v_sgu_ws, v_sgu_b, v_sconv_w, v_grp_norm_w, v_w_out, v_post_mix_norm, v_pre_ffn_norm, v_ffn_up, v_ffn_conv_w, v_ffn_conv_b, v_ffn_down, v_post_ffn_norm):
    given = dict(x=x, pre_mix_norm=pre_mix_norm, w_in=w_in, gdn_conv_w=gdn_conv_w, gdn_a_log=gdn_a_log, gdn_dt_bias=gdn_dt_bias, gdn_norm_w=gdn_norm_w, lru_conv_w=lru_conv_w, lru_conv_b=lru_conv_b, lru_wa=lru_wa, lru_ba=lru_ba, lru_wx=lru_wx, lru_bx=lru_bx, lru_lambda=lru_lambda, sgu_ln_w=sgu_ln_w, sgu_ln_b=sgu_ln_b, sgu_ws=sgu_ws, sgu_b=sgu_b, sconv_w=sconv_w, grp_norm_w=grp_norm_w, w_out=w_out, post_mix_norm=post_mix_norm, pre_ffn_norm=pre_ffn_norm, ffn_up=ffn_up, ffn_conv_w=ffn_conv_w, ffn_conv_b=ffn_conv_b, ffn_down=ffn_down, post_ffn_norm=post_ffn_norm, loss_target=loss_target, m_pre_mix_norm=m_pre_mix_norm, m_w_in=m_w_in, m_gdn_conv_w=m_gdn_conv_w, m_gdn_a_log=m_gdn_a_log, m_gdn_dt_bias=m_gdn_dt_bias, m_gdn_norm_w=m_gdn_norm_w, m_lru_conv_w=m_lru_conv_w, m_lru_conv_b=m_lru_conv_b, m_lru_wa=m_lru_wa, m_lru_ba=m_lru_ba, m_lru_wx=m_lru_wx, m_lru_bx=m_lru_bx, m_lru_lambda=m_lru_lambda, m_sgu_ln_w=m_sgu_ln_w, m_sgu_ln_b=m_sgu_ln_b, m_sgu_ws=m_sgu_ws, m_sgu_b=m_sgu_b, m_sconv_w=m_sconv_w, m_grp_norm_w=m_grp_norm_w, m_w_out=m_w_out, m_post_mix_norm=m_post_mix_norm, m_pre_ffn_norm=m_pre_ffn_norm, m_ffn_up=m_ffn_up, m_ffn_conv_w=m_ffn_conv_w, m_ffn_conv_b=m_ffn_conv_b, m_ffn_down=m_ffn_down, m_post_ffn_norm=m_post_ffn_norm, v_pre_mix_norm=v_pre_mix_norm, v_w_in=v_w_in, v_gdn_conv_w=v_gdn_conv_w, v_gdn_a_log=v_gdn_a_log, v_gdn_dt_bias=v_gdn_dt_bias, v_gdn_norm_w=v_gdn_norm_w, v_lru_conv_w=v_lru_conv_w, v_lru_conv_b=v_lru_conv_b, v_lru_wa=v_lru_wa, v_lru_ba=v_lru_ba, v_lru_wx=v_lru_wx, v_lru_bx=v_lru_bx, v_lru_lambda=v_lru_lambda, v_sgu_ln_w=v_sgu_ln_w, v_sgu_ln_b=v_sgu_ln_b, v_sgu_ws=v_sgu_ws, v_sgu_b=v_sgu_b, v_sconv_w=v_sconv_w, v_grp_norm_w=v_grp_norm_w, v_w_out=v_w_out, v_post_mix_norm=v_post_mix_norm, v_pre_ffn_norm=v_pre_ffn_norm, v_ffn_up=v_ffn_up, v_ffn_conv_w=v_ffn_conv_w, v_ffn_conv_b=v_ffn_conv_b, v_ffn_down=v_ffn_down, v_post_ffn_norm=v_post_ffn_norm)
    weights = {n: given[n] for n in TWIN_WEIGHTS}
    shared = {n: given[n] for n in SHARED_INPUTS}
    per_example = {n: given[n] for n in ['x']}
    grad_fn = _jax.value_and_grad(_loss, argnums=(0, 1))

    def one_microbatch(ex, loss_target):
        ex = dict(ex)
        diff = ex.pop(TWIN_DIFF_INPUT)
        return grad_fn(weights, diff, {**shared, **ex}, loss_target)

    if N_MICROBATCH == 1:
        loss, (grad_w, grad_x) = one_microbatch(per_example, given["loss_target"])
    else:
        def body(carry, xs):
            loss_sum, grad_sum = carry
            l_k, (gw_k, gx_k) = one_microbatch(xs[0], xs[1])
            with _jax.named_scope("update"):
                return (loss_sum + l_k, _jax.tree.map(_jnp.add, grad_sum, gw_k)), gx_k

        init = (_jnp.zeros((), _jnp.float32), _jax.tree.map(_jnp.zeros_like, weights))
        (loss, grad_w), grad_x = _jax.lax.scan(body, init, (per_example, given["loss_target"]))
    with _jax.named_scope("update"):
        delta_w, new_m, new_v = {}, {}, {}
        for n in TWIN_WEIGHTS:
            delta_w[n], new_m[n], new_v[n] = _adamw(weights[n], grad_w[n], given["m_" + n], given["v_" + n])
    return (loss, grad_x, *[grad_w[n] for n in TWIN_WEIGHTS], *[delta_w[n] for n in TWIN_WEIGHTS],
            *[new_m[n] for n in TWIN_WEIGHTS], *[new_v[n] for n in TWIN_WEIGHTS])
```

```python
import functools
import math

import jax
import jax.numpy as jnp
from jax import lax
from jax.experimental import pallas as pl
from jax.experimental.pallas import tpu as pltpu

F32 = jnp.float32
BF16 = jnp.bfloat16
EPS = 1e-6
HALO = 8
VMEM_LIMIT = 56 * 1024 * 1024
MESH = pl.DeviceIdType.MESH
N_DEV = 8

ADAM_LR, ADAM_B1, ADAM_B2, ADAM_EPS, ADAM_WD, ADAM_STEP = 0.001, 0.9, 0.999, 1e-08, 0.01, 10

GDN_HEADS, GDN_DIM, GDN_CHUNK = 4, 128, 64
SGU_GROUPS, SGU_CHUNK = 4, 128
LRU_BLOCKS, LRU_C = 8, 8.0
D_G = 512
N_IN = 5640
N_INP = 5760
BA_COL = 5632

SHARDED_SMALL = ("gdn_conv_w", "lru_conv_w", "sconv_w", "grp_norm_w", "ffn_conv_w")
BIG = ("w_in", "w_out", "ffn_up", "ffn_down")
WEIGHTS = ("pre_mix_norm", "w_in", "gdn_conv_w", "gdn_a_log", "gdn_dt_bias", "gdn_norm_w", "lru_conv_w",
           "lru_conv_b", "lru_wa", "lru_ba", "lru_wx", "lru_bx", "lru_lambda", "sgu_ln_w", "sgu_ln_b", "sgu_ws",
           "sgu_b", "sconv_w", "grp_norm_w", "w_out", "post_mix_norm", "pre_ffn_norm", "ffn_up", "ffn_conv_w",
           "ffn_conv_b", "ffn_down", "post_ffn_norm")
SMALL = tuple(n for n in WEIGHTS if n not in BIG)


def _dot(a, b, ca, cb):
    return lax.dot_general(a.astype(BF16), b.astype(BF16), (((ca,), (cb,)), ((), ())),
                           preferred_element_type=F32)


@jax.custom_vjp
def _mm(a, b):
    return _dot(a, b, 1, 0)


def _mm_f(a, b):
    return _dot(a, b, 1, 0), (a, b)


def _mm_b(res, g):
    a, b = res
    return _dot(g, b, 1, 1), _dot(a, g, 0, 0)


_mm.defvjp(_mm_f, _mm_b)


@jax.custom_vjp
def _mm_nt(a, b):
    return _dot(a, b, 1, 1)


def _mm_nt_f(a, b):
    return _dot(a, b, 1, 1), (a, b)


def _mm_nt_b(res, g):
    a, b = res
    return _dot(g, b, 1, 0), _dot(g, a, 0, 0)


_mm_nt.defvjp(_mm_nt_f, _mm_nt_b)


@jax.custom_vjp
def _mm_tn(a, b):
    return _dot(a, b, 0, 0)


def _mm_tn_f(a, b):
    return _dot(a, b, 0, 0), (a, b)


def _mm_tn_b(res, g):
    a, b = res
    return _dot(b, g, 1, 1), _dot(a, g, 1, 0)


_mm_tn.defvjp(_mm_tn_f, _mm_tn_b)


def _dot_exact(a, b, ca, cb):
    return lax.dot_general(a, b, (((ca,), (cb,)), ((), ())), precision=lax.Precision.HIGHEST,
                           preferred_element_type=F32)


@functools.partial(jax.custom_vjp, nondiff_argnums=(1,))
def _shift_rows(x, s):
    return pltpu.roll(x, s, 0)


def _shift_rows_f(x, s):
    return pltpu.roll(x, s, 0), None


def _shift_rows_b(s, _, g):
    return (pltpu.roll(g, (g.shape[0] - s) % g.shape[0], 0),)


_shift_rows.defvjp(_shift_rows_f, _shift_rows_b)


def _sigmoid(x):
    return 1.0 / (1.0 + jnp.exp(-x))


def _silu(x):
    return x * _sigmoid(x)


def _gelu(x):
    return 0.5 * x * (1.0 + jnp.tanh(0.7978845608028654 * (x + 0.044715 * (x * x * x))))


@jax.custom_vjp
def _softplus(x):
    e = jnp.exp(-jnp.abs(x))
    u = 1.0 + e
    log1p = jnp.where(u == 1.0, e, jnp.log(u) * (e / jnp.where(u == 1.0, 1.0, u - 1.0)))
    return jnp.maximum(x, 0.0) + log1p


def _softplus_f(x):
    return _softplus(x), x


def _softplus_b(x, g):
    return (g * _sigmoid(x),)


_softplus.defvjp(_softplus_f, _softplus_b)


def _neg_expm1(y):
    return -jnp.tanh(0.5 * y) * (jnp.exp(y) + 1.0)


def _rms(x, w):
    return x * lax.rsqrt(jnp.mean(x * x, axis=-1, keepdims=True) + EPS) * w


def _row(w, k):
    sel = lax.broadcasted_iota(jnp.int32, w.shape, 0) == k
    return jnp.sum(jnp.where(sel, w, 0.0), axis=0, keepdims=True)


def _col(x, j):
    sel = lax.broadcasted_iota(jnp.int32, x.shape, 1) == j
    return jnp.sum(jnp.where(sel, x, 0.0), axis=1, keepdims=True)


def _conv(x_ext, w, taps):
    acc = None
    for k in range(taps):
        s = taps - 1 - k
        t = (x_ext if s == 0 else _shift_rows(x_ext, s)) * _row(w, k)
        acc = t if acc is None else acc + t
    return acc[HALO:]


@jax.custom_vjp
def _scan(a, b, h0):
    n = a.shape[0]
    row = lax.broadcasted_iota(jnp.int32, a.shape, 0)
    s = 1
    while s < n:
        keep = row >= s
        a_sh = jnp.where(keep, pltpu.roll(a, s, 0), 1.0)
        b_sh = jnp.where(keep, pltpu.roll(b, s, 0), 0.0)
        b = a * b_sh + b
        a = a * a_sh
        s *= 2
    return b + a * h0


def _scan_f(a, b, h0):
    h = _scan(a, b, h0)
    return h, (a, h, h0)


def _scan_b(res, dh):
    a, h, h0 = res
    n = a.shape[0]
    row = lax.broadcasted_iota(jnp.int32, a.shape, 0)
    an = jnp.where(row < n - 1, pltpu.roll(a, n - 1, 0), 0.0)
    lam = dh
    s = 1
    while s < n:
        keep = row < n - s
        a_sh = jnp.where(keep, pltpu.roll(an, n - s, 0), 1.0)
        l_sh = jnp.where(keep, pltpu.roll(lam, n - s, 0), 0.0)
        lam = an * l_sh + lam
        an = an * a_sh
        s *= 2
    h_prev = jnp.where(row >= 1, pltpu.roll(h, 1, 0), h0)
    al = a * lam
    dh0 = jnp.sum(jnp.where(row == 0, al, 0.0), axis=0, keepdims=True)
    return lam * h_prev, lam, dh0


_scan.defvjp(_scan_f, _scan_b)


def _last_row(x):
    sel = lax.broadcasted_iota(jnp.int32, x.shape, 0) == x.shape[0] - 1
    return jnp.sum(jnp.where(sel, x, 0.0), axis=0, keepdims=True)


def fn_norm(xs, st, ps):
    (x,), (w,) = xs, ps
    return [_rms(x, w).astype(BF16)], []


def fn_norm_keep(xs, st, ps):
    (x,), (w,) = xs, ps
    return [_rms(x, w).astype(BF16), x], []


def fn_res(xs, st, ps):
    (x, y), (w_post, w_next) = xs, ps
    x1 = x + _rms(y, w_post)
    return [x1, _rms(x1, w_next).astype(BF16)], []


def fn_res_last(xs, st, ps):
    (x, y), (w_post,) = xs, ps
    return [x + _rms(y, w_post)], []


def fn_gdn(xs, st, ps):
    qkv_ext, z, ba = xs
    (state,) = st
    cw, gp, nw = ps
    ts = z.shape[0]
    qkv = _silu(_conv(qkv_ext, cw, 4))
    beta_all = _sigmoid(ba)
    g_all = -jnp.exp(_row(gp, 0)) * _softplus(ba + _row(gp, 1))
    c_n = GDN_CHUNK
    ri = lax.broadcasted_iota(jnp.int32, (c_n, c_n), 0)
    ci = lax.broadcasted_iota(jnp.int32, (c_n, c_n), 1)
    causal, strict = ri >= ci, ri > ci
    tril = causal.astype(F32)
    eye = (ri == ci).astype(F32)
    lane = lax.broadcasted_iota(jnp.int32, (c_n, 128), 1)
    s_h = [state[GDN_DIM * h:GDN_DIM * (h + 1)] for h in range(GDN_HEADS)]
    out_rows = []
    for c in range(ts // c_n):
        r0 = c * c_n
        gcum_all = _dot_exact(tril, g_all[r0:r0 + c_n], 1, 0)
        beta_c = beta_all[r0:r0 + c_n]
        heads = []
        for h in range(GDN_HEADS):
            q = qkv[r0:r0 + c_n, GDN_DIM * h:GDN_DIM * (h + 1)]
            k = qkv[r0:r0 + c_n, D_G + GDN_DIM * h:D_G + GDN_DIM * (h + 1)]
            v = qkv[r0:r0 + c_n, 2 * D_G + GDN_DIM * h:2 * D_G + GDN_DIM * (h + 1)]
            q = q * lax.rsqrt(jnp.sum(q * q, axis=-1, keepdims=True) + EPS) * (GDN_DIM ** -0.5)
            k = k * lax.rsqrt(jnp.sum(k * k, axis=-1, keepdims=True) + EPS)
            b = _col(beta_c, h)
            gc = _col(gcum_all, 4 + h)
            pick = (lane == 4 + h).astype(F32)
            gr = _dot_exact(pick, gcum_all, 1, 1)
            decay = jnp.where(causal, jnp.exp(jnp.where(causal, gc - gr, 0.0)), 0.0)
            kb = k * b
            m = jnp.where(strict, _mm_nt(kb, k) * decay, 0.0)
            n_ = -m
            t_ = eye + n_
            p_ = n_
            for _ in range(5):
                p_ = _mm(p_, p_)
                t_ = t_ + _mm(t_, p_)
            eg = jnp.exp(gc)
            w = _mm(t_, kb * eg)
            u = _mm(t_, v * b)
            attn = jnp.where(causal, _mm_nt(q, k) * decay, 0.0)
            g_last = _last_row(gc)
            k_g = k * jnp.exp(g_last - gc)
            v_new = u - _mm(w, s_h[h])
            o = _mm(q * eg, s_h[h]) + _mm(attn, v_new)
            s_h[h] = s_h[h] * jnp.exp(g_last) + _mm_tn(k_g, v_new)
            zz = z[r0:r0 + c_n, GDN_DIM * h:GDN_DIM * (h + 1)]
            heads.append(o * lax.rsqrt(jnp.mean(o * o, axis=-1, keepdims=True) + EPS) * nw * _silu(zz))
        out_rows.append(jnp.concatenate(heads, axis=1))
    y = out_rows[0] if len(out_rows) == 1 else jnp.concatenate(out_rows, axis=0)
    return [y.astype(BF16)], [jnp.concatenate(s_h, axis=0)]


def fn_lru(xs, st, ps):
    x_ext, gate = xs
    (h0,) = st
    cw, cb, wa, ba, wx, bx, lam, gw = ps
    xc = _conv(x_ext, cw, 4) + cb
    r = _sigmoid(_mm(xc, wa) + ba)
    i = _sigmoid(_mm(xc, wx) + bx)
    log_a = -LRU_C * r * _softplus(-lam)
    a = jnp.exp(log_a)
    mult = jnp.sqrt(_neg_expm1(2.0 * log_a))
    h = _scan(a, mult * (i * xc), h0)
    y = _rms(h * _gelu(gate), gw)
    return [y.astype(BF16)], [_last_row(h)]


def fn_sgu(xs, st, ps):
    (uv,) = xs
    lnw, lnb, ws, bst, gw = ps
    ts = uv.shape[0]
    uvf = _gelu(uv)
    u, v = uvf[:, :D_G], uvf[:, D_G:]
    vc = v - jnp.mean(v, axis=-1, keepdims=True)
    v = vc * lax.rsqrt(jnp.mean(vc * vc, axis=-1, keepdims=True) + EPS) * lnw + lnb
    t_n = SGU_CHUNK
    tril = lax.broadcasted_iota(jnp.int32, (t_n, t_n), 0) >= lax.broadcasted_iota(jnp.int32, (t_n, t_n), 1)
    wg = [jnp.where(tril, ws[t_n * g:t_n * (g + 1)], 0.0) for g in range(SGU_GROUPS)]
    bg = [_col(bst, g) for g in range(SGU_GROUPS)]
    rows = []
    for c in range(ts // t_n):
        vcg = v[c * t_n:(c + 1) * t_n]
        rows.append(jnp.concatenate(
            [_mm(wg[g], vcg[:, 128 * g:128 * (g + 1)]) + bg[g] for g in range(SGU_GROUPS)], axis=1))
    vv = rows[0] if len(rows) == 1 else jnp.concatenate(rows, axis=0)
    return [_rms(u * vv, gw).astype(BF16)], []


def fn_sconv(xs, st, ps):
    bg, cg_ext, hh_ext = xs
    cw, gw = ps
    return [_rms(bg * _conv(cg_ext * hh_ext, cw, 3), gw).astype(BF16)], []


def fn_ffn(xs, st, ps):
    g_ext, v_ext = xs
    cwg, cwv, cbg, cbv = ps
    g = _conv(g_ext, cwg, 3) + cbg
    v = _conv(v_ext, cwv, 3) + cbv
    return [(_gelu(g) * v).astype(BF16)], []


def _rin(arr, w=None, col=0, halo=False):
    return dict(arr=arr, w=arr.shape[1] if w is None else w, col=col, halo=halo)


def _par(arr, w=None, col=None):
    return dict(arr=arr, w=w, col=col)


def _colidx(col, j):
    return col(j) if callable(col) else col


def _block_call(name, body, n_rows, ts, ncol, reverse, row_ins, blk_ins, params, row_outs, blk_outs, acc_outs,
                carries):
    nblk = n_rows // ts
    hb = ts // HALO

    def rr(i):
        return (nblk - 1 - i) if reverse else i

    in_specs, operands = [], []
    for s in row_ins:
        in_specs.append(pl.BlockSpec((ts, s["w"]), lambda j, i, s=s: (rr(i), _colidx(s["col"], j))))
        operands.append(s["arr"])
        if s["halo"]:
            in_specs.append(pl.BlockSpec((HALO, s["w"]),
                                         lambda j, i, s=s: (jnp.maximum(rr(i) * hb - 1, 0), _colidx(s["col"], j))))
            operands.append(s["arr"])
    for a in blk_ins:
        nd = a.ndim - 1
        in_specs.append(pl.BlockSpec((None,) + a.shape[1:], lambda j, i, nd=nd: (rr(i),) + (0,) * nd))
        operands.append(a)
    for p in params:
        a = p["arr"]
        if p["col"] is None:
            in_specs.append(pl.BlockSpec(a.shape, lambda j, i: (0, 0)))
        else:
            in_specs.append(pl.BlockSpec((a.shape[0], p["w"]), lambda j, i, p=p: (0, _colidx(p["col"], j))))
        operands.append(a)

    out_specs, out_shape = [], []
    for o in row_outs:
        out_specs.append(pl.BlockSpec((ts, o["w"]), lambda j, i, o=o: (rr(i), _colidx(o["col"], j))))
        out_shape.append(jax.ShapeDtypeStruct((n_rows, o["total"]), o["dtype"]))
    for o in blk_outs:
        nd = len(o["shape"])
        out_specs.append(pl.BlockSpec((None,) + tuple(o["shape"]), lambda j, i, nd=nd: (rr(i),) + (0,) * nd))
        out_shape.append(jax.ShapeDtypeStruct((nblk,) + tuple(o["shape"]), o["dtype"]))
    for o in acc_outs:
        if o["col"] is None:
            out_specs.append(pl.BlockSpec(o["shape"], lambda j, i: (0, 0)))
            out_shape.append(jax.ShapeDtypeStruct(o["shape"], F32))
        else:
            out_specs.append(pl.BlockSpec(o["shape"], lambda j, i, o=o: (0, _colidx(o["col"], j))))
            out_shape.append(jax.ShapeDtypeStruct((o["shape"][0], o["total"]), F32))

    n_in = len(operands)
    n_row_out, n_blk_out, n_acc = len(row_outs), len(blk_outs), len(acc_outs)

    def kern(*refs):
        in_refs = refs[:n_in]
        out_refs = refs[n_in:n_in + n_row_out + n_blk_out + n_acc]
        carry_refs = refs[n_in + n_row_out + n_blk_out + n_acc:]
        acc_refs = out_refs[n_row_out + n_blk_out:]
        i = pl.program_id(1)
        r = rr(i)

        @pl.when(i == 0)
        def _():
            for c_ref in carry_refs:
                c_ref[...] = jnp.zeros(c_ref.shape, c_ref.dtype)
            for a_ref in acc_refs:
                a_ref[...] = jnp.zeros(a_ref.shape, a_ref.dtype)

        k = 0
        xs = []
        for s in row_ins:
            x = in_refs[k][...]
            k += 1
            if s["halo"]:
                hal = in_refs[k][...]
                k += 1
                hal = jnp.where(r == 0, jnp.zeros_like(hal), hal)
                x = jnp.concatenate([hal, x], axis=0)
            xs.append(x)
        blks = []
        for _ in blk_ins:
            blks.append(in_refs[k][...])
            k += 1
        ps = []
        for _ in params:
            ps.append(in_refs[k][...])
            k += 1
        row_vals, blk_vals, acc_vals, new_carries = body(xs, blks, ps, [c[...] for c in carry_refs], r)
        for ref, val in zip(out_refs[:n_row_out], row_vals):
            ref[...] = val.astype(ref.dtype)
        for ref, val in zip(out_refs[n_row_out:n_row_out + n_blk_out], blk_vals):
            ref[...] = val.astype(ref.dtype)
        for ref, val in zip(acc_refs, acc_vals):
            ref[...] += val
        for ref, val in zip(carry_refs, new_carries):
            ref[...] = val

    res = pl.pallas_call(
        kern,
        name=name,
        grid=(ncol, nblk),
        in_specs=in_specs,
        out_specs=out_specs,
        out_shape=out_shape,
        scratch_shapes=[pltpu.VMEM(shape, F32) for shape in carries],
        compiler_params=pltpu.CompilerParams(dimension_semantics=("arbitrary", "arbitrary"),
                                             vmem_limit_bytes=VMEM_LIMIT),
    )(*operands)
    return list(res)


def _out(w, dtype, total=None, col=0):
    return dict(w=w, dtype=dtype, total=w if total is None else total, col=col)


def seq_fwd(name, fn, n_rows, ts, row_ins, params, outs, state_shapes=(), ncol=1):
    def body(xs, blks, ps, carries, r):
        o, new_st = fn(xs, list(carries), ps)
        return o, list(carries), [], new_st

    res = _block_call(name, body, n_rows, ts, ncol, False, row_ins, [], params, outs,
                      [dict(shape=s, dtype=F32) for s in state_shapes], [], list(state_shapes))
    return res[:len(outs)], res[len(outs):]


def seq_bwd(name, fn, n_rows, ts, row_ins, params, cots, saved_states=(), din_dtypes=None, ncol=1, din_specs=None):
    n_x, n_p, n_st = len(row_ins), len(params), len(saved_states)
    halo_idx = [k for k, s in enumerate(row_ins) if s["halo"]]
    state_shapes = [a.shape[1:] for a in saved_states]

    def body(xs_all, blks, ps, carries, r):
        xs, cot_vals = xs_all[:n_x], xs_all[n_x:]
        d_state, d_halo = carries[:n_st], carries[n_st:]
        (o, _), vjp = jax.vjp(lambda a, b, c: fn(a, b, c), xs, blks, ps)
        cot = [c.astype(v.dtype) for c, v in zip(cot_vals, o)]
        dxs, dst, dps = vjp((cot, list(d_state)))
        row_vals, new_halo = [], []
        for k, dx in enumerate(dxs):
            if k in halo_idx:
                hk = halo_idx.index(k)
                body_rows = dx[HALO:]
                tail = dx[ts:ts + HALO] + d_halo[hk]
                row_vals.append(jnp.concatenate([body_rows[:ts - HALO], tail], axis=0))
                new_halo.append(dx[:HALO])
            else:
                row_vals.append(dx)
        return row_vals, [], list(dps), list(dst) + new_halo

    din_dtypes = din_dtypes or [F32] * n_x
    douts = []
    for k, s in enumerate(row_ins):
        total, col = (s["w"], 0) if din_specs is None or din_specs[k] is None else din_specs[k]
        douts.append(_out(s["w"], din_dtypes[k], total, col))
    accs = []
    for p in params:
        a = p["arr"]
        if p["col"] is None:
            accs.append(dict(shape=a.shape, total=None, col=None))
        else:
            accs.append(dict(shape=(a.shape[0], p["w"]), total=a.shape[1], col=p["col"]))
    carries = list(state_shapes) + [(HALO, row_ins[k]["w"]) for k in halo_idx]
    res = _block_call(name, body, n_rows, ts, ncol, True, list(row_ins) + list(cots), list(saved_states), params,
                      douts, [], accs, carries)
    return res[:n_x], res[n_x:]


def matmul(name, a, b, mode, out_dtype, tm, tn, tk):
    if mode == "tn":
        (kk, m), n = a.shape, b.shape[1]
    else:
        (m, kk), n = a.shape, (b.shape[0] if mode == "nt" else b.shape[1])
    tm, tn, tk = min(tm, m), min(tn, n), min(tk, kk)
    nk = kk // tk
    assert m % tm == 0 and n % tn == 0 and kk % tk == 0, (name, a.shape, b.shape, tm, tn, tk)
    a_spec = pl.BlockSpec((tk, tm), lambda i, j, k: (k, i)) if mode == "tn" else pl.BlockSpec((tm, tk), lambda i, j, k: (i, k))
    b_spec = pl.BlockSpec((tn, tk), lambda i, j, k: (j, k)) if mode == "nt" else pl.BlockSpec((tk, tn), lambda i, j, k: (k, j))
    ca, cb = {"nn": (1, 0), "nt": (1, 1), "tn": (0, 0)}[mode]

    def kern(a_ref, b_ref, o_ref, acc_ref):
        part = lax.dot_general(a_ref[...], b_ref[...], (((ca,), (cb,)), ((), ())), preferred_element_type=F32)
        if nk == 1:
            o_ref[...] = part.astype(o_ref.dtype)
        else:
            k = pl.program_id(2)

            @pl.when(k == 0)
            def _():
                acc_ref[...] = part

            @pl.when(k > 0)
            def _():
                acc_ref[...] += part

            @pl.when(k == nk - 1)
            def _():
                o_ref[...] = acc_ref[...].astype(o_ref.dtype)

    return pl.pallas_call(
        kern,
        name=name,
        grid=(m // tm, n // tn, nk),
        in_specs=[a_spec, b_spec],
        out_specs=pl.BlockSpec((tm, tn), lambda i, j, k: (i, j)),
        out_shape=jax.ShapeDtypeStruct((m, n), out_dtype),
        scratch_shapes=[pltpu.VMEM((tm, tn) if nk > 1 else (8, 128), F32)],
        compiler_params=pltpu.CompilerParams(dimension_semantics=("parallel", "parallel", "arbitrary"),
                                             vmem_limit_bytes=VMEM_LIMIT),
    )(a, b)


def _my_pos():
    return lax.axis_index("x"), lax.axis_index("y"), lax.axis_index("c")


def all_gather(name, x, in_vmem):
    def body(x_ref, out_ref, send_sems, recv_sems, local_sem):
        x_, y_, c_ = _my_pos()
        me, sibling = (x_, y_, c_), (x_, y_, 1 - c_)
        chips = [(1 - x_, y_), (x_, 1 - y_), (1 - x_, 1 - y_)]

        def slot(px, py, pc):
            return out_ref.at[4 * px + 2 * py + pc]

        def copy(k, block, to, src=None):
            return pltpu.make_async_remote_copy(
                src_ref=slot(*block) if src is None else src, dst_ref=slot(*block),
                send_sem=send_sems.at[k], recv_sem=recv_sems.at[k], device_id=to, device_id_type=MESH)

        mine = pltpu.make_async_copy(x_ref, slot(*me), local_sem)
        mine.start()
        first = [copy(0, me, sibling, src=x_ref)]
        first += [copy(1 + j, me, (*chip, c_), src=x_ref) for j, chip in enumerate(chips)]
        for cp in first:
            cp.start()
        passed = [copy(4 + j, (*chip, c_), sibling) for j, chip in enumerate(chips)]
        for j, chip in enumerate(chips):
            copy(1 + j, (*chip, c_), me).wait_recv()
            passed[j].start()
        copy(0, sibling, me).wait_recv()
        for j, chip in enumerate(chips):
            copy(4 + j, (*chip, 1 - c_), me).wait_recv()
        for cp in first + passed:
            cp.wait_send()
        mine.wait()

    space = pltpu.VMEM if in_vmem else pl.ANY
    return pl.pallas_call(
        body,
        name=name,
        out_shape=jax.ShapeDtypeStruct((N_DEV,) + x.shape, x.dtype),
        in_specs=[pl.BlockSpec(memory_space=space)],
        out_specs=pl.BlockSpec(memory_space=space),
        scratch_shapes=[pltpu.SemaphoreType.DMA((7,)), pltpu.SemaphoreType.DMA((7,)), pltpu.SemaphoreType.DMA],
        compiler_params=pltpu.CompilerParams(vmem_limit_bytes=VMEM_LIMIT),
    )(x)


def all_to_all(name, g):
    def body(g_ref, out_ref, send_sems, recv_sems, local_sem):
        x_, y_, c_ = _my_pos()
        me = 4 * x_ + 2 * y_ + c_

        def peer(k):
            fx, fy, fc = (k >> 2) & 1, (k >> 1) & 1, k & 1
            return (1 - x_ if fx else x_, 1 - y_ if fy else y_, 1 - c_ if fc else c_)

        def copy(k):
            px, py, pc = peer(k)
            return pltpu.make_async_remote_copy(
                src_ref=g_ref.at[4 * px + 2 * py + pc], dst_ref=out_ref.at[me],
                send_sem=send_sems.at[k - 1], recv_sem=recv_sems.at[k - 1], device_id=(px, py, pc), device_id_type=MESH)

        def landing(k):
            px, py, pc = peer(k)
            return pltpu.make_async_remote_copy(
                src_ref=g_ref.at[me], dst_ref=out_ref.at[4 * px + 2 * py + pc],
                send_sem=send_sems.at[k - 1], recv_sem=recv_sems.at[k - 1], device_id=(px, py, pc), device_id_type=MESH)

        mine = pltpu.make_async_copy(g_ref.at[me], out_ref.at[me], local_sem)
        mine.start()
        sends = [copy(k) for k in range(1, N_DEV)]
        for cp in sends:
            cp.start()
        for k in range(1, N_DEV):
            landing(k).wait_recv()
        for cp in sends:
            cp.wait_send()
        mine.wait()

    return pl.pallas_call(
        body,
        name=name,
        out_shape=jax.ShapeDtypeStruct(g.shape, g.dtype),
        in_specs=[pl.BlockSpec(memory_space=pl.ANY)],
        out_specs=pl.BlockSpec(memory_space=pl.ANY),
        scratch_shapes=[pltpu.SemaphoreType.DMA((7,)), pltpu.SemaphoreType.DMA((7,)), pltpu.SemaphoreType.DMA],
    )(g)


def sum_blocks(name, g):
    def body(g_ref, o_ref):
        acc = g_ref[0]
        for s in range(1, N_DEV):
            acc = acc + g_ref[s]
        o_ref[...] = acc

    r = g.shape[1]
    tr = r // 4 if r % 32 == 0 else r
    return pl.pallas_call(
        body, name=name, grid=(r // tr,),
        in_specs=[pl.BlockSpec((N_DEV, tr, 128), lambda i: (0, i, 0))],
        out_specs=pl.BlockSpec((tr, 128), lambda i: (i, 0)),
        out_shape=jax.ShapeDtypeStruct((r, 128), F32),
        compiler_params=pltpu.CompilerParams(vmem_limit_bytes=VMEM_LIMIT),
    )(g)


def _adamw_math(w, g, m, v):
    m = ADAM_B1 * m + (1.0 - ADAM_B1) * g
    v = ADAM_B2 * v + (1.0 - ADAM_B2) * (g * g)
    m_hat = m / (1.0 - ADAM_B1 ** ADAM_STEP)
    v_hat = v / (1.0 - ADAM_B2 ** ADAM_STEP)
    delta = -ADAM_LR * (m_hat / (jnp.sqrt(v_hat) + ADAM_EPS) + ADAM_WD * w)
    return delta, m, v


def adamw_big(name, w, land, m, v, tr):
    r, c = w.shape

    def body(w_ref, l_ref, m_ref, v_ref, g_out, d_out, m_out, v_out):
        g = l_ref[0].astype(F32)
        for s in range(1, N_DEV):
            g = g + l_ref[s].astype(F32)
        delta, m_new, v_new = _adamw_math(w_ref[...], g, m_ref[...], v_ref[...])
        g_out[...] = g
        d_out[...] = delta
        m_out[...] = m_new
        v_out[...] = v_new

    spec = pl.BlockSpec((tr, c), lambda i: (i, 0))
    return pl.pallas_call(
        body, name=name, grid=(r // tr,),
        in_specs=[spec, pl.BlockSpec((N_DEV, tr, c), lambda i: (0, i, 0)), spec, spec],
        out_specs=[spec] * 4,
        out_shape=[jax.ShapeDtypeStruct((r, c), F32)] * 4,
        compiler_params=pltpu.CompilerParams(dimension_semantics=("parallel",), vmem_limit_bytes=VMEM_LIMIT),
    )(w, land, m, v)


def adamw_small(name, w, g, m, v):
    def body(w_ref, g_ref, m_ref, v_ref, d_out, m_out, v_out):
        delta, m_new, v_new = _adamw_math(w_ref[...], g_ref[...], m_ref[...], v_ref[...])
        d_out[...] = delta
        m_out[...] = m_new
        v_out[...] = v_new

    return pl.pallas_call(
        body, name=name,
        out_shape=[jax.ShapeDtypeStruct(w.shape, F32)] * 3,
        compiler_params=pltpu.CompilerParams(vmem_limit_bytes=VMEM_LIMIT),
    )(w, g, m, v)


def cast_bf16(name, w, tr):
    r, c = w.shape

    def body(w_ref, o_ref):
        o_ref[...] = w_ref[...].astype(BF16)

    spec = pl.BlockSpec((tr, c), lambda i: (i, 0))
    return pl.pallas_call(body, name=name, grid=(r // tr,), in_specs=[spec], out_specs=spec,
                          out_shape=jax.ShapeDtypeStruct((r, c), BF16),
                          compiler_params=pltpu.CompilerParams(dimension_semantics=("parallel",)))(w)


def _pack(arrs):
    flat = jnp.concatenate([a.reshape(-1).astype(F32) for a in arrs])
    n = flat.shape[0]
    pad = (-n) % 1024
    return jnp.pad(flat, (0, pad)).reshape(-1, 128)


def _unpack(packed, shapes):
    flat = packed.reshape(-1)
    out, off = [], 0
    for s in shapes:
        n = math.prod(s)
        out.append(flat[off:off + n].reshape(s))
        off += n
    return out


def _block_diag(w):
    h, d, _ = w.shape
    eye = jnp.eye(h, dtype=w.dtype)
    return (eye[:, None, :, None] * w[:, :, None, :]).reshape(h * d, h * d)


def _block_diag_grad(g, h):
    d = g.shape[0] // h
    g4 = g.reshape(h, d, h, d)
    return jnp.stack([g4[k, :, k, :] for k in range(h)])


def _layer_params(wt, l):
    gp = jnp.zeros((8, 128), F32)
    gp = gp.at[0, 4:8].set(wt["gdn_a_log"][l]).at[1, 4:8].set(wt["gdn_dt_bias"][l])
    d_ffh = wt["ffn_conv_w"].shape[-1] // 2
    return dict(
        pre_mix=wt["pre_mix_norm"][l][None], post_mix=wt["post_mix_norm"][l][None],
        pre_ffn=wt["pre_ffn_norm"][l][None], post_ffn=wt["post_ffn_norm"][l][None],
        gdn_cw=wt["gdn_conv_w"][l], gdn_gp=gp, gdn_nw=wt["gdn_norm_w"][l][None],
        lru_cw=wt["lru_conv_w"][l], lru_cb=wt["lru_conv_b"][l][None],
        lru_wa=_block_diag(wt["lru_wa"][l]), lru_ba=wt["lru_ba"][l].reshape(1, -1),
        lru_wx=_block_diag(wt["lru_wx"][l]), lru_bx=wt["lru_bx"][l].reshape(1, -1),
        lru_lam=wt["lru_lambda"][l][None], gw0=wt["grp_norm_w"][l, 0][None], gw1=wt["grp_norm_w"][l, 1][None],
        gw2=wt["grp_norm_w"][l, 2][None],
        sgu_lnw=wt["sgu_ln_w"][l][None], sgu_lnb=wt["sgu_ln_b"][l][None],
        sgu_ws=wt["sgu_ws"][l].reshape(SGU_GROUPS * SGU_CHUNK, SGU_CHUNK),
        sgu_bt=jnp.pad(wt["sgu_b"][l].T, ((0, 0), (0, 128 - SGU_GROUPS))),
        sc_cw=wt["sconv_w"][l],
        ffn_cw=wt["ffn_conv_w"][l], ffn_cb=wt["ffn_conv_b"][l][None], d_ffh=d_ffh,
    )


TS_ROW = 256
TS_GDN = 128
TS_FFN = 256
TC_FFN = 512


def _mixers_fwd(l, p, lp, n):
    qkv = _rin(p, 3 * D_G, 0, halo=True)
    z = _rin(p, D_G, 3)
    ba = _rin(p, 128, BA_COL // 128)
    gdn_ps = [_par(lp["gdn_cw"]), _par(lp["gdn_gp"]), _par(lp["gdn_nw"])]
    (y_a,), (gdn_st,) = seq_fwd(f"gdn_fwd_{l}", fn_gdn, n, TS_GDN, [qkv, z, ba], gdn_ps, [_out(D_G, BF16)],
                                state_shapes=[(GDN_HEADS * GDN_DIM, GDN_DIM)])
    lru_x = _rin(p, D_G, 4, halo=True)
    lru_gate = _rin(p, D_G, 5)
    lru_ps = [_par(lp[k]) for k in ("lru_cw", "lru_cb", "lru_wa", "lru_ba", "lru_wx", "lru_bx", "lru_lam", "gw0")]
    (y_b,), (lru_st,) = seq_fwd(f"lru_fwd_{l}", fn_lru, n, TS_ROW, [lru_x, lru_gate], lru_ps, [_out(D_G, BF16)],
                                state_shapes=[(1, D_G)])
    uv = _rin(p, 2 * D_G, 3)
    sgu_ps = [_par(lp[k]) for k in ("sgu_lnw", "sgu_lnb", "sgu_ws", "sgu_bt", "gw1")]
    (y_c,), _ = seq_fwd(f"sgu_fwd_{l}", fn_sgu, n, TS_ROW, [uv], sgu_ps, [_out(D_G, BF16)])
    sc = [_rin(p, D_G, 8), _rin(p, D_G, 9, halo=True), _rin(p, D_G, 10, halo=True)]
    sc_ps = [_par(lp["sc_cw"]), _par(lp["gw2"])]
    (y_d,), _ = seq_fwd(f"sconv_fwd_{l}", fn_sconv, n, TS_ROW, sc, sc_ps, [_out(D_G, BF16)])
    ins = dict(gdn=([qkv, z, ba], gdn_ps, [gdn_st]), lru=([lru_x, lru_gate], lru_ps, [lru_st]),
               sgu=([uv], sgu_ps, []), sc=(sc, sc_ps, []))
    return jnp.concatenate([y_a, y_b, y_c, y_d], axis=1), ins


def _mixers_bwd(l, dymix, ins, n):
    cot = lambda g: [_rin(dymix, D_G, g)]
    xs, ps, st = ins["gdn"]
    (dqkv, dz, dba), g_gdn = seq_bwd(f"gdn_bwd_{l}", fn_gdn, n, TS_GDN, xs, ps, cot(0), st, [BF16, BF16, BF16])
    xs, ps, st = ins["lru"]
    (dlx, dlg), g_lru = seq_bwd(f"lru_bwd_{l}", fn_lru, n, TS_ROW, xs, ps, cot(1), st, [BF16, BF16])
    xs, ps, st = ins["sgu"]
    (duv,), g_sgu = seq_bwd(f"sgu_bwd_{l}", fn_sgu, n, TS_ROW, xs, ps, cot(2), st, [BF16])
    xs, ps, st = ins["sc"]
    (dsb, dsc, dsh), g_sc = seq_bwd(f"sconv_bwd_{l}", fn_sconv, n, TS_ROW, xs, ps, cot(3), st, [BF16, BF16, BF16])
    dp = jnp.concatenate([dqkv, dz, dlx, dlg, duv, dsb, dsc, dsh, dba], axis=1)
    return dp, dict(gdn=g_gdn, lru=g_lru, sgu=g_sgu, sc=g_sc)


def _ffn_ops(hid, lp):
    d_ffh = lp["d_ffh"]
    off = d_ffh // TC_FFN
    xs = [_rin(hid, TC_FFN, lambda j: j, halo=True), _rin(hid, TC_FFN, lambda j: j + off, halo=True)]
    ps = [_par(lp["ffn_cw"], TC_FFN, lambda j: j), _par(lp["ffn_cw"], TC_FFN, lambda j: j + off),
          _par(lp["ffn_cb"], TC_FFN, lambda j: j), _par(lp["ffn_cb"], TC_FFN, lambda j: j + off)]
    return xs, ps, d_ffh


def local_step(x, target, wt):
    n, d = x.shape
    depth = wt["pre_mix_norm"].shape[0]
    lps = [_layer_params(wt, l) for l in range(depth)]
    saved = []
    xin = x
    (h,), _ = seq_fwd("norm_fwd", fn_norm, n, TS_ROW, [_rin(x)], [_par(lps[0]["pre_mix"])], [_out(d, BF16)])
    dx_last = loss = None
    for l in range(depth):
        lp = lps[l]
        p = matmul(f"w_in_fwd_{l}", h, wt["w_in"][l], "nn", F32, 1024, 640, d)
        ymix, mix_ins = _mixers_fwd(l, p, lp, n)
        y = matmul(f"w_out_fwd_{l}", ymix, wt["w_out"][l], "nn", F32, 1024, 1024, d)
        res_ps = [_par(lp["post_mix"]), _par(lp["pre_ffn"])]
        (x1, h2), _ = seq_fwd(f"res_mix_fwd_{l}", fn_res, n, TS_ROW, [_rin(xin), _rin(y)], res_ps,
                              [_out(d, F32), _out(d, BF16)])
        hid = matmul(f"ffn_up_fwd_{l}", h2, wt["ffn_up"][l], "nn", F32, 1024, 1024, d)
        f_xs, f_ps, d_ffh = _ffn_ops(hid, lp)
        (act,), _ = seq_fwd(f"ffn_act_fwd_{l}", fn_ffn, n, TS_FFN, f_xs, f_ps,
                            [_out(TC_FFN, BF16, d_ffh, lambda j: j)], ncol=d_ffh // TC_FFN)
        yf = matmul(f"ffn_down_fwd_{l}", act, wt["ffn_down"][l], "nn", F32, 1024, 1024, d_ffh // 4)
        rec = dict(x=xin, h=h, mix_ins=mix_ins, ymix=ymix, y=y, x1=x1, h2=h2, f_xs=f_xs, f_ps=f_ps, act=act, yf=yf)
        if l + 1 < depth:
            ps = [_par(lp["post_ffn"]), _par(lps[l + 1]["pre_mix"])]
            (x2, h), _ = seq_fwd(f"res_ffn_fwd_{l}", fn_res, n, TS_ROW, [_rin(x1), _rin(yf)], ps,
                                 [_out(d, F32), _out(d, BF16)])
            rec["res_ffn_ps"] = ps
            xin = x2
        else:
            def body(xs, blks, ps, carries, r):
                x1_, yf_, t_ = xs
                e = x1_ + _rms(yf_, ps[0]) - t_
                part = 0.5 * jnp.sum(jnp.mean(e * e, axis=-1, keepdims=True), axis=0, keepdims=True)
                return [e * (1.0 / d)], [], [jnp.broadcast_to(part, (8, 128))], []

            ps = [_par(lp["post_ffn"])]
            dx_last, loss = _block_call("loss_fwd", body, n, TS_ROW, 1, False, [_rin(x1), _rin(yf), _rin(target)],
                                        [], ps, [_out(d, F32)], [], [dict(shape=(8, 128), total=None, col=None)], [])
            rec["res_ffn_ps"] = ps
        saved.append(rec)

    grads = {}
    dx2, dh_next = dx_last, None
    for l in reversed(range(depth)):
        rec, lp = saved[l], lps[l]
        d_ffh = lp["d_ffh"]
        g = {}
        if dh_next is None:
            (dx1, dyf), (g["post_ffn"],) = seq_bwd(f"res_ffn_bwd_{l}", fn_res_last, n, TS_ROW,
                                                   [_rin(rec["x1"]), _rin(rec["yf"])], rec["res_ffn_ps"], [_rin(dx2)],
                                                   din_dtypes=[F32, BF16])
        else:
            (dx1, dyf), (g["post_ffn"], g_next_pre) = seq_bwd(
                f"res_ffn_bwd_{l}", fn_res, n, TS_ROW, [_rin(rec["x1"]), _rin(rec["yf"])], rec["res_ffn_ps"],
                [_rin(dx2), _rin(dh_next)], din_dtypes=[F32, BF16])
            grads[l + 1]["pre_mix"] = g_next_pre
        dact = matmul(f"ffn_down_dx_{l}", dyf, wt["ffn_down"][l], "nt", BF16, 1024, 512, d)
        g["ffn_down"] = matmul(f"ffn_down_dw_{l}", rec["act"], dyf, "tn", BF16, 512, 1024, 1024)
        off = d_ffh // TC_FFN
        (dhg, dhv), (g_cwg, g_cwv, g_cbg, g_cbv) = seq_bwd(
            f"ffn_act_bwd_{l}", fn_ffn, n, TS_FFN, rec["f_xs"], rec["f_ps"], [_rin(dact, TC_FFN, lambda j: j)],
            din_dtypes=[BF16, BF16], ncol=off, din_specs=[(d_ffh, lambda j: j), (d_ffh, lambda j: j)])
        dhid = jnp.concatenate([dhg, dhv], axis=1)
        g["ffn_cw"] = jnp.concatenate([g_cwg[:, :d_ffh], g_cwv[:, d_ffh:]], axis=1)
        g["ffn_cb"] = jnp.concatenate([g_cbg[:, :d_ffh], g_cbv[:, d_ffh:]], axis=1)
        dh2 = matmul(f"ffn_up_dx_{l}", dhid, wt["ffn_up"][l], "nt", BF16, 1024, 1024, 1024)
        g["ffn_up"] = matmul(f"ffn_up_dw_{l}", rec["h2"], dhid, "tn", BF16, 1024, 1024, 1024)
        (dx, dy), (g["post_mix"], g["pre_ffn"]) = seq_bwd(
            f"res_mix_bwd_{l}", fn_res, n, TS_ROW, [_rin(rec["x"]), _rin(rec["y"])],
            [_par(lp["post_mix"]), _par(lp["pre_ffn"])], [_rin(dx1), _rin(dh2)], din_dtypes=[F32, BF16])
        dymix = matmul(f"w_out_dx_{l}", dy, wt["w_out"][l], "nt", BF16, 1024, 1024, d)
        g["w_out"] = matmul(f"w_out_dw_{l}", rec["ymix"], dy, "tn", BF16, 1024, 1024, 1024)
        dp, g["mix"] = _mixers_bwd(l, dymix, rec["mix_ins"], n)
        dh = matmul(f"w_in_dx_{l}", dp, wt["w_in"][l], "nt", BF16, 1024, 1024, N_INP // 5)
        g["w_in"] = matmul(f"w_in_dw_{l}", rec["h"], dp, "tn", BF16, 1024, 640, 1024)
        grads[l] = g
        dx2, dh_next = dx, dh
    (grad_x,), (g_pre0,) = seq_bwd("norm_bwd", fn_norm_keep, n, TS_ROW, [_rin(x)], [_par(lps[0]["pre_mix"])],
                                   [_rin(dh_next), _rin(dx2)])
    grads[0]["pre_mix"] = g_pre0
    return loss[0, 0], grad_x, _name_grads(grads, depth)


def _name_grads(grads, depth):
    per = {k: [] for k in WEIGHTS}
    for l in range(depth):
        g = grads[l]
        m = g["mix"]
        cw, gp, nw = m["gdn"]
        lcw, lcb, lwa, lba, lwx, lbx, llam, gw0 = m["lru"]
        lnw, lnb, ws, bst, gw1 = m["sgu"]
        scw, gw2 = m["sc"]
        per["pre_mix_norm"].append(g["pre_mix"][0])
        per["w_in"].append(g["w_in"])
        per["gdn_conv_w"].append(cw)
        per["gdn_a_log"].append(gp[0, 4:8])
        per["gdn_dt_bias"].append(gp[1, 4:8])
        per["gdn_norm_w"].append(nw[0])
        per["lru_conv_w"].append(lcw)
        per["lru_conv_b"].append(lcb[0])
        per["lru_wa"].append(_block_diag_grad(lwa, LRU_BLOCKS))
        per["lru_ba"].append(lba.reshape(LRU_BLOCKS, -1))
        per["lru_wx"].append(_block_diag_grad(lwx, LRU_BLOCKS))
        per["lru_bx"].append(lbx.reshape(LRU_BLOCKS, -1))
        per["lru_lambda"].append(llam[0])
        per["sgu_ln_w"].append(lnw[0])
        per["sgu_ln_b"].append(lnb[0])
        per["sgu_ws"].append(ws.reshape(SGU_GROUPS, SGU_CHUNK, SGU_CHUNK))
        per["sgu_b"].append(bst[:, :SGU_GROUPS].T)
        per["sconv_w"].append(scw)
        per["grp_norm_w"].append(jnp.concatenate([gw0, gw1, gw2], axis=0))
        per["w_out"].append(g["w_out"])
        per["post_mix_norm"].append(g["post_mix"][0])
        per["pre_ffn_norm"].append(g["pre_ffn"][0])
        per["ffn_up"].append(g["ffn_up"])
        per["ffn_conv_w"].append(g["ffn_cw"])
        per["ffn_conv_b"].append(g["ffn_cb"][0])
        per["ffn_down"].append(g["ffn_down"])
        per["post_ffn_norm"].append(g["post_ffn"][0])
    return {k: jnp.stack(v) for k, v in per.items()}


def _regroup_w_in(w):
    pad = jnp.zeros(w.shape[:-1] + (N_INP - N_IN,), w.dtype)
    return jnp.concatenate([w[..., :2048], w[..., 2056:], w[..., 2048:2056], pad], axis=-1)


def _ungroup_w_in(g):
    return jnp.concatenate([g[..., :2048], g[..., BA_COL:BA_COL + 8], g[..., 2048:BA_COL]], axis=-1)


def _cols_from_blocks(b):
    n, l, r, c = b.shape
    return b.transpose(1, 2, 0, 3).reshape(l, r, n * c)


def _cols_to_blocks(a):
    l, r, c8 = a.shape
    return a.reshape(l, r, N_DEV, c8 // N_DEV).transpose(2, 0, 1, 3)


def _rows_from_blocks(b):
    n, l, r, c = b.shape
    return b.transpose(1, 0, 2, 3).reshape(l, n * r, c)


def _rows_to_blocks(a):
    l, r8, c = a.shape
    return a.reshape(l, N_DEV, r8 // N_DEV, c).transpose(1, 0, 2, 3)


def kernel(x, pre_mix_norm, w_in, gdn_conv_w, gdn_a_log, gdn_dt_bias, gdn_norm_w, lru_conv_w, lru_conv_b, lru_wa, lru_ba, lru_wx, lru_bx, lru_lambda, sgu_ln_w, sgu_ln_b, sgu_ws, sgu_b, sconv_w, grp_norm_w, w_out, post_mix_norm, pre_ffn_norm, ffn_up, ffn_conv_w, ffn_conv_b, ffn_down, post_ffn_norm, loss_target, m_pre_mix_norm, m_w_in, m_gdn_conv_w, m_gdn_a_log, m_gdn_dt_bias, m_gdn_norm_w, m_lru_conv_w, m_lru_conv_b, m_lru_wa, m_lru_ba, m_lru_wx, m_lru_bx, m_lru_lambda, m_sgu_ln_w, m_sgu_ln_b, m_sgu_ws, m_sgu_b, m_sconv_w, m_grp_norm_w, m_w_out, m_post_mix_norm, m_pre_ffn_norm, m_ffn_up, m_ffn_conv_w, m_ffn_conv_b, m_ffn_down, m_post_ffn_norm, v_pre_mix_norm, v_w_in, v_gdn_conv_w, v_gdn_a_log, v_gdn_dt_bias, v_gdn_norm_w, v_lru_conv_w, v_lru_conv_b, v_lru_wa, v_lru_ba, v_lru_wx, v_lru_bx, v_lru_lambda, v_sgu_ln_w, v_sgu_ln_b, v_sgu_ws, v_sgu_b, v_sconv_w, v_grp_norm_w, v_w_out, v_post_mix_norm, v_pre_ffn_norm, v_ffn_up, v_ffn_conv_w, v_ffn_conv_b, v_ffn_down, v_post_ffn_norm):
    args = locals()
    w_loc = {k: args[k] for k in WEIGHTS}
    m_loc = {k: args["m_" + k] for k in WEIGHTS}
    v_loc = {k: args["v_" + k] for k in WEIGHTS}
    depth = pre_mix_norm.shape[0]
    x_, y_, c_ = _my_pos()
    me = 4 * x_ + 2 * y_ + c_

    def gather_big(name):
        w = w_loc[name]
        l, r, c = w.shape
        wb = cast_bf16(f"cast_{name}", w.reshape(l * r, c), 256 if (l * r) % 256 == 0 else 64)
        return all_gather(f"gather_{name}", wb, False).reshape(N_DEV, l, r, c)

    wt = dict(w_loc)
    wt["w_in"] = _regroup_w_in(_cols_from_blocks(gather_big("w_in")))
    wt["ffn_up"] = _cols_from_blocks(gather_big("ffn_up"))
    wt["w_out"] = _rows_from_blocks(gather_big("w_out"))
    wt["ffn_down"] = _rows_from_blocks(gather_big("ffn_down"))
    shard_shapes = [w_loc[k].shape for k in SHARDED_SMALL]
    gathered = all_gather("gather_small", _pack([w_loc[k] for k in SHARDED_SMALL]), True)
    per_dev = [_unpack(gathered[s], shard_shapes) for s in range(N_DEV)]
    for k_i, k in enumerate(SHARDED_SMALL):
        wt[k] = jnp.concatenate([per_dev[s][k_i] for s in range(N_DEV)], axis=-1)

    loss_part, grad_x, g_full = local_step(x[0], loss_target[0], wt)
    loss = lax.psum(loss_part, ("x", "y", "c"))

    outs_g, outs_d, outs_m, outs_v = {}, {}, {}, {}
    blocks = dict(w_in=_cols_to_blocks(_ungroup_w_in(g_full["w_in"])), ffn_up=_cols_to_blocks(g_full["ffn_up"]),
                  w_out=_rows_to_blocks(g_full["w_out"]), ffn_down=_rows_to_blocks(g_full["ffn_down"]))
    for name in BIG:
        l, r, c = w_loc[name].shape
        land = all_to_all(f"exchange_{name}", blocks[name].reshape(N_DEV, l * r, c))
        res = adamw_big(f"adamw_{name}", w_loc[name].reshape(l * r, c), land, m_loc[name].reshape(l * r, c),
                        v_loc[name].reshape(l * r, c), 128 if (l * r) % 128 == 0 else 64)
        outs_g[name], outs_d[name], outs_m[name], outs_v[name] = [a.reshape(l, r, c) for a in res]

    full_shapes = [g_full[k].shape for k in SMALL]
    g_all = all_gather("gather_small_grads", _pack([g_full[k] for k in SMALL]), True)
    g_sum = _unpack(sum_blocks("sum_small_grads", g_all), full_shapes)
    g_small = {}
    for k, g in zip(SMALL, g_sum):
        if k in SHARDED_SMALL:
            w = w_loc[k].shape[-1]
            g = lax.dynamic_slice_in_dim(g, me * w, w, axis=g.ndim - 1)
        g_small[k] = g
    loc_shapes = [w_loc[k].shape for k in SMALL]
    res = adamw_small("adamw_small", _pack([w_loc[k] for k in SMALL]), _pack([g_small[k] for k in SMALL]),
                      _pack([m_loc[k] for k in SMALL]), _pack([v_loc[k] for k in SMALL]))
    d_s, m_s, v_s = [_unpack(a, loc_shapes) for a in res]
    for k_i, k in enumerate(SMALL):
        outs_g[k], outs_d[k], outs_m[k], outs_v[k] = g_small[k], d_s[k_i], m_s[k_i], v_s[k_i]

    return (loss, grad_x[None], *[outs_g[k] for k in WEIGHTS], *[outs_d[k] for k in WEIGHTS],
            *[outs_m[k] for k in WEIGHTS], *[outs_v[k] for k in WEIGHTS])
```

```python
import functools
import math

import jax
import jax.numpy as jnp
from jax import lax
from jax.experimental import pallas as pl
from jax.experimental.pallas import tpu as pltpu

F32 = jnp.float32
BF16 = jnp.bfloat16
EPS = 1e-6
HALO = 8
VMEM_LIMIT = 56 * 1024 * 1024
MESH = pl.DeviceIdType.MESH
N_DEV = 8

ADAM_LR, ADAM_B1, ADAM_B2, ADAM_EPS, ADAM_WD, ADAM_STEP = 0.001, 0.9, 0.999, 1e-08, 0.01, 10

GDN_HEADS, GDN_DIM, GDN_CHUNK = 4, 128, 64
SGU_GROUPS, SGU_CHUNK = 4, 128
LRU_BLOCKS, LRU_C = 8, 8.0
D_G = 512
N_IN = 5640
N_INP = 5760
BA_COL = 5632

SHARDED_SMALL = ("gdn_conv_w", "lru_conv_w", "sconv_w", "grp_norm_w", "ffn_conv_w")
BIG = ("w_in", "w_out", "ffn_up", "ffn_down")
WEIGHTS = ("pre_mix_norm", "w_in", "gdn_conv_w", "gdn_a_log", "gdn_dt_bias", "gdn_norm_w", "lru_conv_w",
           "lru_conv_b", "lru_wa", "lru_ba", "lru_wx", "lru_bx", "lru_lambda", "sgu_ln_w", "sgu_ln_b", "sgu_ws",
           "sgu_b", "sconv_w", "grp_norm_w", "w_out", "post_mix_norm", "pre_ffn_norm", "ffn_up", "ffn_conv_w",
           "ffn_conv_b", "ffn_down", "post_ffn_norm")
SMALL = tuple(n for n in WEIGHTS if n not in BIG)


def _dot(a, b, ca, cb):
    return lax.dot_general(a.astype(BF16), b.astype(BF16), (((ca,), (cb,)), ((), ())),
                           preferred_element_type=F32)


@jax.custom_vjp
def _mm(a, b):
    return _dot(a, b, 1, 0)


def _mm_f(a, b):
    return _dot(a, b, 1, 0), (a, b)


def _mm_b(res, g):
    a, b = res
    return _dot(g, b, 1, 1), _dot(a, g, 0, 0)


_mm.defvjp(_mm_f, _mm_b)


@jax.custom_vjp
def _mm_nt(a, b):
    return _dot(a, b, 1, 1)


def _mm_nt_f(a, b):
    return _dot(a, b, 1, 1), (a, b)


def _mm_nt_b(res, g):
    a, b = res
    return _dot(g, b, 1, 0), _dot(g, a, 0, 0)


_mm_nt.defvjp(_mm_nt_f, _mm_nt_b)


@jax.custom_vjp
def _mm_tn(a, b):
    return _dot(a, b, 0, 0)


def _mm_tn_f(a, b):
    return _dot(a, b, 0, 0), (a, b)


def _mm_tn_b(res, g):
    a, b = res
    return _dot(b, g, 1, 1), _dot(a, g, 1, 0)


_mm_tn.defvjp(_mm_tn_f, _mm_tn_b)


def _dot_exact(a, b, ca, cb):
    return lax.dot_general(a, b, (((ca,), (cb,)), ((), ())), precision=lax.Precision.HIGHEST,
                           preferred_element_type=F32)


@functools.partial(jax.custom_vjp, nondiff_argnums=(1,))
def _shift_rows(x, s):
    return pltpu.roll(x, s, 0)


def _shift_rows_f(x, s):
    return pltpu.roll(x, s, 0), None


def _shift_rows_b(s, _, g):
    return (pltpu.roll(g, (g.shape[0] - s) % g.shape[0], 0),)


_shift_rows.defvjp(_shift_rows_f, _shift_rows_b)


def _sigmoid(x):
    return 1.0 / (1.0 + jnp.exp(-x))


def _silu(x):
    return x * _sigmoid(x)


def _gelu(x):
    return 0.5 * x * (1.0 + jnp.tanh(0.7978845608028654 * (x + 0.044715 * (x * x * x))))


@jax.custom_vjp
def _softplus(x):
    e = jnp.exp(-jnp.abs(x))
    u = 1.0 + e
    log1p = jnp.where(u == 1.0, e, jnp.log(u) * (e / jnp.where(u == 1.0, 1.0, u - 1.0)))
    return jnp.maximum(x, 0.0) + log1p


def _softplus_f(x):
    return _softplus(x), x


def _softplus_b(x, g):
    return (g * _sigmoid(x),)


_softplus.defvjp(_softplus_f, _softplus_b)


def _neg_expm1(y):
    return -jnp.tanh(0.5 * y) * (jnp.exp(y) + 1.0)


def _rms(x, w):
    return x * lax.rsqrt(jnp.mean(x * x, axis=-1, keepdims=True) + EPS) * w


def _row(w, k):
    sel = lax.broadcasted_iota(jnp.int32, w.shape, 0) == k
    return jnp.sum(jnp.where(sel, w, 0.0), axis=0, keepdims=True)


def _col(x, j):
    sel = lax.broadcasted_iota(jnp.int32, x.shape, 1) == j
    return jnp.sum(jnp.where(sel, x, 0.0), axis=1, keepdims=True)


def _conv(x_ext, w, taps):
    acc = None
    for k in range(taps):
        s = taps - 1 - k
        t = (x_ext if s == 0 else _shift_rows(x_ext, s)) * _row(w, k)
        acc = t if acc is None else acc + t
    return acc[HALO:]


@jax.custom_vjp
def _scan(a, b, h0):
    n = a.shape[0]
    row = lax.broadcasted_iota(jnp.int32, a.shape, 0)
    s = 1
    while s < n:
        keep = row >= s
        a_sh = jnp.where(keep, pltpu.roll(a, s, 0), 1.0)
        b_sh = jnp.where(keep, pltpu.roll(b, s, 0), 0.0)
        b = a * b_sh + b
        a = a * a_sh
        s *= 2
    return b + a * h0


def _scan_f(a, b, h0):
    h = _scan(a, b, h0)
    return h, (a, h, h0)


def _scan_b(res, dh):
    a, h, h0 = res
    n = a.shape[0]
    row = lax.broadcasted_iota(jnp.int32, a.shape, 0)
    an = jnp.where(row < n - 1, pltpu.roll(a, n - 1, 0), 0.0)
    lam = dh
    s = 1
    while s < n:
        keep = row < n - s
        a_sh = jnp.where(keep, pltpu.roll(an, n - s, 0), 1.0)
        l_sh = jnp.where(keep, pltpu.roll(lam, n - s, 0), 0.0)
        lam = an * l_sh + lam
        an = an * a_sh
        s *= 2
    h_prev = jnp.where(row >= 1, pltpu.roll(h, 1, 0), h0)
    al = a * lam
    dh0 = jnp.sum(jnp.where(row == 0, al, 0.0), axis=0, keepdims=True)
    return lam * h_prev, lam, dh0


_scan.defvjp(_scan_f, _scan_b)


def _last_row(x):
    sel = lax.broadcasted_iota(jnp.int32, x.shape, 0) == x.shape[0] - 1
    return jnp.sum(jnp.where(sel, x, 0.0), axis=0, keepdims=True)


def fn_norm(xs, st, ps):
    (x,), (w,) = xs, ps
    return [_rms(x, w).astype(BF16)], []


def fn_norm_keep(xs, st, ps):
    (x,), (w,) = xs, ps
    return [_rms(x, w).astype(BF16), x], []


def fn_res(xs, st, ps):
    (x, y), (w_post, w_next) = xs, ps
    x1 = x + _rms(y, w_post)
    return [x1, _rms(x1, w_next).astype(BF16)], []


def fn_res_last(xs, st, ps):
    (x, y), (w_post,) = xs, ps
    return [x + _rms(y, w_post)], []


def fn_gdn(xs, st, ps):
    qkv_ext, z, ba = xs
    (state,) = st
    cw, gp, nw = ps
    ts = z.shape[0]
    qkv = _silu(_conv(qkv_ext, cw, 4))
    beta_all = _sigmoid(ba)
    g_all = -jnp.exp(_row(gp, 0)) * _softplus(ba + _row(gp, 1))
    c_n = GDN_CHUNK
    ri = lax.broadcasted_iota(jnp.int32, (c_n, c_n), 0)
    ci = lax.broadcasted_iota(jnp.int32, (c_n, c_n), 1)
    causal, strict = ri >= ci, ri > ci
    tril = causal.astype(F32)
    eye = (ri == ci).astype(F32)
    lane = lax.broadcasted_iota(jnp.int32, (c_n, 128), 1)
    s_h = [state[GDN_DIM * h:GDN_DIM * (h + 1)] for h in range(GDN_HEADS)]
    out_rows = []
    for c in range(ts // c_n):
        r0 = c * c_n
        gcum_all = _dot_exact(tril, g_all[r0:r0 + c_n], 1, 0)
        beta_c = beta_all[r0:r0 + c_n]
        heads = []
        for h in range(GDN_HEADS):
            q = qkv[r0:r0 + c_n, GDN_DIM * h:GDN_DIM * (h + 1)]
            k = qkv[r0:r0 + c_n, D_G + GDN_DIM * h:D_G + GDN_DIM * (h + 1)]
            v = qkv[r0:r0 + c_n, 2 * D_G + GDN_DIM * h:2 * D_G + GDN_DIM * (h + 1)]
            q = q * lax.rsqrt(jnp.sum(q * q, axis=-1, keepdims=True) + EPS) * (GDN_DIM ** -0.5)
            k = k * lax.rsqrt(jnp.sum(k * k, axis=-1, keepdims=True) + EPS)
            b = _col(beta_c, h)
            gc = _col(gcum_all, 4 + h)
            pick = (lane == 4 + h).astype(F32)
            gr = _dot_exact(pick, gcum_all, 1, 1)
            decay = jnp.where(causal, jnp.exp(jnp.where(causal, gc - gr, 0.0)), 0.0)
            kb = k * b
            m = jnp.where(strict, _mm_nt(kb, k) * decay, 0.0)
            n_ = -m
            t_ = eye + n_
            p_ = n_
            for _ in range(5):
                p_ = _mm(p_, p_)
                t_ = t_ + _mm(t_, p_)
            eg = jnp.exp(gc)
            w = _mm(t_, kb * eg)
            u = _mm(t_, v * b)
            attn = jnp.where(causal, _mm_nt(q, k) * decay, 0.0)
            g_last = _last_row(gc)
            k_g = k * jnp.exp(g_last - gc)
            v_new = u - _mm(w, s_h[h])
            o = _mm(q * eg, s_h[h]) + _mm(attn, v_new)
            s_h[h] = s_h[h] * jnp.exp(g_last) + _mm_tn(k_g, v_new)
            zz = z[r0:r0 + c_n, GDN_DIM * h:GDN_DIM * (h + 1)]
            heads.append(o * lax.rsqrt(jnp.mean(o * o, axis=-1, keepdims=True) + EPS) * nw * _silu(zz))
        out_rows.append(jnp.concatenate(heads, axis=1))
    y = out_rows[0] if len(out_rows) == 1 else jnp.concatenate(out_rows, axis=0)
    return [y.astype(BF16)], [jnp.concatenate(s_h, axis=0)]


def fn_lru(xs, st, ps):
    x_ext, gate = xs
    (h0,) = st
    cw, cb, wa, ba, wx, bx, lam, gw = ps
    xc = _conv(x_ext, cw, 4) + cb
    r = _sigmoid(_mm(xc, wa) + ba)
    i = _sigmoid(_mm(xc, wx) + bx)
    log_a = -LRU_C * r * _softplus(-lam)
    a = jnp.exp(log_a)
    mult = jnp.sqrt(_neg_expm1(2.0 * log_a))
    h = _scan(a, mult * (i * xc), h0)
    y = _rms(h * _gelu(gate), gw)
    return [y.astype(BF16)], [_last_row(h)]


def fn_sgu(xs, st, ps):
    (uv,) = xs
    lnw, lnb, ws, bst, gw = ps
    ts = uv.shape[0]
    uvf = _gelu(uv)
    u, v = uvf[:, :D_G], uvf[:, D_G:]
    vc = v - jnp.mean(v, axis=-1, keepdims=True)
    v = vc * lax.rsqrt(jnp.mean(vc * vc, axis=-1, keepdims=True) + EPS) * lnw + lnb
    t_n = SGU_CHUNK
    tril = lax.broadcasted_iota(jnp.int32, (t_n, t_n), 0) >= lax.broadcasted_iota(jnp.int32, (t_n, t_n), 1)
    wg = [jnp.where(tril, ws[t_n * g:t_n * (g + 1)], 0.0) for g in range(SGU_GROUPS)]
    bg = [_col(bst, g) for g in range(SGU_GROUPS)]
    rows = []
    for c in range(ts // t_n):
        vcg = v[c * t_n:(c + 1) * t_n]
        rows.append(jnp.concatenate(
            [_mm(wg[g], vcg[:, 128 * g:128 * (g + 1)]) + bg[g] for g in range(SGU_GROUPS)], axis=1))
    vv = rows[0] if len(rows) == 1 else jnp.concatenate(rows, axis=0)
    return [_rms(u * vv, gw).astype(BF16)], []


def fn_sconv(xs, st, ps):
    bg, cg_ext, hh_ext = xs
    cw, gw = ps
    return [_rms(bg * _conv(cg_ext * hh_ext, cw, 3), gw).astype(BF16)], []


def fn_ffn(xs, st, ps):
    g_ext, v_ext = xs
    cwg, cwv, cbg, cbv = ps
    g = _conv(g_ext, cwg, 3) + cbg
    v = _conv(v_ext, cwv, 3) + cbv
    return [(_gelu(g) * v).astype(BF16)], []


def _my_pos():
    return lax.axis_index("x"), lax.axis_index("y"), lax.axis_index("c")


def _peer(pos, k):
    x_, y_, c_ = pos
    return (1 - x_ if (k >> 2) & 1 else x_, 1 - y_ if (k >> 1) & 1 else y_, 1 - c_ if k & 1 else c_)


def _dev_index(p):
    return 4 * p[0] + 2 * p[1] + p[2]


class _Side:
    def __init__(self, jobs):
        self.jobs = list(jobs)
        n = len(self.jobs)
        self.operands = [a for _, a in self.jobs]
        self.in_specs = [pl.BlockSpec(memory_space=pl.ANY)] * n
        self.out_shape = [jax.ShapeDtypeStruct(((N_DEV,) + a.shape) if kind == "gather" else a.shape, a.dtype)
                          for kind, a in self.jobs]
        self.out_specs = [pl.BlockSpec(memory_space=pl.ANY)] * n
        self.scratch = [pltpu.SemaphoreType.DMA((7 * n,)), pltpu.SemaphoreType.DMA((7 * n,)),
                        pltpu.SemaphoreType.DMA((n,))] if n else []

    def _copies(self, in_refs, out_refs, sems):
        send, recv, local = sems
        pos = _my_pos()
        me = _dev_index(pos)
        mine, outgoing, landing = [], [], []
        for j, (kind, _) in enumerate(self.jobs):
            src, dst = in_refs[j], out_refs[j]
            own = src if kind == "gather" else src.at[me]
            mine.append(pltpu.make_async_copy(own, dst.at[me], local.at[j]))
            for k in range(1, N_DEV):
                p = _peer(pos, k)
                sems_k = dict(send_sem=send.at[7 * j + k - 1], recv_sem=recv.at[7 * j + k - 1], device_id=p,
                              device_id_type=MESH)
                outgoing.append(pltpu.make_async_remote_copy(
                    src_ref=src if kind == "gather" else src.at[_dev_index(p)], dst_ref=dst.at[me], **sems_k))
                landing.append(pltpu.make_async_remote_copy(src_ref=own, dst_ref=dst.at[_dev_index(p)], **sems_k))
        return mine, outgoing, landing

    def start(self, in_refs, out_refs, sems):
        mine, outgoing, _ = self._copies(in_refs, out_refs, sems)
        for cp in mine + outgoing:
            cp.start()

    def wait(self, in_refs, out_refs, sems):
        mine, outgoing, landing = self._copies(in_refs, out_refs, sems)
        for cp in landing:
            cp.wait_recv()
        for cp in outgoing:
            cp.wait_send()
        for cp in mine:
            cp.wait()


def _rin(arr, w=None, col=0, halo=False):
    return dict(arr=arr, w=arr.shape[1] if w is None else w, col=col, halo=halo)


def _par(arr, w=None, col=None):
    return dict(arr=arr, w=w, col=col)


def _colidx(col, j):
    return col(j) if callable(col) else col


def _block_call(name, body, n_rows, ts, ncol, reverse, row_ins, blk_ins, params, row_outs, blk_outs, acc_outs,
                carries, side=()):
    side = _Side(side)
    nblk = n_rows // ts
    hb = ts // HALO

    def rr(i):
        return (nblk - 1 - i) if reverse else i

    in_specs, operands = [], []
    for s in row_ins:
        in_specs.append(pl.BlockSpec((ts, s["w"]), lambda j, i, s=s: (rr(i), _colidx(s["col"], j))))
        operands.append(s["arr"])
        if s["halo"]:
            in_specs.append(pl.BlockSpec((HALO, s["w"]),
                                         lambda j, i, s=s: (jnp.maximum(rr(i) * hb - 1, 0), _colidx(s["col"], j))))
            operands.append(s["arr"])
    for a in blk_ins:
        nd = a.ndim - 1
        in_specs.append(pl.BlockSpec((None,) + a.shape[1:], lambda j, i, nd=nd: (rr(i),) + (0,) * nd))
        operands.append(a)
    for p in params:
        a = p["arr"]
        if p["col"] is None:
            in_specs.append(pl.BlockSpec(a.shape, lambda j, i: (0, 0)))
        else:
            in_specs.append(pl.BlockSpec((a.shape[0], p["w"]), lambda j, i, p=p: (0, _colidx(p["col"], j))))
        operands.append(a)

    out_specs, out_shape = [], []
    for o in row_outs:
        out_specs.append(pl.BlockSpec((ts, o["w"]), lambda j, i, o=o: (rr(i), _colidx(o["col"], j))))
        out_shape.append(jax.ShapeDtypeStruct((n_rows, o["total"]), o["dtype"]))
    for o in blk_outs:
        nd = len(o["shape"])
        out_specs.append(pl.BlockSpec((None,) + tuple(o["shape"]), lambda j, i, nd=nd: (rr(i),) + (0,) * nd))
        out_shape.append(jax.ShapeDtypeStruct((nblk,) + tuple(o["shape"]), o["dtype"]))
    for o in acc_outs:
        if o["col"] is None:
            out_specs.append(pl.BlockSpec(o["shape"], lambda j, i: (0, 0)))
            out_shape.append(jax.ShapeDtypeStruct(o["shape"], F32))
        else:
            out_specs.append(pl.BlockSpec(o["shape"], lambda j, i, o=o: (0, _colidx(o["col"], j))))
            out_shape.append(jax.ShapeDtypeStruct((o["shape"][0], o["total"]), F32))

    n_in = len(operands)
    n_row_out, n_blk_out, n_acc = len(row_outs), len(blk_outs), len(acc_outs)
    n_out = n_row_out + n_blk_out + n_acc
    n_side = len(side.jobs)

    def kern(*refs):
        in_refs = refs[:n_in]
        side_in = refs[n_in:n_in + n_side]
        out_refs = refs[n_in + n_side:n_in + n_side + n_out]
        side_out = refs[n_in + n_side + n_out:n_in + 2 * n_side + n_out]
        scratch = refs[n_in + 2 * n_side + n_out:]
        carry_refs, side_sems = scratch[:len(carries)], scratch[len(carries):]
        acc_refs = out_refs[n_row_out + n_blk_out:]
        i = pl.program_id(1)
        r = rr(i)
        if n_side:
            @pl.when((pl.program_id(0) == 0) & (i == 0))
            def _():
                side.start(side_in, side_out, side_sems)

        @pl.when(i == 0)
        def _():
            for c_ref in carry_refs:
                c_ref[...] = jnp.zeros(c_ref.shape, c_ref.dtype)
            for a_ref in acc_refs:
                a_ref[...] = jnp.zeros(a_ref.shape, a_ref.dtype)

        k = 0
        xs = []
        for s in row_ins:
            x = in_refs[k][...]
            k += 1
            if s["halo"]:
                hal = in_refs[k][...]
                k += 1
                hal = jnp.where(r == 0, jnp.zeros_like(hal), hal)
                x = jnp.concatenate([hal, x], axis=0)
            xs.append(x)
        blks = []
        for _ in blk_ins:
            blks.append(in_refs[k][...])
            k += 1
        ps = []
        for _ in params:
            ps.append(in_refs[k][...])
            k += 1
        row_vals, blk_vals, acc_vals, new_carries = body(xs, blks, ps, [c[...] for c in carry_refs], r)
        for ref, val in zip(out_refs[:n_row_out], row_vals):
            ref[...] = val.astype(ref.dtype)
        for ref, val in zip(out_refs[n_row_out:n_row_out + n_blk_out], blk_vals):
            ref[...] = val.astype(ref.dtype)
        for ref, val in zip(acc_refs, acc_vals):
            ref[...] += val
        for ref, val in zip(carry_refs, new_carries):
            ref[...] = val
        if n_side:
            @pl.when((pl.program_id(0) == ncol - 1) & (i == nblk - 1))
            def _():
                side.wait(side_in, side_out, side_sems)

    res = pl.pallas_call(
        kern,
        name=name,
        grid=(ncol, nblk),
        in_specs=in_specs + side.in_specs,
        out_specs=out_specs + side.out_specs,
        out_shape=out_shape + side.out_shape,
        scratch_shapes=[pltpu.VMEM(shape, F32) for shape in carries] + side.scratch,
        compiler_params=pltpu.CompilerParams(dimension_semantics=("arbitrary", "arbitrary"),
                                             vmem_limit_bytes=VMEM_LIMIT),
    )(*operands, *side.operands)
    return list(res)


def _out(w, dtype, total=None, col=0):
    return dict(w=w, dtype=dtype, total=w if total is None else total, col=col)


def seq_fwd(name, fn, n_rows, ts, row_ins, params, outs, state_shapes=(), ncol=1, side=()):
    def body(xs, blks, ps, carries, r):
        o, new_st = fn(xs, list(carries), ps)
        return o, list(carries), [], new_st

    res = _block_call(name, body, n_rows, ts, ncol, False, row_ins, [], params, outs,
                      [dict(shape=s, dtype=F32) for s in state_shapes], [], list(state_shapes), side)
    n_o, n_s = len(outs), len(state_shapes)
    return (res[:n_o], res[n_o:n_o + n_s]) + ((res[n_o + n_s:],) if side else ())


def seq_bwd(name, fn, n_rows, ts, row_ins, params, cots, saved_states=(), din_dtypes=None, ncol=1, din_specs=None,
            side=()):
    n_x, n_p, n_st = len(row_ins), len(params), len(saved_states)
    halo_idx = [k for k, s in enumerate(row_ins) if s["halo"]]
    state_shapes = [a.shape[1:] for a in saved_states]

    def body(xs_all, blks, ps, carries, r):
        xs, cot_vals = xs_all[:n_x], xs_all[n_x:]
        d_state, d_halo = carries[:n_st], carries[n_st:]
        (o, _), vjp = jax.vjp(lambda a, b, c: fn(a, b, c), xs, blks, ps)
        cot = [c.astype(v.dtype) for c, v in zip(cot_vals, o)]
        dxs, dst, dps = vjp((cot, list(d_state)))
        row_vals, new_halo = [], []
        for k, dx in enumerate(dxs):
            if k in halo_idx:
                hk = halo_idx.index(k)
                body_rows = dx[HALO:]
                tail = dx[ts:ts + HALO] + d_halo[hk]
                row_vals.append(jnp.concatenate([body_rows[:ts - HALO], tail], axis=0))
                new_halo.append(dx[:HALO])
            else:
                row_vals.append(dx)
        return row_vals, [], list(dps), list(dst) + new_halo

    din_dtypes = din_dtypes or [F32] * n_x
    douts = []
    for k, s in enumerate(row_ins):
        total, col = (s["w"], 0) if din_specs is None or din_specs[k] is None else din_specs[k]
        douts.append(_out(s["w"], din_dtypes[k], total, col))
    accs = []
    for p in params:
        a = p["arr"]
        if p["col"] is None:
            accs.append(dict(shape=a.shape, total=None, col=None))
        else:
            accs.append(dict(shape=(a.shape[0], p["w"]), total=a.shape[1], col=p["col"]))
    carries = list(state_shapes) + [(HALO, row_ins[k]["w"]) for k in halo_idx]
    res = _block_call(name, body, n_rows, ts, ncol, True, list(row_ins) + list(cots), list(saved_states), params,
                      douts, [], accs, carries, side)
    return (res[:n_x], res[n_x:n_x + n_p]) + ((res[n_x + n_p:],) if side else ())


def matmul(name, a, b, mode, out_dtype, tm, tn, tk, side=()):
    side = _Side(side)
    n_side = len(side.jobs)
    if mode == "tn":
        (kk, m), n = a.shape, b.shape[1]
    else:
        (m, kk), n = a.shape, (b.shape[0] if mode == "nt" else b.shape[1])
    tm, tn, tk = min(tm, m), min(tn, n), min(tk, kk)
    nk = kk // tk
    assert m % tm == 0 and n % tn == 0 and kk % tk == 0, (name, a.shape, b.shape, tm, tn, tk)
    a_spec = pl.BlockSpec((tk, tm), lambda i, j, k: (k, i)) if mode == "tn" else pl.BlockSpec((tm, tk), lambda i, j, k: (i, k))
    b_spec = pl.BlockSpec((tn, tk), lambda i, j, k: (j, k)) if mode == "nt" else pl.BlockSpec((tk, tn), lambda i, j, k: (k, j))
    ca, cb = {"nn": (1, 0), "nt": (1, 1), "tn": (0, 0)}[mode]

    gm, gn = m // tm, n // tn

    def kern(*refs):
        a_ref, b_ref = refs[:2]
        side_in = refs[2:2 + n_side]
        o_ref = refs[2 + n_side]
        side_out = refs[3 + n_side:3 + 2 * n_side]
        acc_ref = refs[3 + 2 * n_side]
        side_sems = refs[4 + 2 * n_side:]
        i, j, k = pl.program_id(0), pl.program_id(1), pl.program_id(2)
        if n_side:
            @pl.when((i == 0) & (j == 0) & (k == 0))
            def _():
                side.start(side_in, side_out, side_sems)

        part = lax.dot_general(a_ref[...], b_ref[...], (((ca,), (cb,)), ((), ())), preferred_element_type=F32)
        if nk == 1:
            o_ref[...] = part.astype(o_ref.dtype)
        else:
            @pl.when(k == 0)
            def _():
                acc_ref[...] = part

            @pl.when(k > 0)
            def _():
                acc_ref[...] += part

            @pl.when(k == nk - 1)
            def _():
                o_ref[...] = acc_ref[...].astype(o_ref.dtype)

        if n_side:
            @pl.when((i == gm - 1) & (j == gn - 1) & (k == nk - 1))
            def _():
                side.wait(side_in, side_out, side_sems)

    semantics = ("arbitrary",) * 3 if n_side else ("parallel", "parallel", "arbitrary")
    res = pl.pallas_call(
        kern,
        name=name,
        grid=(gm, gn, nk),
        in_specs=[a_spec, b_spec] + side.in_specs,
        out_specs=[pl.BlockSpec((tm, tn), lambda i, j, k: (i, j))] + side.out_specs,
        out_shape=[jax.ShapeDtypeStruct((m, n), out_dtype)] + side.out_shape,
        scratch_shapes=[pltpu.VMEM((tm, tn) if nk > 1 else (8, 128), F32)] + side.scratch,
        compiler_params=pltpu.CompilerParams(dimension_semantics=semantics, vmem_limit_bytes=VMEM_LIMIT),
    )(a, b, *side.operands)
    return (res[0], list(res[1:])) if n_side else res[0]


def all_gather(name, x, in_vmem):
    def body(x_ref, out_ref, send_sems, recv_sems, local_sem):
        x_, y_, c_ = _my_pos()
        me, sibling = (x_, y_, c_), (x_, y_, 1 - c_)
        chips = [(1 - x_, y_), (x_, 1 - y_), (1 - x_, 1 - y_)]

        def slot(px, py, pc):
            return out_ref.at[4 * px + 2 * py + pc]

        def copy(k, block, to, src=None):
            return pltpu.make_async_remote_copy(
                src_ref=slot(*block) if src is None else src, dst_ref=slot(*block),
                send_sem=send_sems.at[k], recv_sem=recv_sems.at[k], device_id=to, device_id_type=MESH)

        mine = pltpu.make_async_copy(x_ref, slot(*me), local_sem)
        mine.start()
        first = [copy(0, me, sibling, src=x_ref)]
        first += [copy(1 + j, me, (*chip, c_), src=x_ref) for j, chip in enumerate(chips)]
        for cp in first:
            cp.start()
        passed = [copy(4 + j, (*chip, c_), sibling) for j, chip in enumerate(chips)]
        for j, chip in enumerate(chips):
            copy(1 + j, (*chip, c_), me).wait_recv()
            passed[j].start()
        copy(0, sibling, me).wait_recv()
        for j, chip in enumerate(chips):
            copy(4 + j, (*chip, 1 - c_), me).wait_recv()
        for cp in first + passed:
            cp.wait_send()
        mine.wait()

    space = pltpu.VMEM if in_vmem else pl.ANY
    return pl.pallas_call(
        body,
        name=name,
        out_shape=jax.ShapeDtypeStruct((N_DEV,) + x.shape, x.dtype),
        in_specs=[pl.BlockSpec(memory_space=space)],
        out_specs=pl.BlockSpec(memory_space=space),
        scratch_shapes=[pltpu.SemaphoreType.DMA((7,)), pltpu.SemaphoreType.DMA((7,)), pltpu.SemaphoreType.DMA],
        compiler_params=pltpu.CompilerParams(vmem_limit_bytes=VMEM_LIMIT),
    )(x)


def all_to_all(name, g):
    def body(g_ref, out_ref, send_sems, recv_sems, local_sem):
        x_, y_, c_ = _my_pos()
        me = 4 * x_ + 2 * y_ + c_

        def peer(k):
            fx, fy, fc = (k >> 2) & 1, (k >> 1) & 1, k & 1
            return (1 - x_ if fx else x_, 1 - y_ if fy else y_, 1 - c_ if fc else c_)

        def copy(k):
            px, py, pc = peer(k)
            return pltpu.make_async_remote_copy(
                src_ref=g_ref.at[4 * px + 2 * py + pc], dst_ref=out_ref.at[me],
                send_sem=send_sems.at[k - 1], recv_sem=recv_sems.at[k - 1], device_id=(px, py, pc), device_id_type=MESH)

        def landing(k):
            px, py, pc = peer(k)
            return pltpu.make_async_remote_copy(
                src_ref=g_ref.at[me], dst_ref=out_ref.at[4 * px + 2 * py + pc],
                send_sem=send_sems.at[k - 1], recv_sem=recv_sems.at[k - 1], device_id=(px, py, pc), device_id_type=MESH)

        mine = pltpu.make_async_copy(g_ref.at[me], out_ref.at[me], local_sem)
        mine.start()
        sends = [copy(k) for k in range(1, N_DEV)]
        for cp in sends:
            cp.start()
        for k in range(1, N_DEV):
            landing(k).wait_recv()
        for cp in sends:
            cp.wait_send()
        mine.wait()

    return pl.pallas_call(
        body,
        name=name,
        out_shape=jax.ShapeDtypeStruct(g.shape, g.dtype),
        in_specs=[pl.BlockSpec(memory_space=pl.ANY)],
        out_specs=pl.BlockSpec(memory_space=pl.ANY),
        scratch_shapes=[pltpu.SemaphoreType.DMA((7,)), pltpu.SemaphoreType.DMA((7,)), pltpu.SemaphoreType.DMA],
    )(g)


def sum_blocks(name, g):
    def body(g_ref, o_ref):
        acc = g_ref[0]
        for s in range(1, N_DEV):
            acc = acc + g_ref[s]
        o_ref[...] = acc

    r = g.shape[1]
    tr = r // 4 if r % 32 == 0 else r
    return pl.pallas_call(
        body, name=name, grid=(r // tr,),
        in_specs=[pl.BlockSpec((N_DEV, tr, 128), lambda i: (0, i, 0))],
        out_specs=pl.BlockSpec((tr, 128), lambda i: (i, 0)),
        out_shape=jax.ShapeDtypeStruct((r, 128), F32),
        compiler_params=pltpu.CompilerParams(vmem_limit_bytes=VMEM_LIMIT),
    )(g)


def _adamw_math(w, g, m, v):
    m = ADAM_B1 * m + (1.0 - ADAM_B1) * g
    v = ADAM_B2 * v + (1.0 - ADAM_B2) * (g * g)
    m_hat = m / (1.0 - ADAM_B1 ** ADAM_STEP)
    v_hat = v / (1.0 - ADAM_B2 ** ADAM_STEP)
    delta = -ADAM_LR * (m_hat / (jnp.sqrt(v_hat) + ADAM_EPS) + ADAM_WD * w)
    return delta, m, v


def adamw_big(name, w, land, m, v, tr):
    r, c = w.shape

    def body(w_ref, l_ref, m_ref, v_ref, g_out, d_out, m_out, v_out):
        g = l_ref[0].astype(F32)
        for s in range(1, N_DEV):
            g = g + l_ref[s].astype(F32)
        delta, m_new, v_new = _adamw_math(w_ref[...], g, m_ref[...], v_ref[...])
        g_out[...] = g
        d_out[...] = delta
        m_out[...] = m_new
        v_out[...] = v_new

    spec = pl.BlockSpec((tr, c), lambda i: (i, 0))
    return pl.pallas_call(
        body, name=name, grid=(r // tr,),
        in_specs=[spec, pl.BlockSpec((N_DEV, tr, c), lambda i: (0, i, 0)), spec, spec],
        out_specs=[spec] * 4,
        out_shape=[jax.ShapeDtypeStruct((r, c), F32)] * 4,
        compiler_params=pltpu.CompilerParams(dimension_semantics=("parallel",), vmem_limit_bytes=VMEM_LIMIT),
    )(w, land, m, v)


def adamw_small(name, w, g, m, v):
    def body(w_ref, g_ref, m_ref, v_ref, d_out, m_out, v_out):
        delta, m_new, v_new = _adamw_math(w_ref[...], g_ref[...], m_ref[...], v_ref[...])
        d_out[...] = delta
        m_out[...] = m_new
        v_out[...] = v_new

    return pl.pallas_call(
        body, name=name,
        out_shape=[jax.ShapeDtypeStruct(w.shape, F32)] * 3,
        compiler_params=pltpu.CompilerParams(vmem_limit_bytes=VMEM_LIMIT),
    )(w, g, m, v)


def cast_bf16(name, w, tr):
    r, c = w.shape

    def body(w_ref, o_ref):
        o_ref[...] = w_ref[...].astype(BF16)

    spec = pl.BlockSpec((tr, c), lambda i: (i, 0))
    return pl.pallas_call(body, name=name, grid=(r // tr,), in_specs=[spec], out_specs=spec,
                          out_shape=jax.ShapeDtypeStruct((r, c), BF16),
                          compiler_params=pltpu.CompilerParams(dimension_semantics=("parallel",)))(w)


def _pack(arrs):
    flat = jnp.concatenate([a.reshape(-1).astype(F32) for a in arrs])
    n = flat.shape[0]
    pad = (-n) % 1024
    return jnp.pad(flat, (0, pad)).reshape(-1, 128)


def _unpack(packed, shapes):
    flat = packed.reshape(-1)
    out, off = [], 0
    for s in shapes:
        n = math.prod(s)
        out.append(flat[off:off + n].reshape(s))
        off += n
    return out


def _block_diag(w):
    h, d, _ = w.shape
    eye = jnp.eye(h, dtype=w.dtype)
    return (eye[:, None, :, None] * w[:, :, None, :]).reshape(h * d, h * d)


def _block_diag_grad(g, h):
    d = g.shape[0] // h
    g4 = g.reshape(h, d, h, d)
    return jnp.stack([g4[k, :, k, :] for k in range(h)])


def _layer_params(wt, l):
    gp = jnp.pad(jnp.stack([wt["gdn_a_log"][l], wt["gdn_dt_bias"][l]]), ((0, 6), (4, 128 - 4 - GDN_HEADS)))
    d_ffh = wt["ffn_conv_w"].shape[-1] // 2
    return dict(
        pre_mix=wt["pre_mix_norm"][l][None], post_mix=wt["post_mix_norm"][l][None],
        pre_ffn=wt["pre_ffn_norm"][l][None], post_ffn=wt["post_ffn_norm"][l][None],
        gdn_cw=wt["gdn_conv_w"][l], gdn_gp=gp, gdn_nw=wt["gdn_norm_w"][l][None],
        lru_cw=wt["lru_conv_w"][l], lru_cb=wt["lru_conv_b"][l][None],
        lru_wa=_block_diag(wt["lru_wa"][l]), lru_ba=wt["lru_ba"][l].reshape(1, -1),
        lru_wx=_block_diag(wt["lru_wx"][l]), lru_bx=wt["lru_bx"][l].reshape(1, -1),
        lru_lam=wt["lru_lambda"][l][None], gw0=wt["grp_norm_w"][l, 0][None], gw1=wt["grp_norm_w"][l, 1][None],
        gw2=wt["grp_norm_w"][l, 2][None],
        sgu_lnw=wt["sgu_ln_w"][l][None], sgu_lnb=wt["sgu_ln_b"][l][None],
        sgu_ws=wt["sgu_ws"][l].reshape(SGU_GROUPS * SGU_CHUNK, SGU_CHUNK),
        sgu_bt=jnp.pad(wt["sgu_b"][l].T, ((0, 0), (0, 128 - SGU_GROUPS))),
        sc_cw=wt["sconv_w"][l],
        ffn_cw=wt["ffn_conv_w"][l], ffn_cb=wt["ffn_conv_b"][l][None], d_ffh=d_ffh,
    )


TS_ROW = 256
TS_GDN = 128
TS_FFN = 256
TC_FFN = 512


def _mixers_fwd(l, p, lp, n, side):
    qkv = _rin(p, 3 * D_G, 0, halo=True)
    z = _rin(p, D_G, 3)
    ba = _rin(p, 128, BA_COL // 128)
    gdn_ps = [_par(lp["gdn_cw"]), _par(lp["gdn_gp"]), _par(lp["gdn_nw"])]
    res = seq_fwd(f"gdn_fwd_{l}", fn_gdn, n, TS_GDN, [qkv, z, ba], gdn_ps, [_out(D_G, BF16)],
                  state_shapes=[(GDN_HEADS * GDN_DIM, GDN_DIM)], side=side)
    (y_a,), (gdn_st,), side_res = res if side else res + ([],)
    lru_x = _rin(p, D_G, 4, halo=True)
    lru_gate = _rin(p, D_G, 5)
    lru_ps = [_par(lp[k]) for k in ("lru_cw", "lru_cb", "lru_wa", "lru_ba", "lru_wx", "lru_bx", "lru_lam", "gw0")]
    (y_b,), (lru_st,) = seq_fwd(f"lru_fwd_{l}", fn_lru, n, TS_ROW, [lru_x, lru_gate], lru_ps, [_out(D_G, BF16)],
                                state_shapes=[(1, D_G)])
    uv = _rin(p, 2 * D_G, 3)
    sgu_ps = [_par(lp[k]) for k in ("sgu_lnw", "sgu_lnb", "sgu_ws", "sgu_bt", "gw1")]
    (y_c,), _ = seq_fwd(f"sgu_fwd_{l}", fn_sgu, n, TS_ROW, [uv], sgu_ps, [_out(D_G, BF16)])
    sc = [_rin(p, D_G, 8), _rin(p, D_G, 9, halo=True), _rin(p, D_G, 10, halo=True)]
    sc_ps = [_par(lp["sc_cw"]), _par(lp["gw2"])]
    (y_d,), _ = seq_fwd(f"sconv_fwd_{l}", fn_sconv, n, TS_ROW, sc, sc_ps, [_out(D_G, BF16)])
    ins = dict(gdn=([qkv, z, ba], gdn_ps, [gdn_st]), lru=([lru_x, lru_gate], lru_ps, [lru_st]),
               sgu=([uv], sgu_ps, []), sc=(sc, sc_ps, []))
    return jnp.concatenate([y_a, y_b, y_c, y_d], axis=1), ins, side_res


def _mixers_bwd(l, dymix, ins, n, side):
    cot = lambda g: [_rin(dymix, D_G, g)]
    xs, ps, st = ins["gdn"]
    res = seq_bwd(f"gdn_bwd_{l}", fn_gdn, n, TS_GDN, xs, ps, cot(0), st, [BF16, BF16, BF16], side=side)
    (dqkv, dz, dba), g_gdn, side_res = res if side else res + ([],)
    xs, ps, st = ins["lru"]
    (dlx, dlg), g_lru = seq_bwd(f"lru_bwd_{l}", fn_lru, n, TS_ROW, xs, ps, cot(1), st, [BF16, BF16])
    xs, ps, st = ins["sgu"]
    (duv,), g_sgu = seq_bwd(f"sgu_bwd_{l}", fn_sgu, n, TS_ROW, xs, ps, cot(2), st, [BF16])
    xs, ps, st = ins["sc"]
    (dsb, dsc, dsh), g_sc = seq_bwd(f"sconv_bwd_{l}", fn_sconv, n, TS_ROW, xs, ps, cot(3), st, [BF16, BF16, BF16])
    dp = jnp.concatenate([dqkv, dz, dlx, dlg, duv, dsb, dsc, dsh, dba], axis=1)
    return dp, dict(gdn=g_gdn, lru=g_lru, sgu=g_sgu, sc=g_sc), side_res


def _ffn_ops(hid, lp):
    d_ffh = lp["d_ffh"]
    off = d_ffh // TC_FFN
    xs = [_rin(hid, TC_FFN, lambda j: j, halo=True), _rin(hid, TC_FFN, lambda j: j + off, halo=True)]
    ps = [_par(lp["ffn_cw"], TC_FFN, lambda j: j), _par(lp["ffn_cw"], TC_FFN, lambda j: j + off),
          _par(lp["ffn_cb"], TC_FFN, lambda j: j), _par(lp["ffn_cb"], TC_FFN, lambda j: j + off)]
    return xs, ps, d_ffh


_FROM_BLOCKS = dict(
    w_in=lambda b: _regroup_w_in(b.transpose(1, 0, 2).reshape(b.shape[1], -1)),
    ffn_up=lambda b: b.transpose(1, 0, 2).reshape(b.shape[1], -1),
    w_out=lambda b: b.reshape(-1, b.shape[2]),
    ffn_down=lambda b: b.reshape(-1, b.shape[2]),
)
_TO_BLOCKS = dict(
    w_in=lambda g: _ungroup_w_in(g).reshape(g.shape[0], N_DEV, -1).transpose(1, 0, 2),
    ffn_up=lambda g: g.reshape(g.shape[0], N_DEV, -1).transpose(1, 0, 2),
    w_out=lambda g: g.reshape(N_DEV, -1, g.shape[1]),
    ffn_down=lambda g: g.reshape(N_DEV, -1, g.shape[1]),
)


class _Traffic:
    def __init__(self, whole=None, shards=None):
        self.whole = dict(whole or {})
        self.shards = shards
        self.pending = {}
        self.landed = {}

    def jobs(self, gather=(), exchange=()):
        if self.shards is None:
            return [], []
        keys = [("gather", k) for k in gather if k not in self.whole] + [("exchange", k) for k in exchange if k in self.pending]
        return [(kind, self.shards[k[0]][k[1]] if kind == "gather" else self.pending[k]) for kind, k in keys], keys

    def done(self, keys, results):
        for (kind, k), r in zip(keys, results):
            if kind == "gather":
                self.whole[k] = _FROM_BLOCKS[k[0]](r)
            else:
                self.landed[k] = r
                del self.pending[k]

    def weight(self, name, l):
        if (name, l) not in self.whole:
            blocks = all_gather(f"gather_{name}_{l}", self.shards[name][l], False)
            self.whole[(name, l)] = _FROM_BLOCKS[name](blocks)
        return self.whole[(name, l)]

    def grad(self, name, l, g):
        if self.shards is None:
            self.landed[(name, l)] = g
        else:
            self.pending[(name, l)] = _TO_BLOCKS[name](g)

    def flush(self):
        for (name, l), blocks in list(self.pending.items()):
            self.landed[(name, l)] = all_to_all(f"exchange_{name}_{l}", blocks)
            del self.pending[(name, l)]


def local_step(x, target, wt, tr):
    n, d = x.shape
    depth = wt["pre_mix_norm"].shape[0]
    lps = [_layer_params(wt, l) for l in range(depth)]
    saved = []
    xin = x

    def mm(name, a, b, mode, dtype, tm, tn, tk, gather=(), exchange=()):
        jobs, keys = tr.jobs(gather, exchange)
        if not jobs:
            return matmul(name, a, b, mode, dtype, tm, tn, tk)
        out, res = matmul(name, a, b, mode, dtype, tm, tn, tk, side=jobs)
        tr.done(keys, res)
        return out

    (h,), _ = seq_fwd("norm_fwd", fn_norm, n, TS_ROW, [_rin(x)], [_par(lps[0]["pre_mix"])], [_out(d, BF16)])
    dx_last = loss = None
    for l in range(depth):
        lp = lps[l]
        p = mm(f"w_in_fwd_{l}", h, tr.weight("w_in", l), "nn", F32, 1024, 640, d,
               gather=[("w_out", l), ("ffn_down", l)])
        jobs, keys = tr.jobs(gather=[("ffn_up", l)])
        ymix, mix_ins, res = _mixers_fwd(l, p, lp, n, jobs)
        tr.done(keys, res)
        y = mm(f"w_out_fwd_{l}", ymix, tr.weight("w_out", l), "nn", F32, 1024, 1024, d)
        res_ps = [_par(lp["post_mix"]), _par(lp["pre_ffn"])]
        (x1, h2), _ = seq_fwd(f"res_mix_fwd_{l}", fn_res, n, TS_ROW, [_rin(xin), _rin(y)], res_ps,
                              [_out(d, F32), _out(d, BF16)])
        nxt = l + 1 < depth
        hid = mm(f"ffn_up_fwd_{l}", h2, tr.weight("ffn_up", l), "nn", F32, 1024, 1024, d,
                 gather=[("w_in", l + 1), ("w_out", l + 1)] if nxt else [])
        f_xs, f_ps, d_ffh = _ffn_ops(hid, lp)
        (act,), _ = seq_fwd(f"ffn_act_fwd_{l}", fn_ffn, n, TS_FFN, f_xs, f_ps,
                            [_out(TC_FFN, BF16, d_ffh, lambda j: j)], ncol=d_ffh // TC_FFN)
        yf = mm(f"ffn_down_fwd_{l}", act, tr.weight("ffn_down", l), "nn", F32, 1024, 1024, d_ffh // 4,
                gather=[("ffn_down", l + 1)] if nxt else [])
        rec = dict(x=xin, h=h, mix_ins=mix_ins, ymix=ymix, y=y, x1=x1, h2=h2, f_xs=f_xs, f_ps=f_ps, act=act, yf=yf)
        if l + 1 < depth:
            ps = [_par(lp["post_ffn"]), _par(lps[l + 1]["pre_mix"])]
            (x2, h), _ = seq_fwd(f"res_ffn_fwd_{l}", fn_res, n, TS_ROW, [_rin(x1), _rin(yf)], ps,
                                 [_out(d, F32), _out(d, BF16)])
            rec["res_ffn_ps"] = ps
            xin = x2
        else:
            def body(xs, blks, ps, carries, r):
                x1_, yf_, t_ = xs
                e = x1_ + _rms(yf_, ps[0]) - t_
                part = 0.5 * jnp.sum(jnp.mean(e * e, axis=-1, keepdims=True), axis=0, keepdims=True)
                return [e * (1.0 / d)], [], [jnp.broadcast_to(part, (8, 128))], []

            ps = [_par(lp["post_ffn"])]
            dx_last, loss = _block_call("loss_fwd", body, n, TS_ROW, 1, False, [_rin(x1), _rin(yf), _rin(target)],
                                        [], ps, [_out(d, F32)], [], [dict(shape=(8, 128), total=None, col=None)], [])
            rec["res_ffn_ps"] = ps
        saved.append(rec)

    grads = {}
    dx2, dh_next = dx_last, None
    for l in reversed(range(depth)):
        rec, lp = saved[l], lps[l]
        d_ffh = lp["d_ffh"]
        g = {}
        if dh_next is None:
            (dx1, dyf), (g["post_ffn"],) = seq_bwd(f"res_ffn_bwd_{l}", fn_res_last, n, TS_ROW,
                                                   [_rin(rec["x1"]), _rin(rec["yf"])], rec["res_ffn_ps"], [_rin(dx2)],
                                                   din_dtypes=[F32, BF16])
        else:
            (dx1, dyf), (g["post_ffn"], g_next_pre) = seq_bwd(
                f"res_ffn_bwd_{l}", fn_res, n, TS_ROW, [_rin(rec["x1"]), _rin(rec["yf"])], rec["res_ffn_ps"],
                [_rin(dx2), _rin(dh_next)], din_dtypes=[F32, BF16])
            grads[l + 1]["pre_mix"] = g_next_pre
        dact = mm(f"ffn_down_dx_{l}", dyf, tr.weight("ffn_down", l), "nt", BF16, 1024, 512, d,
                  exchange=[("w_in", l + 1)])
        tr.grad("ffn_down", l, mm(f"ffn_down_dw_{l}", rec["act"], dyf, "tn", BF16, 512, 1024, 1024))
        off = d_ffh // TC_FFN
        (dhg, dhv), (g_cwg, g_cwv, g_cbg, g_cbv) = seq_bwd(
            f"ffn_act_bwd_{l}", fn_ffn, n, TS_FFN, rec["f_xs"], rec["f_ps"], [_rin(dact, TC_FFN, lambda j: j)],
            din_dtypes=[BF16, BF16], ncol=off, din_specs=[(d_ffh, lambda j: j), (d_ffh, lambda j: j)])
        dhid = jnp.concatenate([dhg, dhv], axis=1)
        g["ffn_cw"] = jnp.concatenate([g_cwg[:, :d_ffh], g_cwv[:, d_ffh:]], axis=1)
        g["ffn_cb"] = jnp.concatenate([g_cbg[:, :d_ffh], g_cbv[:, d_ffh:]], axis=1)
        dh2 = mm(f"ffn_up_dx_{l}", dhid, tr.weight("ffn_up", l), "nt", BF16, 1024, 1024, 1024,
                 exchange=[("ffn_down", l)])
        tr.grad("ffn_up", l, mm(f"ffn_up_dw_{l}", rec["h2"], dhid, "tn", BF16, 1024, 1024, 1024))
        (dx, dy), (g["post_mix"], g["pre_ffn"]) = seq_bwd(
            f"res_mix_bwd_{l}", fn_res, n, TS_ROW, [_rin(rec["x"]), _rin(rec["y"])],
            [_par(lp["post_mix"]), _par(lp["pre_ffn"])], [_rin(dx1), _rin(dh2)], din_dtypes=[F32, BF16])
        dymix = mm(f"w_out_dx_{l}", dy, tr.weight("w_out", l), "nt", BF16, 1024, 1024, d)
        tr.grad("w_out", l, mm(f"w_out_dw_{l}", rec["ymix"], dy, "tn", BF16, 1024, 1024, 1024))
        jobs, keys = tr.jobs(exchange=[("ffn_up", l), ("w_out", l)])
        dp, g["mix"], res = _mixers_bwd(l, dymix, rec["mix_ins"], n, jobs)
        tr.done(keys, res)
        dh = mm(f"w_in_dx_{l}", dp, tr.weight("w_in", l), "nt", BF16, 1024, 1024, N_INP // 5)
        tr.grad("w_in", l, mm(f"w_in_dw_{l}", rec["h"], dp, "tn", BF16, 1024, 640, 1024))
        grads[l] = g
        dx2, dh_next = dx, dh
    (grad_x,), (g_pre0,) = seq_bwd("norm_bwd", fn_norm_keep, n, TS_ROW, [_rin(x)], [_par(lps[0]["pre_mix"])],
                                   [_rin(dh_next), _rin(dx2)])
    grads[0]["pre_mix"] = g_pre0
    tr.flush()
    return loss[0, 0], grad_x, _name_grads(grads, depth)


def _name_grads(grads, depth):
    per = {k: [] for k in SMALL}
    for l in range(depth):
        g = grads[l]
        m = g["mix"]
        cw, gp, nw = m["gdn"]
        lcw, lcb, lwa, lba, lwx, lbx, llam, gw0 = m["lru"]
        lnw, lnb, ws, bst, gw1 = m["sgu"]
        scw, gw2 = m["sc"]
        per["pre_mix_norm"].append(g["pre_mix"][0])
        per["gdn_conv_w"].append(cw)
        per["gdn_a_log"].append(gp[0, 4:8])
        per["gdn_dt_bias"].append(gp[1, 4:8])
        per["gdn_norm_w"].append(nw[0])
        per["lru_conv_w"].append(lcw)
        per["lru_conv_b"].append(lcb[0])
        per["lru_wa"].append(_block_diag_grad(lwa, LRU_BLOCKS))
        per["lru_ba"].append(lba.reshape(LRU_BLOCKS, -1))
        per["lru_wx"].append(_block_diag_grad(lwx, LRU_BLOCKS))
        per["lru_bx"].append(lbx.reshape(LRU_BLOCKS, -1))
        per["lru_lambda"].append(llam[0])
        per["sgu_ln_w"].append(lnw[0])
        per["sgu_ln_b"].append(lnb[0])
        per["sgu_ws"].append(ws.reshape(SGU_GROUPS, SGU_CHUNK, SGU_CHUNK))
        per["sgu_b"].append(bst[:, :SGU_GROUPS].T)
        per["sconv_w"].append(scw)
        per["grp_norm_w"].append(jnp.concatenate([gw0, gw1, gw2], axis=0))
        per["post_mix_norm"].append(g["post_mix"][0])
        per["pre_ffn_norm"].append(g["pre_ffn"][0])
        per["ffn_conv_w"].append(g["ffn_cw"])
        per["ffn_conv_b"].append(g["ffn_cb"][0])
        per["post_ffn_norm"].append(g["post_ffn"][0])
    return {k: jnp.stack(v) for k, v in per.items()}


def _regroup_w_in(w):
    pad = jnp.zeros(w.shape[:-1] + (N_INP - N_IN,), w.dtype)
    return jnp.concatenate([w[..., :2048], w[..., 2056:], w[..., 2048:2056], pad], axis=-1)


def _ungroup_w_in(g):
    return jnp.concatenate([g[..., :2048], g[..., BA_COL:BA_COL + 8], g[..., 2048:BA_COL]], axis=-1)


def kernel(x, pre_mix_norm, w_in, gdn_conv_w, gdn_a_log, gdn_dt_bias, gdn_norm_w, lru_conv_w, lru_conv_b, lru_wa, lru_ba, lru_wx, lru_bx, lru_lambda, sgu_ln_w, sgu_ln_b, sgu_ws, sgu_b, sconv_w, grp_norm_w, w_out, post_mix_norm, pre_ffn_norm, ffn_up, ffn_conv_w, ffn_conv_b, ffn_down, post_ffn_norm, loss_target, m_pre_mix_norm, m_w_in, m_gdn_conv_w, m_gdn_a_log, m_gdn_dt_bias, m_gdn_norm_w, m_lru_conv_w, m_lru_conv_b, m_lru_wa, m_lru_ba, m_lru_wx, m_lru_bx, m_lru_lambda, m_sgu_ln_w, m_sgu_ln_b, m_sgu_ws, m_sgu_b, m_sconv_w, m_grp_norm_w, m_w_out, m_post_mix_norm, m_pre_ffn_norm, m_ffn_up, m_ffn_conv_w, m_ffn_conv_b, m_ffn_down, m_post_ffn_norm, v_pre_mix_norm, v_w_in, v_gdn_conv_w, v_gdn_a_log, v_gdn_dt_bias, v_gdn_norm_w, v_lru_conv_w, v_lru_conv_b, v_lru_wa, v_lru_ba, v_lru_wx, v_lru_bx, v_lru_lambda, v_sgu_ln_w, v_sgu_ln_b, v_sgu_ws, v_sgu_b, v_sconv_w, v_grp_norm_w, v_w_out, v_post_mix_norm, v_pre_ffn_norm, v_ffn_up, v_ffn_conv_w, v_ffn_conv_b, v_ffn_down, v_post_ffn_norm):
    args = locals()
    w_loc = {k: args[k] for k in WEIGHTS}
    m_loc = {k: args["m_" + k] for k in WEIGHTS}
    v_loc = {k: args["v_" + k] for k in WEIGHTS}
    depth = pre_mix_norm.shape[0]
    x_, y_, c_ = _my_pos()
    me = 4 * x_ + 2 * y_ + c_

    shards = {}
    for name in BIG:
        l, r, c = w_loc[name].shape
        wb = cast_bf16(f"cast_{name}", w_loc[name].reshape(l * r, c), 256 if (l * r) % 256 == 0 else 64)
        shards[name] = wb.reshape(l, r, c)
    tr = _Traffic(shards=shards)
    wt = {k: w_loc[k] for k in SMALL}
    shard_shapes = [w_loc[k].shape for k in SHARDED_SMALL]
    gathered = all_gather("gather_small", _pack([w_loc[k] for k in SHARDED_SMALL]), True)
    per_dev = [_unpack(gathered[s], shard_shapes) for s in range(N_DEV)]
    for k_i, k in enumerate(SHARDED_SMALL):
        wt[k] = jnp.concatenate([per_dev[s][k_i] for s in range(N_DEV)], axis=-1)

    loss_part, grad_x, g_full = local_step(x[0], loss_target[0], wt, tr)
    loss = lax.psum(loss_part, ("x", "y", "c"))

    outs_g, outs_d, outs_m, outs_v = {}, {}, {}, {}
    for name in BIG:
        l, r, c = w_loc[name].shape
        land = jnp.stack([tr.landed[(name, k)] for k in range(l)], axis=1).reshape(N_DEV, l * r, c)
        res = adamw_big(f"adamw_{name}", w_loc[name].reshape(l * r, c), land, m_loc[name].reshape(l * r, c),
                        v_loc[name].reshape(l * r, c), 128 if (l * r) % 128 == 0 else 64)
        outs_g[name], outs_d[name], outs_m[name], outs_v[name] = [a.reshape(l, r, c) for a in res]

    full_shapes = [g_full[k].shape for k in SMALL]
    g_all = all_gather("gather_small_grads", _pack([g_full[k] for k in SMALL]), True)
    g_sum = _unpack(sum_blocks("sum_small_grads", g_all), full_shapes)
    g_small = {}
    for k, g in zip(SMALL, g_sum):
        if k in SHARDED_SMALL:
            w = w_loc[k].shape[-1]
            g = lax.dynamic_slice_in_dim(g, me * w, w, axis=g.ndim - 1)
        g_small[k] = g
    loc_shapes = [w_loc[k].shape for k in SMALL]
    res = adamw_small("adamw_small", _pack([w_loc[k] for k in SMALL]), _pack([g_small[k] for k in SMALL]),
                      _pack([m_loc[k] for k in SMALL]), _pack([v_loc[k] for k in SMALL]))
    d_s, m_s, v_s = [_unpack(a, loc_shapes) for a in res]
    for k_i, k in enumerate(SMALL):
        outs_g[k], outs_d[k], outs_m[k], outs_v[k] = g_small[k], d_s[k_i], m_s[k_i], v_s[k_i]

    return (loss, grad_x[None], *[outs_g[k] for k in WEIGHTS], *[outs_d[k] for k in WEIGHTS],
            *[outs_m[k] for k in WEIGHTS], *[outs_v[k] for k in WEIGHTS])
```

```python
import functools
import math

import jax
import jax.numpy as jnp
from jax import lax
from jax.experimental import pallas as pl
from jax.experimental.pallas import tpu as pltpu

F32 = jnp.float32
BF16 = jnp.bfloat16
EPS = 1e-6
HALO = 8
VMEM_LIMIT = 56 * 1024 * 1024
MESH = pl.DeviceIdType.MESH
N_DEV = 8

ADAM_LR, ADAM_B1, ADAM_B2, ADAM_EPS, ADAM_WD, ADAM_STEP = 0.001, 0.9, 0.999, 1e-08, 0.01, 10

GDN_HEADS, GDN_DIM, GDN_CHUNK = 4, 128, 64
SGU_GROUPS, SGU_CHUNK = 4, 128
LRU_BLOCKS, LRU_C = 8, 8.0
D_G = 512
N_IN = 5640
N_INP = 5760
BA_COL = 5632

SHARDED_SMALL = ("gdn_conv_w", "lru_conv_w", "sconv_w", "grp_norm_w", "ffn_conv_w")
BIG = ("w_in", "w_out", "ffn_up", "ffn_down")
WEIGHTS = ("pre_mix_norm", "w_in", "gdn_conv_w", "gdn_a_log", "gdn_dt_bias", "gdn_norm_w", "lru_conv_w",
           "lru_conv_b", "lru_wa", "lru_ba", "lru_wx", "lru_bx", "lru_lambda", "sgu_ln_w", "sgu_ln_b", "sgu_ws",
           "sgu_b", "sconv_w", "grp_norm_w", "w_out", "post_mix_norm", "pre_ffn_norm", "ffn_up", "ffn_conv_w",
           "ffn_conv_b", "ffn_down", "post_ffn_norm")
SMALL = tuple(n for n in WEIGHTS if n not in BIG)


def _dot(a, b, ca, cb):
    return lax.dot_general(a.astype(BF16), b.astype(BF16), (((ca,), (cb,)), ((), ())),
                           preferred_element_type=F32)


@jax.custom_vjp
def _mm(a, b):
    return _dot(a, b, 1, 0)


def _mm_f(a, b):
    return _dot(a, b, 1, 0), (a, b)


def _mm_b(res, g):
    a, b = res
    return _dot(g, b, 1, 1), _dot(a, g, 0, 0)


_mm.defvjp(_mm_f, _mm_b)


@jax.custom_vjp
def _mm_nt(a, b):
    return _dot(a, b, 1, 1)


def _mm_nt_f(a, b):
    return _dot(a, b, 1, 1), (a, b)


def _mm_nt_b(res, g):
    a, b = res
    return _dot(g, b, 1, 0), _dot(g, a, 0, 0)


_mm_nt.defvjp(_mm_nt_f, _mm_nt_b)


@jax.custom_vjp
def _mm_tn(a, b):
    return _dot(a, b, 0, 0)


def _mm_tn_f(a, b):
    return _dot(a, b, 0, 0), (a, b)


def _mm_tn_b(res, g):
    a, b = res
    return _dot(b, g, 1, 1), _dot(a, g, 1, 0)


_mm_tn.defvjp(_mm_tn_f, _mm_tn_b)


def _dot_exact(a, b, ca, cb):
    return lax.dot_general(a, b, (((ca,), (cb,)), ((), ())), precision=lax.Precision.HIGHEST,
                           preferred_element_type=F32)


@functools.partial(jax.custom_vjp, nondiff_argnums=(1,))
def _shift_rows(x, s):
    return pltpu.roll(x, s, 0)


def _shift_rows_f(x, s):
    return pltpu.roll(x, s, 0), None


def _shift_rows_b(s, _, g):
    return (pltpu.roll(g, (g.shape[0] - s) % g.shape[0], 0),)


_shift_rows.defvjp(_shift_rows_f, _shift_rows_b)


def _sigmoid(x):
    return 1.0 / (1.0 + jnp.exp(-x))


def _silu(x):
    return x * _sigmoid(x)


def _gelu(x):
    return 0.5 * x * (1.0 + jnp.tanh(0.7978845608028654 * (x + 0.044715 * (x * x * x))))


@jax.custom_vjp
def _softplus(x):
    e = jnp.exp(-jnp.abs(x))
    u = 1.0 + e
    log1p = jnp.where(u == 1.0, e, jnp.log(u) * (e / jnp.where(u == 1.0, 1.0, u - 1.0)))
    return jnp.maximum(x, 0.0) + log1p


def _softplus_f(x):
    return _softplus(x), x


def _softplus_b(x, g):
    return (g * _sigmoid(x),)


_softplus.defvjp(_softplus_f, _softplus_b)


def _neg_expm1(y):
    return -jnp.tanh(0.5 * y) * (jnp.exp(y) + 1.0)


def _rms(x, w):
    return x * lax.rsqrt(jnp.mean(x * x, axis=-1, keepdims=True) + EPS) * w


def _row(w, k):
    sel = lax.broadcasted_iota(jnp.int32, w.shape, 0) == k
    return jnp.sum(jnp.where(sel, w, 0.0), axis=0, keepdims=True)


def _col(x, j):
    sel = lax.broadcasted_iota(jnp.int32, x.shape, 1) == j
    return jnp.sum(jnp.where(sel, x, 0.0), axis=1, keepdims=True)


def _conv(x_ext, w, taps):
    acc = None
    for k in range(taps):
        s = taps - 1 - k
        t = (x_ext if s == 0 else _shift_rows(x_ext, s)) * _row(w, k)
        acc = t if acc is None else acc + t
    return acc[HALO:]


@jax.custom_vjp
def _scan(a, b, h0):
    n = a.shape[0]
    row = lax.broadcasted_iota(jnp.int32, a.shape, 0)
    s = 1
    while s < n:
        keep = row >= s
        a_sh = jnp.where(keep, pltpu.roll(a, s, 0), 1.0)
        b_sh = jnp.where(keep, pltpu.roll(b, s, 0), 0.0)
        b = a * b_sh + b
        a = a * a_sh
        s *= 2
    return b + a * h0


def _scan_f(a, b, h0):
    h = _scan(a, b, h0)
    return h, (a, h, h0)


def _scan_b(res, dh):
    a, h, h0 = res
    n = a.shape[0]
    row = lax.broadcasted_iota(jnp.int32, a.shape, 0)
    an = jnp.where(row < n - 1, pltpu.roll(a, n - 1, 0), 0.0)
    lam = dh
    s = 1
    while s < n:
        keep = row < n - s
        a_sh = jnp.where(keep, pltpu.roll(an, n - s, 0), 1.0)
        l_sh = jnp.where(keep, pltpu.roll(lam, n - s, 0), 0.0)
        lam = an * l_sh + lam
        an = an * a_sh
        s *= 2
    h_prev = jnp.where(row >= 1, pltpu.roll(h, 1, 0), h0)
    al = a * lam
    dh0 = jnp.sum(jnp.where(row == 0, al, 0.0), axis=0, keepdims=True)
    return lam * h_prev, lam, dh0


_scan.defvjp(_scan_f, _scan_b)


@jax.custom_vjp
def _unit_lower_inverses(ms):
    n = ms[0].shape[0]
    shape = ms[0].shape
    eye = (lax.broadcasted_iota(jnp.int32, shape, 0) == lax.broadcasted_iota(jnp.int32, shape, 1)).astype(F32)
    p = [-m for m in ms]
    t = [eye + a for a in p]
    steps = 1
    while 2 ** steps < n:
        p = [_mm(a, a) for a in p]
        t = [a + _mm(a, c) for a, c in zip(t, p)]
        steps += 1
    return t


def _unit_lower_inverses_f(ms):
    t = _unit_lower_inverses(ms)
    return t, t


def _unit_lower_inverses_b(t, dt):
    x = [_mm_nt(g, a) for g, a in zip(dt, t)]
    return ([-_mm_tn(a, c) for a, c in zip(t, x)],)


_unit_lower_inverses.defvjp(_unit_lower_inverses_f, _unit_lower_inverses_b)


def _last_row(x):
    sel = lax.broadcasted_iota(jnp.int32, x.shape, 0) == x.shape[0] - 1
    return jnp.sum(jnp.where(sel, x, 0.0), axis=0, keepdims=True)


def fn_norm(xs, st, ps):
    (x,), (w,) = xs, ps
    return [_rms(x, w).astype(BF16)], []


def fn_norm_keep(xs, st, ps):
    (x,), (w,) = xs, ps
    return [_rms(x, w).astype(BF16), x], []


def fn_res(xs, st, ps):
    (x, y), (w_post, w_next) = xs, ps
    x1 = x + _rms(y, w_post)
    return [x1, _rms(x1, w_next).astype(BF16)], []


def fn_res_last(xs, st, ps):
    (x, y), (w_post,) = xs, ps
    return [x + _rms(y, w_post)], []


def fn_gdn(xs, st, ps):
    qkv_ext, z, ba = xs
    (state,) = st
    cw, gp, nw = ps
    ts = z.shape[0]
    qkv = _silu(_conv(qkv_ext, cw, 4))
    beta_all = _sigmoid(ba)
    g_all = -jnp.exp(_row(gp, 0)) * _softplus(ba + _row(gp, 1))
    c_n = GDN_CHUNK
    ri = lax.broadcasted_iota(jnp.int32, (c_n, c_n), 0)
    ci = lax.broadcasted_iota(jnp.int32, (c_n, c_n), 1)
    causal, strict = ri >= ci, ri > ci
    tril = causal.astype(F32)
    lane = lax.broadcasted_iota(jnp.int32, (c_n, 128), 1)
    s_h = [state[GDN_DIM * h:GDN_DIM * (h + 1)] for h in range(GDN_HEADS)]
    n_c = ts // c_n
    pairs = [(c, h) for c in range(n_c) for h in range(GDN_HEADS)]
    every = lambda f, *lists: [f(*a) for a in zip(*lists)]

    def piece(c, h, base):
        return qkv[c * c_n:(c + 1) * c_n, base + GDN_DIM * h:base + GDN_DIM * (h + 1)]

    q = [piece(c, h, 0) for c, h in pairs]
    k = [piece(c, h, D_G) for c, h in pairs]
    v = [piece(c, h, 2 * D_G) for c, h in pairs]
    q = every(lambda t: t * lax.rsqrt(jnp.sum(t * t, axis=-1, keepdims=True) + EPS) * (GDN_DIM ** -0.5), q)
    k = every(lambda t: t * lax.rsqrt(jnp.sum(t * t, axis=-1, keepdims=True) + EPS), k)
    gcum_all = [_dot_exact(tril, g_all[c * c_n:(c + 1) * c_n], 1, 0) for c in range(n_c)]
    b = [_col(beta_all[c * c_n:(c + 1) * c_n], h) for c, h in pairs]
    gc = [_col(gcum_all[c], 4 + h) for c, h in pairs]
    gr = [_dot_exact((lane == 4 + h).astype(F32), gcum_all[c], 1, 1) for c, h in pairs]
    decay = every(lambda a, r: jnp.where(causal, jnp.exp(jnp.where(causal, a - r, 0.0)), 0.0), gc, gr)
    kb = every(lambda a, c: a * c, k, b)
    mk = every(lambda a, c, e: _mm_nt(jnp.concatenate([a, c], axis=0), e), kb, q, k)
    m = every(lambda a, dcy: jnp.where(strict, a[:c_n] * dcy, 0.0), mk, decay)
    attn = every(lambda a, dcy: jnp.where(causal, a[c_n:] * dcy, 0.0), mk, decay)
    t_ = _unit_lower_inverses(m)
    eg = every(jnp.exp, gc)
    wu = every(lambda t, a, e, c, d: _mm(t, jnp.concatenate([a * e, c * d], axis=1)), t_, kb, eg, v, b)
    g_last = every(_last_row, gc)
    k_g = every(lambda a, gl, g: a * jnp.exp(gl - g), k, g_last, gc)
    wq = every(lambda a, c, e: jnp.concatenate([a[:, :GDN_DIM], c * e], axis=0), wu, q, eg)
    u = [a[:, GDN_DIM:] for a in wu]
    gl = every(jnp.exp, g_last)

    o = []
    for c in range(n_c):
        idx = range(c * GDN_HEADS, (c + 1) * GDN_HEADS)
        ws = [_mm(wq[i], s_h[h]) for h, i in enumerate(idx)]
        v_new = [u[i] - ws[h][:c_n] for h, i in enumerate(idx)]
        av = [_mm(attn[i], v_new[h]) for h, i in enumerate(idx)]
        kv = [_mm_tn(k_g[i], v_new[h]) for h, i in enumerate(idx)]
        o += [ws[h][c_n:] + av[h] for h in range(GDN_HEADS)]
        s_h = [s_h[h] * gl[i] + kv[h] for h, i in enumerate(idx)]
    zz = [z[c * c_n:(c + 1) * c_n, GDN_DIM * h:GDN_DIM * (h + 1)] for c, h in pairs]
    y = every(lambda a, g: a * lax.rsqrt(jnp.mean(a * a, axis=-1, keepdims=True) + EPS) * nw * _silu(g), o, zz)
    rows = [jnp.concatenate(y[c * GDN_HEADS:(c + 1) * GDN_HEADS], axis=1) for c in range(n_c)]
    y = rows[0] if n_c == 1 else jnp.concatenate(rows, axis=0)
    return [y.astype(BF16)], [jnp.concatenate(s_h, axis=0)]


def fn_lru(xs, st, ps):
    x_ext, gate = xs
    (h0,) = st
    cw, cb, wa, ba, wx, bx, lam, gw = ps
    xc = _conv(x_ext, cw, 4) + cb
    r = _sigmoid(_mm(xc, wa) + ba)
    i = _sigmoid(_mm(xc, wx) + bx)
    log_a = -LRU_C * r * _softplus(-lam)
    a = jnp.exp(log_a)
    mult = jnp.sqrt(_neg_expm1(2.0 * log_a))
    h = _scan(a, mult * (i * xc), h0)
    y = _rms(h * _gelu(gate), gw)
    return [y.astype(BF16)], [_last_row(h)]


def fn_sgu(xs, st, ps):
    (uv,) = xs
    lnw, lnb, ws, bst, gw = ps
    ts = uv.shape[0]
    uvf = _gelu(uv)
    u, v = uvf[:, :D_G], uvf[:, D_G:]
    vc = v - jnp.mean(v, axis=-1, keepdims=True)
    v = vc * lax.rsqrt(jnp.mean(vc * vc, axis=-1, keepdims=True) + EPS) * lnw + lnb
    t_n = SGU_CHUNK
    tril = lax.broadcasted_iota(jnp.int32, (t_n, t_n), 0) >= lax.broadcasted_iota(jnp.int32, (t_n, t_n), 1)
    wg = [jnp.where(tril, ws[t_n * g:t_n * (g + 1)], 0.0) for g in range(SGU_GROUPS)]
    bg = [_col(bst, g) for g in range(SGU_GROUPS)]
    rows = []
    for c in range(ts // t_n):
        vcg = v[c * t_n:(c + 1) * t_n]
        rows.append(jnp.concatenate(
            [_mm(wg[g], vcg[:, 128 * g:128 * (g + 1)]) + bg[g] for g in range(SGU_GROUPS)], axis=1))
    vv = rows[0] if len(rows) == 1 else jnp.concatenate(rows, axis=0)
    return [_rms(u * vv, gw).astype(BF16)], []


def fn_sconv(xs, st, ps):
    bg, cg_ext, hh_ext = xs
    cw, gw = ps
    return [_rms(bg * _conv(cg_ext * hh_ext, cw, 3), gw).astype(BF16)], []


def fn_ffn(xs, st, ps):
    g_ext, v_ext = xs
    cwg, cwv, cbg, cbv = ps
    g = _conv(g_ext, cwg, 3) + cbg
    v = _conv(v_ext, cwv, 3) + cbv
    return [(_gelu(g) * v).astype(BF16)], []


def _my_pos():
    return lax.axis_index("x"), lax.axis_index("y"), lax.axis_index("c")


def _peer(pos, k):
    x_, y_, c_ = pos
    return (1 - x_ if (k >> 2) & 1 else x_, 1 - y_ if (k >> 1) & 1 else y_, 1 - c_ if k & 1 else c_)


def _dev_index(p):
    return 4 * p[0] + 2 * p[1] + p[2]


class _Side:
    def __init__(self, jobs):
        self.jobs = list(jobs)
        n = len(self.jobs)
        self.operands = [a for _, a in self.jobs]
        self.in_specs = [pl.BlockSpec(memory_space=pl.ANY)] * n
        self.out_shape = [jax.ShapeDtypeStruct(((N_DEV,) + a.shape) if kind == "gather" else a.shape, a.dtype)
                          for kind, a in self.jobs]
        self.out_specs = [pl.BlockSpec(memory_space=pl.ANY)] * n
        self.scratch = [pltpu.SemaphoreType.DMA((7 * n,)), pltpu.SemaphoreType.DMA((7 * n,)),
                        pltpu.SemaphoreType.DMA((n,))] if n else []

    def _copies(self, in_refs, out_refs, sems, landings=True):
        send, recv, local = sems
        pos = _my_pos()
        me = _dev_index(pos)
        mine, outgoing, landing = [], [], []
        for j, (kind, _) in enumerate(self.jobs):
            src, dst = in_refs[j], out_refs[j]
            own = src if kind == "gather" else src.at[me]
            mine.append(pltpu.make_async_copy(own, dst.at[me], local.at[j]))
            for k in range(1, N_DEV):
                p = _peer(pos, k)
                sems_k = dict(send_sem=send.at[7 * j + k - 1], recv_sem=recv.at[7 * j + k - 1], device_id=p,
                              device_id_type=MESH)
                outgoing.append(pltpu.make_async_remote_copy(
                    src_ref=src if kind == "gather" else src.at[_dev_index(p)], dst_ref=dst.at[me], **sems_k))
                if landings:
                    landing.append(pltpu.make_async_remote_copy(src_ref=own, dst_ref=dst.at[_dev_index(p)], **sems_k))
        return mine, outgoing, landing

    def start(self, in_refs, out_refs, sems):
        mine, outgoing, _ = self._copies(in_refs, out_refs, sems, landings=False)
        for cp in mine + outgoing:
            cp.start()

    def wait(self, in_refs, out_refs, sems):
        mine, outgoing, landing = self._copies(in_refs, out_refs, sems)
        for cp in landing:
            cp.wait_recv()
        for cp in outgoing:
            cp.wait_send()
        for cp in mine:
            cp.wait()


def _rin(arr, w=None, col=0, halo=False):
    return dict(arr=arr, w=arr.shape[1] if w is None else w, col=col, halo=halo)


def _par(arr, w=None, col=None):
    return dict(arr=arr, w=w, col=col)


def _colidx(col, j):
    return col(j) if callable(col) else col


def _block_call(name, body, n_rows, ts, ncol, reverse, row_ins, blk_ins, params, row_outs, blk_outs, acc_outs,
                carries, side=()):
    side = _Side(side)
    nblk = n_rows // ts
    hb = ts // HALO

    def rr(i):
        return (nblk - 1 - i) if reverse else i

    in_specs, operands = [], []
    for s in row_ins:
        in_specs.append(pl.BlockSpec((ts, s["w"]), lambda j, i, s=s: (rr(i), _colidx(s["col"], j))))
        operands.append(s["arr"])
        if s["halo"]:
            in_specs.append(pl.BlockSpec((HALO, s["w"]),
                                         lambda j, i, s=s: (jnp.maximum(rr(i) * hb - 1, 0), _colidx(s["col"], j))))
            operands.append(s["arr"])
    for a in blk_ins:
        nd = a.ndim - 1
        in_specs.append(pl.BlockSpec((None,) + a.shape[1:], lambda j, i, nd=nd: (rr(i),) + (0,) * nd))
        operands.append(a)
    for p in params:
        a = p["arr"]
        if p["col"] is None:
            in_specs.append(pl.BlockSpec(a.shape, lambda j, i: (0, 0)))
        else:
            in_specs.append(pl.BlockSpec((a.shape[0], p["w"]), lambda j, i, p=p: (0, _colidx(p["col"], j))))
        operands.append(a)

    out_specs, out_shape = [], []
    for o in row_outs:
        out_specs.append(pl.BlockSpec((ts, o["w"]), lambda j, i, o=o: (rr(i), _colidx(o["col"], j))))
        out_shape.append(jax.ShapeDtypeStruct((n_rows, o["total"]), o["dtype"]))
    for o in blk_outs:
        nd = len(o["shape"])
        out_specs.append(pl.BlockSpec((None,) + tuple(o["shape"]), lambda j, i, nd=nd: (rr(i),) + (0,) * nd))
        out_shape.append(jax.ShapeDtypeStruct((nblk,) + tuple(o["shape"]), o["dtype"]))
    for o in acc_outs:
        if o["col"] is None:
            out_specs.append(pl.BlockSpec(o["shape"], lambda j, i: (0, 0)))
            out_shape.append(jax.ShapeDtypeStruct(o["shape"], F32))
        else:
            out_specs.append(pl.BlockSpec(o["shape"], lambda j, i, o=o: (0, _colidx(o["col"], j))))
            out_shape.append(jax.ShapeDtypeStruct((o["shape"][0], o["total"]), F32))

    n_in = len(operands)
    n_row_out, n_blk_out, n_acc = len(row_outs), len(blk_outs), len(acc_outs)
    n_out = n_row_out + n_blk_out + n_acc
    n_side = len(side.jobs)

    def kern(*refs):
        in_refs = refs[:n_in]
        side_in = refs[n_in:n_in + n_side]
        out_refs = refs[n_in + n_side:n_in + n_side + n_out]
        side_out = refs[n_in + n_side + n_out:n_in + 2 * n_side + n_out]
        scratch = refs[n_in + 2 * n_side + n_out:]
        carry_refs, side_sems = scratch[:len(carries)], scratch[len(carries):]
        acc_refs = out_refs[n_row_out + n_blk_out:]
        i = pl.program_id(1)
        r = rr(i)
        if n_side:
            @pl.when((pl.program_id(0) == 0) & (i == 0))
            def _():
                side.start(side_in, side_out, side_sems)

        @pl.when(i == 0)
        def _():
            for c_ref in carry_refs:
                c_ref[...] = jnp.zeros(c_ref.shape, c_ref.dtype)
            for a_ref in acc_refs:
                a_ref[...] = jnp.zeros(a_ref.shape, a_ref.dtype)

        k = 0
        xs = []
        for s in row_ins:
            x = in_refs[k][...]
            k += 1
            if s["halo"]:
                hal = in_refs[k][...]
                k += 1
                hal = jnp.where(r == 0, jnp.zeros_like(hal), hal)
                x = jnp.concatenate([hal, x], axis=0)
            xs.append(x)
        blks = []
        for _ in blk_ins:
            blks.append(in_refs[k][...])
            k += 1
        ps = []
        for _ in params:
            ps.append(in_refs[k][...])
            k += 1
        row_vals, blk_vals, acc_vals, new_carries = body(xs, blks, ps, [c[...] for c in carry_refs], r)
        for ref, val in zip(out_refs[:n_row_out], row_vals):
            ref[...] = val.astype(ref.dtype)
        for ref, val in zip(out_refs[n_row_out:n_row_out + n_blk_out], blk_vals):
            ref[...] = val.astype(ref.dtype)
        for ref, val in zip(acc_refs, acc_vals):
            ref[...] += val
        for ref, val in zip(carry_refs, new_carries):
            ref[...] = val
        if n_side:
            @pl.when((pl.program_id(0) == ncol - 1) & (i == nblk - 1))
            def _():
                side.wait(side_in, side_out, side_sems)

    res = pl.pallas_call(
        kern,
        name=name,
        grid=(ncol, nblk),
        in_specs=in_specs + side.in_specs,
        out_specs=out_specs + side.out_specs,
        out_shape=out_shape + side.out_shape,
        scratch_shapes=[pltpu.VMEM(shape, F32) for shape in carries] + side.scratch,
        compiler_params=pltpu.CompilerParams(dimension_semantics=("arbitrary", "arbitrary"),
                                             vmem_limit_bytes=VMEM_LIMIT),
    )(*operands, *side.operands)
    return list(res)


def _out(w, dtype, total=None, col=0):
    return dict(w=w, dtype=dtype, total=w if total is None else total, col=col)


def seq_fwd(name, fn, n_rows, ts, row_ins, params, outs, state_shapes=(), ncol=1, side=()):
    def body(xs, blks, ps, carries, r):
        o, new_st = fn(xs, list(carries), ps)
        return o, list(carries), [], new_st

    res = _block_call(name, body, n_rows, ts, ncol, False, row_ins, [], params, outs,
                      [dict(shape=s, dtype=F32) for s in state_shapes], [], list(state_shapes), side)
    n_o, n_s = len(outs), len(state_shapes)
    return (res[:n_o], res[n_o:n_o + n_s]) + ((res[n_o + n_s:],) if side else ())


def seq_bwd(name, fn, n_rows, ts, row_ins, params, cots, saved_states=(), din_dtypes=None, ncol=1, din_specs=None,
            side=()):
    n_x, n_p, n_st = len(row_ins), len(params), len(saved_states)
    halo_idx = [k for k, s in enumerate(row_ins) if s["halo"]]
    state_shapes = [a.shape[1:] for a in saved_states]

    def body(xs_all, blks, ps, carries, r):
        xs, cot_vals = xs_all[:n_x], xs_all[n_x:]
        d_state, d_halo = carries[:n_st], carries[n_st:]
        (o, _), vjp = jax.vjp(lambda a, b, c: fn(a, b, c), xs, blks, ps)
        cot = [c.astype(v.dtype) for c, v in zip(cot_vals, o)]
        dxs, dst, dps = vjp((cot, list(d_state)))
        row_vals, new_halo = [], []
        for k, dx in enumerate(dxs):
            if k in halo_idx:
                hk = halo_idx.index(k)
                body_rows = dx[HALO:]
                tail = dx[ts:ts + HALO] + d_halo[hk]
                row_vals.append(jnp.concatenate([body_rows[:ts - HALO], tail], axis=0))
                new_halo.append(dx[:HALO])
            else:
                row_vals.append(dx)
        return row_vals, [], list(dps), list(dst) + new_halo

    din_dtypes = din_dtypes or [F32] * n_x
    douts = []
    for k, s in enumerate(row_ins):
        total, col = (s["w"], 0) if din_specs is None or din_specs[k] is None else din_specs[k]
        douts.append(_out(s["w"], din_dtypes[k], total, col))
    accs = []
    for p in params:
        a = p["arr"]
        if p["col"] is None:
            accs.append(dict(shape=a.shape, total=None, col=None))
        else:
            accs.append(dict(shape=(a.shape[0], p["w"]), total=a.shape[1], col=p["col"]))
    carries = list(state_shapes) + [(HALO, row_ins[k]["w"]) for k in halo_idx]
    res = _block_call(name, body, n_rows, ts, ncol, True, list(row_ins) + list(cots), list(saved_states), params,
                      douts, [], accs, carries, side)
    return (res[:n_x], res[n_x:n_x + n_p]) + ((res[n_x + n_p:],) if side else ())


def matmul(name, a, b, mode, out_dtype, tm, tn, tk, side=()):
    side = _Side(side)
    n_side = len(side.jobs)
    if mode == "tn":
        (kk, m), n = a.shape, b.shape[1]
    else:
        (m, kk), n = a.shape, (b.shape[0] if mode == "nt" else b.shape[1])
    tm, tn, tk = min(tm, m), min(tn, n), min(tk, kk)
    nk = kk // tk
    assert m % tm == 0 and n % tn == 0 and kk % tk == 0, (name, a.shape, b.shape, tm, tn, tk)
    a_spec = pl.BlockSpec((tk, tm), lambda i, j, k: (k, i)) if mode == "tn" else pl.BlockSpec((tm, tk), lambda i, j, k: (i, k))
    b_spec = pl.BlockSpec((tn, tk), lambda i, j, k: (j, k)) if mode == "nt" else pl.BlockSpec((tk, tn), lambda i, j, k: (k, j))
    ca, cb = {"nn": (1, 0), "nt": (1, 1), "tn": (0, 0)}[mode]

    gm, gn = m // tm, n // tn

    def kern(*refs):
        a_ref, b_ref = refs[:2]
        side_in = refs[2:2 + n_side]
        o_ref = refs[2 + n_side]
        side_out = refs[3 + n_side:3 + 2 * n_side]
        acc_ref = refs[3 + 2 * n_side]
        side_sems = refs[4 + 2 * n_side:]
        i, j, k = pl.program_id(0), pl.program_id(1), pl.program_id(2)
        if n_side:
            @pl.when((i == 0) & (j == 0) & (k == 0))
            def _():
                side.start(side_in, side_out, side_sems)

        part = lax.dot_general(a_ref[...], b_ref[...], (((ca,), (cb,)), ((), ())), preferred_element_type=F32)
        if nk == 1:
            o_ref[...] = part.astype(o_ref.dtype)
        else:
            @pl.when(k == 0)
            def _():
                acc_ref[...] = part

            @pl.when(k > 0)
            def _():
                acc_ref[...] += part

            @pl.when(k == nk - 1)
            def _():
                o_ref[...] = acc_ref[...].astype(o_ref.dtype)

        if n_side:
            @pl.when((i == gm - 1) & (j == gn - 1) & (k == nk - 1))
            def _():
                side.wait(side_in, side_out, side_sems)

    semantics = ("arbitrary",) * 3 if n_side else ("parallel", "parallel", "arbitrary")
    res = pl.pallas_call(
        kern,
        name=name,
        grid=(gm, gn, nk),
        in_specs=[a_spec, b_spec] + side.in_specs,
        out_specs=[pl.BlockSpec((tm, tn), lambda i, j, k: (i, j))] + side.out_specs,
        out_shape=[jax.ShapeDtypeStruct((m, n), out_dtype)] + side.out_shape,
        scratch_shapes=[pltpu.VMEM((tm, tn) if nk > 1 else (8, 128), F32)] + side.scratch,
        compiler_params=pltpu.CompilerParams(dimension_semantics=semantics, vmem_limit_bytes=VMEM_LIMIT),
    )(a, b, *side.operands)
    return (res[0], list(res[1:])) if n_side else res[0]


def all_gather(name, x, in_vmem):
    def body(x_ref, out_ref, send_sems, recv_sems, local_sem):
        x_, y_, c_ = _my_pos()
        me, sibling = (x_, y_, c_), (x_, y_, 1 - c_)
        chips = [(1 - x_, y_), (x_, 1 - y_), (1 - x_, 1 - y_)]

        def slot(px, py, pc):
            return out_ref.at[4 * px + 2 * py + pc]

        def copy(k, block, to, src=None):
            return pltpu.make_async_remote_copy(
                src_ref=slot(*block) if src is None else src, dst_ref=slot(*block),
                send_sem=send_sems.at[k], recv_sem=recv_sems.at[k], device_id=to, device_id_type=MESH)

        mine = pltpu.make_async_copy(x_ref, slot(*me), local_sem)
        mine.start()
        first = [copy(0, me, sibling, src=x_ref)]
        first += [copy(1 + j, me, (*chip, c_), src=x_ref) for j, chip in enumerate(chips)]
        for cp in first:
            cp.start()
        passed = [copy(4 + j, (*chip, c_), sibling) for j, chip in enumerate(chips)]
        for j, chip in enumerate(chips):
            copy(1 + j, (*chip, c_), me).wait_recv()
            passed[j].start()
        copy(0, sibling, me).wait_recv()
        for j, chip in enumerate(chips):
            copy(4 + j, (*chip, 1 - c_), me).wait_recv()
        for cp in first + passed:
            cp.wait_send()
        mine.wait()

    space = pltpu.VMEM if in_vmem else pl.ANY
    return pl.pallas_call(
        body,
        name=name,
        out_shape=jax.ShapeDtypeStruct((N_DEV,) + x.shape, x.dtype),
        in_specs=[pl.BlockSpec(memory_space=space)],
        out_specs=pl.BlockSpec(memory_space=space),
        scratch_shapes=[pltpu.SemaphoreType.DMA((7,)), pltpu.SemaphoreType.DMA((7,)), pltpu.SemaphoreType.DMA],
        compiler_params=pltpu.CompilerParams(vmem_limit_bytes=VMEM_LIMIT),
    )(x)


def all_to_all(name, g):
    def body(g_ref, out_ref, send_sems, recv_sems, local_sem):
        x_, y_, c_ = _my_pos()
        me = 4 * x_ + 2 * y_ + c_

        def peer(k):
            fx, fy, fc = (k >> 2) & 1, (k >> 1) & 1, k & 1
            return (1 - x_ if fx else x_, 1 - y_ if fy else y_, 1 - c_ if fc else c_)

        def copy(k):
            px, py, pc = peer(k)
            return pltpu.make_async_remote_copy(
                src_ref=g_ref.at[4 * px + 2 * py + pc], dst_ref=out_ref.at[me],
                send_sem=send_sems.at[k - 1], recv_sem=recv_sems.at[k - 1], device_id=(px, py, pc), device_id_type=MESH)

        def landing(k):
            px, py, pc = peer(k)
            return pltpu.make_async_remote_copy(
                src_ref=g_ref.at[me], dst_ref=out_ref.at[4 * px + 2 * py + pc],
                send_sem=send_sems.at[k - 1], recv_sem=recv_sems.at[k - 1], device_id=(px, py, pc), device_id_type=MESH)

        mine = pltpu.make_async_copy(g_ref.at[me], out_ref.at[me], local_sem)
        mine.start()
        sends = [copy(k) for k in range(1, N_DEV)]
        for cp in sends:
            cp.start()
        for k in range(1, N_DEV):
            landing(k).wait_recv()
        for cp in sends:
            cp.wait_send()
        mine.wait()

    return pl.pallas_call(
        body,
        name=name,
        out_shape=jax.ShapeDtypeStruct(g.shape, g.dtype),
        in_specs=[pl.BlockSpec(memory_space=pl.ANY)],
        out_specs=pl.BlockSpec(memory_space=pl.ANY),
        scratch_shapes=[pltpu.SemaphoreType.DMA((7,)), pltpu.SemaphoreType.DMA((7,)), pltpu.SemaphoreType.DMA],
    )(g)


def sum_blocks(name, g):
    def body(g_ref, o_ref):
        acc = g_ref[0]
        for s in range(1, N_DEV):
            acc = acc + g_ref[s]
        o_ref[...] = acc

    r = g.shape[1]
    tr = r // 4 if r % 32 == 0 else r
    return pl.pallas_call(
        body, name=name, grid=(r // tr,),
        in_specs=[pl.BlockSpec((N_DEV, tr, 128), lambda i: (0, i, 0))],
        out_specs=pl.BlockSpec((tr, 128), lambda i: (i, 0)),
        out_shape=jax.ShapeDtypeStruct((r, 128), F32),
        compiler_params=pltpu.CompilerParams(vmem_limit_bytes=VMEM_LIMIT),
    )(g)


def _adamw_math(w, g, m, v):
    m = ADAM_B1 * m + (1.0 - ADAM_B1) * g
    v = ADAM_B2 * v + (1.0 - ADAM_B2) * (g * g)
    m_hat = m / (1.0 - ADAM_B1 ** ADAM_STEP)
    v_hat = v / (1.0 - ADAM_B2 ** ADAM_STEP)
    delta = -ADAM_LR * (m_hat / (jnp.sqrt(v_hat) + ADAM_EPS) + ADAM_WD * w)
    return delta, m, v


def adamw_big(name, w, land, m, v, tr):
    r, c = w.shape

    def body(w_ref, l_ref, m_ref, v_ref, g_out, d_out, m_out, v_out):
        g = l_ref[0].astype(F32)
        for s in range(1, N_DEV):
            g = g + l_ref[s].astype(F32)
        delta, m_new, v_new = _adamw_math(w_ref[...], g, m_ref[...], v_ref[...])
        g_out[...] = g
        d_out[...] = delta
        m_out[...] = m_new
        v_out[...] = v_new

    spec = pl.BlockSpec((tr, c), lambda i: (i, 0))
    return pl.pallas_call(
        body, name=name, grid=(r // tr,),
        in_specs=[spec, pl.BlockSpec((N_DEV, tr, c), lambda i: (0, i, 0)), spec, spec],
        out_specs=[spec] * 4,
        out_shape=[jax.ShapeDtypeStruct((r, c), F32)] * 4,
        compiler_params=pltpu.CompilerParams(dimension_semantics=("parallel",), vmem_limit_bytes=VMEM_LIMIT),
    )(w, land, m, v)


def adamw_small(name, w, g, m, v):
    def body(w_ref, g_ref, m_ref, v_ref, d_out, m_out, v_out):
        delta, m_new, v_new = _adamw_math(w_ref[...], g_ref[...], m_ref[...], v_ref[...])
        d_out[...] = delta
        m_out[...] = m_new
        v_out[...] = v_new

    return pl.pallas_call(
        body, name=name,
        out_shape=[jax.ShapeDtypeStruct(w.shape, F32)] * 3,
        compiler_params=pltpu.CompilerParams(vmem_limit_bytes=VMEM_LIMIT),
    )(w, g, m, v)


def cast_bf16(name, w, tr):
    r, c = w.shape

    def body(w_ref, o_ref):
        o_ref[...] = w_ref[...].astype(BF16)

    spec = pl.BlockSpec((tr, c), lambda i: (i, 0))
    return pl.pallas_call(body, name=name, grid=(r // tr,), in_specs=[spec], out_specs=spec,
                          out_shape=jax.ShapeDtypeStruct((r, c), BF16),
                          compiler_params=pltpu.CompilerParams(dimension_semantics=("parallel",)))(w)


def _pack(arrs):
    flat = jnp.concatenate([a.reshape(-1).astype(F32) for a in arrs])
    n = flat.shape[0]
    pad = (-n) % 1024
    return jnp.pad(flat, (0, pad)).reshape(-1, 128)


def _unpack(packed, shapes):
    flat = packed.reshape(-1)
    out, off = [], 0
    for s in shapes:
        n = math.prod(s)
        out.append(flat[off:off + n].reshape(s))
        off += n
    return out


def _block_diag(w):
    h, d, _ = w.shape
    eye = jnp.eye(h, dtype=w.dtype)
    return (eye[:, None, :, None] * w[:, :, None, :]).reshape(h * d, h * d)


def _block_diag_grad(g, h):
    d = g.shape[0] // h
    g4 = g.reshape(h, d, h, d)
    return jnp.stack([g4[k, :, k, :] for k in range(h)])


def _layer_params(wt, l):
    gp = jnp.pad(jnp.stack([wt["gdn_a_log"][l], wt["gdn_dt_bias"][l]]), ((0, 6), (4, 128 - 4 - GDN_HEADS)))
    d_ffh = wt["ffn_conv_w"].shape[-1] // 2
    return dict(
        pre_mix=wt["pre_mix_norm"][l][None], post_mix=wt["post_mix_norm"][l][None],
        pre_ffn=wt["pre_ffn_norm"][l][None], post_ffn=wt["post_ffn_norm"][l][None],
        gdn_cw=wt["gdn_conv_w"][l], gdn_gp=gp, gdn_nw=wt["gdn_norm_w"][l][None],
        lru_cw=wt["lru_conv_w"][l], lru_cb=wt["lru_conv_b"][l][None],
        lru_wa=_block_diag(wt["lru_wa"][l]), lru_ba=wt["lru_ba"][l].reshape(1, -1),
        lru_wx=_block_diag(wt["lru_wx"][l]), lru_bx=wt["lru_bx"][l].reshape(1, -1),
        lru_lam=wt["lru_lambda"][l][None], gw0=wt["grp_norm_w"][l, 0][None], gw1=wt["grp_norm_w"][l, 1][None],
        gw2=wt["grp_norm_w"][l, 2][None],
        sgu_lnw=wt["sgu_ln_w"][l][None], sgu_lnb=wt["sgu_ln_b"][l][None],
        sgu_ws=wt["sgu_ws"][l].reshape(SGU_GROUPS * SGU_CHUNK, SGU_CHUNK),
        sgu_bt=jnp.pad(wt["sgu_b"][l].T, ((0, 0), (0, 128 - SGU_GROUPS))),
        sc_cw=wt["sconv_w"][l],
        ffn_cw=wt["ffn_conv_w"][l], ffn_cb=wt["ffn_conv_b"][l][None], d_ffh=d_ffh,
    )


TS_ROW = 256
TS_GDN = 256
TS_FFN = 256
TC_FFN = 512


def _mixers_fwd(l, p, lp, n, side):
    qkv = _rin(p, 3 * D_G, 0, halo=True)
    z = _rin(p, D_G, 3)
    ba = _rin(p, 128, BA_COL // 128)
    gdn_ps = [_par(lp["gdn_cw"]), _par(lp["gdn_gp"]), _par(lp["gdn_nw"])]
    res = seq_fwd(f"gdn_fwd_{l}", fn_gdn, n, TS_GDN, [qkv, z, ba], gdn_ps, [_out(D_G, BF16)],
                  state_shapes=[(GDN_HEADS * GDN_DIM, GDN_DIM)], side=side)
    (y_a,), (gdn_st,), side_res = res if side else res + ([],)
    lru_x = _rin(p, D_G, 4, halo=True)
    lru_gate = _rin(p, D_G, 5)
    lru_ps = [_par(lp[k]) for k in ("lru_cw", "lru_cb", "lru_wa", "lru_ba", "lru_wx", "lru_bx", "lru_lam", "gw0")]
    (y_b,), (lru_st,) = seq_fwd(f"lru_fwd_{l}", fn_lru, n, TS_ROW, [lru_x, lru_gate], lru_ps, [_out(D_G, BF16)],
                                state_shapes=[(1, D_G)])
    uv = _rin(p, 2 * D_G, 3)
    sgu_ps = [_par(lp[k]) for k in ("sgu_lnw", "sgu_lnb", "sgu_ws", "sgu_bt", "gw1")]
    (y_c,), _ = seq_fwd(f"sgu_fwd_{l}", fn_sgu, n, TS_ROW, [uv], sgu_ps, [_out(D_G, BF16)])
    sc = [_rin(p, D_G, 8), _rin(p, D_G, 9, halo=True), _rin(p, D_G, 10, halo=True)]
    sc_ps = [_par(lp["sc_cw"]), _par(lp["gw2"])]
    (y_d,), _ = seq_fwd(f"sconv_fwd_{l}", fn_sconv, n, TS_ROW, sc, sc_ps, [_out(D_G, BF16)])
    ins = dict(gdn=([qkv, z, ba], gdn_ps, [gdn_st]), lru=([lru_x, lru_gate], lru_ps, [lru_st]),
               sgu=([uv], sgu_ps, []), sc=(sc, sc_ps, []))
    return jnp.concatenate([y_a, y_b, y_c, y_d], axis=1), ins, side_res


def _mixers_bwd(l, dymix, ins, n, side):
    cot = lambda g: [_rin(dymix, D_G, g)]
    xs, ps, st = ins["gdn"]
    res = seq_bwd(f"gdn_bwd_{l}", fn_gdn, n, TS_GDN, xs, ps, cot(0), st, [BF16, BF16, BF16], side=side)
    (dqkv, dz, dba), g_gdn, side_res = res if side else res + ([],)
    xs, ps, st = ins["lru"]
    (dlx, dlg), g_lru = seq_bwd(f"lru_bwd_{l}", fn_lru, n, TS_ROW, xs, ps, cot(1), st, [BF16, BF16])
    xs, ps, st = ins["sgu"]
    (duv,), g_sgu = seq_bwd(f"sgu_bwd_{l}", fn_sgu, n, TS_ROW, xs, ps, cot(2), st, [BF16])
    xs, ps, st = ins["sc"]
    (dsb, dsc, dsh), g_sc = seq_bwd(f"sconv_bwd_{l}", fn_sconv, n, TS_ROW, xs, ps, cot(3), st, [BF16, BF16, BF16])
    dp = jnp.concatenate([dqkv, dz, dlx, dlg, duv, dsb, dsc, dsh, dba], axis=1)
    return dp, dict(gdn=g_gdn, lru=g_lru, sgu=g_sgu, sc=g_sc), side_res


def _ffn_ops(hid, lp):
    d_ffh = lp["d_ffh"]
    off = d_ffh // TC_FFN
    xs = [_rin(hid, TC_FFN, lambda j: j, halo=True), _rin(hid, TC_FFN, lambda j: j + off, halo=True)]
    ps = [_par(lp["ffn_cw"], TC_FFN, lambda j: j), _par(lp["ffn_cw"], TC_FFN, lambda j: j + off),
          _par(lp["ffn_cb"], TC_FFN, lambda j: j), _par(lp["ffn_cb"], TC_FFN, lambda j: j + off)]
    return xs, ps, d_ffh


_FROM_BLOCKS = dict(
    w_in=lambda b: _regroup_w_in(b.transpose(1, 0, 2).reshape(b.shape[1], -1)),
    ffn_up=lambda b: b.transpose(1, 0, 2).reshape(b.shape[1], -1),
    w_out=lambda b: b.reshape(-1, b.shape[2]),
    ffn_down=lambda b: b.reshape(-1, b.shape[2]),
)
_TO_BLOCKS = dict(
    w_in=lambda g: _ungroup_w_in(g).reshape(g.shape[0], N_DEV, -1).transpose(1, 0, 2),
    ffn_up=lambda g: g.reshape(g.shape[0], N_DEV, -1).transpose(1, 0, 2),
    w_out=lambda g: g.reshape(N_DEV, -1, g.shape[1]),
    ffn_down=lambda g: g.reshape(N_DEV, -1, g.shape[1]),
)


class _Traffic:
    PARTS = dict(w_in=1, w_out=1, ffn_up=2, ffn_down=1)

    def __init__(self, whole=None, shards=None):
        self.whole = dict(whole or {})
        self.shards = shards
        self.gathered = {}
        self.pending = {}
        self.landed = {}

    def _rows(self, key):
        name, l, part = key
        rows = self.shards[name].shape[1] // self.PARTS[name]
        return slice(part * rows, (part + 1) * rows)

    def jobs(self, gather=(), exchange=()):
        if self.shards is None:
            return [], []
        keys = [("gather", k) for k in gather if k not in self.gathered and k[:2] not in self.whole]
        keys += [("exchange", k) for k in exchange if k in self.pending]
        jobs = [(kind, self.shards[k[0]][k[1]][self._rows(k)] if kind == "gather" else self.pending[k])
                for kind, k in keys]
        return jobs, keys

    def done(self, keys, results):
        for (kind, k), r in zip(keys, results):
            if kind == "gather":
                self.gathered[k] = r
            else:
                self.landed[k] = r
                del self.pending[k]

    def weight(self, name, l):
        if (name, l) not in self.whole:
            parts = []
            for part in range(self.PARTS[name]):
                k = (name, l, part)
                if k not in self.gathered:
                    self.gathered[k] = all_gather(f"gather_{name}_{l}_{part}", self.shards[name][l][self._rows(k)], False)
                parts.append(self.gathered[k])
            blocks = parts[0] if len(parts) == 1 else jnp.concatenate(parts, axis=1)
            self.whole[(name, l)] = _FROM_BLOCKS[name](blocks)
        return self.whole[(name, l)]

    def grad(self, name, l, g):
        if self.shards is None:
            self.landed[(name, l)] = g
            return
        blocks = _TO_BLOCKS[name](g)
        for part in range(self.PARTS[name]):
            k = (name, l, part)
            self.pending[k] = blocks[:, self._rows(k)]

    def flush(self):
        for (name, l, part), blocks in list(self.pending.items()):
            self.landed[(name, l, part)] = all_to_all(f"exchange_{name}_{l}_{part}", blocks)
            del self.pending[(name, l, part)]

    def landed_blocks(self, name, depth):
        return jnp.concatenate([self.landed[(name, l, part)] for l in range(depth) for part in range(self.PARTS[name])],
                               axis=1)


def local_step(x, target, wt, tr):
    n, d = x.shape
    depth = wt["pre_mix_norm"].shape[0]
    lps = [_layer_params(wt, l) for l in range(depth)]
    saved = []
    xin = x

    def mm(name, a, b, mode, dtype, tm, tn, tk, gather=(), exchange=()):
        jobs, keys = tr.jobs(gather, exchange)
        if not jobs:
            return matmul(name, a, b, mode, dtype, tm, tn, tk)
        out, res = matmul(name, a, b, mode, dtype, tm, tn, tk, side=jobs)
        tr.done(keys, res)
        return out

    (h,), _ = seq_fwd("norm_fwd", fn_norm, n, TS_ROW, [_rin(x)], [_par(lps[0]["pre_mix"])], [_out(d, BF16)])
    dx_last = loss = None
    for l in range(depth):
        lp = lps[l]
        p = mm(f"w_in_fwd_{l}", h, tr.weight("w_in", l), "nn", F32, 1024, 640, d,
               gather=[("ffn_up", l, 0), ("w_out", l, 0)])
        jobs, keys = tr.jobs(gather=[("ffn_up", l, 1)])
        ymix, mix_ins, res = _mixers_fwd(l, p, lp, n, jobs)
        tr.done(keys, res)
        y = mm(f"w_out_fwd_{l}", ymix, tr.weight("w_out", l), "nn", F32, 1024, 1024, d)
        res_ps = [_par(lp["post_mix"]), _par(lp["pre_ffn"])]
        (x1, h2), _ = seq_fwd(f"res_mix_fwd_{l}", fn_res, n, TS_ROW, [_rin(xin), _rin(y)], res_ps,
                              [_out(d, F32), _out(d, BF16)])
        nxt = l + 1 < depth
        hid = mm(f"ffn_up_fwd_{l}", h2, tr.weight("ffn_up", l), "nn", F32, 1024, 1024, d,
                 gather=[("ffn_down", l, 0)] + ([("w_out", l + 1, 0)] if nxt else []))
        f_xs, f_ps, d_ffh = _ffn_ops(hid, lp)
        (act,), _ = seq_fwd(f"ffn_act_fwd_{l}", fn_ffn, n, TS_FFN, f_xs, f_ps,
                            [_out(TC_FFN, BF16, d_ffh, lambda j: j)], ncol=d_ffh // TC_FFN)
        yf = mm(f"ffn_down_fwd_{l}", act, tr.weight("ffn_down", l), "nn", F32, 1024, 1024, d_ffh // 4,
                gather=[("w_in", l + 1, 0)] if nxt else [])
        rec = dict(x=xin, h=h, mix_ins=mix_ins, ymix=ymix, y=y, x1=x1, h2=h2, f_xs=f_xs, f_ps=f_ps, act=act, yf=yf)
        if l + 1 < depth:
            ps = [_par(lp["post_ffn"]), _par(lps[l + 1]["pre_mix"])]
            (x2, h), _ = seq_fwd(f"res_ffn_fwd_{l}", fn_res, n, TS_ROW, [_rin(x1), _rin(yf)], ps,
                                 [_out(d, F32), _out(d, BF16)])
            rec["res_ffn_ps"] = ps
            xin = x2
        else:
            def body(xs, blks, ps, carries, r):
                x1_, yf_, t_ = xs
                e = x1_ + _rms(yf_, ps[0]) - t_
                part = 0.5 * jnp.sum(jnp.mean(e * e, axis=-1, keepdims=True), axis=0, keepdims=True)
                return [e * (1.0 / d)], [], [jnp.broadcast_to(part, (8, 128))], []

            ps = [_par(lp["post_ffn"])]
            dx_last, loss = _block_call("loss_fwd", body, n, TS_ROW, 1, False, [_rin(x1), _rin(yf), _rin(target)],
                                        [], ps, [_out(d, F32)], [], [dict(shape=(8, 128), total=None, col=None)], [])
            rec["res_ffn_ps"] = ps
        saved.append(rec)

    grads = {}
    dx2, dh_next = dx_last, None
    for l in reversed(range(depth)):
        rec, lp = saved[l], lps[l]
        d_ffh = lp["d_ffh"]
        g = {}
        if dh_next is None:
            (dx1, dyf), (g["post_ffn"],) = seq_bwd(f"res_ffn_bwd_{l}", fn_res_last, n, TS_ROW,
                                                   [_rin(rec["x1"]), _rin(rec["yf"])], rec["res_ffn_ps"], [_rin(dx2)],
                                                   din_dtypes=[F32, BF16])
        else:
            (dx1, dyf), (g["post_ffn"], g_next_pre) = seq_bwd(
                f"res_ffn_bwd_{l}", fn_res, n, TS_ROW, [_rin(rec["x1"]), _rin(rec["yf"])], rec["res_ffn_ps"],
                [_rin(dx2), _rin(dh_next)], din_dtypes=[F32, BF16])
            grads[l + 1]["pre_mix"] = g_next_pre
        dact = mm(f"ffn_down_dx_{l}", dyf, tr.weight("ffn_down", l), "nt", BF16, 1024, 512, d)
        tr.grad("ffn_down", l, mm(f"ffn_down_dw_{l}", rec["act"], dyf, "tn", BF16, 512, 1024, 1024))
        off = d_ffh // TC_FFN
        (dhg, dhv), (g_cwg, g_cwv, g_cbg, g_cbv) = seq_bwd(
            f"ffn_act_bwd_{l}", fn_ffn, n, TS_FFN, rec["f_xs"], rec["f_ps"], [_rin(dact, TC_FFN, lambda j: j)],
            din_dtypes=[BF16, BF16], ncol=off, din_specs=[(d_ffh, lambda j: j), (d_ffh, lambda j: j)])
        dhid = jnp.concatenate([dhg, dhv], axis=1)
        g["ffn_cw"] = jnp.concatenate([g_cwg[:, :d_ffh], g_cwv[:, d_ffh:]], axis=1)
        g["ffn_cb"] = jnp.concatenate([g_cbg[:, :d_ffh], g_cbv[:, d_ffh:]], axis=1)
        dh2 = mm(f"ffn_up_dx_{l}", dhid, tr.weight("ffn_up", l), "nt", BF16, 1024, 1024, 1024,
                 exchange=[("ffn_down", l, 0)])
        tr.grad("ffn_up", l, mm(f"ffn_up_dw_{l}", rec["h2"], dhid, "tn", BF16, 1024, 1024, 1024))
        (dx, dy), (g["post_mix"], g["pre_ffn"]) = seq_bwd(
            f"res_mix_bwd_{l}", fn_res, n, TS_ROW, [_rin(rec["x"]), _rin(rec["y"])],
            [_par(lp["post_mix"]), _par(lp["pre_ffn"])], [_rin(dx1), _rin(dh2)], din_dtypes=[F32, BF16])
        dymix = mm(f"w_out_dx_{l}", dy, tr.weight("w_out", l), "nt", BF16, 1024, 1024, d)
        tr.grad("w_out", l, mm(f"w_out_dw_{l}", rec["ymix"], dy, "tn", BF16, 1024, 1024, 1024))
        jobs, keys = tr.jobs(exchange=[("ffn_up", l, 0), ("w_out", l, 0)])
        dp, g["mix"], res = _mixers_bwd(l, dymix, rec["mix_ins"], n, jobs)
        tr.done(keys, res)
        tr.grad("w_in", l, mm(f"w_in_dw_{l}", rec["h"], dp, "tn", BF16, 1024, 640, 1024,
                              exchange=[("ffn_up", l, 1)]))
        dh = mm(f"w_in_dx_{l}", dp, tr.weight("w_in", l), "nt", BF16, 1024, 1024, N_INP // 5,
                exchange=[("w_in", l, 0)])
        grads[l] = g
        dx2, dh_next = dx, dh
    (grad_x,), (g_pre0,) = seq_bwd("norm_bwd", fn_norm_keep, n, TS_ROW, [_rin(x)], [_par(lps[0]["pre_mix"])],
                                   [_rin(dh_next), _rin(dx2)])
    grads[0]["pre_mix"] = g_pre0
    tr.flush()
    return loss[0, 0], grad_x, _name_grads(grads, depth)


def _name_grads(grads, depth):
    per = {k: [] for k in SMALL}
    for l in range(depth):
        g = grads[l]
        m = g["mix"]
        cw, gp, nw = m["gdn"]
        lcw, lcb, lwa, lba, lwx, lbx, llam, gw0 = m["lru"]
        lnw, lnb, ws, bst, gw1 = m["sgu"]
        scw, gw2 = m["sc"]
        per["pre_mix_norm"].append(g["pre_mix"][0])
        per["gdn_conv_w"].append(cw)
        per["gdn_a_log"].append(gp[0, 4:8])
        per["gdn_dt_bias"].append(gp[1, 4:8])
        per["gdn_norm_w"].append(nw[0])
        per["lru_conv_w"].append(lcw)
        per["lru_conv_b"].append(lcb[0])
        per["lru_wa"].append(_block_diag_grad(lwa, LRU_BLOCKS))
        per["lru_ba"].append(lba.reshape(LRU_BLOCKS, -1))
        per["lru_wx"].append(_block_diag_grad(lwx, LRU_BLOCKS))
        per["lru_bx"].append(lbx.reshape(LRU_BLOCKS, -1))
        per["lru_lambda"].append(llam[0])
        per["sgu_ln_w"].append(lnw[0])
        per["sgu_ln_b"].append(lnb[0])
        per["sgu_ws"].append(ws.reshape(SGU_GROUPS, SGU_CHUNK, SGU_CHUNK))
        per["sgu_b"].append(bst[:, :SGU_GROUPS].T)
        per["sconv_w"].append(scw)
        per["grp_norm_w"].append(jnp.concatenate([gw0, gw1, gw2], axis=0))
        per["post_mix_norm"].append(g["post_mix"][0])
        per["pre_ffn_norm"].append(g["pre_ffn"][0])
        per["ffn_conv_w"].append(g["ffn_cw"])
        per["ffn_conv_b"].append(g["ffn_cb"][0])
        per["post_ffn_norm"].append(g["post_ffn"][0])
    return {k: jnp.stack(v) for k, v in per.items()}


def _regroup_w_in(w):
    pad = jnp.zeros(w.shape[:-1] + (N_INP - N_IN,), w.dtype)
    return jnp.concatenate([w[..., :2048], w[..., 2056:], w[..., 2048:2056], pad], axis=-1)


def _ungroup_w_in(g):
    return jnp.concatenate([g[..., :2048], g[..., BA_COL:BA_COL + 8], g[..., 2048:BA_COL]], axis=-1)


def kernel(x, pre_mix_norm, w_in, gdn_conv_w, gdn_a_log, gdn_dt_bias, gdn_norm_w, lru_conv_w, lru_conv_b, lru_wa, lru_ba, lru_wx, lru_bx, lru_lambda, sgu_ln_w, sgu_ln_b, sgu_ws, sgu_b, sconv_w, grp_norm_w, w_out, post_mix_norm, pre_ffn_norm, ffn_up, ffn_conv_w, ffn_conv_b, ffn_down, post_ffn_norm, loss_target, m_pre_mix_norm, m_w_in, m_gdn_conv_w, m_gdn_a_log, m_gdn_dt_bias, m_gdn_norm_w, m_lru_conv_w, m_lru_conv_b, m_lru_wa, m_lru_ba, m_lru_wx, m_lru_bx, m_lru_lambda, m_sgu_ln_w, m_sgu_ln_b, m_sgu_ws, m_sgu_b, m_sconv_w, m_grp_norm_w, m_w_out, m_post_mix_norm, m_pre_ffn_norm, m_ffn_up, m_ffn_conv_w, m_ffn_conv_b, m_ffn_down, m_post_ffn_norm, v_pre_mix_norm, v_w_in, v_gdn_conv_w, v_gdn_a_log, v_gdn_dt_bias, v_gdn_norm_w, v_lru_conv_w, v_lru_conv_b, v_lru_wa, v_lru_ba, v_lru_wx, v_lru_bx, v_lru_lambda, v_sgu_ln_w, v_sgu_ln_b, v_sgu_ws, v_sgu_b, v_sconv_w, v_grp_norm_w, v_w_out, v_post_mix_norm, v_pre_ffn_norm, v_ffn_up, v_ffn_conv_w, v_ffn_conv_b, v_ffn_down, v_post_ffn_norm):
    args = locals()
    w_loc = {k: args[k] for k in WEIGHTS}
    m_loc = {k: args["m_" + k] for k in WEIGHTS}
    v_loc = {k: args["v_" + k] for k in WEIGHTS}
    depth = pre_mix_norm.shape[0]
    x_, y_, c_ = _my_pos()
    me = 4 * x_ + 2 * y_ + c_

    shards = {}
    for name in BIG:
        l, r, c = w_loc[name].shape
        wb = cast_bf16(f"cast_{name}", w_loc[name].reshape(l * r, c), 256 if (l * r) % 256 == 0 else 64)
        shards[name] = wb.reshape(l, r, c)
    tr = _Traffic(shards=shards)
    wt = {k: w_loc[k] for k in SMALL}
    shard_shapes = [w_loc[k].shape for k in SHARDED_SMALL]
    gathered = all_gather("gather_small", _pack([w_loc[k] for k in SHARDED_SMALL]), True)
    per_dev = [_unpack(gathered[s], shard_shapes) for s in range(N_DEV)]
    for k_i, k in enumerate(SHARDED_SMALL):
        wt[k] = jnp.concatenate([per_dev[s][k_i] for s in range(N_DEV)], axis=-1)

    loss_part, grad_x, g_full = local_step(x[0], loss_target[0], wt, tr)
    loss = lax.psum(loss_part, ("x", "y", "c"))

    outs_g, outs_d, outs_m, outs_v = {}, {}, {}, {}
    for name in BIG:
        l, r, c = w_loc[name].shape
        land = tr.landed_blocks(name, l)
        res = adamw_big(f"adamw_{name}", w_loc[name].reshape(l * r, c), land, m_loc[name].reshape(l * r, c),
                        v_loc[name].reshape(l * r, c), 128 if (l * r) % 128 == 0 else 64)
        outs_g[name], outs_d[name], outs_m[name], outs_v[name] = [a.reshape(l, r, c) for a in res]

    full_shapes = [g_full[k].shape for k in SMALL]
    g_all = all_gather("gather_small_grads", _pack([g_full[k] for k in SMALL]), True)
    g_sum = _unpack(sum_blocks("sum_small_grads", g_all), full_shapes)
    g_small = {}
    for k, g in zip(SMALL, g_sum):
        if k in SHARDED_SMALL:
            w = w_loc[k].shape[-1]
            g = lax.dynamic_slice_in_dim(g, me * w, w, axis=g.ndim - 1)
        g_small[k] = g
    loc_shapes = [w_loc[k].shape for k in SMALL]
    res = adamw_small("adamw_small", _pack([w_loc[k] for k in SMALL]), _pack([g_small[k] for k in SMALL]),
                      _pack([m_loc[k] for k in SMALL]), _pack([v_loc[k] for k in SMALL]))
    d_s, m_s, v_s = [_unpack(a, loc_shapes) for a in res]
    for k_i, k in enumerate(SMALL):
        outs_g[k], outs_d[k], outs_m[k], outs_v[k] = g_small[k], d_s[k_i], m_s[k_i], v_s[k_i]

    return (loss, grad_x[None], *[outs_g[k] for k in WEIGHTS], *[outs_d[k] for k in WEIGHTS],
            *[outs_m[k] for k in WEIGHTS], *[outs_v[k] for k in WEIGHTS])
```

```python
import functools
import math

import jax
import jax.numpy as jnp
from jax import lax
from jax.experimental import pallas as pl
from jax.experimental.pallas import tpu as pltpu

F32 = jnp.float32
BF16 = jnp.bfloat16
EPS = 1e-6
HALO = 8
VMEM_LIMIT = 56 * 1024 * 1024
MESH = pl.DeviceIdType.MESH
N_DEV = 8

ADAM_LR, ADAM_B1, ADAM_B2, ADAM_EPS, ADAM_WD, ADAM_STEP = 0.001, 0.9, 0.999, 1e-08, 0.01, 10

GDN_HEADS, GDN_DIM, GDN_CHUNK = 4, 128, 64
SGU_GROUPS, SGU_CHUNK = 4, 128
LRU_BLOCKS, LRU_C = 8, 8.0
D_G = 512
N_IN = 5640
N_INP = 5760
BA_COL = 5632

SHARDED_SMALL = ("gdn_conv_w", "lru_conv_w", "sconv_w", "grp_norm_w", "ffn_conv_w")
BIG = ("w_in", "w_out", "ffn_up", "ffn_down")
WEIGHTS = ("pre_mix_norm", "w_in", "gdn_conv_w", "gdn_a_log", "gdn_dt_bias", "gdn_norm_w", "lru_conv_w",
           "lru_conv_b", "lru_wa", "lru_ba", "lru_wx", "lru_bx", "lru_lambda", "sgu_ln_w", "sgu_ln_b", "sgu_ws",
           "sgu_b", "sconv_w", "grp_norm_w", "w_out", "post_mix_norm", "pre_ffn_norm", "ffn_up", "ffn_conv_w",
           "ffn_conv_b", "ffn_down", "post_ffn_norm")
SMALL = tuple(n for n in WEIGHTS if n not in BIG)


def _dot(a, b, ca, cb):
    return lax.dot_general(a.astype(BF16), b.astype(BF16), (((ca,), (cb,)), ((), ())),
                           preferred_element_type=F32)


@jax.custom_vjp
def _mm(a, b):
    return _dot(a, b, 1, 0)


def _mm_f(a, b):
    return _dot(a, b, 1, 0), (a, b)


def _mm_b(res, g):
    a, b = res
    return _dot(g, b, 1, 1), _dot(a, g, 0, 0)


_mm.defvjp(_mm_f, _mm_b)


@jax.custom_vjp
def _mm_nt(a, b):
    return _dot(a, b, 1, 1)


def _mm_nt_f(a, b):
    return _dot(a, b, 1, 1), (a, b)


def _mm_nt_b(res, g):
    a, b = res
    return _dot(g, b, 1, 0), _dot(g, a, 0, 0)


_mm_nt.defvjp(_mm_nt_f, _mm_nt_b)


@jax.custom_vjp
def _mm_tn(a, b):
    return _dot(a, b, 0, 0)


def _mm_tn_f(a, b):
    return _dot(a, b, 0, 0), (a, b)


def _mm_tn_b(res, g):
    a, b = res
    return _dot(b, g, 1, 1), _dot(a, g, 1, 0)


_mm_tn.defvjp(_mm_tn_f, _mm_tn_b)


def _dot_exact(a, b, ca, cb):
    return lax.dot_general(a, b, (((ca,), (cb,)), ((), ())), precision=lax.Precision.HIGHEST,
                           preferred_element_type=F32)


@functools.partial(jax.custom_vjp, nondiff_argnums=(1,))
def _shift_rows(x, s):
    return pltpu.roll(x, s, 0)


def _shift_rows_f(x, s):
    return pltpu.roll(x, s, 0), None


def _shift_rows_b(s, _, g):
    return (pltpu.roll(g, (g.shape[0] - s) % g.shape[0], 0),)


_shift_rows.defvjp(_shift_rows_f, _shift_rows_b)


def _sigmoid(x):
    return 1.0 / (1.0 + jnp.exp(-x))


def _silu(x):
    return x * _sigmoid(x)


def _gelu(x):
    return 0.5 * x * (1.0 + jnp.tanh(0.7978845608028654 * (x + 0.044715 * (x * x * x))))


@jax.custom_vjp
def _softplus(x):
    e = jnp.exp(-jnp.abs(x))
    u = 1.0 + e
    log1p = jnp.where(u == 1.0, e, jnp.log(u) * (e / jnp.where(u == 1.0, 1.0, u - 1.0)))
    return jnp.maximum(x, 0.0) + log1p


def _softplus_f(x):
    return _softplus(x), x


def _softplus_b(x, g):
    return (g * _sigmoid(x),)


_softplus.defvjp(_softplus_f, _softplus_b)


def _neg_expm1(y):
    return -jnp.tanh(0.5 * y) * (jnp.exp(y) + 1.0)


def _rms(x, w):
    return x * lax.rsqrt(jnp.mean(x * x, axis=-1, keepdims=True) + EPS) * w


def _row(w, k):
    sel = lax.broadcasted_iota(jnp.int32, w.shape, 0) == k
    return jnp.sum(jnp.where(sel, w, 0.0), axis=0, keepdims=True)


def _col(x, j):
    sel = lax.broadcasted_iota(jnp.int32, x.shape, 1) == j
    return jnp.sum(jnp.where(sel, x, 0.0), axis=1, keepdims=True)


def _conv(x_ext, w, taps):
    acc = None
    for k in range(taps):
        s = taps - 1 - k
        t = (x_ext if s == 0 else _shift_rows(x_ext, s)) * _row(w, k)
        acc = t if acc is None else acc + t
    return acc[HALO:]


@jax.custom_vjp
def _scan(a, b, h0):
    n = a.shape[0]
    row = lax.broadcasted_iota(jnp.int32, a.shape, 0)
    s = 1
    while s < n:
        keep = row >= s
        a_sh = jnp.where(keep, pltpu.roll(a, s, 0), 1.0)
        b_sh = jnp.where(keep, pltpu.roll(b, s, 0), 0.0)
        b = a * b_sh + b
        a = a * a_sh
        s *= 2
    return b + a * h0


def _scan_f(a, b, h0):
    h = _scan(a, b, h0)
    return h, (a, h, h0)


def _scan_b(res, dh):
    a, h, h0 = res
    n = a.shape[0]
    row = lax.broadcasted_iota(jnp.int32, a.shape, 0)
    an = jnp.where(row < n - 1, pltpu.roll(a, n - 1, 0), 0.0)
    lam = dh
    s = 1
    while s < n:
        keep = row < n - s
        a_sh = jnp.where(keep, pltpu.roll(an, n - s, 0), 1.0)
        l_sh = jnp.where(keep, pltpu.roll(lam, n - s, 0), 0.0)
        lam = an * l_sh + lam
        an = an * a_sh
        s *= 2
    h_prev = jnp.where(row >= 1, pltpu.roll(h, 1, 0), h0)
    al = a * lam
    dh0 = jnp.sum(jnp.where(row == 0, al, 0.0), axis=0, keepdims=True)
    return lam * h_prev, lam, dh0


_scan.defvjp(_scan_f, _scan_b)


@jax.custom_vjp
def _unit_lower_inverses(ms):
    n = ms[0].shape[0]
    shape = ms[0].shape
    eye = (lax.broadcasted_iota(jnp.int32, shape, 0) == lax.broadcasted_iota(jnp.int32, shape, 1)).astype(F32)
    p = [-m for m in ms]
    t = [eye + a for a in p]
    steps = 1
    while 2 ** steps < n:
        p = [_mm(a, a) for a in p]
        t = [a + _mm(a, c) for a, c in zip(t, p)]
        steps += 1
    return t


def _unit_lower_inverses_f(ms):
    t = _unit_lower_inverses(ms)
    return t, t


def _unit_lower_inverses_b(t, dt):
    x = [_mm_nt(g, a) for g, a in zip(dt, t)]
    return ([-_mm_tn(a, c) for a, c in zip(t, x)],)


_unit_lower_inverses.defvjp(_unit_lower_inverses_f, _unit_lower_inverses_b)


def _last_row(x):
    sel = lax.broadcasted_iota(jnp.int32, x.shape, 0) == x.shape[0] - 1
    return jnp.sum(jnp.where(sel, x, 0.0), axis=0, keepdims=True)


def fn_norm(xs, st, ps):
    (x,), (w,) = xs, ps
    return [_rms(x, w).astype(BF16)], []


def fn_norm_keep(xs, st, ps):
    (x,), (w,) = xs, ps
    return [_rms(x, w).astype(BF16), x], []


def fn_res(xs, st, ps):
    (x, y), (w_post, w_next) = xs, ps
    x1 = x + _rms(y, w_post)
    return [x1, _rms(x1, w_next).astype(BF16)], []


def fn_res_last(xs, st, ps):
    (x, y), (w_post,) = xs, ps
    return [x + _rms(y, w_post)], []


def fn_gdn(xs, st, ps):
    qkv_ext, z, ba = xs
    (state,) = st
    cw, gp, nw = ps
    ts = z.shape[0]
    qkv = _silu(_conv(qkv_ext, cw, 4))
    beta_all = _sigmoid(ba)
    g_all = -jnp.exp(_row(gp, 0)) * _softplus(ba + _row(gp, 1))
    c_n = GDN_CHUNK
    ri = lax.broadcasted_iota(jnp.int32, (c_n, c_n), 0)
    ci = lax.broadcasted_iota(jnp.int32, (c_n, c_n), 1)
    causal, strict = ri >= ci, ri > ci
    tril = causal.astype(F32)
    lane = lax.broadcasted_iota(jnp.int32, (c_n, 128), 1)
    s_h = [state[GDN_DIM * h:GDN_DIM * (h + 1)] for h in range(GDN_HEADS)]
    n_c = ts // c_n
    pairs = [(c, h) for c in range(n_c) for h in range(GDN_HEADS)]
    every = lambda f, *lists: [f(*a) for a in zip(*lists)]

    def piece(c, h, base):
        return qkv[c * c_n:(c + 1) * c_n, base + GDN_DIM * h:base + GDN_DIM * (h + 1)]

    q = [piece(c, h, 0) for c, h in pairs]
    k = [piece(c, h, D_G) for c, h in pairs]
    v = [piece(c, h, 2 * D_G) for c, h in pairs]
    q = every(lambda t: t * lax.rsqrt(jnp.sum(t * t, axis=-1, keepdims=True) + EPS) * (GDN_DIM ** -0.5), q)
    k = every(lambda t: t * lax.rsqrt(jnp.sum(t * t, axis=-1, keepdims=True) + EPS), k)
    gcum_all = [_dot_exact(tril, g_all[c * c_n:(c + 1) * c_n], 1, 0) for c in range(n_c)]
    b = [_col(beta_all[c * c_n:(c + 1) * c_n], h) for c, h in pairs]
    gc = [_col(gcum_all[c], 4 + h) for c, h in pairs]
    gr = [_dot_exact((lane == 4 + h).astype(F32), gcum_all[c], 1, 1) for c, h in pairs]
    decay = every(lambda a, r: jnp.where(causal, jnp.exp(jnp.where(causal, a - r, 0.0)), 0.0), gc, gr)
    kb = every(lambda a, c: a * c, k, b)
    mk = every(lambda a, c, e: _mm_nt(jnp.concatenate([a, c], axis=0), e), kb, q, k)
    m = every(lambda a, dcy: jnp.where(strict, a[:c_n] * dcy, 0.0), mk, decay)
    attn = every(lambda a, dcy: jnp.where(causal, a[c_n:] * dcy, 0.0), mk, decay)
    t_ = _unit_lower_inverses(m)
    eg = every(jnp.exp, gc)
    wu = every(lambda t, a, e, c, d: _mm(t, jnp.concatenate([a * e, c * d], axis=1)), t_, kb, eg, v, b)
    g_last = every(_last_row, gc)
    k_g = every(lambda a, gl, g: a * jnp.exp(gl - g), k, g_last, gc)
    wq = every(lambda a, c, e: jnp.concatenate([a[:, :GDN_DIM], c * e], axis=0), wu, q, eg)
    u = [a[:, GDN_DIM:] for a in wu]
    gl = every(jnp.exp, g_last)

    o = []
    for c in range(n_c):
        idx = range(c * GDN_HEADS, (c + 1) * GDN_HEADS)
        ws = [_mm(wq[i], s_h[h]) for h, i in enumerate(idx)]
        v_new = [u[i] - ws[h][:c_n] for h, i in enumerate(idx)]
        av = [_mm(attn[i], v_new[h]) for h, i in enumerate(idx)]
        kv = [_mm_tn(k_g[i], v_new[h]) for h, i in enumerate(idx)]
        o += [ws[h][c_n:] + av[h] for h in range(GDN_HEADS)]
        s_h = [s_h[h] * gl[i] + kv[h] for h, i in enumerate(idx)]
    zz = [z[c * c_n:(c + 1) * c_n, GDN_DIM * h:GDN_DIM * (h + 1)] for c, h in pairs]
    y = every(lambda a, g: a * lax.rsqrt(jnp.mean(a * a, axis=-1, keepdims=True) + EPS) * nw * _silu(g), o, zz)
    rows = [jnp.concatenate(y[c * GDN_HEADS:(c + 1) * GDN_HEADS], axis=1) for c in range(n_c)]
    y = rows[0] if n_c == 1 else jnp.concatenate(rows, axis=0)
    return [y.astype(BF16)], [jnp.concatenate(s_h, axis=0)]


def fn_lru(xs, st, ps):
    x_ext, gate = xs
    (h0,) = st
    cw, cb, wa, ba, wx, bx, lam, gw = ps
    xc = _conv(x_ext, cw, 4) + cb
    r = _sigmoid(_mm(xc, wa) + ba)
    i = _sigmoid(_mm(xc, wx) + bx)
    log_a = -LRU_C * r * _softplus(-lam)
    a = jnp.exp(log_a)
    mult = jnp.sqrt(_neg_expm1(2.0 * log_a))
    h = _scan(a, mult * (i * xc), h0)
    y = _rms(h * _gelu(gate), gw)
    return [y.astype(BF16)], [_last_row(h)]


def fn_sgu(xs, st, ps):
    (uv,) = xs
    lnw, lnb, ws, bst, gw = ps
    ts = uv.shape[0]
    uvf = _gelu(uv)
    u, v = uvf[:, :D_G], uvf[:, D_G:]
    vc = v - jnp.mean(v, axis=-1, keepdims=True)
    v = vc * lax.rsqrt(jnp.mean(vc * vc, axis=-1, keepdims=True) + EPS) * lnw + lnb
    t_n = SGU_CHUNK
    tril = lax.broadcasted_iota(jnp.int32, (t_n, t_n), 0) >= lax.broadcasted_iota(jnp.int32, (t_n, t_n), 1)
    wg = [jnp.where(tril, ws[t_n * g:t_n * (g + 1)], 0.0) for g in range(SGU_GROUPS)]
    bg = [_col(bst, g) for g in range(SGU_GROUPS)]
    rows = []
    for c in range(ts // t_n):
        vcg = v[c * t_n:(c + 1) * t_n]
        rows.append(jnp.concatenate(
            [_mm(wg[g], vcg[:, 128 * g:128 * (g + 1)]) + bg[g] for g in range(SGU_GROUPS)], axis=1))
    vv = rows[0] if len(rows) == 1 else jnp.concatenate(rows, axis=0)
    return [_rms(u * vv, gw).astype(BF16)], []


def fn_sconv(xs, st, ps):
    bg, cg_ext, hh_ext = xs
    cw, gw = ps
    return [_rms(bg * _conv(cg_ext * hh_ext, cw, 3), gw).astype(BF16)], []


def fn_ffn(xs, st, ps):
    g_ext, v_ext = xs
    cwg, cwv, cbg, cbv = ps
    g = _conv(g_ext, cwg, 3) + cbg
    v = _conv(v_ext, cwv, 3) + cbv
    return [(_gelu(g) * v).astype(BF16)], []


def _my_pos():
    return lax.axis_index("x"), lax.axis_index("y"), lax.axis_index("c")


def _peer(pos, k):
    x_, y_, c_ = pos
    return (1 - x_ if (k >> 2) & 1 else x_, 1 - y_ if (k >> 1) & 1 else y_, 1 - c_ if k & 1 else c_)


def _dev_index(p):
    return 4 * p[0] + 2 * p[1] + p[2]


class _Side:
    def __init__(self, jobs):
        self.jobs = list(jobs)
        n = len(self.jobs)
        self.operands = [a for _, a in self.jobs]
        self.in_specs = [pl.BlockSpec(memory_space=pl.ANY)] * n
        self.out_shape = [jax.ShapeDtypeStruct(((N_DEV,) + a.shape) if kind == "gather" else a.shape, a.dtype)
                          for kind, a in self.jobs]
        self.out_specs = [pl.BlockSpec(memory_space=pl.ANY)] * n
        self.scratch = [pltpu.SemaphoreType.DMA((7 * n,)), pltpu.SemaphoreType.DMA((7 * n,)),
                        pltpu.SemaphoreType.DMA((n,))] if n else []

    def _copies(self, in_refs, out_refs, sems, landings=True):
        send, recv, local = sems
        pos = _my_pos()
        me = _dev_index(pos)
        mine, outgoing, landing = [], [], []
        for j, (kind, _) in enumerate(self.jobs):
            src, dst = in_refs[j], out_refs[j]
            own = src if kind == "gather" else src.at[me]
            mine.append(pltpu.make_async_copy(own, dst.at[me], local.at[j]))
            for k in range(1, N_DEV):
                p = _peer(pos, k)
                sems_k = dict(send_sem=send.at[7 * j + k - 1], recv_sem=recv.at[7 * j + k - 1], device_id=p,
                              device_id_type=MESH)
                outgoing.append(pltpu.make_async_remote_copy(
                    src_ref=src if kind == "gather" else src.at[_dev_index(p)], dst_ref=dst.at[me], **sems_k))
                if landings:
                    landing.append(pltpu.make_async_remote_copy(src_ref=own, dst_ref=dst.at[_dev_index(p)], **sems_k))
        return mine, outgoing, landing

    def start(self, in_refs, out_refs, sems):
        mine, outgoing, _ = self._copies(in_refs, out_refs, sems, landings=False)
        for cp in mine + outgoing:
            cp.start()

    def wait(self, in_refs, out_refs, sems):
        mine, outgoing, landing = self._copies(in_refs, out_refs, sems)
        for cp in landing:
            cp.wait_recv()
        for cp in outgoing:
            cp.wait_send()
        for cp in mine:
            cp.wait()


def _rin(arr, w=None, col=0, halo=False):
    return dict(arr=arr, w=arr.shape[1] if w is None else w, col=col, halo=halo)


def _par(arr, w=None, col=None):
    return dict(arr=arr, w=w, col=col)


def _colidx(col, j):
    return col(j) if callable(col) else col


def _block_call(name, body, n_rows, ts, ncol, reverse, row_ins, blk_ins, params, row_outs, blk_outs, acc_outs,
                carries, side=()):
    side = _Side(side)
    nblk = n_rows // ts
    hb = ts // HALO

    def rr(i):
        return (nblk - 1 - i) if reverse else i

    in_specs, operands = [], []
    for s in row_ins:
        in_specs.append(pl.BlockSpec((ts, s["w"]), lambda j, i, s=s: (rr(i), _colidx(s["col"], j))))
        operands.append(s["arr"])
        if s["halo"]:
            in_specs.append(pl.BlockSpec((HALO, s["w"]),
                                         lambda j, i, s=s: (jnp.maximum(rr(i) * hb - 1, 0), _colidx(s["col"], j))))
            operands.append(s["arr"])
    for a in blk_ins:
        nd = a.ndim - 1
        in_specs.append(pl.BlockSpec((None,) + a.shape[1:], lambda j, i, nd=nd: (rr(i),) + (0,) * nd))
        operands.append(a)
    for p in params:
        a = p["arr"]
        if p["col"] is None:
            in_specs.append(pl.BlockSpec(a.shape, lambda j, i: (0, 0)))
        else:
            in_specs.append(pl.BlockSpec((a.shape[0], p["w"]), lambda j, i, p=p: (0, _colidx(p["col"], j))))
        operands.append(a)

    out_specs, out_shape = [], []
    for o in row_outs:
        out_specs.append(pl.BlockSpec((ts, o["w"]), lambda j, i, o=o: (rr(i), _colidx(o["col"], j))))
        out_shape.append(jax.ShapeDtypeStruct((n_rows, o["total"]), o["dtype"]))
    for o in blk_outs:
        nd = len(o["shape"])
        out_specs.append(pl.BlockSpec((None,) + tuple(o["shape"]), lambda j, i, nd=nd: (rr(i),) + (0,) * nd))
        out_shape.append(jax.ShapeDtypeStruct((nblk,) + tuple(o["shape"]), o["dtype"]))
    for o in acc_outs:
        if o["col"] is None:
            out_specs.append(pl.BlockSpec(o["shape"], lambda j, i: (0, 0)))
            out_shape.append(jax.ShapeDtypeStruct(o["shape"], F32))
        else:
            out_specs.append(pl.BlockSpec(o["shape"], lambda j, i, o=o: (0, _colidx(o["col"], j))))
            out_shape.append(jax.ShapeDtypeStruct((o["shape"][0], o["total"]), F32))

    n_in = len(operands)
    n_row_out, n_blk_out, n_acc = len(row_outs), len(blk_outs), len(acc_outs)
    n_out = n_row_out + n_blk_out + n_acc
    n_side = len(side.jobs)

    def kern(*refs):
        in_refs = refs[:n_in]
        side_in = refs[n_in:n_in + n_side]
        out_refs = refs[n_in + n_side:n_in + n_side + n_out]
        side_out = refs[n_in + n_side + n_out:n_in + 2 * n_side + n_out]
        scratch = refs[n_in + 2 * n_side + n_out:]
        carry_refs, side_sems = scratch[:len(carries)], scratch[len(carries):]
        acc_refs = out_refs[n_row_out + n_blk_out:]
        i = pl.program_id(1)
        r = rr(i)
        if n_side:
            @pl.when((pl.program_id(0) == 0) & (i == 0))
            def _():
                side.start(side_in, side_out, side_sems)

        @pl.when(i == 0)
        def _():
            for c_ref in carry_refs:
                c_ref[...] = jnp.zeros(c_ref.shape, c_ref.dtype)
            for a_ref in acc_refs:
                a_ref[...] = jnp.zeros(a_ref.shape, a_ref.dtype)

        k = 0
        xs = []
        for s in row_ins:
            x = in_refs[k][...]
            k += 1
            if s["halo"]:
                hal = in_refs[k][...]
                k += 1
                hal = jnp.where(r == 0, jnp.zeros_like(hal), hal)
                x = jnp.concatenate([hal, x], axis=0)
            xs.append(x)
        blks = []
        for _ in blk_ins:
            blks.append(in_refs[k][...])
            k += 1
        ps = []
        for _ in params:
            ps.append(in_refs[k][...])
            k += 1
        row_vals, blk_vals, acc_vals, new_carries = body(xs, blks, ps, [c[...] for c in carry_refs], r)
        for ref, val in zip(out_refs[:n_row_out], row_vals):
            ref[...] = val.astype(ref.dtype)
        for ref, val in zip(out_refs[n_row_out:n_row_out + n_blk_out], blk_vals):
            ref[...] = val.astype(ref.dtype)
        for ref, val in zip(acc_refs, acc_vals):
            ref[...] += val
        for ref, val in zip(carry_refs, new_carries):
            ref[...] = val
        if n_side:
            @pl.when((pl.program_id(0) == ncol - 1) & (i == nblk - 1))
            def _():
                side.wait(side_in, side_out, side_sems)

    res = pl.pallas_call(
        kern,
        name=name,
        grid=(ncol, nblk),
        in_specs=in_specs + side.in_specs,
        out_specs=out_specs + side.out_specs,
        out_shape=out_shape + side.out_shape,
        scratch_shapes=[pltpu.VMEM(shape, F32) for shape in carries] + side.scratch,
        compiler_params=pltpu.CompilerParams(dimension_semantics=("arbitrary", "arbitrary"),
                                             vmem_limit_bytes=VMEM_LIMIT),
    )(*operands, *side.operands)
    return list(res)


def _out(w, dtype, total=None, col=0):
    return dict(w=w, dtype=dtype, total=w if total is None else total, col=col)


def seq_fwd(name, fn, n_rows, ts, row_ins, params, outs, state_shapes=(), ncol=1, side=()):
    def body(xs, blks, ps, carries, r):
        o, new_st = fn(xs, list(carries), ps)
        return o, list(carries), [], new_st

    res = _block_call(name, body, n_rows, ts, ncol, False, row_ins, [], params, outs,
                      [dict(shape=s, dtype=F32) for s in state_shapes], [], list(state_shapes), side)
    n_o, n_s = len(outs), len(state_shapes)
    return (res[:n_o], res[n_o:n_o + n_s]) + ((res[n_o + n_s:],) if side else ())


def seq_bwd(name, fn, n_rows, ts, row_ins, params, cots, saved_states=(), din_dtypes=None, ncol=1, din_specs=None,
            side=()):
    n_x, n_p, n_st = len(row_ins), len(params), len(saved_states)
    halo_idx = [k for k, s in enumerate(row_ins) if s["halo"]]
    state_shapes = [a.shape[1:] for a in saved_states]

    def body(xs_all, blks, ps, carries, r):
        xs, cot_vals = xs_all[:n_x], xs_all[n_x:]
        d_state, d_halo = carries[:n_st], carries[n_st:]
        (o, _), vjp = jax.vjp(lambda a, b, c: fn(a, b, c), xs, blks, ps)
        cot = [c.astype(v.dtype) for c, v in zip(cot_vals, o)]
        dxs, dst, dps = vjp((cot, list(d_state)))
        row_vals, new_halo = [], []
        for k, dx in enumerate(dxs):
            if k in halo_idx:
                hk = halo_idx.index(k)
                body_rows = dx[HALO:]
                tail = dx[ts:ts + HALO] + d_halo[hk]
                row_vals.append(jnp.concatenate([body_rows[:ts - HALO], tail], axis=0))
                new_halo.append(dx[:HALO])
            else:
                row_vals.append(dx)
        return row_vals, [], list(dps), list(dst) + new_halo

    din_dtypes = din_dtypes or [F32] * n_x
    douts = []
    for k, s in enumerate(row_ins):
        total, col = (s["w"], 0) if din_specs is None or din_specs[k] is None else din_specs[k]
        douts.append(_out(s["w"], din_dtypes[k], total, col))
    accs = []
    for p in params:
        a = p["arr"]
        if p["col"] is None:
            accs.append(dict(shape=a.shape, total=None, col=None))
        else:
            accs.append(dict(shape=(a.shape[0], p["w"]), total=a.shape[1], col=p["col"]))
    carries = list(state_shapes) + [(HALO, row_ins[k]["w"]) for k in halo_idx]
    res = _block_call(name, body, n_rows, ts, ncol, True, list(row_ins) + list(cots), list(saved_states), params,
                      douts, [], accs, carries, side)
    return (res[:n_x], res[n_x:n_x + n_p]) + ((res[n_x + n_p:],) if side else ())


def matmul(name, a, b, mode, out_dtype, tm, tn, tk, side=()):
    side = _Side(side)
    n_side = len(side.jobs)
    if mode == "tn":
        (kk, m), n = a.shape, b.shape[1]
    else:
        (m, kk), n = a.shape, (b.shape[0] if mode == "nt" else b.shape[1])
    tm, tn, tk = min(tm, m), min(tn, n), min(tk, kk)
    nk = kk // tk
    assert m % tm == 0 and n % tn == 0 and kk % tk == 0, (name, a.shape, b.shape, tm, tn, tk)
    a_spec = pl.BlockSpec((tk, tm), lambda i, j, k: (k, i)) if mode == "tn" else pl.BlockSpec((tm, tk), lambda i, j, k: (i, k))
    b_spec = pl.BlockSpec((tn, tk), lambda i, j, k: (j, k)) if mode == "nt" else pl.BlockSpec((tk, tn), lambda i, j, k: (k, j))
    ca, cb = {"nn": (1, 0), "nt": (1, 1), "tn": (0, 0)}[mode]

    gm, gn = m // tm, n // tn

    def kern(*refs):
        a_ref, b_ref = refs[:2]
        side_in = refs[2:2 + n_side]
        o_ref = refs[2 + n_side]
        side_out = refs[3 + n_side:3 + 2 * n_side]
        acc_ref = refs[3 + 2 * n_side]
        side_sems = refs[4 + 2 * n_side:]
        i, j, k = pl.program_id(0), pl.program_id(1), pl.program_id(2)
        if n_side:
            @pl.when((i == 0) & (j == 0) & (k == 0))
            def _():
                side.start(side_in, side_out, side_sems)

        part = lax.dot_general(a_ref[...], b_ref[...], (((ca,), (cb,)), ((), ())), preferred_element_type=F32)
        if nk == 1:
            o_ref[...] = part.astype(o_ref.dtype)
        else:
            @pl.when(k == 0)
            def _():
                acc_ref[...] = part

            @pl.when(k > 0)
            def _():
                acc_ref[...] += part

            @pl.when(k == nk - 1)
            def _():
                o_ref[...] = acc_ref[...].astype(o_ref.dtype)

        if n_side:
            @pl.when((i == gm - 1) & (j == gn - 1) & (k == nk - 1))
            def _():
                side.wait(side_in, side_out, side_sems)

    semantics = ("arbitrary",) * 3 if n_side else ("parallel", "parallel", "arbitrary")
    res = pl.pallas_call(
        kern,
        name=name,
        grid=(gm, gn, nk),
        in_specs=[a_spec, b_spec] + side.in_specs,
        out_specs=[pl.BlockSpec((tm, tn), lambda i, j, k: (i, j))] + side.out_specs,
        out_shape=[jax.ShapeDtypeStruct((m, n), out_dtype)] + side.out_shape,
        scratch_shapes=[pltpu.VMEM((tm, tn) if nk > 1 else (8, 128), F32)] + side.scratch,
        compiler_params=pltpu.CompilerParams(dimension_semantics=semantics, vmem_limit_bytes=VMEM_LIMIT),
    )(a, b, *side.operands)
    return (res[0], list(res[1:])) if n_side else res[0]


def all_gather(name, x, in_vmem):
    def body(x_ref, out_ref, send_sems, recv_sems, local_sem):
        x_, y_, c_ = _my_pos()
        me, sibling = (x_, y_, c_), (x_, y_, 1 - c_)
        chips = [(1 - x_, y_), (x_, 1 - y_), (1 - x_, 1 - y_)]

        def slot(px, py, pc):
            return out_ref.at[4 * px + 2 * py + pc]

        def copy(k, block, to, src=None):
            return pltpu.make_async_remote_copy(
                src_ref=slot(*block) if src is None else src, dst_ref=slot(*block),
                send_sem=send_sems.at[k], recv_sem=recv_sems.at[k], device_id=to, device_id_type=MESH)

        mine = pltpu.make_async_copy(x_ref, slot(*me), local_sem)
        mine.start()
        first = [copy(0, me, sibling, src=x_ref)]
        first += [copy(1 + j, me, (*chip, c_), src=x_ref) for j, chip in enumerate(chips)]
        for cp in first:
            cp.start()
        passed = [copy(4 + j, (*chip, c_), sibling) for j, chip in enumerate(chips)]
        for j, chip in enumerate(chips):
            copy(1 + j, (*chip, c_), me).wait_recv()
            passed[j].start()
        copy(0, sibling, me).wait_recv()
        for j, chip in enumerate(chips):
            copy(4 + j, (*chip, 1 - c_), me).wait_recv()
        for cp in first + passed:
            cp.wait_send()
        mine.wait()

    space = pltpu.VMEM if in_vmem else pl.ANY
    return pl.pallas_call(
        body,
        name=name,
        out_shape=jax.ShapeDtypeStruct((N_DEV,) + x.shape, x.dtype),
        in_specs=[pl.BlockSpec(memory_space=space)],
        out_specs=pl.BlockSpec(memory_space=space),
        scratch_shapes=[pltpu.SemaphoreType.DMA((7,)), pltpu.SemaphoreType.DMA((7,)), pltpu.SemaphoreType.DMA],
        compiler_params=pltpu.CompilerParams(vmem_limit_bytes=VMEM_LIMIT),
    )(x)


def all_to_all(name, g):
    def body(g_ref, out_ref, send_sems, recv_sems, local_sem):
        x_, y_, c_ = _my_pos()
        me = 4 * x_ + 2 * y_ + c_

        def peer(k):
            fx, fy, fc = (k >> 2) & 1, (k >> 1) & 1, k & 1
            return (1 - x_ if fx else x_, 1 - y_ if fy else y_, 1 - c_ if fc else c_)

        def copy(k):
            px, py, pc = peer(k)
            return pltpu.make_async_remote_copy(
                src_ref=g_ref.at[4 * px + 2 * py + pc], dst_ref=out_ref.at[me],
                send_sem=send_sems.at[k - 1], recv_sem=recv_sems.at[k - 1], device_id=(px, py, pc), device_id_type=MESH)

        def landing(k):
            px, py, pc = peer(k)
            return pltpu.make_async_remote_copy(
                src_ref=g_ref.at[me], dst_ref=out_ref.at[4 * px + 2 * py + pc],
                send_sem=send_sems.at[k - 1], recv_sem=recv_sems.at[k - 1], device_id=(px, py, pc), device_id_type=MESH)

        mine = pltpu.make_async_copy(g_ref.at[me], out_ref.at[me], local_sem)
        mine.start()
        sends = [copy(k) for k in range(1, N_DEV)]
        for cp in sends:
            cp.start()
        for k in range(1, N_DEV):
            landing(k).wait_recv()
        for cp in sends:
            cp.wait_send()
        mine.wait()

    return pl.pallas_call(
        body,
        name=name,
        out_shape=jax.ShapeDtypeStruct(g.shape, g.dtype),
        in_specs=[pl.BlockSpec(memory_space=pl.ANY)],
        out_specs=pl.BlockSpec(memory_space=pl.ANY),
        scratch_shapes=[pltpu.SemaphoreType.DMA((7,)), pltpu.SemaphoreType.DMA((7,)), pltpu.SemaphoreType.DMA],
    )(g)


def sum_blocks(name, g):
    def body(g_ref, o_ref):
        acc = g_ref[0]
        for s in range(1, N_DEV):
            acc = acc + g_ref[s]
        o_ref[...] = acc

    r = g.shape[1]
    tr = r // 4 if r % 32 == 0 else r
    return pl.pallas_call(
        body, name=name, grid=(r // tr,),
        in_specs=[pl.BlockSpec((N_DEV, tr, 128), lambda i: (0, i, 0))],
        out_specs=pl.BlockSpec((tr, 128), lambda i: (i, 0)),
        out_shape=jax.ShapeDtypeStruct((r, 128), F32),
        compiler_params=pltpu.CompilerParams(vmem_limit_bytes=VMEM_LIMIT),
    )(g)


def _adamw_math(w, g, m, v):
    m = ADAM_B1 * m + (1.0 - ADAM_B1) * g
    v = ADAM_B2 * v + (1.0 - ADAM_B2) * (g * g)
    m_hat = m / (1.0 - ADAM_B1 ** ADAM_STEP)
    v_hat = v / (1.0 - ADAM_B2 ** ADAM_STEP)
    delta = -ADAM_LR * (m_hat / (jnp.sqrt(v_hat) + ADAM_EPS) + ADAM_WD * w)
    return delta, m, v


ADAMW_BLOCK_BYTES = 3 << 19


def _row_tile(rows, row_bytes, limit):
    best = 8
    for t in range(8, rows + 1, 8):
        if rows % t == 0 and t * row_bytes <= limit:
            best = t
    return best


def adamw_big(name, w, lands, m, v):
    depth, r, c = w.shape
    outs = None
    for (l, part), land in sorted(lands.items()):
        rows = land.shape[1]
        tr = _row_tile(rows, 4 * (-(-c // 128) * 128), ADAMW_BLOCK_BYTES)
        first = part * rows // tr

        def body(w_ref, l_ref, m_ref, v_ref, *rest):
            g_out, d_out, m_out, v_out = rest[-4:]
            g = l_ref[0].astype(F32)
            for s in range(1, N_DEV):
                g = g + l_ref[s].astype(F32)
            delta, m_new, v_new = _adamw_math(w_ref[...], g, m_ref[...], v_ref[...])
            g_out[...] = g
            d_out[...] = delta
            m_out[...] = m_new
            v_out[...] = v_new

        spec = pl.BlockSpec((None, tr, c), lambda i, l=l, first=first: (l, first + i, 0))
        carried = [] if outs is None else list(outs)
        outs = pl.pallas_call(
            body, name=f"{name}_{l}_{part}", grid=(rows // tr,),
            in_specs=[spec, pl.BlockSpec((N_DEV, tr, c), lambda i: (0, i, 0)), spec, spec]
            + [pl.BlockSpec(memory_space=pl.ANY)] * len(carried),
            out_specs=[spec] * 4,
            out_shape=[jax.ShapeDtypeStruct((depth, r, c), F32)] * 4,
            input_output_aliases={4 + k: k for k in range(len(carried))},
            compiler_params=pltpu.CompilerParams(dimension_semantics=("parallel",), vmem_limit_bytes=VMEM_LIMIT),
        )(w, land, m, v, *carried)
    return outs


def adamw_small(name, w, g, m, v):
    def body(w_ref, g_ref, m_ref, v_ref, d_out, m_out, v_out):
        delta, m_new, v_new = _adamw_math(w_ref[...], g_ref[...], m_ref[...], v_ref[...])
        d_out[...] = delta
        m_out[...] = m_new
        v_out[...] = v_new

    return pl.pallas_call(
        body, name=name,
        out_shape=[jax.ShapeDtypeStruct(w.shape, F32)] * 3,
        compiler_params=pltpu.CompilerParams(vmem_limit_bytes=VMEM_LIMIT),
    )(w, g, m, v)


def cast_bf16(name, w):
    depth, r, c = w.shape
    tr = _row_tile(r, 4 * (-(-c // 128) * 128), ADAMW_BLOCK_BYTES)

    def body(w_ref, o_ref):
        o_ref[...] = w_ref[...].astype(BF16)

    spec = pl.BlockSpec((None, tr, c), lambda l, i: (l, i, 0))
    return pl.pallas_call(body, name=name, grid=(depth, r // tr), in_specs=[spec], out_specs=spec,
                          out_shape=jax.ShapeDtypeStruct((depth, r, c), BF16),
                          compiler_params=pltpu.CompilerParams(dimension_semantics=("parallel", "parallel")))(w)


def _rows_of(shape):
    return -(-math.prod(shape) // 128)


def _pack(arrs):
    pieces = []
    for a in arrs:
        flat = a.reshape(-1).astype(F32)
        pieces.append(jnp.pad(flat, (0, (-flat.shape[0]) % 128)).reshape(-1, 128))
    rows = sum(p.shape[0] for p in pieces)
    if rows % 8:
        pieces.append(jnp.zeros((8 - rows % 8, 128), F32))
    return jnp.concatenate(pieces, axis=0)


def _unpack(packed, shapes, lead=()):
    out, r0 = [], 0
    for s in shapes:
        rows, n = _rows_of(s), math.prod(s)
        piece = packed[..., r0:r0 + rows, :].reshape(lead + (rows * 128,))
        out.append(piece[..., :n].reshape(lead + tuple(s)))
        r0 += rows
    return out


def _block_diag(w):
    h, d, _ = w.shape
    eye = jnp.eye(h, dtype=w.dtype)
    return (eye[:, None, :, None] * w[:, :, None, :]).reshape(h * d, h * d)


def _block_diag_grad(g, h):
    d = g.shape[0] // h
    g4 = g.reshape(h, d, h, d)
    return jnp.stack([g4[k, :, k, :] for k in range(h)])


def _layer_params(wt, l):
    gp = jnp.pad(jnp.stack([wt["gdn_a_log"][l], wt["gdn_dt_bias"][l]]), ((0, 6), (4, 128 - 4 - GDN_HEADS)))
    d_ffh = wt["ffn_conv_w"].shape[-1] // 2
    return dict(
        pre_mix=wt["pre_mix_norm"][l][None], post_mix=wt["post_mix_norm"][l][None],
        pre_ffn=wt["pre_ffn_norm"][l][None], post_ffn=wt["post_ffn_norm"][l][None],
        gdn_cw=wt["gdn_conv_w"][l], gdn_gp=gp, gdn_nw=wt["gdn_norm_w"][l][None],
        lru_cw=wt["lru_conv_w"][l], lru_cb=wt["lru_conv_b"][l][None],
        lru_wa=_block_diag(wt["lru_wa"][l]), lru_ba=wt["lru_ba"][l].reshape(1, -1),
        lru_wx=_block_diag(wt["lru_wx"][l]), lru_bx=wt["lru_bx"][l].reshape(1, -1),
        lru_lam=wt["lru_lambda"][l][None], gw0=wt["grp_norm_w"][l, 0][None], gw1=wt["grp_norm_w"][l, 1][None],
        gw2=wt["grp_norm_w"][l, 2][None],
        sgu_lnw=wt["sgu_ln_w"][l][None], sgu_lnb=wt["sgu_ln_b"][l][None],
        sgu_ws=wt["sgu_ws"][l].reshape(SGU_GROUPS * SGU_CHUNK, SGU_CHUNK),
        sgu_bt=jnp.pad(wt["sgu_b"][l].T, ((0, 0), (0, 128 - SGU_GROUPS))),
        sc_cw=wt["sconv_w"][l],
        ffn_cw=wt["ffn_conv_w"][l], ffn_cb=wt["ffn_conv_b"][l][None], d_ffh=d_ffh,
    )


TS_ROW = 256
TS_GDN = 256
TS_FFN = 256
TC_FFN = 512


def _mixers_fwd(l, p, lp, n, side):
    qkv = _rin(p, 3 * D_G, 0, halo=True)
    z = _rin(p, D_G, 3)
    ba = _rin(p, 128, BA_COL // 128)
    gdn_ps = [_par(lp["gdn_cw"]), _par(lp["gdn_gp"]), _par(lp["gdn_nw"])]
    res = seq_fwd(f"gdn_fwd_{l}", fn_gdn, n, TS_GDN, [qkv, z, ba], gdn_ps, [_out(D_G, BF16)],
                  state_shapes=[(GDN_HEADS * GDN_DIM, GDN_DIM)], side=side)
    (y_a,), (gdn_st,), side_res = res if side else res + ([],)
    lru_x = _rin(p, D_G, 4, halo=True)
    lru_gate = _rin(p, D_G, 5)
    lru_ps = [_par(lp[k]) for k in ("lru_cw", "lru_cb", "lru_wa", "lru_ba", "lru_wx", "lru_bx", "lru_lam", "gw0")]
    (y_b,), (lru_st,) = seq_fwd(f"lru_fwd_{l}", fn_lru, n, TS_ROW, [lru_x, lru_gate], lru_ps, [_out(D_G, BF16)],
                                state_shapes=[(1, D_G)])
    uv = _rin(p, 2 * D_G, 3)
    sgu_ps = [_par(lp[k]) for k in ("sgu_lnw", "sgu_lnb", "sgu_ws", "sgu_bt", "gw1")]
    (y_c,), _ = seq_fwd(f"sgu_fwd_{l}", fn_sgu, n, TS_ROW, [uv], sgu_ps, [_out(D_G, BF16)])
    sc = [_rin(p, D_G, 8), _rin(p, D_G, 9, halo=True), _rin(p, D_G, 10, halo=True)]
    sc_ps = [_par(lp["sc_cw"]), _par(lp["gw2"])]
    (y_d,), _ = seq_fwd(f"sconv_fwd_{l}", fn_sconv, n, TS_ROW, sc, sc_ps, [_out(D_G, BF16)])
    ins = dict(gdn=([qkv, z, ba], gdn_ps, [gdn_st]), lru=([lru_x, lru_gate], lru_ps, [lru_st]),
               sgu=([uv], sgu_ps, []), sc=(sc, sc_ps, []))
    return jnp.concatenate([y_a, y_b, y_c, y_d], axis=1), ins, side_res


def _mixers_bwd(l, dymix, ins, n, side):
    cot = lambda g: [_rin(dymix, D_G, g)]
    xs, ps, st = ins["gdn"]
    res = seq_bwd(f"gdn_bwd_{l}", fn_gdn, n, TS_GDN, xs, ps, cot(0), st, [BF16, BF16, BF16], side=side)
    (dqkv, dz, dba), g_gdn, side_res = res if side else res + ([],)
    xs, ps, st = ins["lru"]
    (dlx, dlg), g_lru = seq_bwd(f"lru_bwd_{l}", fn_lru, n, TS_ROW, xs, ps, cot(1), st, [BF16, BF16])
    xs, ps, st = ins["sgu"]
    (duv,), g_sgu = seq_bwd(f"sgu_bwd_{l}", fn_sgu, n, TS_ROW, xs, ps, cot(2), st, [BF16])
    xs, ps, st = ins["sc"]
    (dsb, dsc, dsh), g_sc = seq_bwd(f"sconv_bwd_{l}", fn_sconv, n, TS_ROW, xs, ps, cot(3), st, [BF16, BF16, BF16])
    dp = jnp.concatenate([dqkv, dz, dlx, dlg, duv, dsb, dsc, dsh, dba], axis=1)
    return dp, dict(gdn=g_gdn, lru=g_lru, sgu=g_sgu, sc=g_sc), side_res


def _ffn_ops(hid, lp):
    d_ffh = lp["d_ffh"]
    off = d_ffh // TC_FFN
    xs = [_rin(hid, TC_FFN, lambda j: j, halo=True), _rin(hid, TC_FFN, lambda j: j + off, halo=True)]
    ps = [_par(lp["ffn_cw"], TC_FFN, lambda j: j), _par(lp["ffn_cw"], TC_FFN, lambda j: j + off),
          _par(lp["ffn_cb"], TC_FFN, lambda j: j), _par(lp["ffn_cb"], TC_FFN, lambda j: j + off)]
    return xs, ps, d_ffh


_FROM_BLOCKS = dict(
    w_in=lambda b: _regroup_w_in(b.transpose(1, 0, 2).reshape(b.shape[1], -1)),
    ffn_up=lambda b: b.transpose(1, 0, 2).reshape(b.shape[1], -1),
    w_out=lambda b: b.reshape(-1, b.shape[2]),
    ffn_down=lambda b: b.reshape(-1, b.shape[2]),
)
_TO_BLOCKS = dict(
    w_in=lambda g: _ungroup_w_in(g).reshape(g.shape[0], N_DEV, -1).transpose(1, 0, 2),
    ffn_up=lambda g: g.reshape(g.shape[0], N_DEV, -1).transpose(1, 0, 2),
    w_out=lambda g: g.reshape(N_DEV, -1, g.shape[1]),
    ffn_down=lambda g: g.reshape(N_DEV, -1, g.shape[1]),
)


class _Traffic:
    PARTS = dict(w_in=1, w_out=1, ffn_up=2, ffn_down=1)

    def __init__(self, whole=None, shards=None):
        self.whole = dict(whole or {})
        self.shards = shards
        self.gathered = {}
        self.pending = {}
        self.landed = {}

    def _rows(self, key):
        name, l, part = key
        rows = self.shards[name].shape[1] // self.PARTS[name]
        return slice(part * rows, (part + 1) * rows)

    def jobs(self, gather=(), exchange=()):
        if self.shards is None:
            return [], []
        keys = [("gather", k) for k in gather if k not in self.gathered and k[:2] not in self.whole]
        keys += [("exchange", k) for k in exchange if k in self.pending]
        jobs = [(kind, self.shards[k[0]][k[1]][self._rows(k)] if kind == "gather" else self.pending[k])
                for kind, k in keys]
        return jobs, keys

    def done(self, keys, results):
        for (kind, k), r in zip(keys, results):
            if kind == "gather":
                self.gathered[k] = r
            else:
                self.landed[k] = r
                del self.pending[k]

    def weight(self, name, l):
        if (name, l) not in self.whole:
            parts = []
            for part in range(self.PARTS[name]):
                k = (name, l, part)
                if k not in self.gathered:
                    self.gathered[k] = all_gather(f"gather_{name}_{l}_{part}", self.shards[name][l][self._rows(k)], False)
                parts.append(self.gathered[k])
            blocks = parts[0] if len(parts) == 1 else jnp.concatenate(parts, axis=1)
            self.whole[(name, l)] = _FROM_BLOCKS[name](blocks)
        return self.whole[(name, l)]

    def grad(self, name, l, g):
        if self.shards is None:
            self.landed[(name, l)] = g
            return
        blocks = _TO_BLOCKS[name](g)
        for part in range(self.PARTS[name]):
            k = (name, l, part)
            self.pending[k] = blocks[:, self._rows(k)]

    def flush(self):
        for (name, l, part), blocks in list(self.pending.items()):
            self.landed[(name, l, part)] = all_to_all(f"exchange_{name}_{l}_{part}", blocks)
            del self.pending[(name, l, part)]


def local_step(x, target, wt, tr):
    n, d = x.shape
    depth = wt["pre_mix_norm"].shape[0]
    lps = [_layer_params(wt, l) for l in range(depth)]
    saved = []
    xin = x

    def mm(name, a, b, mode, dtype, tm, tn, tk, gather=(), exchange=()):
        jobs, keys = tr.jobs(gather, exchange)
        if not jobs:
            return matmul(name, a, b, mode, dtype, tm, tn, tk)
        out, res = matmul(name, a, b, mode, dtype, tm, tn, tk, side=jobs)
        tr.done(keys, res)
        return out

    (h,), _ = seq_fwd("norm_fwd", fn_norm, n, TS_ROW, [_rin(x)], [_par(lps[0]["pre_mix"])], [_out(d, BF16)])
    dx_last = loss = None
    for l in range(depth):
        lp = lps[l]
        p = mm(f"w_in_fwd_{l}", h, tr.weight("w_in", l), "nn", F32, 1024, N_INP // 5, d,
               gather=[("ffn_up", l, 0), ("w_out", l, 0)])
        jobs, keys = tr.jobs(gather=[("ffn_up", l, 1)])
        ymix, mix_ins, res = _mixers_fwd(l, p, lp, n, jobs)
        tr.done(keys, res)
        y = mm(f"w_out_fwd_{l}", ymix, tr.weight("w_out", l), "nn", F32, 1024, 1024, d)
        res_ps = [_par(lp["post_mix"]), _par(lp["pre_ffn"])]
        (x1, h2), _ = seq_fwd(f"res_mix_fwd_{l}", fn_res, n, TS_ROW, [_rin(xin), _rin(y)], res_ps,
                              [_out(d, F32), _out(d, BF16)])
        nxt = l + 1 < depth
        hid = mm(f"ffn_up_fwd_{l}", h2, tr.weight("ffn_up", l), "nn", F32, 1024, 1024, d,
                 gather=[("ffn_down", l, 0)] + ([("w_out", l + 1, 0)] if nxt else []))
        f_xs, f_ps, d_ffh = _ffn_ops(hid, lp)
        (act,), _ = seq_fwd(f"ffn_act_fwd_{l}", fn_ffn, n, TS_FFN, f_xs, f_ps,
                            [_out(TC_FFN, BF16, d_ffh, lambda j: j)], ncol=d_ffh // TC_FFN)
        yf = mm(f"ffn_down_fwd_{l}", act, tr.weight("ffn_down", l), "nn", F32, 1024, 1024, d_ffh // 2,
                gather=[("w_in", l + 1, 0)] if nxt else [])
        rec = dict(x=xin, h=h, mix_ins=mix_ins, ymix=ymix, y=y, x1=x1, h2=h2, f_xs=f_xs, f_ps=f_ps, act=act, yf=yf)
        if l + 1 < depth:
            ps = [_par(lp["post_ffn"]), _par(lps[l + 1]["pre_mix"])]
            (x2, h), _ = seq_fwd(f"res_ffn_fwd_{l}", fn_res, n, TS_ROW, [_rin(x1), _rin(yf)], ps,
                                 [_out(d, F32), _out(d, BF16)])
            rec["res_ffn_ps"] = ps
            xin = x2
        else:
            def body(xs, blks, ps, carries, r):
                x1_, yf_, t_ = xs
                e = x1_ + _rms(yf_, ps[0]) - t_
                part = 0.5 * jnp.sum(jnp.mean(e * e, axis=-1, keepdims=True), axis=0, keepdims=True)
                return [e * (1.0 / d)], [], [jnp.broadcast_to(part, (8, 128))], []

            ps = [_par(lp["post_ffn"])]
            dx_last, loss = _block_call("loss_fwd", body, n, TS_ROW, 1, False, [_rin(x1), _rin(yf), _rin(target)],
                                        [], ps, [_out(d, F32)], [], [dict(shape=(8, 128), total=None, col=None)], [])
            rec["res_ffn_ps"] = ps
        saved.append(rec)

    grads = {}
    dx2, dh_next = dx_last, None
    for l in reversed(range(depth)):
        rec, lp = saved[l], lps[l]
        d_ffh = lp["d_ffh"]
        g = {}
        if dh_next is None:
            (dx1, dyf), (g["post_ffn"],) = seq_bwd(f"res_ffn_bwd_{l}", fn_res_last, n, TS_ROW,
                                                   [_rin(rec["x1"]), _rin(rec["yf"])], rec["res_ffn_ps"], [_rin(dx2)],
                                                   din_dtypes=[F32, BF16])
        else:
            (dx1, dyf), (g["post_ffn"], g_next_pre) = seq_bwd(
                f"res_ffn_bwd_{l}", fn_res, n, TS_ROW, [_rin(rec["x1"]), _rin(rec["yf"])], rec["res_ffn_ps"],
                [_rin(dx2), _rin(dh_next)], din_dtypes=[F32, BF16])
            grads[l + 1]["pre_mix"] = g_next_pre
        dact = mm(f"ffn_down_dx_{l}", dyf, tr.weight("ffn_down", l), "nt", BF16, 1024, 512, d)
        tr.grad("ffn_down", l, mm(f"ffn_down_dw_{l}", rec["act"], dyf, "tn", BF16, d_ffh // 4, 1024, 2048))
        off = d_ffh // TC_FFN
        (dhg, dhv), (g_cwg, g_cwv, g_cbg, g_cbv) = seq_bwd(
            f"ffn_act_bwd_{l}", fn_ffn, n, TS_FFN, rec["f_xs"], rec["f_ps"], [_rin(dact, TC_FFN, lambda j: j)],
            din_dtypes=[BF16, BF16], ncol=off, din_specs=[(d_ffh, lambda j: j), (d_ffh, lambda j: j)])
        dhid = jnp.concatenate([dhg, dhv], axis=1)
        g["ffn_cw"] = jnp.concatenate([g_cwg[:, :d_ffh], g_cwv[:, d_ffh:]], axis=1)
        g["ffn_cb"] = jnp.concatenate([g_cbg[:, :d_ffh], g_cbv[:, d_ffh:]], axis=1)
        dh2 = mm(f"ffn_up_dx_{l}", dhid, tr.weight("ffn_up", l), "nt", BF16, 1024, 1024, d_ffh // 2,
                 exchange=[("ffn_down", l, 0)])
        tr.grad("ffn_up", l, mm(f"ffn_up_dw_{l}", rec["h2"], dhid, "tn", BF16, 1024, 1024, 2048))
        (dx, dy), (g["post_mix"], g["pre_ffn"]) = seq_bwd(
            f"res_mix_bwd_{l}", fn_res, n, TS_ROW, [_rin(rec["x"]), _rin(rec["y"])],
            [_par(lp["post_mix"]), _par(lp["pre_ffn"])], [_rin(dx1), _rin(dh2)], din_dtypes=[F32, BF16])
        dymix = mm(f"w_out_dx_{l}", dy, tr.weight("w_out", l), "nt", BF16, 1024, 1024, d)
        tr.grad("w_out", l, mm(f"w_out_dw_{l}", rec["ymix"], dy, "tn", BF16, 1024, 1024, 2048))
        jobs, keys = tr.jobs(exchange=[("ffn_up", l, 0), ("w_out", l, 0)])
        dp, g["mix"], res = _mixers_bwd(l, dymix, rec["mix_ins"], n, jobs)
        tr.done(keys, res)
        tr.grad("w_in", l, mm(f"w_in_dw_{l}", rec["h"], dp, "tn", BF16, 1024, N_INP // 5, 2048,
                              exchange=[("ffn_up", l, 1)]))
        dh = mm(f"w_in_dx_{l}", dp, tr.weight("w_in", l), "nt", BF16, 1024, 1024, N_INP // 3,
                exchange=[("w_in", l, 0)])
        grads[l] = g
        dx2, dh_next = dx, dh
    (grad_x,), (g_pre0,) = seq_bwd("norm_bwd", fn_norm_keep, n, TS_ROW, [_rin(x)], [_par(lps[0]["pre_mix"])],
                                   [_rin(dh_next), _rin(dx2)])
    grads[0]["pre_mix"] = g_pre0
    tr.flush()
    return loss[0, 0], grad_x, _name_grads(grads, depth)


def _name_grads(grads, depth):
    per = {k: [] for k in SMALL}
    for l in range(depth):
        g = grads[l]
        m = g["mix"]
        cw, gp, nw = m["gdn"]
        lcw, lcb, lwa, lba, lwx, lbx, llam, gw0 = m["lru"]
        lnw, lnb, ws, bst, gw1 = m["sgu"]
        scw, gw2 = m["sc"]
        per["pre_mix_norm"].append(g["pre_mix"][0])
        per["gdn_conv_w"].append(cw)
        per["gdn_a_log"].append(gp[0, 4:8])
        per["gdn_dt_bias"].append(gp[1, 4:8])
        per["gdn_norm_w"].append(nw[0])
        per["lru_conv_w"].append(lcw)
        per["lru_conv_b"].append(lcb[0])
        per["lru_wa"].append(_block_diag_grad(lwa, LRU_BLOCKS))
        per["lru_ba"].append(lba.reshape(LRU_BLOCKS, -1))
        per["lru_wx"].append(_block_diag_grad(lwx, LRU_BLOCKS))
        per["lru_bx"].append(lbx.reshape(LRU_BLOCKS, -1))
        per["lru_lambda"].append(llam[0])
        per["sgu_ln_w"].append(lnw[0])
        per["sgu_ln_b"].append(lnb[0])
        per["sgu_ws"].append(ws.reshape(SGU_GROUPS, SGU_CHUNK, SGU_CHUNK))
        per["sgu_b"].append(bst[:, :SGU_GROUPS].T)
        per["sconv_w"].append(scw)
        per["grp_norm_w"].append(jnp.concatenate([gw0, gw1, gw2], axis=0))
        per["post_mix_norm"].append(g["post_mix"][0])
        per["pre_ffn_norm"].append(g["pre_ffn"][0])
        per["ffn_conv_w"].append(g["ffn_cw"])
        per["ffn_conv_b"].append(g["ffn_cb"][0])
        per["post_ffn_norm"].append(g["post_ffn"][0])
    return {k: jnp.stack(v) for k, v in per.items()}


def _regroup_w_in(w):
    pad = jnp.zeros(w.shape[:-1] + (N_INP - N_IN,), w.dtype)
    return jnp.concatenate([w[..., :2048], w[..., 2056:], w[..., 2048:2056], pad], axis=-1)


def _ungroup_w_in(g):
    return jnp.concatenate([g[..., :2048], g[..., BA_COL:BA_COL + 8], g[..., 2048:BA_COL]], axis=-1)


def kernel(x, pre_mix_norm, w_in, gdn_conv_w, gdn_a_log, gdn_dt_bias, gdn_norm_w, lru_conv_w, lru_conv_b, lru_wa, lru_ba, lru_wx, lru_bx, lru_lambda, sgu_ln_w, sgu_ln_b, sgu_ws, sgu_b, sconv_w, grp_norm_w, w_out, post_mix_norm, pre_ffn_norm, ffn_up, ffn_conv_w, ffn_conv_b, ffn_down, post_ffn_norm, loss_target, m_pre_mix_norm, m_w_in, m_gdn_conv_w, m_gdn_a_log, m_gdn_dt_bias, m_gdn_norm_w, m_lru_conv_w, m_lru_conv_b, m_lru_wa, m_lru_ba, m_lru_wx, m_lru_bx, m_lru_lambda, m_sgu_ln_w, m_sgu_ln_b, m_sgu_ws, m_sgu_b, m_sconv_w, m_grp_norm_w, m_w_out, m_post_mix_norm, m_pre_ffn_norm, m_ffn_up, m_ffn_conv_w, m_ffn_conv_b, m_ffn_down, m_post_ffn_norm, v_pre_mix_norm, v_w_in, v_gdn_conv_w, v_gdn_a_log, v_gdn_dt_bias, v_gdn_norm_w, v_lru_conv_w, v_lru_conv_b, v_lru_wa, v_lru_ba, v_lru_wx, v_lru_bx, v_lru_lambda, v_sgu_ln_w, v_sgu_ln_b, v_sgu_ws, v_sgu_b, v_sconv_w, v_grp_norm_w, v_w_out, v_post_mix_norm, v_pre_ffn_norm, v_ffn_up, v_ffn_conv_w, v_ffn_conv_b, v_ffn_down, v_post_ffn_norm):
    args = locals()
    w_loc = {k: args[k] for k in WEIGHTS}
    m_loc = {k: args["m_" + k] for k in WEIGHTS}
    v_loc = {k: args["v_" + k] for k in WEIGHTS}
    depth = pre_mix_norm.shape[0]
    x_, y_, c_ = _my_pos()
    me = 4 * x_ + 2 * y_ + c_

    tr = _Traffic(shards={name: cast_bf16(f"cast_{name}", w_loc[name]) for name in BIG})
    wt = {k: w_loc[k] for k in SMALL}
    shard_shapes = [w_loc[k].shape for k in SHARDED_SMALL]
    gathered = all_gather("gather_small", _pack([w_loc[k] for k in SHARDED_SMALL]), True)
    for k, a in zip(SHARDED_SMALL, _unpack(gathered, shard_shapes, lead=(N_DEV,))):
        a = jnp.moveaxis(a, 0, -2)
        wt[k] = a.reshape(a.shape[:-2] + (-1,))

    loss_part, grad_x, g_full = local_step(x[0], loss_target[0], wt, tr)
    loss = lax.psum(loss_part, ("x", "y", "c"))

    outs_g, outs_d, outs_m, outs_v = {}, {}, {}, {}
    for name in BIG:
        lands = {(l, part): a for (n_, l, part), a in tr.landed.items() if n_ == name}
        outs_g[name], outs_d[name], outs_m[name], outs_v[name] = adamw_big(
            f"adamw_{name}", w_loc[name], lands, m_loc[name], v_loc[name])

    full_shapes = [g_full[k].shape for k in SMALL]
    g_all = all_gather("gather_small_grads", _pack([g_full[k] for k in SMALL]), True)
    g_sum = _unpack(sum_blocks("sum_small_grads", g_all), full_shapes)
    g_small = {}
    for k, g in zip(SMALL, g_sum):
        if k in SHARDED_SMALL:
            w = w_loc[k].shape[-1]
            g = lax.dynamic_slice_in_dim(g, me * w, w, axis=g.ndim - 1)
        g_small[k] = g
    loc_shapes = [w_loc[k].shape for k in SMALL]
    res = adamw_small("adamw_small", _pack([w_loc[k] for k in SMALL]), _pack([g_small[k] for k in SMALL]),
                      _pack([m_loc[k] for k in SMALL]), _pack([v_loc[k] for k in SMALL]))
    d_s, m_s, v_s = [_unpack(a, loc_shapes) for a in res]
    for k_i, k in enumerate(SMALL):
        outs_g[k], outs_d[k], outs_m[k], outs_v[k] = g_small[k], d_s[k_i], m_s[k_i], v_s[k_i]

    return (loss, grad_x[None], *[outs_g[k] for k in WEIGHTS], *[outs_d[k] for k in WEIGHTS],
            *[outs_m[k] for k in WEIGHTS], *[outs_v[k] for k in WEIGHTS])
```

```python
import functools
import math

import jax
import jax.numpy as jnp
from jax import lax
from jax.experimental import pallas as pl
from jax.experimental.pallas import tpu as pltpu

F32 = jnp.float32
BF16 = jnp.bfloat16
EPS = 1e-6
HALO = 8
VMEM_LIMIT = 56 * 1024 * 1024
MESH = pl.DeviceIdType.MESH
N_DEV = 8

ADAM_LR, ADAM_B1, ADAM_B2, ADAM_EPS, ADAM_WD, ADAM_STEP = 0.001, 0.9, 0.999, 1e-08, 0.01, 10

GDN_HEADS, GDN_DIM, GDN_CHUNK = 4, 128, 64
SGU_GROUPS, SGU_CHUNK = 4, 128
LRU_BLOCKS, LRU_C = 8, 8.0
D_G = 512
N_IN = 5640
N_INP = 5760
BA_COL = 5632

SHARDED_SMALL = ("gdn_conv_w", "lru_conv_w", "sconv_w", "grp_norm_w", "ffn_conv_w")
BIG = ("w_in", "w_out", "ffn_up", "ffn_down")
WEIGHTS = ("pre_mix_norm", "w_in", "gdn_conv_w", "gdn_a_log", "gdn_dt_bias", "gdn_norm_w", "lru_conv_w",
           "lru_conv_b", "lru_wa", "lru_ba", "lru_wx", "lru_bx", "lru_lambda", "sgu_ln_w", "sgu_ln_b", "sgu_ws",
           "sgu_b", "sconv_w", "grp_norm_w", "w_out", "post_mix_norm", "pre_ffn_norm", "ffn_up", "ffn_conv_w",
           "ffn_conv_b", "ffn_down", "post_ffn_norm")
SMALL = tuple(n for n in WEIGHTS if n not in BIG)


def _dot(a, b, ca, cb):
    return lax.dot_general(a.astype(BF16), b.astype(BF16), (((ca,), (cb,)), ((), ())),
                           preferred_element_type=F32)


@jax.custom_vjp
def _mm(a, b):
    return _dot(a, b, 1, 0)


def _mm_f(a, b):
    return _dot(a, b, 1, 0), (a, b)


def _mm_b(res, g):
    a, b = res
    return _dot(g, b, 1, 1), _dot(a, g, 0, 0)


_mm.defvjp(_mm_f, _mm_b)


@jax.custom_vjp
def _mm_nt(a, b):
    return _dot(a, b, 1, 1)


def _mm_nt_f(a, b):
    return _dot(a, b, 1, 1), (a, b)


def _mm_nt_b(res, g):
    a, b = res
    return _dot(g, b, 1, 0), _dot(g, a, 0, 0)


_mm_nt.defvjp(_mm_nt_f, _mm_nt_b)


@jax.custom_vjp
def _mm_tn(a, b):
    return _dot(a, b, 0, 0)


def _mm_tn_f(a, b):
    return _dot(a, b, 0, 0), (a, b)


def _mm_tn_b(res, g):
    a, b = res
    return _dot(b, g, 1, 1), _dot(a, g, 1, 0)


_mm_tn.defvjp(_mm_tn_f, _mm_tn_b)


def _dot_exact(a, b, ca, cb):
    return lax.dot_general(a, b, (((ca,), (cb,)), ((), ())), precision=lax.Precision.HIGHEST,
                           preferred_element_type=F32)


@functools.partial(jax.custom_vjp, nondiff_argnums=(1,))
def _shift_rows(x, s):
    return pltpu.roll(x, s, 0)


def _shift_rows_f(x, s):
    return pltpu.roll(x, s, 0), None


def _shift_rows_b(s, _, g):
    return (pltpu.roll(g, (g.shape[0] - s) % g.shape[0], 0),)


_shift_rows.defvjp(_shift_rows_f, _shift_rows_b)


def _sigmoid(x):
    return 1.0 / (1.0 + jnp.exp(-x))


def _silu(x):
    return x * _sigmoid(x)


def _gelu(x):
    return 0.5 * x * (1.0 + jnp.tanh(0.7978845608028654 * (x + 0.044715 * (x * x * x))))


@jax.custom_vjp
def _softplus(x):
    e = jnp.exp(-jnp.abs(x))
    u = 1.0 + e
    log1p = jnp.where(u == 1.0, e, jnp.log(u) * (e / jnp.where(u == 1.0, 1.0, u - 1.0)))
    return jnp.maximum(x, 0.0) + log1p


def _softplus_f(x):
    return _softplus(x), x


def _softplus_b(x, g):
    return (g * _sigmoid(x),)


_softplus.defvjp(_softplus_f, _softplus_b)


def _neg_expm1(y):
    return -jnp.tanh(0.5 * y) * (jnp.exp(y) + 1.0)


def _rms(x, w):
    return x * lax.rsqrt(jnp.mean(x * x, axis=-1, keepdims=True) + EPS) * w


def _row(w, k):
    sel = lax.broadcasted_iota(jnp.int32, w.shape, 0) == k
    return jnp.sum(jnp.where(sel, w, 0.0), axis=0, keepdims=True)


def _col(x, j):
    sel = lax.broadcasted_iota(jnp.int32, x.shape, 1) == j
    return jnp.sum(jnp.where(sel, x, 0.0), axis=1, keepdims=True)


def _conv(x_ext, w, taps):
    acc = None
    for k in range(taps):
        s = taps - 1 - k
        t = (x_ext if s == 0 else _shift_rows(x_ext, s)) * _row(w, k)
        acc = t if acc is None else acc + t
    return acc[HALO:]


@jax.custom_vjp
def _scan(a, b, h0):
    n = a.shape[0]
    row = lax.broadcasted_iota(jnp.int32, a.shape, 0)
    s = 1
    while s < n:
        keep = row >= s
        a_sh = jnp.where(keep, pltpu.roll(a, s, 0), 1.0)
        b_sh = jnp.where(keep, pltpu.roll(b, s, 0), 0.0)
        b = a * b_sh + b
        a = a * a_sh
        s *= 2
    return b + a * h0


def _scan_f(a, b, h0):
    h = _scan(a, b, h0)
    return h, (a, h, h0)


def _scan_b(res, dh):
    a, h, h0 = res
    n = a.shape[0]
    row = lax.broadcasted_iota(jnp.int32, a.shape, 0)
    an = jnp.where(row < n - 1, pltpu.roll(a, n - 1, 0), 0.0)
    lam = dh
    s = 1
    while s < n:
        keep = row < n - s
        a_sh = jnp.where(keep, pltpu.roll(an, n - s, 0), 1.0)
        l_sh = jnp.where(keep, pltpu.roll(lam, n - s, 0), 0.0)
        lam = an * l_sh + lam
        an = an * a_sh
        s *= 2
    h_prev = jnp.where(row >= 1, pltpu.roll(h, 1, 0), h0)
    al = a * lam
    dh0 = jnp.sum(jnp.where(row == 0, al, 0.0), axis=0, keepdims=True)
    return lam * h_prev, lam, dh0


_scan.defvjp(_scan_f, _scan_b)


@jax.custom_vjp
def _unit_lower_inverses(ms):
    n = ms[0].shape[0]
    shape = ms[0].shape
    eye = (lax.broadcasted_iota(jnp.int32, shape, 0) == lax.broadcasted_iota(jnp.int32, shape, 1)).astype(F32)
    p = [-m for m in ms]
    t = [eye + a for a in p]
    steps = 1
    while 2 ** steps < n:
        p = [_mm(a, a) for a in p]
        t = [a + _mm(a, c) for a, c in zip(t, p)]
        steps += 1
    return t


def _unit_lower_inverses_f(ms):
    t = _unit_lower_inverses(ms)
    return t, t


def _unit_lower_inverses_b(t, dt):
    x = [_mm_nt(g, a) for g, a in zip(dt, t)]
    return ([-_mm_tn(a, c) for a, c in zip(t, x)],)


_unit_lower_inverses.defvjp(_unit_lower_inverses_f, _unit_lower_inverses_b)


def _last_row(x):
    sel = lax.broadcasted_iota(jnp.int32, x.shape, 0) == x.shape[0] - 1
    return jnp.sum(jnp.where(sel, x, 0.0), axis=0, keepdims=True)


def fn_norm(xs, st, ps):
    (x,), (w,) = xs, ps
    return [_rms(x, w).astype(BF16)], []


def fn_norm_keep(xs, st, ps):
    (x,), (w,) = xs, ps
    return [_rms(x, w).astype(BF16), x], []


def fn_res(xs, st, ps):
    (x, y), (w_post, w_next) = xs, ps
    x1 = x + _rms(y, w_post)
    return [x1, _rms(x1, w_next).astype(BF16)], []


def fn_res_last(xs, st, ps):
    (x, y), (w_post,) = xs, ps
    return [x + _rms(y, w_post)], []


def fn_gdn(xs, st, ps):
    qkv_ext, z, ba = xs
    (state,) = st
    cw, gp, nw = ps
    ts = z.shape[0]
    qkv = _silu(_conv(qkv_ext, cw, 4))
    beta_all = _sigmoid(ba)
    g_all = -jnp.exp(_row(gp, 0)) * _softplus(ba + _row(gp, 1))
    c_n = GDN_CHUNK
    ri = lax.broadcasted_iota(jnp.int32, (c_n, c_n), 0)
    ci = lax.broadcasted_iota(jnp.int32, (c_n, c_n), 1)
    causal, strict = ri >= ci, ri > ci
    tril = causal.astype(F32)
    lane = lax.broadcasted_iota(jnp.int32, (c_n, 128), 1)
    s_h = [state[GDN_DIM * h:GDN_DIM * (h + 1)] for h in range(GDN_HEADS)]
    n_c = ts // c_n
    pairs = [(c, h) for c in range(n_c) for h in range(GDN_HEADS)]
    every = lambda f, *lists: [f(*a) for a in zip(*lists)]

    def piece(c, h, base):
        return qkv[c * c_n:(c + 1) * c_n, base + GDN_DIM * h:base + GDN_DIM * (h + 1)]

    q = [piece(c, h, 0) for c, h in pairs]
    k = [piece(c, h, D_G) for c, h in pairs]
    v = [piece(c, h, 2 * D_G) for c, h in pairs]
    q = every(lambda t: t * lax.rsqrt(jnp.sum(t * t, axis=-1, keepdims=True) + EPS) * (GDN_DIM ** -0.5), q)
    k = every(lambda t: t * lax.rsqrt(jnp.sum(t * t, axis=-1, keepdims=True) + EPS), k)
    gcum_all = [_dot_exact(tril, g_all[c * c_n:(c + 1) * c_n], 1, 0) for c in range(n_c)]
    b = [_col(beta_all[c * c_n:(c + 1) * c_n], h) for c, h in pairs]
    gc = [_col(gcum_all[c], 4 + h) for c, h in pairs]
    gr = [_dot_exact((lane == 4 + h).astype(F32), gcum_all[c], 1, 1) for c, h in pairs]
    decay = every(lambda a, r: jnp.where(causal, jnp.exp(jnp.where(causal, a - r, 0.0)), 0.0), gc, gr)
    kb = every(lambda a, c: a * c, k, b)
    mk = every(lambda a, c, e: _mm_nt(jnp.concatenate([a, c], axis=0), e), kb, q, k)
    m = every(lambda a, dcy: jnp.where(strict, a[:c_n] * dcy, 0.0), mk, decay)
    attn = every(lambda a, dcy: jnp.where(causal, a[c_n:] * dcy, 0.0), mk, decay)
    t_ = _unit_lower_inverses(m)
    eg = every(jnp.exp, gc)
    wu = every(lambda t, a, e, c, d: _mm(t, jnp.concatenate([a * e, c * d], axis=1)), t_, kb, eg, v, b)
    g_last = every(_last_row, gc)
    k_g = every(lambda a, gl, g: a * jnp.exp(gl - g), k, g_last, gc)
    wq = every(lambda a, c, e: jnp.concatenate([a[:, :GDN_DIM], c * e], axis=0), wu, q, eg)
    u = [a[:, GDN_DIM:] for a in wu]
    gl = every(jnp.exp, g_last)

    o = []
    for c in range(n_c):
        idx = range(c * GDN_HEADS, (c + 1) * GDN_HEADS)
        ws = [_mm(wq[i], s_h[h]) for h, i in enumerate(idx)]
        v_new = [u[i] - ws[h][:c_n] for h, i in enumerate(idx)]
        av = [_mm(attn[i], v_new[h]) for h, i in enumerate(idx)]
        kv = [_mm_tn(k_g[i], v_new[h]) for h, i in enumerate(idx)]
        o += [ws[h][c_n:] + av[h] for h in range(GDN_HEADS)]
        s_h = [s_h[h] * gl[i] + kv[h] for h, i in enumerate(idx)]
    zz = [z[c * c_n:(c + 1) * c_n, GDN_DIM * h:GDN_DIM * (h + 1)] for c, h in pairs]
    y = every(lambda a, g: a * lax.rsqrt(jnp.mean(a * a, axis=-1, keepdims=True) + EPS) * nw * _silu(g), o, zz)
    rows = [jnp.concatenate(y[c * GDN_HEADS:(c + 1) * GDN_HEADS], axis=1) for c in range(n_c)]
    y = rows[0] if n_c == 1 else jnp.concatenate(rows, axis=0)
    return [y.astype(BF16)], [jnp.concatenate(s_h, axis=0)]


def fn_lru(xs, st, ps):
    x_ext, gate = xs
    (h0,) = st
    cw, cb, wa, ba, wx, bx, lam, gw = ps
    xc = _conv(x_ext, cw, 4) + cb
    r = _sigmoid(_mm(xc, wa) + ba)
    i = _sigmoid(_mm(xc, wx) + bx)
    log_a = -LRU_C * r * _softplus(-lam)
    a = jnp.exp(log_a)
    mult = jnp.sqrt(_neg_expm1(2.0 * log_a))
    h = _scan(a, mult * (i * xc), h0)
    y = _rms(h * _gelu(gate), gw)
    return [y.astype(BF16)], [_last_row(h)]


def fn_sgu(xs, st, ps):
    (uv,) = xs
    lnw, lnb, ws, bst, gw = ps
    ts = uv.shape[0]
    uvf = _gelu(uv)
    u, v = uvf[:, :D_G], uvf[:, D_G:]
    vc = v - jnp.mean(v, axis=-1, keepdims=True)
    v = vc * lax.rsqrt(jnp.mean(vc * vc, axis=-1, keepdims=True) + EPS) * lnw + lnb
    t_n = SGU_CHUNK
    tril = lax.broadcasted_iota(jnp.int32, (t_n, t_n), 0) >= lax.broadcasted_iota(jnp.int32, (t_n, t_n), 1)
    wg = [jnp.where(tril, ws[t_n * g:t_n * (g + 1)], 0.0) for g in range(SGU_GROUPS)]
    bg = [_col(bst, g) for g in range(SGU_GROUPS)]
    rows = []
    for c in range(ts // t_n):
        vcg = v[c * t_n:(c + 1) * t_n]
        rows.append(jnp.concatenate(
            [_mm(wg[g], vcg[:, 128 * g:128 * (g + 1)]) + bg[g] for g in range(SGU_GROUPS)], axis=1))
    vv = rows[0] if len(rows) == 1 else jnp.concatenate(rows, axis=0)
    return [_rms(u * vv, gw).astype(BF16)], []


def fn_sconv(xs, st, ps):
    bg, cg_ext, hh_ext = xs
    cw, gw = ps
    return [_rms(bg * _conv(cg_ext * hh_ext, cw, 3), gw).astype(BF16)], []


def fn_ffn(xs, st, ps):
    (x_ext,), (cw, cb) = xs, ps
    c = _conv(x_ext, cw, 3) + cb
    half = c.shape[1] // 2
    return [(_gelu(c[:, :half]) * c[:, half:]).astype(BF16)], []


def _my_pos():
    return lax.axis_index("x"), lax.axis_index("y"), lax.axis_index("c")


def _peer(pos, k):
    x_, y_, c_ = pos
    return (1 - x_ if (k >> 2) & 1 else x_, 1 - y_ if (k >> 1) & 1 else y_, 1 - c_ if k & 1 else c_)


def _dev_index(p):
    return 4 * p[0] + 2 * p[1] + p[2]


class _Side:
    def __init__(self, jobs):
        self.jobs = list(jobs)
        n = len(self.jobs)
        self.operands = [a for _, a in self.jobs]
        self.in_specs = [pl.BlockSpec(memory_space=pl.ANY)] * n
        self.out_shape = [jax.ShapeDtypeStruct(((N_DEV,) + a.shape) if kind == "gather" else a.shape, a.dtype)
                          for kind, a in self.jobs]
        self.out_specs = [pl.BlockSpec(memory_space=pl.ANY)] * n
        self.scratch = [pltpu.SemaphoreType.DMA((7 * n,)), pltpu.SemaphoreType.DMA((7 * n,)),
                        pltpu.SemaphoreType.DMA((n,))] if n else []

    def _copies(self, in_refs, out_refs, sems, landings=True):
        send, recv, local = sems
        pos = _my_pos()
        me = _dev_index(pos)
        mine, outgoing, landing = [], [], []
        for j, (kind, _) in enumerate(self.jobs):
            src, dst = in_refs[j], out_refs[j]
            own = src if kind == "gather" else src.at[me]
            mine.append(pltpu.make_async_copy(own, dst.at[me], local.at[j]))
            for k in range(1, N_DEV):
                p = _peer(pos, k)
                sems_k = dict(send_sem=send.at[7 * j + k - 1], recv_sem=recv.at[7 * j + k - 1], device_id=p,
                              device_id_type=MESH)
                outgoing.append(pltpu.make_async_remote_copy(
                    src_ref=src if kind == "gather" else src.at[_dev_index(p)], dst_ref=dst.at[me], **sems_k))
                if landings:
                    landing.append(pltpu.make_async_remote_copy(src_ref=own, dst_ref=dst.at[_dev_index(p)], **sems_k))
        return mine, outgoing, landing

    def start(self, in_refs, out_refs, sems):
        mine, outgoing, _ = self._copies(in_refs, out_refs, sems, landings=False)
        for cp in mine + outgoing:
            cp.start()

    def wait(self, in_refs, out_refs, sems):
        mine, outgoing, landing = self._copies(in_refs, out_refs, sems)
        for cp in landing:
            cp.wait_recv()
        for cp in outgoing:
            cp.wait_send()
        for cp in mine:
            cp.wait()


def _rin(arr, w=None, col=0, halo=False):
    return dict(arr=arr, w=arr.shape[1] if w is None else w, col=col, halo=halo)


def _par(arr, w=None, col=None):
    return dict(arr=arr, w=w, col=col)


def _colidx(col, j):
    return col(j) if callable(col) else col


def _block_call(name, body, n_rows, ts, ncol, reverse, row_ins, blk_ins, params, row_outs, blk_outs, acc_outs,
                carries, side=()):
    side = _Side(side)
    ts = min(ts, n_rows)
    nblk = n_rows // ts
    hb = ts // HALO

    def rr(i):
        return (nblk - 1 - i) if reverse else i

    in_specs, operands = [], []
    for s in row_ins:
        in_specs.append(pl.BlockSpec((ts, s["w"]), lambda j, i, s=s: (rr(i), _colidx(s["col"], j))))
        operands.append(s["arr"])
        if s["halo"]:
            in_specs.append(pl.BlockSpec((HALO, s["w"]),
                                         lambda j, i, s=s: (jnp.maximum(rr(i) * hb - 1, 0), _colidx(s["col"], j))))
            operands.append(s["arr"])
    for a in blk_ins:
        nd = a.ndim - 1
        in_specs.append(pl.BlockSpec((None,) + a.shape[1:], lambda j, i, nd=nd: (rr(i),) + (0,) * nd))
        operands.append(a)
    for p in params:
        a = p["arr"]
        if p["col"] is None:
            in_specs.append(pl.BlockSpec(a.shape, lambda j, i: (0, 0)))
        else:
            in_specs.append(pl.BlockSpec((a.shape[0], p["w"]), lambda j, i, p=p: (0, _colidx(p["col"], j))))
        operands.append(a)

    out_specs, out_shape = [], []
    for o in row_outs:
        out_specs.append(pl.BlockSpec((ts, o["w"]), lambda j, i, o=o: (rr(i), _colidx(o["col"], j))))
        out_shape.append(jax.ShapeDtypeStruct((n_rows, o["total"]), o["dtype"]))
    for o in blk_outs:
        nd = len(o["shape"])
        out_specs.append(pl.BlockSpec((None,) + tuple(o["shape"]), lambda j, i, nd=nd: (rr(i),) + (0,) * nd))
        out_shape.append(jax.ShapeDtypeStruct((nblk,) + tuple(o["shape"]), o["dtype"]))
    for o in acc_outs:
        if o["col"] is None:
            out_specs.append(pl.BlockSpec(o["shape"], lambda j, i: (0, 0)))
            out_shape.append(jax.ShapeDtypeStruct(o["shape"], F32))
        else:
            out_specs.append(pl.BlockSpec(o["shape"], lambda j, i, o=o: (0, _colidx(o["col"], j))))
            out_shape.append(jax.ShapeDtypeStruct((o["shape"][0], o["total"]), F32))

    n_in = len(operands)
    n_row_out, n_blk_out, n_acc = len(row_outs), len(blk_outs), len(acc_outs)
    n_out = n_row_out + n_blk_out + n_acc
    n_side = len(side.jobs)

    def kern(*refs):
        in_refs = refs[:n_in]
        side_in = refs[n_in:n_in + n_side]
        out_refs = refs[n_in + n_side:n_in + n_side + n_out]
        side_out = refs[n_in + n_side + n_out:n_in + 2 * n_side + n_out]
        scratch = refs[n_in + 2 * n_side + n_out:]
        carry_refs, side_sems = scratch[:len(carries)], scratch[len(carries):]
        acc_refs = out_refs[n_row_out + n_blk_out:]
        i = pl.program_id(1)
        r = rr(i)
        if n_side:
            @pl.when((pl.program_id(0) == 0) & (i == 0))
            def _():
                side.start(side_in, side_out, side_sems)

        @pl.when(i == 0)
        def _():
            for c_ref in carry_refs:
                c_ref[...] = jnp.zeros(c_ref.shape, c_ref.dtype)
            for a_ref in acc_refs:
                a_ref[...] = jnp.zeros(a_ref.shape, a_ref.dtype)

        k = 0
        xs = []
        for s in row_ins:
            x = in_refs[k][...]
            k += 1
            if s["halo"]:
                hal = in_refs[k][...]
                k += 1
                hal = jnp.where(r == 0, jnp.zeros_like(hal), hal)
                x = jnp.concatenate([hal, x], axis=0)
            xs.append(x)
        blks = []
        for _ in blk_ins:
            blks.append(in_refs[k][...])
            k += 1
        ps = []
        for _ in params:
            ps.append(in_refs[k][...])
            k += 1
        row_vals, blk_vals, acc_vals, new_carries = body(xs, blks, ps, [c[...] for c in carry_refs], r)
        for ref, val in zip(out_refs[:n_row_out], row_vals):
            ref[...] = val.astype(ref.dtype)
        for ref, val in zip(out_refs[n_row_out:n_row_out + n_blk_out], blk_vals):
            ref[...] = val.astype(ref.dtype)
        for ref, val in zip(acc_refs, acc_vals):
            ref[...] += val
        for ref, val in zip(carry_refs, new_carries):
            ref[...] = val
        if n_side:
            @pl.when((pl.program_id(0) == ncol - 1) & (i == nblk - 1))
            def _():
                side.wait(side_in, side_out, side_sems)

    res = pl.pallas_call(
        kern,
        name=name,
        grid=(ncol, nblk),
        in_specs=in_specs + side.in_specs,
        out_specs=out_specs + side.out_specs,
        out_shape=out_shape + side.out_shape,
        scratch_shapes=[pltpu.VMEM(shape, F32) for shape in carries] + side.scratch,
        compiler_params=pltpu.CompilerParams(dimension_semantics=("arbitrary", "arbitrary"),
                                             vmem_limit_bytes=VMEM_LIMIT),
    )(*operands, *side.operands)
    return list(res)


def _out(w, dtype, total=None, col=0):
    return dict(w=w, dtype=dtype, total=w if total is None else total, col=col)


def seq_fwd(name, fn, n_rows, ts, row_ins, params, outs, state_shapes=(), ncol=1, side=()):
    def body(xs, blks, ps, carries, r):
        o, new_st = fn(xs, list(carries), ps)
        return o, list(carries), [], new_st

    res = _block_call(name, body, n_rows, ts, ncol, False, row_ins, [], params, outs,
                      [dict(shape=s, dtype=F32) for s in state_shapes], [], list(state_shapes), side)
    n_o, n_s = len(outs), len(state_shapes)
    return (res[:n_o], res[n_o:n_o + n_s]) + ((res[n_o + n_s:],) if side else ())


def seq_bwd(name, fn, n_rows, ts, row_ins, params, cots, saved_states=(), din_dtypes=None, ncol=1, din_specs=None,
            side=()):
    n_x, n_p, n_st = len(row_ins), len(params), len(saved_states)
    halo_idx = [k for k, s in enumerate(row_ins) if s["halo"]]
    state_shapes = [a.shape[1:] for a in saved_states]

    def body(xs_all, blks, ps, carries, r):
        xs, cot_vals = xs_all[:n_x], xs_all[n_x:]
        d_state, d_halo = carries[:n_st], carries[n_st:]
        (o, _), vjp = jax.vjp(lambda a, b, c: fn(a, b, c), xs, blks, ps)
        cot = [c.astype(v.dtype) for c, v in zip(cot_vals, o)]
        dxs, dst, dps = vjp((cot, list(d_state)))
        row_vals, new_halo = [], []
        for k, dx in enumerate(dxs):
            if k in halo_idx:
                hk = halo_idx.index(k)
                rows = dx.shape[0] - HALO
                tail = dx[rows:] + d_halo[hk]
                row_vals.append(jnp.concatenate([dx[HALO:rows], tail], axis=0))
                new_halo.append(dx[:HALO])
            else:
                row_vals.append(dx)
        return row_vals, [], list(dps), list(dst) + new_halo

    din_dtypes = din_dtypes or [F32] * n_x
    douts = []
    for k, s in enumerate(row_ins):
        total, col = (s["w"], 0) if din_specs is None or din_specs[k] is None else din_specs[k]
        douts.append(_out(s["w"], din_dtypes[k], total, col))
    accs = []
    for p in params:
        a = p["arr"]
        if p["col"] is None:
            accs.append(dict(shape=a.shape, total=None, col=None))
        else:
            accs.append(dict(shape=(a.shape[0], p["w"]), total=a.shape[1], col=p["col"]))
    carries = list(state_shapes) + [(HALO, row_ins[k]["w"]) for k in halo_idx]
    res = _block_call(name, body, n_rows, ts, ncol, True, list(row_ins) + list(cots), list(saved_states), params,
                      douts, [], accs, carries, side)
    return (res[:n_x], res[n_x:n_x + n_p]) + ((res[n_x + n_p:],) if side else ())


def matmul(name, a, b, mode, out_dtype, tm, tn, tk, side=()):
    side = _Side(side)
    n_side = len(side.jobs)
    if mode == "tn":
        (kk, m), n = a.shape, b.shape[1]
    else:
        (m, kk), n = a.shape, (b.shape[0] if mode == "nt" else b.shape[1])
    tm, tn, tk = min(tm, m), min(tn, n), min(tk, kk)
    nk = kk // tk
    assert m % tm == 0 and n % tn == 0 and kk % tk == 0, (name, a.shape, b.shape, tm, tn, tk)
    a_spec = pl.BlockSpec((tk, tm), lambda i, j, k: (k, i)) if mode == "tn" else pl.BlockSpec((tm, tk), lambda i, j, k: (i, k))
    b_spec = pl.BlockSpec((tn, tk), lambda i, j, k: (j, k)) if mode == "nt" else pl.BlockSpec((tk, tn), lambda i, j, k: (k, j))
    ca, cb = {"nn": (1, 0), "nt": (1, 1), "tn": (0, 0)}[mode]

    gm, gn = m // tm, n // tn

    def kern(*refs):
        a_ref, b_ref = refs[:2]
        side_in = refs[2:2 + n_side]
        o_ref = refs[2 + n_side]
        side_out = refs[3 + n_side:3 + 2 * n_side]
        acc_ref = refs[3 + 2 * n_side]
        side_sems = refs[4 + 2 * n_side:]
        i, j, k = pl.program_id(0), pl.program_id(1), pl.program_id(2)
        if n_side:
            @pl.when((i == 0) & (j == 0) & (k == 0))
            def _():
                side.start(side_in, side_out, side_sems)

        part = lax.dot_general(a_ref[...], b_ref[...], (((ca,), (cb,)), ((), ())), preferred_element_type=F32)
        if nk == 1:
            o_ref[...] = part.astype(o_ref.dtype)
        else:
            @pl.when(k == 0)
            def _():
                acc_ref[...] = part

            @pl.when(k > 0)
            def _():
                acc_ref[...] += part

            @pl.when(k == nk - 1)
            def _():
                o_ref[...] = acc_ref[...].astype(o_ref.dtype)

        if n_side:
            @pl.when((i == gm - 1) & (j == gn - 1) & (k == nk - 1))
            def _():
                side.wait(side_in, side_out, side_sems)

    semantics = ("arbitrary",) * 3 if n_side else ("parallel", "parallel", "arbitrary")
    res = pl.pallas_call(
        kern,
        name=name,
        grid=(gm, gn, nk),
        in_specs=[a_spec, b_spec] + side.in_specs,
        out_specs=[pl.BlockSpec((tm, tn), lambda i, j, k: (i, j))] + side.out_specs,
        out_shape=[jax.ShapeDtypeStruct((m, n), out_dtype)] + side.out_shape,
        scratch_shapes=[pltpu.VMEM((tm, tn) if nk > 1 else (8, 128), F32)] + side.scratch,
        compiler_params=pltpu.CompilerParams(dimension_semantics=semantics, vmem_limit_bytes=VMEM_LIMIT),
    )(a, b, *side.operands)
    return (res[0], list(res[1:])) if n_side else res[0]


def all_gather(name, x, in_vmem):
    def body(x_ref, out_ref, send_sems, recv_sems, local_sem):
        x_, y_, c_ = _my_pos()
        me, sibling = (x_, y_, c_), (x_, y_, 1 - c_)
        chips = [(1 - x_, y_), (x_, 1 - y_), (1 - x_, 1 - y_)]

        def slot(px, py, pc):
            return out_ref.at[4 * px + 2 * py + pc]

        def copy(k, block, to, src=None):
            return pltpu.make_async_remote_copy(
                src_ref=slot(*block) if src is None else src, dst_ref=slot(*block),
                send_sem=send_sems.at[k], recv_sem=recv_sems.at[k], device_id=to, device_id_type=MESH)

        mine = pltpu.make_async_copy(x_ref, slot(*me), local_sem)
        mine.start()
        first = [copy(0, me, sibling, src=x_ref)]
        first += [copy(1 + j, me, (*chip, c_), src=x_ref) for j, chip in enumerate(chips)]
        for cp in first:
            cp.start()
        passed = [copy(4 + j, (*chip, c_), sibling) for j, chip in enumerate(chips)]
        for j, chip in enumerate(chips):
            copy(1 + j, (*chip, c_), me).wait_recv()
            passed[j].start()
        copy(0, sibling, me).wait_recv()
        for j, chip in enumerate(chips):
            copy(4 + j, (*chip, 1 - c_), me).wait_recv()
        for cp in first + passed:
            cp.wait_send()
        mine.wait()

    space = pltpu.VMEM if in_vmem else pl.ANY
    return pl.pallas_call(
        body,
        name=name,
        out_shape=jax.ShapeDtypeStruct((N_DEV,) + x.shape, x.dtype),
        in_specs=[pl.BlockSpec(memory_space=space)],
        out_specs=pl.BlockSpec(memory_space=space),
        scratch_shapes=[pltpu.SemaphoreType.DMA((7,)), pltpu.SemaphoreType.DMA((7,)), pltpu.SemaphoreType.DMA],
        compiler_params=pltpu.CompilerParams(vmem_limit_bytes=VMEM_LIMIT),
    )(x)


def all_to_all(name, g):
    def body(g_ref, out_ref, send_sems, recv_sems, local_sem):
        x_, y_, c_ = _my_pos()
        me = 4 * x_ + 2 * y_ + c_

        def peer(k):
            fx, fy, fc = (k >> 2) & 1, (k >> 1) & 1, k & 1
            return (1 - x_ if fx else x_, 1 - y_ if fy else y_, 1 - c_ if fc else c_)

        def copy(k):
            px, py, pc = peer(k)
            return pltpu.make_async_remote_copy(
                src_ref=g_ref.at[4 * px + 2 * py + pc], dst_ref=out_ref.at[me],
                send_sem=send_sems.at[k - 1], recv_sem=recv_sems.at[k - 1], device_id=(px, py, pc), device_id_type=MESH)

        def landing(k):
            px, py, pc = peer(k)
            return pltpu.make_async_remote_copy(
                src_ref=g_ref.at[me], dst_ref=out_ref.at[4 * px + 2 * py + pc],
                send_sem=send_sems.at[k - 1], recv_sem=recv_sems.at[k - 1], device_id=(px, py, pc), device_id_type=MESH)

        mine = pltpu.make_async_copy(g_ref.at[me], out_ref.at[me], local_sem)
        mine.start()
        sends = [copy(k) for k in range(1, N_DEV)]
        for cp in sends:
            cp.start()
        for k in range(1, N_DEV):
            landing(k).wait_recv()
        for cp in sends:
            cp.wait_send()
        mine.wait()

    return pl.pallas_call(
        body,
        name=name,
        out_shape=jax.ShapeDtypeStruct(g.shape, g.dtype),
        in_specs=[pl.BlockSpec(memory_space=pl.ANY)],
        out_specs=pl.BlockSpec(memory_space=pl.ANY),
        scratch_shapes=[pltpu.SemaphoreType.DMA((7,)), pltpu.SemaphoreType.DMA((7,)), pltpu.SemaphoreType.DMA],
    )(g)


def sum_blocks(name, g):
    def body(g_ref, o_ref):
        acc = g_ref[0]
        for s in range(1, N_DEV):
            acc = acc + g_ref[s]
        o_ref[...] = acc

    r = g.shape[1]
    tr = r // 4 if r % 32 == 0 else r
    return pl.pallas_call(
        body, name=name, grid=(r // tr,),
        in_specs=[pl.BlockSpec((N_DEV, tr, 128), lambda i: (0, i, 0))],
        out_specs=pl.BlockSpec((tr, 128), lambda i: (i, 0)),
        out_shape=jax.ShapeDtypeStruct((r, 128), F32),
        compiler_params=pltpu.CompilerParams(vmem_limit_bytes=VMEM_LIMIT),
    )(g)


def _adamw_math(w, g, m, v):
    m = ADAM_B1 * m + (1.0 - ADAM_B1) * g
    v = ADAM_B2 * v + (1.0 - ADAM_B2) * (g * g)
    m_hat = m / (1.0 - ADAM_B1 ** ADAM_STEP)
    v_hat = v / (1.0 - ADAM_B2 ** ADAM_STEP)
    delta = -ADAM_LR * (m_hat / (jnp.sqrt(v_hat) + ADAM_EPS) + ADAM_WD * w)
    return delta, m, v


ADAMW_BLOCK_BYTES = 3 << 19


def _row_tile(rows, row_bytes, limit):
    best = 8
    for t in range(8, rows + 1, 8):
        if rows % t == 0 and t * row_bytes <= limit:
            best = t
    return best


def adamw_big(name, w, lands, m, v):
    depth, r, c = w.shape
    outs = None
    for (l, part), land in sorted(lands.items()):
        rows = land.shape[1]
        tr = _row_tile(rows, 4 * (-(-c // 128) * 128), ADAMW_BLOCK_BYTES)
        first = part * rows // tr

        def body(w_ref, l_ref, m_ref, v_ref, *rest):
            g_out, d_out, m_out, v_out = rest[-4:]
            g = l_ref[0].astype(F32)
            for s in range(1, N_DEV):
                g = g + l_ref[s].astype(F32)
            delta, m_new, v_new = _adamw_math(w_ref[...], g, m_ref[...], v_ref[...])
            g_out[...] = g
            d_out[...] = delta
            m_out[...] = m_new
            v_out[...] = v_new

        spec = pl.BlockSpec((None, tr, c), lambda i, l=l, first=first: (l, first + i, 0))
        carried = [] if outs is None else list(outs)
        outs = pl.pallas_call(
            body, name=f"{name}_{l}_{part}", grid=(rows // tr,),
            in_specs=[spec, pl.BlockSpec((N_DEV, tr, c), lambda i: (0, i, 0)), spec, spec]
            + [pl.BlockSpec(memory_space=pl.ANY)] * len(carried),
            out_specs=[spec] * 4,
            out_shape=[jax.ShapeDtypeStruct((depth, r, c), F32)] * 4,
            input_output_aliases={4 + k: k for k in range(len(carried))},
            compiler_params=pltpu.CompilerParams(dimension_semantics=("parallel",), vmem_limit_bytes=VMEM_LIMIT),
        )(w, land, m, v, *carried)
    return outs


def adamw_small(name, ws, gs, ms, vs):
    n = len(ws)

    def body(*refs):
        ins, outs = refs[:4 * n], refs[4 * n:]
        for k in range(n):
            delta, m_new, v_new = _adamw_math(ins[k][...], ins[n + k][...], ins[2 * n + k][...], ins[3 * n + k][...])
            outs[k][...] = delta
            outs[n + k][...] = m_new
            outs[2 * n + k][...] = v_new

    res = pl.pallas_call(
        body, name=name,
        out_shape=[jax.ShapeDtypeStruct(w.shape, F32) for w in ws] * 3,
        compiler_params=pltpu.CompilerParams(vmem_limit_bytes=VMEM_LIMIT),
    )(*ws, *gs, *ms, *vs)
    return res[:n], res[n:2 * n], res[2 * n:]


def cast_bf16(name, w):
    depth, r, c = w.shape
    tr = _row_tile(r, 4 * (-(-c // 128) * 128), ADAMW_BLOCK_BYTES)

    def body(w_ref, o_ref):
        o_ref[...] = w_ref[...].astype(BF16)

    spec = pl.BlockSpec((None, tr, c), lambda l, i: (l, i, 0))
    return pl.pallas_call(body, name=name, grid=(depth, r // tr), in_specs=[spec], out_specs=spec,
                          out_shape=jax.ShapeDtypeStruct((depth, r, c), BF16),
                          compiler_params=pltpu.CompilerParams(dimension_semantics=("parallel", "parallel")))(w)


def _rows_of(shape):
    return -(-math.prod(shape) // 128)


def _pack(arrs):
    pieces = []
    for a in arrs:
        flat = a.reshape(-1).astype(F32)
        pieces.append(jnp.pad(flat, (0, (-flat.shape[0]) % 128)).reshape(-1, 128))
    rows = sum(p.shape[0] for p in pieces)
    if rows % 8:
        pieces.append(jnp.zeros((8 - rows % 8, 128), F32))
    return jnp.concatenate(pieces, axis=0)


def _unpack(packed, shapes, lead=()):
    out, r0 = [], 0
    for s in shapes:
        rows, n = _rows_of(s), math.prod(s)
        piece = packed[..., r0:r0 + rows, :].reshape(lead + (rows * 128,))
        out.append(piece[..., :n].reshape(lead + tuple(s)))
        r0 += rows
    return out


def _block_diag(w):
    h, d, _ = w.shape
    eye = jnp.eye(h, dtype=w.dtype)
    return (eye[:, None, :, None] * w[:, :, None, :]).reshape(h * d, h * d)


def _block_diag_grad(g, h):
    d = g.shape[0] // h
    eye = jnp.eye(h, dtype=g.dtype)
    return jnp.sum(g.reshape(h, d, h, d) * eye[:, None, :, None], axis=2)


def _layer_params(wt, l):
    gp = jnp.pad(jnp.stack([wt["gdn_a_log"][l], wt["gdn_dt_bias"][l]]), ((0, 6), (4, 128 - 4 - GDN_HEADS)))
    d_ffh = wt["ffn_conv_w"].shape[-1] // 2
    return dict(
        pre_mix=wt["pre_mix_norm"][l][None], post_mix=wt["post_mix_norm"][l][None],
        pre_ffn=wt["pre_ffn_norm"][l][None], post_ffn=wt["post_ffn_norm"][l][None],
        gdn_cw=wt["gdn_conv_w"][l], gdn_gp=gp, gdn_nw=wt["gdn_norm_w"][l][None],
        lru_cw=wt["lru_conv_w"][l], lru_cb=wt["lru_conv_b"][l][None],
        lru_wa=_block_diag(wt["lru_wa"][l]), lru_ba=wt["lru_ba"][l].reshape(1, -1),
        lru_wx=_block_diag(wt["lru_wx"][l]), lru_bx=wt["lru_bx"][l].reshape(1, -1),
        lru_lam=wt["lru_lambda"][l][None], gw0=wt["grp_norm_w"][l, 0][None], gw1=wt["grp_norm_w"][l, 1][None],
        gw2=wt["grp_norm_w"][l, 2][None],
        sgu_lnw=wt["sgu_ln_w"][l][None], sgu_lnb=wt["sgu_ln_b"][l][None],
        sgu_ws=wt["sgu_ws"][l].reshape(SGU_GROUPS * SGU_CHUNK, SGU_CHUNK),
        sgu_bt=jnp.pad(wt["sgu_b"][l].T, ((0, 0), (0, 128 - SGU_GROUPS))),
        sc_cw=wt["sconv_w"][l],
        ffn_cw=_pair_tiles(wt["ffn_conv_w"][l]), ffn_cb=_pair_tiles(wt["ffn_conv_b"][l][None]), d_ffh=d_ffh,
    )


TS_ROW = 256
TS_GDN = 256
TS_FFN = 512
TS_FFN_BWD = 256
TC_FFN = 512


def _mixers_fwd(l, p, lp, n, side, side_lru):
    qkv = _rin(p, 3 * D_G, 0, halo=True)
    z = _rin(p, D_G, 3)
    ba = _rin(p, 128, BA_COL // 128)
    gdn_ps = [_par(lp["gdn_cw"]), _par(lp["gdn_gp"]), _par(lp["gdn_nw"])]
    res = seq_fwd(f"gdn_fwd_{l}", fn_gdn, n, TS_GDN, [qkv, z, ba], gdn_ps, [_out(D_G, BF16)],
                  state_shapes=[(GDN_HEADS * GDN_DIM, GDN_DIM)], side=side)
    (y_a,), (gdn_st,), side_res = res if side else res + ([],)
    lru_x = _rin(p, D_G, 4, halo=True)
    lru_gate = _rin(p, D_G, 5)
    lru_ps = [_par(lp[k]) for k in ("lru_cw", "lru_cb", "lru_wa", "lru_ba", "lru_wx", "lru_bx", "lru_lam", "gw0")]
    res = seq_fwd(f"lru_fwd_{l}", fn_lru, n, TS_ROW, [lru_x, lru_gate], lru_ps, [_out(D_G, BF16)],
                  state_shapes=[(1, D_G)], side=side_lru)
    (y_b,), (lru_st,), side_res_lru = res if side_lru else res + ([],)
    uv = _rin(p, 2 * D_G, 3)
    sgu_ps = [_par(lp[k]) for k in ("sgu_lnw", "sgu_lnb", "sgu_ws", "sgu_bt", "gw1")]
    (y_c,), _ = seq_fwd(f"sgu_fwd_{l}", fn_sgu, n, TS_ROW, [uv], sgu_ps, [_out(D_G, BF16)])
    sc = [_rin(p, D_G, 8), _rin(p, D_G, 9, halo=True), _rin(p, D_G, 10, halo=True)]
    sc_ps = [_par(lp["sc_cw"]), _par(lp["gw2"])]
    (y_d,), _ = seq_fwd(f"sconv_fwd_{l}", fn_sconv, n, TS_ROW, sc, sc_ps, [_out(D_G, BF16)])
    ins = dict(gdn=([qkv, z, ba], gdn_ps, [gdn_st]), lru=([lru_x, lru_gate], lru_ps, [lru_st]),
               sgu=([uv], sgu_ps, []), sc=(sc, sc_ps, []))
    return jnp.concatenate([y_a, y_b, y_c, y_d], axis=1), ins, side_res, side_res_lru


def _mixers_bwd(l, dymix, ins, n, side):
    cot = lambda g: [_rin(dymix, D_G, g)]
    xs, ps, st = ins["gdn"]
    res = seq_bwd(f"gdn_bwd_{l}", fn_gdn, n, TS_GDN, xs, ps, cot(0), st, [BF16, BF16, BF16], side=side)
    (dqkv, dz, dba), g_gdn, side_res = res if side else res + ([],)
    xs, ps, st = ins["lru"]
    (dlx, dlg), g_lru = seq_bwd(f"lru_bwd_{l}", fn_lru, n, TS_ROW, xs, ps, cot(1), st, [BF16, BF16])
    xs, ps, st = ins["sgu"]
    (duv,), g_sgu = seq_bwd(f"sgu_bwd_{l}", fn_sgu, n, TS_ROW, xs, ps, cot(2), st, [BF16])
    xs, ps, st = ins["sc"]
    (dsb, dsc, dsh), g_sc = seq_bwd(f"sconv_bwd_{l}", fn_sconv, n, TS_ROW, xs, ps, cot(3), st, [BF16, BF16, BF16])
    dp = jnp.concatenate([dqkv, dz, dlx, dlg, duv, dsb, dsc, dsh, dba], axis=1)
    return dp, dict(gdn=g_gdn, lru=g_lru, sgu=g_sgu, sc=g_sc), side_res


def _pair_tiles(a):
    lead, n = a.shape[:-1], a.shape[-1] // (2 * TC_FFN)
    return jnp.swapaxes(a.reshape(lead + (2, n, TC_FFN)), -3, -2).reshape(a.shape)


def _unpair_tiles(a):
    lead, n = a.shape[:-1], a.shape[-1] // (2 * TC_FFN)
    return jnp.swapaxes(a.reshape(lead + (n, 2, TC_FFN)), -3, -2).reshape(a.shape)


def _ffn_ops(hid, lp):
    xs = [_rin(hid, 2 * TC_FFN, lambda j: j, halo=True)]
    ps = [_par(lp["ffn_cw"], 2 * TC_FFN, lambda j: j), _par(lp["ffn_cb"], 2 * TC_FFN, lambda j: j)]
    return xs, ps, lp["d_ffh"]


_FROM_BLOCKS = dict(
    w_in=lambda b: _regroup_w_in(b.transpose(1, 0, 2).reshape(b.shape[1], -1)),
    ffn_up=lambda b: _pair_tiles(b.transpose(1, 0, 2).reshape(b.shape[1], -1)),
    w_out=lambda b: b.reshape(-1, b.shape[2]),
    ffn_down=lambda b: b.reshape(-1, b.shape[2]),
)
_TO_BLOCKS = dict(
    w_in=lambda g: _ungroup_w_in(g).reshape(g.shape[0], N_DEV, -1).transpose(1, 0, 2),
    ffn_up=lambda g: _unpair_tiles(g).reshape(g.shape[0], N_DEV, -1).transpose(1, 0, 2),
    w_out=lambda g: g.reshape(N_DEV, -1, g.shape[1]),
    ffn_down=lambda g: g.reshape(N_DEV, -1, g.shape[1]),
)


class _Traffic:
    PARTS = dict(w_in=1, w_out=1, ffn_up=2, ffn_down=1)

    def __init__(self, whole=None, shards=None):
        self.whole = dict(whole or {})
        self.shards = shards
        self.gathered = {}
        self.pending = {}
        self.landed = {}

    def _rows(self, key):
        name, l, part = key
        rows = self.shards[name].shape[1] // self.PARTS[name]
        return slice(part * rows, (part + 1) * rows)

    def jobs(self, gather=(), exchange=()):
        if self.shards is None:
            return [], []
        keys = [("gather", k) for k in gather if k not in self.gathered and k[:2] not in self.whole]
        keys += [("exchange", k) for k in exchange if k in self.pending]
        jobs = [(kind, self.shards[k[0]][k[1]][self._rows(k)] if kind == "gather" else self.pending[k])
                for kind, k in keys]
        return jobs, keys

    def done(self, keys, results):
        for (kind, k), r in zip(keys, results):
            if kind == "gather":
                self.gathered[k] = r
            else:
                self.landed[k] = r
                del self.pending[k]

    def weight(self, name, l):
        if (name, l) not in self.whole:
            parts = []
            for part in range(self.PARTS[name]):
                k = (name, l, part)
                if k not in self.gathered:
                    self.gathered[k] = all_gather(f"gather_{name}_{l}_{part}", self.shards[name][l][self._rows(k)], False)
                parts.append(self.gathered[k])
            blocks = parts[0] if len(parts) == 1 else jnp.concatenate(parts, axis=1)
            self.whole[(name, l)] = _FROM_BLOCKS[name](blocks)
        return self.whole[(name, l)]

    def grad(self, name, l, g):
        if self.shards is None:
            self.landed[(name, l)] = g
            return
        blocks = _TO_BLOCKS[name](g)
        for part in range(self.PARTS[name]):
            k = (name, l, part)
            self.pending[k] = blocks[:, self._rows(k)]

    def flush(self):
        for (name, l, part), blocks in list(self.pending.items()):
            self.landed[(name, l, part)] = all_to_all(f"exchange_{name}_{l}_{part}", blocks)
            del self.pending[(name, l, part)]


def local_step(x, target, wt, tr):
    n, d = x.shape
    depth = wt["pre_mix_norm"].shape[0]
    lps = [_layer_params(wt, l) for l in range(depth)]
    saved = []
    xin = x

    def mm(name, a, b, mode, dtype, tm, tn, tk, gather=(), exchange=()):
        jobs, keys = tr.jobs(gather, exchange)
        if not jobs:
            return matmul(name, a, b, mode, dtype, tm, tn, tk)
        out, res = matmul(name, a, b, mode, dtype, tm, tn, tk, side=jobs)
        tr.done(keys, res)
        return out

    (h,), _ = seq_fwd("norm_fwd", fn_norm, n, TS_ROW, [_rin(x)], [_par(lps[0]["pre_mix"])], [_out(d, BF16)])
    dx_last = loss = None
    for l in range(depth):
        lp = lps[l]
        p = mm(f"w_in_fwd_{l}", h, tr.weight("w_in", l), "nn", F32, 1024, N_INP // 5, d,
               gather=[("ffn_up", l, 0)])
        jobs, keys = tr.jobs(gather=[("ffn_up", l, 1)])
        jobs_lru, keys_lru = tr.jobs(gather=[("w_out", l, 0)])
        ymix, mix_ins, res, res_lru = _mixers_fwd(l, p, lp, n, jobs, jobs_lru)
        tr.done(keys, res)
        tr.done(keys_lru, res_lru)
        y = mm(f"w_out_fwd_{l}", ymix, tr.weight("w_out", l), "nn", F32, 1024, 1024, d)
        res_ps = [_par(lp["post_mix"]), _par(lp["pre_ffn"])]
        (x1, h2), _ = seq_fwd(f"res_mix_fwd_{l}", fn_res, n, TS_ROW, [_rin(xin), _rin(y)], res_ps,
                              [_out(d, F32), _out(d, BF16)])
        nxt = l + 1 < depth
        hid = mm(f"ffn_up_fwd_{l}", h2, tr.weight("ffn_up", l), "nn", F32, 1024, 1024, d,
                 gather=[("ffn_down", l, 0)] + ([("w_out", l + 1, 0)] if nxt else []))
        f_xs, f_ps, d_ffh = _ffn_ops(hid, lp)
        (act,), _ = seq_fwd(f"ffn_act_fwd_{l}", fn_ffn, n, TS_FFN, f_xs, f_ps,
                            [_out(TC_FFN, BF16, d_ffh, lambda j: j)], ncol=d_ffh // TC_FFN)
        yf = mm(f"ffn_down_fwd_{l}", act, tr.weight("ffn_down", l), "nn", F32, 1024, 1024, d_ffh // 2,
                gather=[("w_in", l + 1, 0)] if nxt else [])
        rec = dict(x=xin, h=h, mix_ins=mix_ins, ymix=ymix, y=y, x1=x1, h2=h2, f_xs=f_xs, f_ps=f_ps, act=act, yf=yf)
        if l + 1 < depth:
            ps = [_par(lp["post_ffn"]), _par(lps[l + 1]["pre_mix"])]
            (x2, h), _ = seq_fwd(f"res_ffn_fwd_{l}", fn_res, n, TS_ROW, [_rin(x1), _rin(yf)], ps,
                                 [_out(d, F32), _out(d, BF16)])
            rec["res_ffn_ps"] = ps
            xin = x2
        else:
            def body(xs, blks, ps, carries, r):
                x1_, yf_, t_ = xs
                e = x1_ + _rms(yf_, ps[0]) - t_
                part = 0.5 * jnp.sum(jnp.mean(e * e, axis=-1, keepdims=True), axis=0, keepdims=True)
                return [e * (1.0 / d)], [], [jnp.broadcast_to(part, (8, 128))], []

            ps = [_par(lp["post_ffn"])]
            dx_last, loss = _block_call("loss_fwd", body, n, TS_ROW, 1, False, [_rin(x1), _rin(yf), _rin(target)],
                                        [], ps, [_out(d, F32)], [], [dict(shape=(8, 128), total=None, col=None)], [])
            rec["res_ffn_ps"] = ps
        saved.append(rec)

    grads = {}
    dx2, dh_next = dx_last, None
    for l in reversed(range(depth)):
        rec, lp = saved[l], lps[l]
        d_ffh = lp["d_ffh"]
        g = {}
        if dh_next is None:
            (dx1, dyf), (g["post_ffn"],) = seq_bwd(f"res_ffn_bwd_{l}", fn_res_last, n, TS_ROW,
                                                   [_rin(rec["x1"]), _rin(rec["yf"])], rec["res_ffn_ps"], [_rin(dx2)],
                                                   din_dtypes=[F32, BF16])
        else:
            (dx1, dyf), (g["post_ffn"], g_next_pre) = seq_bwd(
                f"res_ffn_bwd_{l}", fn_res, n, TS_ROW, [_rin(rec["x1"]), _rin(rec["yf"])], rec["res_ffn_ps"],
                [_rin(dx2), _rin(dh_next)], din_dtypes=[F32, BF16])
            grads[l + 1]["pre_mix"] = g_next_pre
        dact = mm(f"ffn_down_dx_{l}", dyf, tr.weight("ffn_down", l), "nt", BF16, 1024, 512, d)
        tr.grad("ffn_down", l, mm(f"ffn_down_dw_{l}", rec["act"], dyf, "tn", BF16, d_ffh // 4, 1024, 2048))
        (dhid,), (g["ffn_cw"], g["ffn_cb"]) = seq_bwd(
            f"ffn_act_bwd_{l}", fn_ffn, n, TS_FFN_BWD, rec["f_xs"], rec["f_ps"], [_rin(dact, TC_FFN, lambda j: j)],
            din_dtypes=[BF16], ncol=d_ffh // TC_FFN, din_specs=[(2 * d_ffh, lambda j: j)])
        dh2 = mm(f"ffn_up_dx_{l}", dhid, tr.weight("ffn_up", l), "nt", BF16, 1024, 1024, d_ffh // 2,
                 exchange=[("ffn_down", l, 0)])
        tr.grad("ffn_up", l, mm(f"ffn_up_dw_{l}", rec["h2"], dhid, "tn", BF16, 1024, 1024, 2048))
        (dx, dy), (g["post_mix"], g["pre_ffn"]) = seq_bwd(
            f"res_mix_bwd_{l}", fn_res, n, TS_ROW, [_rin(rec["x"]), _rin(rec["y"])],
            [_par(lp["post_mix"]), _par(lp["pre_ffn"])], [_rin(dx1), _rin(dh2)], din_dtypes=[F32, BF16])
        dymix = mm(f"w_out_dx_{l}", dy, tr.weight("w_out", l), "nt", BF16, 1024, 1024, d)
        tr.grad("w_out", l, mm(f"w_out_dw_{l}", rec["ymix"], dy, "tn", BF16, 1024, 1024, 2048))
        jobs, keys = tr.jobs(exchange=[("ffn_up", l, 0), ("w_out", l, 0)])
        dp, g["mix"], res = _mixers_bwd(l, dymix, rec["mix_ins"], n, jobs)
        tr.done(keys, res)
        tr.grad("w_in", l, mm(f"w_in_dw_{l}", rec["h"], dp, "tn", BF16, 1024, N_INP // 5, 2048,
                              exchange=[("ffn_up", l, 1)]))
        dh = mm(f"w_in_dx_{l}", dp, tr.weight("w_in", l), "nt", BF16, 1024, 1024, N_INP // 3,
                exchange=[("w_in", l, 0)])
        grads[l] = g
        dx2, dh_next = dx, dh
    (grad_x,), (g_pre0,) = seq_bwd("norm_bwd", fn_norm_keep, n, TS_ROW, [_rin(x)], [_par(lps[0]["pre_mix"])],
                                   [_rin(dh_next), _rin(dx2)])
    grads[0]["pre_mix"] = g_pre0
    tr.flush()
    return loss[0, 0], grad_x, _name_grads(grads, depth)


def _name_grads(grads, depth):
    per = {k: [] for k in SMALL}
    for l in range(depth):
        g = grads[l]
        m = g["mix"]
        cw, gp, nw = m["gdn"]
        lcw, lcb, lwa, lba, lwx, lbx, llam, gw0 = m["lru"]
        lnw, lnb, ws, bst, gw1 = m["sgu"]
        scw, gw2 = m["sc"]
        per["pre_mix_norm"].append(g["pre_mix"][0])
        per["gdn_conv_w"].append(cw)
        per["gdn_a_log"].append(gp[0, 4:8])
        per["gdn_dt_bias"].append(gp[1, 4:8])
        per["gdn_norm_w"].append(nw[0])
        per["lru_conv_w"].append(lcw)
        per["lru_conv_b"].append(lcb[0])
        per["lru_wa"].append(_block_diag_grad(lwa, LRU_BLOCKS))
        per["lru_ba"].append(lba.reshape(LRU_BLOCKS, -1))
        per["lru_wx"].append(_block_diag_grad(lwx, LRU_BLOCKS))
        per["lru_bx"].append(lbx.reshape(LRU_BLOCKS, -1))
        per["lru_lambda"].append(llam[0])
        per["sgu_ln_w"].append(lnw[0])
        per["sgu_ln_b"].append(lnb[0])
        per["sgu_ws"].append(ws.reshape(SGU_GROUPS, SGU_CHUNK, SGU_CHUNK))
        per["sgu_b"].append(bst[:, :SGU_GROUPS].T)
        per["sconv_w"].append(scw)
        per["grp_norm_w"].append(jnp.concatenate([gw0, gw1, gw2], axis=0))
        per["post_mix_norm"].append(g["post_mix"][0])
        per["pre_ffn_norm"].append(g["pre_ffn"][0])
        per["ffn_conv_w"].append(_unpair_tiles(g["ffn_cw"]))
        per["ffn_conv_b"].append(_unpair_tiles(g["ffn_cb"])[0])
        per["post_ffn_norm"].append(g["post_ffn"][0])
    return {k: jnp.stack(v) for k, v in per.items()}


def _regroup_w_in(w):
    pad = jnp.zeros(w.shape[:-1] + (N_INP - N_IN,), w.dtype)
    return jnp.concatenate([w[..., :2048], w[..., 2056:], w[..., 2048:2056], pad], axis=-1)


def _ungroup_w_in(g):
    return jnp.concatenate([g[..., :2048], g[..., BA_COL:BA_COL + 8], g[..., 2048:BA_COL]], axis=-1)


def kernel(x, pre_mix_norm, w_in, gdn_conv_w, gdn_a_log, gdn_dt_bias, gdn_norm_w, lru_conv_w, lru_conv_b, lru_wa, lru_ba, lru_wx, lru_bx, lru_lambda, sgu_ln_w, sgu_ln_b, sgu_ws, sgu_b, sconv_w, grp_norm_w, w_out, post_mix_norm, pre_ffn_norm, ffn_up, ffn_conv_w, ffn_conv_b, ffn_down, post_ffn_norm, loss_target, m_pre_mix_norm, m_w_in, m_gdn_conv_w, m_gdn_a_log, m_gdn_dt_bias, m_gdn_norm_w, m_lru_conv_w, m_lru_conv_b, m_lru_wa, m_lru_ba, m_lru_wx, m_lru_bx, m_lru_lambda, m_sgu_ln_w, m_sgu_ln_b, m_sgu_ws, m_sgu_b, m_sconv_w, m_grp_norm_w, m_w_out, m_post_mix_norm, m_pre_ffn_norm, m_ffn_up, m_ffn_conv_w, m_ffn_conv_b, m_ffn_down, m_post_ffn_norm, v_pre_mix_norm, v_w_in, v_gdn_conv_w, v_gdn_a_log, v_gdn_dt_bias, v_gdn_norm_w, v_lru_conv_w, v_lru_conv_b, v_lru_wa, v_lru_ba, v_lru_wx, v_lru_bx, v_lru_lambda, v_sgu_ln_w, v_sgu_ln_b, v_sgu_ws, v_sgu_b, v_sconv_w, v_grp_norm_w, v_w_out, v_post_mix_norm, v_pre_ffn_norm, v_ffn_up, v_ffn_conv_w, v_ffn_conv_b, v_ffn_down, v_post_ffn_norm):
    args = locals()
    w_loc = {k: args[k] for k in WEIGHTS}
    m_loc = {k: args["m_" + k] for k in WEIGHTS}
    v_loc = {k: args["v_" + k] for k in WEIGHTS}
    depth = pre_mix_norm.shape[0]
    x_, y_, c_ = _my_pos()
    me = 4 * x_ + 2 * y_ + c_

    tr = _Traffic(shards={name: cast_bf16(f"cast_{name}", w_loc[name]) for name in BIG})
    wt = {k: w_loc[k] for k in SMALL}
    shard_shapes = [w_loc[k].shape for k in SHARDED_SMALL]
    gathered = all_gather("gather_small", _pack([w_loc[k] for k in SHARDED_SMALL]), True)
    for k, a in zip(SHARDED_SMALL, _unpack(gathered, shard_shapes, lead=(N_DEV,))):
        a = jnp.moveaxis(a, 0, -2)
        wt[k] = a.reshape(a.shape[:-2] + (-1,))

    loss_part, grad_x, g_full = local_step(x[0], loss_target[0], wt, tr)
    loss = lax.psum(loss_part, ("x", "y", "c"))

    outs_g, outs_d, outs_m, outs_v = {}, {}, {}, {}
    for name in BIG:
        lands = {(l, part): a for (n_, l, part), a in tr.landed.items() if n_ == name}
        outs_g[name], outs_d[name], outs_m[name], outs_v[name] = adamw_big(
            f"adamw_{name}", w_loc[name], lands, m_loc[name], v_loc[name])

    full_shapes = [g_full[k].shape for k in SMALL]
    g_all = all_gather("gather_small_grads", _pack([g_full[k] for k in SMALL]), True)
    g_sum = _unpack(sum_blocks("sum_small_grads", g_all), full_shapes)
    g_small = {}
    for k, g in zip(SMALL, g_sum):
        if k in SHARDED_SMALL:
            w = w_loc[k].shape[-1]
            g = lax.dynamic_slice_in_dim(g, me * w, w, axis=g.ndim - 1)
        g_small[k] = g
    d_s, m_s, v_s = adamw_small("adamw_small", [w_loc[k] for k in SMALL], [g_small[k] for k in SMALL],
                                [m_loc[k] for k in SMALL], [v_loc[k] for k in SMALL])
    for k_i, k in enumerate(SMALL):
        outs_g[k], outs_d[k], outs_m[k], outs_v[k] = g_small[k], d_s[k_i], m_s[k_i], v_s[k_i]

    return (loss, grad_x[None], *[outs_g[k] for k in WEIGHTS], *[outs_d[k] for k in WEIGHTS],
            *[outs_m[k] for k in WEIGHTS], *[outs_v[k] for k in WEIGHTS])
```

```python
import functools
import math

import jax
import jax.numpy as jnp
from jax import lax
from jax.experimental import pallas as pl
from jax.experimental.pallas import tpu as pltpu

F32 = jnp.float32
BF16 = jnp.bfloat16
EPS = 1e-6
HALO = 8
VMEM_LIMIT = 56 * 1024 * 1024
MESH = pl.DeviceIdType.MESH
N_DEV = 8

ADAM_LR, ADAM_B1, ADAM_B2, ADAM_EPS, ADAM_WD, ADAM_STEP = 0.001, 0.9, 0.999, 1e-08, 0.01, 10

GDN_HEADS, GDN_DIM, GDN_CHUNK = 4, 128, 64
SGU_GROUPS, SGU_CHUNK = 4, 128
LRU_BLOCKS, LRU_C = 8, 8.0
D_G = 512
N_IN = 5640
N_INP = 5760
BA_COL = 5632

SHARDED_SMALL = ("gdn_conv_w", "lru_conv_w", "sconv_w", "grp_norm_w", "ffn_conv_w")
BIG = ("w_in", "w_out", "ffn_up", "ffn_down")
WEIGHTS = ("pre_mix_norm", "w_in", "gdn_conv_w", "gdn_a_log", "gdn_dt_bias", "gdn_norm_w", "lru_conv_w",
           "lru_conv_b", "lru_wa", "lru_ba", "lru_wx", "lru_bx", "lru_lambda", "sgu_ln_w", "sgu_ln_b", "sgu_ws",
           "sgu_b", "sconv_w", "grp_norm_w", "w_out", "post_mix_norm", "pre_ffn_norm", "ffn_up", "ffn_conv_w",
           "ffn_conv_b", "ffn_down", "post_ffn_norm")
SMALL = tuple(n for n in WEIGHTS if n not in BIG)


def _dot(a, b, ca, cb):
    return lax.dot_general(a.astype(BF16), b.astype(BF16), (((ca,), (cb,)), ((), ())),
                           preferred_element_type=F32)


@jax.custom_vjp
def _mm(a, b):
    return _dot(a, b, 1, 0)


def _mm_f(a, b):
    return _dot(a, b, 1, 0), (a, b)


def _mm_b(res, g):
    a, b = res
    return _dot(g, b, 1, 1), _dot(a, g, 0, 0)


_mm.defvjp(_mm_f, _mm_b)


@jax.custom_vjp
def _mm_nt(a, b):
    return _dot(a, b, 1, 1)


def _mm_nt_f(a, b):
    return _dot(a, b, 1, 1), (a, b)


def _mm_nt_b(res, g):
    a, b = res
    return _dot(g, b, 1, 0), _dot(g, a, 0, 0)


_mm_nt.defvjp(_mm_nt_f, _mm_nt_b)


@jax.custom_vjp
def _mm_tn(a, b):
    return _dot(a, b, 0, 0)


def _mm_tn_f(a, b):
    return _dot(a, b, 0, 0), (a, b)


def _mm_tn_b(res, g):
    a, b = res
    return _dot(b, g, 1, 1), _dot(a, g, 1, 0)


_mm_tn.defvjp(_mm_tn_f, _mm_tn_b)


def _dot_exact(a, b, ca, cb):
    return lax.dot_general(a, b, (((ca,), (cb,)), ((), ())), precision=lax.Precision.HIGHEST,
                           preferred_element_type=F32)


@functools.partial(jax.custom_vjp, nondiff_argnums=(1,))
def _shift_rows(x, s):
    return pltpu.roll(x, s, 0)


def _shift_rows_f(x, s):
    return pltpu.roll(x, s, 0), None


def _shift_rows_b(s, _, g):
    return (pltpu.roll(g, (g.shape[0] - s) % g.shape[0], 0),)


_shift_rows.defvjp(_shift_rows_f, _shift_rows_b)


def _sigmoid(x):
    return 1.0 / (1.0 + jnp.exp(-x))


def _silu(x):
    return x * _sigmoid(x)


def _gelu(x):
    return 0.5 * x * (1.0 + jnp.tanh(0.7978845608028654 * (x + 0.044715 * (x * x * x))))


@jax.custom_vjp
def _softplus(x):
    e = jnp.exp(-jnp.abs(x))
    u = 1.0 + e
    log1p = jnp.where(u == 1.0, e, jnp.log(u) * (e / jnp.where(u == 1.0, 1.0, u - 1.0)))
    return jnp.maximum(x, 0.0) + log1p


def _softplus_f(x):
    return _softplus(x), x


def _softplus_b(x, g):
    return (g * _sigmoid(x),)


_softplus.defvjp(_softplus_f, _softplus_b)


def _neg_expm1(y):
    return -jnp.tanh(0.5 * y) * (jnp.exp(y) + 1.0)


def _rms(x, w):
    return x * lax.rsqrt(jnp.mean(x * x, axis=-1, keepdims=True) + EPS) * w


def _row(w, k):
    sel = lax.broadcasted_iota(jnp.int32, w.shape, 0) == k
    return jnp.sum(jnp.where(sel, w, 0.0), axis=0, keepdims=True)


def _col(x, j):
    sel = lax.broadcasted_iota(jnp.int32, x.shape, 1) == j
    return jnp.sum(jnp.where(sel, x, 0.0), axis=1, keepdims=True)


def _conv(x_ext, w, taps):
    acc = None
    for k in range(taps):
        s = taps - 1 - k
        t = (x_ext if s == 0 else _shift_rows(x_ext, s)) * _row(w, k)
        acc = t if acc is None else acc + t
    return acc[HALO:]


@jax.custom_vjp
def _scan(a, b, h0):
    n = a.shape[0]
    row = lax.broadcasted_iota(jnp.int32, a.shape, 0)
    s = 1
    while s < n:
        keep = row >= s
        a_sh = jnp.where(keep, pltpu.roll(a, s, 0), 1.0)
        b_sh = jnp.where(keep, pltpu.roll(b, s, 0), 0.0)
        b = a * b_sh + b
        a = a * a_sh
        s *= 2
    return b + a * h0


def _scan_f(a, b, h0):
    h = _scan(a, b, h0)
    return h, (a, h, h0)


def _scan_b(res, dh):
    a, h, h0 = res
    n = a.shape[0]
    row = lax.broadcasted_iota(jnp.int32, a.shape, 0)
    an = jnp.where(row < n - 1, pltpu.roll(a, n - 1, 0), 0.0)
    lam = dh
    s = 1
    while s < n:
        keep = row < n - s
        a_sh = jnp.where(keep, pltpu.roll(an, n - s, 0), 1.0)
        l_sh = jnp.where(keep, pltpu.roll(lam, n - s, 0), 0.0)
        lam = an * l_sh + lam
        an = an * a_sh
        s *= 2
    h_prev = jnp.where(row >= 1, pltpu.roll(h, 1, 0), h0)
    al = a * lam
    dh0 = jnp.sum(jnp.where(row == 0, al, 0.0), axis=0, keepdims=True)
    return lam * h_prev, lam, dh0


_scan.defvjp(_scan_f, _scan_b)


@jax.custom_vjp
def _unit_lower_inverses(ms):
    n = ms[0].shape[0]
    shape = ms[0].shape
    eye = (lax.broadcasted_iota(jnp.int32, shape, 0) == lax.broadcasted_iota(jnp.int32, shape, 1)).astype(F32)
    p = [-m for m in ms]
    t = [eye + a for a in p]
    steps = 1
    while 2 ** steps < n:
        p = [_mm(a, a) for a in p]
        t = [a + _mm(a, c) for a, c in zip(t, p)]
        steps += 1
    return t


def _unit_lower_inverses_f(ms):
    t = _unit_lower_inverses(ms)
    return t, t


def _unit_lower_inverses_b(t, dt):
    x = [_mm_nt(g, a) for g, a in zip(dt, t)]
    return ([-_mm_tn(a, c) for a, c in zip(t, x)],)


_unit_lower_inverses.defvjp(_unit_lower_inverses_f, _unit_lower_inverses_b)


def _last_row(x):
    sel = lax.broadcasted_iota(jnp.int32, x.shape, 0) == x.shape[0] - 1
    return jnp.sum(jnp.where(sel, x, 0.0), axis=0, keepdims=True)


def fn_norm(xs, st, ps):
    (x,), (w,) = xs, ps
    return [_rms(x, w).astype(BF16)], []


def fn_norm_keep(xs, st, ps):
    (x,), (w,) = xs, ps
    return [_rms(x, w).astype(BF16), x], []


def fn_res(xs, st, ps):
    (x, y), (w_post, w_next) = xs, ps
    x1 = x + _rms(y, w_post)
    return [x1, _rms(x1, w_next).astype(BF16)], []


def fn_res_last(xs, st, ps):
    (x, y), (w_post,) = xs, ps
    return [x + _rms(y, w_post)], []


def fn_gdn(xs, st, ps):
    qkv_ext, z, ba = xs
    (state,) = st
    cw, gp, nw = ps
    ts = z.shape[0]
    qkv = _silu(_conv(qkv_ext, cw, 4))
    beta_all = _sigmoid(ba)
    g_all = -jnp.exp(_row(gp, 0)) * _softplus(ba + _row(gp, 1))
    c_n = GDN_CHUNK
    ri = lax.broadcasted_iota(jnp.int32, (c_n, c_n), 0)
    ci = lax.broadcasted_iota(jnp.int32, (c_n, c_n), 1)
    causal, strict = ri >= ci, ri > ci
    tril = causal.astype(F32)
    lane = lax.broadcasted_iota(jnp.int32, (c_n, 128), 1)
    s_h = [state[GDN_DIM * h:GDN_DIM * (h + 1)] for h in range(GDN_HEADS)]
    n_c = ts // c_n
    pairs = [(c, h) for c in range(n_c) for h in range(GDN_HEADS)]
    every = lambda f, *lists: [f(*a) for a in zip(*lists)]

    def piece(c, h, base):
        return qkv[c * c_n:(c + 1) * c_n, base + GDN_DIM * h:base + GDN_DIM * (h + 1)]

    q = [piece(c, h, 0) for c, h in pairs]
    k = [piece(c, h, D_G) for c, h in pairs]
    v = [piece(c, h, 2 * D_G) for c, h in pairs]
    q = every(lambda t: t * lax.rsqrt(jnp.sum(t * t, axis=-1, keepdims=True) + EPS) * (GDN_DIM ** -0.5), q)
    k = every(lambda t: t * lax.rsqrt(jnp.sum(t * t, axis=-1, keepdims=True) + EPS), k)
    gcum_all = [_dot_exact(tril, g_all[c * c_n:(c + 1) * c_n], 1, 0) for c in range(n_c)]
    b = [_col(beta_all[c * c_n:(c + 1) * c_n], h) for c, h in pairs]
    gc = [_col(gcum_all[c], 4 + h) for c, h in pairs]
    gr = [_dot_exact((lane == 4 + h).astype(F32), gcum_all[c], 1, 1) for c, h in pairs]
    decay = every(lambda a, r: jnp.where(causal, jnp.exp(jnp.where(causal, a - r, 0.0)), 0.0), gc, gr)
    kb = every(lambda a, c: a * c, k, b)
    mk = every(lambda a, c, e: _mm_nt(jnp.concatenate([a, c], axis=0), e), kb, q, k)
    m = every(lambda a, dcy: jnp.where(strict, a[:c_n] * dcy, 0.0), mk, decay)
    attn = every(lambda a, dcy: jnp.where(causal, a[c_n:] * dcy, 0.0), mk, decay)
    t_ = _unit_lower_inverses(m)
    eg = every(jnp.exp, gc)
    wu = every(lambda t, a, e, c, d: _mm(t, jnp.concatenate([a * e, c * d], axis=1)), t_, kb, eg, v, b)
    g_last = every(_last_row, gc)
    k_g = every(lambda a, gl, g: a * jnp.exp(gl - g), k, g_last, gc)
    wq = every(lambda a, c, e: jnp.concatenate([a[:, :GDN_DIM], c * e], axis=0), wu, q, eg)
    u = [a[:, GDN_DIM:] for a in wu]
    gl = every(jnp.exp, g_last)

    o = []
    for c in range(n_c):
        idx = range(c * GDN_HEADS, (c + 1) * GDN_HEADS)
        ws = [_mm(wq[i], s_h[h]) for h, i in enumerate(idx)]
        v_new = [u[i] - ws[h][:c_n] for h, i in enumerate(idx)]
        av = [_mm(attn[i], v_new[h]) for h, i in enumerate(idx)]
        kv = [_mm_tn(k_g[i], v_new[h]) for h, i in enumerate(idx)]
        o += [ws[h][c_n:] + av[h] for h in range(GDN_HEADS)]
        s_h = [s_h[h] * gl[i] + kv[h] for h, i in enumerate(idx)]
    zz = [z[c * c_n:(c + 1) * c_n, GDN_DIM * h:GDN_DIM * (h + 1)] for c, h in pairs]
    y = every(lambda a, g: a * lax.rsqrt(jnp.mean(a * a, axis=-1, keepdims=True) + EPS) * nw * _silu(g), o, zz)
    rows = [jnp.concatenate(y[c * GDN_HEADS:(c + 1) * GDN_HEADS], axis=1) for c in range(n_c)]
    y = rows[0] if n_c == 1 else jnp.concatenate(rows, axis=0)
    return [y.astype(BF16)], [jnp.concatenate(s_h, axis=0)]


def fn_lru(xs, st, ps):
    x_ext, gate = xs
    (h0,) = st
    cw, cb, wa, ba, wx, bx, lam, gw = ps
    xc = _conv(x_ext, cw, 4) + cb
    r = _sigmoid(_mm(xc, wa) + ba)
    i = _sigmoid(_mm(xc, wx) + bx)
    log_a = -LRU_C * r * _softplus(-lam)
    a = jnp.exp(log_a)
    mult = jnp.sqrt(_neg_expm1(2.0 * log_a))
    h = _scan(a, mult * (i * xc), h0)
    y = _rms(h * _gelu(gate), gw)
    return [y.astype(BF16)], [_last_row(h)]


def fn_sgu(xs, st, ps):
    (uv,) = xs
    lnw, lnb, ws, bst, gw = ps
    ts = uv.shape[0]
    uvf = _gelu(uv)
    u, v = uvf[:, :D_G], uvf[:, D_G:]
    vc = v - jnp.mean(v, axis=-1, keepdims=True)
    v = vc * lax.rsqrt(jnp.mean(vc * vc, axis=-1, keepdims=True) + EPS) * lnw + lnb
    t_n = SGU_CHUNK
    tril = lax.broadcasted_iota(jnp.int32, (t_n, t_n), 0) >= lax.broadcasted_iota(jnp.int32, (t_n, t_n), 1)
    wg = [jnp.where(tril, ws[t_n * g:t_n * (g + 1)], 0.0) for g in range(SGU_GROUPS)]
    bg = [_col(bst, g) for g in range(SGU_GROUPS)]
    rows = []
    for c in range(ts // t_n):
        vcg = v[c * t_n:(c + 1) * t_n]
        rows.append(jnp.concatenate(
            [_mm(wg[g], vcg[:, 128 * g:128 * (g + 1)]) + bg[g] for g in range(SGU_GROUPS)], axis=1))
    vv = rows[0] if len(rows) == 1 else jnp.concatenate(rows, axis=0)
    return [_rms(u * vv, gw).astype(BF16)], []


def fn_sconv(xs, st, ps):
    bg, cg_ext, hh_ext = xs
    cw, gw = ps
    return [_rms(bg * _conv(cg_ext * hh_ext, cw, 3), gw).astype(BF16)], []


def fn_ffn(xs, st, ps):
    g_ext, v_ext = xs
    cwg, cwv, cbg, cbv = ps
    g = _conv(g_ext, cwg, 3) + cbg
    v = _conv(v_ext, cwv, 3) + cbv
    return [(_gelu(g) * v).astype(BF16)], []


def _my_pos():
    return lax.axis_index("x"), lax.axis_index("y"), lax.axis_index("c")


def _peer(pos, k):
    x_, y_, c_ = pos
    return (1 - x_ if (k >> 2) & 1 else x_, 1 - y_ if (k >> 1) & 1 else y_, 1 - c_ if k & 1 else c_)


def _dev_index(p):
    return 4 * p[0] + 2 * p[1] + p[2]


class _Side:
    def __init__(self, jobs):
        self.jobs = list(jobs)
        n = len(self.jobs)
        self.operands = [a for _, a in self.jobs]
        self.in_specs = [pl.BlockSpec(memory_space=pl.ANY)] * n
        self.out_shape = [jax.ShapeDtypeStruct(((N_DEV,) + a.shape) if kind == "gather" else a.shape, a.dtype)
                          for kind, a in self.jobs]
        self.out_specs = [pl.BlockSpec(memory_space=pl.ANY)] * n
        self.scratch = [pltpu.SemaphoreType.DMA((7 * n,)), pltpu.SemaphoreType.DMA((7 * n,)),
                        pltpu.SemaphoreType.DMA((n,))] if n else []

    def _copies(self, in_refs, out_refs, sems, landings=True):
        send, recv, local = sems
        pos = _my_pos()
        me = _dev_index(pos)
        mine, outgoing, landing = [], [], []
        for j, (kind, _) in enumerate(self.jobs):
            src, dst = in_refs[j], out_refs[j]
            own = src if kind == "gather" else src.at[me]
            mine.append(pltpu.make_async_copy(own, dst.at[me], local.at[j]))
            for k in range(1, N_DEV):
                p = _peer(pos, k)
                sems_k = dict(send_sem=send.at[7 * j + k - 1], recv_sem=recv.at[7 * j + k - 1], device_id=p,
                              device_id_type=MESH)
                outgoing.append(pltpu.make_async_remote_copy(
                    src_ref=src if kind == "gather" else src.at[_dev_index(p)], dst_ref=dst.at[me], **sems_k))
                if landings:
                    landing.append(pltpu.make_async_remote_copy(src_ref=own, dst_ref=dst.at[_dev_index(p)], **sems_k))
        return mine, outgoing, landing

    def start(self, in_refs, out_refs, sems):
        mine, outgoing, _ = self._copies(in_refs, out_refs, sems, landings=False)
        for cp in mine + outgoing:
            cp.start()

    def wait(self, in_refs, out_refs, sems):
        mine, outgoing, landing = self._copies(in_refs, out_refs, sems)
        for cp in landing:
            cp.wait_recv()
        for cp in outgoing:
            cp.wait_send()
        for cp in mine:
            cp.wait()


def _rin(arr, w=None, col=0, halo=False):
    return dict(arr=arr, w=arr.shape[1] if w is None else w, col=col, halo=halo)


def _par(arr, w=None, col=None, row=None):
    return dict(arr=arr, w=w, col=col, row=row)


def _colidx(col, j):
    return col(j) if callable(col) else col


def _block_call(name, body, n_rows, ts, ncol, reverse, row_ins, blk_ins, params, row_outs, blk_outs, acc_outs,
                carries, side=()):
    side = _Side(side)
    ts = min(ts, n_rows)
    nblk = n_rows // ts
    hb = ts // HALO

    def rr(i):
        return (nblk - 1 - i) if reverse else i

    in_specs, operands = [], []
    for s in row_ins:
        in_specs.append(pl.BlockSpec((ts, s["w"]), lambda j, i, s=s: (rr(i), _colidx(s["col"], j))))
        operands.append(s["arr"])
        if s["halo"]:
            in_specs.append(pl.BlockSpec((HALO, s["w"]),
                                         lambda j, i, s=s: (jnp.maximum(rr(i) * hb - 1, 0), _colidx(s["col"], j))))
            operands.append(s["arr"])
    for a in blk_ins:
        nd = a.ndim - 1
        in_specs.append(pl.BlockSpec((None,) + a.shape[1:], lambda j, i, nd=nd: (rr(i),) + (0,) * nd))
        operands.append(a)
    for p in params:
        a = p["arr"]
        if p["row"] is not None:
            in_specs.append(pl.BlockSpec((p["w"], a.shape[1]), lambda j, i, p=p: (_colidx(p["row"], j), 0)))
        elif p["col"] is None:
            in_specs.append(pl.BlockSpec(a.shape, lambda j, i: (0, 0)))
        else:
            in_specs.append(pl.BlockSpec((a.shape[0], p["w"]), lambda j, i, p=p: (0, _colidx(p["col"], j))))
        operands.append(a)

    out_specs, out_shape = [], []
    for o in row_outs:
        out_specs.append(pl.BlockSpec((ts, o["w"]), lambda j, i, o=o: (rr(i), _colidx(o["col"], j))))
        out_shape.append(jax.ShapeDtypeStruct((n_rows, o["total"]), o["dtype"]))
    for o in blk_outs:
        nd = len(o["shape"])
        out_specs.append(pl.BlockSpec((None,) + tuple(o["shape"]), lambda j, i, nd=nd: (rr(i),) + (0,) * nd))
        out_shape.append(jax.ShapeDtypeStruct((nblk,) + tuple(o["shape"]), o["dtype"]))
    for o in acc_outs:
        if o["col"] is None:
            out_specs.append(pl.BlockSpec(o["shape"], lambda j, i: (0, 0)))
            out_shape.append(jax.ShapeDtypeStruct(o["shape"], F32))
        else:
            out_specs.append(pl.BlockSpec(o["shape"], lambda j, i, o=o: (0, _colidx(o["col"], j))))
            out_shape.append(jax.ShapeDtypeStruct((o["shape"][0], o["total"]), F32))

    n_in = len(operands)
    n_row_out, n_blk_out, n_acc = len(row_outs), len(blk_outs), len(acc_outs)
    n_out = n_row_out + n_blk_out + n_acc
    n_side = len(side.jobs)

    def kern(*refs):
        in_refs = refs[:n_in]
        side_in = refs[n_in:n_in + n_side]
        out_refs = refs[n_in + n_side:n_in + n_side + n_out]
        side_out = refs[n_in + n_side + n_out:n_in + 2 * n_side + n_out]
        scratch = refs[n_in + 2 * n_side + n_out:]
        carry_refs, side_sems = scratch[:len(carries)], scratch[len(carries):]
        acc_refs = out_refs[n_row_out + n_blk_out:]
        i = pl.program_id(1)
        r = rr(i)
        if n_side:
            @pl.when((pl.program_id(0) == 0) & (i == 0))
            def _():
                side.start(side_in, side_out, side_sems)

        @pl.when(i == 0)
        def _():
            for c_ref in carry_refs:
                c_ref[...] = jnp.zeros(c_ref.shape, c_ref.dtype)
            for a_ref in acc_refs:
                a_ref[...] = jnp.zeros(a_ref.shape, a_ref.dtype)

        k = 0
        xs = []
        for s in row_ins:
            x = in_refs[k][...]
            k += 1
            if s["halo"]:
                hal = in_refs[k][...]
                k += 1
                hal = jnp.where(r == 0, jnp.zeros_like(hal), hal)
                x = jnp.concatenate([hal, x], axis=0)
            xs.append(x)
        blks = []
        for _ in blk_ins:
            blks.append(in_refs[k][...])
            k += 1
        ps = []
        for _ in params:
            ps.append(in_refs[k][...])
            k += 1
        row_vals, blk_vals, acc_vals, new_carries = body(xs, blks, ps, [c[...] for c in carry_refs], r)
        for ref, val in zip(out_refs[:n_row_out], row_vals):
            ref[...] = val.astype(ref.dtype)
        for ref, val in zip(out_refs[n_row_out:n_row_out + n_blk_out], blk_vals):
            ref[...] = val.astype(ref.dtype)
        for ref, val in zip(acc_refs, acc_vals):
            ref[...] += val
        for ref, val in zip(carry_refs, new_carries):
            ref[...] = val
        if n_side:
            @pl.when((pl.program_id(0) == ncol - 1) & (i == nblk - 1))
            def _():
                side.wait(side_in, side_out, side_sems)

    res = pl.pallas_call(
        kern,
        name=name,
        grid=(ncol, nblk),
        in_specs=in_specs + side.in_specs,
        out_specs=out_specs + side.out_specs,
        out_shape=out_shape + side.out_shape,
        scratch_shapes=[pltpu.VMEM(shape, F32) for shape in carries] + side.scratch,
        compiler_params=pltpu.CompilerParams(dimension_semantics=("arbitrary", "arbitrary"),
                                             vmem_limit_bytes=VMEM_LIMIT),
    )(*operands, *side.operands)
    return list(res)


def _out(w, dtype, total=None, col=0):
    return dict(w=w, dtype=dtype, total=w if total is None else total, col=col)


def seq_fwd(name, fn, n_rows, ts, row_ins, params, outs, state_shapes=(), ncol=1, side=()):
    def body(xs, blks, ps, carries, r):
        o, new_st = fn(xs, list(carries), ps)
        return o, list(carries), [], new_st

    res = _block_call(name, body, n_rows, ts, ncol, False, row_ins, [], params, outs,
                      [dict(shape=s, dtype=F32) for s in state_shapes], [], list(state_shapes), side)
    n_o, n_s = len(outs), len(state_shapes)
    return (res[:n_o], res[n_o:n_o + n_s]) + ((res[n_o + n_s:],) if side else ())


def seq_bwd(name, fn, n_rows, ts, row_ins, params, cots, saved_states=(), din_dtypes=None, ncol=1, din_specs=None,
            side=(), cot_map=None, aux=()):
    n_x, n_p, n_st = len(row_ins), len(params), len(saved_states)
    halo_idx = [k for k, s in enumerate(row_ins) if s["halo"]]
    state_shapes = [a.shape[1:] for a in saved_states]

    def body(xs_all, blks, ps, carries, r):
        xs, cot_vals = xs_all[:n_x], xs_all[n_x:]
        d_state, d_halo = carries[:n_st], carries[n_st:]
        if cot_map is not None:
            cot_vals = cot_map(cot_vals, ps[n_p:])
        (o, _), vjp = jax.vjp(lambda a, b, c: fn(a, b, c), xs, blks, ps[:n_p])
        cot = [c.astype(v.dtype) for c, v in zip(cot_vals, o)]
        dxs, dst, dps = vjp((cot, list(d_state)))
        row_vals, new_halo = [], []
        for k, dx in enumerate(dxs):
            if k in halo_idx:
                hk = halo_idx.index(k)
                rows = dx.shape[0] - HALO
                tail = dx[rows:] + d_halo[hk]
                row_vals.append(jnp.concatenate([dx[HALO:rows], tail], axis=0))
                new_halo.append(dx[:HALO])
            else:
                row_vals.append(dx)
        return row_vals, [], list(dps), list(dst) + new_halo

    din_dtypes = din_dtypes or [F32] * n_x
    douts = []
    for k, s in enumerate(row_ins):
        total, col = (s["w"], 0) if din_specs is None or din_specs[k] is None else din_specs[k]
        douts.append(_out(s["w"], din_dtypes[k], total, col))
    accs = []
    for p in params:
        a = p["arr"]
        if p["col"] is None:
            accs.append(dict(shape=a.shape, total=None, col=None))
        else:
            accs.append(dict(shape=(a.shape[0], p["w"]), total=a.shape[1], col=p["col"]))
    carries = list(state_shapes) + [(HALO, row_ins[k]["w"]) for k in halo_idx]
    res = _block_call(name, body, n_rows, ts, ncol, True, list(row_ins) + list(cots), list(saved_states),
                      list(params) + list(aux), douts, [], accs, carries, side)
    return (res[:n_x], res[n_x:n_x + n_p]) + ((res[n_x + n_p:],) if side else ())


def matmul(name, a, b, mode, out_dtype, tm, tn, tk, side=()):
    side = _Side(side)
    n_side = len(side.jobs)
    if mode == "tn":
        (kk, m), n = a.shape, b.shape[1]
    else:
        (m, kk), n = a.shape, (b.shape[0] if mode == "nt" else b.shape[1])
    tm, tn, tk = min(tm, m), min(tn, n), min(tk, kk)
    nk = kk // tk
    assert m % tm == 0 and n % tn == 0 and kk % tk == 0, (name, a.shape, b.shape, tm, tn, tk)
    a_spec = pl.BlockSpec((tk, tm), lambda i, j, k: (k, i)) if mode == "tn" else pl.BlockSpec((tm, tk), lambda i, j, k: (i, k))
    b_spec = pl.BlockSpec((tn, tk), lambda i, j, k: (j, k)) if mode == "nt" else pl.BlockSpec((tk, tn), lambda i, j, k: (k, j))
    ca, cb = {"nn": (1, 0), "nt": (1, 1), "tn": (0, 0)}[mode]

    gm, gn = m // tm, n // tn

    def kern(*refs):
        a_ref, b_ref = refs[:2]
        side_in = refs[2:2 + n_side]
        o_ref = refs[2 + n_side]
        side_out = refs[3 + n_side:3 + 2 * n_side]
        acc_ref = refs[3 + 2 * n_side]
        side_sems = refs[4 + 2 * n_side:]
        i, j, k = pl.program_id(0), pl.program_id(1), pl.program_id(2)
        if n_side:
            @pl.when((i == 0) & (j == 0) & (k == 0))
            def _():
                side.start(side_in, side_out, side_sems)

        part = lax.dot_general(a_ref[...], b_ref[...], (((ca,), (cb,)), ((), ())), preferred_element_type=F32)
        if nk == 1:
            o_ref[...] = part.astype(o_ref.dtype)
        else:
            @pl.when(k == 0)
            def _():
                acc_ref[...] = part

            @pl.when(k > 0)
            def _():
                acc_ref[...] += part

            @pl.when(k == nk - 1)
            def _():
                o_ref[...] = acc_ref[...].astype(o_ref.dtype)

        if n_side:
            @pl.when((i == gm - 1) & (j == gn - 1) & (k == nk - 1))
            def _():
                side.wait(side_in, side_out, side_sems)

    semantics = ("arbitrary",) * 3 if n_side else ("parallel", "parallel", "arbitrary")
    res = pl.pallas_call(
        kern,
        name=name,
        grid=(gm, gn, nk),
        in_specs=[a_spec, b_spec] + side.in_specs,
        out_specs=[pl.BlockSpec((tm, tn), lambda i, j, k: (i, j))] + side.out_specs,
        out_shape=[jax.ShapeDtypeStruct((m, n), out_dtype)] + side.out_shape,
        scratch_shapes=[pltpu.VMEM((tm, tn) if nk > 1 else (8, 128), F32)] + side.scratch,
        compiler_params=pltpu.CompilerParams(dimension_semantics=semantics, vmem_limit_bytes=VMEM_LIMIT),
    )(a, b, *side.operands)
    return (res[0], list(res[1:])) if n_side else res[0]


def all_gather(name, x, in_vmem):
    def body(x_ref, out_ref, send_sems, recv_sems, local_sem):
        x_, y_, c_ = _my_pos()
        me, sibling = (x_, y_, c_), (x_, y_, 1 - c_)
        chips = [(1 - x_, y_), (x_, 1 - y_), (1 - x_, 1 - y_)]

        def slot(px, py, pc):
            return out_ref.at[4 * px + 2 * py + pc]

        def copy(k, block, to, src=None):
            return pltpu.make_async_remote_copy(
                src_ref=slot(*block) if src is None else src, dst_ref=slot(*block),
                send_sem=send_sems.at[k], recv_sem=recv_sems.at[k], device_id=to, device_id_type=MESH)

        mine = pltpu.make_async_copy(x_ref, slot(*me), local_sem)
        mine.start()
        first = [copy(0, me, sibling, src=x_ref)]
        first += [copy(1 + j, me, (*chip, c_), src=x_ref) for j, chip in enumerate(chips)]
        for cp in first:
            cp.start()
        passed = [copy(4 + j, (*chip, c_), sibling) for j, chip in enumerate(chips)]
        for j, chip in enumerate(chips):
            copy(1 + j, (*chip, c_), me).wait_recv()
            passed[j].start()
        copy(0, sibling, me).wait_recv()
        for j, chip in enumerate(chips):
            copy(4 + j, (*chip, 1 - c_), me).wait_recv()
        for cp in first + passed:
            cp.wait_send()
        mine.wait()

    space = pltpu.VMEM if in_vmem else pl.ANY
    return pl.pallas_call(
        body,
        name=name,
        out_shape=jax.ShapeDtypeStruct((N_DEV,) + x.shape, x.dtype),
        in_specs=[pl.BlockSpec(memory_space=space)],
        out_specs=pl.BlockSpec(memory_space=space),
        scratch_shapes=[pltpu.SemaphoreType.DMA((7,)), pltpu.SemaphoreType.DMA((7,)), pltpu.SemaphoreType.DMA],
        compiler_params=pltpu.CompilerParams(vmem_limit_bytes=VMEM_LIMIT),
    )(x)


def all_to_all(name, g):
    def body(g_ref, out_ref, send_sems, recv_sems, local_sem):
        x_, y_, c_ = _my_pos()
        me = 4 * x_ + 2 * y_ + c_

        def peer(k):
            fx, fy, fc = (k >> 2) & 1, (k >> 1) & 1, k & 1
            return (1 - x_ if fx else x_, 1 - y_ if fy else y_, 1 - c_ if fc else c_)

        def copy(k):
            px, py, pc = peer(k)
            return pltpu.make_async_remote_copy(
                src_ref=g_ref.at[4 * px + 2 * py + pc], dst_ref=out_ref.at[me],
                send_sem=send_sems.at[k - 1], recv_sem=recv_sems.at[k - 1], device_id=(px, py, pc), device_id_type=MESH)

        def landing(k):
            px, py, pc = peer(k)
            return pltpu.make_async_remote_copy(
                src_ref=g_ref.at[me], dst_ref=out_ref.at[4 * px + 2 * py + pc],
                send_sem=send_sems.at[k - 1], recv_sem=recv_sems.at[k - 1], device_id=(px, py, pc), device_id_type=MESH)

        mine = pltpu.make_async_copy(g_ref.at[me], out_ref.at[me], local_sem)
        mine.start()
        sends = [copy(k) for k in range(1, N_DEV)]
        for cp in sends:
            cp.start()
        for k in range(1, N_DEV):
            landing(k).wait_recv()
        for cp in sends:
            cp.wait_send()
        mine.wait()

    return pl.pallas_call(
        body,
        name=name,
        out_shape=jax.ShapeDtypeStruct(g.shape, g.dtype),
        in_specs=[pl.BlockSpec(memory_space=pl.ANY)],
        out_specs=pl.BlockSpec(memory_space=pl.ANY),
        scratch_shapes=[pltpu.SemaphoreType.DMA((7,)), pltpu.SemaphoreType.DMA((7,)), pltpu.SemaphoreType.DMA],
    )(g)


def sum_blocks(name, g):
    def body(g_ref, o_ref):
        acc = g_ref[0]
        for s in range(1, N_DEV):
            acc = acc + g_ref[s]
        o_ref[...] = acc

    r = g.shape[1]
    tr = r // 4 if r % 32 == 0 else r
    return pl.pallas_call(
        body, name=name, grid=(r // tr,),
        in_specs=[pl.BlockSpec((N_DEV, tr, 128), lambda i: (0, i, 0))],
        out_specs=pl.BlockSpec((tr, 128), lambda i: (i, 0)),
        out_shape=jax.ShapeDtypeStruct((r, 128), F32),
        compiler_params=pltpu.CompilerParams(vmem_limit_bytes=VMEM_LIMIT),
    )(g)


def _adamw_math(w, g, m, v):
    m = ADAM_B1 * m + (1.0 - ADAM_B1) * g
    v = ADAM_B2 * v + (1.0 - ADAM_B2) * (g * g)
    m_hat = m / (1.0 - ADAM_B1 ** ADAM_STEP)
    v_hat = v / (1.0 - ADAM_B2 ** ADAM_STEP)
    delta = -ADAM_LR * (m_hat / (jnp.sqrt(v_hat) + ADAM_EPS) + ADAM_WD * w)
    return delta, m, v


ADAMW_BLOCK_BYTES = 3 << 19


def _row_tile(rows, row_bytes, limit):
    best = 8
    for t in range(8, rows + 1, 8):
        if rows % t == 0 and t * row_bytes <= limit:
            best = t
    return best


def adamw_big(name, w, lands, m, v):
    depth, r, c = w.shape
    outs = None
    for (l, part), land in sorted(lands.items()):
        rows = land.shape[1]
        tr = _row_tile(rows, 4 * (-(-c // 128) * 128), ADAMW_BLOCK_BYTES)
        first = part * rows // tr

        def body(w_ref, l_ref, m_ref, v_ref, *rest):
            g_out, d_out, m_out, v_out = rest[-4:]
            g = l_ref[0].astype(F32)
            for s in range(1, N_DEV):
                g = g + l_ref[s].astype(F32)
            delta, m_new, v_new = _adamw_math(w_ref[...], g, m_ref[...], v_ref[...])
            g_out[...] = g
            d_out[...] = delta
            m_out[...] = m_new
            v_out[...] = v_new

        spec = pl.BlockSpec((None, tr, c), lambda i, l=l, first=first: (l, first + i, 0))
        carried = [] if outs is None else list(outs)
        outs = pl.pallas_call(
            body, name=f"{name}_{l}_{part}", grid=(rows // tr,),
            in_specs=[spec, pl.BlockSpec((N_DEV, tr, c), lambda i: (0, i, 0)), spec, spec]
            + [pl.BlockSpec(memory_space=pl.ANY)] * len(carried),
            out_specs=[spec] * 4,
            out_shape=[jax.ShapeDtypeStruct((depth, r, c), F32)] * 4,
            input_output_aliases={4 + k: k for k in range(len(carried))},
            compiler_params=pltpu.CompilerParams(dimension_semantics=("parallel",), vmem_limit_bytes=VMEM_LIMIT),
        )(w, land, m, v, *carried)
    return outs


def adamw_small(name, ws, gs, ms, vs):
    n = len(ws)

    def body(*refs):
        ins, outs = refs[:4 * n], refs[4 * n:]
        for k in range(n):
            delta, m_new, v_new = _adamw_math(ins[k][...], ins[n + k][...], ins[2 * n + k][...], ins[3 * n + k][...])
            outs[k][...] = delta
            outs[n + k][...] = m_new
            outs[2 * n + k][...] = v_new

    res = pl.pallas_call(
        body, name=name,
        out_shape=[jax.ShapeDtypeStruct(w.shape, F32) for w in ws] * 3,
        compiler_params=pltpu.CompilerParams(vmem_limit_bytes=VMEM_LIMIT),
    )(*ws, *gs, *ms, *vs)
    return res[:n], res[n:2 * n], res[2 * n:]


def cast_bf16(name, w):
    depth, r, c = w.shape
    tr = _row_tile(r, 4 * (-(-c // 128) * 128), ADAMW_BLOCK_BYTES)

    def body(w_ref, o_ref):
        o_ref[...] = w_ref[...].astype(BF16)

    spec = pl.BlockSpec((None, tr, c), lambda l, i: (l, i, 0))
    return pl.pallas_call(body, name=name, grid=(depth, r // tr), in_specs=[spec], out_specs=spec,
                          out_shape=jax.ShapeDtypeStruct((depth, r, c), BF16),
                          compiler_params=pltpu.CompilerParams(dimension_semantics=("parallel", "parallel")))(w)


def _rows_of(shape):
    return -(-math.prod(shape) // 128)


def _pack(arrs):
    pieces = []
    for a in arrs:
        flat = a.reshape(-1).astype(F32)
        pieces.append(jnp.pad(flat, (0, (-flat.shape[0]) % 128)).reshape(-1, 128))
    rows = sum(p.shape[0] for p in pieces)
    if rows % 8:
        pieces.append(jnp.zeros((8 - rows % 8, 128), F32))
    return jnp.concatenate(pieces, axis=0)


def _unpack(packed, shapes, lead=()):
    out, r0 = [], 0
    for s in shapes:
        rows, n = _rows_of(s), math.prod(s)
        piece = packed[..., r0:r0 + rows, :].reshape(lead + (rows * 128,))
        out.append(piece[..., :n].reshape(lead + tuple(s)))
        r0 += rows
    return out


def _block_diag(w):
    h, d, _ = w.shape
    eye = jnp.eye(h, dtype=w.dtype)
    return (eye[:, None, :, None] * w[:, :, None, :]).reshape(h * d, h * d)


def _block_diag_grad(g, h):
    d = g.shape[0] // h
    eye = jnp.eye(h, dtype=g.dtype)
    return jnp.sum(g.reshape(h, d, h, d) * eye[:, None, :, None], axis=2)


def _layer_params(wt, l):
    gp = jnp.pad(jnp.stack([wt["gdn_a_log"][l], wt["gdn_dt_bias"][l]]), ((0, 6), (4, 128 - 4 - GDN_HEADS)))
    d_ffh = wt["ffn_conv_w"].shape[-1] // 2
    return dict(
        pre_mix=wt["pre_mix_norm"][l][None], post_mix=wt["post_mix_norm"][l][None],
        pre_ffn=wt["pre_ffn_norm"][l][None], post_ffn=wt["post_ffn_norm"][l][None],
        gdn_cw=wt["gdn_conv_w"][l], gdn_gp=gp, gdn_nw=wt["gdn_norm_w"][l][None],
        lru_cw=wt["lru_conv_w"][l], lru_cb=wt["lru_conv_b"][l][None],
        lru_wa=_block_diag(wt["lru_wa"][l]), lru_ba=wt["lru_ba"][l].reshape(1, -1),
        lru_wx=_block_diag(wt["lru_wx"][l]), lru_bx=wt["lru_bx"][l].reshape(1, -1),
        lru_lam=wt["lru_lambda"][l][None], gw0=wt["grp_norm_w"][l, 0][None], gw1=wt["grp_norm_w"][l, 1][None],
        gw2=wt["grp_norm_w"][l, 2][None],
        sgu_lnw=wt["sgu_ln_w"][l][None], sgu_lnb=wt["sgu_ln_b"][l][None],
        sgu_ws=wt["sgu_ws"][l].reshape(SGU_GROUPS * SGU_CHUNK, SGU_CHUNK),
        sgu_bt=jnp.pad(wt["sgu_b"][l].T, ((0, 0), (0, 128 - SGU_GROUPS))),
        sc_cw=wt["sconv_w"][l],
        ffn_cw=wt["ffn_conv_w"][l], ffn_cb=wt["ffn_conv_b"][l][None], d_ffh=d_ffh,
    )


TS_ROW = 256
TS_GDN = 256
TS_FFN = 512
TS_FFN_BWD = 256
TC_FFN = 512


def _mixers_fwd(l, p, lp, n, side, side_lru):
    qkv = _rin(p, 3 * D_G, 0, halo=True)
    z = _rin(p, D_G, 3)
    ba = _rin(p, 128, BA_COL // 128)
    gdn_ps = [_par(lp["gdn_cw"]), _par(lp["gdn_gp"]), _par(lp["gdn_nw"])]
    res = seq_fwd(f"gdn_fwd_{l}", fn_gdn, n, TS_GDN, [qkv, z, ba], gdn_ps, [_out(D_G, BF16)],
                  state_shapes=[(GDN_HEADS * GDN_DIM, GDN_DIM)], side=side)
    (y_a,), (gdn_st,), side_res = res if side else res + ([],)
    lru_x = _rin(p, D_G, 4, halo=True)
    lru_gate = _rin(p, D_G, 5)
    lru_ps = [_par(lp[k]) for k in ("lru_cw", "lru_cb", "lru_wa", "lru_ba", "lru_wx", "lru_bx", "lru_lam", "gw0")]
    res = seq_fwd(f"lru_fwd_{l}", fn_lru, n, TS_ROW, [lru_x, lru_gate], lru_ps, [_out(D_G, BF16)],
                  state_shapes=[(1, D_G)], side=side_lru)
    (y_b,), (lru_st,), side_res_lru = res if side_lru else res + ([],)
    uv = _rin(p, 2 * D_G, 3)
    sgu_ps = [_par(lp[k]) for k in ("sgu_lnw", "sgu_lnb", "sgu_ws", "sgu_bt", "gw1")]
    (y_c,), _ = seq_fwd(f"sgu_fwd_{l}", fn_sgu, n, TS_ROW, [uv], sgu_ps, [_out(D_G, BF16)])
    sc = [_rin(p, D_G, 8), _rin(p, D_G, 9, halo=True), _rin(p, D_G, 10, halo=True)]
    sc_ps = [_par(lp["sc_cw"]), _par(lp["gw2"])]
    (y_d,), _ = seq_fwd(f"sconv_fwd_{l}", fn_sconv, n, TS_ROW, sc, sc_ps, [_out(D_G, BF16)])
    ins = dict(gdn=([qkv, z, ba], gdn_ps, [gdn_st]), lru=([lru_x, lru_gate], lru_ps, [lru_st]),
               sgu=([uv], sgu_ps, []), sc=(sc, sc_ps, []))
    return jnp.concatenate([y_a, y_b, y_c, y_d], axis=1), ins, side_res, side_res_lru


def _mixers_bwd(l, dymix, ins, n, side):
    cot = lambda g: [_rin(dymix, D_G, g)]
    xs, ps, st = ins["gdn"]
    res = seq_bwd(f"gdn_bwd_{l}", fn_gdn, n, TS_GDN, xs, ps, cot(0), st, [BF16, BF16, BF16], side=side)
    (dqkv, dz, dba), g_gdn, side_res = res if side else res + ([],)
    xs, ps, st = ins["lru"]
    (dlx, dlg), g_lru = seq_bwd(f"lru_bwd_{l}", fn_lru, n, TS_ROW, xs, ps, cot(1), st, [BF16, BF16])
    xs, ps, st = ins["sgu"]
    (duv,), g_sgu = seq_bwd(f"sgu_bwd_{l}", fn_sgu, n, TS_ROW, xs, ps, cot(2), st, [BF16])
    xs, ps, st = ins["sc"]
    (dsb, dsc, dsh), g_sc = seq_bwd(f"sconv_bwd_{l}", fn_sconv, n, TS_ROW, xs, ps, cot(3), st, [BF16, BF16, BF16])
    dp = jnp.concatenate([dqkv, dz, dlx, dlg, duv, dsb, dsc, dsh, dba], axis=1)
    return dp, dict(gdn=g_gdn, lru=g_lru, sgu=g_sgu, sc=g_sc), side_res


def _ffn_ops(hid, lp):
    d_ffh = lp["d_ffh"]
    off = d_ffh // TC_FFN
    xs = [_rin(hid, TC_FFN, lambda j: j, halo=True), _rin(hid, TC_FFN, lambda j: j + off, halo=True)]
    ps = [_par(lp["ffn_cw"], TC_FFN, lambda j: j), _par(lp["ffn_cw"], TC_FFN, lambda j: j + off),
          _par(lp["ffn_cb"], TC_FFN, lambda j: j), _par(lp["ffn_cb"], TC_FFN, lambda j: j + off)]
    return xs, ps, d_ffh


_FROM_BLOCKS = dict(
    w_in=lambda b: _regroup_w_in(b.transpose(1, 0, 2).reshape(b.shape[1], -1)),
    ffn_up=lambda b: b.transpose(1, 0, 2).reshape(b.shape[1], -1),
    w_out=lambda b: b.reshape(-1, b.shape[2]),
    ffn_down=lambda b: b.reshape(-1, b.shape[2]),
)
_TO_BLOCKS = dict(
    w_in=lambda g: _ungroup_w_in(g).reshape(g.shape[0], N_DEV, -1).transpose(1, 0, 2),
    ffn_up=lambda g: g.reshape(g.shape[0], N_DEV, -1).transpose(1, 0, 2),
    w_out=lambda g: g.reshape(N_DEV, -1, g.shape[1]),
    ffn_down=lambda g: g.reshape(N_DEV, -1, g.shape[1]),
)


class _Traffic:
    PARTS = dict(w_in=1, w_out=1, ffn_up=2, ffn_down=1)

    def __init__(self, whole=None, shards=None):
        self.whole = dict(whole or {})
        self.shards = shards
        self.gathered = {}
        self.pending = {}
        self.landed = {}

    def _rows(self, key):
        name, l, part = key
        rows = self.shards[name].shape[1] // self.PARTS[name]
        return slice(part * rows, (part + 1) * rows)

    def jobs(self, gather=(), exchange=()):
        if self.shards is None:
            return [], []
        keys = [("gather", k) for k in gather if k not in self.gathered and k[:2] not in self.whole]
        keys += [("exchange", k) for k in exchange if k in self.pending]
        jobs = [(kind, self.shards[k[0]][k[1]][self._rows(k)] if kind == "gather" else self.pending[k])
                for kind, k in keys]
        return jobs, keys

    def done(self, keys, results):
        for (kind, k), r in zip(keys, results):
            if kind == "gather":
                self.gathered[k] = r
            else:
                self.landed[k] = r
                del self.pending[k]

    def weight(self, name, l):
        if (name, l) not in self.whole:
            parts = []
            for part in range(self.PARTS[name]):
                k = (name, l, part)
                if k not in self.gathered:
                    self.gathered[k] = all_gather(f"gather_{name}_{l}_{part}", self.shards[name][l][self._rows(k)], False)
                parts.append(self.gathered[k])
            blocks = parts[0] if len(parts) == 1 else jnp.concatenate(parts, axis=1)
            self.whole[(name, l)] = _FROM_BLOCKS[name](blocks)
        return self.whole[(name, l)]

    def grad(self, name, l, g):
        if self.shards is None:
            self.landed[(name, l)] = g
            return
        blocks = _TO_BLOCKS[name](g)
        for part in range(self.PARTS[name]):
            k = (name, l, part)
            self.pending[k] = blocks[:, self._rows(k)]

    def flush(self):
        for (name, l, part), blocks in list(self.pending.items()):
            self.landed[(name, l, part)] = all_to_all(f"exchange_{name}_{l}_{part}", blocks)
            del self.pending[(name, l, part)]


def local_step(x, target, wt, tr):
    n, d = x.shape
    depth = wt["pre_mix_norm"].shape[0]
    lps = [_layer_params(wt, l) for l in range(depth)]
    saved = []
    xin = x

    def mm(name, a, b, mode, dtype, tm, tn, tk, gather=(), exchange=()):
        jobs, keys = tr.jobs(gather, exchange)
        if not jobs:
            return matmul(name, a, b, mode, dtype, tm, tn, tk)
        out, res = matmul(name, a, b, mode, dtype, tm, tn, tk, side=jobs)
        tr.done(keys, res)
        return out

    (h,), _ = seq_fwd("norm_fwd", fn_norm, n, TS_ROW, [_rin(x)], [_par(lps[0]["pre_mix"])], [_out(d, BF16)])
    dx_last = loss = None
    for l in range(depth):
        lp = lps[l]
        p = mm(f"w_in_fwd_{l}", h, tr.weight("w_in", l), "nn", F32, 1024, N_INP // 5, d,
               gather=[("ffn_up", l, 0)])
        jobs, keys = tr.jobs(gather=[("ffn_up", l, 1)])
        jobs_lru, keys_lru = tr.jobs(gather=[("w_out", l, 0)])
        ymix, mix_ins, res, res_lru = _mixers_fwd(l, p, lp, n, jobs, jobs_lru)
        tr.done(keys, res)
        tr.done(keys_lru, res_lru)
        y = mm(f"w_out_fwd_{l}", ymix, tr.weight("w_out", l), "nn", F32, 1024, 1024, d)
        res_ps = [_par(lp["post_mix"]), _par(lp["pre_ffn"])]
        (x1, h2), _ = seq_fwd(f"res_mix_fwd_{l}", fn_res, n, TS_ROW, [_rin(xin), _rin(y)], res_ps,
                              [_out(d, F32), _out(d, BF16)])
        nxt = l + 1 < depth
        hid = mm(f"ffn_up_fwd_{l}", h2, tr.weight("ffn_up", l), "nn", F32, 1024, 1024, d,
                 gather=[("ffn_down", l, 0)] + ([("w_out", l + 1, 0)] if nxt else []))
        f_xs, f_ps, d_ffh = _ffn_ops(hid, lp)
        (act,), _ = seq_fwd(f"ffn_act_fwd_{l}", fn_ffn, n, TS_FFN, f_xs, f_ps,
                            [_out(TC_FFN, BF16, d_ffh, lambda j: j)], ncol=d_ffh // TC_FFN)
        yf = mm(f"ffn_down_fwd_{l}", act, tr.weight("ffn_down", l), "nn", F32, 1024, 1024, d_ffh // 2,
                gather=[("w_in", l + 1, 0)] if nxt else [])
        rec = dict(x=xin, h=h, mix_ins=mix_ins, ymix=ymix, y=y, x1=x1, h2=h2, f_xs=f_xs, f_ps=f_ps, act=act, yf=yf)
        if l + 1 < depth:
            ps = [_par(lp["post_ffn"]), _par(lps[l + 1]["pre_mix"])]
            (x2, h), _ = seq_fwd(f"res_ffn_fwd_{l}", fn_res, n, TS_ROW, [_rin(x1), _rin(yf)], ps,
                                 [_out(d, F32), _out(d, BF16)])
            rec["res_ffn_ps"] = ps
            xin = x2
        else:
            def body(xs, blks, ps, carries, r):
                x1_, yf_, t_ = xs
                e = x1_ + _rms(yf_, ps[0]) - t_
                part = 0.5 * jnp.sum(jnp.mean(e * e, axis=-1, keepdims=True), axis=0, keepdims=True)
                return [e * (1.0 / d)], [], [jnp.broadcast_to(part, (8, 128))], []

            ps = [_par(lp["post_ffn"])]
            dx_last, loss = _block_call("loss_fwd", body, n, TS_ROW, 1, False, [_rin(x1), _rin(yf), _rin(target)],
                                        [], ps, [_out(d, F32)], [], [dict(shape=(8, 128), total=None, col=None)], [])
            rec["res_ffn_ps"] = ps
        saved.append(rec)

    grads = {}
    dx2, dh_next = dx_last, None
    for l in reversed(range(depth)):
        rec, lp = saved[l], lps[l]
        d_ffh = lp["d_ffh"]
        g = {}
        if dh_next is None:
            (dx1, dyf), (g["post_ffn"],) = seq_bwd(f"res_ffn_bwd_{l}", fn_res_last, n, TS_ROW,
                                                   [_rin(rec["x1"]), _rin(rec["yf"])], rec["res_ffn_ps"], [_rin(dx2)],
                                                   din_dtypes=[F32, BF16])
        else:
            (dx1, dyf), (g["post_ffn"], g_next_pre) = seq_bwd(
                f"res_ffn_bwd_{l}", fn_res, n, TS_ROW, [_rin(rec["x1"]), _rin(rec["yf"])], rec["res_ffn_ps"],
                [_rin(dx2), _rin(dh_next)], din_dtypes=[F32, BF16])
            grads[l + 1]["pre_mix"] = g_next_pre
        tr.grad("ffn_down", l, mm(f"ffn_down_dw_{l}", rec["act"], dyf, "tn", BF16, d_ffh // 4, 1024, 2048))

        def dact(cot_blocks, aux_blocks):
            return [_dot(cot_blocks[0], aux_blocks[0], 1, 1).astype(BF16)]

        (dhg, dhv), (g_cwg, g_cwv, g_cbg, g_cbv) = seq_bwd(
            f"ffn_act_bwd_{l}", fn_ffn, n, TS_FFN_BWD, rec["f_xs"], rec["f_ps"], [_rin(dyf)],
            din_dtypes=[BF16, BF16], ncol=d_ffh // TC_FFN, din_specs=[(d_ffh, lambda j: j), (d_ffh, lambda j: j)],
            cot_map=dact, aux=[_par(tr.weight("ffn_down", l), TC_FFN, row=lambda j: j)])
        dhid = jnp.concatenate([dhg, dhv], axis=1)
        g["ffn_cw"] = jnp.concatenate([g_cwg[:, :d_ffh], g_cwv[:, d_ffh:]], axis=1)
        g["ffn_cb"] = jnp.concatenate([g_cbg[:, :d_ffh], g_cbv[:, d_ffh:]], axis=1)
        dh2 = mm(f"ffn_up_dx_{l}", dhid, tr.weight("ffn_up", l), "nt", BF16, 1024, 1024, d_ffh // 2,
                 exchange=[("ffn_down", l, 0)])
        tr.grad("ffn_up", l, mm(f"ffn_up_dw_{l}", rec["h2"], dhid, "tn", BF16, 1024, 1024, 2048))
        (dx, dy), (g["post_mix"], g["pre_ffn"]) = seq_bwd(
            f"res_mix_bwd_{l}", fn_res, n, TS_ROW, [_rin(rec["x"]), _rin(rec["y"])],
            [_par(lp["post_mix"]), _par(lp["pre_ffn"])], [_rin(dx1), _rin(dh2)], din_dtypes=[F32, BF16])
        dymix = mm(f"w_out_dx_{l}", dy, tr.weight("w_out", l), "nt", BF16, 1024, 1024, d)
        tr.grad("w_out", l, mm(f"w_out_dw_{l}", rec["ymix"], dy, "tn", BF16, 1024, 1024, 2048))
        jobs, keys = tr.jobs(exchange=[("ffn_up", l, 0), ("w_out", l, 0)])
        dp, g["mix"], res = _mixers_bwd(l, dymix, rec["mix_ins"], n, jobs)
        tr.done(keys, res)
        tr.grad("w_in", l, mm(f"w_in_dw_{l}", rec["h"], dp, "tn", BF16, 1024, N_INP // 5, 2048,
                              exchange=[("ffn_up", l, 1)]))
        dh = mm(f"w_in_dx_{l}", dp, tr.weight("w_in", l), "nt", BF16, 1024, 1024, N_INP // 3,
                exchange=[("w_in", l, 0)])
        grads[l] = g
        dx2, dh_next = dx, dh
    (grad_x,), (g_pre0,) = seq_bwd("norm_bwd", fn_norm_keep, n, TS_ROW, [_rin(x)], [_par(lps[0]["pre_mix"])],
                                   [_rin(dh_next), _rin(dx2)])
    grads[0]["pre_mix"] = g_pre0
    tr.flush()
    return loss[0, 0], grad_x, _name_grads(grads, depth)


def _name_grads(grads, depth):
    per = {k: [] for k in SMALL}
    for l in range(depth):
        g = grads[l]
        m = g["mix"]
        cw, gp, nw = m["gdn"]
        lcw, lcb, lwa, lba, lwx, lbx, llam, gw0 = m["lru"]
        lnw, lnb, ws, bst, gw1 = m["sgu"]
        scw, gw2 = m["sc"]
        per["pre_mix_norm"].append(g["pre_mix"][0])
        per["gdn_conv_w"].append(cw)
        per["gdn_a_log"].append(gp[0, 4:8])
        per["gdn_dt_bias"].append(gp[1, 4:8])
        per["gdn_norm_w"].append(nw[0])
        per["lru_conv_w"].append(lcw)
        per["lru_conv_b"].append(lcb[0])
        per["lru_wa"].append(_block_diag_grad(lwa, LRU_BLOCKS))
        per["lru_ba"].append(lba.reshape(LRU_BLOCKS, -1))
        per["lru_wx"].append(_block_diag_grad(lwx, LRU_BLOCKS))
        per["lru_bx"].append(lbx.reshape(LRU_BLOCKS, -1))
        per["lru_lambda"].append(llam[0])
        per["sgu_ln_w"].append(lnw[0])
        per["sgu_ln_b"].append(lnb[0])
        per["sgu_ws"].append(ws.reshape(SGU_GROUPS, SGU_CHUNK, SGU_CHUNK))
        per["sgu_b"].append(bst[:, :SGU_GROUPS].T)
        per["sconv_w"].append(scw)
        per["grp_norm_w"].append(jnp.concatenate([gw0, gw1, gw2], axis=0))
        per["post_mix_norm"].append(g["post_mix"][0])
        per["pre_ffn_norm"].append(g["pre_ffn"][0])
        per["ffn_conv_w"].append(g["ffn_cw"])
        per["ffn_conv_b"].append(g["ffn_cb"][0])
        per["post_ffn_norm"].append(g["post_ffn"][0])
    return {k: jnp.stack(v) for k, v in per.items()}


def _regroup_w_in(w):
    pad = jnp.zeros(w.shape[:-1] + (N_INP - N_IN,), w.dtype)
    return jnp.concatenate([w[..., :2048], w[..., 2056:], w[..., 2048:2056], pad], axis=-1)


def _ungroup_w_in(g):
    return jnp.concatenate([g[..., :2048], g[..., BA_COL:BA_COL + 8], g[..., 2048:BA_COL]], axis=-1)


def kernel(x, pre_mix_norm, w_in, gdn_conv_w, gdn_a_log, gdn_dt_bias, gdn_norm_w, lru_conv_w, lru_conv_b, lru_wa, lru_ba, lru_wx, lru_bx, lru_lambda, sgu_ln_w, sgu_ln_b, sgu_ws, sgu_b, sconv_w, grp_norm_w, w_out, post_mix_norm, pre_ffn_norm, ffn_up, ffn_conv_w, ffn_conv_b, ffn_down, post_ffn_norm, loss_target, m_pre_mix_norm, m_w_in, m_gdn_conv_w, m_gdn_a_log, m_gdn_dt_bias, m_gdn_norm_w, m_lru_conv_w, m_lru_conv_b, m_lru_wa, m_lru_ba, m_lru_wx, m_lru_bx, m_lru_lambda, m_sgu_ln_w, m_sgu_ln_b, m_sgu_ws, m_sgu_b, m_sconv_w, m_grp_norm_w, m_w_out, m_post_mix_norm, m_pre_ffn_norm, m_ffn_up, m_ffn_conv_w, m_ffn_conv_b, m_ffn_down, m_post_ffn_norm, v_pre_mix_norm, v_w_in, v_gdn_conv_w, v_gdn_a_log, v_gdn_dt_bias, v_gdn_norm_w, v_lru_conv_w, v_lru_conv_b, v_lru_wa, v_lru_ba, v_lru_wx, v_lru_bx, v_lru_lambda, v_sgu_ln_w, v_sgu_ln_b, v_sgu_ws, v_sgu_b, v_sconv_w, v_grp_norm_w, v_w_out, v_post_mix_norm, v_pre_ffn_norm, v_ffn_up, v_ffn_conv_w, v_ffn_conv_b, v_ffn_down, v_post_ffn_norm):
    args = locals()
    w_loc = {k: args[k] for k in WEIGHTS}
    m_loc = {k: args["m_" + k] for k in WEIGHTS}
    v_loc = {k: args["v_" + k] for k in WEIGHTS}
    depth = pre_mix_norm.shape[0]
    x_, y_, c_ = _my_pos()
    me = 4 * x_ + 2 * y_ + c_

    tr = _Traffic(shards={name: cast_bf16(f"cast_{name}", w_loc[name]) for name in BIG})
    wt = {k: w_loc[k] for k in SMALL}
    shard_shapes = [w_loc[k].shape for k in SHARDED_SMALL]
    gathered = all_gather("gather_small", _pack([w_loc[k] for k in SHARDED_SMALL]), True)
    for k, a in zip(SHARDED_SMALL, _unpack(gathered, shard_shapes, lead=(N_DEV,))):
        a = jnp.moveaxis(a, 0, -2)
        wt[k] = a.reshape(a.shape[:-2] + (-1,))

    loss_part, grad_x, g_full = local_step(x[0], loss_target[0], wt, tr)
    loss = lax.psum(loss_part, ("x", "y", "c"))

    outs_g, outs_d, outs_m, outs_v = {}, {}, {}, {}
    for name in BIG:
        lands = {(l, part): a for (n_, l, part), a in tr.landed.items() if n_ == name}
        outs_g[name], outs_d[name], outs_m[name], outs_v[name] = adamw_big(
            f"adamw_{name}", w_loc[name], lands, m_loc[name], v_loc[name])

    full_shapes = [g_full[k].shape for k in SMALL]
    g_all = all_gather("gather_small_grads", _pack([g_full[k] for k in SMALL]), True)
    g_sum = _unpack(sum_blocks("sum_small_grads", g_all), full_shapes)
    g_small = {}
    for k, g in zip(SMALL, g_sum):
        if k in SHARDED_SMALL:
            w = w_loc[k].shape[-1]
            g = lax.dynamic_slice_in_dim(g, me * w, w, axis=g.ndim - 1)
        g_small[k] = g
    d_s, m_s, v_s = adamw_small("adamw_small", [w_loc[k] for k in SMALL], [g_small[k] for k in SMALL],
                                [m_loc[k] for k in SMALL], [v_loc[k] for k in SMALL])
    for k_i, k in enumerate(SMALL):
        outs_g[k], outs_d[k], outs_m[k], outs_v[k] = g_small[k], d_s[k_i], m_s[k_i], v_s[k_i]

    return (loss, grad_x[None], *[outs_g[k] for k in WEIGHTS], *[outs_d[k] for k in WEIGHTS],
            *[outs_m[k] for k in WEIGHTS], *[outs_v[k] for k in WEIGHTS])
```

```python
import functools
import math

import jax
import jax.numpy as jnp
from jax import lax
from jax.experimental import pallas as pl
from jax.experimental.pallas import tpu as pltpu

F32 = jnp.float32
BF16 = jnp.bfloat16
EPS = 1e-6
HALO = 8
VMEM_LIMIT = 56 * 1024 * 1024
MESH = pl.DeviceIdType.MESH
N_DEV = 8

ADAM_LR, ADAM_B1, ADAM_B2, ADAM_EPS, ADAM_WD, ADAM_STEP = 0.001, 0.9, 0.999, 1e-08, 0.01, 10

GDN_HEADS, GDN_DIM, GDN_CHUNK = 4, 128, 64
SGU_GROUPS, SGU_CHUNK = 4, 128
LRU_BLOCKS, LRU_C = 8, 8.0
D_G = 512
N_IN = 5640
N_INP = 5760
BA_COL = 5632

SHARDED_SMALL = ("gdn_conv_w", "lru_conv_w", "sconv_w", "grp_norm_w", "ffn_conv_w")
BIG = ("w_in", "w_out", "ffn_up", "ffn_down")
WEIGHTS = ("pre_mix_norm", "w_in", "gdn_conv_w", "gdn_a_log", "gdn_dt_bias", "gdn_norm_w", "lru_conv_w",
           "lru_conv_b", "lru_wa", "lru_ba", "lru_wx", "lru_bx", "lru_lambda", "sgu_ln_w", "sgu_ln_b", "sgu_ws",
           "sgu_b", "sconv_w", "grp_norm_w", "w_out", "post_mix_norm", "pre_ffn_norm", "ffn_up", "ffn_conv_w",
           "ffn_conv_b", "ffn_down", "post_ffn_norm")
SMALL = tuple(n for n in WEIGHTS if n not in BIG)


def _dot(a, b, ca, cb):
    return lax.dot_general(a.astype(BF16), b.astype(BF16), (((ca,), (cb,)), ((), ())),
                           preferred_element_type=F32)


@jax.custom_vjp
def _mm(a, b):
    return _dot(a, b, 1, 0)


def _mm_f(a, b):
    return _dot(a, b, 1, 0), (a, b)


def _mm_b(res, g):
    a, b = res
    return _dot(g, b, 1, 1), _dot(a, g, 0, 0)


_mm.defvjp(_mm_f, _mm_b)


@jax.custom_vjp
def _mm_nt(a, b):
    return _dot(a, b, 1, 1)


def _mm_nt_f(a, b):
    return _dot(a, b, 1, 1), (a, b)


def _mm_nt_b(res, g):
    a, b = res
    return _dot(g, b, 1, 0), _dot(g, a, 0, 0)


_mm_nt.defvjp(_mm_nt_f, _mm_nt_b)


@jax.custom_vjp
def _mm_tn(a, b):
    return _dot(a, b, 0, 0)


def _mm_tn_f(a, b):
    return _dot(a, b, 0, 0), (a, b)


def _mm_tn_b(res, g):
    a, b = res
    return _dot(b, g, 1, 1), _dot(a, g, 1, 0)


_mm_tn.defvjp(_mm_tn_f, _mm_tn_b)


def _dot_exact(a, b, ca, cb):
    return lax.dot_general(a, b, (((ca,), (cb,)), ((), ())), precision=lax.Precision.HIGHEST,
                           preferred_element_type=F32)


@functools.partial(jax.custom_vjp, nondiff_argnums=(1,))
def _shift_rows(x, s):
    return pltpu.roll(x, s, 0)


def _shift_rows_f(x, s):
    return pltpu.roll(x, s, 0), None


def _shift_rows_b(s, _, g):
    return (pltpu.roll(g, (g.shape[0] - s) % g.shape[0], 0),)


_shift_rows.defvjp(_shift_rows_f, _shift_rows_b)


def _sigmoid(x):
    return 1.0 / (1.0 + jnp.exp(-x))


def _silu(x):
    return x * _sigmoid(x)


GELU_C, GELU_A = 0.7978845608028654, 0.044715


@jax.custom_vjp
def _gelu(x):
    return 0.5 * x * (1.0 + jnp.tanh(GELU_C * (x + GELU_A * (x * x * x))))


def _gelu_f(x):
    t = jnp.tanh(GELU_C * (x + GELU_A * (x * x * x)))
    return 0.5 * x * (1.0 + t), (x, t)


def _gelu_b(res, g):
    x, t = res
    slope = 0.5 * (1.0 + t) + (0.5 * GELU_C) * x * (1.0 - t * t) * (1.0 + (3.0 * GELU_A) * (x * x))
    return (g * slope,)


_gelu.defvjp(_gelu_f, _gelu_b)


@jax.custom_vjp
def _softplus(x):
    e = jnp.exp(-jnp.abs(x))
    u = 1.0 + e
    log1p = jnp.where(u == 1.0, e, jnp.log(u) * (e / jnp.where(u == 1.0, 1.0, u - 1.0)))
    return jnp.maximum(x, 0.0) + log1p


def _softplus_f(x):
    return _softplus(x), x


def _softplus_b(x, g):
    return (g * _sigmoid(x),)


_softplus.defvjp(_softplus_f, _softplus_b)


def _neg_expm1(y):
    return -jnp.tanh(0.5 * y) * (jnp.exp(y) + 1.0)


def _rms(x, w):
    return x * lax.rsqrt(jnp.mean(x * x, axis=-1, keepdims=True) + EPS) * w


def _row(w, k):
    sel = lax.broadcasted_iota(jnp.int32, w.shape, 0) == k
    return jnp.sum(jnp.where(sel, w, 0.0), axis=0, keepdims=True)


def _col(x, j):
    sel = lax.broadcasted_iota(jnp.int32, x.shape, 1) == j
    return jnp.sum(jnp.where(sel, x, 0.0), axis=1, keepdims=True)


def _conv(x_ext, w, taps):
    acc = None
    for k in range(taps):
        s = taps - 1 - k
        t = (x_ext if s == 0 else _shift_rows(x_ext, s)) * _row(w, k)
        acc = t if acc is None else acc + t
    return acc[HALO:]


@jax.custom_vjp
def _scan(a, b, h0):
    n = a.shape[0]
    row = lax.broadcasted_iota(jnp.int32, a.shape, 0)
    s = 1
    while s < n:
        keep = row >= s
        a_sh = jnp.where(keep, pltpu.roll(a, s, 0), 1.0)
        b_sh = jnp.where(keep, pltpu.roll(b, s, 0), 0.0)
        b = a * b_sh + b
        a = a * a_sh
        s *= 2
    return b + a * h0


def _scan_f(a, b, h0):
    h = _scan(a, b, h0)
    return h, (a, h, h0)


def _scan_b(res, dh):
    a, h, h0 = res
    n = a.shape[0]
    row = lax.broadcasted_iota(jnp.int32, a.shape, 0)
    an = jnp.where(row < n - 1, pltpu.roll(a, n - 1, 0), 0.0)
    lam = dh
    s = 1
    while s < n:
        keep = row < n - s
        a_sh = jnp.where(keep, pltpu.roll(an, n - s, 0), 1.0)
        l_sh = jnp.where(keep, pltpu.roll(lam, n - s, 0), 0.0)
        lam = an * l_sh + lam
        an = an * a_sh
        s *= 2
    h_prev = jnp.where(row >= 1, pltpu.roll(h, 1, 0), h0)
    al = a * lam
    dh0 = jnp.sum(jnp.where(row == 0, al, 0.0), axis=0, keepdims=True)
    return lam * h_prev, lam, dh0


_scan.defvjp(_scan_f, _scan_b)


@jax.custom_vjp
def _unit_lower_inverses(ms):
    n = ms[0].shape[0]
    shape = ms[0].shape
    eye = (lax.broadcasted_iota(jnp.int32, shape, 0) == lax.broadcasted_iota(jnp.int32, shape, 1)).astype(F32)
    p = [-m for m in ms]
    t = [eye + a for a in p]
    steps = 1
    while 2 ** steps < n:
        p = [_mm(a, a) for a in p]
        t = [a + _mm(a, c) for a, c in zip(t, p)]
        steps += 1
    return t


def _unit_lower_inverses_f(ms):
    t = _unit_lower_inverses(ms)
    return t, t


def _unit_lower_inverses_b(t, dt):
    x = [_mm_nt(g, a) for g, a in zip(dt, t)]
    return ([-_mm_tn(a, c) for a, c in zip(t, x)],)


_unit_lower_inverses.defvjp(_unit_lower_inverses_f, _unit_lower_inverses_b)


def _last_row(x):
    sel = lax.broadcasted_iota(jnp.int32, x.shape, 0) == x.shape[0] - 1
    return jnp.sum(jnp.where(sel, x, 0.0), axis=0, keepdims=True)


def fn_norm(xs, st, ps):
    (x,), (w,) = xs, ps
    return [_rms(x, w).astype(BF16)], []


def fn_norm_keep(xs, st, ps):
    (x,), (w,) = xs, ps
    return [_rms(x, w).astype(BF16), x], []


def fn_res(xs, st, ps):
    (x, y), (w_post, w_next) = xs, ps
    x1 = x + _rms(y, w_post)
    return [x1, _rms(x1, w_next).astype(BF16)], []


def fn_res_last(xs, st, ps):
    (x, y), (w_post,) = xs, ps
    return [x + _rms(y, w_post)], []


def fn_gdn(xs, st, ps):
    qkv_ext, z, ba = xs
    (state,) = st
    cw, gp, nw = ps
    ts = z.shape[0]
    qkv = _silu(_conv(qkv_ext, cw, 4))
    beta_all = _sigmoid(ba)
    g_all = -jnp.exp(_row(gp, 0)) * _softplus(ba + _row(gp, 1))
    c_n = GDN_CHUNK
    ri = lax.broadcasted_iota(jnp.int32, (c_n, c_n), 0)
    ci = lax.broadcasted_iota(jnp.int32, (c_n, c_n), 1)
    causal, strict = ri >= ci, ri > ci
    tril = causal.astype(F32)
    lane = lax.broadcasted_iota(jnp.int32, (c_n, 128), 1)
    s_h = [state[GDN_DIM * h:GDN_DIM * (h + 1)] for h in range(GDN_HEADS)]
    n_c = ts // c_n
    pairs = [(c, h) for c in range(n_c) for h in range(GDN_HEADS)]
    every = lambda f, *lists: [f(*a) for a in zip(*lists)]

    def piece(c, h, base):
        return qkv[c * c_n:(c + 1) * c_n, base + GDN_DIM * h:base + GDN_DIM * (h + 1)]

    q = [piece(c, h, 0) for c, h in pairs]
    k = [piece(c, h, D_G) for c, h in pairs]
    v = [piece(c, h, 2 * D_G) for c, h in pairs]
    q = every(lambda t: t * lax.rsqrt(jnp.sum(t * t, axis=-1, keepdims=True) + EPS) * (GDN_DIM ** -0.5), q)
    k = every(lambda t: t * lax.rsqrt(jnp.sum(t * t, axis=-1, keepdims=True) + EPS), k)
    gcum_all = [_dot_exact(tril, g_all[c * c_n:(c + 1) * c_n], 1, 0) for c in range(n_c)]
    b = [_col(beta_all[c * c_n:(c + 1) * c_n], h) for c, h in pairs]
    gc = [_col(gcum_all[c], 4 + h) for c, h in pairs]
    gr = [_dot_exact((lane == 4 + h).astype(F32), gcum_all[c], 1, 1) for c, h in pairs]
    decay = every(lambda a, r: jnp.where(causal, jnp.exp(jnp.where(causal, a - r, 0.0)), 0.0), gc, gr)
    kb = every(lambda a, c: a * c, k, b)
    mk = every(lambda a, c, e: _mm_nt(jnp.concatenate([a, c], axis=0), e), kb, q, k)
    m = every(lambda a, dcy: jnp.where(strict, a[:c_n] * dcy, 0.0), mk, decay)
    attn = every(lambda a, dcy: jnp.where(causal, a[c_n:] * dcy, 0.0), mk, decay)
    t_ = _unit_lower_inverses(m)
    eg = every(jnp.exp, gc)
    wu = every(lambda t, a, e, c, d: _mm(t, jnp.concatenate([a * e, c * d], axis=1)), t_, kb, eg, v, b)
    g_last = every(_last_row, gc)
    k_g = every(lambda a, gl, g: a * jnp.exp(gl - g), k, g_last, gc)
    wq = every(lambda a, c, e: jnp.concatenate([a[:, :GDN_DIM], c * e], axis=0), wu, q, eg)
    u = [a[:, GDN_DIM:] for a in wu]
    gl = every(jnp.exp, g_last)

    o = []
    for c in range(n_c):
        idx = range(c * GDN_HEADS, (c + 1) * GDN_HEADS)
        ws = [_mm(wq[i], s_h[h]) for h, i in enumerate(idx)]
        v_new = [u[i] - ws[h][:c_n] for h, i in enumerate(idx)]
        av = [_mm(attn[i], v_new[h]) for h, i in enumerate(idx)]
        kv = [_mm_tn(k_g[i], v_new[h]) for h, i in enumerate(idx)]
        o += [ws[h][c_n:] + av[h] for h in range(GDN_HEADS)]
        s_h = [s_h[h] * gl[i] + kv[h] for h, i in enumerate(idx)]
    zz = [z[c * c_n:(c + 1) * c_n, GDN_DIM * h:GDN_DIM * (h + 1)] for c, h in pairs]
    y = every(lambda a, g: a * lax.rsqrt(jnp.mean(a * a, axis=-1, keepdims=True) + EPS) * nw * _silu(g), o, zz)
    rows = [jnp.concatenate(y[c * GDN_HEADS:(c + 1) * GDN_HEADS], axis=1) for c in range(n_c)]
    y = rows[0] if n_c == 1 else jnp.concatenate(rows, axis=0)
    return [y.astype(BF16)], [jnp.concatenate(s_h, axis=0)]


def fn_lru(xs, st, ps):
    x_ext, gate = xs
    (h0,) = st
    cw, cb, wa, ba, wx, bx, lam, gw = ps
    xc = _conv(x_ext, cw, 4) + cb
    r = _sigmoid(_mm(xc, wa) + ba)
    i = _sigmoid(_mm(xc, wx) + bx)
    log_a = -LRU_C * r * _softplus(-lam)
    a = jnp.exp(log_a)
    mult = jnp.sqrt(_neg_expm1(2.0 * log_a))
    h = _scan(a, mult * (i * xc), h0)
    y = _rms(h * _gelu(gate), gw)
    return [y.astype(BF16)], [_last_row(h)]


def fn_sgu(xs, st, ps):
    (uv,) = xs
    lnw, lnb, ws, bst, gw = ps
    ts = uv.shape[0]
    uvf = _gelu(uv)
    u, v = uvf[:, :D_G], uvf[:, D_G:]
    vc = v - jnp.mean(v, axis=-1, keepdims=True)
    v = vc * lax.rsqrt(jnp.mean(vc * vc, axis=-1, keepdims=True) + EPS) * lnw + lnb
    t_n = SGU_CHUNK
    tril = lax.broadcasted_iota(jnp.int32, (t_n, t_n), 0) >= lax.broadcasted_iota(jnp.int32, (t_n, t_n), 1)
    wg = [jnp.where(tril, ws[t_n * g:t_n * (g + 1)], 0.0) for g in range(SGU_GROUPS)]
    bg = [_col(bst, g) for g in range(SGU_GROUPS)]
    rows = []
    for c in range(ts // t_n):
        vcg = v[c * t_n:(c + 1) * t_n]
        rows.append(jnp.concatenate(
            [_mm(wg[g], vcg[:, 128 * g:128 * (g + 1)]) + bg[g] for g in range(SGU_GROUPS)], axis=1))
    vv = rows[0] if len(rows) == 1 else jnp.concatenate(rows, axis=0)
    return [_rms(u * vv, gw).astype(BF16)], []


def fn_sconv(xs, st, ps):
    bg, cg_ext, hh_ext = xs
    cw, gw = ps
    return [_rms(bg * _conv(cg_ext * hh_ext, cw, 3), gw).astype(BF16)], []


def fn_ffn(xs, st, ps):
    g_ext, v_ext = xs
    cwg, cwv, cbg, cbv = ps
    g = _conv(g_ext, cwg, 3) + cbg
    v = _conv(v_ext, cwv, 3) + cbv
    return [(_gelu(g) * v).astype(BF16)], []


def _my_pos():
    return lax.axis_index("x"), lax.axis_index("y"), lax.axis_index("c")


def _peer(pos, k):
    x_, y_, c_ = pos
    return (1 - x_ if (k >> 2) & 1 else x_, 1 - y_ if (k >> 1) & 1 else y_, 1 - c_ if k & 1 else c_)


def _dev_index(p):
    return 4 * p[0] + 2 * p[1] + p[2]


class _Side:
    def __init__(self, jobs):
        self.jobs = list(jobs)
        n = len(self.jobs)
        self.operands = [a for _, a in self.jobs]
        self.in_specs = [pl.BlockSpec(memory_space=pl.ANY)] * n
        self.out_shape = [jax.ShapeDtypeStruct(((N_DEV,) + a.shape) if kind == "gather" else a.shape, a.dtype)
                          for kind, a in self.jobs]
        self.out_specs = [pl.BlockSpec(memory_space=pl.ANY)] * n
        self.scratch = [pltpu.SemaphoreType.DMA((7 * n,)), pltpu.SemaphoreType.DMA((7 * n,)),
                        pltpu.SemaphoreType.DMA((n,))] if n else []

    def _copies(self, in_refs, out_refs, sems, landings=True):
        send, recv, local = sems
        pos = _my_pos()
        me = _dev_index(pos)
        mine, outgoing, landing = [], [], []
        for j, (kind, _) in enumerate(self.jobs):
            src, dst = in_refs[j], out_refs[j]
            own = src if kind == "gather" else src.at[me]
            mine.append(pltpu.make_async_copy(own, dst.at[me], local.at[j]))
            for k in range(1, N_DEV):
                p = _peer(pos, k)
                sems_k = dict(send_sem=send.at[7 * j + k - 1], recv_sem=recv.at[7 * j + k - 1], device_id=p,
                              device_id_type=MESH)
                outgoing.append(pltpu.make_async_remote_copy(
                    src_ref=src if kind == "gather" else src.at[_dev_index(p)], dst_ref=dst.at[me], **sems_k))
                if landings:
                    landing.append(pltpu.make_async_remote_copy(src_ref=own, dst_ref=dst.at[_dev_index(p)], **sems_k))
        return mine, outgoing, landing

    def start(self, in_refs, out_refs, sems):
        mine, outgoing, _ = self._copies(in_refs, out_refs, sems, landings=False)
        for cp in mine + outgoing:
            cp.start()

    def wait(self, in_refs, out_refs, sems):
        mine, outgoing, landing = self._copies(in_refs, out_refs, sems)
        for cp in landing:
            cp.wait_recv()
        for cp in outgoing:
            cp.wait_send()
        for cp in mine:
            cp.wait()


def _rin(arr, w=None, col=0, halo=False):
    return dict(arr=arr, w=arr.shape[1] if w is None else w, col=col, halo=halo)


def _par(arr, w=None, col=None, row=None):
    return dict(arr=arr, w=w, col=col, row=row)


def _colidx(col, j):
    return col(j) if callable(col) else col


def _block_call(name, body, n_rows, ts, ncol, reverse, row_ins, blk_ins, params, row_outs, blk_outs, acc_outs,
                carries, side=()):
    side = _Side(side)
    ts = min(ts, n_rows)
    nblk = n_rows // ts
    hb = ts // HALO

    def rr(i):
        return (nblk - 1 - i) if reverse else i

    in_specs, operands = [], []
    for s in row_ins:
        in_specs.append(pl.BlockSpec((ts, s["w"]), lambda j, i, s=s: (rr(i), _colidx(s["col"], j))))
        operands.append(s["arr"])
        if s["halo"]:
            in_specs.append(pl.BlockSpec((HALO, s["w"]),
                                         lambda j, i, s=s: (jnp.maximum(rr(i) * hb - 1, 0), _colidx(s["col"], j))))
            operands.append(s["arr"])
    for a in blk_ins:
        nd = a.ndim - 1
        in_specs.append(pl.BlockSpec((None,) + a.shape[1:], lambda j, i, nd=nd: (rr(i),) + (0,) * nd))
        operands.append(a)
    for p in params:
        a = p["arr"]
        if p["row"] is not None:
            in_specs.append(pl.BlockSpec((p["w"], a.shape[1]), lambda j, i, p=p: (_colidx(p["row"], j), 0)))
        elif p["col"] is None:
            in_specs.append(pl.BlockSpec(a.shape, lambda j, i: (0, 0)))
        else:
            in_specs.append(pl.BlockSpec((a.shape[0], p["w"]), lambda j, i, p=p: (0, _colidx(p["col"], j))))
        operands.append(a)

    out_specs, out_shape = [], []
    for o in row_outs:
        out_specs.append(pl.BlockSpec((ts, o["w"]), lambda j, i, o=o: (rr(i), _colidx(o["col"], j))))
        out_shape.append(jax.ShapeDtypeStruct((n_rows, o["total"]), o["dtype"]))
    for o in blk_outs:
        nd = len(o["shape"])
        out_specs.append(pl.BlockSpec((None,) + tuple(o["shape"]), lambda j, i, nd=nd: (rr(i),) + (0,) * nd))
        out_shape.append(jax.ShapeDtypeStruct((nblk,) + tuple(o["shape"]), o["dtype"]))
    for o in acc_outs:
        if o["col"] is None:
            out_specs.append(pl.BlockSpec(o["shape"], lambda j, i: (0, 0)))
            out_shape.append(jax.ShapeDtypeStruct(o["shape"], F32))
        else:
            out_specs.append(pl.BlockSpec(o["shape"], lambda j, i, o=o: (0, _colidx(o["col"], j))))
            out_shape.append(jax.ShapeDtypeStruct((o["shape"][0], o["total"]), F32))

    n_in = len(operands)
    n_row_out, n_blk_out, n_acc = len(row_outs), len(blk_outs), len(acc_outs)
    n_out = n_row_out + n_blk_out + n_acc
    n_side = len(side.jobs)

    def kern(*refs):
        in_refs = refs[:n_in]
        side_in = refs[n_in:n_in + n_side]
        out_refs = refs[n_in + n_side:n_in + n_side + n_out]
        side_out = refs[n_in + n_side + n_out:n_in + 2 * n_side + n_out]
        scratch = refs[n_in + 2 * n_side + n_out:]
        carry_refs, side_sems = scratch[:len(carries)], scratch[len(carries):]
        acc_refs = out_refs[n_row_out + n_blk_out:]
        i = pl.program_id(1)
        r = rr(i)
        if n_side:
            @pl.when((pl.program_id(0) == 0) & (i == 0))
            def _():
                side.start(side_in, side_out, side_sems)

        @pl.when(i == 0)
        def _():
            for c_ref in carry_refs:
                c_ref[...] = jnp.zeros(c_ref.shape, c_ref.dtype)
            for a_ref in acc_refs:
                a_ref[...] = jnp.zeros(a_ref.shape, a_ref.dtype)

        k = 0
        xs = []
        for s in row_ins:
            x = in_refs[k][...]
            k += 1
            if s["halo"]:
                hal = in_refs[k][...]
                k += 1
                hal = jnp.where(r == 0, jnp.zeros_like(hal), hal)
                x = jnp.concatenate([hal, x], axis=0)
            xs.append(x)
        blks = []
        for _ in blk_ins:
            blks.append(in_refs[k][...])
            k += 1
        ps = []
        for _ in params:
            ps.append(in_refs[k][...])
            k += 1
        row_vals, blk_vals, acc_vals, new_carries = body(xs, blks, ps, [c[...] for c in carry_refs], r)
        for ref, val in zip(out_refs[:n_row_out], row_vals):
            ref[...] = val.astype(ref.dtype)
        for ref, val in zip(out_refs[n_row_out:n_row_out + n_blk_out], blk_vals):
            ref[...] = val.astype(ref.dtype)
        for ref, val in zip(acc_refs, acc_vals):
            ref[...] += val
        for ref, val in zip(carry_refs, new_carries):
            ref[...] = val
        if n_side:
            @pl.when((pl.program_id(0) == ncol - 1) & (i == nblk - 1))
            def _():
                side.wait(side_in, side_out, side_sems)

    res = pl.pallas_call(
        kern,
        name=name,
        grid=(ncol, nblk),
        in_specs=in_specs + side.in_specs,
        out_specs=out_specs + side.out_specs,
        out_shape=out_shape + side.out_shape,
        scratch_shapes=[pltpu.VMEM(shape, F32) for shape in carries] + side.scratch,
        compiler_params=pltpu.CompilerParams(dimension_semantics=("arbitrary", "arbitrary"),
                                             vmem_limit_bytes=VMEM_LIMIT),
    )(*operands, *side.operands)
    return list(res)


def _out(w, dtype, total=None, col=0):
    return dict(w=w, dtype=dtype, total=w if total is None else total, col=col)


def seq_fwd(name, fn, n_rows, ts, row_ins, params, outs, state_shapes=(), ncol=1, side=()):
    def body(xs, blks, ps, carries, r):
        o, new_st = fn(xs, list(carries), ps)
        return o, list(carries), [], new_st

    res = _block_call(name, body, n_rows, ts, ncol, False, row_ins, [], params, outs,
                      [dict(shape=s, dtype=F32) for s in state_shapes], [], list(state_shapes), side)
    n_o, n_s = len(outs), len(state_shapes)
    return (res[:n_o], res[n_o:n_o + n_s]) + ((res[n_o + n_s:],) if side else ())


def seq_bwd(name, fn, n_rows, ts, row_ins, params, cots, saved_states=(), din_dtypes=None, ncol=1, din_specs=None,
            side=(), cot_map=None, aux=()):
    n_x, n_p, n_st = len(row_ins), len(params), len(saved_states)
    halo_idx = [k for k, s in enumerate(row_ins) if s["halo"]]
    state_shapes = [a.shape[1:] for a in saved_states]

    def body(xs_all, blks, ps, carries, r):
        xs, cot_vals = xs_all[:n_x], xs_all[n_x:]
        d_state, d_halo = carries[:n_st], carries[n_st:]
        if cot_map is not None:
            cot_vals = cot_map(cot_vals, ps[n_p:])
        (o, _), vjp = jax.vjp(lambda a, b, c: fn(a, b, c), xs, blks, ps[:n_p])
        cot = [c.astype(v.dtype) for c, v in zip(cot_vals, o)]
        dxs, dst, dps = vjp((cot, list(d_state)))
        row_vals, new_halo = [], []
        for k, dx in enumerate(dxs):
            if k in halo_idx:
                hk = halo_idx.index(k)
                rows = dx.shape[0] - HALO
                tail = dx[rows:] + d_halo[hk]
                row_vals.append(jnp.concatenate([dx[HALO:rows], tail], axis=0))
                new_halo.append(dx[:HALO])
            else:
                row_vals.append(dx)
        return row_vals, [], list(dps), list(dst) + new_halo

    din_dtypes = din_dtypes or [F32] * n_x
    douts = []
    for k, s in enumerate(row_ins):
        total, col = (s["w"], 0) if din_specs is None or din_specs[k] is None else din_specs[k]
        douts.append(_out(s["w"], din_dtypes[k], total, col))
    accs = []
    for p in params:
        a = p["arr"]
        if p["col"] is None:
            accs.append(dict(shape=a.shape, total=None, col=None))
        else:
            accs.append(dict(shape=(a.shape[0], p["w"]), total=a.shape[1], col=p["col"]))
    carries = list(state_shapes) + [(HALO, row_ins[k]["w"]) for k in halo_idx]
    res = _block_call(name, body, n_rows, ts, ncol, True, list(row_ins) + list(cots), list(saved_states),
                      list(params) + list(aux), douts, [], accs, carries, side)
    return (res[:n_x], res[n_x:n_x + n_p]) + ((res[n_x + n_p:],) if side else ())


def matmul(name, a, b, mode, out_dtype, tm, tn, tk, side=(), a2=None, b2=None):
    side = _Side(side)
    n_side = len(side.jobs)
    if mode == "tn":
        (kk, m), n = a.shape, b.shape[1]
    else:
        (m, kk), n = a.shape, (b.shape[0] if mode == "nt" else b.shape[1])
    k1, n1 = kk, n
    if a2 is not None:
        assert mode != "tn" and b2 is None
        kk += a2.shape[1]
    if b2 is not None:
        assert mode == "tn"
        n += b2.shape[1]
    tm, tn, tk = min(tm, m), min(tn, n), min(tk, kk)
    nk, gm, gn = kk // tk, m // tm, n // tn
    assert m % tm == 0 and n % tn == 0 and kk % tk == 0 and k1 % tk == 0 and n1 % tn == 0, (name, a.shape, b.shape)
    nk1, gn1 = k1 // tk, n1 // tn
    if mode == "tn":
        a_specs = [pl.BlockSpec((tk, tm), lambda i, j, k: (k, i))]
        b_specs = [pl.BlockSpec((tk, tn), lambda i, j, k: (k, jnp.minimum(j, gn1 - 1)))]
        if b2 is not None:
            b_specs.append(pl.BlockSpec((tk, tn), lambda i, j, k: (k, jnp.maximum(j - gn1, 0))))
    else:
        a_specs = [pl.BlockSpec((tm, tk), lambda i, j, k: (i, jnp.minimum(k, nk1 - 1)))]
        if a2 is not None:
            a_specs.append(pl.BlockSpec((tm, tk), lambda i, j, k: (i, jnp.maximum(k - nk1, 0))))
        b_specs = [pl.BlockSpec((tn, tk), lambda i, j, k: (j, k)) if mode == "nt"
                   else pl.BlockSpec((tk, tn), lambda i, j, k: (k, j))]
    ca, cb = {"nn": (1, 0), "nt": (1, 1), "tn": (0, 0)}[mode]
    n_a, n_b = len(a_specs), len(b_specs)

    def kern(*refs):
        a_refs, b_refs = refs[:n_a], refs[n_a:n_a + n_b]
        rest = refs[n_a + n_b:]
        side_in = rest[:n_side]
        o_ref = rest[n_side]
        side_out = rest[1 + n_side:1 + 2 * n_side]
        acc_ref = rest[1 + 2 * n_side]
        side_sems = rest[2 + 2 * n_side:]
        i, j, k = pl.program_id(0), pl.program_id(1), pl.program_id(2)
        if n_side:
            @pl.when((i == 0) & (j == 0) & (k == 0))
            def _():
                side.start(side_in, side_out, side_sems)

        def step(a_ref, b_ref):
            part = lax.dot_general(a_ref[...], b_ref[...], (((ca,), (cb,)), ((), ())), preferred_element_type=F32)
            if nk == 1:
                o_ref[...] = part.astype(o_ref.dtype)
            else:
                @pl.when(k == 0)
                def _():
                    acc_ref[...] = part

                @pl.when(k > 0)
                def _():
                    acc_ref[...] += part

                @pl.when(k == nk - 1)
                def _():
                    o_ref[...] = acc_ref[...].astype(o_ref.dtype)

        if n_a == 2:
            pl.when(k < nk1)(lambda: step(a_refs[0], b_refs[0]))
            pl.when(k >= nk1)(lambda: step(a_refs[1], b_refs[0]))
        elif n_b == 2:
            pl.when(j < gn1)(lambda: step(a_refs[0], b_refs[0]))
            pl.when(j >= gn1)(lambda: step(a_refs[0], b_refs[1]))
        else:
            step(a_refs[0], b_refs[0])

        if n_side:
            @pl.when((i == gm - 1) & (j == gn - 1) & (k == nk - 1))
            def _():
                side.wait(side_in, side_out, side_sems)

    semantics = ("arbitrary",) * 3 if n_side else ("parallel", "parallel", "arbitrary")
    operands = [a] + ([a2] if a2 is not None else []) + [b] + ([b2] if b2 is not None else [])
    res = pl.pallas_call(
        kern,
        name=name,
        grid=(gm, gn, nk),
        in_specs=a_specs + b_specs + side.in_specs,
        out_specs=[pl.BlockSpec((tm, tn), lambda i, j, k: (i, j))] + side.out_specs,
        out_shape=[jax.ShapeDtypeStruct((m, n), out_dtype)] + side.out_shape,
        scratch_shapes=[pltpu.VMEM((tm, tn) if nk > 1 else (8, 128), F32)] + side.scratch,
        compiler_params=pltpu.CompilerParams(dimension_semantics=semantics, vmem_limit_bytes=VMEM_LIMIT),
    )(*operands, *side.operands)
    return (res[0], list(res[1:])) if n_side else res[0]


def all_gather(name, x, in_vmem):
    def body(x_ref, out_ref, send_sems, recv_sems, local_sem):
        x_, y_, c_ = _my_pos()
        me, sibling = (x_, y_, c_), (x_, y_, 1 - c_)
        chips = [(1 - x_, y_), (x_, 1 - y_), (1 - x_, 1 - y_)]

        def slot(px, py, pc):
            return out_ref.at[4 * px + 2 * py + pc]

        def copy(k, block, to, src=None):
            return pltpu.make_async_remote_copy(
                src_ref=slot(*block) if src is None else src, dst_ref=slot(*block),
                send_sem=send_sems.at[k], recv_sem=recv_sems.at[k], device_id=to, device_id_type=MESH)

        mine = pltpu.make_async_copy(x_ref, slot(*me), local_sem)
        mine.start()
        first = [copy(0, me, sibling, src=x_ref)]
        first += [copy(1 + j, me, (*chip, c_), src=x_ref) for j, chip in enumerate(chips)]
        for cp in first:
            cp.start()
        passed = [copy(4 + j, (*chip, c_), sibling) for j, chip in enumerate(chips)]
        for j, chip in enumerate(chips):
            copy(1 + j, (*chip, c_), me).wait_recv()
            passed[j].start()
        copy(0, sibling, me).wait_recv()
        for j, chip in enumerate(chips):
            copy(4 + j, (*chip, 1 - c_), me).wait_recv()
        for cp in first + passed:
            cp.wait_send()
        mine.wait()

    space = pltpu.VMEM if in_vmem else pl.ANY
    return pl.pallas_call(
        body,
        name=name,
        out_shape=jax.ShapeDtypeStruct((N_DEV,) + x.shape, x.dtype),
        in_specs=[pl.BlockSpec(memory_space=space)],
        out_specs=pl.BlockSpec(memory_space=space),
        scratch_shapes=[pltpu.SemaphoreType.DMA((7,)), pltpu.SemaphoreType.DMA((7,)), pltpu.SemaphoreType.DMA],
        compiler_params=pltpu.CompilerParams(vmem_limit_bytes=VMEM_LIMIT),
    )(x)


def all_to_all(name, g):
    def body(g_ref, out_ref, send_sems, recv_sems, local_sem):
        x_, y_, c_ = _my_pos()
        me = 4 * x_ + 2 * y_ + c_

        def peer(k):
            fx, fy, fc = (k >> 2) & 1, (k >> 1) & 1, k & 1
            return (1 - x_ if fx else x_, 1 - y_ if fy else y_, 1 - c_ if fc else c_)

        def copy(k):
            px, py, pc = peer(k)
            return pltpu.make_async_remote_copy(
                src_ref=g_ref.at[4 * px + 2 * py + pc], dst_ref=out_ref.at[me],
                send_sem=send_sems.at[k - 1], recv_sem=recv_sems.at[k - 1], device_id=(px, py, pc), device_id_type=MESH)

        def landing(k):
            px, py, pc = peer(k)
            return pltpu.make_async_remote_copy(
                src_ref=g_ref.at[me], dst_ref=out_ref.at[4 * px + 2 * py + pc],
                send_sem=send_sems.at[k - 1], recv_sem=recv_sems.at[k - 1], device_id=(px, py, pc), device_id_type=MESH)

        mine = pltpu.make_async_copy(g_ref.at[me], out_ref.at[me], local_sem)
        mine.start()
        sends = [copy(k) for k in range(1, N_DEV)]
        for cp in sends:
            cp.start()
        for k in range(1, N_DEV):
            landing(k).wait_recv()
        for cp in sends:
            cp.wait_send()
        mine.wait()

    return pl.pallas_call(
        body,
        name=name,
        out_shape=jax.ShapeDtypeStruct(g.shape, g.dtype),
        in_specs=[pl.BlockSpec(memory_space=pl.ANY)],
        out_specs=pl.BlockSpec(memory_space=pl.ANY),
        scratch_shapes=[pltpu.SemaphoreType.DMA((7,)), pltpu.SemaphoreType.DMA((7,)), pltpu.SemaphoreType.DMA],
    )(g)


def sum_blocks(name, g):
    def body(g_ref, o_ref):
        acc = g_ref[0]
        for s in range(1, N_DEV):
            acc = acc + g_ref[s]
        o_ref[...] = acc

    r = g.shape[1]
    tr = r // 4 if r % 32 == 0 else r
    return pl.pallas_call(
        body, name=name, grid=(r // tr,),
        in_specs=[pl.BlockSpec((N_DEV, tr, 128), lambda i: (0, i, 0))],
        out_specs=pl.BlockSpec((tr, 128), lambda i: (i, 0)),
        out_shape=jax.ShapeDtypeStruct((r, 128), F32),
        compiler_params=pltpu.CompilerParams(vmem_limit_bytes=VMEM_LIMIT),
    )(g)


def _adamw_math(w, g, m, v):
    m = ADAM_B1 * m + (1.0 - ADAM_B1) * g
    v = ADAM_B2 * v + (1.0 - ADAM_B2) * (g * g)
    m_hat = m / (1.0 - ADAM_B1 ** ADAM_STEP)
    v_hat = v / (1.0 - ADAM_B2 ** ADAM_STEP)
    delta = -ADAM_LR * (m_hat / (jnp.sqrt(v_hat) + ADAM_EPS) + ADAM_WD * w)
    return delta, m, v


ADAMW_BLOCK_BYTES = 3 << 19


def _row_tile(rows, row_bytes, limit):
    best = 8
    for t in range(8, rows + 1, 8):
        if rows % t == 0 and t * row_bytes <= limit:
            best = t
    return best


def adamw_big(name, w, lands, m, v):
    depth, r, c = w.shape
    outs = None
    for (l, part), land in sorted(lands.items()):
        rows = land.shape[1]
        tr = _row_tile(rows, 4 * (-(-c // 128) * 128), ADAMW_BLOCK_BYTES)
        first = part * rows // tr

        def body(w_ref, l_ref, m_ref, v_ref, *rest):
            g_out, d_out, m_out, v_out = rest[-4:]
            g = l_ref[0].astype(F32)
            for s in range(1, N_DEV):
                g = g + l_ref[s].astype(F32)
            delta, m_new, v_new = _adamw_math(w_ref[...], g, m_ref[...], v_ref[...])
            g_out[...] = g
            d_out[...] = delta
            m_out[...] = m_new
            v_out[...] = v_new

        spec = pl.BlockSpec((None, tr, c), lambda i, l=l, first=first: (l, first + i, 0))
        carried = [] if outs is None else list(outs)
        outs = pl.pallas_call(
            body, name=f"{name}_{l}_{part}", grid=(rows // tr,),
            in_specs=[spec, pl.BlockSpec((N_DEV, tr, c), lambda i: (0, i, 0)), spec, spec]
            + [pl.BlockSpec(memory_space=pl.ANY)] * len(carried),
            out_specs=[spec] * 4,
            out_shape=[jax.ShapeDtypeStruct((depth, r, c), F32)] * 4,
            input_output_aliases={4 + k: k for k in range(len(carried))},
            compiler_params=pltpu.CompilerParams(dimension_semantics=("parallel",), vmem_limit_bytes=VMEM_LIMIT),
        )(w, land, m, v, *carried)
    return outs


def adamw_small(name, ws, gs, ms, vs):
    n = len(ws)

    def body(*refs):
        ins, outs = refs[:4 * n], refs[4 * n:]
        for k in range(n):
            delta, m_new, v_new = _adamw_math(ins[k][...], ins[n + k][...], ins[2 * n + k][...], ins[3 * n + k][...])
            outs[k][...] = delta
            outs[n + k][...] = m_new
            outs[2 * n + k][...] = v_new

    res = pl.pallas_call(
        body, name=name,
        out_shape=[jax.ShapeDtypeStruct(w.shape, F32) for w in ws] * 3,
        compiler_params=pltpu.CompilerParams(vmem_limit_bytes=VMEM_LIMIT),
    )(*ws, *gs, *ms, *vs)
    return res[:n], res[n:2 * n], res[2 * n:]


def cast_bf16(name, w):
    depth, r, c = w.shape
    tr = _row_tile(r, 4 * (-(-c // 128) * 128), ADAMW_BLOCK_BYTES)

    def body(w_ref, o_ref):
        o_ref[...] = w_ref[...].astype(BF16)

    spec = pl.BlockSpec((None, tr, c), lambda l, i: (l, i, 0))
    return pl.pallas_call(body, name=name, grid=(depth, r // tr), in_specs=[spec], out_specs=spec,
                          out_shape=jax.ShapeDtypeStruct((depth, r, c), BF16),
                          compiler_params=pltpu.CompilerParams(dimension_semantics=("parallel", "parallel")))(w)


def _rows_of(shape):
    return -(-math.prod(shape) // 128)


def _pack(arrs):
    pieces = []
    for a in arrs:
        flat = a.reshape(-1).astype(F32)
        pieces.append(jnp.pad(flat, (0, (-flat.shape[0]) % 128)).reshape(-1, 128))
    rows = sum(p.shape[0] for p in pieces)
    if rows % 8:
        pieces.append(jnp.zeros((8 - rows % 8, 128), F32))
    return jnp.concatenate(pieces, axis=0)


def _unpack(packed, shapes, lead=()):
    out, r0 = [], 0
    for s in shapes:
        rows, n = _rows_of(s), math.prod(s)
        piece = packed[..., r0:r0 + rows, :].reshape(lead + (rows * 128,))
        out.append(piece[..., :n].reshape(lead + tuple(s)))
        r0 += rows
    return out


def _block_diag(w):
    h, d, _ = w.shape
    eye = jnp.eye(h, dtype=w.dtype)
    return (eye[:, None, :, None] * w[:, :, None, :]).reshape(h * d, h * d)


def _block_diag_grad(g, h):
    d = g.shape[0] // h
    eye = jnp.eye(h, dtype=g.dtype)
    return jnp.sum(g.reshape(h, d, h, d) * eye[:, None, :, None], axis=2)


def _layer_params(wt, l):
    gp = jnp.pad(jnp.stack([wt["gdn_a_log"][l], wt["gdn_dt_bias"][l]]), ((0, 6), (4, 128 - 4 - GDN_HEADS)))
    d_ffh = wt["ffn_conv_w"].shape[-1] // 2
    return dict(
        pre_mix=wt["pre_mix_norm"][l][None], post_mix=wt["post_mix_norm"][l][None],
        pre_ffn=wt["pre_ffn_norm"][l][None], post_ffn=wt["post_ffn_norm"][l][None],
        gdn_cw=wt["gdn_conv_w"][l], gdn_gp=gp, gdn_nw=wt["gdn_norm_w"][l][None],
        lru_cw=wt["lru_conv_w"][l], lru_cb=wt["lru_conv_b"][l][None],
        lru_wa=_block_diag(wt["lru_wa"][l]), lru_ba=wt["lru_ba"][l].reshape(1, -1),
        lru_wx=_block_diag(wt["lru_wx"][l]), lru_bx=wt["lru_bx"][l].reshape(1, -1),
        lru_lam=wt["lru_lambda"][l][None], gw0=wt["grp_norm_w"][l, 0][None], gw1=wt["grp_norm_w"][l, 1][None],
        gw2=wt["grp_norm_w"][l, 2][None],
        sgu_lnw=wt["sgu_ln_w"][l][None], sgu_lnb=wt["sgu_ln_b"][l][None],
        sgu_ws=wt["sgu_ws"][l].reshape(SGU_GROUPS * SGU_CHUNK, SGU_CHUNK),
        sgu_bt=jnp.pad(wt["sgu_b"][l].T, ((0, 0), (0, 128 - SGU_GROUPS))),
        sc_cw=wt["sconv_w"][l],
        ffn_cw=wt["ffn_conv_w"][l], ffn_cb=wt["ffn_conv_b"][l][None], d_ffh=d_ffh,
    )


TS_ROW = 256
TS_GDN = 256
TS_FFN = 512
TS_FFN_BWD = 256
TC_FFN = 512


def _mixers_fwd(l, p, lp, n, side, side_lru):
    qkv = _rin(p, 3 * D_G, 0, halo=True)
    z = _rin(p, D_G, 3)
    ba = _rin(p, 128, BA_COL // 128)
    gdn_ps = [_par(lp["gdn_cw"]), _par(lp["gdn_gp"]), _par(lp["gdn_nw"])]
    res = seq_fwd(f"gdn_fwd_{l}", fn_gdn, n, TS_GDN, [qkv, z, ba], gdn_ps, [_out(D_G, BF16)],
                  state_shapes=[(GDN_HEADS * GDN_DIM, GDN_DIM)], side=side)
    (y_a,), (gdn_st,), side_res = res if side else res + ([],)
    lru_x = _rin(p, D_G, 4, halo=True)
    lru_gate = _rin(p, D_G, 5)
    lru_ps = [_par(lp[k]) for k in ("lru_cw", "lru_cb", "lru_wa", "lru_ba", "lru_wx", "lru_bx", "lru_lam", "gw0")]
    res = seq_fwd(f"lru_fwd_{l}", fn_lru, n, TS_ROW, [lru_x, lru_gate], lru_ps, [_out(D_G, BF16)],
                  state_shapes=[(1, D_G)], side=side_lru)
    (y_b,), (lru_st,), side_res_lru = res if side_lru else res + ([],)
    uv = _rin(p, 2 * D_G, 3)
    sgu_ps = [_par(lp[k]) for k in ("sgu_lnw", "sgu_lnb", "sgu_ws", "sgu_bt", "gw1")]
    (y_c,), _ = seq_fwd(f"sgu_fwd_{l}", fn_sgu, n, TS_ROW, [uv], sgu_ps, [_out(D_G, BF16)])
    sc = [_rin(p, D_G, 8), _rin(p, D_G, 9, halo=True), _rin(p, D_G, 10, halo=True)]
    sc_ps = [_par(lp["sc_cw"]), _par(lp["gw2"])]
    (y_d,), _ = seq_fwd(f"sconv_fwd_{l}", fn_sconv, n, TS_ROW, sc, sc_ps, [_out(D_G, BF16)])
    ins = dict(gdn=([qkv, z, ba], gdn_ps, [gdn_st]), lru=([lru_x, lru_gate], lru_ps, [lru_st]),
               sgu=([uv], sgu_ps, []), sc=(sc, sc_ps, []))
    return jnp.concatenate([y_a, y_b, y_c, y_d], axis=1), ins, side_res, side_res_lru


def _mixers_bwd(l, dymix, ins, n, side):
    cot = lambda g: [_rin(dymix, D_G, g)]
    xs, ps, st = ins["gdn"]
    res = seq_bwd(f"gdn_bwd_{l}", fn_gdn, n, TS_GDN, xs, ps, cot(0), st, [BF16, BF16, BF16], side=side)
    (dqkv, dz, dba), g_gdn, side_res = res if side else res + ([],)
    xs, ps, st = ins["lru"]
    (dlx, dlg), g_lru = seq_bwd(f"lru_bwd_{l}", fn_lru, n, TS_ROW, xs, ps, cot(1), st, [BF16, BF16])
    xs, ps, st = ins["sgu"]
    (duv,), g_sgu = seq_bwd(f"sgu_bwd_{l}", fn_sgu, n, TS_ROW, xs, ps, cot(2), st, [BF16])
    xs, ps, st = ins["sc"]
    (dsb, dsc, dsh), g_sc = seq_bwd(f"sconv_bwd_{l}", fn_sconv, n, TS_ROW, xs, ps, cot(3), st, [BF16, BF16, BF16])
    dp = jnp.concatenate([dqkv, dz, dlx, dlg, duv, dsb, dsc, dsh, dba], axis=1)
    return dp, dict(gdn=g_gdn, lru=g_lru, sgu=g_sgu, sc=g_sc), side_res


def _ffn_ops(hid, lp):
    d_ffh = lp["d_ffh"]
    off = d_ffh // TC_FFN
    xs = [_rin(hid, TC_FFN, lambda j: j, halo=True), _rin(hid, TC_FFN, lambda j: j + off, halo=True)]
    ps = [_par(lp["ffn_cw"], TC_FFN, lambda j: j), _par(lp["ffn_cw"], TC_FFN, lambda j: j + off),
          _par(lp["ffn_cb"], TC_FFN, lambda j: j), _par(lp["ffn_cb"], TC_FFN, lambda j: j + off)]
    return xs, ps, d_ffh


_FROM_BLOCKS = dict(
    w_in=lambda b: _regroup_w_in(b.transpose(1, 0, 2).reshape(b.shape[1], -1)),
    ffn_up=lambda b: b.transpose(1, 0, 2).reshape(b.shape[1], -1),
    w_out=lambda b: b.reshape(-1, b.shape[2]),
    ffn_down=lambda b: b.reshape(-1, b.shape[2]),
)
_TO_BLOCKS = dict(
    w_in=lambda g: _ungroup_w_in(g).reshape(g.shape[0], N_DEV, -1).transpose(1, 0, 2),
    ffn_up=lambda g: g.reshape(g.shape[0], N_DEV, -1).transpose(1, 0, 2),
    w_out=lambda g: g.reshape(N_DEV, -1, g.shape[1]),
    ffn_down=lambda g: g.reshape(N_DEV, -1, g.shape[1]),
)


class _Traffic:
    PARTS = dict(w_in=1, w_out=1, ffn_up=2, ffn_down=1)

    def __init__(self, whole=None, shards=None):
        self.whole = dict(whole or {})
        self.shards = shards
        self.gathered = {}
        self.pending = {}
        self.landed = {}

    def _rows(self, key):
        name, l, part = key
        rows = self.shards[name].shape[1] // self.PARTS[name]
        return slice(part * rows, (part + 1) * rows)

    def jobs(self, gather=(), exchange=()):
        if self.shards is None:
            return [], []
        keys = [("gather", k) for k in gather if k not in self.gathered and k[:2] not in self.whole]
        keys += [("exchange", k) for k in exchange if k in self.pending]
        jobs = [(kind, self.shards[k[0]][k[1]][self._rows(k)] if kind == "gather" else self.pending[k])
                for kind, k in keys]
        return jobs, keys

    def done(self, keys, results):
        for (kind, k), r in zip(keys, results):
            if kind == "gather":
                self.gathered[k] = r
            else:
                self.landed[k] = r
                del self.pending[k]

    def weight(self, name, l):
        if (name, l) not in self.whole:
            parts = []
            for part in range(self.PARTS[name]):
                k = (name, l, part)
                if k not in self.gathered:
                    self.gathered[k] = all_gather(f"gather_{name}_{l}_{part}", self.shards[name][l][self._rows(k)], False)
                parts.append(self.gathered[k])
            blocks = parts[0] if len(parts) == 1 else jnp.concatenate(parts, axis=1)
            self.whole[(name, l)] = _FROM_BLOCKS[name](blocks)
        return self.whole[(name, l)]

    def grad(self, name, l, g):
        if self.shards is None:
            self.landed[(name, l)] = g
            return
        blocks = _TO_BLOCKS[name](g)
        for part in range(self.PARTS[name]):
            k = (name, l, part)
            self.pending[k] = blocks[:, self._rows(k)]

    def flush(self):
        for (name, l, part), blocks in list(self.pending.items()):
            self.landed[(name, l, part)] = all_to_all(f"exchange_{name}_{l}_{part}", blocks)
            del self.pending[(name, l, part)]


def local_step(x, target, wt, tr):
    n, d = x.shape
    depth = wt["pre_mix_norm"].shape[0]
    lps = [_layer_params(wt, l) for l in range(depth)]
    saved = []
    xin = x

    def mm(name, a, b, mode, dtype, tm, tn, tk, gather=(), exchange=(), **split):
        jobs, keys = tr.jobs(gather, exchange)
        if not jobs:
            return matmul(name, a, b, mode, dtype, tm, tn, tk, **split)
        out, res = matmul(name, a, b, mode, dtype, tm, tn, tk, side=jobs, **split)
        tr.done(keys, res)
        return out

    (h,), _ = seq_fwd("norm_fwd", fn_norm, n, TS_ROW, [_rin(x)], [_par(lps[0]["pre_mix"])], [_out(d, BF16)])
    dx_last = loss = None
    for l in range(depth):
        lp = lps[l]
        p = mm(f"w_in_fwd_{l}", h, tr.weight("w_in", l), "nn", F32, 1024, N_INP // 5, d,
               gather=[("ffn_up", l, 0)])
        jobs, keys = tr.jobs(gather=[("ffn_up", l, 1)])
        jobs_lru, keys_lru = tr.jobs(gather=[("w_out", l, 0)])
        ymix, mix_ins, res, res_lru = _mixers_fwd(l, p, lp, n, jobs, jobs_lru)
        tr.done(keys, res)
        tr.done(keys_lru, res_lru)
        y = mm(f"w_out_fwd_{l}", ymix, tr.weight("w_out", l), "nn", F32, 1024, 1024, d)
        res_ps = [_par(lp["post_mix"]), _par(lp["pre_ffn"])]
        (x1, h2), _ = seq_fwd(f"res_mix_fwd_{l}", fn_res, n, TS_ROW, [_rin(xin), _rin(y)], res_ps,
                              [_out(d, F32), _out(d, BF16)])
        nxt = l + 1 < depth
        hid = mm(f"ffn_up_fwd_{l}", h2, tr.weight("ffn_up", l), "nn", F32, 1024, 1024, d,
                 gather=[("ffn_down", l, 0)] + ([("w_out", l + 1, 0)] if nxt else []))
        f_xs, f_ps, d_ffh = _ffn_ops(hid, lp)
        (act,), _ = seq_fwd(f"ffn_act_fwd_{l}", fn_ffn, n, TS_FFN, f_xs, f_ps,
                            [_out(TC_FFN, BF16, d_ffh, lambda j: j)], ncol=d_ffh // TC_FFN)
        yf = mm(f"ffn_down_fwd_{l}", act, tr.weight("ffn_down", l), "nn", F32, 1024, 1024, d_ffh // 2,
                gather=[("w_in", l + 1, 0)] if nxt else [])
        rec = dict(x=xin, h=h, mix_ins=mix_ins, ymix=ymix, y=y, x1=x1, h2=h2, f_xs=f_xs, f_ps=f_ps, act=act, yf=yf)
        if l + 1 < depth:
            ps = [_par(lp["post_ffn"]), _par(lps[l + 1]["pre_mix"])]
            (x2, h), _ = seq_fwd(f"res_ffn_fwd_{l}", fn_res, n, TS_ROW, [_rin(x1), _rin(yf)], ps,
                                 [_out(d, F32), _out(d, BF16)])
            rec["res_ffn_ps"] = ps
            xin = x2
        else:
            def body(xs, blks, ps, carries, r):
                x1_, yf_, t_ = xs
                e = x1_ + _rms(yf_, ps[0]) - t_
                part = 0.5 * jnp.sum(jnp.mean(e * e, axis=-1, keepdims=True), axis=0, keepdims=True)
                return [e * (1.0 / d)], [], [jnp.broadcast_to(part, (8, 128))], []

            ps = [_par(lp["post_ffn"])]
            dx_last, loss = _block_call("loss_fwd", body, n, TS_ROW, 1, False, [_rin(x1), _rin(yf), _rin(target)],
                                        [], ps, [_out(d, F32)], [], [dict(shape=(8, 128), total=None, col=None)], [])
            rec["res_ffn_ps"] = ps
        saved.append(rec)

    grads = {}
    dx2, dh_next = dx_last, None
    for l in reversed(range(depth)):
        rec, lp = saved[l], lps[l]
        d_ffh = lp["d_ffh"]
        g = {}
        if dh_next is None:
            (dx1, dyf), (g["post_ffn"],) = seq_bwd(f"res_ffn_bwd_{l}", fn_res_last, n, TS_ROW,
                                                   [_rin(rec["x1"]), _rin(rec["yf"])], rec["res_ffn_ps"], [_rin(dx2)],
                                                   din_dtypes=[F32, BF16])
        else:
            (dx1, dyf), (g["post_ffn"], g_next_pre) = seq_bwd(
                f"res_ffn_bwd_{l}", fn_res, n, TS_ROW, [_rin(rec["x1"]), _rin(rec["yf"])], rec["res_ffn_ps"],
                [_rin(dx2), _rin(dh_next)], din_dtypes=[F32, BF16])
            grads[l + 1]["pre_mix"] = g_next_pre
        tr.grad("ffn_down", l, mm(f"ffn_down_dw_{l}", rec["act"], dyf, "tn", BF16, d_ffh // 4, 1024, 2048))

        def dact(cot_blocks, aux_blocks):
            return [_dot(cot_blocks[0], aux_blocks[0], 1, 1).astype(BF16)]

        (dhg, dhv), (g_cwg, g_cwv, g_cbg, g_cbv) = seq_bwd(
            f"ffn_act_bwd_{l}", fn_ffn, n, TS_FFN_BWD, rec["f_xs"], rec["f_ps"], [_rin(dyf)],
            din_dtypes=[BF16, BF16], ncol=d_ffh // TC_FFN, din_specs=[(d_ffh, lambda j: j), (d_ffh, lambda j: j)],
            cot_map=dact, aux=[_par(tr.weight("ffn_down", l), TC_FFN, row=lambda j: j)])
        g["ffn_cw"] = jnp.concatenate([g_cwg[:, :d_ffh], g_cwv[:, d_ffh:]], axis=1)
        g["ffn_cb"] = jnp.concatenate([g_cbg[:, :d_ffh], g_cbv[:, d_ffh:]], axis=1)
        dh2 = mm(f"ffn_up_dx_{l}", dhg, tr.weight("ffn_up", l), "nt", BF16, 1024, 1024, d_ffh // 2,
                 exchange=[("ffn_down", l, 0)], a2=dhv)
        tr.grad("ffn_up", l, mm(f"ffn_up_dw_{l}", rec["h2"], dhg, "tn", BF16, 1024, TC_FFN, 2048, b2=dhv))
        (dx, dy), (g["post_mix"], g["pre_ffn"]) = seq_bwd(
            f"res_mix_bwd_{l}", fn_res, n, TS_ROW, [_rin(rec["x"]), _rin(rec["y"])],
            [_par(lp["post_mix"]), _par(lp["pre_ffn"])], [_rin(dx1), _rin(dh2)], din_dtypes=[F32, BF16])
        dymix = mm(f"w_out_dx_{l}", dy, tr.weight("w_out", l), "nt", BF16, 1024, 1024, d)
        tr.grad("w_out", l, mm(f"w_out_dw_{l}", rec["ymix"], dy, "tn", BF16, 1024, 1024, 2048))
        jobs, keys = tr.jobs(exchange=[("ffn_up", l, 0), ("w_out", l, 0)])
        dp, g["mix"], res = _mixers_bwd(l, dymix, rec["mix_ins"], n, jobs)
        tr.done(keys, res)
        tr.grad("w_in", l, mm(f"w_in_dw_{l}", rec["h"], dp, "tn", BF16, 1024, N_INP // 5, 2048,
                              exchange=[("ffn_up", l, 1)]))
        dh = mm(f"w_in_dx_{l}", dp, tr.weight("w_in", l), "nt", BF16, 1024, 1024, N_INP // 3,
                exchange=[("w_in", l, 0)])
        grads[l] = g
        dx2, dh_next = dx, dh
    (grad_x,), (g_pre0,) = seq_bwd("norm_bwd", fn_norm_keep, n, TS_ROW, [_rin(x)], [_par(lps[0]["pre_mix"])],
                                   [_rin(dh_next), _rin(dx2)])
    grads[0]["pre_mix"] = g_pre0
    tr.flush()
    return loss[0, 0], grad_x, _name_grads(grads, depth)


def _name_grads(grads, depth):
    per = {k: [] for k in SMALL}
    for l in range(depth):
        g = grads[l]
        m = g["mix"]
        cw, gp, nw = m["gdn"]
        lcw, lcb, lwa, lba, lwx, lbx, llam, gw0 = m["lru"]
        lnw, lnb, ws, bst, gw1 = m["sgu"]
        scw, gw2 = m["sc"]
        per["pre_mix_norm"].append(g["pre_mix"][0])
        per["gdn_conv_w"].append(cw)
        per["gdn_a_log"].append(gp[0, 4:8])
        per["gdn_dt_bias"].append(gp[1, 4:8])
        per["gdn_norm_w"].append(nw[0])
        per["lru_conv_w"].append(lcw)
        per["lru_conv_b"].append(lcb[0])
        per["lru_wa"].append(_block_diag_grad(lwa, LRU_BLOCKS))
        per["lru_ba"].append(lba.reshape(LRU_BLOCKS, -1))
        per["lru_wx"].append(_block_diag_grad(lwx, LRU_BLOCKS))
        per["lru_bx"].append(lbx.reshape(LRU_BLOCKS, -1))
        per["lru_lambda"].append(llam[0])
        per["sgu_ln_w"].append(lnw[0])
        per["sgu_ln_b"].append(lnb[0])
        per["sgu_ws"].append(ws.reshape(SGU_GROUPS, SGU_CHUNK, SGU_CHUNK))
        per["sgu_b"].append(bst[:, :SGU_GROUPS].T)
        per["sconv_w"].append(scw)
        per["grp_norm_w"].append(jnp.concatenate([gw0, gw1, gw2], axis=0))
        per["post_mix_norm"].append(g["post_mix"][0])
        per["pre_ffn_norm"].append(g["pre_ffn"][0])
        per["ffn_conv_w"].append(g["ffn_cw"])
        per["ffn_conv_b"].append(g["ffn_cb"][0])
        per["post_ffn_norm"].append(g["post_ffn"][0])
    return {k: jnp.stack(v) for k, v in per.items()}


def _regroup_w_in(w):
    pad = jnp.zeros(w.shape[:-1] + (N_INP - N_IN,), w.dtype)
    return jnp.concatenate([w[..., :2048], w[..., 2056:], w[..., 2048:2056], pad], axis=-1)


def _ungroup_w_in(g):
    return jnp.concatenate([g[..., :2048], g[..., BA_COL:BA_COL + 8], g[..., 2048:BA_COL]], axis=-1)


def kernel(x, pre_mix_norm, w_in, gdn_conv_w, gdn_a_log, gdn_dt_bias, gdn_norm_w, lru_conv_w, lru_conv_b, lru_wa, lru_ba, lru_wx, lru_bx, lru_lambda, sgu_ln_w, sgu_ln_b, sgu_ws, sgu_b, sconv_w, grp_norm_w, w_out, post_mix_norm, pre_ffn_norm, ffn_up, ffn_conv_w, ffn_conv_b, ffn_down, post_ffn_norm, loss_target, m_pre_mix_norm, m_w_in, m_gdn_conv_w, m_gdn_a_log, m_gdn_dt_bias, m_gdn_norm_w, m_lru_conv_w, m_lru_conv_b, m_lru_wa, m_lru_ba, m_lru_wx, m_lru_bx, m_lru_lambda, m_sgu_ln_w, m_sgu_ln_b, m_sgu_ws, m_sgu_b, m_sconv_w, m_grp_norm_w, m_w_out, m_post_mix_norm, m_pre_ffn_norm, m_ffn_up, m_ffn_conv_w, m_ffn_conv_b, m_ffn_down, m_post_ffn_norm, v_pre_mix_norm, v_w_in, v_gdn_conv_w, v_gdn_a_log, v_gdn_dt_bias, v_gdn_norm_w, v_lru_conv_w, v_lru_conv_b, v_lru_wa, v_lru_ba, v_lru_wx, v_lru_bx, v_lru_lambda, v_sgu_ln_w, v_sgu_ln_b, v_sgu_ws, v_sgu_b, v_sconv_w, v_grp_norm_w, v_w_out, v_post_mix_norm, v_pre_ffn_norm, v_ffn_up, v_ffn_conv_w, v_ffn_conv_b, v_ffn_down, v_post_ffn_norm):
    args = locals()
    w_loc = {k: args[k] for k in WEIGHTS}
    m_loc = {k: args["m_" + k] for k in WEIGHTS}
    v_loc = {k: args["v_" + k] for k in WEIGHTS}
    depth = pre_mix_norm.shape[0]
    x_, y_, c_ = _my_pos()
    me = 4 * x_ + 2 * y_ + c_

    tr = _Traffic(shards={name: cast_bf16(f"cast_{name}", w_loc[name]) for name in BIG})
    wt = {k: w_loc[k] for k in SMALL}
    shard_shapes = [w_loc[k].shape for k in SHARDED_SMALL]
    gathered = all_gather("gather_small", _pack([w_loc[k] for k in SHARDED_SMALL]), True)
    for k, a in zip(SHARDED_SMALL, _unpack(gathered, shard_shapes, lead=(N_DEV,))):
        a = jnp.moveaxis(a, 0, -2)
        wt[k] = a.reshape(a.shape[:-2] + (-1,))

    loss_part, grad_x, g_full = local_step(x[0], loss_target[0], wt, tr)
    loss = lax.psum(loss_part, ("x", "y", "c"))

    outs_g, outs_d, outs_m, outs_v = {}, {}, {}, {}
    for name in BIG:
        lands = {(l, part): a for (n_, l, part), a in tr.landed.items() if n_ == name}
        outs_g[name], outs_d[name], outs_m[name], outs_v[name] = adamw_big(
            f"adamw_{name}", w_loc[name], lands, m_loc[name], v_loc[name])

    full_shapes = [g_full[k].shape for k in SMALL]
    g_all = all_gather("gather_small_grads", _pack([g_full[k] for k in SMALL]), True)
    g_sum = _unpack(sum_blocks("sum_small_grads", g_all), full_shapes)
    g_small = {}
    for k, g in zip(SMALL, g_sum):
        if k in SHARDED_SMALL:
            w = w_loc[k].shape[-1]
            g = lax.dynamic_slice_in_dim(g, me * w, w, axis=g.ndim - 1)
        g_small[k] = g
    d_s, m_s, v_s = adamw_small("adamw_small", [w_loc[k] for k in SMALL], [g_small[k] for k in SMALL],
                                [m_loc[k] for k in SMALL], [v_loc[k] for k in SMALL])
    for k_i, k in enumerate(SMALL):
        outs_g[k], outs_d[k], outs_m[k], outs_v[k] = g_small[k], d_s[k_i], m_s[k_i], v_s[k_i]

    return (loss, grad_x[None], *[outs_g[k] for k in WEIGHTS], *[outs_d[k] for k in WEIGHTS],
            *[outs_m[k] for k in WEIGHTS], *[outs_v[k] for k in WEIGHTS])
```

```python
import functools
import math

import jax
import jax.numpy as jnp
from jax import lax
from jax.experimental import pallas as pl
from jax.experimental.pallas import tpu as pltpu

F32 = jnp.float32
BF16 = jnp.bfloat16
EPS = 1e-6
HALO = 8
VMEM_LIMIT = 56 * 1024 * 1024
MESH = pl.DeviceIdType.MESH
N_DEV = 8

ADAM_LR, ADAM_B1, ADAM_B2, ADAM_EPS, ADAM_WD, ADAM_STEP = 0.001, 0.9, 0.999, 1e-08, 0.01, 10

GDN_HEADS, GDN_DIM, GDN_CHUNK = 4, 128, 64
SGU_GROUPS, SGU_CHUNK = 4, 128
LRU_BLOCKS, LRU_C = 8, 8.0
D_G = 512
N_IN = 5640
N_INP = 5760
BA_COL = 5632

SHARDED_SMALL = ("gdn_conv_w", "lru_conv_w", "sconv_w", "grp_norm_w", "ffn_conv_w")
BIG = ("w_in", "w_out", "ffn_up", "ffn_down")
WEIGHTS = ("pre_mix_norm", "w_in", "gdn_conv_w", "gdn_a_log", "gdn_dt_bias", "gdn_norm_w", "lru_conv_w",
           "lru_conv_b", "lru_wa", "lru_ba", "lru_wx", "lru_bx", "lru_lambda", "sgu_ln_w", "sgu_ln_b", "sgu_ws",
           "sgu_b", "sconv_w", "grp_norm_w", "w_out", "post_mix_norm", "pre_ffn_norm", "ffn_up", "ffn_conv_w",
           "ffn_conv_b", "ffn_down", "post_ffn_norm")
SMALL = tuple(n for n in WEIGHTS if n not in BIG)


def _dot(a, b, ca, cb):
    return lax.dot_general(a.astype(BF16), b.astype(BF16), (((ca,), (cb,)), ((), ())),
                           preferred_element_type=F32)


@jax.custom_vjp
def _mm(a, b):
    return _dot(a, b, 1, 0)


def _mm_f(a, b):
    return _dot(a, b, 1, 0), (a, b)


def _mm_b(res, g):
    a, b = res
    return _dot(g, b, 1, 1), _dot(a, g, 0, 0)


_mm.defvjp(_mm_f, _mm_b)


@jax.custom_vjp
def _mm_nt(a, b):
    return _dot(a, b, 1, 1)


def _mm_nt_f(a, b):
    return _dot(a, b, 1, 1), (a, b)


def _mm_nt_b(res, g):
    a, b = res
    return _dot(g, b, 1, 0), _dot(g, a, 0, 0)


_mm_nt.defvjp(_mm_nt_f, _mm_nt_b)


@jax.custom_vjp
def _mm_tn(a, b):
    return _dot(a, b, 0, 0)


def _mm_tn_f(a, b):
    return _dot(a, b, 0, 0), (a, b)


def _mm_tn_b(res, g):
    a, b = res
    return _dot(b, g, 1, 1), _dot(a, g, 1, 0)


_mm_tn.defvjp(_mm_tn_f, _mm_tn_b)


def _dot_exact(a, b, ca, cb):
    return lax.dot_general(a, b, (((ca,), (cb,)), ((), ())), precision=lax.Precision.HIGHEST,
                           preferred_element_type=F32)


@functools.partial(jax.custom_vjp, nondiff_argnums=(1,))
def _shift_rows(x, s):
    return pltpu.roll(x, s, 0)


def _shift_rows_f(x, s):
    return pltpu.roll(x, s, 0), None


def _shift_rows_b(s, _, g):
    return (pltpu.roll(g, (g.shape[0] - s) % g.shape[0], 0),)


_shift_rows.defvjp(_shift_rows_f, _shift_rows_b)


def _sigmoid(x):
    return 1.0 / (1.0 + jnp.exp(-x))


def _silu(x):
    return x * _sigmoid(x)


GELU_C, GELU_A = 0.7978845608028654, 0.044715


@jax.custom_vjp
def _gelu(x):
    return 0.5 * x * (1.0 + jnp.tanh(GELU_C * (x + GELU_A * (x * x * x))))


def _gelu_f(x):
    t = jnp.tanh(GELU_C * (x + GELU_A * (x * x * x)))
    return 0.5 * x * (1.0 + t), (x, t)


def _gelu_b(res, g):
    x, t = res
    slope = 0.5 * (1.0 + t) + (0.5 * GELU_C) * x * (1.0 - t * t) * (1.0 + (3.0 * GELU_A) * (x * x))
    return (g * slope,)


_gelu.defvjp(_gelu_f, _gelu_b)


@jax.custom_vjp
def _softplus(x):
    e = jnp.exp(-jnp.abs(x))
    u = 1.0 + e
    log1p = jnp.where(u == 1.0, e, jnp.log(u) * (e / jnp.where(u == 1.0, 1.0, u - 1.0)))
    return jnp.maximum(x, 0.0) + log1p


def _softplus_f(x):
    return _softplus(x), x


def _softplus_b(x, g):
    return (g * _sigmoid(x),)


_softplus.defvjp(_softplus_f, _softplus_b)


def _neg_expm1(y):
    return -jnp.tanh(0.5 * y) * (jnp.exp(y) + 1.0)


def _rms(x, w):
    return x * lax.rsqrt(jnp.mean(x * x, axis=-1, keepdims=True) + EPS) * w


def _row(w, k):
    sel = lax.broadcasted_iota(jnp.int32, w.shape, 0) == k
    return jnp.sum(jnp.where(sel, w, 0.0), axis=0, keepdims=True)


def _col(x, j):
    sel = lax.broadcasted_iota(jnp.int32, x.shape, 1) == j
    return jnp.sum(jnp.where(sel, x, 0.0), axis=1, keepdims=True)


def _conv(x_ext, w, taps):
    acc = None
    for k in range(taps):
        s = taps - 1 - k
        t = (x_ext if s == 0 else _shift_rows(x_ext, s)) * _row(w, k)
        acc = t if acc is None else acc + t
    return acc[HALO:]


@jax.custom_vjp
def _scan(a, b, h0):
    n = a.shape[0]
    row = lax.broadcasted_iota(jnp.int32, a.shape, 0)
    s = 1
    while s < n:
        keep = row >= s
        a_sh = jnp.where(keep, pltpu.roll(a, s, 0), 1.0)
        b_sh = jnp.where(keep, pltpu.roll(b, s, 0), 0.0)
        b = a * b_sh + b
        a = a * a_sh
        s *= 2
    return b + a * h0


def _scan_f(a, b, h0):
    h = _scan(a, b, h0)
    return h, (a, h, h0)


def _scan_b(res, dh):
    a, h, h0 = res
    n = a.shape[0]
    row = lax.broadcasted_iota(jnp.int32, a.shape, 0)
    an = jnp.where(row < n - 1, pltpu.roll(a, n - 1, 0), 0.0)
    lam = dh
    s = 1
    while s < n:
        keep = row < n - s
        a_sh = jnp.where(keep, pltpu.roll(an, n - s, 0), 1.0)
        l_sh = jnp.where(keep, pltpu.roll(lam, n - s, 0), 0.0)
        lam = an * l_sh + lam
        an = an * a_sh
        s *= 2
    h_prev = jnp.where(row >= 1, pltpu.roll(h, 1, 0), h0)
    al = a * lam
    dh0 = jnp.sum(jnp.where(row == 0, al, 0.0), axis=0, keepdims=True)
    return lam * h_prev, lam, dh0


_scan.defvjp(_scan_f, _scan_b)


@jax.custom_vjp
def _unit_lower_inverses(ms):
    n = ms[0].shape[0]
    shape = ms[0].shape
    eye = (lax.broadcasted_iota(jnp.int32, shape, 0) == lax.broadcasted_iota(jnp.int32, shape, 1)).astype(F32)
    p = [-m for m in ms]
    t = [eye + a for a in p]
    steps = 1
    while 2 ** steps < n:
        p = [_mm(a, a) for a in p]
        t = [a + _mm(a, c) for a, c in zip(t, p)]
        steps += 1
    return t


def _unit_lower_inverses_f(ms):
    t = _unit_lower_inverses(ms)
    return t, t


def _unit_lower_inverses_b(t, dt):
    x = [_mm_nt(g, a) for g, a in zip(dt, t)]
    return ([-_mm_tn(a, c) for a, c in zip(t, x)],)


_unit_lower_inverses.defvjp(_unit_lower_inverses_f, _unit_lower_inverses_b)


def _last_row(x):
    sel = lax.broadcasted_iota(jnp.int32, x.shape, 0) == x.shape[0] - 1
    return jnp.sum(jnp.where(sel, x, 0.0), axis=0, keepdims=True)


def fn_norm(xs, st, ps):
    (x,), (w,) = xs, ps
    return [_rms(x, w).astype(BF16)], []


def fn_norm_keep(xs, st, ps):
    (x,), (w,) = xs, ps
    return [_rms(x, w).astype(BF16), x], []


def fn_res(xs, st, ps):
    (x, y), (w_post, w_next) = xs, ps
    x1 = x + _rms(y, w_post)
    return [x1, _rms(x1, w_next).astype(BF16)], []


def fn_res_last(xs, st, ps):
    (x, y), (w_post,) = xs, ps
    return [x + _rms(y, w_post)], []


def fn_gdn(xs, st, ps):
    qkv_ext, z, ba = xs
    (state,) = st
    cw, gp, nw = ps
    ts = z.shape[0]
    qkv = _silu(_conv(qkv_ext, cw, 4))
    beta_all = _sigmoid(ba)
    g_all = -jnp.exp(_row(gp, 0)) * _softplus(ba + _row(gp, 1))
    c_n = GDN_CHUNK
    ri = lax.broadcasted_iota(jnp.int32, (c_n, c_n), 0)
    ci = lax.broadcasted_iota(jnp.int32, (c_n, c_n), 1)
    causal, strict = ri >= ci, ri > ci
    tril = causal.astype(F32)
    lane = lax.broadcasted_iota(jnp.int32, (c_n, 128), 1)
    s_h = [state[GDN_DIM * h:GDN_DIM * (h + 1)] for h in range(GDN_HEADS)]
    n_c = ts // c_n
    pairs = [(c, h) for c in range(n_c) for h in range(GDN_HEADS)]
    every = lambda f, *lists: [f(*a) for a in zip(*lists)]

    def piece(c, h, base):
        return qkv[c * c_n:(c + 1) * c_n, base + GDN_DIM * h:base + GDN_DIM * (h + 1)]

    q = [piece(c, h, 0) for c, h in pairs]
    k = [piece(c, h, D_G) for c, h in pairs]
    v = [piece(c, h, 2 * D_G) for c, h in pairs]
    q = every(lambda t: t * lax.rsqrt(jnp.sum(t * t, axis=-1, keepdims=True) + EPS) * (GDN_DIM ** -0.5), q)
    k = every(lambda t: t * lax.rsqrt(jnp.sum(t * t, axis=-1, keepdims=True) + EPS), k)
    gcum_all = [_dot_exact(tril, g_all[c * c_n:(c + 1) * c_n], 1, 0) for c in range(n_c)]
    b = [_col(beta_all[c * c_n:(c + 1) * c_n], h) for c, h in pairs]
    gc = [_col(gcum_all[c], 4 + h) for c, h in pairs]
    gr = [_dot_exact((lane == 4 + h).astype(F32), gcum_all[c], 1, 1) for c, h in pairs]
    decay = every(lambda a, r: jnp.where(causal, jnp.exp(jnp.where(causal, a - r, 0.0)), 0.0), gc, gr)
    kb = every(lambda a, c: a * c, k, b)
    mk = every(lambda a, c, e: _mm_nt(jnp.concatenate([a, c], axis=0), e), kb, q, k)
    m = every(lambda a, dcy: jnp.where(strict, a[:c_n] * dcy, 0.0), mk, decay)
    attn = every(lambda a, dcy: jnp.where(causal, a[c_n:] * dcy, 0.0), mk, decay)
    t_ = _unit_lower_inverses(m)
    eg = every(jnp.exp, gc)
    wu = every(lambda t, a, e, c, d: _mm(t, jnp.concatenate([a * e, c * d], axis=1)), t_, kb, eg, v, b)
    g_last = every(_last_row, gc)
    k_g = every(lambda a, gl, g: a * jnp.exp(gl - g), k, g_last, gc)
    wq = every(lambda a, c, e: jnp.concatenate([a[:, :GDN_DIM], c * e], axis=0), wu, q, eg)
    u = [a[:, GDN_DIM:] for a in wu]
    gl = every(jnp.exp, g_last)

    o = []
    for c in range(n_c):
        idx = range(c * GDN_HEADS, (c + 1) * GDN_HEADS)
        ws = [_mm(wq[i], s_h[h]) for h, i in enumerate(idx)]
        v_new = [u[i] - ws[h][:c_n] for h, i in enumerate(idx)]
        av = [_mm(attn[i], v_new[h]) for h, i in enumerate(idx)]
        kv = [_mm_tn(k_g[i], v_new[h]) for h, i in enumerate(idx)]
        o += [ws[h][c_n:] + av[h] for h in range(GDN_HEADS)]
        s_h = [s_h[h] * gl[i] + kv[h] for h, i in enumerate(idx)]
    zz = [z[c * c_n:(c + 1) * c_n, GDN_DIM * h:GDN_DIM * (h + 1)] for c, h in pairs]
    y = every(lambda a, g: a * lax.rsqrt(jnp.mean(a * a, axis=-1, keepdims=True) + EPS) * nw * _silu(g), o, zz)
    rows = [jnp.concatenate(y[c * GDN_HEADS:(c + 1) * GDN_HEADS], axis=1) for c in range(n_c)]
    y = rows[0] if n_c == 1 else jnp.concatenate(rows, axis=0)
    return [y.astype(BF16)], [jnp.concatenate(s_h, axis=0)]


def fn_lru(xs, st, ps):
    x_ext, gate = xs
    (h0,) = st
    cw, cb, wa, ba, wx, bx, lam, gw = ps
    xc = _conv(x_ext, cw, 4) + cb
    r = _sigmoid(_mm(xc, wa) + ba)
    i = _sigmoid(_mm(xc, wx) + bx)
    log_a = -LRU_C * r * _softplus(-lam)
    a = jnp.exp(log_a)
    mult = jnp.sqrt(_neg_expm1(2.0 * log_a))
    h = _scan(a, mult * (i * xc), h0)
    y = _rms(h * _gelu(gate), gw)
    return [y.astype(BF16)], [_last_row(h)]


def fn_sgu(xs, st, ps):
    (uv,) = xs
    lnw, lnb, ws, bst, gw = ps
    ts = uv.shape[0]
    uvf = _gelu(uv)
    u, v = uvf[:, :D_G], uvf[:, D_G:]
    vc = v - jnp.mean(v, axis=-1, keepdims=True)
    v = vc * lax.rsqrt(jnp.mean(vc * vc, axis=-1, keepdims=True) + EPS) * lnw + lnb
    t_n = SGU_CHUNK
    tril = lax.broadcasted_iota(jnp.int32, (t_n, t_n), 0) >= lax.broadcasted_iota(jnp.int32, (t_n, t_n), 1)
    wg = [jnp.where(tril, ws[t_n * g:t_n * (g + 1)], 0.0) for g in range(SGU_GROUPS)]
    bg = [_col(bst, g) for g in range(SGU_GROUPS)]
    rows = []
    for c in range(ts // t_n):
        vcg = v[c * t_n:(c + 1) * t_n]
        rows.append(jnp.concatenate(
            [_mm(wg[g], vcg[:, 128 * g:128 * (g + 1)]) + bg[g] for g in range(SGU_GROUPS)], axis=1))
    vv = rows[0] if len(rows) == 1 else jnp.concatenate(rows, axis=0)
    return [_rms(u * vv, gw).astype(BF16)], []


def fn_sconv(xs, st, ps):
    bg, cg_ext, hh_ext = xs
    cw, gw = ps
    return [_rms(bg * _conv(cg_ext * hh_ext, cw, 3), gw).astype(BF16)], []


def fn_ffn(xs, st, ps):
    g_ext, v_ext = xs
    cwg, cwv, cbg, cbv = ps
    g = _conv(g_ext, cwg, 3) + cbg
    v = _conv(v_ext, cwv, 3) + cbv
    return [(_gelu(g) * v).astype(BF16)], []


def _my_pos():
    return lax.axis_index("x"), lax.axis_index("y"), lax.axis_index("c")


def _peer(pos, k):
    x_, y_, c_ = pos
    return (1 - x_ if (k >> 2) & 1 else x_, 1 - y_ if (k >> 1) & 1 else y_, 1 - c_ if k & 1 else c_)


def _dev_index(p):
    return 4 * p[0] + 2 * p[1] + p[2]


class _Side:
    def __init__(self, jobs):
        self.jobs = list(jobs)
        n = len(self.jobs)
        self.operands = [a for _, a in self.jobs]
        self.in_specs = [pl.BlockSpec(memory_space=pl.ANY)] * n
        self.out_shape = [jax.ShapeDtypeStruct(((N_DEV,) + a.shape) if kind == "gather" else a.shape, a.dtype)
                          for kind, a in self.jobs]
        self.out_specs = [pl.BlockSpec(memory_space=pl.ANY)] * n
        self.scratch = [pltpu.SemaphoreType.DMA((7 * n,)), pltpu.SemaphoreType.DMA((7 * n,)),
                        pltpu.SemaphoreType.DMA((n,))] if n else []

    def _copies(self, in_refs, out_refs, sems, landings=True):
        send, recv, local = sems
        pos = _my_pos()
        me = _dev_index(pos)
        mine, outgoing, landing = [], [], []
        for j, (kind, _) in enumerate(self.jobs):
            src, dst = in_refs[j], out_refs[j]
            own = src if kind == "gather" else src.at[me]
            mine.append(pltpu.make_async_copy(own, dst.at[me], local.at[j]))
            for k in range(1, N_DEV):
                p = _peer(pos, k)
                sems_k = dict(send_sem=send.at[7 * j + k - 1], recv_sem=recv.at[7 * j + k - 1], device_id=p,
                              device_id_type=MESH)
                outgoing.append(pltpu.make_async_remote_copy(
                    src_ref=src if kind == "gather" else src.at[_dev_index(p)], dst_ref=dst.at[me], **sems_k))
                if landings:
                    landing.append(pltpu.make_async_remote_copy(src_ref=own, dst_ref=dst.at[_dev_index(p)], **sems_k))
        return mine, outgoing, landing

    def start(self, in_refs, out_refs, sems):
        mine, outgoing, _ = self._copies(in_refs, out_refs, sems, landings=False)
        for cp in mine + outgoing:
            cp.start()

    def wait(self, in_refs, out_refs, sems):
        mine, outgoing, landing = self._copies(in_refs, out_refs, sems)
        for cp in landing:
            cp.wait_recv()
        for cp in outgoing:
            cp.wait_send()
        for cp in mine:
            cp.wait()


def _rin(arr, w=None, col=0, halo=False):
    return dict(arr=arr, w=arr.shape[1] if w is None else w, col=col, halo=halo)


def _par(arr, w=None, col=None, row=None):
    return dict(arr=arr, w=w, col=col, row=row)


def _colidx(col, j):
    return col(j) if callable(col) else col


def _block_call(name, body, n_rows, ts, ncol, reverse, row_ins, blk_ins, params, row_outs, blk_outs, acc_outs,
                carries, side=()):
    side = _Side(side)
    ts = min(ts, n_rows)
    nblk = n_rows // ts
    hb = ts // HALO

    def rr(i):
        return (nblk - 1 - i) if reverse else i

    in_specs, operands = [], []
    for s in row_ins:
        in_specs.append(pl.BlockSpec((ts, s["w"]), lambda j, i, s=s: (rr(i), _colidx(s["col"], j))))
        operands.append(s["arr"])
        if s["halo"]:
            in_specs.append(pl.BlockSpec((HALO, s["w"]),
                                         lambda j, i, s=s: (jnp.maximum(rr(i) * hb - 1, 0), _colidx(s["col"], j))))
            operands.append(s["arr"])
    for a in blk_ins:
        nd = a.ndim - 1
        in_specs.append(pl.BlockSpec((None,) + a.shape[1:], lambda j, i, nd=nd: (rr(i),) + (0,) * nd))
        operands.append(a)
    for p in params:
        a = p["arr"]
        if p["row"] is not None:
            in_specs.append(pl.BlockSpec((p["w"], a.shape[1]), lambda j, i, p=p: (_colidx(p["row"], j), 0)))
        elif p["col"] is None:
            in_specs.append(pl.BlockSpec(a.shape, lambda j, i: (0, 0)))
        else:
            in_specs.append(pl.BlockSpec((a.shape[0], p["w"]), lambda j, i, p=p: (0, _colidx(p["col"], j))))
        operands.append(a)

    out_specs, out_shape = [], []
    for o in row_outs:
        out_specs.append(pl.BlockSpec((ts, o["w"]), lambda j, i, o=o: (rr(i), _colidx(o["col"], j))))
        out_shape.append(jax.ShapeDtypeStruct((n_rows, o["total"]), o["dtype"]))
    for o in blk_outs:
        nd = len(o["shape"])
        out_specs.append(pl.BlockSpec((None,) + tuple(o["shape"]), lambda j, i, nd=nd: (rr(i),) + (0,) * nd))
        out_shape.append(jax.ShapeDtypeStruct((nblk,) + tuple(o["shape"]), o["dtype"]))
    for o in acc_outs:
        if o["col"] is None:
            out_specs.append(pl.BlockSpec(o["shape"], lambda j, i: (0, 0)))
            out_shape.append(jax.ShapeDtypeStruct(o["shape"], F32))
        else:
            out_specs.append(pl.BlockSpec(o["shape"], lambda j, i, o=o: (0, _colidx(o["col"], j))))
            out_shape.append(jax.ShapeDtypeStruct((o["shape"][0], o["total"]), F32))

    n_in = len(operands)
    n_row_out, n_blk_out, n_acc = len(row_outs), len(blk_outs), len(acc_outs)
    n_out = n_row_out + n_blk_out + n_acc
    n_side = len(side.jobs)

    def kern(*refs):
        in_refs = refs[:n_in]
        side_in = refs[n_in:n_in + n_side]
        out_refs = refs[n_in + n_side:n_in + n_side + n_out]
        side_out = refs[n_in + n_side + n_out:n_in + 2 * n_side + n_out]
        scratch = refs[n_in + 2 * n_side + n_out:]
        carry_refs, side_sems = scratch[:len(carries)], scratch[len(carries):]
        acc_refs = out_refs[n_row_out + n_blk_out:]
        i = pl.program_id(1)
        r = rr(i)
        if n_side:
            @pl.when((pl.program_id(0) == 0) & (i == 0))
            def _():
                side.start(side_in, side_out, side_sems)

        @pl.when(i == 0)
        def _():
            for c_ref in carry_refs:
                c_ref[...] = jnp.zeros(c_ref.shape, c_ref.dtype)
            for a_ref in acc_refs:
                a_ref[...] = jnp.zeros(a_ref.shape, a_ref.dtype)

        k = 0
        xs = []
        for s in row_ins:
            x = in_refs[k][...]
            k += 1
            if s["halo"]:
                hal = in_refs[k][...]
                k += 1
                hal = jnp.where(r == 0, jnp.zeros_like(hal), hal)
                x = jnp.concatenate([hal, x], axis=0)
            xs.append(x)
        blks = []
        for _ in blk_ins:
            blks.append(in_refs[k][...])
            k += 1
        ps = []
        for _ in params:
            ps.append(in_refs[k][...])
            k += 1
        row_vals, blk_vals, acc_vals, new_carries = body(xs, blks, ps, [c[...] for c in carry_refs], r)
        for ref, val in zip(out_refs[:n_row_out], row_vals):
            ref[...] = val.astype(ref.dtype)
        for ref, val in zip(out_refs[n_row_out:n_row_out + n_blk_out], blk_vals):
            ref[...] = val.astype(ref.dtype)
        for ref, val in zip(acc_refs, acc_vals):
            ref[...] += val
        for ref, val in zip(carry_refs, new_carries):
            ref[...] = val
        if n_side:
            @pl.when((pl.program_id(0) == ncol - 1) & (i == nblk - 1))
            def _():
                side.wait(side_in, side_out, side_sems)

    res = pl.pallas_call(
        kern,
        name=name,
        grid=(ncol, nblk),
        in_specs=in_specs + side.in_specs,
        out_specs=out_specs + side.out_specs,
        out_shape=out_shape + side.out_shape,
        scratch_shapes=[pltpu.VMEM(shape, F32) for shape in carries] + side.scratch,
        compiler_params=pltpu.CompilerParams(dimension_semantics=("arbitrary", "arbitrary"),
                                             vmem_limit_bytes=VMEM_LIMIT),
    )(*operands, *side.operands)
    return list(res)


def _out(w, dtype, total=None, col=0):
    return dict(w=w, dtype=dtype, total=w if total is None else total, col=col)


def seq_fwd(name, fn, n_rows, ts, row_ins, params, outs, state_shapes=(), ncol=1, side=()):
    def body(xs, blks, ps, carries, r):
        o, new_st = fn(xs, list(carries), ps)
        return o, list(carries), [], new_st

    res = _block_call(name, body, n_rows, ts, ncol, False, row_ins, [], params, outs,
                      [dict(shape=s, dtype=F32) for s in state_shapes], [], list(state_shapes), side)
    n_o, n_s = len(outs), len(state_shapes)
    return (res[:n_o], res[n_o:n_o + n_s]) + ((res[n_o + n_s:],) if side else ())


def seq_bwd(name, fn, n_rows, ts, row_ins, params, cots, saved_states=(), din_dtypes=None, ncol=1, din_specs=None,
            side=(), cot_map=None, aux=()):
    n_x, n_p, n_st = len(row_ins), len(params), len(saved_states)
    halo_idx = [k for k, s in enumerate(row_ins) if s["halo"]]
    state_shapes = [a.shape[1:] for a in saved_states]

    def body(xs_all, blks, ps, carries, r):
        xs, cot_vals = xs_all[:n_x], xs_all[n_x:]
        d_state, d_halo = carries[:n_st], carries[n_st:]
        if cot_map is not None:
            cot_vals = cot_map(cot_vals, ps[n_p:])
        (o, _), vjp = jax.vjp(lambda a, b, c: fn(a, b, c), xs, blks, ps[:n_p])
        cot = [c.astype(v.dtype) for c, v in zip(cot_vals, o)]
        dxs, dst, dps = vjp((cot, list(d_state)))
        row_vals, new_halo = [], []
        for k, dx in enumerate(dxs):
            if k in halo_idx:
                hk = halo_idx.index(k)
                rows = dx.shape[0] - HALO
                tail = dx[rows:] + d_halo[hk]
                row_vals.append(jnp.concatenate([dx[HALO:rows], tail], axis=0))
                new_halo.append(dx[:HALO])
            else:
                row_vals.append(dx)
        return row_vals, [], list(dps), list(dst) + new_halo

    din_dtypes = din_dtypes or [F32] * n_x
    douts = []
    for k, s in enumerate(row_ins):
        total, col = (s["w"], 0) if din_specs is None or din_specs[k] is None else din_specs[k]
        douts.append(_out(s["w"], din_dtypes[k], total, col))
    accs = []
    for p in params:
        a = p["arr"]
        if p["col"] is None:
            accs.append(dict(shape=a.shape, total=None, col=None))
        else:
            accs.append(dict(shape=(a.shape[0], p["w"]), total=a.shape[1], col=p["col"]))
    carries = list(state_shapes) + [(HALO, row_ins[k]["w"]) for k in halo_idx]
    res = _block_call(name, body, n_rows, ts, ncol, True, list(row_ins) + list(cots), list(saved_states),
                      list(params) + list(aux), douts, [], accs, carries, side)
    return (res[:n_x], res[n_x:n_x + n_p]) + ((res[n_x + n_p:],) if side else ())


def matmul(name, a, b, mode, out_dtype, tm, tn, tk, side=(), a2=None, b2=None):
    side = _Side(side)
    n_side = len(side.jobs)
    if mode == "tn":
        (kk, m), n = a.shape, b.shape[1]
    else:
        (m, kk), n = a.shape, (b.shape[0] if mode == "nt" else b.shape[1])
    k1, n1 = kk, n
    if a2 is not None:
        assert mode != "tn" and b2 is None
        kk += a2.shape[1]
    if b2 is not None:
        assert mode == "tn"
        n += b2.shape[1]
    tm, tn, tk = min(tm, m), min(tn, n), min(tk, kk)
    nk, gm, gn = kk // tk, m // tm, n // tn
    assert m % tm == 0 and n % tn == 0 and kk % tk == 0 and k1 % tk == 0 and n1 % tn == 0, (name, a.shape, b.shape)
    nk1, gn1 = k1 // tk, n1 // tn
    if mode == "tn":
        a_specs = [pl.BlockSpec((tk, tm), lambda i, j, k: (k, i))]
        b_specs = [pl.BlockSpec((tk, tn), lambda i, j, k: (k, jnp.minimum(j, gn1 - 1)))]
        if b2 is not None:
            b_specs.append(pl.BlockSpec((tk, tn), lambda i, j, k: (k, jnp.maximum(j - gn1, 0))))
    else:
        a_specs = [pl.BlockSpec((tm, tk), lambda i, j, k: (i, jnp.minimum(k, nk1 - 1)))]
        if a2 is not None:
            a_specs.append(pl.BlockSpec((tm, tk), lambda i, j, k: (i, jnp.maximum(k - nk1, 0))))
        b_specs = [pl.BlockSpec((tn, tk), lambda i, j, k: (j, k)) if mode == "nt"
                   else pl.BlockSpec((tk, tn), lambda i, j, k: (k, j))]
    ca, cb = {"nn": (1, 0), "nt": (1, 1), "tn": (0, 0)}[mode]
    n_a, n_b = len(a_specs), len(b_specs)

    def kern(*refs):
        a_refs, b_refs = refs[:n_a], refs[n_a:n_a + n_b]
        rest = refs[n_a + n_b:]
        side_in = rest[:n_side]
        o_ref = rest[n_side]
        side_out = rest[1 + n_side:1 + 2 * n_side]
        acc_ref = rest[1 + 2 * n_side]
        side_sems = rest[2 + 2 * n_side:]
        i, j, k = pl.program_id(0), pl.program_id(1), pl.program_id(2)
        if n_side:
            @pl.when((i == 0) & (j == 0) & (k == 0))
            def _():
                side.start(side_in, side_out, side_sems)

        def step(a_ref, b_ref):
            part = lax.dot_general(a_ref[...], b_ref[...], (((ca,), (cb,)), ((), ())), preferred_element_type=F32)
            if nk == 1:
                o_ref[...] = part.astype(o_ref.dtype)
            else:
                @pl.when(k == 0)
                def _():
                    acc_ref[...] = part

                @pl.when(k > 0)
                def _():
                    acc_ref[...] += part

                @pl.when(k == nk - 1)
                def _():
                    o_ref[...] = acc_ref[...].astype(o_ref.dtype)

        if n_a == 2:
            pl.when(k < nk1)(lambda: step(a_refs[0], b_refs[0]))
            pl.when(k >= nk1)(lambda: step(a_refs[1], b_refs[0]))
        elif n_b == 2:
            pl.when(j < gn1)(lambda: step(a_refs[0], b_refs[0]))
            pl.when(j >= gn1)(lambda: step(a_refs[0], b_refs[1]))
        else:
            step(a_refs[0], b_refs[0])

        if n_side:
            @pl.when((i == gm - 1) & (j == gn - 1) & (k == nk - 1))
            def _():
                side.wait(side_in, side_out, side_sems)

    semantics = ("arbitrary",) * 3 if n_side else ("parallel", "parallel", "arbitrary")
    operands = [a] + ([a2] if a2 is not None else []) + [b] + ([b2] if b2 is not None else [])
    res = pl.pallas_call(
        kern,
        name=name,
        grid=(gm, gn, nk),
        in_specs=a_specs + b_specs + side.in_specs,
        out_specs=[pl.BlockSpec((tm, tn), lambda i, j, k: (i, j))] + side.out_specs,
        out_shape=[jax.ShapeDtypeStruct((m, n), out_dtype)] + side.out_shape,
        scratch_shapes=[pltpu.VMEM((tm, tn) if nk > 1 else (8, 128), F32)] + side.scratch,
        compiler_params=pltpu.CompilerParams(dimension_semantics=semantics, vmem_limit_bytes=VMEM_LIMIT),
    )(*operands, *side.operands)
    return (res[0], list(res[1:])) if n_side else res[0]


def all_gather(name, x, in_vmem):
    def body(x_ref, out_ref, send_sems, recv_sems, local_sem):
        x_, y_, c_ = _my_pos()
        me, sibling = (x_, y_, c_), (x_, y_, 1 - c_)
        chips = [(1 - x_, y_), (x_, 1 - y_), (1 - x_, 1 - y_)]

        def slot(px, py, pc):
            return out_ref.at[4 * px + 2 * py + pc]

        def copy(k, block, to, src=None):
            return pltpu.make_async_remote_copy(
                src_ref=slot(*block) if src is None else src, dst_ref=slot(*block),
                send_sem=send_sems.at[k], recv_sem=recv_sems.at[k], device_id=to, device_id_type=MESH)

        mine = pltpu.make_async_copy(x_ref, slot(*me), local_sem)
        mine.start()
        first = [copy(0, me, sibling, src=x_ref)]
        first += [copy(1 + j, me, (*chip, c_), src=x_ref) for j, chip in enumerate(chips)]
        for cp in first:
            cp.start()
        passed = [copy(4 + j, (*chip, c_), sibling) for j, chip in enumerate(chips)]
        for j, chip in enumerate(chips):
            copy(1 + j, (*chip, c_), me).wait_recv()
            passed[j].start()
        copy(0, sibling, me).wait_recv()
        for j, chip in enumerate(chips):
            copy(4 + j, (*chip, 1 - c_), me).wait_recv()
        for cp in first + passed:
            cp.wait_send()
        mine.wait()

    space = pltpu.VMEM if in_vmem else pl.ANY
    return pl.pallas_call(
        body,
        name=name,
        out_shape=jax.ShapeDtypeStruct((N_DEV,) + x.shape, x.dtype),
        in_specs=[pl.BlockSpec(memory_space=space)],
        out_specs=pl.BlockSpec(memory_space=space),
        scratch_shapes=[pltpu.SemaphoreType.DMA((7,)), pltpu.SemaphoreType.DMA((7,)), pltpu.SemaphoreType.DMA],
        compiler_params=pltpu.CompilerParams(vmem_limit_bytes=VMEM_LIMIT),
    )(x)


def all_to_all(name, g):
    def body(g_ref, out_ref, send_sems, recv_sems, local_sem):
        x_, y_, c_ = _my_pos()
        me = 4 * x_ + 2 * y_ + c_

        def peer(k):
            fx, fy, fc = (k >> 2) & 1, (k >> 1) & 1, k & 1
            return (1 - x_ if fx else x_, 1 - y_ if fy else y_, 1 - c_ if fc else c_)

        def copy(k):
            px, py, pc = peer(k)
            return pltpu.make_async_remote_copy(
                src_ref=g_ref.at[4 * px + 2 * py + pc], dst_ref=out_ref.at[me],
                send_sem=send_sems.at[k - 1], recv_sem=recv_sems.at[k - 1], device_id=(px, py, pc), device_id_type=MESH)

        def landing(k):
            px, py, pc = peer(k)
            return pltpu.make_async_remote_copy(
                src_ref=g_ref.at[me], dst_ref=out_ref.at[4 * px + 2 * py + pc],
                send_sem=send_sems.at[k - 1], recv_sem=recv_sems.at[k - 1], device_id=(px, py, pc), device_id_type=MESH)

        mine = pltpu.make_async_copy(g_ref.at[me], out_ref.at[me], local_sem)
        mine.start()
        sends = [copy(k) for k in range(1, N_DEV)]
        for cp in sends:
            cp.start()
        for k in range(1, N_DEV):
            landing(k).wait_recv()
        for cp in sends:
            cp.wait_send()
        mine.wait()

    return pl.pallas_call(
        body,
        name=name,
        out_shape=jax.ShapeDtypeStruct(g.shape, g.dtype),
        in_specs=[pl.BlockSpec(memory_space=pl.ANY)],
        out_specs=pl.BlockSpec(memory_space=pl.ANY),
        scratch_shapes=[pltpu.SemaphoreType.DMA((7,)), pltpu.SemaphoreType.DMA((7,)), pltpu.SemaphoreType.DMA],
    )(g)


def sum_blocks(name, g):
    def body(g_ref, o_ref):
        acc = g_ref[0]
        for s in range(1, N_DEV):
            acc = acc + g_ref[s]
        o_ref[...] = acc

    r = g.shape[1]
    tr = r // 4 if r % 32 == 0 else r
    return pl.pallas_call(
        body, name=name, grid=(r // tr,),
        in_specs=[pl.BlockSpec((N_DEV, tr, 128), lambda i: (0, i, 0))],
        out_specs=pl.BlockSpec((tr, 128), lambda i: (i, 0)),
        out_shape=jax.ShapeDtypeStruct((r, 128), F32),
        compiler_params=pltpu.CompilerParams(vmem_limit_bytes=VMEM_LIMIT),
    )(g)


def _adamw_math(w, g, m, v):
    m = ADAM_B1 * m + (1.0 - ADAM_B1) * g
    v = ADAM_B2 * v + (1.0 - ADAM_B2) * (g * g)
    m_hat = m / (1.0 - ADAM_B1 ** ADAM_STEP)
    v_hat = v / (1.0 - ADAM_B2 ** ADAM_STEP)
    delta = -ADAM_LR * (m_hat / (jnp.sqrt(v_hat) + ADAM_EPS) + ADAM_WD * w)
    return delta, m, v


ADAMW_BLOCK_BYTES = 3 << 19


def _row_tile(rows, row_bytes, limit):
    best = 8
    for t in range(8, rows + 1, 8):
        if rows % t == 0 and t * row_bytes <= limit:
            best = t
    return best


def adamw_big(name, w, lands, m, v):
    depth, r, c = w.shape
    outs = None
    for (l, part), land in sorted(lands.items()):
        rows = land.shape[1]
        tr = _row_tile(rows, 4 * (-(-c // 128) * 128), ADAMW_BLOCK_BYTES)
        first = part * rows // tr

        def body(w_ref, l_ref, m_ref, v_ref, *rest):
            g_out, d_out, m_out, v_out = rest[-4:]
            g = l_ref[0].astype(F32)
            for s in range(1, N_DEV):
                g = g + l_ref[s].astype(F32)
            delta, m_new, v_new = _adamw_math(w_ref[...], g, m_ref[...], v_ref[...])
            g_out[...] = g
            d_out[...] = delta
            m_out[...] = m_new
            v_out[...] = v_new

        spec = pl.BlockSpec((None, tr, c), lambda i, l=l, first=first: (l, first + i, 0))
        carried = [] if outs is None else list(outs)
        outs = pl.pallas_call(
            body, name=f"{name}_{l}_{part}", grid=(rows // tr,),
            in_specs=[spec, pl.BlockSpec((N_DEV, tr, c), lambda i: (0, i, 0)), spec, spec]
            + [pl.BlockSpec(memory_space=pl.ANY)] * len(carried),
            out_specs=[spec] * 4,
            out_shape=[jax.ShapeDtypeStruct((depth, r, c), F32)] * 4,
            input_output_aliases={4 + k: k for k in range(len(carried))},
            compiler_params=pltpu.CompilerParams(dimension_semantics=("parallel",), vmem_limit_bytes=VMEM_LIMIT),
        )(w, land, m, v, *carried)
    return outs


def adamw_small(name, ws, gs, ms, vs):
    n = len(ws)

    def body(*refs):
        ins, outs = refs[:4 * n], refs[4 * n:]
        for k in range(n):
            delta, m_new, v_new = _adamw_math(ins[k][...], ins[n + k][...], ins[2 * n + k][...], ins[3 * n + k][...])
            outs[k][...] = delta
            outs[n + k][...] = m_new
            outs[2 * n + k][...] = v_new

    res = pl.pallas_call(
        body, name=name,
        out_shape=[jax.ShapeDtypeStruct(w.shape, F32) for w in ws] * 3,
        compiler_params=pltpu.CompilerParams(vmem_limit_bytes=VMEM_LIMIT),
    )(*ws, *gs, *ms, *vs)
    return res[:n], res[n:2 * n], res[2 * n:]


def cast_bf16(name, w):
    depth, r, c = w.shape
    tr = _row_tile(r, 4 * (-(-c // 128) * 128), ADAMW_BLOCK_BYTES)

    def body(w_ref, o_ref):
        o_ref[...] = w_ref[...].astype(BF16)

    spec = pl.BlockSpec((None, tr, c), lambda l, i: (l, i, 0))
    return pl.pallas_call(body, name=name, grid=(depth, r // tr), in_specs=[spec], out_specs=spec,
                          out_shape=jax.ShapeDtypeStruct((depth, r, c), BF16),
                          compiler_params=pltpu.CompilerParams(dimension_semantics=("parallel", "parallel")))(w)


def _rows_of(shape):
    return -(-math.prod(shape) // 128)


def _pack(arrs):
    pieces = []
    for a in arrs:
        flat = a.reshape(-1).astype(F32)
        pieces.append(jnp.pad(flat, (0, (-flat.shape[0]) % 128)).reshape(-1, 128))
    rows = sum(p.shape[0] for p in pieces)
    if rows % 8:
        pieces.append(jnp.zeros((8 - rows % 8, 128), F32))
    return jnp.concatenate(pieces, axis=0)


def _unpack(packed, shapes, lead=()):
    out, r0 = [], 0
    for s in shapes:
        rows, n = _rows_of(s), math.prod(s)
        piece = packed[..., r0:r0 + rows, :].reshape(lead + (rows * 128,))
        out.append(piece[..., :n].reshape(lead + tuple(s)))
        r0 += rows
    return out


def _block_diag(w):
    h, d, _ = w.shape
    eye = jnp.eye(h, dtype=w.dtype)
    return (eye[:, None, :, None] * w[:, :, None, :]).reshape(h * d, h * d)


def _block_diag_grad(g, h):
    d = g.shape[0] // h
    eye = jnp.eye(h, dtype=g.dtype)
    return jnp.sum(g.reshape(h, d, h, d) * eye[:, None, :, None], axis=2)


def _layer_params(wt, l):
    gp = jnp.pad(jnp.stack([wt["gdn_a_log"][l], wt["gdn_dt_bias"][l]]), ((0, 6), (4, 128 - 4 - GDN_HEADS)))
    d_ffh = wt["ffn_conv_w"].shape[-1] // 2
    return dict(
        pre_mix=wt["pre_mix_norm"][l][None], post_mix=wt["post_mix_norm"][l][None],
        pre_ffn=wt["pre_ffn_norm"][l][None], post_ffn=wt["post_ffn_norm"][l][None],
        gdn_cw=wt["gdn_conv_w"][l], gdn_gp=gp, gdn_nw=wt["gdn_norm_w"][l][None],
        lru_cw=wt["lru_conv_w"][l], lru_cb=wt["lru_conv_b"][l][None],
        lru_wa=_block_diag(wt["lru_wa"][l]), lru_ba=wt["lru_ba"][l].reshape(1, -1),
        lru_wx=_block_diag(wt["lru_wx"][l]), lru_bx=wt["lru_bx"][l].reshape(1, -1),
        lru_lam=wt["lru_lambda"][l][None], gw0=wt["grp_norm_w"][l, 0][None], gw1=wt["grp_norm_w"][l, 1][None],
        gw2=wt["grp_norm_w"][l, 2][None],
        sgu_lnw=wt["sgu_ln_w"][l][None], sgu_lnb=wt["sgu_ln_b"][l][None],
        sgu_ws=wt["sgu_ws"][l].reshape(SGU_GROUPS * SGU_CHUNK, SGU_CHUNK),
        sgu_bt=jnp.pad(wt["sgu_b"][l].T, ((0, 0), (0, 128 - SGU_GROUPS))),
        sc_cw=wt["sconv_w"][l],
        ffn_cw=wt["ffn_conv_w"][l], ffn_cb=wt["ffn_conv_b"][l][None], d_ffh=d_ffh,
    )


TS_ROW = 256
TS_GDN = 256
TS_FFN = 512
TS_FFN_BWD = 256
TC_FFN = 512


def _mixers_fwd(l, p, lp, n, side, side_lru):
    qkv = _rin(p, 3 * D_G, 0, halo=True)
    z = _rin(p, D_G, 3)
    ba = _rin(p, 128, BA_COL // 128)
    gdn_ps = [_par(lp["gdn_cw"]), _par(lp["gdn_gp"]), _par(lp["gdn_nw"])]
    res = seq_fwd(f"gdn_fwd_{l}", fn_gdn, n, TS_GDN, [qkv, z, ba], gdn_ps, [_out(D_G, BF16)],
                  state_shapes=[(GDN_HEADS * GDN_DIM, GDN_DIM)], side=side)
    (y_a,), (gdn_st,), side_res = res if side else res + ([],)
    lru_x = _rin(p, D_G, 4, halo=True)
    lru_gate = _rin(p, D_G, 5)
    lru_ps = [_par(lp[k]) for k in ("lru_cw", "lru_cb", "lru_wa", "lru_ba", "lru_wx", "lru_bx", "lru_lam", "gw0")]
    res = seq_fwd(f"lru_fwd_{l}", fn_lru, n, TS_ROW, [lru_x, lru_gate], lru_ps, [_out(D_G, BF16)],
                  state_shapes=[(1, D_G)], side=side_lru)
    (y_b,), (lru_st,), side_res_lru = res if side_lru else res + ([],)
    uv = _rin(p, 2 * D_G, 3)
    sgu_ps = [_par(lp[k]) for k in ("sgu_lnw", "sgu_lnb", "sgu_ws", "sgu_bt", "gw1")]
    (y_c,), _ = seq_fwd(f"sgu_fwd_{l}", fn_sgu, n, TS_ROW, [uv], sgu_ps, [_out(D_G, BF16)])
    sc = [_rin(p, D_G, 8), _rin(p, D_G, 9, halo=True), _rin(p, D_G, 10, halo=True)]
    sc_ps = [_par(lp["sc_cw"]), _par(lp["gw2"])]
    (y_d,), _ = seq_fwd(f"sconv_fwd_{l}", fn_sconv, n, TS_ROW, sc, sc_ps, [_out(D_G, BF16)])
    ins = dict(gdn=([qkv, z, ba], gdn_ps, [gdn_st]), lru=([lru_x, lru_gate], lru_ps, [lru_st]),
               sgu=([uv], sgu_ps, []), sc=(sc, sc_ps, []))
    return jnp.concatenate([y_a, y_b, y_c, y_d], axis=1), ins, side_res, side_res_lru


def _mixers_bwd(l, dymix, ins, n, side):
    cot = lambda g: [_rin(dymix, D_G, g)]
    xs, ps, st = ins["gdn"]
    res = seq_bwd(f"gdn_bwd_{l}", fn_gdn, n, TS_GDN, xs, ps, cot(0), st, [BF16, BF16, BF16], side=side)
    (dqkv, dz, dba), g_gdn, side_res = res if side else res + ([],)
    xs, ps, st = ins["lru"]
    (dlx, dlg), g_lru = seq_bwd(f"lru_bwd_{l}", fn_lru, n, TS_ROW, xs, ps, cot(1), st, [BF16, BF16])
    xs, ps, st = ins["sgu"]
    (duv,), g_sgu = seq_bwd(f"sgu_bwd_{l}", fn_sgu, n, TS_ROW, xs, ps, cot(2), st, [BF16])
    xs, ps, st = ins["sc"]
    (dsb, dsc, dsh), g_sc = seq_bwd(f"sconv_bwd_{l}", fn_sconv, n, TS_ROW, xs, ps, cot(3), st, [BF16, BF16, BF16])
    dp = jnp.concatenate([dqkv, dz, dlx, dlg, duv, dsb, dsc, dsh, dba], axis=1)
    return dp, dict(gdn=g_gdn, lru=g_lru, sgu=g_sgu, sc=g_sc), side_res


def _ffn_ops(hid, lp):
    d_ffh = lp["d_ffh"]
    off = d_ffh // TC_FFN
    xs = [_rin(hid, TC_FFN, lambda j: j, halo=True), _rin(hid, TC_FFN, lambda j: j + off, halo=True)]
    ps = [_par(lp["ffn_cw"], TC_FFN, lambda j: j), _par(lp["ffn_cw"], TC_FFN, lambda j: j + off),
          _par(lp["ffn_cb"], TC_FFN, lambda j: j), _par(lp["ffn_cb"], TC_FFN, lambda j: j + off)]
    return xs, ps, d_ffh


_FROM_BLOCKS = dict(
    w_in=lambda b: _regroup_w_in(b.transpose(1, 0, 2).reshape(b.shape[1], -1)),
    ffn_up=lambda b: b.transpose(1, 0, 2).reshape(b.shape[1], -1),
    w_out=lambda b: b.reshape(-1, b.shape[2]),
    ffn_down=lambda b: b.reshape(-1, b.shape[2]),
)
_TO_BLOCKS = dict(
    w_in=lambda g: _ungroup_w_in(g).reshape(g.shape[0], N_DEV, -1).transpose(1, 0, 2),
    ffn_up=lambda g: g.reshape(g.shape[0], N_DEV, -1).transpose(1, 0, 2),
    w_out=lambda g: g.reshape(N_DEV, -1, g.shape[1]),
    ffn_down=lambda g: g.reshape(N_DEV, -1, g.shape[1]),
)


class _Traffic:
    PARTS = dict(w_in=1, w_out=1, ffn_up=2, ffn_down=1)

    def __init__(self, whole=None, shards=None):
        self.whole = dict(whole or {})
        self.shards = shards
        self.gathered = {}
        self.pending = {}
        self.landed = {}

    def _rows(self, key):
        name, l, part = key
        rows = self.shards[name].shape[1] // self.PARTS[name]
        return slice(part * rows, (part + 1) * rows)

    def jobs(self, gather=(), exchange=()):
        if self.shards is None:
            return [], []
        keys = [("gather", k) for k in gather if k not in self.gathered and k[:2] not in self.whole]
        keys += [("exchange", k) for k in exchange if k in self.pending]
        jobs = [(kind, self.shards[k[0]][k[1]][self._rows(k)] if kind == "gather" else self.pending[k])
                for kind, k in keys]
        return jobs, keys

    def done(self, keys, results):
        for (kind, k), r in zip(keys, results):
            if kind == "gather":
                self.gathered[k] = r
            else:
                self.landed[k] = r
                del self.pending[k]

    def weight(self, name, l):
        if (name, l) not in self.whole:
            parts = []
            for part in range(self.PARTS[name]):
                k = (name, l, part)
                if k not in self.gathered:
                    self.gathered[k] = all_gather(f"gather_{name}_{l}_{part}", self.shards[name][l][self._rows(k)], False)
                parts.append(self.gathered[k])
            blocks = parts[0] if len(parts) == 1 else jnp.concatenate(parts, axis=1)
            self.whole[(name, l)] = _FROM_BLOCKS[name](blocks)
        return self.whole[(name, l)]

    def grad(self, name, l, g):
        if self.shards is None:
            self.landed[(name, l)] = g
            return
        blocks = _TO_BLOCKS[name](g)
        for part in range(self.PARTS[name]):
            k = (name, l, part)
            self.pending[k] = blocks[:, self._rows(k)]

    def flush(self):
        for (name, l, part), blocks in list(self.pending.items()):
            self.landed[(name, l, part)] = all_to_all(f"exchange_{name}_{l}_{part}", blocks)
            del self.pending[(name, l, part)]


def local_step(x, target, wt, tr):
    n, d = x.shape
    depth = wt["pre_mix_norm"].shape[0]
    lps = [_layer_params(wt, l) for l in range(depth)]
    saved = []
    xin = x

    def mm(name, a, b, mode, dtype, tm, tn, tk, gather=(), exchange=(), **split):
        jobs, keys = tr.jobs(gather, exchange)
        if not jobs:
            return matmul(name, a, b, mode, dtype, tm, tn, tk, **split)
        out, res = matmul(name, a, b, mode, dtype, tm, tn, tk, side=jobs, **split)
        tr.done(keys, res)
        return out

    (h,), _ = seq_fwd("norm_fwd", fn_norm, n, TS_ROW, [_rin(x)], [_par(lps[0]["pre_mix"])], [_out(d, BF16)])
    dx_last = loss = None
    for l in range(depth):
        lp = lps[l]
        p = mm(f"w_in_fwd_{l}", h, tr.weight("w_in", l), "nn", F32, 1024, N_INP // 5, d,
               gather=[("ffn_up", l, 0)])
        jobs, keys = tr.jobs(gather=[("ffn_up", l, 1)])
        jobs_lru, keys_lru = tr.jobs(gather=[("w_out", l, 0)])
        ymix, mix_ins, res, res_lru = _mixers_fwd(l, p, lp, n, jobs, jobs_lru)
        tr.done(keys, res)
        tr.done(keys_lru, res_lru)
        y = mm(f"w_out_fwd_{l}", ymix, tr.weight("w_out", l), "nn", F32, 1024, 1024, d)
        res_ps = [_par(lp["post_mix"]), _par(lp["pre_ffn"])]
        (x1, h2), _ = seq_fwd(f"res_mix_fwd_{l}", fn_res, n, TS_ROW, [_rin(xin), _rin(y)], res_ps,
                              [_out(d, F32), _out(d, BF16)])
        nxt = l + 1 < depth
        hid = mm(f"ffn_up_fwd_{l}", h2, tr.weight("ffn_up", l), "nn", F32, 1024, 1024, d,
                 gather=[("ffn_down", l, 0)] + ([("w_out", l + 1, 0)] if nxt else []))
        f_xs, f_ps, d_ffh = _ffn_ops(hid, lp)
        (act,), _ = seq_fwd(f"ffn_act_fwd_{l}", fn_ffn, n, TS_FFN, f_xs, f_ps,
                            [_out(TC_FFN, BF16, d_ffh, lambda j: j)], ncol=d_ffh // TC_FFN)
        yf = mm(f"ffn_down_fwd_{l}", act, tr.weight("ffn_down", l), "nn", F32, 1024, 1024, d_ffh // 2,
                gather=[("w_in", l + 1, 0)] if nxt else [])
        rec = dict(x=xin, h=h, mix_ins=mix_ins, ymix=ymix, y=y, x1=x1, h2=h2, f_xs=f_xs, f_ps=f_ps, act=act, yf=yf)
        if l + 1 < depth:
            ps = [_par(lp["post_ffn"]), _par(lps[l + 1]["pre_mix"])]
            (x2, h), _ = seq_fwd(f"res_ffn_fwd_{l}", fn_res, n, TS_ROW, [_rin(x1), _rin(yf)], ps,
                                 [_out(d, F32), _out(d, BF16)])
            rec["res_ffn_ps"] = ps
            xin = x2
        else:
            def body(xs, blks, ps, carries, r):
                x1_, yf_, t_ = xs
                e = x1_ + _rms(yf_, ps[0]) - t_
                part = 0.5 * jnp.sum(jnp.mean(e * e, axis=-1, keepdims=True), axis=0, keepdims=True)
                return [e * (1.0 / d)], [], [jnp.broadcast_to(part, (8, 128))], []

            ps = [_par(lp["post_ffn"])]
            dx_last, loss = _block_call("loss_fwd", body, n, TS_ROW, 1, False, [_rin(x1), _rin(yf), _rin(target)],
                                        [], ps, [_out(d, F32)], [], [dict(shape=(8, 128), total=None, col=None)], [])
            rec["res_ffn_ps"] = ps
        saved.append(rec)

    grads = {}
    dx2, dh_next = dx_last, None
    for l in reversed(range(depth)):
        rec, lp = saved[l], lps[l]
        d_ffh = lp["d_ffh"]
        g = {}
        if dh_next is None:
            (dx1, dyf), (g["post_ffn"],) = seq_bwd(f"res_ffn_bwd_{l}", fn_res_last, n, TS_ROW,
                                                   [_rin(rec["x1"]), _rin(rec["yf"])], rec["res_ffn_ps"], [_rin(dx2)],
                                                   din_dtypes=[F32, BF16])
        else:
            (dx1, dyf), (g["post_ffn"], g_next_pre) = seq_bwd(
                f"res_ffn_bwd_{l}", fn_res, n, TS_ROW, [_rin(rec["x1"]), _rin(rec["yf"])], rec["res_ffn_ps"],
                [_rin(dx2), _rin(dh_next)], din_dtypes=[F32, BF16])
            grads[l + 1]["pre_mix"] = g_next_pre
        tr.grad("ffn_down", l, mm(f"ffn_down_dw_{l}", rec["act"], dyf, "tn", BF16, d_ffh // 4, 1024, 2048))

        def dact(cot_blocks, aux_blocks):
            return [_dot(cot_blocks[0], aux_blocks[0], 1, 1).astype(BF16)]

        (dhg, dhv), (g_cwg, g_cwv, g_cbg, g_cbv) = seq_bwd(
            f"ffn_act_bwd_{l}", fn_ffn, n, TS_FFN_BWD, rec["f_xs"], rec["f_ps"], [_rin(dyf)],
            din_dtypes=[BF16, BF16], ncol=d_ffh // TC_FFN, din_specs=[(d_ffh, lambda j: j), (d_ffh, lambda j: j)],
            cot_map=dact, aux=[_par(tr.weight("ffn_down", l), TC_FFN, row=lambda j: j)])
        g["ffn_cw"] = jnp.concatenate([g_cwg[:, :d_ffh], g_cwv[:, d_ffh:]], axis=1)
        g["ffn_cb"] = jnp.concatenate([g_cbg[:, :d_ffh], g_cbv[:, d_ffh:]], axis=1)
        dh2 = mm(f"ffn_up_dx_{l}", dhg, tr.weight("ffn_up", l), "nt", BF16, 1024, 1024, d_ffh // 2,
                 exchange=[("ffn_down", l, 0)], a2=dhv)
        tr.grad("ffn_up", l, mm(f"ffn_up_dw_{l}", rec["h2"], dhg, "tn", BF16, 1024, d_ffh // 4, 2048, b2=dhv))
        (dx, dy), (g["post_mix"], g["pre_ffn"]) = seq_bwd(
            f"res_mix_bwd_{l}", fn_res, n, TS_ROW, [_rin(rec["x"]), _rin(rec["y"])],
            [_par(lp["post_mix"]), _par(lp["pre_ffn"])], [_rin(dx1), _rin(dh2)], din_dtypes=[F32, BF16])
        dymix = mm(f"w_out_dx_{l}", dy, tr.weight("w_out", l), "nt", BF16, 1024, 1024, d)
        tr.grad("w_out", l, mm(f"w_out_dw_{l}", rec["ymix"], dy, "tn", BF16, 1024, 1024, 2048))
        jobs, keys = tr.jobs(exchange=[("ffn_up", l, 0), ("w_out", l, 0)])
        dp, g["mix"], res = _mixers_bwd(l, dymix, rec["mix_ins"], n, jobs)
        tr.done(keys, res)
        tr.grad("w_in", l, mm(f"w_in_dw_{l}", rec["h"], dp, "tn", BF16, 1024, N_INP // 5, 2048,
                              exchange=[("ffn_up", l, 1)]))
        dh = mm(f"w_in_dx_{l}", dp, tr.weight("w_in", l), "nt", BF16, 1024, 1024, N_INP // 3,
                exchange=[("w_in", l, 0)])
        grads[l] = g
        dx2, dh_next = dx, dh
    (grad_x,), (g_pre0,) = seq_bwd("norm_bwd", fn_norm_keep, n, TS_ROW, [_rin(x)], [_par(lps[0]["pre_mix"])],
                                   [_rin(dh_next), _rin(dx2)])
    grads[0]["pre_mix"] = g_pre0
    tr.flush()
    return loss[0, 0], grad_x, _name_grads(grads, depth)


def _name_grads(grads, depth):
    per = {k: [] for k in SMALL}
    for l in range(depth):
        g = grads[l]
        m = g["mix"]
        cw, gp, nw = m["gdn"]
        lcw, lcb, lwa, lba, lwx, lbx, llam, gw0 = m["lru"]
        lnw, lnb, ws, bst, gw1 = m["sgu"]
        scw, gw2 = m["sc"]
        per["pre_mix_norm"].append(g["pre_mix"][0])
        per["gdn_conv_w"].append(cw)
        per["gdn_a_log"].append(gp[0, 4:8])
        per["gdn_dt_bias"].append(gp[1, 4:8])
        per["gdn_norm_w"].append(nw[0])
        per["lru_conv_w"].append(lcw)
        per["lru_conv_b"].append(lcb[0])
        per["lru_wa"].append(_block_diag_grad(lwa, LRU_BLOCKS))
        per["lru_ba"].append(lba.reshape(LRU_BLOCKS, -1))
        per["lru_wx"].append(_block_diag_grad(lwx, LRU_BLOCKS))
        per["lru_bx"].append(lbx.reshape(LRU_BLOCKS, -1))
        per["lru_lambda"].append(llam[0])
        per["sgu_ln_w"].append(lnw[0])
        per["sgu_ln_b"].append(lnb[0])
        per["sgu_ws"].append(ws.reshape(SGU_GROUPS, SGU_CHUNK, SGU_CHUNK))
        per["sgu_b"].append(bst[:, :SGU_GROUPS].T)
        per["sconv_w"].append(scw)
        per["grp_norm_w"].append(jnp.concatenate([gw0, gw1, gw2], axis=0))
        per["post_mix_norm"].append(g["post_mix"][0])
        per["pre_ffn_norm"].append(g["pre_ffn"][0])
        per["ffn_conv_w"].append(g["ffn_cw"])
        per["ffn_conv_b"].append(g["ffn_cb"][0])
        per["post_ffn_norm"].append(g["post_ffn"][0])
    return {k: jnp.stack(v) for k, v in per.items()}


def _regroup_w_in(w):
    pad = jnp.zeros(w.shape[:-1] + (N_INP - N_IN,), w.dtype)
    return jnp.concatenate([w[..., :2048], w[..., 2056:], w[..., 2048:2056], pad], axis=-1)


def _ungroup_w_in(g):
    return jnp.concatenate([g[..., :2048], g[..., BA_COL:BA_COL + 8], g[..., 2048:BA_COL]], axis=-1)


def kernel(x, pre_mix_norm, w_in, gdn_conv_w, gdn_a_log, gdn_dt_bias, gdn_norm_w, lru_conv_w, lru_conv_b, lru_wa, lru_ba, lru_wx, lru_bx, lru_lambda, sgu_ln_w, sgu_ln_b, sgu_ws, sgu_b, sconv_w, grp_norm_w, w_out, post_mix_norm, pre_ffn_norm, ffn_up, ffn_conv_w, ffn_conv_b, ffn_down, post_ffn_norm, loss_target, m_pre_mix_norm, m_w_in, m_gdn_conv_w, m_gdn_a_log, m_gdn_dt_bias, m_gdn_norm_w, m_lru_conv_w, m_lru_conv_b, m_lru_wa, m_lru_ba, m_lru_wx, m_lru_bx, m_lru_lambda, m_sgu_ln_w, m_sgu_ln_b, m_sgu_ws, m_sgu_b, m_sconv_w, m_grp_norm_w, m_w_out, m_post_mix_norm, m_pre_ffn_norm, m_ffn_up, m_ffn_conv_w, m_ffn_conv_b, m_ffn_down, m_post_ffn_norm, v_pre_mix_norm, v_w_in, v_gdn_conv_w, v_gdn_a_log, v_gdn_dt_bias, v_gdn_norm_w, v_lru_conv_w, v_lru_conv_b, v_lru_wa, v_lru_ba, v_lru_wx, v_lru_bx, v_lru_lambda, v_sgu_ln_w, v_sgu_ln_b, v_sgu_ws, v_sgu_b, v_sconv_w, v_grp_norm_w, v_w_out, v_post_mix_norm, v_pre_ffn_norm, v_ffn_up, v_ffn_conv_w, v_ffn_conv_b, v_ffn_down, v_post_ffn_norm):
    args = locals()
    w_loc = {k: args[k] for k in WEIGHTS}
    m_loc = {k: args["m_" + k] for k in WEIGHTS}
    v_loc = {k: args["v_" + k] for k in WEIGHTS}
    depth = pre_mix_norm.shape[0]
    x_, y_, c_ = _my_pos()
    me = 4 * x_ + 2 * y_ + c_

    tr = _Traffic(shards={name: cast_bf16(f"cast_{name}", w_loc[name]) for name in BIG})
    wt = {k: w_loc[k] for k in SMALL}
    shard_shapes = [w_loc[k].shape for k in SHARDED_SMALL]
    gathered = all_gather("gather_small", _pack([w_loc[k] for k in SHARDED_SMALL]), True)
    for k, a in zip(SHARDED_SMALL, _unpack(gathered, shard_shapes, lead=(N_DEV,))):
        a = jnp.moveaxis(a, 0, -2)
        wt[k] = a.reshape(a.shape[:-2] + (-1,))

    loss_part, grad_x, g_full = local_step(x[0], loss_target[0], wt, tr)
    loss = lax.psum(loss_part, ("x", "y", "c"))

    outs_g, outs_d, outs_m, outs_v = {}, {}, {}, {}
    for name in BIG:
        lands = {(l, part): a for (n_, l, part), a in tr.landed.items() if n_ == name}
        outs_g[name], outs_d[name], outs_m[name], outs_v[name] = adamw_big(
            f"adamw_{name}", w_loc[name], lands, m_loc[name], v_loc[name])

    full_shapes = [g_full[k].shape for k in SMALL]
    g_all = all_gather("gather_small_grads", _pack([g_full[k] for k in SMALL]), True)
    g_sum = _unpack(sum_blocks("sum_small_grads", g_all), full_shapes)
    g_small = {}
    for k, g in zip(SMALL, g_sum):
        if k in SHARDED_SMALL:
            w = w_loc[k].shape[-1]
            g = lax.dynamic_slice_in_dim(g, me * w, w, axis=g.ndim - 1)
        g_small[k] = g
    d_s, m_s, v_s = adamw_small("adamw_small", [w_loc[k] for k in SMALL], [g_small[k] for k in SMALL],
                                [m_loc[k] for k in SMALL], [v_loc[k] for k in SMALL])
    for k_i, k in enumerate(SMALL):
        outs_g[k], outs_d[k], outs_m[k], outs_v[k] = g_small[k], d_s[k_i], m_s[k_i], v_s[k_i]

    return (loss, grad_x[None], *[outs_g[k] for k in WEIGHTS], *[outs_d[k] for k in WEIGHTS],
            *[outs_m[k] for k in WEIGHTS], *[outs_v[k] for k in WEIGHTS])
```

```python
import functools
import math

import jax
import jax.numpy as jnp
from jax import lax
from jax.experimental import pallas as pl
from jax.experimental.pallas import tpu as pltpu

F32 = jnp.float32
BF16 = jnp.bfloat16
EPS = 1e-6
HALO = 8
VMEM_LIMIT = 56 * 1024 * 1024
MESH = pl.DeviceIdType.MESH
N_DEV = 8

ADAM_LR, ADAM_B1, ADAM_B2, ADAM_EPS, ADAM_WD, ADAM_STEP = 0.001, 0.9, 0.999, 1e-08, 0.01, 10

GDN_HEADS, GDN_DIM, GDN_CHUNK = 4, 128, 64
SGU_GROUPS, SGU_CHUNK = 4, 128
LRU_BLOCKS, LRU_C = 8, 8.0
D_G = 512
N_IN = 5640
N_INP = 5760
BA_COL = 5632

SHARDED_SMALL = ("gdn_conv_w", "lru_conv_w", "sconv_w", "grp_norm_w", "ffn_conv_w")
BIG = ("w_in", "w_out", "ffn_up", "ffn_down")
WEIGHTS = ("pre_mix_norm", "w_in", "gdn_conv_w", "gdn_a_log", "gdn_dt_bias", "gdn_norm_w", "lru_conv_w",
           "lru_conv_b", "lru_wa", "lru_ba", "lru_wx", "lru_bx", "lru_lambda", "sgu_ln_w", "sgu_ln_b", "sgu_ws",
           "sgu_b", "sconv_w", "grp_norm_w", "w_out", "post_mix_norm", "pre_ffn_norm", "ffn_up", "ffn_conv_w",
           "ffn_conv_b", "ffn_down", "post_ffn_norm")
SMALL = tuple(n for n in WEIGHTS if n not in BIG)


def _dot(a, b, ca, cb):
    return lax.dot_general(a.astype(BF16), b.astype(BF16), (((ca,), (cb,)), ((), ())),
                           preferred_element_type=F32)


@jax.custom_vjp
def _mm(a, b):
    return _dot(a, b, 1, 0)


def _mm_f(a, b):
    return _dot(a, b, 1, 0), (a, b)


def _mm_b(res, g):
    a, b = res
    return _dot(g, b, 1, 1), _dot(a, g, 0, 0)


_mm.defvjp(_mm_f, _mm_b)


@jax.custom_vjp
def _mm_nt(a, b):
    return _dot(a, b, 1, 1)


def _mm_nt_f(a, b):
    return _dot(a, b, 1, 1), (a, b)


def _mm_nt_b(res, g):
    a, b = res
    return _dot(g, b, 1, 0), _dot(g, a, 0, 0)


_mm_nt.defvjp(_mm_nt_f, _mm_nt_b)


@jax.custom_vjp
def _mm_tn(a, b):
    return _dot(a, b, 0, 0)


def _mm_tn_f(a, b):
    return _dot(a, b, 0, 0), (a, b)


def _mm_tn_b(res, g):
    a, b = res
    return _dot(b, g, 1, 1), _dot(a, g, 1, 0)


_mm_tn.defvjp(_mm_tn_f, _mm_tn_b)


def _dot_exact(a, b, ca, cb):
    return lax.dot_general(a, b, (((ca,), (cb,)), ((), ())), precision=lax.Precision.HIGHEST,
                           preferred_element_type=F32)


@functools.partial(jax.custom_vjp, nondiff_argnums=(1,))
def _shift_rows(x, s):
    return pltpu.roll(x, s, 0)


def _shift_rows_f(x, s):
    return pltpu.roll(x, s, 0), None


def _shift_rows_b(s, _, g):
    return (pltpu.roll(g, (g.shape[0] - s) % g.shape[0], 0),)


_shift_rows.defvjp(_shift_rows_f, _shift_rows_b)


def _sigmoid(x):
    return 1.0 / (1.0 + jnp.exp(-x))


def _silu(x):
    return x * _sigmoid(x)


GELU_C, GELU_A = 0.7978845608028654, 0.044715


@jax.custom_vjp
def _gelu(x):
    return 0.5 * x * (1.0 + jnp.tanh(GELU_C * (x + GELU_A * (x * x * x))))


def _gelu_f(x):
    t = jnp.tanh(GELU_C * (x + GELU_A * (x * x * x)))
    return 0.5 * x * (1.0 + t), (x, t)


def _gelu_b(res, g):
    x, t = res
    slope = 0.5 * (1.0 + t) + (0.5 * GELU_C) * x * (1.0 - t * t) * (1.0 + (3.0 * GELU_A) * (x * x))
    return (g * slope,)


_gelu.defvjp(_gelu_f, _gelu_b)


@jax.custom_vjp
def _softplus(x):
    e = jnp.exp(-jnp.abs(x))
    u = 1.0 + e
    log1p = jnp.where(u == 1.0, e, jnp.log(u) * (e / jnp.where(u == 1.0, 1.0, u - 1.0)))
    return jnp.maximum(x, 0.0) + log1p


def _softplus_f(x):
    return _softplus(x), x


def _softplus_b(x, g):
    return (g * _sigmoid(x),)


_softplus.defvjp(_softplus_f, _softplus_b)


def _neg_expm1(y):
    return -jnp.tanh(0.5 * y) * (jnp.exp(y) + 1.0)


def _rms(x, w):
    return x * lax.rsqrt(jnp.mean(x * x, axis=-1, keepdims=True) + EPS) * w


def _row(w, k):
    sel = lax.broadcasted_iota(jnp.int32, w.shape, 0) == k
    return jnp.sum(jnp.where(sel, w, 0.0), axis=0, keepdims=True)


def _col(x, j):
    sel = lax.broadcasted_iota(jnp.int32, x.shape, 1) == j
    return jnp.sum(jnp.where(sel, x, 0.0), axis=1, keepdims=True)


def _conv(x_ext, w, taps):
    acc = None
    for k in range(taps):
        s = taps - 1 - k
        t = (x_ext if s == 0 else _shift_rows(x_ext, s)) * _row(w, k)
        acc = t if acc is None else acc + t
    return acc[HALO:]


@jax.custom_vjp
def _scan(a, b, h0):
    n = a.shape[0]
    row = lax.broadcasted_iota(jnp.int32, a.shape, 0)
    s = 1
    while s < n:
        keep = row >= s
        a_sh = jnp.where(keep, pltpu.roll(a, s, 0), 1.0)
        b_sh = jnp.where(keep, pltpu.roll(b, s, 0), 0.0)
        b = a * b_sh + b
        a = a * a_sh
        s *= 2
    return b + a * h0


def _scan_f(a, b, h0):
    h = _scan(a, b, h0)
    return h, (a, h, h0)


def _scan_b(res, dh):
    a, h, h0 = res
    n = a.shape[0]
    row = lax.broadcasted_iota(jnp.int32, a.shape, 0)
    an = jnp.where(row < n - 1, pltpu.roll(a, n - 1, 0), 0.0)
    lam = dh
    s = 1
    while s < n:
        keep = row < n - s
        a_sh = jnp.where(keep, pltpu.roll(an, n - s, 0), 1.0)
        l_sh = jnp.where(keep, pltpu.roll(lam, n - s, 0), 0.0)
        lam = an * l_sh + lam
        an = an * a_sh
        s *= 2
    h_prev = jnp.where(row >= 1, pltpu.roll(h, 1, 0), h0)
    al = a * lam
    dh0 = jnp.sum(jnp.where(row == 0, al, 0.0), axis=0, keepdims=True)
    return lam * h_prev, lam, dh0


_scan.defvjp(_scan_f, _scan_b)


@jax.custom_vjp
def _unit_lower_inverses(ms):
    n = ms[0].shape[0]
    shape = ms[0].shape
    eye = (lax.broadcasted_iota(jnp.int32, shape, 0) == lax.broadcasted_iota(jnp.int32, shape, 1)).astype(F32)
    p = [-m for m in ms]
    t = [eye + a for a in p]
    steps = 1
    while 2 ** steps < n:
        p = [_mm(a, a) for a in p]
        t = [a + _mm(a, c) for a, c in zip(t, p)]
        steps += 1
    return t


def _unit_lower_inverses_f(ms):
    t = _unit_lower_inverses(ms)
    return t, t


def _unit_lower_inverses_b(t, dt):
    x = [_mm_nt(g, a) for g, a in zip(dt, t)]
    return ([-_mm_tn(a, c) for a, c in zip(t, x)],)


_unit_lower_inverses.defvjp(_unit_lower_inverses_f, _unit_lower_inverses_b)


def _last_row(x):
    sel = lax.broadcasted_iota(jnp.int32, x.shape, 0) == x.shape[0] - 1
    return jnp.sum(jnp.where(sel, x, 0.0), axis=0, keepdims=True)


def fn_norm(xs, st, ps):
    (x,), (w,) = xs, ps
    return [_rms(x, w).astype(BF16)], []


def fn_norm_keep(xs, st, ps):
    (x,), (w,) = xs, ps
    return [_rms(x, w).astype(BF16), x], []


def fn_res(xs, st, ps):
    (x, y), (w_post, w_next) = xs, ps
    x1 = x + _rms(y, w_post)
    return [x1, _rms(x1, w_next).astype(BF16)], []


def fn_res_last(xs, st, ps):
    (x, y), (w_post,) = xs, ps
    return [x + _rms(y, w_post)], []


def fn_gdn(xs, st, ps):
    qkv_ext, z, ba = xs
    (state,) = st
    cw, gp, nw = ps
    ts = z.shape[0]
    qkv = _silu(_conv(qkv_ext, cw, 4))
    beta_all = _sigmoid(ba)
    g_all = -jnp.exp(_row(gp, 0)) * _softplus(ba + _row(gp, 1))
    c_n = GDN_CHUNK
    ri = lax.broadcasted_iota(jnp.int32, (c_n, c_n), 0)
    ci = lax.broadcasted_iota(jnp.int32, (c_n, c_n), 1)
    causal, strict = ri >= ci, ri > ci
    tril = causal.astype(F32)
    lane = lax.broadcasted_iota(jnp.int32, (c_n, 128), 1)
    s_h = [state[GDN_DIM * h:GDN_DIM * (h + 1)] for h in range(GDN_HEADS)]
    n_c = ts // c_n
    pairs = [(c, h) for c in range(n_c) for h in range(GDN_HEADS)]
    every = lambda f, *lists: [f(*a) for a in zip(*lists)]

    def piece(c, h, base):
        return qkv[c * c_n:(c + 1) * c_n, base + GDN_DIM * h:base + GDN_DIM * (h + 1)]

    q = [piece(c, h, 0) for c, h in pairs]
    k = [piece(c, h, D_G) for c, h in pairs]
    v = [piece(c, h, 2 * D_G) for c, h in pairs]
    q = every(lambda t: t * lax.rsqrt(jnp.sum(t * t, axis=-1, keepdims=True) + EPS) * (GDN_DIM ** -0.5), q)
    k = every(lambda t: t * lax.rsqrt(jnp.sum(t * t, axis=-1, keepdims=True) + EPS), k)
    gcum_all = [_dot_exact(tril, g_all[c * c_n:(c + 1) * c_n], 1, 0) for c in range(n_c)]
    b = [_col(beta_all[c * c_n:(c + 1) * c_n], h) for c, h in pairs]
    gc = [_col(gcum_all[c], 4 + h) for c, h in pairs]
    gr = [_dot_exact((lane == 4 + h).astype(F32), gcum_all[c], 1, 1) for c, h in pairs]
    decay = every(lambda a, r: jnp.where(causal, jnp.exp(jnp.where(causal, a - r, 0.0)), 0.0), gc, gr)
    kb = every(lambda a, c: a * c, k, b)
    mk = every(lambda a, c, e: _mm_nt(jnp.concatenate([a, c], axis=0), e), kb, q, k)
    m = every(lambda a, dcy: jnp.where(strict, a[:c_n] * dcy, 0.0), mk, decay)
    attn = every(lambda a, dcy: jnp.where(causal, a[c_n:] * dcy, 0.0), mk, decay)
    t_ = _unit_lower_inverses(m)
    eg = every(jnp.exp, gc)
    wu = every(lambda t, a, e, c, d: _mm(t, jnp.concatenate([a * e, c * d], axis=1)), t_, kb, eg, v, b)
    g_last = every(_last_row, gc)
    k_g = every(lambda a, gl, g: a * jnp.exp(gl - g), k, g_last, gc)
    wq = every(lambda a, c, e: jnp.concatenate([a[:, :GDN_DIM], c * e], axis=0), wu, q, eg)
    u = [a[:, GDN_DIM:] for a in wu]
    gl = every(jnp.exp, g_last)

    o = []
    for c in range(n_c):
        idx = range(c * GDN_HEADS, (c + 1) * GDN_HEADS)
        ws = [_mm(wq[i], s_h[h]) for h, i in enumerate(idx)]
        v_new = [u[i] - ws[h][:c_n] for h, i in enumerate(idx)]
        av = [_mm(attn[i], v_new[h]) for h, i in enumerate(idx)]
        kv = [_mm_tn(k_g[i], v_new[h]) for h, i in enumerate(idx)]
        o += [ws[h][c_n:] + av[h] for h in range(GDN_HEADS)]
        s_h = [s_h[h] * gl[i] + kv[h] for h, i in enumerate(idx)]
    zz = [z[c * c_n:(c + 1) * c_n, GDN_DIM * h:GDN_DIM * (h + 1)] for c, h in pairs]
    y = every(lambda a, g: a * lax.rsqrt(jnp.mean(a * a, axis=-1, keepdims=True) + EPS) * nw * _silu(g), o, zz)
    rows = [jnp.concatenate(y[c * GDN_HEADS:(c + 1) * GDN_HEADS], axis=1) for c in range(n_c)]
    y = rows[0] if n_c == 1 else jnp.concatenate(rows, axis=0)
    return [y.astype(BF16)], [jnp.concatenate(s_h, axis=0)]


def fn_lru(xs, st, ps):
    x_ext, gate = xs
    (h0,) = st
    cw, cb, wa, ba, wx, bx, lam, gw = ps
    xc = _conv(x_ext, cw, 4) + cb
    r = _sigmoid(_mm(xc, wa) + ba)
    i = _sigmoid(_mm(xc, wx) + bx)
    log_a = -LRU_C * r * _softplus(-lam)
    a = jnp.exp(log_a)
    mult = jnp.sqrt(_neg_expm1(2.0 * log_a))
    h = _scan(a, mult * (i * xc), h0)
    y = _rms(h * _gelu(gate), gw)
    return [y.astype(BF16)], [_last_row(h)]


def fn_sgu(xs, st, ps):
    (uv,) = xs
    lnw, lnb, ws, bst, gw = ps
    ts = uv.shape[0]
    uvf = _gelu(uv)
    u, v = uvf[:, :D_G], uvf[:, D_G:]
    vc = v - jnp.mean(v, axis=-1, keepdims=True)
    v = vc * lax.rsqrt(jnp.mean(vc * vc, axis=-1, keepdims=True) + EPS) * lnw + lnb
    t_n = SGU_CHUNK
    tril = lax.broadcasted_iota(jnp.int32, (t_n, t_n), 0) >= lax.broadcasted_iota(jnp.int32, (t_n, t_n), 1)
    wg = [jnp.where(tril, ws[t_n * g:t_n * (g + 1)], 0.0) for g in range(SGU_GROUPS)]
    bg = [_col(bst, g) for g in range(SGU_GROUPS)]
    rows = []
    for c in range(ts // t_n):
        vcg = v[c * t_n:(c + 1) * t_n]
        rows.append(jnp.concatenate(
            [_mm(wg[g], vcg[:, 128 * g:128 * (g + 1)]) + bg[g] for g in range(SGU_GROUPS)], axis=1))
    vv = rows[0] if len(rows) == 1 else jnp.concatenate(rows, axis=0)
    return [_rms(u * vv, gw).astype(BF16)], []


def fn_sconv(xs, st, ps):
    bg, cg_ext, hh_ext = xs
    cw, gw = ps
    return [_rms(bg * _conv(cg_ext * hh_ext, cw, 3), gw).astype(BF16)], []


def fn_ffn(xs, st, ps):
    g_ext, v_ext = xs
    cwg, cwv, cbg, cbv = ps
    g = _conv(g_ext, cwg, 3) + cbg
    v = _conv(v_ext, cwv, 3) + cbv
    return [(_gelu(g) * v).astype(BF16)], []


def _my_pos():
    return lax.axis_index("x"), lax.axis_index("y"), lax.axis_index("c")


def _peer(pos, k):
    x_, y_, c_ = pos
    return (1 - x_ if (k >> 2) & 1 else x_, 1 - y_ if (k >> 1) & 1 else y_, 1 - c_ if k & 1 else c_)


def _dev_index(p):
    return 4 * p[0] + 2 * p[1] + p[2]


class _Side:
    def __init__(self, jobs):
        self.jobs = list(jobs)
        n = len(self.jobs)
        self.operands = [a for _, a in self.jobs]
        self.in_specs = [pl.BlockSpec(memory_space=pl.ANY)] * n
        self.out_shape = [jax.ShapeDtypeStruct(((N_DEV,) + a.shape) if kind == "gather" else a.shape, a.dtype)
                          for kind, a in self.jobs]
        self.out_specs = [pl.BlockSpec(memory_space=pl.ANY)] * n
        self.scratch = [pltpu.SemaphoreType.DMA((7 * n,)), pltpu.SemaphoreType.DMA((7 * n,)),
                        pltpu.SemaphoreType.DMA((n,))] if n else []

    def _copies(self, in_refs, out_refs, sems, landings=True):
        send, recv, local = sems
        pos = _my_pos()
        me = _dev_index(pos)
        mine, outgoing, landing = [], [], []
        for j, (kind, _) in enumerate(self.jobs):
            src, dst = in_refs[j], out_refs[j]
            own = src if kind == "gather" else src.at[me]
            mine.append(pltpu.make_async_copy(own, dst.at[me], local.at[j]))
            for k in range(1, N_DEV):
                p = _peer(pos, k)
                sems_k = dict(send_sem=send.at[7 * j + k - 1], recv_sem=recv.at[7 * j + k - 1], device_id=p,
                              device_id_type=MESH)
                outgoing.append(pltpu.make_async_remote_copy(
                    src_ref=src if kind == "gather" else src.at[_dev_index(p)], dst_ref=dst.at[me], **sems_k))
                if landings:
                    landing.append(pltpu.make_async_remote_copy(src_ref=own, dst_ref=dst.at[_dev_index(p)], **sems_k))
        return mine, outgoing, landing

    def start(self, in_refs, out_refs, sems):
        mine, outgoing, _ = self._copies(in_refs, out_refs, sems, landings=False)
        for cp in mine + outgoing:
            cp.start()

    def wait(self, in_refs, out_refs, sems):
        mine, outgoing, landing = self._copies(in_refs, out_refs, sems)
        for cp in landing:
            cp.wait_recv()
        for cp in outgoing:
            cp.wait_send()
        for cp in mine:
            cp.wait()


def _rin(arr, w=None, col=0, halo=False):
    return dict(arr=arr, w=arr.shape[1] if w is None else w, col=col, halo=halo)


def _par(arr, w=None, col=None, row=None):
    return dict(arr=arr, w=w, col=col, row=row)


def _colidx(col, j):
    return col(j) if callable(col) else col


def _block_call(name, body, n_rows, ts, ncol, reverse, row_ins, blk_ins, params, row_outs, blk_outs, acc_outs,
                carries, side=()):
    side = _Side(side)
    ts = min(ts, n_rows)
    nblk = n_rows // ts
    hb = ts // HALO

    def rr(i):
        return (nblk - 1 - i) if reverse else i

    in_specs, operands = [], []
    for s in row_ins:
        in_specs.append(pl.BlockSpec((ts, s["w"]), lambda j, i, s=s: (rr(i), _colidx(s["col"], j))))
        operands.append(s["arr"])
        if s["halo"]:
            in_specs.append(pl.BlockSpec((HALO, s["w"]),
                                         lambda j, i, s=s: (jnp.maximum(rr(i) * hb - 1, 0), _colidx(s["col"], j))))
            operands.append(s["arr"])
    for a in blk_ins:
        nd = a.ndim - 1
        in_specs.append(pl.BlockSpec((None,) + a.shape[1:], lambda j, i, nd=nd: (rr(i),) + (0,) * nd))
        operands.append(a)
    for p in params:
        a = p["arr"]
        if p["row"] is not None:
            in_specs.append(pl.BlockSpec((p["w"], a.shape[1]), lambda j, i, p=p: (_colidx(p["row"], j), 0)))
        elif p["col"] is None:
            in_specs.append(pl.BlockSpec(a.shape, lambda j, i: (0, 0)))
        else:
            in_specs.append(pl.BlockSpec((a.shape[0], p["w"]), lambda j, i, p=p: (0, _colidx(p["col"], j))))
        operands.append(a)

    out_specs, out_shape = [], []
    for o in row_outs:
        out_specs.append(pl.BlockSpec((ts, o["w"]), lambda j, i, o=o: (rr(i), _colidx(o["col"], j))))
        out_shape.append(jax.ShapeDtypeStruct((n_rows, o["total"]), o["dtype"]))
    for o in blk_outs:
        nd = len(o["shape"])
        out_specs.append(pl.BlockSpec((None,) + tuple(o["shape"]), lambda j, i, nd=nd: (rr(i),) + (0,) * nd))
        out_shape.append(jax.ShapeDtypeStruct((nblk,) + tuple(o["shape"]), o["dtype"]))
    for o in acc_outs:
        if o["col"] is None:
            out_specs.append(pl.BlockSpec(o["shape"], lambda j, i: (0, 0)))
            out_shape.append(jax.ShapeDtypeStruct(o["shape"], F32))
        else:
            out_specs.append(pl.BlockSpec(o["shape"], lambda j, i, o=o: (0, _colidx(o["col"], j))))
            out_shape.append(jax.ShapeDtypeStruct((o["shape"][0], o["total"]), F32))

    n_in = len(operands)
    n_row_out, n_blk_out, n_acc = len(row_outs), len(blk_outs), len(acc_outs)
    n_out = n_row_out + n_blk_out + n_acc
    n_side = len(side.jobs)

    def kern(*refs):
        in_refs = refs[:n_in]
        side_in = refs[n_in:n_in + n_side]
        out_refs = refs[n_in + n_side:n_in + n_side + n_out]
        side_out = refs[n_in + n_side + n_out:n_in + 2 * n_side + n_out]
        scratch = refs[n_in + 2 * n_side + n_out:]
        carry_refs, side_sems = scratch[:len(carries)], scratch[len(carries):]
        acc_refs = out_refs[n_row_out + n_blk_out:]
        i = pl.program_id(1)
        r = rr(i)
        if n_side:
            @pl.when((pl.program_id(0) == 0) & (i == 0))
            def _():
                side.start(side_in, side_out, side_sems)

        @pl.when(i == 0)
        def _():
            for c_ref in carry_refs:
                c_ref[...] = jnp.zeros(c_ref.shape, c_ref.dtype)
            for a_ref in acc_refs:
                a_ref[...] = jnp.zeros(a_ref.shape, a_ref.dtype)

        k = 0
        xs = []
        for s in row_ins:
            x = in_refs[k][...]
            k += 1
            if s["halo"]:
                hal = in_refs[k][...]
                k += 1
                hal = jnp.where(r == 0, jnp.zeros_like(hal), hal)
                x = jnp.concatenate([hal, x], axis=0)
            xs.append(x)
        blks = []
        for _ in blk_ins:
            blks.append(in_refs[k][...])
            k += 1
        ps = []
        for _ in params:
            ps.append(in_refs[k][...])
            k += 1
        row_vals, blk_vals, acc_vals, new_carries = body(xs, blks, ps, [c[...] for c in carry_refs], r)
        for ref, val in zip(out_refs[:n_row_out], row_vals):
            ref[...] = val.astype(ref.dtype)
        for ref, val in zip(out_refs[n_row_out:n_row_out + n_blk_out], blk_vals):
            ref[...] = val.astype(ref.dtype)
        for ref, val in zip(acc_refs, acc_vals):
            ref[...] += val
        for ref, val in zip(carry_refs, new_carries):
            ref[...] = val
        if n_side:
            @pl.when((pl.program_id(0) == ncol - 1) & (i == nblk - 1))
            def _():
                side.wait(side_in, side_out, side_sems)

    res = pl.pallas_call(
        kern,
        name=name,
        grid=(ncol, nblk),
        in_specs=in_specs + side.in_specs,
        out_specs=out_specs + side.out_specs,
        out_shape=out_shape + side.out_shape,
        scratch_shapes=[pltpu.VMEM(shape, F32) for shape in carries] + side.scratch,
        compiler_params=pltpu.CompilerParams(dimension_semantics=("arbitrary", "arbitrary"),
                                             vmem_limit_bytes=VMEM_LIMIT),
    )(*operands, *side.operands)
    return list(res)


def _out(w, dtype, total=None, col=0):
    return dict(w=w, dtype=dtype, total=w if total is None else total, col=col)


def seq_fwd(name, fn, n_rows, ts, row_ins, params, outs, state_shapes=(), ncol=1, side=()):
    def body(xs, blks, ps, carries, r):
        o, new_st = fn(xs, list(carries), ps)
        return o, list(carries), [], new_st

    res = _block_call(name, body, n_rows, ts, ncol, False, row_ins, [], params, outs,
                      [dict(shape=s, dtype=F32) for s in state_shapes], [], list(state_shapes), side)
    n_o, n_s = len(outs), len(state_shapes)
    return (res[:n_o], res[n_o:n_o + n_s]) + ((res[n_o + n_s:],) if side else ())


def seq_bwd(name, fn, n_rows, ts, row_ins, params, cots, saved_states=(), din_dtypes=None, ncol=1, din_specs=None,
            side=(), cot_map=None, aux=()):
    n_x, n_p, n_st = len(row_ins), len(params), len(saved_states)
    halo_idx = [k for k, s in enumerate(row_ins) if s["halo"]]
    state_shapes = [a.shape[1:] for a in saved_states]

    def body(xs_all, blks, ps, carries, r):
        xs, cot_vals = xs_all[:n_x], xs_all[n_x:]
        d_state, d_halo = carries[:n_st], carries[n_st:]
        if cot_map is not None:
            cot_vals = cot_map(cot_vals, ps[n_p:])
        (o, _), vjp = jax.vjp(lambda a, b, c: fn(a, b, c), xs, blks, ps[:n_p])
        cot = [c.astype(v.dtype) for c, v in zip(cot_vals, o)]
        dxs, dst, dps = vjp((cot, list(d_state)))
        row_vals, new_halo = [], []
        for k, dx in enumerate(dxs):
            if k in halo_idx:
                hk = halo_idx.index(k)
                rows = dx.shape[0] - HALO
                tail = dx[rows:] + d_halo[hk]
                row_vals.append(jnp.concatenate([dx[HALO:rows], tail], axis=0))
                new_halo.append(dx[:HALO])
            else:
                row_vals.append(dx)
        return row_vals, [], list(dps), list(dst) + new_halo

    din_dtypes = din_dtypes or [F32] * n_x
    douts = []
    for k, s in enumerate(row_ins):
        total, col = (s["w"], 0) if din_specs is None or din_specs[k] is None else din_specs[k]
        douts.append(_out(s["w"], din_dtypes[k], total, col))
    accs = []
    for p in params:
        a = p["arr"]
        if p["col"] is None:
            accs.append(dict(shape=a.shape, total=None, col=None))
        else:
            accs.append(dict(shape=(a.shape[0], p["w"]), total=a.shape[1], col=p["col"]))
    carries = list(state_shapes) + [(HALO, row_ins[k]["w"]) for k in halo_idx]
    res = _block_call(name, body, n_rows, ts, ncol, True, list(row_ins) + list(cots), list(saved_states),
                      list(params) + list(aux), douts, [], accs, carries, side)
    return (res[:n_x], res[n_x:n_x + n_p]) + ((res[n_x + n_p:],) if side else ())


def matmul(name, a, b, mode, out_dtype, tm, tn, tk, side=(), a2=None, b2=None, col_blocks=False):
    side = _Side(side)
    n_side = len(side.jobs)
    if mode == "tn":
        (kk, m), n = a.shape, b.shape[1]
    else:
        (m, kk), n = a.shape, (b.shape[0] if mode == "nt" else b.shape[1])
    k1, n1 = kk, n
    if a2 is not None:
        assert mode != "tn" and b2 is None
        kk += a2.shape[1]
    if b2 is not None:
        assert mode == "tn"
        n += b2.shape[1]
    tm, tn, tk = min(tm, m), min(tn, n), min(tk, kk)
    nk, gm, gn = kk // tk, m // tm, n // tn
    assert m % tm == 0 and n % tn == 0 and kk % tk == 0 and k1 % tk == 0 and n1 % tn == 0, (name, a.shape, b.shape)
    nk1, gn1 = k1 // tk, n1 // tn
    if mode == "tn":
        a_specs = [pl.BlockSpec((tk, tm), lambda i, j, k: (k, i))]
        b_specs = [pl.BlockSpec((tk, tn), lambda i, j, k: (k, jnp.minimum(j, gn1 - 1)))]
        if b2 is not None:
            b_specs.append(pl.BlockSpec((tk, tn), lambda i, j, k: (k, jnp.maximum(j - gn1, 0))))
    else:
        a_specs = [pl.BlockSpec((tm, tk), lambda i, j, k: (i, jnp.minimum(k, nk1 - 1)))]
        if a2 is not None:
            a_specs.append(pl.BlockSpec((tm, tk), lambda i, j, k: (i, jnp.maximum(k - nk1, 0))))
        b_specs = [pl.BlockSpec((tn, tk), lambda i, j, k: (j, k)) if mode == "nt"
                   else pl.BlockSpec((tk, tn), lambda i, j, k: (k, j))]
    ca, cb = {"nn": (1, 0), "nt": (1, 1), "tn": (0, 0)}[mode]
    n_a, n_b = len(a_specs), len(b_specs)

    def kern(*refs):
        a_refs, b_refs = refs[:n_a], refs[n_a:n_a + n_b]
        rest = refs[n_a + n_b:]
        side_in = rest[:n_side]
        o_ref = rest[n_side]
        side_out = rest[1 + n_side:1 + 2 * n_side]
        acc_ref = rest[1 + 2 * n_side]
        side_sems = rest[2 + 2 * n_side:]
        i, j, k = pl.program_id(0), pl.program_id(1), pl.program_id(2)
        if n_side:
            @pl.when((i == 0) & (j == 0) & (k == 0))
            def _():
                side.start(side_in, side_out, side_sems)

        def step(a_ref, b_ref):
            part = lax.dot_general(a_ref[...], b_ref[...], (((ca,), (cb,)), ((), ())), preferred_element_type=F32)
            if nk == 1:
                o_ref[...] = part.astype(o_ref.dtype)
            else:
                @pl.when(k == 0)
                def _():
                    acc_ref[...] = part

                @pl.when(k > 0)
                def _():
                    acc_ref[...] += part

                @pl.when(k == nk - 1)
                def _():
                    o_ref[...] = acc_ref[...].astype(o_ref.dtype)

        if n_a == 2:
            pl.when(k < nk1)(lambda: step(a_refs[0], b_refs[0]))
            pl.when(k >= nk1)(lambda: step(a_refs[1], b_refs[0]))
        elif n_b == 2:
            pl.when(j < gn1)(lambda: step(a_refs[0], b_refs[0]))
            pl.when(j >= gn1)(lambda: step(a_refs[0], b_refs[1]))
        else:
            step(a_refs[0], b_refs[0])

        if n_side:
            @pl.when((i == gm - 1) & (j == gn - 1) & (k == nk - 1))
            def _():
                side.wait(side_in, side_out, side_sems)

    semantics = ("arbitrary",) * 3 if n_side else ("parallel", "parallel", "arbitrary")
    operands = [a] + ([a2] if a2 is not None else []) + [b] + ([b2] if b2 is not None else [])
    res = pl.pallas_call(
        kern,
        name=name,
        grid=(gm, gn, nk),
        in_specs=a_specs + b_specs + side.in_specs,
        out_specs=[pl.BlockSpec((None, tm, tn), lambda i, j, k: (j, i, 0)) if col_blocks
                   else pl.BlockSpec((tm, tn), lambda i, j, k: (i, j))] + side.out_specs,
        out_shape=[jax.ShapeDtypeStruct((gn, m, tn) if col_blocks else (m, n), out_dtype)] + side.out_shape,
        scratch_shapes=[pltpu.VMEM((tm, tn) if nk > 1 else (8, 128), F32)] + side.scratch,
        compiler_params=pltpu.CompilerParams(dimension_semantics=semantics, vmem_limit_bytes=VMEM_LIMIT),
    )(*operands, *side.operands)
    return (res[0], list(res[1:])) if n_side else res[0]


def all_gather(name, x, in_vmem):
    def body(x_ref, out_ref, send_sems, recv_sems, local_sem):
        x_, y_, c_ = _my_pos()
        me, sibling = (x_, y_, c_), (x_, y_, 1 - c_)
        chips = [(1 - x_, y_), (x_, 1 - y_), (1 - x_, 1 - y_)]

        def slot(px, py, pc):
            return out_ref.at[4 * px + 2 * py + pc]

        def copy(k, block, to, src=None):
            return pltpu.make_async_remote_copy(
                src_ref=slot(*block) if src is None else src, dst_ref=slot(*block),
                send_sem=send_sems.at[k], recv_sem=recv_sems.at[k], device_id=to, device_id_type=MESH)

        mine = pltpu.make_async_copy(x_ref, slot(*me), local_sem)
        mine.start()
        first = [copy(0, me, sibling, src=x_ref)]
        first += [copy(1 + j, me, (*chip, c_), src=x_ref) for j, chip in enumerate(chips)]
        for cp in first:
            cp.start()
        passed = [copy(4 + j, (*chip, c_), sibling) for j, chip in enumerate(chips)]
        for j, chip in enumerate(chips):
            copy(1 + j, (*chip, c_), me).wait_recv()
            passed[j].start()
        copy(0, sibling, me).wait_recv()
        for j, chip in enumerate(chips):
            copy(4 + j, (*chip, 1 - c_), me).wait_recv()
        for cp in first + passed:
            cp.wait_send()
        mine.wait()

    space = pltpu.VMEM if in_vmem else pl.ANY
    return pl.pallas_call(
        body,
        name=name,
        out_shape=jax.ShapeDtypeStruct((N_DEV,) + x.shape, x.dtype),
        in_specs=[pl.BlockSpec(memory_space=space)],
        out_specs=pl.BlockSpec(memory_space=space),
        scratch_shapes=[pltpu.SemaphoreType.DMA((7,)), pltpu.SemaphoreType.DMA((7,)), pltpu.SemaphoreType.DMA],
        compiler_params=pltpu.CompilerParams(vmem_limit_bytes=VMEM_LIMIT),
    )(x)


def all_to_all(name, g):
    def body(g_ref, out_ref, send_sems, recv_sems, local_sem):
        x_, y_, c_ = _my_pos()
        me = 4 * x_ + 2 * y_ + c_

        def peer(k):
            fx, fy, fc = (k >> 2) & 1, (k >> 1) & 1, k & 1
            return (1 - x_ if fx else x_, 1 - y_ if fy else y_, 1 - c_ if fc else c_)

        def copy(k):
            px, py, pc = peer(k)
            return pltpu.make_async_remote_copy(
                src_ref=g_ref.at[4 * px + 2 * py + pc], dst_ref=out_ref.at[me],
                send_sem=send_sems.at[k - 1], recv_sem=recv_sems.at[k - 1], device_id=(px, py, pc), device_id_type=MESH)

        def landing(k):
            px, py, pc = peer(k)
            return pltpu.make_async_remote_copy(
                src_ref=g_ref.at[me], dst_ref=out_ref.at[4 * px + 2 * py + pc],
                send_sem=send_sems.at[k - 1], recv_sem=recv_sems.at[k - 1], device_id=(px, py, pc), device_id_type=MESH)

        mine = pltpu.make_async_copy(g_ref.at[me], out_ref.at[me], local_sem)
        mine.start()
        sends = [copy(k) for k in range(1, N_DEV)]
        for cp in sends:
            cp.start()
        for k in range(1, N_DEV):
            landing(k).wait_recv()
        for cp in sends:
            cp.wait_send()
        mine.wait()

    return pl.pallas_call(
        body,
        name=name,
        out_shape=jax.ShapeDtypeStruct(g.shape, g.dtype),
        in_specs=[pl.BlockSpec(memory_space=pl.ANY)],
        out_specs=pl.BlockSpec(memory_space=pl.ANY),
        scratch_shapes=[pltpu.SemaphoreType.DMA((7,)), pltpu.SemaphoreType.DMA((7,)), pltpu.SemaphoreType.DMA],
    )(g)


def sum_blocks(name, g):
    def body(g_ref, o_ref):
        acc = g_ref[0]
        for s in range(1, N_DEV):
            acc = acc + g_ref[s]
        o_ref[...] = acc

    r = g.shape[1]
    tr = r // 4 if r % 32 == 0 else r
    return pl.pallas_call(
        body, name=name, grid=(r // tr,),
        in_specs=[pl.BlockSpec((N_DEV, tr, 128), lambda i: (0, i, 0))],
        out_specs=pl.BlockSpec((tr, 128), lambda i: (i, 0)),
        out_shape=jax.ShapeDtypeStruct((r, 128), F32),
        compiler_params=pltpu.CompilerParams(vmem_limit_bytes=VMEM_LIMIT),
    )(g)


def _adamw_math(w, g, m, v):
    m = ADAM_B1 * m + (1.0 - ADAM_B1) * g
    v = ADAM_B2 * v + (1.0 - ADAM_B2) * (g * g)
    m_hat = m / (1.0 - ADAM_B1 ** ADAM_STEP)
    v_hat = v / (1.0 - ADAM_B2 ** ADAM_STEP)
    delta = -ADAM_LR * (m_hat / (jnp.sqrt(v_hat) + ADAM_EPS) + ADAM_WD * w)
    return delta, m, v


ADAMW_BLOCK_BYTES = 3 << 19


def _row_tile(rows, row_bytes, limit):
    best = 8
    for t in range(8, rows + 1, 8):
        if rows % t == 0 and t * row_bytes <= limit:
            best = t
    return best


def adamw_big(name, w, lands, m, v):
    depth, r, c = w.shape
    outs = None
    for (l, part), land in sorted(lands.items()):
        rows = land.shape[1]
        tr = _row_tile(rows, 4 * (-(-c // 128) * 128), ADAMW_BLOCK_BYTES)
        first = part * rows // tr

        def body(w_ref, l_ref, m_ref, v_ref, *rest):
            g_out, d_out, m_out, v_out = rest[-4:]
            g = l_ref[0].astype(F32)
            for s in range(1, N_DEV):
                g = g + l_ref[s].astype(F32)
            delta, m_new, v_new = _adamw_math(w_ref[...], g, m_ref[...], v_ref[...])
            g_out[...] = g
            d_out[...] = delta
            m_out[...] = m_new
            v_out[...] = v_new

        spec = pl.BlockSpec((None, tr, c), lambda i, l=l, first=first: (l, first + i, 0))
        carried = [] if outs is None else list(outs)
        outs = pl.pallas_call(
            body, name=f"{name}_{l}_{part}", grid=(rows // tr,),
            in_specs=[spec, pl.BlockSpec((N_DEV, tr, c), lambda i: (0, i, 0)), spec, spec]
            + [pl.BlockSpec(memory_space=pl.ANY)] * len(carried),
            out_specs=[spec] * 4,
            out_shape=[jax.ShapeDtypeStruct((depth, r, c), F32)] * 4,
            input_output_aliases={4 + k: k for k in range(len(carried))},
            compiler_params=pltpu.CompilerParams(dimension_semantics=("parallel",), vmem_limit_bytes=VMEM_LIMIT),
        )(w, land, m, v, *carried)
    return outs


def adamw_small(name, ws, gs, ms, vs):
    n = len(ws)

    def body(*refs):
        ins, outs = refs[:4 * n], refs[4 * n:]
        for k in range(n):
            delta, m_new, v_new = _adamw_math(ins[k][...], ins[n + k][...], ins[2 * n + k][...], ins[3 * n + k][...])
            outs[k][...] = delta
            outs[n + k][...] = m_new
            outs[2 * n + k][...] = v_new

    res = pl.pallas_call(
        body, name=name,
        out_shape=[jax.ShapeDtypeStruct(w.shape, F32) for w in ws] * 3,
        compiler_params=pltpu.CompilerParams(vmem_limit_bytes=VMEM_LIMIT),
    )(*ws, *gs, *ms, *vs)
    return res[:n], res[n:2 * n], res[2 * n:]


def cast_bf16(name, w):
    depth, r, c = w.shape
    tr = _row_tile(r, 4 * (-(-c // 128) * 128), ADAMW_BLOCK_BYTES)

    def body(w_ref, o_ref):
        o_ref[...] = w_ref[...].astype(BF16)

    spec = pl.BlockSpec((None, tr, c), lambda l, i: (l, i, 0))
    return pl.pallas_call(body, name=name, grid=(depth, r // tr), in_specs=[spec], out_specs=spec,
                          out_shape=jax.ShapeDtypeStruct((depth, r, c), BF16),
                          compiler_params=pltpu.CompilerParams(dimension_semantics=("parallel", "parallel")))(w)


def _rows_of(shape):
    return -(-math.prod(shape) // 128)


def _pack(arrs):
    pieces = []
    for a in arrs:
        flat = a.reshape(-1).astype(F32)
        pieces.append(jnp.pad(flat, (0, (-flat.shape[0]) % 128)).reshape(-1, 128))
    rows = sum(p.shape[0] for p in pieces)
    if rows % 8:
        pieces.append(jnp.zeros((8 - rows % 8, 128), F32))
    return jnp.concatenate(pieces, axis=0)


def _unpack(packed, shapes, lead=()):
    out, r0 = [], 0
    for s in shapes:
        rows, n = _rows_of(s), math.prod(s)
        piece = packed[..., r0:r0 + rows, :].reshape(lead + (rows * 128,))
        out.append(piece[..., :n].reshape(lead + tuple(s)))
        r0 += rows
    return out


def _block_diag(w):
    h, d, _ = w.shape
    eye = jnp.eye(h, dtype=w.dtype)
    return (eye[:, None, :, None] * w[:, :, None, :]).reshape(h * d, h * d)


def _block_diag_grad(g, h):
    d = g.shape[0] // h
    eye = jnp.eye(h, dtype=g.dtype)
    return jnp.sum(g.reshape(h, d, h, d) * eye[:, None, :, None], axis=2)


def _layer_params(wt, l):
    gp = jnp.pad(jnp.stack([wt["gdn_a_log"][l], wt["gdn_dt_bias"][l]]), ((0, 6), (4, 128 - 4 - GDN_HEADS)))
    d_ffh = wt["ffn_conv_w"].shape[-1] // 2
    return dict(
        pre_mix=wt["pre_mix_norm"][l][None], post_mix=wt["post_mix_norm"][l][None],
        pre_ffn=wt["pre_ffn_norm"][l][None], post_ffn=wt["post_ffn_norm"][l][None],
        gdn_cw=wt["gdn_conv_w"][l], gdn_gp=gp, gdn_nw=wt["gdn_norm_w"][l][None],
        lru_cw=wt["lru_conv_w"][l], lru_cb=wt["lru_conv_b"][l][None],
        lru_wa=_block_diag(wt["lru_wa"][l]), lru_ba=wt["lru_ba"][l].reshape(1, -1),
        lru_wx=_block_diag(wt["lru_wx"][l]), lru_bx=wt["lru_bx"][l].reshape(1, -1),
        lru_lam=wt["lru_lambda"][l][None], gw0=wt["grp_norm_w"][l, 0][None], gw1=wt["grp_norm_w"][l, 1][None],
        gw2=wt["grp_norm_w"][l, 2][None],
        sgu_lnw=wt["sgu_ln_w"][l][None], sgu_lnb=wt["sgu_ln_b"][l][None],
        sgu_ws=wt["sgu_ws"][l].reshape(SGU_GROUPS * SGU_CHUNK, SGU_CHUNK),
        sgu_bt=jnp.pad(wt["sgu_b"][l].T, ((0, 0), (0, 128 - SGU_GROUPS))),
        sc_cw=wt["sconv_w"][l],
        ffn_cw=wt["ffn_conv_w"][l], ffn_cb=wt["ffn_conv_b"][l][None], d_ffh=d_ffh,
    )


TS_ROW = 256
TS_GDN = 256
TS_FFN = 512
TS_FFN_BWD = 256
TC_FFN = 512


def _mixers_fwd(l, p, lp, n, side, side_lru):
    qkv = _rin(p, 3 * D_G, 0, halo=True)
    z = _rin(p, D_G, 3)
    ba = _rin(p, 128, BA_COL // 128)
    gdn_ps = [_par(lp["gdn_cw"]), _par(lp["gdn_gp"]), _par(lp["gdn_nw"])]
    res = seq_fwd(f"gdn_fwd_{l}", fn_gdn, n, TS_GDN, [qkv, z, ba], gdn_ps, [_out(D_G, BF16)],
                  state_shapes=[(GDN_HEADS * GDN_DIM, GDN_DIM)], side=side)
    (y_a,), (gdn_st,), side_res = res if side else res + ([],)
    lru_x = _rin(p, D_G, 4, halo=True)
    lru_gate = _rin(p, D_G, 5)
    lru_ps = [_par(lp[k]) for k in ("lru_cw", "lru_cb", "lru_wa", "lru_ba", "lru_wx", "lru_bx", "lru_lam", "gw0")]
    res = seq_fwd(f"lru_fwd_{l}", fn_lru, n, TS_ROW, [lru_x, lru_gate], lru_ps, [_out(D_G, BF16)],
                  state_shapes=[(1, D_G)], side=side_lru)
    (y_b,), (lru_st,), side_res_lru = res if side_lru else res + ([],)
    uv = _rin(p, 2 * D_G, 3)
    sgu_ps = [_par(lp[k]) for k in ("sgu_lnw", "sgu_lnb", "sgu_ws", "sgu_bt", "gw1")]
    (y_c,), _ = seq_fwd(f"sgu_fwd_{l}", fn_sgu, n, TS_ROW, [uv], sgu_ps, [_out(D_G, BF16)])
    sc = [_rin(p, D_G, 8), _rin(p, D_G, 9, halo=True), _rin(p, D_G, 10, halo=True)]
    sc_ps = [_par(lp["sc_cw"]), _par(lp["gw2"])]
    (y_d,), _ = seq_fwd(f"sconv_fwd_{l}", fn_sconv, n, TS_ROW, sc, sc_ps, [_out(D_G, BF16)])
    ins = dict(gdn=([qkv, z, ba], gdn_ps, [gdn_st]), lru=([lru_x, lru_gate], lru_ps, [lru_st]),
               sgu=([uv], sgu_ps, []), sc=(sc, sc_ps, []))
    return jnp.concatenate([y_a, y_b, y_c, y_d], axis=1), ins, side_res, side_res_lru


def _mixers_bwd(l, dymix, ins, n, side):
    cot = lambda g: [_rin(dymix, D_G, g)]
    xs, ps, st = ins["gdn"]
    res = seq_bwd(f"gdn_bwd_{l}", fn_gdn, n, TS_GDN, xs, ps, cot(0), st, [BF16, BF16, BF16], side=side)
    (dqkv, dz, dba), g_gdn, side_res = res if side else res + ([],)
    xs, ps, st = ins["lru"]
    (dlx, dlg), g_lru = seq_bwd(f"lru_bwd_{l}", fn_lru, n, TS_ROW, xs, ps, cot(1), st, [BF16, BF16])
    xs, ps, st = ins["sgu"]
    (duv,), g_sgu = seq_bwd(f"sgu_bwd_{l}", fn_sgu, n, TS_ROW, xs, ps, cot(2), st, [BF16])
    xs, ps, st = ins["sc"]
    (dsb, dsc, dsh), g_sc = seq_bwd(f"sconv_bwd_{l}", fn_sconv, n, TS_ROW, xs, ps, cot(3), st, [BF16, BF16, BF16])
    dp = jnp.concatenate([dqkv, dz, dlx, dlg, duv, dsb, dsc, dsh, dba], axis=1)
    return dp, dict(gdn=g_gdn, lru=g_lru, sgu=g_sgu, sc=g_sc), side_res


def _ffn_ops(hid, lp):
    d_ffh = lp["d_ffh"]
    off = d_ffh // TC_FFN
    xs = [_rin(hid, TC_FFN, lambda j: j, halo=True), _rin(hid, TC_FFN, lambda j: j + off, halo=True)]
    ps = [_par(lp["ffn_cw"], TC_FFN, lambda j: j), _par(lp["ffn_cw"], TC_FFN, lambda j: j + off),
          _par(lp["ffn_cb"], TC_FFN, lambda j: j), _par(lp["ffn_cb"], TC_FFN, lambda j: j + off)]
    return xs, ps, d_ffh


_FROM_BLOCKS = dict(
    w_in=lambda b: _regroup_w_in(b.transpose(1, 0, 2).reshape(b.shape[1], -1)),
    ffn_up=lambda b: b.transpose(1, 0, 2).reshape(b.shape[1], -1),
    w_out=lambda b: b.reshape(-1, b.shape[2]),
    ffn_down=lambda b: b.reshape(-1, b.shape[2]),
)
_TO_BLOCKS = dict(
    w_in=lambda g: _ungroup_w_in(g).reshape(g.shape[0], N_DEV, -1).transpose(1, 0, 2),
    ffn_up=lambda g: g,
    w_out=lambda g: g.reshape(N_DEV, -1, g.shape[1]),
    ffn_down=lambda g: g.reshape(N_DEV, -1, g.shape[1]),
)


class _Traffic:
    PARTS = dict(w_in=1, w_out=1, ffn_up=2, ffn_down=1)

    def __init__(self, whole=None, shards=None):
        self.whole = dict(whole or {})
        self.shards = shards
        self.gathered = {}
        self.pending = {}
        self.landed = {}

    def _rows(self, key):
        name, l, part = key
        rows = self.shards[name].shape[1] // self.PARTS[name]
        return slice(part * rows, (part + 1) * rows)

    def jobs(self, gather=(), exchange=()):
        if self.shards is None:
            return [], []
        keys = [("gather", k) for k in gather if k not in self.gathered and k[:2] not in self.whole]
        keys += [("exchange", k) for k in exchange if k in self.pending]
        jobs = [(kind, self.shards[k[0]][k[1]][self._rows(k)] if kind == "gather" else self.pending[k])
                for kind, k in keys]
        return jobs, keys

    def done(self, keys, results):
        for (kind, k), r in zip(keys, results):
            if kind == "gather":
                self.gathered[k] = r
            else:
                self.landed[k] = r
                del self.pending[k]

    def weight(self, name, l):
        if (name, l) not in self.whole:
            parts = []
            for part in range(self.PARTS[name]):
                k = (name, l, part)
                if k not in self.gathered:
                    self.gathered[k] = all_gather(f"gather_{name}_{l}_{part}", self.shards[name][l][self._rows(k)], False)
                parts.append(self.gathered[k])
            blocks = parts[0] if len(parts) == 1 else jnp.concatenate(parts, axis=1)
            self.whole[(name, l)] = _FROM_BLOCKS[name](blocks)
        return self.whole[(name, l)]

    def grad(self, name, l, g):
        if self.shards is None:
            self.landed[(name, l)] = g
            return
        blocks = _TO_BLOCKS[name](g)
        for part in range(self.PARTS[name]):
            k = (name, l, part)
            self.pending[k] = blocks[:, self._rows(k)]

    def flush(self):
        for (name, l, part), blocks in list(self.pending.items()):
            self.landed[(name, l, part)] = all_to_all(f"exchange_{name}_{l}_{part}", blocks)
            del self.pending[(name, l, part)]


def local_step(x, target, wt, tr):
    n, d = x.shape
    depth = wt["pre_mix_norm"].shape[0]
    lps = [_layer_params(wt, l) for l in range(depth)]
    saved = []
    xin = x

    def mm(name, a, b, mode, dtype, tm, tn, tk, gather=(), exchange=(), **split):
        jobs, keys = tr.jobs(gather, exchange)
        if not jobs:
            return matmul(name, a, b, mode, dtype, tm, tn, tk, **split)
        out, res = matmul(name, a, b, mode, dtype, tm, tn, tk, side=jobs, **split)
        tr.done(keys, res)
        return out

    (h,), _ = seq_fwd("norm_fwd", fn_norm, n, TS_ROW, [_rin(x)], [_par(lps[0]["pre_mix"])], [_out(d, BF16)])
    dx_last = loss = None
    for l in range(depth):
        lp = lps[l]
        p = mm(f"w_in_fwd_{l}", h, tr.weight("w_in", l), "nn", F32, 1024, N_INP // 5, d,
               gather=[("ffn_up", l, 0)])
        jobs, keys = tr.jobs(gather=[("ffn_up", l, 1)])
        jobs_lru, keys_lru = tr.jobs(gather=[("w_out", l, 0)])
        ymix, mix_ins, res, res_lru = _mixers_fwd(l, p, lp, n, jobs, jobs_lru)
        tr.done(keys, res)
        tr.done(keys_lru, res_lru)
        y = mm(f"w_out_fwd_{l}", ymix, tr.weight("w_out", l), "nn", F32, 1024, 1024, d)
        res_ps = [_par(lp["post_mix"]), _par(lp["pre_ffn"])]
        (x1, h2), _ = seq_fwd(f"res_mix_fwd_{l}", fn_res, n, TS_ROW, [_rin(xin), _rin(y)], res_ps,
                              [_out(d, F32), _out(d, BF16)])
        nxt = l + 1 < depth
        hid = mm(f"ffn_up_fwd_{l}", h2, tr.weight("ffn_up", l), "nn", F32, 1024, 1024, d,
                 gather=[("ffn_down", l, 0)] + ([("w_out", l + 1, 0)] if nxt else []))
        f_xs, f_ps, d_ffh = _ffn_ops(hid, lp)
        (act,), _ = seq_fwd(f"ffn_act_fwd_{l}", fn_ffn, n, TS_FFN, f_xs, f_ps,
                            [_out(TC_FFN, BF16, d_ffh, lambda j: j)], ncol=d_ffh // TC_FFN)
        yf = mm(f"ffn_down_fwd_{l}", act, tr.weight("ffn_down", l), "nn", F32, 1024, 1024, d_ffh // 2,
                gather=[("w_in", l + 1, 0)] if nxt else [])
        rec = dict(x=xin, h=h, mix_ins=mix_ins, ymix=ymix, y=y, x1=x1, h2=h2, f_xs=f_xs, f_ps=f_ps, act=act, yf=yf)
        if l + 1 < depth:
            ps = [_par(lp["post_ffn"]), _par(lps[l + 1]["pre_mix"])]
            (x2, h), _ = seq_fwd(f"res_ffn_fwd_{l}", fn_res, n, TS_ROW, [_rin(x1), _rin(yf)], ps,
                                 [_out(d, F32), _out(d, BF16)])
            rec["res_ffn_ps"] = ps
            xin = x2
        else:
            def body(xs, blks, ps, carries, r):
                x1_, yf_, t_ = xs
                e = x1_ + _rms(yf_, ps[0]) - t_
                part = 0.5 * jnp.sum(jnp.mean(e * e, axis=-1, keepdims=True), axis=0, keepdims=True)
                return [e * (1.0 / d)], [], [jnp.broadcast_to(part, (8, 128))], []

            ps = [_par(lp["post_ffn"])]
            dx_last, loss = _block_call("loss_fwd", body, n, TS_ROW, 1, False, [_rin(x1), _rin(yf), _rin(target)],
                                        [], ps, [_out(d, F32)], [], [dict(shape=(8, 128), total=None, col=None)], [])
            rec["res_ffn_ps"] = ps
        saved.append(rec)

    grads = {}
    dx2, dh_next = dx_last, None
    for l in reversed(range(depth)):
        rec, lp = saved[l], lps[l]
        d_ffh = lp["d_ffh"]
        g = {}
        if dh_next is None:
            (dx1, dyf), (g["post_ffn"],) = seq_bwd(f"res_ffn_bwd_{l}", fn_res_last, n, TS_ROW,
                                                   [_rin(rec["x1"]), _rin(rec["yf"])], rec["res_ffn_ps"], [_rin(dx2)],
                                                   din_dtypes=[F32, BF16])
        else:
            (dx1, dyf), (g["post_ffn"], g_next_pre) = seq_bwd(
                f"res_ffn_bwd_{l}", fn_res, n, TS_ROW, [_rin(rec["x1"]), _rin(rec["yf"])], rec["res_ffn_ps"],
                [_rin(dx2), _rin(dh_next)], din_dtypes=[F32, BF16])
            grads[l + 1]["pre_mix"] = g_next_pre
        tr.grad("ffn_down", l, mm(f"ffn_down_dw_{l}", rec["act"], dyf, "tn", BF16, d_ffh // 4, 1024, 2048))

        def dact(cot_blocks, aux_blocks):
            return [_dot(cot_blocks[0], aux_blocks[0], 1, 1).astype(BF16)]

        (dhg, dhv), (g_cwg, g_cwv, g_cbg, g_cbv) = seq_bwd(
            f"ffn_act_bwd_{l}", fn_ffn, n, TS_FFN_BWD, rec["f_xs"], rec["f_ps"], [_rin(dyf)],
            din_dtypes=[BF16, BF16], ncol=d_ffh // TC_FFN, din_specs=[(d_ffh, lambda j: j), (d_ffh, lambda j: j)],
            cot_map=dact, aux=[_par(tr.weight("ffn_down", l), TC_FFN, row=lambda j: j)])
        g["ffn_cw"] = jnp.concatenate([g_cwg[:, :d_ffh], g_cwv[:, d_ffh:]], axis=1)
        g["ffn_cb"] = jnp.concatenate([g_cbg[:, :d_ffh], g_cbv[:, d_ffh:]], axis=1)
        dh2 = mm(f"ffn_up_dx_{l}", dhg, tr.weight("ffn_up", l), "nt", BF16, 1024, 1024, d_ffh // 2,
                 exchange=[("ffn_down", l, 0)], a2=dhv)
        tr.grad("ffn_up", l, mm(f"ffn_up_dw_{l}", rec["h2"], dhg, "tn", BF16, 1024, 2 * d_ffh // N_DEV, 2048, b2=dhv,
                                col_blocks=True))
        (dx, dy), (g["post_mix"], g["pre_ffn"]) = seq_bwd(
            f"res_mix_bwd_{l}", fn_res, n, TS_ROW, [_rin(rec["x"]), _rin(rec["y"])],
            [_par(lp["post_mix"]), _par(lp["pre_ffn"])], [_rin(dx1), _rin(dh2)], din_dtypes=[F32, BF16])
        dymix = mm(f"w_out_dx_{l}", dy, tr.weight("w_out", l), "nt", BF16, 1024, 1024, d)
        tr.grad("w_out", l, mm(f"w_out_dw_{l}", rec["ymix"], dy, "tn", BF16, 1024, 1024, 2048))
        jobs, keys = tr.jobs(exchange=[("ffn_up", l, 0), ("w_out", l, 0)])
        dp, g["mix"], res = _mixers_bwd(l, dymix, rec["mix_ins"], n, jobs)
        tr.done(keys, res)
        tr.grad("w_in", l, mm(f"w_in_dw_{l}", rec["h"], dp, "tn", BF16, 1024, N_INP // 5, 2048,
                              exchange=[("ffn_up", l, 1)]))
        dh = mm(f"w_in_dx_{l}", dp, tr.weight("w_in", l), "nt", BF16, 1024, 1024, N_INP // 3,
                exchange=[("w_in", l, 0)])
        grads[l] = g
        dx2, dh_next = dx, dh
    (grad_x,), (g_pre0,) = seq_bwd("norm_bwd", fn_norm_keep, n, TS_ROW, [_rin(x)], [_par(lps[0]["pre_mix"])],
                                   [_rin(dh_next), _rin(dx2)])
    grads[0]["pre_mix"] = g_pre0
    tr.flush()
    return loss[0, 0], grad_x, _name_grads(grads, depth)


def _name_grads(grads, depth):
    per = {k: [] for k in SMALL}
    for l in range(depth):
        g = grads[l]
        m = g["mix"]
        cw, gp, nw = m["gdn"]
        lcw, lcb, lwa, lba, lwx, lbx, llam, gw0 = m["lru"]
        lnw, lnb, ws, bst, gw1 = m["sgu"]
        scw, gw2 = m["sc"]
        per["pre_mix_norm"].append(g["pre_mix"][0])
        per["gdn_conv_w"].append(cw)
        per["gdn_a_log"].append(gp[0, 4:8])
        per["gdn_dt_bias"].append(gp[1, 4:8])
        per["gdn_norm_w"].append(nw[0])
        per["lru_conv_w"].append(lcw)
        per["lru_conv_b"].append(lcb[0])
        per["lru_wa"].append(_block_diag_grad(lwa, LRU_BLOCKS))
        per["lru_ba"].append(lba.reshape(LRU_BLOCKS, -1))
        per["lru_wx"].append(_block_diag_grad(lwx, LRU_BLOCKS))
        per["lru_bx"].append(lbx.reshape(LRU_BLOCKS, -1))
        per["lru_lambda"].append(llam[0])
        per["sgu_ln_w"].append(lnw[0])
        per["sgu_ln_b"].append(lnb[0])
        per["sgu_ws"].append(ws.reshape(SGU_GROUPS, SGU_CHUNK, SGU_CHUNK))
        per["sgu_b"].append(bst[:, :SGU_GROUPS].T)
        per["sconv_w"].append(scw)
        per["grp_norm_w"].append(jnp.concatenate([gw0, gw1, gw2], axis=0))
        per["post_mix_norm"].append(g["post_mix"][0])
        per["pre_ffn_norm"].append(g["pre_ffn"][0])
        per["ffn_conv_w"].append(g["ffn_cw"])
        per["ffn_conv_b"].append(g["ffn_cb"][0])
        per["post_ffn_norm"].append(g["post_ffn"][0])
    return {k: jnp.stack(v) for k, v in per.items()}


def _regroup_w_in(w):
    pad = jnp.zeros(w.shape[:-1] + (N_INP - N_IN,), w.dtype)
    return jnp.concatenate([w[..., :2048], w[..., 2056:], w[..., 2048:2056], pad], axis=-1)


def _ungroup_w_in(g):
    return jnp.concatenate([g[..., :2048], g[..., BA_COL:BA_COL + 8], g[..., 2048:BA_COL]], axis=-1)


def kernel(x, pre_mix_norm, w_in, gdn_conv_w, gdn_a_log, gdn_dt_bias, gdn_norm_w, lru_conv_w, lru_conv_b, lru_wa, lru_ba, lru_wx, lru_bx, lru_lambda, sgu_ln_w, sgu_ln_b, sgu_ws, sgu_b, sconv_w, grp_norm_w, w_out, post_mix_norm, pre_ffn_norm, ffn_up, ffn_conv_w, ffn_conv_b, ffn_down, post_ffn_norm, loss_target, m_pre_mix_norm, m_w_in, m_gdn_conv_w, m_gdn_a_log, m_gdn_dt_bias, m_gdn_norm_w, m_lru_conv_w, m_lru_conv_b, m_lru_wa, m_lru_ba, m_lru_wx, m_lru_bx, m_lru_lambda, m_sgu_ln_w, m_sgu_ln_b, m_sgu_ws, m_sgu_b, m_sconv_w, m_grp_norm_w, m_w_out, m_post_mix_norm, m_pre_ffn_norm, m_ffn_up, m_ffn_conv_w, m_ffn_conv_b, m_ffn_down, m_post_ffn_norm, v_pre_mix_norm, v_w_in, v_gdn_conv_w, v_gdn_a_log, v_gdn_dt_bias, v_gdn_norm_w, v_lru_conv_w, v_lru_conv_b, v_lru_wa, v_lru_ba, v_lru_wx, v_lru_bx, v_lru_lambda, v_sgu_ln_w, v_sgu_ln_b, v_sgu_ws, v_sgu_b, v_sconv_w, v_grp_norm_w, v_w_out, v_post_mix_norm, v_pre_ffn_norm, v_ffn_up, v_ffn_conv_w, v_ffn_conv_b, v_ffn_down, v_post_ffn_norm):
    args = locals()
    w_loc = {k: args[k] for k in WEIGHTS}
    m_loc = {k: args["m_" + k] for k in WEIGHTS}
    v_loc = {k: args["v_" + k] for k in WEIGHTS}
    depth = pre_mix_norm.shape[0]
    x_, y_, c_ = _my_pos()
    me = 4 * x_ + 2 * y_ + c_

    tr = _Traffic(shards={name: cast_bf16(f"cast_{name}", w_loc[name]) for name in BIG})
    wt = {k: w_loc[k] for k in SMALL}
    shard_shapes = [w_loc[k].shape for k in SHARDED_SMALL]
    gathered = all_gather("gather_small", _pack([w_loc[k] for k in SHARDED_SMALL]), True)
    for k, a in zip(SHARDED_SMALL, _unpack(gathered, shard_shapes, lead=(N_DEV,))):
        a = jnp.moveaxis(a, 0, -2)
        wt[k] = a.reshape(a.shape[:-2] + (-1,))

    loss_part, grad_x, g_full = local_step(x[0], loss_target[0], wt, tr)
    loss = lax.psum(loss_part, ("x", "y", "c"))

    outs_g, outs_d, outs_m, outs_v = {}, {}, {}, {}
    for name in BIG:
        lands = {(l, part): a for (n_, l, part), a in tr.landed.items() if n_ == name}
        outs_g[name], outs_d[name], outs_m[name], outs_v[name] = adamw_big(
            f"adamw_{name}", w_loc[name], lands, m_loc[name], v_loc[name])

    full_shapes = [g_full[k].shape for k in SMALL]
    g_all = all_gather("gather_small_grads", _pack([g_full[k] for k in SMALL]), True)
    g_sum = _unpack(sum_blocks("sum_small_grads", g_all), full_shapes)
    g_small = {}
    for k, g in zip(SMALL, g_sum):
        if k in SHARDED_SMALL:
            w = w_loc[k].shape[-1]
            g = lax.dynamic_slice_in_dim(g, me * w, w, axis=g.ndim - 1)
        g_small[k] = g
    d_s, m_s, v_s = adamw_small("adamw_small", [w_loc[k] for k in SMALL], [g_small[k] for k in SMALL],
                                [m_loc[k] for k in SMALL], [v_loc[k] for k in SMALL])
    for k_i, k in enumerate(SMALL):
        outs_g[k], outs_d[k], outs_m[k], outs_v[k] = g_small[k], d_s[k_i], m_s[k_i], v_s[k_i]

    return (loss, grad_x[None], *[outs_g[k] for k in WEIGHTS], *[outs_d[k] for k in WEIGHTS],
            *[outs_m[k] for k in WEIGHTS], *[outs_v[k] for k in WEIGHTS])
```

```python
import functools
import math

import jax
import jax.numpy as jnp
from jax import lax
from jax.experimental import pallas as pl
from jax.experimental.pallas import tpu as pltpu

F32 = jnp.float32
BF16 = jnp.bfloat16
EPS = 1e-6
HALO = 8
VMEM_LIMIT = 56 * 1024 * 1024
MESH = pl.DeviceIdType.MESH
N_DEV = 8

ADAM_LR, ADAM_B1, ADAM_B2, ADAM_EPS, ADAM_WD, ADAM_STEP = 0.001, 0.9, 0.999, 1e-08, 0.01, 10

GDN_HEADS, GDN_DIM, GDN_CHUNK = 4, 128, 64
SGU_GROUPS, SGU_CHUNK = 4, 128
LRU_BLOCKS, LRU_C = 8, 8.0
D_G = 512
N_IN = 5640
N_INP = 5760
BA_COL = 5632

SHARDED_SMALL = ("gdn_conv_w", "lru_conv_w", "sconv_w", "grp_norm_w", "ffn_conv_w")
BIG = ("w_in", "w_out", "ffn_up", "ffn_down")
WEIGHTS = ("pre_mix_norm", "w_in", "gdn_conv_w", "gdn_a_log", "gdn_dt_bias", "gdn_norm_w", "lru_conv_w",
           "lru_conv_b", "lru_wa", "lru_ba", "lru_wx", "lru_bx", "lru_lambda", "sgu_ln_w", "sgu_ln_b", "sgu_ws",
           "sgu_b", "sconv_w", "grp_norm_w", "w_out", "post_mix_norm", "pre_ffn_norm", "ffn_up", "ffn_conv_w",
           "ffn_conv_b", "ffn_down", "post_ffn_norm")
SMALL = tuple(n for n in WEIGHTS if n not in BIG)


def _dot(a, b, ca, cb):
    return lax.dot_general(a.astype(BF16), b.astype(BF16), (((ca,), (cb,)), ((), ())),
                           preferred_element_type=F32)


@jax.custom_vjp
def _mm(a, b):
    return _dot(a, b, 1, 0)


def _mm_f(a, b):
    return _dot(a, b, 1, 0), (a, b)


def _mm_b(res, g):
    a, b = res
    return _dot(g, b, 1, 1), _dot(a, g, 0, 0)


_mm.defvjp(_mm_f, _mm_b)


@jax.custom_vjp
def _mm_nt(a, b):
    return _dot(a, b, 1, 1)


def _mm_nt_f(a, b):
    return _dot(a, b, 1, 1), (a, b)


def _mm_nt_b(res, g):
    a, b = res
    return _dot(g, b, 1, 0), _dot(g, a, 0, 0)


_mm_nt.defvjp(_mm_nt_f, _mm_nt_b)


@jax.custom_vjp
def _mm_tn(a, b):
    return _dot(a, b, 0, 0)


def _mm_tn_f(a, b):
    return _dot(a, b, 0, 0), (a, b)


def _mm_tn_b(res, g):
    a, b = res
    return _dot(b, g, 1, 1), _dot(a, g, 1, 0)


_mm_tn.defvjp(_mm_tn_f, _mm_tn_b)


def _dot_exact(a, b, ca, cb):
    return lax.dot_general(a, b, (((ca,), (cb,)), ((), ())), precision=lax.Precision.HIGHEST,
                           preferred_element_type=F32)


@functools.partial(jax.custom_vjp, nondiff_argnums=(1,))
def _shift_rows(x, s):
    return pltpu.roll(x, s, 0)


def _shift_rows_f(x, s):
    return pltpu.roll(x, s, 0), None


def _shift_rows_b(s, _, g):
    return (pltpu.roll(g, (g.shape[0] - s) % g.shape[0], 0),)


_shift_rows.defvjp(_shift_rows_f, _shift_rows_b)


def _sigmoid(x):
    return 1.0 / (1.0 + jnp.exp(-x))


def _silu(x):
    return x * _sigmoid(x)


GELU_C, GELU_A = 0.7978845608028654, 0.044715


@jax.custom_vjp
def _gelu(x):
    return 0.5 * x * (1.0 + jnp.tanh(GELU_C * (x + GELU_A * (x * x * x))))


def _gelu_f(x):
    t = jnp.tanh(GELU_C * (x + GELU_A * (x * x * x)))
    return 0.5 * x * (1.0 + t), (x, t)


def _gelu_b(res, g):
    x, t = res
    slope = 0.5 * (1.0 + t) + (0.5 * GELU_C) * x * (1.0 - t * t) * (1.0 + (3.0 * GELU_A) * (x * x))
    return (g * slope,)


_gelu.defvjp(_gelu_f, _gelu_b)


@jax.custom_vjp
def _softplus(x):
    e = jnp.exp(-jnp.abs(x))
    u = 1.0 + e
    log1p = jnp.where(u == 1.0, e, jnp.log(u) * (e / jnp.where(u == 1.0, 1.0, u - 1.0)))
    return jnp.maximum(x, 0.0) + log1p


def _softplus_f(x):
    return _softplus(x), x


def _softplus_b(x, g):
    return (g * _sigmoid(x),)


_softplus.defvjp(_softplus_f, _softplus_b)


def _neg_expm1(y):
    return -jnp.tanh(0.5 * y) * (jnp.exp(y) + 1.0)


def _rms(x, w):
    return x * lax.rsqrt(jnp.mean(x * x, axis=-1, keepdims=True) + EPS) * w


def _row(w, k):
    sel = lax.broadcasted_iota(jnp.int32, w.shape, 0) == k
    return jnp.sum(jnp.where(sel, w, 0.0), axis=0, keepdims=True)


def _col(x, j):
    sel = lax.broadcasted_iota(jnp.int32, x.shape, 1) == j
    return jnp.sum(jnp.where(sel, x, 0.0), axis=1, keepdims=True)


def _conv(x_ext, w, taps):
    acc = None
    for k in range(taps):
        s = taps - 1 - k
        t = (x_ext if s == 0 else _shift_rows(x_ext, s)) * _row(w, k)
        acc = t if acc is None else acc + t
    return acc[HALO:]


@jax.custom_vjp
def _scan(a, b, h0):
    n = a.shape[0]
    row = lax.broadcasted_iota(jnp.int32, a.shape, 0)
    s = 1
    while s < n:
        keep = row >= s
        a_sh = jnp.where(keep, pltpu.roll(a, s, 0), 1.0)
        b_sh = jnp.where(keep, pltpu.roll(b, s, 0), 0.0)
        b = a * b_sh + b
        a = a * a_sh
        s *= 2
    return b + a * h0


def _scan_f(a, b, h0):
    h = _scan(a, b, h0)
    return h, (a, h, h0)


def _scan_b(res, dh):
    a, h, h0 = res
    n = a.shape[0]
    row = lax.broadcasted_iota(jnp.int32, a.shape, 0)
    an = jnp.where(row < n - 1, pltpu.roll(a, n - 1, 0), 0.0)
    lam = dh
    s = 1
    while s < n:
        keep = row < n - s
        a_sh = jnp.where(keep, pltpu.roll(an, n - s, 0), 1.0)
        l_sh = jnp.where(keep, pltpu.roll(lam, n - s, 0), 0.0)
        lam = an * l_sh + lam
        an = an * a_sh
        s *= 2
    h_prev = jnp.where(row >= 1, pltpu.roll(h, 1, 0), h0)
    al = a * lam
    dh0 = jnp.sum(jnp.where(row == 0, al, 0.0), axis=0, keepdims=True)
    return lam * h_prev, lam, dh0


_scan.defvjp(_scan_f, _scan_b)


@jax.custom_vjp
def _unit_lower_inverses(ms):
    n = ms[0].shape[0]
    shape = ms[0].shape
    eye = (lax.broadcasted_iota(jnp.int32, shape, 0) == lax.broadcasted_iota(jnp.int32, shape, 1)).astype(F32)
    p = [-m for m in ms]
    t = [eye + a for a in p]
    steps = 1
    while 2 ** steps < n:
        p = [_mm(a, a) for a in p]
        t = [a + _mm(a, c) for a, c in zip(t, p)]
        steps += 1
    return t


def _unit_lower_inverses_f(ms):
    t = _unit_lower_inverses(ms)
    return t, t


def _unit_lower_inverses_b(t, dt):
    x = [_mm_nt(g, a) for g, a in zip(dt, t)]
    return ([-_mm_tn(a, c) for a, c in zip(t, x)],)


_unit_lower_inverses.defvjp(_unit_lower_inverses_f, _unit_lower_inverses_b)


def _last_row(x):
    sel = lax.broadcasted_iota(jnp.int32, x.shape, 0) == x.shape[0] - 1
    return jnp.sum(jnp.where(sel, x, 0.0), axis=0, keepdims=True)


def fn_norm(xs, st, ps):
    (x,), (w,) = xs, ps
    return [_rms(x, w).astype(BF16)], []


def fn_norm_keep(xs, st, ps):
    (x,), (w,) = xs, ps
    return [_rms(x, w).astype(BF16), x], []


def fn_res(xs, st, ps):
    (x, y), (w_post, w_next) = xs, ps
    x1 = x + _rms(y, w_post)
    return [x1, _rms(x1, w_next).astype(BF16)], []


def fn_res_last(xs, st, ps):
    (x, y), (w_post,) = xs, ps
    return [x + _rms(y, w_post)], []


def fn_gdn(xs, st, ps):
    qkv_ext, z, ba = xs
    (state,) = st
    cw, gp, nw = ps
    ts = z.shape[0]
    qkv = _silu(_conv(qkv_ext, cw, 4))
    beta_all = _sigmoid(ba)
    g_all = -jnp.exp(_row(gp, 0)) * _softplus(ba + _row(gp, 1))
    c_n = GDN_CHUNK
    ri = lax.broadcasted_iota(jnp.int32, (c_n, c_n), 0)
    ci = lax.broadcasted_iota(jnp.int32, (c_n, c_n), 1)
    causal, strict = ri >= ci, ri > ci
    tril = causal.astype(F32)
    lane = lax.broadcasted_iota(jnp.int32, (c_n, 128), 1)
    s_h = [state[GDN_DIM * h:GDN_DIM * (h + 1)] for h in range(GDN_HEADS)]
    n_c = ts // c_n
    pairs = [(c, h) for c in range(n_c) for h in range(GDN_HEADS)]
    every = lambda f, *lists: [f(*a) for a in zip(*lists)]

    def piece(c, h, base):
        return qkv[c * c_n:(c + 1) * c_n, base + GDN_DIM * h:base + GDN_DIM * (h + 1)]

    q = [piece(c, h, 0) for c, h in pairs]
    k = [piece(c, h, D_G) for c, h in pairs]
    v = [piece(c, h, 2 * D_G) for c, h in pairs]
    q = every(lambda t: t * lax.rsqrt(jnp.sum(t * t, axis=-1, keepdims=True) + EPS) * (GDN_DIM ** -0.5), q)
    k = every(lambda t: t * lax.rsqrt(jnp.sum(t * t, axis=-1, keepdims=True) + EPS), k)
    gcum_all = [_dot_exact(tril, g_all[c * c_n:(c + 1) * c_n], 1, 0) for c in range(n_c)]
    b = [_col(beta_all[c * c_n:(c + 1) * c_n], h) for c, h in pairs]
    gc = [_col(gcum_all[c], 4 + h) for c, h in pairs]
    gr = [_dot_exact((lane == 4 + h).astype(F32), gcum_all[c], 1, 1) for c, h in pairs]
    decay = every(lambda a, r: jnp.where(causal, jnp.exp(jnp.where(causal, a - r, 0.0)), 0.0), gc, gr)
    kb = every(lambda a, c: a * c, k, b)
    mk = every(lambda a, c, e: _mm_nt(jnp.concatenate([a, c], axis=0), e), kb, q, k)
    m = every(lambda a, dcy: jnp.where(strict, a[:c_n] * dcy, 0.0), mk, decay)
    attn = every(lambda a, dcy: jnp.where(causal, a[c_n:] * dcy, 0.0), mk, decay)
    t_ = _unit_lower_inverses(m)
    eg = every(jnp.exp, gc)
    wu = every(lambda t, a, e, c, d: _mm(t, jnp.concatenate([a * e, c * d], axis=1)), t_, kb, eg, v, b)
    g_last = every(_last_row, gc)
    k_g = every(lambda a, gl, g: a * jnp.exp(gl - g), k, g_last, gc)
    wq = every(lambda a, c, e: jnp.concatenate([a[:, :GDN_DIM], c * e], axis=0), wu, q, eg)
    u = [a[:, GDN_DIM:] for a in wu]
    gl = every(jnp.exp, g_last)

    o = []
    for c in range(n_c):
        idx = range(c * GDN_HEADS, (c + 1) * GDN_HEADS)
        ws = [_mm(wq[i], s_h[h]) for h, i in enumerate(idx)]
        v_new = [u[i] - ws[h][:c_n] for h, i in enumerate(idx)]
        av = [_mm(attn[i], v_new[h]) for h, i in enumerate(idx)]
        kv = [_mm_tn(k_g[i], v_new[h]) for h, i in enumerate(idx)]
        o += [ws[h][c_n:] + av[h] for h in range(GDN_HEADS)]
        s_h = [s_h[h] * gl[i] + kv[h] for h, i in enumerate(idx)]
    zz = [z[c * c_n:(c + 1) * c_n, GDN_DIM * h:GDN_DIM * (h + 1)] for c, h in pairs]
    y = every(lambda a, g: a * lax.rsqrt(jnp.mean(a * a, axis=-1, keepdims=True) + EPS) * nw * _silu(g), o, zz)
    rows = [jnp.concatenate(y[c * GDN_HEADS:(c + 1) * GDN_HEADS], axis=1) for c in range(n_c)]
    y = rows[0] if n_c == 1 else jnp.concatenate(rows, axis=0)
    return [y.astype(BF16)], [jnp.concatenate(s_h, axis=0)]


def fn_lru(xs, st, ps):
    x_ext, gate = xs
    (h0,) = st
    cw, cb, wa, ba, wx, bx, lam, gw = ps
    xc = _conv(x_ext, cw, 4) + cb
    r = _sigmoid(_mm(xc, wa) + ba)
    i = _sigmoid(_mm(xc, wx) + bx)
    log_a = -LRU_C * r * _softplus(-lam)
    a = jnp.exp(log_a)
    mult = jnp.sqrt(_neg_expm1(2.0 * log_a))
    h = _scan(a, mult * (i * xc), h0)
    y = _rms(h * _gelu(gate), gw)
    return [y.astype(BF16)], [_last_row(h)]


def fn_sgu(xs, st, ps):
    (uv,) = xs
    lnw, lnb, ws, bst, gw = ps
    ts = uv.shape[0]
    uvf = _gelu(uv)
    u, v = uvf[:, :D_G], uvf[:, D_G:]
    vc = v - jnp.mean(v, axis=-1, keepdims=True)
    v = vc * lax.rsqrt(jnp.mean(vc * vc, axis=-1, keepdims=True) + EPS) * lnw + lnb
    t_n = SGU_CHUNK
    tril = lax.broadcasted_iota(jnp.int32, (t_n, t_n), 0) >= lax.broadcasted_iota(jnp.int32, (t_n, t_n), 1)
    wg = [jnp.where(tril, ws[t_n * g:t_n * (g + 1)], 0.0) for g in range(SGU_GROUPS)]
    bg = [_col(bst, g) for g in range(SGU_GROUPS)]
    rows = []
    for c in range(ts // t_n):
        vcg = v[c * t_n:(c + 1) * t_n]
        rows.append(jnp.concatenate(
            [_mm(wg[g], vcg[:, 128 * g:128 * (g + 1)]) + bg[g] for g in range(SGU_GROUPS)], axis=1))
    vv = rows[0] if len(rows) == 1 else jnp.concatenate(rows, axis=0)
    return [_rms(u * vv, gw).astype(BF16)], []


def fn_sconv(xs, st, ps):
    bg, cg_ext, hh_ext = xs
    cw, gw = ps
    return [_rms(bg * _conv(cg_ext * hh_ext, cw, 3), gw).astype(BF16)], []


def fn_ffn(xs, st, ps):
    g_ext, v_ext = xs
    cwg, cwv, cbg, cbv = ps
    g = _conv(g_ext, cwg, 3) + cbg
    v = _conv(v_ext, cwv, 3) + cbv
    return [(_gelu(g) * v).astype(BF16)], []


def _my_pos():
    return lax.axis_index("x"), lax.axis_index("y"), lax.axis_index("c")


def _peer(pos, k):
    x_, y_, c_ = pos
    return (1 - x_ if (k >> 2) & 1 else x_, 1 - y_ if (k >> 1) & 1 else y_, 1 - c_ if k & 1 else c_)


def _dev_index(p):
    return 4 * p[0] + 2 * p[1] + p[2]


class _Side:
    def __init__(self, jobs):
        self.jobs = list(jobs)
        n = len(self.jobs)
        self.operands = [a for _, a in self.jobs]
        self.in_specs = [pl.BlockSpec(memory_space=pl.ANY)] * n
        self.out_shape = [jax.ShapeDtypeStruct(((N_DEV,) + a.shape) if kind == "gather" else a.shape, a.dtype)
                          for kind, a in self.jobs]
        self.out_specs = [pl.BlockSpec(memory_space=pl.ANY)] * n
        self.scratch = [pltpu.SemaphoreType.DMA((7 * n,)), pltpu.SemaphoreType.DMA((7 * n,)),
                        pltpu.SemaphoreType.DMA((n,))] if n else []

    def _copies(self, in_refs, out_refs, sems, landings=True):
        send, recv, local = sems
        pos = _my_pos()
        me = _dev_index(pos)
        mine, outgoing, landing = [], [], []
        for j, (kind, _) in enumerate(self.jobs):
            src, dst = in_refs[j], out_refs[j]
            own = src if kind == "gather" else src.at[me]
            mine.append(pltpu.make_async_copy(own, dst.at[me], local.at[j]))
            for k in range(1, N_DEV):
                p = _peer(pos, k)
                sems_k = dict(send_sem=send.at[7 * j + k - 1], recv_sem=recv.at[7 * j + k - 1], device_id=p,
                              device_id_type=MESH)
                outgoing.append(pltpu.make_async_remote_copy(
                    src_ref=src if kind == "gather" else src.at[_dev_index(p)], dst_ref=dst.at[me], **sems_k))
                if landings:
                    landing.append(pltpu.make_async_remote_copy(src_ref=own, dst_ref=dst.at[_dev_index(p)], **sems_k))
        return mine, outgoing, landing

    def start(self, in_refs, out_refs, sems):
        mine, outgoing, _ = self._copies(in_refs, out_refs, sems, landings=False)
        for cp in mine + outgoing:
            cp.start()

    def wait(self, in_refs, out_refs, sems):
        mine, outgoing, landing = self._copies(in_refs, out_refs, sems)
        for cp in landing:
            cp.wait_recv()
        for cp in outgoing:
            cp.wait_send()
        for cp in mine:
            cp.wait()


def _rin(arr, w=None, col=0, halo=False):
    return dict(arr=arr, w=arr.shape[1] if w is None else w, col=col, halo=halo)


def _par(arr, w=None, col=None, row=None):
    return dict(arr=arr, w=w, col=col, row=row)


def _colidx(col, j):
    return col(j) if callable(col) else col


def _block_call(name, body, n_rows, ts, ncol, reverse, row_ins, blk_ins, params, row_outs, blk_outs, acc_outs,
                carries, side=()):
    side = _Side(side)
    ts = min(ts, n_rows)
    nblk = n_rows // ts
    hb = ts // HALO

    def rr(i):
        return (nblk - 1 - i) if reverse else i

    in_specs, operands = [], []
    for s in row_ins:
        in_specs.append(pl.BlockSpec((ts, s["w"]), lambda j, i, s=s: (rr(i), _colidx(s["col"], j))))
        operands.append(s["arr"])
        if s["halo"]:
            in_specs.append(pl.BlockSpec((HALO, s["w"]),
                                         lambda j, i, s=s: (jnp.maximum(rr(i) * hb - 1, 0), _colidx(s["col"], j))))
            operands.append(s["arr"])
    for a in blk_ins:
        nd = a.ndim - 1
        in_specs.append(pl.BlockSpec((None,) + a.shape[1:], lambda j, i, nd=nd: (rr(i),) + (0,) * nd))
        operands.append(a)
    for p in params:
        a = p["arr"]
        if p["row"] is not None:
            in_specs.append(pl.BlockSpec((p["w"], a.shape[1]), lambda j, i, p=p: (_colidx(p["row"], j), 0)))
        elif p["col"] is None:
            in_specs.append(pl.BlockSpec(a.shape, lambda j, i: (0, 0)))
        else:
            in_specs.append(pl.BlockSpec((a.shape[0], p["w"]), lambda j, i, p=p: (0, _colidx(p["col"], j))))
        operands.append(a)

    out_specs, out_shape = [], []
    for o in row_outs:
        out_specs.append(pl.BlockSpec((ts, o["w"]), lambda j, i, o=o: (rr(i), _colidx(o["col"], j))))
        out_shape.append(jax.ShapeDtypeStruct((n_rows, o["total"]), o["dtype"]))
    for o in blk_outs:
        nd = len(o["shape"])
        out_specs.append(pl.BlockSpec((None,) + tuple(o["shape"]), lambda j, i, nd=nd: (rr(i),) + (0,) * nd))
        out_shape.append(jax.ShapeDtypeStruct((nblk,) + tuple(o["shape"]), o["dtype"]))
    for o in acc_outs:
        if o["col"] is None:
            out_specs.append(pl.BlockSpec(o["shape"], lambda j, i: (0, 0)))
            out_shape.append(jax.ShapeDtypeStruct(o["shape"], F32))
        else:
            out_specs.append(pl.BlockSpec(o["shape"], lambda j, i, o=o: (0, _colidx(o["col"], j))))
            out_shape.append(jax.ShapeDtypeStruct((o["shape"][0], o["total"]), F32))

    n_in = len(operands)
    n_row_out, n_blk_out, n_acc = len(row_outs), len(blk_outs), len(acc_outs)
    n_out = n_row_out + n_blk_out + n_acc
    n_side = len(side.jobs)

    def kern(*refs):
        in_refs = refs[:n_in]
        side_in = refs[n_in:n_in + n_side]
        out_refs = refs[n_in + n_side:n_in + n_side + n_out]
        side_out = refs[n_in + n_side + n_out:n_in + 2 * n_side + n_out]
        scratch = refs[n_in + 2 * n_side + n_out:]
        carry_refs, side_sems = scratch[:len(carries)], scratch[len(carries):]
        acc_refs = out_refs[n_row_out + n_blk_out:]
        i = pl.program_id(1)
        r = rr(i)
        if n_side:
            @pl.when((pl.program_id(0) == 0) & (i == 0))
            def _():
                side.start(side_in, side_out, side_sems)

        @pl.when(i == 0)
        def _():
            for c_ref in carry_refs:
                c_ref[...] = jnp.zeros(c_ref.shape, c_ref.dtype)
            for a_ref in acc_refs:
                a_ref[...] = jnp.zeros(a_ref.shape, a_ref.dtype)

        k = 0
        xs = []
        for s in row_ins:
            x = in_refs[k][...]
            k += 1
            if s["halo"]:
                hal = in_refs[k][...]
                k += 1
                hal = jnp.where(r == 0, jnp.zeros_like(hal), hal)
                x = jnp.concatenate([hal, x], axis=0)
            xs.append(x)
        blks = []
        for _ in blk_ins:
            blks.append(in_refs[k][...])
            k += 1
        ps = []
        for _ in params:
            ps.append(in_refs[k][...])
            k += 1
        row_vals, blk_vals, acc_vals, new_carries = body(xs, blks, ps, [c[...] for c in carry_refs], r)
        for ref, val in zip(out_refs[:n_row_out], row_vals):
            ref[...] = val.astype(ref.dtype)
        for ref, val in zip(out_refs[n_row_out:n_row_out + n_blk_out], blk_vals):
            ref[...] = val.astype(ref.dtype)
        for ref, val in zip(acc_refs, acc_vals):
            ref[...] += val
        for ref, val in zip(carry_refs, new_carries):
            ref[...] = val
        if n_side:
            @pl.when((pl.program_id(0) == ncol - 1) & (i == nblk - 1))
            def _():
                side.wait(side_in, side_out, side_sems)

    res = pl.pallas_call(
        kern,
        name=name,
        grid=(ncol, nblk),
        in_specs=in_specs + side.in_specs,
        out_specs=out_specs + side.out_specs,
        out_shape=out_shape + side.out_shape,
        scratch_shapes=[pltpu.VMEM(shape, F32) for shape in carries] + side.scratch,
        compiler_params=pltpu.CompilerParams(dimension_semantics=("arbitrary", "arbitrary"),
                                             vmem_limit_bytes=VMEM_LIMIT),
    )(*operands, *side.operands)
    return list(res)


def _out(w, dtype, total=None, col=0):
    return dict(w=w, dtype=dtype, total=w if total is None else total, col=col)


def seq_fwd(name, fn, n_rows, ts, row_ins, params, outs, state_shapes=(), ncol=1, side=()):
    def body(xs, blks, ps, carries, r):
        o, new_st = fn(xs, list(carries), ps)
        return o, list(carries), [], new_st

    res = _block_call(name, body, n_rows, ts, ncol, False, row_ins, [], params, outs,
                      [dict(shape=s, dtype=F32) for s in state_shapes], [], list(state_shapes), side)
    n_o, n_s = len(outs), len(state_shapes)
    return (res[:n_o], res[n_o:n_o + n_s]) + ((res[n_o + n_s:],) if side else ())


def seq_bwd(name, fn, n_rows, ts, row_ins, params, cots, saved_states=(), din_dtypes=None, ncol=1, din_specs=None,
            side=(), cot_map=None, aux=()):
    n_x, n_p, n_st = len(row_ins), len(params), len(saved_states)
    halo_idx = [k for k, s in enumerate(row_ins) if s["halo"]]
    state_shapes = [a.shape[1:] for a in saved_states]

    def body(xs_all, blks, ps, carries, r):
        xs, cot_vals = xs_all[:n_x], xs_all[n_x:]
        d_state, d_halo = carries[:n_st], carries[n_st:]
        if cot_map is not None:
            cot_vals = cot_map(cot_vals, ps[n_p:])
        (o, _), vjp = jax.vjp(lambda a, b, c: fn(a, b, c), xs, blks, ps[:n_p])
        cot = [c.astype(v.dtype) for c, v in zip(cot_vals, o)]
        dxs, dst, dps = vjp((cot, list(d_state)))
        row_vals, new_halo = [], []
        for k, dx in enumerate(dxs):
            if k in halo_idx:
                hk = halo_idx.index(k)
                rows = dx.shape[0] - HALO
                tail = dx[rows:] + d_halo[hk]
                row_vals.append(jnp.concatenate([dx[HALO:rows], tail], axis=0))
                new_halo.append(dx[:HALO])
            else:
                row_vals.append(dx)
        return row_vals, [], list(dps), list(dst) + new_halo

    din_dtypes = din_dtypes or [F32] * n_x
    douts = []
    for k, s in enumerate(row_ins):
        total, col = (s["w"], 0) if din_specs is None or din_specs[k] is None else din_specs[k]
        douts.append(_out(s["w"], din_dtypes[k], total, col))
    accs = []
    for p in params:
        a = p["arr"]
        if p["col"] is None:
            accs.append(dict(shape=a.shape, total=None, col=None))
        else:
            accs.append(dict(shape=(a.shape[0], p["w"]), total=a.shape[1], col=p["col"]))
    carries = list(state_shapes) + [(HALO, row_ins[k]["w"]) for k in halo_idx]
    res = _block_call(name, body, n_rows, ts, ncol, True, list(row_ins) + list(cots), list(saved_states),
                      list(params) + list(aux), douts, [], accs, carries, side)
    return (res[:n_x], res[n_x:n_x + n_p]) + ((res[n_x + n_p:],) if side else ())


def matmul(name, a, b, mode, out_dtype, tm, tn, tk, side=(), a2=None, b2=None, col_blocks=False):
    side = _Side(side)
    n_side = len(side.jobs)
    if mode == "tn":
        (kk, m), n = a.shape, b.shape[1]
    else:
        (m, kk), n = a.shape, (b.shape[0] if mode == "nt" else b.shape[1])
    k1, n1 = kk, n
    if a2 is not None:
        assert mode != "tn" and b2 is None
        kk += a2.shape[1]
    if b2 is not None:
        assert mode == "tn"
        n += b2.shape[1]
    tm, tn, tk = min(tm, m), min(tn, n), min(tk, kk)
    nk, gm, gn = kk // tk, m // tm, n // tn
    assert m % tm == 0 and n % tn == 0 and kk % tk == 0 and k1 % tk == 0 and n1 % tn == 0, (name, a.shape, b.shape)
    nk1, gn1 = k1 // tk, n1 // tn
    if mode == "tn":
        a_specs = [pl.BlockSpec((tk, tm), lambda i, j, k: (k, i))]
        b_specs = [pl.BlockSpec((tk, tn), lambda i, j, k: (k, jnp.minimum(j, gn1 - 1)))]
        if b2 is not None:
            b_specs.append(pl.BlockSpec((tk, tn), lambda i, j, k: (k, jnp.maximum(j - gn1, 0))))
    else:
        a_specs = [pl.BlockSpec((tm, tk), lambda i, j, k: (i, jnp.minimum(k, nk1 - 1)))]
        if a2 is not None:
            a_specs.append(pl.BlockSpec((tm, tk), lambda i, j, k: (i, jnp.maximum(k - nk1, 0))))
        b_specs = [pl.BlockSpec((tn, tk), lambda i, j, k: (j, k)) if mode == "nt"
                   else pl.BlockSpec((tk, tn), lambda i, j, k: (k, j))]
    ca, cb = {"nn": (1, 0), "nt": (1, 1), "tn": (0, 0)}[mode]
    n_a, n_b = len(a_specs), len(b_specs)

    def kern(*refs):
        a_refs, b_refs = refs[:n_a], refs[n_a:n_a + n_b]
        rest = refs[n_a + n_b:]
        side_in = rest[:n_side]
        o_ref = rest[n_side]
        side_out = rest[1 + n_side:1 + 2 * n_side]
        acc_ref = rest[1 + 2 * n_side]
        side_sems = rest[2 + 2 * n_side:]
        i, j, k = pl.program_id(0), pl.program_id(1), pl.program_id(2)
        if n_side:
            @pl.when((i == 0) & (j == 0) & (k == 0))
            def _():
                side.start(side_in, side_out, side_sems)

        def step(a_ref, b_ref):
            part = lax.dot_general(a_ref[...], b_ref[...], (((ca,), (cb,)), ((), ())), preferred_element_type=F32)
            if nk == 1:
                o_ref[...] = part.astype(o_ref.dtype)
            else:
                @pl.when(k == 0)
                def _():
                    acc_ref[...] = part

                @pl.when(k > 0)
                def _():
                    acc_ref[...] += part

                @pl.when(k == nk - 1)
                def _():
                    o_ref[...] = acc_ref[...].astype(o_ref.dtype)

        if n_a == 2:
            pl.when(k < nk1)(lambda: step(a_refs[0], b_refs[0]))
            pl.when(k >= nk1)(lambda: step(a_refs[1], b_refs[0]))
        elif n_b == 2:
            pl.when(j < gn1)(lambda: step(a_refs[0], b_refs[0]))
            pl.when(j >= gn1)(lambda: step(a_refs[0], b_refs[1]))
        else:
            step(a_refs[0], b_refs[0])

        if n_side:
            @pl.when((i == gm - 1) & (j == gn - 1) & (k == nk - 1))
            def _():
                side.wait(side_in, side_out, side_sems)

    semantics = ("arbitrary",) * 3 if n_side else ("parallel", "parallel", "arbitrary")
    operands = [a] + ([a2] if a2 is not None else []) + [b] + ([b2] if b2 is not None else [])
    res = pl.pallas_call(
        kern,
        name=name,
        grid=(gm, gn, nk),
        in_specs=a_specs + b_specs + side.in_specs,
        out_specs=[pl.BlockSpec((None, tm, tn), lambda i, j, k: (j, i, 0)) if col_blocks
                   else pl.BlockSpec((tm, tn), lambda i, j, k: (i, j))] + side.out_specs,
        out_shape=[jax.ShapeDtypeStruct((gn, m, tn) if col_blocks else (m, n), out_dtype)] + side.out_shape,
        scratch_shapes=[pltpu.VMEM((tm, tn) if nk > 1 else (8, 128), F32)] + side.scratch,
        compiler_params=pltpu.CompilerParams(dimension_semantics=semantics, vmem_limit_bytes=VMEM_LIMIT),
    )(*operands, *side.operands)
    return (res[0], list(res[1:])) if n_side else res[0]


def all_gather(name, x, in_vmem):
    def body(x_ref, out_ref, send_sems, recv_sems, local_sem):
        x_, y_, c_ = _my_pos()
        me, sibling = (x_, y_, c_), (x_, y_, 1 - c_)
        chips = [(1 - x_, y_), (x_, 1 - y_), (1 - x_, 1 - y_)]

        def slot(px, py, pc):
            return out_ref.at[4 * px + 2 * py + pc]

        def copy(k, block, to, src=None):
            return pltpu.make_async_remote_copy(
                src_ref=slot(*block) if src is None else src, dst_ref=slot(*block),
                send_sem=send_sems.at[k], recv_sem=recv_sems.at[k], device_id=to, device_id_type=MESH)

        mine = pltpu.make_async_copy(x_ref, slot(*me), local_sem)
        mine.start()
        first = [copy(0, me, sibling, src=x_ref)]
        first += [copy(1 + j, me, (*chip, c_), src=x_ref) for j, chip in enumerate(chips)]
        for cp in first:
            cp.start()
        passed = [copy(4 + j, (*chip, c_), sibling) for j, chip in enumerate(chips)]
        for j, chip in enumerate(chips):
            copy(1 + j, (*chip, c_), me).wait_recv()
            passed[j].start()
        copy(0, sibling, me).wait_recv()
        for j, chip in enumerate(chips):
            copy(4 + j, (*chip, 1 - c_), me).wait_recv()
        for cp in first + passed:
            cp.wait_send()
        mine.wait()

    space = pltpu.VMEM if in_vmem else pl.ANY
    return pl.pallas_call(
        body,
        name=name,
        out_shape=jax.ShapeDtypeStruct((N_DEV,) + x.shape, x.dtype),
        in_specs=[pl.BlockSpec(memory_space=space)],
        out_specs=pl.BlockSpec(memory_space=space),
        scratch_shapes=[pltpu.SemaphoreType.DMA((7,)), pltpu.SemaphoreType.DMA((7,)), pltpu.SemaphoreType.DMA],
        compiler_params=pltpu.CompilerParams(vmem_limit_bytes=VMEM_LIMIT),
    )(x)


def all_to_all(name, g):
    def body(g_ref, out_ref, send_sems, recv_sems, local_sem):
        x_, y_, c_ = _my_pos()
        me = 4 * x_ + 2 * y_ + c_

        def peer(k):
            fx, fy, fc = (k >> 2) & 1, (k >> 1) & 1, k & 1
            return (1 - x_ if fx else x_, 1 - y_ if fy else y_, 1 - c_ if fc else c_)

        def copy(k):
            px, py, pc = peer(k)
            return pltpu.make_async_remote_copy(
                src_ref=g_ref.at[4 * px + 2 * py + pc], dst_ref=out_ref.at[me],
                send_sem=send_sems.at[k - 1], recv_sem=recv_sems.at[k - 1], device_id=(px, py, pc), device_id_type=MESH)

        def landing(k):
            px, py, pc = peer(k)
            return pltpu.make_async_remote_copy(
                src_ref=g_ref.at[me], dst_ref=out_ref.at[4 * px + 2 * py + pc],
                send_sem=send_sems.at[k - 1], recv_sem=recv_sems.at[k - 1], device_id=(px, py, pc), device_id_type=MESH)

        mine = pltpu.make_async_copy(g_ref.at[me], out_ref.at[me], local_sem)
        mine.start()
        sends = [copy(k) for k in range(1, N_DEV)]
        for cp in sends:
            cp.start()
        for k in range(1, N_DEV):
            landing(k).wait_recv()
        for cp in sends:
            cp.wait_send()
        mine.wait()

    return pl.pallas_call(
        body,
        name=name,
        out_shape=jax.ShapeDtypeStruct(g.shape, g.dtype),
        in_specs=[pl.BlockSpec(memory_space=pl.ANY)],
        out_specs=pl.BlockSpec(memory_space=pl.ANY),
        scratch_shapes=[pltpu.SemaphoreType.DMA((7,)), pltpu.SemaphoreType.DMA((7,)), pltpu.SemaphoreType.DMA],
    )(g)


def sum_blocks(name, g):
    def body(g_ref, o_ref):
        acc = g_ref[0]
        for s in range(1, N_DEV):
            acc = acc + g_ref[s]
        o_ref[...] = acc

    r = g.shape[1]
    tr = r // 4 if r % 32 == 0 else r
    return pl.pallas_call(
        body, name=name, grid=(r // tr,),
        in_specs=[pl.BlockSpec((N_DEV, tr, 128), lambda i: (0, i, 0))],
        out_specs=pl.BlockSpec((tr, 128), lambda i: (i, 0)),
        out_shape=jax.ShapeDtypeStruct((r, 128), F32),
        compiler_params=pltpu.CompilerParams(vmem_limit_bytes=VMEM_LIMIT),
    )(g)


def _adamw_math(w, g, m, v):
    m = ADAM_B1 * m + (1.0 - ADAM_B1) * g
    v = ADAM_B2 * v + (1.0 - ADAM_B2) * (g * g)
    m_hat = m / (1.0 - ADAM_B1 ** ADAM_STEP)
    v_hat = v / (1.0 - ADAM_B2 ** ADAM_STEP)
    delta = -ADAM_LR * (m_hat / (jnp.sqrt(v_hat) + ADAM_EPS) + ADAM_WD * w)
    return delta, m, v


ADAMW_BLOCK_BYTES = 3 << 19


def _row_tile(rows, row_bytes, limit):
    best = 8
    for t in range(8, rows + 1, 8):
        if rows % t == 0 and t * row_bytes <= limit:
            best = t
    return best


def adamw_big(name, w, lands, m, v):
    depth, r, c = w.shape
    outs = None
    for (l, part), land in sorted(lands.items()):
        rows = land.shape[1]
        tr = _row_tile(rows, 4 * (-(-c // 128) * 128), ADAMW_BLOCK_BYTES)
        first = part * rows // tr

        def body(w_ref, l_ref, m_ref, v_ref, *rest):
            g_out, d_out, m_out, v_out = rest[-4:]
            g = l_ref[0].astype(F32)
            for s in range(1, N_DEV):
                g = g + l_ref[s].astype(F32)
            delta, m_new, v_new = _adamw_math(w_ref[...], g, m_ref[...], v_ref[...])
            g_out[...] = g
            d_out[...] = delta
            m_out[...] = m_new
            v_out[...] = v_new

        spec = pl.BlockSpec((None, tr, c), lambda i, l=l, first=first: (l, first + i, 0))
        carried = [] if outs is None else list(outs)
        outs = pl.pallas_call(
            body, name=f"{name}_{l}_{part}", grid=(rows // tr,),
            in_specs=[spec, pl.BlockSpec((N_DEV, tr, c), lambda i: (0, i, 0)), spec, spec]
            + [pl.BlockSpec(memory_space=pl.ANY)] * len(carried),
            out_specs=[spec] * 4,
            out_shape=[jax.ShapeDtypeStruct((depth, r, c), F32)] * 4,
            input_output_aliases={4 + k: k for k in range(len(carried))},
            compiler_params=pltpu.CompilerParams(dimension_semantics=("parallel",), vmem_limit_bytes=VMEM_LIMIT),
        )(w, land, m, v, *carried)
    return outs


def adamw_small(name, ws, gs, ms, vs):
    n = len(ws)

    def body(*refs):
        ins, outs = refs[:4 * n], refs[4 * n:]
        for k in range(n):
            delta, m_new, v_new = _adamw_math(ins[k][...], ins[n + k][...], ins[2 * n + k][...], ins[3 * n + k][...])
            outs[k][...] = delta
            outs[n + k][...] = m_new
            outs[2 * n + k][...] = v_new

    res = pl.pallas_call(
        body, name=name,
        out_shape=[jax.ShapeDtypeStruct(w.shape, F32) for w in ws] * 3,
        compiler_params=pltpu.CompilerParams(vmem_limit_bytes=VMEM_LIMIT),
    )(*ws, *gs, *ms, *vs)
    return res[:n], res[n:2 * n], res[2 * n:]


def cast_bf16(name, w):
    depth, r, c = w.shape
    tr = _row_tile(r, 4 * (-(-c // 128) * 128), ADAMW_BLOCK_BYTES)

    def body(w_ref, o_ref):
        o_ref[...] = w_ref[...].astype(BF16)

    spec = pl.BlockSpec((None, tr, c), lambda l, i: (l, i, 0))
    return pl.pallas_call(body, name=name, grid=(depth, r // tr), in_specs=[spec], out_specs=spec,
                          out_shape=jax.ShapeDtypeStruct((depth, r, c), BF16),
                          compiler_params=pltpu.CompilerParams(dimension_semantics=("parallel", "parallel")))(w)


def _rows_of(shape):
    return -(-math.prod(shape) // 128)


def _pack(arrs):
    pieces = []
    for a in arrs:
        flat = a.reshape(-1).astype(F32)
        pieces.append(jnp.pad(flat, (0, (-flat.shape[0]) % 128)).reshape(-1, 128))
    rows = sum(p.shape[0] for p in pieces)
    if rows % 8:
        pieces.append(jnp.zeros((8 - rows % 8, 128), F32))
    return jnp.concatenate(pieces, axis=0)


def _unpack(packed, shapes, lead=()):
    out, r0 = [], 0
    for s in shapes:
        rows, n = _rows_of(s), math.prod(s)
        piece = packed[..., r0:r0 + rows, :].reshape(lead + (rows * 128,))
        out.append(piece[..., :n].reshape(lead + tuple(s)))
        r0 += rows
    return out


def _block_diag(w):
    h, d, _ = w.shape
    eye = jnp.eye(h, dtype=w.dtype)
    return (eye[:, None, :, None] * w[:, :, None, :]).reshape(h * d, h * d)


def _block_diag_grad(g, h):
    d = g.shape[0] // h
    eye = jnp.eye(h, dtype=g.dtype)
    return jnp.sum(g.reshape(h, d, h, d) * eye[:, None, :, None], axis=2)


def _layer_params(wt, l):
    gp = jnp.pad(jnp.stack([wt["gdn_a_log"][l], wt["gdn_dt_bias"][l]]), ((0, 6), (4, 128 - 4 - GDN_HEADS)))
    d_ffh = wt["ffn_conv_w"].shape[-1] // 2
    return dict(
        pre_mix=wt["pre_mix_norm"][l][None], post_mix=wt["post_mix_norm"][l][None],
        pre_ffn=wt["pre_ffn_norm"][l][None], post_ffn=wt["post_ffn_norm"][l][None],
        gdn_cw=wt["gdn_conv_w"][l], gdn_gp=gp, gdn_nw=wt["gdn_norm_w"][l][None],
        lru_cw=wt["lru_conv_w"][l], lru_cb=wt["lru_conv_b"][l][None],
        lru_wa=_block_diag(wt["lru_wa"][l]), lru_ba=wt["lru_ba"][l].reshape(1, -1),
        lru_wx=_block_diag(wt["lru_wx"][l]), lru_bx=wt["lru_bx"][l].reshape(1, -1),
        lru_lam=wt["lru_lambda"][l][None], gw0=wt["grp_norm_w"][l, 0][None], gw1=wt["grp_norm_w"][l, 1][None],
        gw2=wt["grp_norm_w"][l, 2][None],
        sgu_lnw=wt["sgu_ln_w"][l][None], sgu_lnb=wt["sgu_ln_b"][l][None],
        sgu_ws=wt["sgu_ws"][l].reshape(SGU_GROUPS * SGU_CHUNK, SGU_CHUNK),
        sgu_bt=jnp.pad(wt["sgu_b"][l].T, ((0, 0), (0, 128 - SGU_GROUPS))),
        sc_cw=wt["sconv_w"][l],
        ffn_cw=wt["ffn_conv_w"][l], ffn_cb=wt["ffn_conv_b"][l][None], d_ffh=d_ffh,
    )


TS_ROW = 256
TS_GDN = 256
TS_FFN = 512
TS_FFN_BWD = 256
TC_FFN = 512


def _mixers_fwd(l, p, lp, n, side, side_lru):
    qkv = _rin(p, 3 * D_G, 0, halo=True)
    z = _rin(p, D_G, 3)
    ba = _rin(p, 128, BA_COL // 128)
    gdn_ps = [_par(lp["gdn_cw"]), _par(lp["gdn_gp"]), _par(lp["gdn_nw"])]
    res = seq_fwd(f"gdn_fwd_{l}", fn_gdn, n, TS_GDN, [qkv, z, ba], gdn_ps, [_out(D_G, BF16)],
                  state_shapes=[(GDN_HEADS * GDN_DIM, GDN_DIM)], side=side)
    (y_a,), (gdn_st,), side_res = res if side else res + ([],)
    lru_x = _rin(p, D_G, 4, halo=True)
    lru_gate = _rin(p, D_G, 5)
    lru_ps = [_par(lp[k]) for k in ("lru_cw", "lru_cb", "lru_wa", "lru_ba", "lru_wx", "lru_bx", "lru_lam", "gw0")]
    res = seq_fwd(f"lru_fwd_{l}", fn_lru, n, TS_ROW, [lru_x, lru_gate], lru_ps, [_out(D_G, BF16)],
                  state_shapes=[(1, D_G)], side=side_lru)
    (y_b,), (lru_st,), side_res_lru = res if side_lru else res + ([],)
    uv = _rin(p, 2 * D_G, 3)
    sgu_ps = [_par(lp[k]) for k in ("sgu_lnw", "sgu_lnb", "sgu_ws", "sgu_bt", "gw1")]
    (y_c,), _ = seq_fwd(f"sgu_fwd_{l}", fn_sgu, n, TS_ROW, [uv], sgu_ps, [_out(D_G, BF16)])
    sc = [_rin(p, D_G, 8), _rin(p, D_G, 9, halo=True), _rin(p, D_G, 10, halo=True)]
    sc_ps = [_par(lp["sc_cw"]), _par(lp["gw2"])]
    (y_d,), _ = seq_fwd(f"sconv_fwd_{l}", fn_sconv, n, TS_ROW, sc, sc_ps, [_out(D_G, BF16)])
    ins = dict(gdn=([qkv, z, ba], gdn_ps, [gdn_st]), lru=([lru_x, lru_gate], lru_ps, [lru_st]),
               sgu=([uv], sgu_ps, []), sc=(sc, sc_ps, []))
    return jnp.concatenate([y_a, y_b, y_c, y_d], axis=1), ins, side_res, side_res_lru


def _mixers_bwd(l, dymix, ins, n, side):
    cot = lambda g: [_rin(dymix, D_G, g)]
    xs, ps, st = ins["gdn"]
    res = seq_bwd(f"gdn_bwd_{l}", fn_gdn, n, TS_GDN, xs, ps, cot(0), st, [BF16, BF16, BF16], side=side)
    (dqkv, dz, dba), g_gdn, side_res = res if side else res + ([],)
    xs, ps, st = ins["lru"]
    (dlx, dlg), g_lru = seq_bwd(f"lru_bwd_{l}", fn_lru, n, TS_ROW, xs, ps, cot(1), st, [BF16, BF16])
    xs, ps, st = ins["sgu"]
    (duv,), g_sgu = seq_bwd(f"sgu_bwd_{l}", fn_sgu, n, TS_ROW, xs, ps, cot(2), st, [BF16])
    xs, ps, st = ins["sc"]
    (dsb, dsc, dsh), g_sc = seq_bwd(f"sconv_bwd_{l}", fn_sconv, n, TS_ROW, xs, ps, cot(3), st, [BF16, BF16, BF16])
    dp = jnp.concatenate([dqkv, dz, dlx, dlg, duv, dsb, dsc, dsh, dba], axis=1)
    return dp, dict(gdn=g_gdn, lru=g_lru, sgu=g_sgu, sc=g_sc), side_res


def _ffn_ops(hid, lp):
    d_ffh = lp["d_ffh"]
    off = d_ffh // TC_FFN
    xs = [_rin(hid, TC_FFN, lambda j: j, halo=True), _rin(hid, TC_FFN, lambda j: j + off, halo=True)]
    ps = [_par(lp["ffn_cw"], TC_FFN, lambda j: j), _par(lp["ffn_cw"], TC_FFN, lambda j: j + off),
          _par(lp["ffn_cb"], TC_FFN, lambda j: j), _par(lp["ffn_cb"], TC_FFN, lambda j: j + off)]
    return xs, ps, d_ffh


_FROM_BLOCKS = dict(
    w_in=lambda b: _regroup_w_in(b.transpose(1, 0, 2).reshape(b.shape[1], -1)),
    ffn_up=lambda b: b.transpose(1, 0, 2).reshape(b.shape[1], -1),
    w_out=lambda b: b.reshape(-1, b.shape[2]),
    ffn_down=lambda b: b.reshape(-1, b.shape[2]),
)
_TO_BLOCKS = dict(
    w_in=lambda g: _ungroup_w_in(g).reshape(g.shape[0], N_DEV, -1).transpose(1, 0, 2),
    ffn_up=lambda g: g,
    w_out=lambda g: g.reshape(N_DEV, -1, g.shape[1]),
    ffn_down=lambda g: g.reshape(N_DEV, -1, g.shape[1]),
)


class _Traffic:
    PARTS = dict(w_in=1, w_out=1, ffn_up=2, ffn_down=1)

    def __init__(self, whole=None, shards=None):
        self.whole = dict(whole or {})
        self.shards = shards
        self.gathered = {}
        self.pending = {}
        self.landed = {}

    def _rows(self, key):
        name, l, part = key
        rows = self.shards[name].shape[1] // self.PARTS[name]
        return slice(part * rows, (part + 1) * rows)

    def jobs(self, gather=(), exchange=()):
        if self.shards is None:
            return [], []
        keys = [("gather", k) for k in gather if k not in self.gathered and k[:2] not in self.whole]
        keys += [("exchange", k) for k in exchange if k in self.pending]
        jobs = [(kind, self.shards[k[0]][k[1]][self._rows(k)] if kind == "gather" else self.pending[k])
                for kind, k in keys]
        return jobs, keys

    def done(self, keys, results):
        for (kind, k), r in zip(keys, results):
            if kind == "gather":
                self.gathered[k] = r
            else:
                self.landed[k] = r
                del self.pending[k]

    def weight(self, name, l):
        if (name, l) not in self.whole:
            parts = []
            for part in range(self.PARTS[name]):
                k = (name, l, part)
                if k not in self.gathered:
                    self.gathered[k] = all_gather(f"gather_{name}_{l}_{part}", self.shards[name][l][self._rows(k)], False)
                parts.append(self.gathered[k])
            blocks = parts[0] if len(parts) == 1 else jnp.concatenate(parts, axis=1)
            self.whole[(name, l)] = _FROM_BLOCKS[name](blocks)
        return self.whole[(name, l)]

    def grad(self, name, l, g):
        if self.shards is None:
            self.landed[(name, l)] = g
            return
        blocks = _TO_BLOCKS[name](g)
        for part in range(self.PARTS[name]):
            k = (name, l, part)
            self.pending[k] = blocks[:, self._rows(k)]

    def flush(self):
        for (name, l, part), blocks in list(self.pending.items()):
            self.landed[(name, l, part)] = all_to_all(f"exchange_{name}_{l}_{part}", blocks)
            del self.pending[(name, l, part)]


def local_step(x, target, wt, tr):
    n, d = x.shape
    depth = wt["pre_mix_norm"].shape[0]
    lps = [_layer_params(wt, l) for l in range(depth)]
    saved = []
    xin = x

    def mm(name, a, b, mode, dtype, tm, tn, tk, gather=(), exchange=(), **split):
        jobs, keys = tr.jobs(gather, exchange)
        if not jobs:
            return matmul(name, a, b, mode, dtype, tm, tn, tk, **split)
        out, res = matmul(name, a, b, mode, dtype, tm, tn, tk, side=jobs, **split)
        tr.done(keys, res)
        return out

    (h,), _ = seq_fwd("norm_fwd", fn_norm, n, TS_ROW, [_rin(x)], [_par(lps[0]["pre_mix"])], [_out(d, BF16)])
    dx_last = loss = None
    for l in range(depth):
        lp = lps[l]
        p = mm(f"w_in_fwd_{l}", h, tr.weight("w_in", l), "nn", F32, 1024, N_INP // 5, d,
               gather=[("ffn_up", l, 0)])
        jobs, keys = tr.jobs(gather=[("ffn_up", l, 1)])
        jobs_lru, keys_lru = tr.jobs(gather=[("w_out", l, 0)])
        ymix, mix_ins, res, res_lru = _mixers_fwd(l, p, lp, n, jobs, jobs_lru)
        tr.done(keys, res)
        tr.done(keys_lru, res_lru)
        y = mm(f"w_out_fwd_{l}", ymix, tr.weight("w_out", l), "nn", F32, 1024, 1024, d)
        res_ps = [_par(lp["post_mix"]), _par(lp["pre_ffn"])]
        (x1, h2), _ = seq_fwd(f"res_mix_fwd_{l}", fn_res, n, TS_ROW, [_rin(xin), _rin(y)], res_ps,
                              [_out(d, F32), _out(d, BF16)])
        nxt = l + 1 < depth
        hid = mm(f"ffn_up_fwd_{l}", h2, tr.weight("ffn_up", l), "nn", F32, 1024, 1024, d,
                 gather=[("ffn_down", l, 0)] + ([("w_out", l + 1, 0)] if nxt else []))
        f_xs, f_ps, d_ffh = _ffn_ops(hid, lp)
        jobs, keys = tr.jobs(gather=[("w_in", l + 1, 0)] if nxt else [])
        res = seq_fwd(f"ffn_act_fwd_{l}", fn_ffn, n, TS_FFN, f_xs, f_ps,
                      [_out(TC_FFN, BF16, d_ffh, lambda j: j)], ncol=d_ffh // TC_FFN, side=jobs)
        (act,) = res[0]
        tr.done(keys, res[2] if jobs else [])
        yf = mm(f"ffn_down_fwd_{l}", act, tr.weight("ffn_down", l), "nn", F32, 1024, 1024, d_ffh // 2)
        rec = dict(x=xin, h=h, mix_ins=mix_ins, ymix=ymix, y=y, x1=x1, h2=h2, f_xs=f_xs, f_ps=f_ps, act=act, yf=yf)
        if l + 1 < depth:
            ps = [_par(lp["post_ffn"]), _par(lps[l + 1]["pre_mix"])]
            (x2, h), _ = seq_fwd(f"res_ffn_fwd_{l}", fn_res, n, TS_ROW, [_rin(x1), _rin(yf)], ps,
                                 [_out(d, F32), _out(d, BF16)])
            rec["res_ffn_ps"] = ps
            xin = x2
        else:
            def body(xs, blks, ps, carries, r):
                x1_, yf_, t_ = xs
                e = x1_ + _rms(yf_, ps[0]) - t_
                part = 0.5 * jnp.sum(jnp.mean(e * e, axis=-1, keepdims=True), axis=0, keepdims=True)
                return [e * (1.0 / d)], [], [jnp.broadcast_to(part, (8, 128))], []

            ps = [_par(lp["post_ffn"])]
            dx_last, loss = _block_call("loss_fwd", body, n, TS_ROW, 1, False, [_rin(x1), _rin(yf), _rin(target)],
                                        [], ps, [_out(d, F32)], [], [dict(shape=(8, 128), total=None, col=None)], [])
            rec["res_ffn_ps"] = ps
        saved.append(rec)

    grads = {}
    dx2, dh_next = dx_last, None
    for l in reversed(range(depth)):
        rec, lp = saved[l], lps[l]
        d_ffh = lp["d_ffh"]
        g = {}
        if dh_next is None:
            (dx1, dyf), (g["post_ffn"],) = seq_bwd(f"res_ffn_bwd_{l}", fn_res_last, n, TS_ROW,
                                                   [_rin(rec["x1"]), _rin(rec["yf"])], rec["res_ffn_ps"], [_rin(dx2)],
                                                   din_dtypes=[F32, BF16])
        else:
            (dx1, dyf), (g["post_ffn"], g_next_pre) = seq_bwd(
                f"res_ffn_bwd_{l}", fn_res, n, TS_ROW, [_rin(rec["x1"]), _rin(rec["yf"])], rec["res_ffn_ps"],
                [_rin(dx2), _rin(dh_next)], din_dtypes=[F32, BF16])
            grads[l + 1]["pre_mix"] = g_next_pre
        tr.grad("ffn_down", l, mm(f"ffn_down_dw_{l}", rec["act"], dyf, "tn", BF16, d_ffh // 4, 1024, 2048))

        def dact(cot_blocks, aux_blocks):
            return [_dot(cot_blocks[0], aux_blocks[0], 1, 1).astype(BF16)]

        (dhg, dhv), (g_cwg, g_cwv, g_cbg, g_cbv) = seq_bwd(
            f"ffn_act_bwd_{l}", fn_ffn, n, TS_FFN_BWD, rec["f_xs"], rec["f_ps"], [_rin(dyf)],
            din_dtypes=[BF16, BF16], ncol=d_ffh // TC_FFN, din_specs=[(d_ffh, lambda j: j), (d_ffh, lambda j: j)],
            cot_map=dact, aux=[_par(tr.weight("ffn_down", l), TC_FFN, row=lambda j: j)])
        g["ffn_cw"] = jnp.concatenate([g_cwg[:, :d_ffh], g_cwv[:, d_ffh:]], axis=1)
        g["ffn_cb"] = jnp.concatenate([g_cbg[:, :d_ffh], g_cbv[:, d_ffh:]], axis=1)
        dh2 = mm(f"ffn_up_dx_{l}", dhg, tr.weight("ffn_up", l), "nt", BF16, 1024, 1024, d_ffh // 2,
                 exchange=[("ffn_down", l, 0)], a2=dhv)
        tr.grad("ffn_up", l, mm(f"ffn_up_dw_{l}", rec["h2"], dhg, "tn", BF16, 1024, 2 * d_ffh // N_DEV, 2048, b2=dhv,
                                col_blocks=True))
        (dx, dy), (g["post_mix"], g["pre_ffn"]) = seq_bwd(
            f"res_mix_bwd_{l}", fn_res, n, TS_ROW, [_rin(rec["x"]), _rin(rec["y"])],
            [_par(lp["post_mix"]), _par(lp["pre_ffn"])], [_rin(dx1), _rin(dh2)], din_dtypes=[F32, BF16])
        dymix = mm(f"w_out_dx_{l}", dy, tr.weight("w_out", l), "nt", BF16, 1024, 1024, d)
        tr.grad("w_out", l, mm(f"w_out_dw_{l}", rec["ymix"], dy, "tn", BF16, 1024, 1024, 2048))
        jobs, keys = tr.jobs(exchange=[("ffn_up", l, 0), ("w_out", l, 0)])
        dp, g["mix"], res = _mixers_bwd(l, dymix, rec["mix_ins"], n, jobs)
        tr.done(keys, res)
        tr.grad("w_in", l, mm(f"w_in_dw_{l}", rec["h"], dp, "tn", BF16, 1024, N_INP // 5, 2048,
                              exchange=[("ffn_up", l, 1)]))
        dh = mm(f"w_in_dx_{l}", dp, tr.weight("w_in", l), "nt", BF16, 1024, 1024, N_INP // 3,
                exchange=[("w_in", l, 0)])
        grads[l] = g
        dx2, dh_next = dx, dh
    (grad_x,), (g_pre0,) = seq_bwd("norm_bwd", fn_norm_keep, n, TS_ROW, [_rin(x)], [_par(lps[0]["pre_mix"])],
                                   [_rin(dh_next), _rin(dx2)])
    grads[0]["pre_mix"] = g_pre0
    tr.flush()
    return loss[0, 0], grad_x, _name_grads(grads, depth)


def _name_grads(grads, depth):
    per = {k: [] for k in SMALL}
    for l in range(depth):
        g = grads[l]
        m = g["mix"]
        cw, gp, nw = m["gdn"]
        lcw, lcb, lwa, lba, lwx, lbx, llam, gw0 = m["lru"]
        lnw, lnb, ws, bst, gw1 = m["sgu"]
        scw, gw2 = m["sc"]
        per["pre_mix_norm"].append(g["pre_mix"][0])
        per["gdn_conv_w"].append(cw)
        per["gdn_a_log"].append(gp[0, 4:8])
        per["gdn_dt_bias"].append(gp[1, 4:8])
        per["gdn_norm_w"].append(nw[0])
        per["lru_conv_w"].append(lcw)
        per["lru_conv_b"].append(lcb[0])
        per["lru_wa"].append(_block_diag_grad(lwa, LRU_BLOCKS))
        per["lru_ba"].append(lba.reshape(LRU_BLOCKS, -1))
        per["lru_wx"].append(_block_diag_grad(lwx, LRU_BLOCKS))
        per["lru_bx"].append(lbx.reshape(LRU_BLOCKS, -1))
        per["lru_lambda"].append(llam[0])
        per["sgu_ln_w"].append(lnw[0])
        per["sgu_ln_b"].append(lnb[0])
        per["sgu_ws"].append(ws.reshape(SGU_GROUPS, SGU_CHUNK, SGU_CHUNK))
        per["sgu_b"].append(bst[:, :SGU_GROUPS].T)
        per["sconv_w"].append(scw)
        per["grp_norm_w"].append(jnp.concatenate([gw0, gw1, gw2], axis=0))
        per["post_mix_norm"].append(g["post_mix"][0])
        per["pre_ffn_norm"].append(g["pre_ffn"][0])
        per["ffn_conv_w"].append(g["ffn_cw"])
        per["ffn_conv_b"].append(g["ffn_cb"][0])
        per["post_ffn_norm"].append(g["post_ffn"][0])
    return {k: jnp.stack(v) for k, v in per.items()}


def _regroup_w_in(w):
    pad = jnp.zeros(w.shape[:-1] + (N_INP - N_IN,), w.dtype)
    return jnp.concatenate([w[..., :2048], w[..., 2056:], w[..., 2048:2056], pad], axis=-1)


def _ungroup_w_in(g):
    return jnp.concatenate([g[..., :2048], g[..., BA_COL:BA_COL + 8], g[..., 2048:BA_COL]], axis=-1)


def kernel(x, pre_mix_norm, w_in, gdn_conv_w, gdn_a_log, gdn_dt_bias, gdn_norm_w, lru_conv_w, lru_conv_b, lru_wa, lru_ba, lru_wx, lru_bx, lru_lambda, sgu_ln_w, sgu_ln_b, sgu_ws, sgu_b, sconv_w, grp_norm_w, w_out, post_mix_norm, pre_ffn_norm, ffn_up, ffn_conv_w, ffn_conv_b, ffn_down, post_ffn_norm, loss_target, m_pre_mix_norm, m_w_in, m_gdn_conv_w, m_gdn_a_log, m_gdn_dt_bias, m_gdn_norm_w, m_lru_conv_w, m_lru_conv_b, m_lru_wa, m_lru_ba, m_lru_wx, m_lru_bx, m_lru_lambda, m_sgu_ln_w, m_sgu_ln_b, m_sgu_ws, m_sgu_b, m_sconv_w, m_grp_norm_w, m_w_out, m_post_mix_norm, m_pre_ffn_norm, m_ffn_up, m_ffn_conv_w, m_ffn_conv_b, m_ffn_down, m_post_ffn_norm, v_pre_mix_norm, v_w_in, v_gdn_conv_w, v_gdn_a_log, v_gdn_dt_bias, v_gdn_norm_w, v_lru_conv_w, v_lru_conv_b, v_lru_wa, v_lru_ba, v_lru_wx, v_lru_bx, v_lru_lambda, v_sgu_ln_w, v_sgu_ln_b, v_sgu_ws, v_sgu_b, v_sconv_w, v_grp_norm_w, v_w_out, v_post_mix_norm, v_pre_ffn_norm, v_ffn_up, v_ffn_conv_w, v_ffn_conv_b, v_ffn_down, v_post_ffn_norm):
    args = locals()
    w_loc = {k: args[k] for k in WEIGHTS}
    m_loc = {k: args["m_" + k] for k in WEIGHTS}
    v_loc = {k: args["v_" + k] for k in WEIGHTS}
    depth = pre_mix_norm.shape[0]
    x_, y_, c_ = _my_pos()
    me = 4 * x_ + 2 * y_ + c_

    tr = _Traffic(shards={name: cast_bf16(f"cast_{name}", w_loc[name]) for name in BIG})
    wt = {k: w_loc[k] for k in SMALL}
    shard_shapes = [w_loc[k].shape for k in SHARDED_SMALL]
    gathered = all_gather("gather_small", _pack([w_loc[k] for k in SHARDED_SMALL]), True)
    for k, a in zip(SHARDED_SMALL, _unpack(gathered, shard_shapes, lead=(N_DEV,))):
        a = jnp.moveaxis(a, 0, -2)
        wt[k] = a.reshape(a.shape[:-2] + (-1,))

    loss_part, grad_x, g_full = local_step(x[0], loss_target[0], wt, tr)
    loss = lax.psum(loss_part, ("x", "y", "c"))

    outs_g, outs_d, outs_m, outs_v = {}, {}, {}, {}
    for name in BIG:
        lands = {(l, part): a for (n_, l, part), a in tr.landed.items() if n_ == name}
        outs_g[name], outs_d[name], outs_m[name], outs_v[name] = adamw_big(
            f"adamw_{name}", w_loc[name], lands, m_loc[name], v_loc[name])

    full_shapes = [g_full[k].shape for k in SMALL]
    g_all = all_gather("gather_small_grads", _pack([g_full[k] for k in SMALL]), True)
    g_sum = _unpack(sum_blocks("sum_small_grads", g_all), full_shapes)
    g_small = {}
    for k, g in zip(SMALL, g_sum):
        if k in SHARDED_SMALL:
            w = w_loc[k].shape[-1]
            g = lax.dynamic_slice_in_dim(g, me * w, w, axis=g.ndim - 1)
        g_small[k] = g
    d_s, m_s, v_s = adamw_small("adamw_small", [w_loc[k] for k in SMALL], [g_small[k] for k in SMALL],
                                [m_loc[k] for k in SMALL], [v_loc[k] for k in SMALL])
    for k_i, k in enumerate(SMALL):
        outs_g[k], outs_d[k], outs_m[k], outs_v[k] = g_small[k], d_s[k_i], m_s[k_i], v_s[k_i]

    return (loss, grad_x[None], *[outs_g[k] for k in WEIGHTS], *[outs_d[k] for k in WEIGHTS],
            *[outs_m[k] for k in WEIGHTS], *[outs_v[k] for k in WEIGHTS])
```

```python
import functools
import math

import jax
import jax.numpy as jnp
from jax import lax
from jax.experimental import pallas as pl
from jax.experimental.pallas import tpu as pltpu

F32 = jnp.float32
BF16 = jnp.bfloat16
EPS = 1e-6
HALO = 8
VMEM_LIMIT = 56 * 1024 * 1024
MESH = pl.DeviceIdType.MESH
N_DEV = 8

ADAM_LR, ADAM_B1, ADAM_B2, ADAM_EPS, ADAM_WD, ADAM_STEP = 0.001, 0.9, 0.999, 1e-08, 0.01, 10

GDN_HEADS, GDN_DIM, GDN_CHUNK = 4, 128, 64
SGU_GROUPS, SGU_CHUNK = 4, 128
LRU_BLOCKS, LRU_C = 8, 8.0
D_G = 512
N_IN = 5640
N_INP = 5760
BA_COL = 5632

SHARDED_SMALL = ("gdn_conv_w", "lru_conv_w", "sconv_w", "grp_norm_w", "ffn_conv_w")
BIG = ("w_in", "w_out", "ffn_up", "ffn_down")
WEIGHTS = ("pre_mix_norm", "w_in", "gdn_conv_w", "gdn_a_log", "gdn_dt_bias", "gdn_norm_w", "lru_conv_w",
           "lru_conv_b", "lru_wa", "lru_ba", "lru_wx", "lru_bx", "lru_lambda", "sgu_ln_w", "sgu_ln_b", "sgu_ws",
           "sgu_b", "sconv_w", "grp_norm_w", "w_out", "post_mix_norm", "pre_ffn_norm", "ffn_up", "ffn_conv_w",
           "ffn_conv_b", "ffn_down", "post_ffn_norm")
SMALL = tuple(n for n in WEIGHTS if n not in BIG)


def _dot(a, b, ca, cb):
    return lax.dot_general(a.astype(BF16), b.astype(BF16), (((ca,), (cb,)), ((), ())),
                           preferred_element_type=F32)


@jax.custom_vjp
def _mm(a, b):
    return _dot(a, b, 1, 0)


def _mm_f(a, b):
    return _dot(a, b, 1, 0), (a, b)


def _mm_b(res, g):
    a, b = res
    return _dot(g, b, 1, 1), _dot(a, g, 0, 0)


_mm.defvjp(_mm_f, _mm_b)


@jax.custom_vjp
def _mm_nt(a, b):
    return _dot(a, b, 1, 1)


def _mm_nt_f(a, b):
    return _dot(a, b, 1, 1), (a, b)


def _mm_nt_b(res, g):
    a, b = res
    return _dot(g, b, 1, 0), _dot(g, a, 0, 0)


_mm_nt.defvjp(_mm_nt_f, _mm_nt_b)


@jax.custom_vjp
def _mm_tn(a, b):
    return _dot(a, b, 0, 0)


def _mm_tn_f(a, b):
    return _dot(a, b, 0, 0), (a, b)


def _mm_tn_b(res, g):
    a, b = res
    return _dot(b, g, 1, 1), _dot(a, g, 1, 0)


_mm_tn.defvjp(_mm_tn_f, _mm_tn_b)


def _dot_exact(a, b, ca, cb):
    return lax.dot_general(a, b, (((ca,), (cb,)), ((), ())), precision=lax.Precision.HIGHEST,
                           preferred_element_type=F32)


@functools.partial(jax.custom_vjp, nondiff_argnums=(1,))
def _shift_rows(x, s):
    return pltpu.roll(x, s, 0)


def _shift_rows_f(x, s):
    return pltpu.roll(x, s, 0), None


def _shift_rows_b(s, _, g):
    return (pltpu.roll(g, (g.shape[0] - s) % g.shape[0], 0),)


_shift_rows.defvjp(_shift_rows_f, _shift_rows_b)


def _sigmoid(x):
    return 1.0 / (1.0 + jnp.exp(-x))


def _silu(x):
    return x * _sigmoid(x)


GELU_C, GELU_A = 0.7978845608028654, 0.044715


@jax.custom_vjp
def _gelu(x):
    return 0.5 * x * (1.0 + jnp.tanh(GELU_C * (x + GELU_A * (x * x * x))))


def _gelu_f(x):
    t = jnp.tanh(GELU_C * (x + GELU_A * (x * x * x)))
    return 0.5 * x * (1.0 + t), (x, t)


def _gelu_b(res, g):
    x, t = res
    slope = 0.5 * (1.0 + t) + (0.5 * GELU_C) * x * (1.0 - t * t) * (1.0 + (3.0 * GELU_A) * (x * x))
    return (g * slope,)


_gelu.defvjp(_gelu_f, _gelu_b)


@jax.custom_vjp
def _softplus(x):
    e = jnp.exp(-jnp.abs(x))
    u = 1.0 + e
    log1p = jnp.where(u == 1.0, e, jnp.log(u) * (e / jnp.where(u == 1.0, 1.0, u - 1.0)))
    return jnp.maximum(x, 0.0) + log1p


def _softplus_f(x):
    return _softplus(x), x


def _softplus_b(x, g):
    return (g * _sigmoid(x),)


_softplus.defvjp(_softplus_f, _softplus_b)


def _neg_expm1(y):
    return -jnp.tanh(0.5 * y) * (jnp.exp(y) + 1.0)


def _rms(x, w):
    return x * lax.rsqrt(jnp.mean(x * x, axis=-1, keepdims=True) + EPS) * w


def _row(w, k):
    sel = lax.broadcasted_iota(jnp.int32, w.shape, 0) == k
    return jnp.sum(jnp.where(sel, w, 0.0), axis=0, keepdims=True)


def _col(x, j):
    sel = lax.broadcasted_iota(jnp.int32, x.shape, 1) == j
    return jnp.sum(jnp.where(sel, x, 0.0), axis=1, keepdims=True)


def _conv(x_ext, w, taps):
    acc = None
    for k in range(taps):
        s = taps - 1 - k
        t = (x_ext if s == 0 else _shift_rows(x_ext, s)) * _row(w, k)
        acc = t if acc is None else acc + t
    return acc[HALO:]


@jax.custom_vjp
def _scan(a, b, h0):
    n = a.shape[0]
    row = lax.broadcasted_iota(jnp.int32, a.shape, 0)
    s = 1
    while s < n:
        keep = row >= s
        a_sh = jnp.where(keep, pltpu.roll(a, s, 0), 1.0)
        b_sh = jnp.where(keep, pltpu.roll(b, s, 0), 0.0)
        b = a * b_sh + b
        a = a * a_sh
        s *= 2
    return b + a * h0


def _scan_f(a, b, h0):
    h = _scan(a, b, h0)
    return h, (a, h, h0)


def _scan_b(res, dh):
    a, h, h0 = res
    n = a.shape[0]
    row = lax.broadcasted_iota(jnp.int32, a.shape, 0)
    an = jnp.where(row < n - 1, pltpu.roll(a, n - 1, 0), 0.0)
    lam = dh
    s = 1
    while s < n:
        keep = row < n - s
        a_sh = jnp.where(keep, pltpu.roll(an, n - s, 0), 1.0)
        l_sh = jnp.where(keep, pltpu.roll(lam, n - s, 0), 0.0)
        lam = an * l_sh + lam
        an = an * a_sh
        s *= 2
    h_prev = jnp.where(row >= 1, pltpu.roll(h, 1, 0), h0)
    al = a * lam
    dh0 = jnp.sum(jnp.where(row == 0, al, 0.0), axis=0, keepdims=True)
    return lam * h_prev, lam, dh0


_scan.defvjp(_scan_f, _scan_b)


@jax.custom_vjp
def _unit_lower_inverses(ms):
    n = ms[0].shape[0]
    shape = ms[0].shape
    eye = (lax.broadcasted_iota(jnp.int32, shape, 0) == lax.broadcasted_iota(jnp.int32, shape, 1)).astype(F32)
    p = [-m for m in ms]
    t = [eye + a for a in p]
    steps = 1
    while 2 ** steps < n:
        p = [_mm(a, a) for a in p]
        t = [a + _mm(a, c) for a, c in zip(t, p)]
        steps += 1
    return t


def _unit_lower_inverses_f(ms):
    t = _unit_lower_inverses(ms)
    return t, t


def _unit_lower_inverses_b(t, dt):
    x = [_mm_nt(g, a) for g, a in zip(dt, t)]
    return ([-_mm_tn(a, c) for a, c in zip(t, x)],)


_unit_lower_inverses.defvjp(_unit_lower_inverses_f, _unit_lower_inverses_b)


def _last_row(x):
    sel = lax.broadcasted_iota(jnp.int32, x.shape, 0) == x.shape[0] - 1
    return jnp.sum(jnp.where(sel, x, 0.0), axis=0, keepdims=True)


def fn_norm(xs, st, ps):
    (x,), (w,) = xs, ps
    return [_rms(x, w).astype(BF16)], []


def fn_norm_keep(xs, st, ps):
    (x,), (w,) = xs, ps
    return [_rms(x, w).astype(BF16), x], []


def fn_res(xs, st, ps):
    (x, y), (w_post, w_next) = xs, ps
    x1 = x + _rms(y, w_post)
    return [x1, _rms(x1, w_next).astype(BF16)], []


def fn_res_last(xs, st, ps):
    (x, y), (w_post,) = xs, ps
    return [x + _rms(y, w_post)], []


def fn_gdn(xs, st, ps):
    qkv_ext, z, ba = xs
    (state,) = st
    cw, gp, nw = ps
    ts = z.shape[0]
    qkv = _silu(_conv(qkv_ext, cw, 4))
    beta_all = _sigmoid(ba)
    g_all = -jnp.exp(_row(gp, 0)) * _softplus(ba + _row(gp, 1))
    c_n = GDN_CHUNK
    ri = lax.broadcasted_iota(jnp.int32, (c_n, c_n), 0)
    ci = lax.broadcasted_iota(jnp.int32, (c_n, c_n), 1)
    causal, strict = ri >= ci, ri > ci
    tril = causal.astype(F32)
    lane = lax.broadcasted_iota(jnp.int32, (c_n, 128), 1)
    s_h = [state[GDN_DIM * h:GDN_DIM * (h + 1)] for h in range(GDN_HEADS)]
    n_c = ts // c_n
    pairs = [(c, h) for c in range(n_c) for h in range(GDN_HEADS)]
    every = lambda f, *lists: [f(*a) for a in zip(*lists)]

    def piece(c, h, base):
        return qkv[c * c_n:(c + 1) * c_n, base + GDN_DIM * h:base + GDN_DIM * (h + 1)]

    q = [piece(c, h, 0) for c, h in pairs]
    k = [piece(c, h, D_G) for c, h in pairs]
    v = [piece(c, h, 2 * D_G) for c, h in pairs]
    q = every(lambda t: t * lax.rsqrt(jnp.sum(t * t, axis=-1, keepdims=True) + EPS) * (GDN_DIM ** -0.5), q)
    k = every(lambda t: t * lax.rsqrt(jnp.sum(t * t, axis=-1, keepdims=True) + EPS), k)
    gcum_all = [_dot_exact(tril, g_all[c * c_n:(c + 1) * c_n], 1, 0) for c in range(n_c)]
    b = [_col(beta_all[c * c_n:(c + 1) * c_n], h) for c, h in pairs]
    gc = [_col(gcum_all[c], 4 + h) for c, h in pairs]
    gr = [_dot_exact((lane == 4 + h).astype(F32), gcum_all[c], 1, 1) for c, h in pairs]
    decay = every(lambda a, r: jnp.where(causal, jnp.exp(jnp.where(causal, a - r, 0.0)), 0.0), gc, gr)
    kb = every(lambda a, c: a * c, k, b)
    mk = every(lambda a, c, e: _mm_nt(jnp.concatenate([a, c], axis=0), e), kb, q, k)
    m = every(lambda a, dcy: jnp.where(strict, a[:c_n] * dcy, 0.0), mk, decay)
    attn = every(lambda a, dcy: jnp.where(causal, a[c_n:] * dcy, 0.0), mk, decay)
    t_ = _unit_lower_inverses(m)
    eg = every(jnp.exp, gc)
    wu = every(lambda t, a, e, c, d: _mm(t, jnp.concatenate([a * e, c * d], axis=1)), t_, kb, eg, v, b)
    g_last = every(_last_row, gc)
    k_g = every(lambda a, gl, g: a * jnp.exp(gl - g), k, g_last, gc)
    wq = every(lambda a, c, e: jnp.concatenate([a[:, :GDN_DIM], c * e], axis=0), wu, q, eg)
    u = [a[:, GDN_DIM:] for a in wu]
    gl = every(jnp.exp, g_last)

    o = []
    for c in range(n_c):
        idx = range(c * GDN_HEADS, (c + 1) * GDN_HEADS)
        ws = [_mm(wq[i], s_h[h]) for h, i in enumerate(idx)]
        v_new = [u[i] - ws[h][:c_n] for h, i in enumerate(idx)]
        av = [_mm(attn[i], v_new[h]) for h, i in enumerate(idx)]
        kv = [_mm_tn(k_g[i], v_new[h]) for h, i in enumerate(idx)]
        o += [ws[h][c_n:] + av[h] for h in range(GDN_HEADS)]
        s_h = [s_h[h] * gl[i] + kv[h] for h, i in enumerate(idx)]
    zz = [z[c * c_n:(c + 1) * c_n, GDN_DIM * h:GDN_DIM * (h + 1)] for c, h in pairs]
    y = every(lambda a, g: a * lax.rsqrt(jnp.mean(a * a, axis=-1, keepdims=True) + EPS) * nw * _silu(g), o, zz)
    rows = [jnp.concatenate(y[c * GDN_HEADS:(c + 1) * GDN_HEADS], axis=1) for c in range(n_c)]
    y = rows[0] if n_c == 1 else jnp.concatenate(rows, axis=0)
    return [y.astype(BF16)], [jnp.concatenate(s_h, axis=0)]


def fn_lru(xs, st, ps):
    x_ext, gate = xs
    (h0,) = st
    cw, cb, wa, ba, wx, bx, lam, gw = ps
    xc = _conv(x_ext, cw, 4) + cb
    r = _sigmoid(_mm(xc, wa) + ba)
    i = _sigmoid(_mm(xc, wx) + bx)
    log_a = -LRU_C * r * _softplus(-lam)
    a = jnp.exp(log_a)
    mult = jnp.sqrt(_neg_expm1(2.0 * log_a))
    h = _scan(a, mult * (i * xc), h0)
    y = _rms(h * _gelu(gate), gw)
    return [y.astype(BF16)], [_last_row(h)]


def fn_sgu(xs, st, ps):
    (uv,) = xs
    lnw, lnb, ws, bst, gw = ps
    ts = uv.shape[0]
    uvf = _gelu(uv)
    u, v = uvf[:, :D_G], uvf[:, D_G:]
    vc = v - jnp.mean(v, axis=-1, keepdims=True)
    v = vc * lax.rsqrt(jnp.mean(vc * vc, axis=-1, keepdims=True) + EPS) * lnw + lnb
    t_n = SGU_CHUNK
    tril = lax.broadcasted_iota(jnp.int32, (t_n, t_n), 0) >= lax.broadcasted_iota(jnp.int32, (t_n, t_n), 1)
    wg = [jnp.where(tril, ws[t_n * g:t_n * (g + 1)], 0.0) for g in range(SGU_GROUPS)]
    bg = [_col(bst, g) for g in range(SGU_GROUPS)]
    rows = []
    for c in range(ts // t_n):
        vcg = v[c * t_n:(c + 1) * t_n]
        rows.append(jnp.concatenate(
            [_mm(wg[g], vcg[:, 128 * g:128 * (g + 1)]) + bg[g] for g in range(SGU_GROUPS)], axis=1))
    vv = rows[0] if len(rows) == 1 else jnp.concatenate(rows, axis=0)
    return [_rms(u * vv, gw).astype(BF16)], []


def fn_sconv(xs, st, ps):
    bg, cg_ext, hh_ext = xs
    cw, gw = ps
    return [_rms(bg * _conv(cg_ext * hh_ext, cw, 3), gw).astype(BF16)], []


def fn_ffn(xs, st, ps):
    g_ext, v_ext = xs
    cwg, cwv, cbg, cbv = ps
    g = _conv(g_ext, cwg, 3) + cbg
    v = _conv(v_ext, cwv, 3) + cbv
    return [(_gelu(g) * v).astype(BF16)], []


def _my_pos():
    return lax.axis_index("x"), lax.axis_index("y"), lax.axis_index("c")


def _peer(pos, k):
    x_, y_, c_ = pos
    return (1 - x_ if (k >> 2) & 1 else x_, 1 - y_ if (k >> 1) & 1 else y_, 1 - c_ if k & 1 else c_)


def _dev_index(p):
    return 4 * p[0] + 2 * p[1] + p[2]


class _Side:
    def __init__(self, jobs):
        self.jobs = list(jobs)
        n = len(self.jobs)
        self.operands = [a for _, a in self.jobs]
        self.in_specs = [pl.BlockSpec(memory_space=pl.ANY)] * n
        self.out_shape = [jax.ShapeDtypeStruct(((N_DEV,) + a.shape) if kind == "gather" else a.shape, a.dtype)
                          for kind, a in self.jobs]
        self.out_specs = [pl.BlockSpec(memory_space=pl.ANY)] * n
        self.scratch = [pltpu.SemaphoreType.DMA((7 * n,)), pltpu.SemaphoreType.DMA((7 * n,)),
                        pltpu.SemaphoreType.DMA((n,))] if n else []

    def _copies(self, in_refs, out_refs, sems, landings=True):
        send, recv, local = sems
        pos = _my_pos()
        me = _dev_index(pos)
        mine, outgoing, landing = [], [], []
        for j, (kind, _) in enumerate(self.jobs):
            src, dst = in_refs[j], out_refs[j]
            own = src if kind == "gather" else src.at[me]
            mine.append(pltpu.make_async_copy(own, dst.at[me], local.at[j]))
            for k in range(1, N_DEV):
                p = _peer(pos, k)
                sems_k = dict(send_sem=send.at[7 * j + k - 1], recv_sem=recv.at[7 * j + k - 1], device_id=p,
                              device_id_type=MESH)
                outgoing.append(pltpu.make_async_remote_copy(
                    src_ref=src if kind == "gather" else src.at[_dev_index(p)], dst_ref=dst.at[me], **sems_k))
                if landings:
                    landing.append(pltpu.make_async_remote_copy(src_ref=own, dst_ref=dst.at[_dev_index(p)], **sems_k))
        return mine, outgoing, landing

    def start(self, in_refs, out_refs, sems):
        mine, outgoing, _ = self._copies(in_refs, out_refs, sems, landings=False)
        for cp in mine + outgoing:
            cp.start()

    def wait(self, in_refs, out_refs, sems):
        mine, outgoing, landing = self._copies(in_refs, out_refs, sems)
        for cp in landing:
            cp.wait_recv()
        for cp in outgoing:
            cp.wait_send()
        for cp in mine:
            cp.wait()


def _rin(arr, w=None, col=0, halo=False):
    return dict(arr=arr, w=arr.shape[1] if w is None else w, col=col, halo=halo)


def _par(arr, w=None, col=None, row=None):
    return dict(arr=arr, w=w, col=col, row=row)


def _colidx(col, j):
    return col(j) if callable(col) else col


def _block_call(name, body, n_rows, ts, ncol, reverse, row_ins, blk_ins, params, row_outs, blk_outs, acc_outs,
                carries, side=()):
    side = _Side(side)
    ts = min(ts, n_rows)
    nblk = n_rows // ts
    hb = ts // HALO

    def rr(i):
        return (nblk - 1 - i) if reverse else i

    in_specs, operands = [], []
    for s in row_ins:
        in_specs.append(pl.BlockSpec((ts, s["w"]), lambda j, i, s=s: (rr(i), _colidx(s["col"], j))))
        operands.append(s["arr"])
        if s["halo"]:
            in_specs.append(pl.BlockSpec((HALO, s["w"]),
                                         lambda j, i, s=s: (jnp.maximum(rr(i) * hb - 1, 0), _colidx(s["col"], j))))
            operands.append(s["arr"])
    for a in blk_ins:
        nd = a.ndim - 1
        in_specs.append(pl.BlockSpec((None,) + a.shape[1:], lambda j, i, nd=nd: (rr(i),) + (0,) * nd))
        operands.append(a)
    for p in params:
        a = p["arr"]
        if p["row"] is not None:
            in_specs.append(pl.BlockSpec((p["w"], a.shape[1]), lambda j, i, p=p: (_colidx(p["row"], j), 0)))
        elif p["col"] is None:
            in_specs.append(pl.BlockSpec(a.shape, lambda j, i: (0, 0)))
        else:
            in_specs.append(pl.BlockSpec((a.shape[0], p["w"]), lambda j, i, p=p: (0, _colidx(p["col"], j))))
        operands.append(a)

    out_specs, out_shape = [], []
    for o in row_outs:
        out_specs.append(pl.BlockSpec((ts, o["w"]), lambda j, i, o=o: (rr(i), _colidx(o["col"], j))))
        out_shape.append(jax.ShapeDtypeStruct((n_rows, o["total"]), o["dtype"]))
    for o in blk_outs:
        nd = len(o["shape"])
        out_specs.append(pl.BlockSpec((None,) + tuple(o["shape"]), lambda j, i, nd=nd: (rr(i),) + (0,) * nd))
        out_shape.append(jax.ShapeDtypeStruct((nblk,) + tuple(o["shape"]), o["dtype"]))
    for o in acc_outs:
        if o["col"] is None:
            out_specs.append(pl.BlockSpec(o["shape"], lambda j, i: (0, 0)))
            out_shape.append(jax.ShapeDtypeStruct(o["shape"], F32))
        else:
            out_specs.append(pl.BlockSpec(o["shape"], lambda j, i, o=o: (0, _colidx(o["col"], j))))
            out_shape.append(jax.ShapeDtypeStruct((o["shape"][0], o["total"]), F32))

    n_in = len(operands)
    n_row_out, n_blk_out, n_acc = len(row_outs), len(blk_outs), len(acc_outs)
    n_out = n_row_out + n_blk_out + n_acc
    n_side = len(side.jobs)

    def kern(*refs):
        in_refs = refs[:n_in]
        side_in = refs[n_in:n_in + n_side]
        out_refs = refs[n_in + n_side:n_in + n_side + n_out]
        side_out = refs[n_in + n_side + n_out:n_in + 2 * n_side + n_out]
        scratch = refs[n_in + 2 * n_side + n_out:]
        carry_refs, side_sems = scratch[:len(carries)], scratch[len(carries):]
        acc_refs = out_refs[n_row_out + n_blk_out:]
        i = pl.program_id(1)
        r = rr(i)
        if n_side:
            @pl.when((pl.program_id(0) == 0) & (i == 0))
            def _():
                side.start(side_in, side_out, side_sems)

        @pl.when(i == 0)
        def _():
            for c_ref in carry_refs:
                c_ref[...] = jnp.zeros(c_ref.shape, c_ref.dtype)
            for a_ref in acc_refs:
                a_ref[...] = jnp.zeros(a_ref.shape, a_ref.dtype)

        k = 0
        xs = []
        for s in row_ins:
            x = in_refs[k][...]
            k += 1
            if s["halo"]:
                hal = in_refs[k][...]
                k += 1
                hal = jnp.where(r == 0, jnp.zeros_like(hal), hal)
                x = jnp.concatenate([hal, x], axis=0)
            xs.append(x)
        blks = []
        for _ in blk_ins:
            blks.append(in_refs[k][...])
            k += 1
        ps = []
        for _ in params:
            ps.append(in_refs[k][...])
            k += 1
        row_vals, blk_vals, acc_vals, new_carries = body(xs, blks, ps, [c[...] for c in carry_refs], r)
        for ref, val in zip(out_refs[:n_row_out], row_vals):
            ref[...] = val.astype(ref.dtype)
        for ref, val in zip(out_refs[n_row_out:n_row_out + n_blk_out], blk_vals):
            ref[...] = val.astype(ref.dtype)
        for ref, val in zip(acc_refs, acc_vals):
            ref[...] += val
        for ref, val in zip(carry_refs, new_carries):
            ref[...] = val
        if n_side:
            @pl.when((pl.program_id(0) == ncol - 1) & (i == nblk - 1))
            def _():
                side.wait(side_in, side_out, side_sems)

    res = pl.pallas_call(
        kern,
        name=name,
        grid=(ncol, nblk),
        in_specs=in_specs + side.in_specs,
        out_specs=out_specs + side.out_specs,
        out_shape=out_shape + side.out_shape,
        scratch_shapes=[pltpu.VMEM(shape, F32) for shape in carries] + side.scratch,
        compiler_params=pltpu.CompilerParams(dimension_semantics=("arbitrary", "arbitrary"),
                                             vmem_limit_bytes=VMEM_LIMIT),
    )(*operands, *side.operands)
    return list(res)


def _out(w, dtype, total=None, col=0):
    return dict(w=w, dtype=dtype, total=w if total is None else total, col=col)


def seq_fwd(name, fn, n_rows, ts, row_ins, params, outs, state_shapes=(), ncol=1, side=()):
    def body(xs, blks, ps, carries, r):
        o, new_st = fn(xs, list(carries), ps)
        return o, list(carries), [], new_st

    res = _block_call(name, body, n_rows, ts, ncol, False, row_ins, [], params, outs,
                      [dict(shape=s, dtype=F32) for s in state_shapes], [], list(state_shapes), side)
    n_o, n_s = len(outs), len(state_shapes)
    return (res[:n_o], res[n_o:n_o + n_s]) + ((res[n_o + n_s:],) if side else ())


def seq_bwd(name, fn, n_rows, ts, row_ins, params, cots, saved_states=(), din_dtypes=None, ncol=1, din_specs=None,
            side=(), cot_map=None, aux=()):
    n_x, n_p, n_st = len(row_ins), len(params), len(saved_states)
    halo_idx = [k for k, s in enumerate(row_ins) if s["halo"]]
    state_shapes = [a.shape[1:] for a in saved_states]

    def body(xs_all, blks, ps, carries, r):
        xs, cot_vals = xs_all[:n_x], xs_all[n_x:]
        d_state, d_halo = carries[:n_st], carries[n_st:]
        if cot_map is not None:
            cot_vals = cot_map(cot_vals, ps[n_p:])
        (o, _), vjp = jax.vjp(lambda a, b, c: fn(a, b, c), xs, blks, ps[:n_p])
        cot = [c.astype(v.dtype) for c, v in zip(cot_vals, o)]
        dxs, dst, dps = vjp((cot, list(d_state)))
        row_vals, new_halo = [], []
        for k, dx in enumerate(dxs):
            if k in halo_idx:
                hk = halo_idx.index(k)
                rows = dx.shape[0] - HALO
                tail = dx[rows:] + d_halo[hk]
                row_vals.append(jnp.concatenate([dx[HALO:rows], tail], axis=0))
                new_halo.append(dx[:HALO])
            else:
                row_vals.append(dx)
        return row_vals, [], list(dps), list(dst) + new_halo

    din_dtypes = din_dtypes or [F32] * n_x
    douts = []
    for k, s in enumerate(row_ins):
        total, col = (s["w"], 0) if din_specs is None or din_specs[k] is None else din_specs[k]
        douts.append(_out(s["w"], din_dtypes[k], total, col))
    accs = []
    for p in params:
        a = p["arr"]
        if p["col"] is None:
            accs.append(dict(shape=a.shape, total=None, col=None))
        else:
            accs.append(dict(shape=(a.shape[0], p["w"]), total=a.shape[1], col=p["col"]))
    carries = list(state_shapes) + [(HALO, row_ins[k]["w"]) for k in halo_idx]
    res = _block_call(name, body, n_rows, ts, ncol, True, list(row_ins) + list(cots), list(saved_states),
                      list(params) + list(aux), douts, [], accs, carries, side)
    return (res[:n_x], res[n_x:n_x + n_p]) + ((res[n_x + n_p:],) if side else ())


def matmul(name, a, b, mode, out_dtype, tm, tn, tk, side=(), a2=None, b2=None, col_blocks=False):
    side = _Side(side)
    n_side = len(side.jobs)
    if mode == "tn":
        (kk, m), n = a.shape, b.shape[1]
    else:
        (m, kk), n = a.shape, (b.shape[0] if mode == "nt" else b.shape[1])
    k1, n1 = kk, n
    if a2 is not None:
        assert mode != "tn" and b2 is None
        kk += a2.shape[1]
    if b2 is not None:
        assert mode == "tn"
        n += b2.shape[1]
    tm, tn, tk = min(tm, m), min(tn, n), min(tk, kk)
    nk, gm, gn = kk // tk, m // tm, n // tn
    assert m % tm == 0 and n % tn == 0 and kk % tk == 0 and k1 % tk == 0 and n1 % tn == 0, (name, a.shape, b.shape)
    nk1, gn1 = k1 // tk, n1 // tn
    if mode == "tn":
        a_specs = [pl.BlockSpec((tk, tm), lambda i, j, k: (k, i))]
        b_specs = [pl.BlockSpec((tk, tn), lambda i, j, k: (k, jnp.minimum(j, gn1 - 1)))]
        if b2 is not None:
            b_specs.append(pl.BlockSpec((tk, tn), lambda i, j, k: (k, jnp.maximum(j - gn1, 0))))
    else:
        a_specs = [pl.BlockSpec((tm, tk), lambda i, j, k: (i, jnp.minimum(k, nk1 - 1)))]
        if a2 is not None:
            a_specs.append(pl.BlockSpec((tm, tk), lambda i, j, k: (i, jnp.maximum(k - nk1, 0))))
        b_specs = [pl.BlockSpec((tn, tk), lambda i, j, k: (j, k)) if mode == "nt"
                   else pl.BlockSpec((tk, tn), lambda i, j, k: (k, j))]
    ca, cb = {"nn": (1, 0), "nt": (1, 1), "tn": (0, 0)}[mode]
    n_a, n_b = len(a_specs), len(b_specs)

    def kern(*refs):
        a_refs, b_refs = refs[:n_a], refs[n_a:n_a + n_b]
        rest = refs[n_a + n_b:]
        side_in = rest[:n_side]
        o_ref = rest[n_side]
        side_out = rest[1 + n_side:1 + 2 * n_side]
        acc_ref = rest[1 + 2 * n_side]
        side_sems = rest[2 + 2 * n_side:]
        i, j, k = pl.program_id(0), pl.program_id(1), pl.program_id(2)
        if n_side:
            @pl.when((i == 0) & (j == 0) & (k == 0))
            def _():
                side.start(side_in, side_out, side_sems)

        def step(a_ref, b_ref):
            part = lax.dot_general(a_ref[...], b_ref[...], (((ca,), (cb,)), ((), ())), preferred_element_type=F32)
            if nk == 1:
                o_ref[...] = part.astype(o_ref.dtype)
            else:
                @pl.when(k == 0)
                def _():
                    acc_ref[...] = part

                @pl.when(k > 0)
                def _():
                    acc_ref[...] += part

                @pl.when(k == nk - 1)
                def _():
                    o_ref[...] = acc_ref[...].astype(o_ref.dtype)

        if n_a == 2:
            pl.when(k < nk1)(lambda: step(a_refs[0], b_refs[0]))
            pl.when(k >= nk1)(lambda: step(a_refs[1], b_refs[0]))
        elif n_b == 2:
            pl.when(j < gn1)(lambda: step(a_refs[0], b_refs[0]))
            pl.when(j >= gn1)(lambda: step(a_refs[0], b_refs[1]))
        else:
            step(a_refs[0], b_refs[0])

        if n_side:
            @pl.when((i == gm - 1) & (j == gn - 1) & (k == nk - 1))
            def _():
                side.wait(side_in, side_out, side_sems)

    semantics = ("arbitrary",) * 3 if n_side else ("parallel", "parallel", "arbitrary")
    operands = [a] + ([a2] if a2 is not None else []) + [b] + ([b2] if b2 is not None else [])
    res = pl.pallas_call(
        kern,
        name=name,
        grid=(gm, gn, nk),
        in_specs=a_specs + b_specs + side.in_specs,
        out_specs=[pl.BlockSpec((None, tm, tn), lambda i, j, k: (j, i, 0)) if col_blocks
                   else pl.BlockSpec((tm, tn), lambda i, j, k: (i, j))] + side.out_specs,
        out_shape=[jax.ShapeDtypeStruct((gn, m, tn) if col_blocks else (m, n), out_dtype)] + side.out_shape,
        scratch_shapes=[pltpu.VMEM((tm, tn) if nk > 1 else (8, 128), F32)] + side.scratch,
        compiler_params=pltpu.CompilerParams(dimension_semantics=semantics, vmem_limit_bytes=VMEM_LIMIT),
    )(*operands, *side.operands)
    return (res[0], list(res[1:])) if n_side else res[0]


def all_gather(name, x, in_vmem):
    def body(x_ref, out_ref, send_sems, recv_sems, local_sem):
        x_, y_, c_ = _my_pos()
        me, sibling = (x_, y_, c_), (x_, y_, 1 - c_)
        chips = [(1 - x_, y_), (x_, 1 - y_), (1 - x_, 1 - y_)]

        def slot(px, py, pc):
            return out_ref.at[4 * px + 2 * py + pc]

        def copy(k, block, to, src=None):
            return pltpu.make_async_remote_copy(
                src_ref=slot(*block) if src is None else src, dst_ref=slot(*block),
                send_sem=send_sems.at[k], recv_sem=recv_sems.at[k], device_id=to, device_id_type=MESH)

        mine = pltpu.make_async_copy(x_ref, slot(*me), local_sem)
        mine.start()
        first = [copy(0, me, sibling, src=x_ref)]
        first += [copy(1 + j, me, (*chip, c_), src=x_ref) for j, chip in enumerate(chips)]
        for cp in first:
            cp.start()
        passed = [copy(4 + j, (*chip, c_), sibling) for j, chip in enumerate(chips)]
        for j, chip in enumerate(chips):
            copy(1 + j, (*chip, c_), me).wait_recv()
            passed[j].start()
        copy(0, sibling, me).wait_recv()
        for j, chip in enumerate(chips):
            copy(4 + j, (*chip, 1 - c_), me).wait_recv()
        for cp in first + passed:
            cp.wait_send()
        mine.wait()

    space = pltpu.VMEM if in_vmem else pl.ANY
    return pl.pallas_call(
        body,
        name=name,
        out_shape=jax.ShapeDtypeStruct((N_DEV,) + x.shape, x.dtype),
        in_specs=[pl.BlockSpec(memory_space=space)],
        out_specs=pl.BlockSpec(memory_space=space),
        scratch_shapes=[pltpu.SemaphoreType.DMA((7,)), pltpu.SemaphoreType.DMA((7,)), pltpu.SemaphoreType.DMA],
        compiler_params=pltpu.CompilerParams(vmem_limit_bytes=VMEM_LIMIT),
    )(x)


def all_to_all(name, g):
    def body(g_ref, out_ref, send_sems, recv_sems, local_sem):
        x_, y_, c_ = _my_pos()
        me = 4 * x_ + 2 * y_ + c_

        def peer(k):
            fx, fy, fc = (k >> 2) & 1, (k >> 1) & 1, k & 1
            return (1 - x_ if fx else x_, 1 - y_ if fy else y_, 1 - c_ if fc else c_)

        def copy(k):
            px, py, pc = peer(k)
            return pltpu.make_async_remote_copy(
                src_ref=g_ref.at[4 * px + 2 * py + pc], dst_ref=out_ref.at[me],
                send_sem=send_sems.at[k - 1], recv_sem=recv_sems.at[k - 1], device_id=(px, py, pc), device_id_type=MESH)

        def landing(k):
            px, py, pc = peer(k)
            return pltpu.make_async_remote_copy(
                src_ref=g_ref.at[me], dst_ref=out_ref.at[4 * px + 2 * py + pc],
                send_sem=send_sems.at[k - 1], recv_sem=recv_sems.at[k - 1], device_id=(px, py, pc), device_id_type=MESH)

        mine = pltpu.make_async_copy(g_ref.at[me], out_ref.at[me], local_sem)
        mine.start()
        sends = [copy(k) for k in range(1, N_DEV)]
        for cp in sends:
            cp.start()
        for k in range(1, N_DEV):
            landing(k).wait_recv()
        for cp in sends:
            cp.wait_send()
        mine.wait()

    return pl.pallas_call(
        body,
        name=name,
        out_shape=jax.ShapeDtypeStruct(g.shape, g.dtype),
        in_specs=[pl.BlockSpec(memory_space=pl.ANY)],
        out_specs=pl.BlockSpec(memory_space=pl.ANY),
        scratch_shapes=[pltpu.SemaphoreType.DMA((7,)), pltpu.SemaphoreType.DMA((7,)), pltpu.SemaphoreType.DMA],
    )(g)


def sum_blocks(name, g):
    def body(g_ref, o_ref):
        acc = g_ref[0]
        for s in range(1, N_DEV):
            acc = acc + g_ref[s]
        o_ref[...] = acc

    r = g.shape[1]
    tr = r // 4 if r % 32 == 0 else r
    return pl.pallas_call(
        body, name=name, grid=(r // tr,),
        in_specs=[pl.BlockSpec((N_DEV, tr, 128), lambda i: (0, i, 0))],
        out_specs=pl.BlockSpec((tr, 128), lambda i: (i, 0)),
        out_shape=jax.ShapeDtypeStruct((r, 128), F32),
        compiler_params=pltpu.CompilerParams(vmem_limit_bytes=VMEM_LIMIT),
    )(g)


def _adamw_math(w, g, m, v):
    m = ADAM_B1 * m + (1.0 - ADAM_B1) * g
    v = ADAM_B2 * v + (1.0 - ADAM_B2) * (g * g)
    m_hat = m / (1.0 - ADAM_B1 ** ADAM_STEP)
    v_hat = v / (1.0 - ADAM_B2 ** ADAM_STEP)
    delta = -ADAM_LR * (m_hat / (jnp.sqrt(v_hat) + ADAM_EPS) + ADAM_WD * w)
    return delta, m, v


ADAMW_BLOCK_BYTES = 3 << 19


def _row_tile(rows, row_bytes, limit):
    best = 8
    for t in range(8, rows + 1, 8):
        if rows % t == 0 and t * row_bytes <= limit:
            best = t
    return best


def adamw_big(name, w, lands, m, v):
    depth, r, c = w.shape
    outs = None
    for (l, part), land in sorted(lands.items()):
        rows = land.shape[1]
        tr = _row_tile(rows, 4 * (-(-c // 128) * 128), ADAMW_BLOCK_BYTES)
        first = part * rows // tr

        def body(w_ref, l_ref, m_ref, v_ref, *rest):
            g_out, d_out, m_out, v_out = rest[-4:]
            g = l_ref[0].astype(F32)
            for s in range(1, N_DEV):
                g = g + l_ref[s].astype(F32)
            delta, m_new, v_new = _adamw_math(w_ref[...], g, m_ref[...], v_ref[...])
            g_out[...] = g
            d_out[...] = delta
            m_out[...] = m_new
            v_out[...] = v_new

        spec = pl.BlockSpec((None, tr, c), lambda i, l=l, first=first: (l, first + i, 0))
        carried = [] if outs is None else list(outs)
        outs = pl.pallas_call(
            body, name=f"{name}_{l}_{part}", grid=(rows // tr,),
            in_specs=[spec, pl.BlockSpec((N_DEV, tr, c), lambda i: (0, i, 0)), spec, spec]
            + [pl.BlockSpec(memory_space=pl.ANY)] * len(carried),
            out_specs=[spec] * 4,
            out_shape=[jax.ShapeDtypeStruct((depth, r, c), F32)] * 4,
            input_output_aliases={4 + k: k for k in range(len(carried))},
            compiler_params=pltpu.CompilerParams(dimension_semantics=("parallel",), vmem_limit_bytes=VMEM_LIMIT),
        )(w, land, m, v, *carried)
    return outs


def adamw_small(name, ws, gs, ms, vs):
    n = len(ws)

    def body(*refs):
        ins, outs = refs[:4 * n], refs[4 * n:]
        for k in range(n):
            delta, m_new, v_new = _adamw_math(ins[k][...], ins[n + k][...], ins[2 * n + k][...], ins[3 * n + k][...])
            outs[k][...] = delta
            outs[n + k][...] = m_new
            outs[2 * n + k][...] = v_new

    res = pl.pallas_call(
        body, name=name,
        out_shape=[jax.ShapeDtypeStruct(w.shape, F32) for w in ws] * 3,
        compiler_params=pltpu.CompilerParams(vmem_limit_bytes=VMEM_LIMIT),
    )(*ws, *gs, *ms, *vs)
    return res[:n], res[n:2 * n], res[2 * n:]


def cast_bf16(name, w):
    depth, r, c = w.shape
    tr = _row_tile(r, 4 * (-(-c // 128) * 128), ADAMW_BLOCK_BYTES)

    def body(w_ref, o_ref):
        o_ref[...] = w_ref[...].astype(BF16)

    spec = pl.BlockSpec((None, tr, c), lambda l, i: (l, i, 0))
    return pl.pallas_call(body, name=name, grid=(depth, r // tr), in_specs=[spec], out_specs=spec,
                          out_shape=jax.ShapeDtypeStruct((depth, r, c), BF16),
                          compiler_params=pltpu.CompilerParams(dimension_semantics=("parallel", "parallel")))(w)


def _rows_of(shape):
    return -(-math.prod(shape) // 128)


def _pack(arrs):
    pieces = []
    for a in arrs:
        flat = a.reshape(-1).astype(F32)
        pieces.append(jnp.pad(flat, (0, (-flat.shape[0]) % 128)).reshape(-1, 128))
    rows = sum(p.shape[0] for p in pieces)
    if rows % 8:
        pieces.append(jnp.zeros((8 - rows % 8, 128), F32))
    return jnp.concatenate(pieces, axis=0)


def _unpack(packed, shapes, lead=()):
    out, r0 = [], 0
    for s in shapes:
        rows, n = _rows_of(s), math.prod(s)
        piece = packed[..., r0:r0 + rows, :].reshape(lead + (rows * 128,))
        out.append(piece[..., :n].reshape(lead + tuple(s)))
        r0 += rows
    return out


def _block_diag(w):
    h, d, _ = w.shape
    eye = jnp.eye(h, dtype=w.dtype)
    return (eye[:, None, :, None] * w[:, :, None, :]).reshape(h * d, h * d)


def _block_diag_grad(g, h):
    d = g.shape[0] // h
    eye = jnp.eye(h, dtype=g.dtype)
    return jnp.sum(g.reshape(h, d, h, d) * eye[:, None, :, None], axis=2)


def _layer_params(wt, l):
    gp = jnp.pad(jnp.stack([wt["gdn_a_log"][l], wt["gdn_dt_bias"][l]]), ((0, 6), (4, 128 - 4 - GDN_HEADS)))
    d_ffh = wt["ffn_conv_w"].shape[-1] // 2
    return dict(
        pre_mix=wt["pre_mix_norm"][l][None], post_mix=wt["post_mix_norm"][l][None],
        pre_ffn=wt["pre_ffn_norm"][l][None], post_ffn=wt["post_ffn_norm"][l][None],
        gdn_cw=wt["gdn_conv_w"][l], gdn_gp=gp, gdn_nw=wt["gdn_norm_w"][l][None],
        lru_cw=wt["lru_conv_w"][l], lru_cb=wt["lru_conv_b"][l][None],
        lru_wa=_block_diag(wt["lru_wa"][l]), lru_ba=wt["lru_ba"][l].reshape(1, -1),
        lru_wx=_block_diag(wt["lru_wx"][l]), lru_bx=wt["lru_bx"][l].reshape(1, -1),
        lru_lam=wt["lru_lambda"][l][None], gw0=wt["grp_norm_w"][l, 0][None], gw1=wt["grp_norm_w"][l, 1][None],
        gw2=wt["grp_norm_w"][l, 2][None],
        sgu_lnw=wt["sgu_ln_w"][l][None], sgu_lnb=wt["sgu_ln_b"][l][None],
        sgu_ws=wt["sgu_ws"][l].reshape(SGU_GROUPS * SGU_CHUNK, SGU_CHUNK),
        sgu_bt=jnp.pad(wt["sgu_b"][l].T, ((0, 0), (0, 128 - SGU_GROUPS))),
        sc_cw=wt["sconv_w"][l],
        ffn_cw=wt["ffn_conv_w"][l], ffn_cb=wt["ffn_conv_b"][l][None], d_ffh=d_ffh,
    )


TS_ROW = 256
TS_GDN = 256
TS_FFN = 512
TS_FFN_BWD = 256
TC_FFN = 512


def _mixers_fwd(l, p, lp, n, side, side_lru):
    qkv = _rin(p, 3 * D_G, 0, halo=True)
    z = _rin(p, D_G, 3)
    ba = _rin(p, 128, BA_COL // 128)
    gdn_ps = [_par(lp["gdn_cw"]), _par(lp["gdn_gp"]), _par(lp["gdn_nw"])]
    res = seq_fwd(f"gdn_fwd_{l}", fn_gdn, n, TS_GDN, [qkv, z, ba], gdn_ps, [_out(D_G, BF16)],
                  state_shapes=[(GDN_HEADS * GDN_DIM, GDN_DIM)], side=side)
    (y_a,), (gdn_st,), side_res = res if side else res + ([],)
    lru_x = _rin(p, D_G, 4, halo=True)
    lru_gate = _rin(p, D_G, 5)
    lru_ps = [_par(lp[k]) for k in ("lru_cw", "lru_cb", "lru_wa", "lru_ba", "lru_wx", "lru_bx", "lru_lam", "gw0")]
    res = seq_fwd(f"lru_fwd_{l}", fn_lru, n, TS_ROW, [lru_x, lru_gate], lru_ps, [_out(D_G, BF16)],
                  state_shapes=[(1, D_G)], side=side_lru)
    (y_b,), (lru_st,), side_res_lru = res if side_lru else res + ([],)
    uv = _rin(p, 2 * D_G, 3)
    sgu_ps = [_par(lp[k]) for k in ("sgu_lnw", "sgu_lnb", "sgu_ws", "sgu_bt", "gw1")]
    (y_c,), _ = seq_fwd(f"sgu_fwd_{l}", fn_sgu, n, TS_ROW, [uv], sgu_ps, [_out(D_G, BF16)])
    sc = [_rin(p, D_G, 8), _rin(p, D_G, 9, halo=True), _rin(p, D_G, 10, halo=True)]
    sc_ps = [_par(lp["sc_cw"]), _par(lp["gw2"])]
    (y_d,), _ = seq_fwd(f"sconv_fwd_{l}", fn_sconv, n, TS_ROW, sc, sc_ps, [_out(D_G, BF16)])
    ins = dict(gdn=([qkv, z, ba], gdn_ps, [gdn_st]), lru=([lru_x, lru_gate], lru_ps, [lru_st]),
               sgu=([uv], sgu_ps, []), sc=(sc, sc_ps, []))
    return jnp.concatenate([y_a, y_b, y_c, y_d], axis=1), ins, side_res, side_res_lru


def _mixers_bwd(l, dymix, ins, n, side):
    cot = lambda g: [_rin(dymix, D_G, g)]
    xs, ps, st = ins["gdn"]
    res = seq_bwd(f"gdn_bwd_{l}", fn_gdn, n, TS_GDN, xs, ps, cot(0), st, [BF16, BF16, BF16], side=side)
    (dqkv, dz, dba), g_gdn, side_res = res if side else res + ([],)
    xs, ps, st = ins["lru"]
    (dlx, dlg), g_lru = seq_bwd(f"lru_bwd_{l}", fn_lru, n, TS_ROW, xs, ps, cot(1), st, [BF16, BF16])
    xs, ps, st = ins["sgu"]
    (duv,), g_sgu = seq_bwd(f"sgu_bwd_{l}", fn_sgu, n, TS_ROW, xs, ps, cot(2), st, [BF16])
    xs, ps, st = ins["sc"]
    (dsb, dsc, dsh), g_sc = seq_bwd(f"sconv_bwd_{l}", fn_sconv, n, TS_ROW, xs, ps, cot(3), st, [BF16, BF16, BF16])
    dp = jnp.concatenate([dqkv, dz, dlx, dlg, duv, dsb, dsc, dsh, dba], axis=1)
    return dp, dict(gdn=g_gdn, lru=g_lru, sgu=g_sgu, sc=g_sc), side_res


def _ffn_ops(hid, lp):
    d_ffh = lp["d_ffh"]
    off = d_ffh // TC_FFN
    xs = [_rin(hid, TC_FFN, lambda j: j, halo=True), _rin(hid, TC_FFN, lambda j: j + off, halo=True)]
    ps = [_par(lp["ffn_cw"], TC_FFN, lambda j: j), _par(lp["ffn_cw"], TC_FFN, lambda j: j + off),
          _par(lp["ffn_cb"], TC_FFN, lambda j: j), _par(lp["ffn_cb"], TC_FFN, lambda j: j + off)]
    return xs, ps, d_ffh


_FROM_BLOCKS = dict(
    w_in=lambda b: _regroup_w_in(b.transpose(1, 0, 2).reshape(b.shape[1], -1)),
    ffn_up=lambda b: b.transpose(1, 0, 2).reshape(b.shape[1], -1),
    w_out=lambda b: b.reshape(-1, b.shape[2]),
    ffn_down=lambda b: b.reshape(-1, b.shape[2]),
)
_TO_BLOCKS = dict(
    w_in=lambda g: _ungroup_w_in(g).reshape(g.shape[0], N_DEV, -1).transpose(1, 0, 2),
    ffn_up=lambda g: g,
    w_out=lambda g: g.reshape(N_DEV, -1, g.shape[1]),
    ffn_down=lambda g: g.reshape(N_DEV, -1, g.shape[1]),
)


class _Traffic:
    PARTS = dict(w_in=1, w_out=1, ffn_up=2, ffn_down=1)

    def __init__(self, whole=None, shards=None):
        self.whole = dict(whole or {})
        self.shards = shards
        self.gathered = {}
        self.pending = {}
        self.landed = {}

    def _rows(self, key):
        name, l, part = key
        rows = self.shards[name].shape[1] // self.PARTS[name]
        return slice(part * rows, (part + 1) * rows)

    def jobs(self, gather=(), exchange=()):
        if self.shards is None:
            return [], []
        keys = [("gather", k) for k in gather if k not in self.gathered and k[:2] not in self.whole]
        keys += [("exchange", k) for k in exchange if k in self.pending]
        jobs = [(kind, self.shards[k[0]][k[1]][self._rows(k)] if kind == "gather" else self.pending[k])
                for kind, k in keys]
        return jobs, keys

    def done(self, keys, results):
        for (kind, k), r in zip(keys, results):
            if kind == "gather":
                self.gathered[k] = r
            else:
                self.landed[k] = r
                del self.pending[k]

    def weight(self, name, l):
        if (name, l) not in self.whole:
            parts = []
            for part in range(self.PARTS[name]):
                k = (name, l, part)
                if k not in self.gathered:
                    self.gathered[k] = all_gather(f"gather_{name}_{l}_{part}", self.shards[name][l][self._rows(k)], False)
                parts.append(self.gathered[k])
            blocks = parts[0] if len(parts) == 1 else jnp.concatenate(parts, axis=1)
            self.whole[(name, l)] = _FROM_BLOCKS[name](blocks)
        return self.whole[(name, l)]

    def grad(self, name, l, g):
        if self.shards is None:
            self.landed[(name, l)] = g
            return
        blocks = _TO_BLOCKS[name](g)
        for part in range(self.PARTS[name]):
            k = (name, l, part)
            self.pending[k] = blocks[:, self._rows(k)]

    def flush(self):
        for (name, l, part), blocks in list(self.pending.items()):
            self.landed[(name, l, part)] = all_to_all(f"exchange_{name}_{l}_{part}", blocks)
            del self.pending[(name, l, part)]


def local_step(x, target, wt, tr):
    n, d = x.shape
    depth = wt["pre_mix_norm"].shape[0]
    lps = [_layer_params(wt, l) for l in range(depth)]
    saved = []
    xin = x

    def mm(name, a, b, mode, dtype, tm, tn, tk, gather=(), exchange=(), **split):
        jobs, keys = tr.jobs(gather, exchange)
        if not jobs:
            return matmul(name, a, b, mode, dtype, tm, tn, tk, **split)
        out, res = matmul(name, a, b, mode, dtype, tm, tn, tk, side=jobs, **split)
        tr.done(keys, res)
        return out

    (h,), _ = seq_fwd("norm_fwd", fn_norm, n, TS_ROW, [_rin(x)], [_par(lps[0]["pre_mix"])], [_out(d, BF16)])
    dx_last = loss = None
    for l in range(depth):
        lp = lps[l]
        p = mm(f"w_in_fwd_{l}", h, tr.weight("w_in", l), "nn", F32, 2048, N_INP // 5, d,
               gather=[("ffn_up", l, 0)])
        jobs, keys = tr.jobs(gather=[("ffn_up", l, 1)])
        jobs_lru, keys_lru = tr.jobs(gather=[("w_out", l, 0)])
        ymix, mix_ins, res, res_lru = _mixers_fwd(l, p, lp, n, jobs, jobs_lru)
        tr.done(keys, res)
        tr.done(keys_lru, res_lru)
        y = mm(f"w_out_fwd_{l}", ymix, tr.weight("w_out", l), "nn", F32, 2048, 1024, d)
        res_ps = [_par(lp["post_mix"]), _par(lp["pre_ffn"])]
        (x1, h2), _ = seq_fwd(f"res_mix_fwd_{l}", fn_res, n, TS_ROW, [_rin(xin), _rin(y)], res_ps,
                              [_out(d, F32), _out(d, BF16)])
        nxt = l + 1 < depth
        hid = mm(f"ffn_up_fwd_{l}", h2, tr.weight("ffn_up", l), "nn", F32, 2048, 1024, d,
                 gather=[("ffn_down", l, 0)] + ([("w_out", l + 1, 0)] if nxt else []))
        f_xs, f_ps, d_ffh = _ffn_ops(hid, lp)
        jobs, keys = tr.jobs(gather=[("w_in", l + 1, 0)] if nxt else [])
        res = seq_fwd(f"ffn_act_fwd_{l}", fn_ffn, n, TS_FFN, f_xs, f_ps,
                      [_out(TC_FFN, BF16, d_ffh, lambda j: j)], ncol=d_ffh // TC_FFN, side=jobs)
        (act,) = res[0]
        tr.done(keys, res[2] if jobs else [])
        yf = mm(f"ffn_down_fwd_{l}", act, tr.weight("ffn_down", l), "nn", F32, 1024, 1024, d_ffh // 2)
        rec = dict(x=xin, h=h, mix_ins=mix_ins, ymix=ymix, y=y, x1=x1, h2=h2, f_xs=f_xs, f_ps=f_ps, act=act, yf=yf)
        if l + 1 < depth:
            ps = [_par(lp["post_ffn"]), _par(lps[l + 1]["pre_mix"])]
            (x2, h), _ = seq_fwd(f"res_ffn_fwd_{l}", fn_res, n, TS_ROW, [_rin(x1), _rin(yf)], ps,
                                 [_out(d, F32), _out(d, BF16)])
            rec["res_ffn_ps"] = ps
            xin = x2
        else:
            def body(xs, blks, ps, carries, r):
                x1_, yf_, t_ = xs
                e = x1_ + _rms(yf_, ps[0]) - t_
                part = 0.5 * jnp.sum(jnp.mean(e * e, axis=-1, keepdims=True), axis=0, keepdims=True)
                return [e * (1.0 / d)], [], [jnp.broadcast_to(part, (8, 128))], []

            ps = [_par(lp["post_ffn"])]
            dx_last, loss = _block_call("loss_fwd", body, n, TS_ROW, 1, False, [_rin(x1), _rin(yf), _rin(target)],
                                        [], ps, [_out(d, F32)], [], [dict(shape=(8, 128), total=None, col=None)], [])
            rec["res_ffn_ps"] = ps
        saved.append(rec)

    grads = {}
    dx2, dh_next = dx_last, None
    for l in reversed(range(depth)):
        rec, lp = saved[l], lps[l]
        d_ffh = lp["d_ffh"]
        g = {}
        if dh_next is None:
            (dx1, dyf), (g["post_ffn"],) = seq_bwd(f"res_ffn_bwd_{l}", fn_res_last, n, TS_ROW,
                                                   [_rin(rec["x1"]), _rin(rec["yf"])], rec["res_ffn_ps"], [_rin(dx2)],
                                                   din_dtypes=[F32, BF16])
        else:
            (dx1, dyf), (g["post_ffn"], g_next_pre) = seq_bwd(
                f"res_ffn_bwd_{l}", fn_res, n, TS_ROW, [_rin(rec["x1"]), _rin(rec["yf"])], rec["res_ffn_ps"],
                [_rin(dx2), _rin(dh_next)], din_dtypes=[F32, BF16])
            grads[l + 1]["pre_mix"] = g_next_pre
        tr.grad("ffn_down", l, mm(f"ffn_down_dw_{l}", rec["act"], dyf, "tn", BF16, d_ffh // 4, 1024, 2048))

        def dact(cot_blocks, aux_blocks):
            return [_dot(cot_blocks[0], aux_blocks[0], 1, 1).astype(BF16)]

        (dhg, dhv), (g_cwg, g_cwv, g_cbg, g_cbv) = seq_bwd(
            f"ffn_act_bwd_{l}", fn_ffn, n, TS_FFN_BWD, rec["f_xs"], rec["f_ps"], [_rin(dyf)],
            din_dtypes=[BF16, BF16], ncol=d_ffh // TC_FFN, din_specs=[(d_ffh, lambda j: j), (d_ffh, lambda j: j)],
            cot_map=dact, aux=[_par(tr.weight("ffn_down", l), TC_FFN, row=lambda j: j)])
        g["ffn_cw"] = jnp.concatenate([g_cwg[:, :d_ffh], g_cwv[:, d_ffh:]], axis=1)
        g["ffn_cb"] = jnp.concatenate([g_cbg[:, :d_ffh], g_cbv[:, d_ffh:]], axis=1)
        dh2 = mm(f"ffn_up_dx_{l}", dhg, tr.weight("ffn_up", l), "nt", BF16, 1024, 1024, d_ffh // 2,
                 exchange=[("ffn_down", l, 0)], a2=dhv)
        tr.grad("ffn_up", l, mm(f"ffn_up_dw_{l}", rec["h2"], dhg, "tn", BF16, 1024, 2 * d_ffh // N_DEV, 2048, b2=dhv,
                                col_blocks=True))
        (dx, dy), (g["post_mix"], g["pre_ffn"]) = seq_bwd(
            f"res_mix_bwd_{l}", fn_res, n, TS_ROW, [_rin(rec["x"]), _rin(rec["y"])],
            [_par(lp["post_mix"]), _par(lp["pre_ffn"])], [_rin(dx1), _rin(dh2)], din_dtypes=[F32, BF16])
        dymix = mm(f"w_out_dx_{l}", dy, tr.weight("w_out", l), "nt", BF16, 2048, 1024, d)
        tr.grad("w_out", l, mm(f"w_out_dw_{l}", rec["ymix"], dy, "tn", BF16, 1024, 1024, 2048))
        jobs, keys = tr.jobs(exchange=[("ffn_up", l, 0), ("w_out", l, 0)])
        dp, g["mix"], res = _mixers_bwd(l, dymix, rec["mix_ins"], n, jobs)
        tr.done(keys, res)
        tr.grad("w_in", l, mm(f"w_in_dw_{l}", rec["h"], dp, "tn", BF16, 1024, N_INP // 5, 2048,
                              exchange=[("ffn_up", l, 1)]))
        dh = mm(f"w_in_dx_{l}", dp, tr.weight("w_in", l), "nt", BF16, 1024, 1024, N_INP // 3,
                exchange=[("w_in", l, 0)])
        grads[l] = g
        dx2, dh_next = dx, dh
    (grad_x,), (g_pre0,) = seq_bwd("norm_bwd", fn_norm_keep, n, TS_ROW, [_rin(x)], [_par(lps[0]["pre_mix"])],
                                   [_rin(dh_next), _rin(dx2)])
    grads[0]["pre_mix"] = g_pre0
    tr.flush()
    return loss[0, 0], grad_x, _name_grads(grads, depth)


def _name_grads(grads, depth):
    per = {k: [] for k in SMALL}
    for l in range(depth):
        g = grads[l]
        m = g["mix"]
        cw, gp, nw = m["gdn"]
        lcw, lcb, lwa, lba, lwx, lbx, llam, gw0 = m["lru"]
        lnw, lnb, ws, bst, gw1 = m["sgu"]
        scw, gw2 = m["sc"]
        per["pre_mix_norm"].append(g["pre_mix"][0])
        per["gdn_conv_w"].append(cw)
        per["gdn_a_log"].append(gp[0, 4:8])
        per["gdn_dt_bias"].append(gp[1, 4:8])
        per["gdn_norm_w"].append(nw[0])
        per["lru_conv_w"].append(lcw)
        per["lru_conv_b"].append(lcb[0])
        per["lru_wa"].append(_block_diag_grad(lwa, LRU_BLOCKS))
        per["lru_ba"].append(lba.reshape(LRU_BLOCKS, -1))
        per["lru_wx"].append(_block_diag_grad(lwx, LRU_BLOCKS))
        per["lru_bx"].append(lbx.reshape(LRU_BLOCKS, -1))
        per["lru_lambda"].append(llam[0])
        per["sgu_ln_w"].append(lnw[0])
        per["sgu_ln_b"].append(lnb[0])
        per["sgu_ws"].append(ws.reshape(SGU_GROUPS, SGU_CHUNK, SGU_CHUNK))
        per["sgu_b"].append(bst[:, :SGU_GROUPS].T)
        per["sconv_w"].append(scw)
        per["grp_norm_w"].append(jnp.concatenate([gw0, gw1, gw2], axis=0))
        per["post_mix_norm"].append(g["post_mix"][0])
        per["pre_ffn_norm"].append(g["pre_ffn"][0])
        per["ffn_conv_w"].append(g["ffn_cw"])
        per["ffn_conv_b"].append(g["ffn_cb"][0])
        per["post_ffn_norm"].append(g["post_ffn"][0])
    return {k: jnp.stack(v) for k, v in per.items()}


def _regroup_w_in(w):
    pad = jnp.zeros(w.shape[:-1] + (N_INP - N_IN,), w.dtype)
    return jnp.concatenate([w[..., :2048], w[..., 2056:], w[..., 2048:2056], pad], axis=-1)


def _ungroup_w_in(g):
    return jnp.concatenate([g[..., :2048], g[..., BA_COL:BA_COL + 8], g[..., 2048:BA_COL]], axis=-1)


def kernel(x, pre_mix_norm, w_in, gdn_conv_w, gdn_a_log, gdn_dt_bias, gdn_norm_w, lru_conv_w, lru_conv_b, lru_wa, lru_ba, lru_wx, lru_bx, lru_lambda, sgu_ln_w, sgu_ln_b, sgu_ws, sgu_b, sconv_w, grp_norm_w, w_out, post_mix_norm, pre_ffn_norm, ffn_up, ffn_conv_w, ffn_conv_b, ffn_down, post_ffn_norm, loss_target, m_pre_mix_norm, m_w_in, m_gdn_conv_w, m_gdn_a_log, m_gdn_dt_bias, m_gdn_norm_w, m_lru_conv_w, m_lru_conv_b, m_lru_wa, m_lru_ba, m_lru_wx, m_lru_bx, m_lru_lambda, m_sgu_ln_w, m_sgu_ln_b, m_sgu_ws, m_sgu_b, m_sconv_w, m_grp_norm_w, m_w_out, m_post_mix_norm, m_pre_ffn_norm, m_ffn_up, m_ffn_conv_w, m_ffn_conv_b, m_ffn_down, m_post_ffn_norm, v_pre_mix_norm, v_w_in, v_gdn_conv_w, v_gdn_a_log, v_gdn_dt_bias, v_gdn_norm_w, v_lru_conv_w, v_lru_conv_b, v_lru_wa, v_lru_ba, v_lru_wx, v_lru_bx, v_lru_lambda, v_sgu_ln_w, v_sgu_ln_b, v_sgu_ws, v_sgu_b, v_sconv_w, v_grp_norm_w, v_w_out, v_post_mix_norm, v_pre_ffn_norm, v_ffn_up, v_ffn_conv_w, v_ffn_conv_b, v_ffn_down, v_post_ffn_norm):
    args = locals()
    w_loc = {k: args[k] for k in WEIGHTS}
    m_loc = {k: args["m_" + k] for k in WEIGHTS}
    v_loc = {k: args["v_" + k] for k in WEIGHTS}
    depth = pre_mix_norm.shape[0]
    x_, y_, c_ = _my_pos()
    me = 4 * x_ + 2 * y_ + c_

    tr = _Traffic(shards={name: cast_bf16(f"cast_{name}", w_loc[name]) for name in BIG})
    wt = {k: w_loc[k] for k in SMALL}
    shard_shapes = [w_loc[k].shape for k in SHARDED_SMALL]
    gathered = all_gather("gather_small", _pack([w_loc[k] for k in SHARDED_SMALL]), True)
    for k, a in zip(SHARDED_SMALL, _unpack(gathered, shard_shapes, lead=(N_DEV,))):
        a = jnp.moveaxis(a, 0, -2)
        wt[k] = a.reshape(a.shape[:-2] + (-1,))

    loss_part, grad_x, g_full = local_step(x[0], loss_target[0], wt, tr)
    loss = lax.psum(loss_part, ("x", "y", "c"))

    outs_g, outs_d, outs_m, outs_v = {}, {}, {}, {}
    for name in BIG:
        lands = {(l, part): a for (n_, l, part), a in tr.landed.items() if n_ == name}
        outs_g[name], outs_d[name], outs_m[name], outs_v[name] = adamw_big(
            f"adamw_{name}", w_loc[name], lands, m_loc[name], v_loc[name])

    full_shapes = [g_full[k].shape for k in SMALL]
    g_all = all_gather("gather_small_grads", _pack([g_full[k] for k in SMALL]), True)
    g_sum = _unpack(sum_blocks("sum_small_grads", g_all), full_shapes)
    g_small = {}
    for k, g in zip(SMALL, g_sum):
        if k in SHARDED_SMALL:
            w = w_loc[k].shape[-1]
            g = lax.dynamic_slice_in_dim(g, me * w, w, axis=g.ndim - 1)
        g_small[k] = g
    d_s, m_s, v_s = adamw_small("adamw_small", [w_loc[k] for k in SMALL], [g_small[k] for k in SMALL],
                                [m_loc[k] for k in SMALL], [v_loc[k] for k in SMALL])
    for k_i, k in enumerate(SMALL):
        outs_g[k], outs_d[k], outs_m[k], outs_v[k] = g_small[k], d_s[k_i], m_s[k_i], v_s[k_i]

    return (loss, grad_x[None], *[outs_g[k] for k in WEIGHTS], *[outs_d[k] for k in WEIGHTS],
            *[outs_m[k] for k in WEIGHTS], *[outs_v[k] for k in WEIGHTS])
```

```python
import functools
import math

import jax
import jax.numpy as jnp
from jax import lax
from jax.experimental import pallas as pl
from jax.experimental.pallas import tpu as pltpu

F32 = jnp.float32
BF16 = jnp.bfloat16
EPS = 1e-6
HALO = 8
VMEM_LIMIT = 56 * 1024 * 1024
MESH = pl.DeviceIdType.MESH
N_DEV = 8

ADAM_LR, ADAM_B1, ADAM_B2, ADAM_EPS, ADAM_WD, ADAM_STEP = 0.001, 0.9, 0.999, 1e-08, 0.01, 10

GDN_HEADS, GDN_DIM, GDN_CHUNK = 4, 128, 64
SGU_GROUPS, SGU_CHUNK = 4, 128
LRU_BLOCKS, LRU_C = 8, 8.0
D_G = 512
N_IN = 5640
N_INP = 5760
BA_COL = 5632

SHARDED_SMALL = ("gdn_conv_w", "lru_conv_w", "sconv_w", "grp_norm_w", "ffn_conv_w")
BIG = ("w_in", "w_out", "ffn_up", "ffn_down")
WEIGHTS = ("pre_mix_norm", "w_in", "gdn_conv_w", "gdn_a_log", "gdn_dt_bias", "gdn_norm_w", "lru_conv_w",
           "lru_conv_b", "lru_wa", "lru_ba", "lru_wx", "lru_bx", "lru_lambda", "sgu_ln_w", "sgu_ln_b", "sgu_ws",
           "sgu_b", "sconv_w", "grp_norm_w", "w_out", "post_mix_norm", "pre_ffn_norm", "ffn_up", "ffn_conv_w",
           "ffn_conv_b", "ffn_down", "post_ffn_norm")
SMALL = tuple(n for n in WEIGHTS if n not in BIG)


def _dot(a, b, ca, cb):
    return lax.dot_general(a.astype(BF16), b.astype(BF16), (((ca,), (cb,)), ((), ())),
                           preferred_element_type=F32)


@jax.custom_vjp
def _mm(a, b):
    return _dot(a, b, 1, 0)


def _mm_f(a, b):
    return _dot(a, b, 1, 0), (a, b)


def _mm_b(res, g):
    a, b = res
    return _dot(g, b, 1, 1), _dot(a, g, 0, 0)


_mm.defvjp(_mm_f, _mm_b)


@jax.custom_vjp
def _mm_nt(a, b):
    return _dot(a, b, 1, 1)


def _mm_nt_f(a, b):
    return _dot(a, b, 1, 1), (a, b)


def _mm_nt_b(res, g):
    a, b = res
    return _dot(g, b, 1, 0), _dot(g, a, 0, 0)


_mm_nt.defvjp(_mm_nt_f, _mm_nt_b)


@jax.custom_vjp
def _mm_tn(a, b):
    return _dot(a, b, 0, 0)


def _mm_tn_f(a, b):
    return _dot(a, b, 0, 0), (a, b)


def _mm_tn_b(res, g):
    a, b = res
    return _dot(b, g, 1, 1), _dot(a, g, 1, 0)


_mm_tn.defvjp(_mm_tn_f, _mm_tn_b)


def _dot_exact(a, b, ca, cb):
    return lax.dot_general(a, b, (((ca,), (cb,)), ((), ())), precision=lax.Precision.HIGHEST,
                           preferred_element_type=F32)


@functools.partial(jax.custom_vjp, nondiff_argnums=(1,))
def _shift_rows(x, s):
    return pltpu.roll(x, s, 0)


def _shift_rows_f(x, s):
    return pltpu.roll(x, s, 0), None


def _shift_rows_b(s, _, g):
    return (pltpu.roll(g, (g.shape[0] - s) % g.shape[0], 0),)


_shift_rows.defvjp(_shift_rows_f, _shift_rows_b)


def _sigmoid(x):
    return 1.0 / (1.0 + jnp.exp(-x))


def _silu(x):
    return x * _sigmoid(x)


GELU_C, GELU_A = 0.7978845608028654, 0.044715


@jax.custom_vjp
def _gelu(x):
    return 0.5 * x * (1.0 + jnp.tanh(GELU_C * (x + GELU_A * (x * x * x))))


def _gelu_f(x):
    t = jnp.tanh(GELU_C * (x + GELU_A * (x * x * x)))
    return 0.5 * x * (1.0 + t), (x, t)


def _gelu_b(res, g):
    x, t = res
    slope = 0.5 * (1.0 + t) + (0.5 * GELU_C) * x * (1.0 - t * t) * (1.0 + (3.0 * GELU_A) * (x * x))
    return (g * slope,)


_gelu.defvjp(_gelu_f, _gelu_b)


@jax.custom_vjp
def _softplus(x):
    e = jnp.exp(-jnp.abs(x))
    u = 1.0 + e
    log1p = jnp.where(u == 1.0, e, jnp.log(u) * (e / jnp.where(u == 1.0, 1.0, u - 1.0)))
    return jnp.maximum(x, 0.0) + log1p


def _softplus_f(x):
    return _softplus(x), x


def _softplus_b(x, g):
    return (g * _sigmoid(x),)


_softplus.defvjp(_softplus_f, _softplus_b)


def _neg_expm1(y):
    return -jnp.tanh(0.5 * y) * (jnp.exp(y) + 1.0)


def _rms(x, w):
    return x * lax.rsqrt(jnp.mean(x * x, axis=-1, keepdims=True) + EPS) * w


def _row(w, k):
    sel = lax.broadcasted_iota(jnp.int32, w.shape, 0) == k
    return jnp.sum(jnp.where(sel, w, 0.0), axis=0, keepdims=True)


def _col(x, j):
    sel = lax.broadcasted_iota(jnp.int32, x.shape, 1) == j
    return jnp.sum(jnp.where(sel, x, 0.0), axis=1, keepdims=True)


def _conv(x_ext, w, taps):
    acc = None
    for k in range(taps):
        s = taps - 1 - k
        t = (x_ext if s == 0 else _shift_rows(x_ext, s)) * _row(w, k)
        acc = t if acc is None else acc + t
    return acc[HALO:]


@jax.custom_vjp
def _scan(a, b, h0):
    n = a.shape[0]
    row = lax.broadcasted_iota(jnp.int32, a.shape, 0)
    s = 1
    while s < n:
        keep = row >= s
        a_sh = jnp.where(keep, pltpu.roll(a, s, 0), 1.0)
        b_sh = jnp.where(keep, pltpu.roll(b, s, 0), 0.0)
        b = a * b_sh + b
        a = a * a_sh
        s *= 2
    return b + a * h0


def _scan_f(a, b, h0):
    h = _scan(a, b, h0)
    return h, (a, h, h0)


def _scan_b(res, dh):
    a, h, h0 = res
    n = a.shape[0]
    row = lax.broadcasted_iota(jnp.int32, a.shape, 0)
    an = jnp.where(row < n - 1, pltpu.roll(a, n - 1, 0), 0.0)
    lam = dh
    s = 1
    while s < n:
        keep = row < n - s
        a_sh = jnp.where(keep, pltpu.roll(an, n - s, 0), 1.0)
        l_sh = jnp.where(keep, pltpu.roll(lam, n - s, 0), 0.0)
        lam = an * l_sh + lam
        an = an * a_sh
        s *= 2
    h_prev = jnp.where(row >= 1, pltpu.roll(h, 1, 0), h0)
    al = a * lam
    dh0 = jnp.sum(jnp.where(row == 0, al, 0.0), axis=0, keepdims=True)
    return lam * h_prev, lam, dh0


_scan.defvjp(_scan_f, _scan_b)


@jax.custom_vjp
def _unit_lower_inverses(ms):
    n = ms[0].shape[0]
    shape = ms[0].shape
    eye = (lax.broadcasted_iota(jnp.int32, shape, 0) == lax.broadcasted_iota(jnp.int32, shape, 1)).astype(F32)
    p = [-m for m in ms]
    t = [eye + a for a in p]
    steps = 1
    while 2 ** steps < n:
        p = [_mm(a, a) for a in p]
        t = [a + _mm(a, c) for a, c in zip(t, p)]
        steps += 1
    return t


def _unit_lower_inverses_f(ms):
    t = _unit_lower_inverses(ms)
    return t, t


def _unit_lower_inverses_b(t, dt):
    x = [_mm_nt(g, a) for g, a in zip(dt, t)]
    return ([-_mm_tn(a, c) for a, c in zip(t, x)],)


_unit_lower_inverses.defvjp(_unit_lower_inverses_f, _unit_lower_inverses_b)


def _last_row(x):
    sel = lax.broadcasted_iota(jnp.int32, x.shape, 0) == x.shape[0] - 1
    return jnp.sum(jnp.where(sel, x, 0.0), axis=0, keepdims=True)


def fn_norm(xs, st, ps):
    (x,), (w,) = xs, ps
    return [_rms(x, w).astype(BF16)], []


def fn_norm_keep(xs, st, ps):
    (x,), (w,) = xs, ps
    return [_rms(x, w).astype(BF16), x], []


def fn_res(xs, st, ps):
    (x, y), (w_post, w_next) = xs, ps
    x1 = x + _rms(y.astype(F32), w_post)
    return [x1, _rms(x1, w_next).astype(BF16)], []


def fn_res_last(xs, st, ps):
    (x, y), (w_post,) = xs, ps
    return [x + _rms(y.astype(F32), w_post)], []


def fn_gdn(xs, st, ps):
    qkv_ext, z, ba = xs
    (state,) = st
    cw, gp, nw = ps
    ts = z.shape[0]
    qkv = _silu(_conv(qkv_ext, cw, 4))
    beta_all = _sigmoid(ba)
    g_all = -jnp.exp(_row(gp, 0)) * _softplus(ba + _row(gp, 1))
    c_n = GDN_CHUNK
    ri = lax.broadcasted_iota(jnp.int32, (c_n, c_n), 0)
    ci = lax.broadcasted_iota(jnp.int32, (c_n, c_n), 1)
    causal, strict = ri >= ci, ri > ci
    tril = causal.astype(F32)
    lane = lax.broadcasted_iota(jnp.int32, (c_n, 128), 1)
    s_h = [state[GDN_DIM * h:GDN_DIM * (h + 1)] for h in range(GDN_HEADS)]
    n_c = ts // c_n
    pairs = [(c, h) for c in range(n_c) for h in range(GDN_HEADS)]
    every = lambda f, *lists: [f(*a) for a in zip(*lists)]

    def piece(c, h, base):
        return qkv[c * c_n:(c + 1) * c_n, base + GDN_DIM * h:base + GDN_DIM * (h + 1)]

    q = [piece(c, h, 0) for c, h in pairs]
    k = [piece(c, h, D_G) for c, h in pairs]
    v = [piece(c, h, 2 * D_G) for c, h in pairs]
    q = every(lambda t: t * lax.rsqrt(jnp.sum(t * t, axis=-1, keepdims=True) + EPS) * (GDN_DIM ** -0.5), q)
    k = every(lambda t: t * lax.rsqrt(jnp.sum(t * t, axis=-1, keepdims=True) + EPS), k)
    gcum_all = [_dot_exact(tril, g_all[c * c_n:(c + 1) * c_n], 1, 0) for c in range(n_c)]
    b = [_col(beta_all[c * c_n:(c + 1) * c_n], h) for c, h in pairs]
    gc = [_col(gcum_all[c], 4 + h) for c, h in pairs]
    gr = [_dot_exact((lane == 4 + h).astype(F32), gcum_all[c], 1, 1) for c, h in pairs]
    decay = every(lambda a, r: jnp.where(causal, jnp.exp(jnp.where(causal, a - r, 0.0)), 0.0), gc, gr)
    kb = every(lambda a, c: a * c, k, b)
    mk = every(lambda a, c, e: _mm_nt(jnp.concatenate([a, c], axis=0), e), kb, q, k)
    m = every(lambda a, dcy: jnp.where(strict, a[:c_n] * dcy, 0.0), mk, decay)
    attn = every(lambda a, dcy: jnp.where(causal, a[c_n:] * dcy, 0.0), mk, decay)
    t_ = _unit_lower_inverses(m)
    eg = every(jnp.exp, gc)
    wu = every(lambda t, a, e, c, d: _mm(t, jnp.concatenate([a * e, c * d], axis=1)), t_, kb, eg, v, b)
    g_last = every(_last_row, gc)
    k_g = every(lambda a, gl, g: a * jnp.exp(gl - g), k, g_last, gc)
    wq = every(lambda a, c, e: jnp.concatenate([a[:, :GDN_DIM], c * e], axis=0), wu, q, eg)
    u = [a[:, GDN_DIM:] for a in wu]
    gl = every(jnp.exp, g_last)

    o = []
    for c in range(n_c):
        idx = range(c * GDN_HEADS, (c + 1) * GDN_HEADS)
        ws = [_mm(wq[i], s_h[h]) for h, i in enumerate(idx)]
        v_new = [u[i] - ws[h][:c_n] for h, i in enumerate(idx)]
        av = [_mm(attn[i], v_new[h]) for h, i in enumerate(idx)]
        kv = [_mm_tn(k_g[i], v_new[h]) for h, i in enumerate(idx)]
        o += [ws[h][c_n:] + av[h] for h in range(GDN_HEADS)]
        s_h = [s_h[h] * gl[i] + kv[h] for h, i in enumerate(idx)]
    zz = [z[c * c_n:(c + 1) * c_n, GDN_DIM * h:GDN_DIM * (h + 1)] for c, h in pairs]
    y = every(lambda a, g: a * lax.rsqrt(jnp.mean(a * a, axis=-1, keepdims=True) + EPS) * nw * _silu(g), o, zz)
    rows = [jnp.concatenate(y[c * GDN_HEADS:(c + 1) * GDN_HEADS], axis=1) for c in range(n_c)]
    y = rows[0] if n_c == 1 else jnp.concatenate(rows, axis=0)
    return [y.astype(BF16)], [jnp.concatenate(s_h, axis=0)]


def fn_lru(xs, st, ps):
    x_ext, gate = xs
    (h0,) = st
    cw, cb, wa, ba, wx, bx, lam, gw = ps
    xc = _conv(x_ext, cw, 4) + cb
    r = _sigmoid(_mm(xc, wa) + ba)
    i = _sigmoid(_mm(xc, wx) + bx)
    log_a = -LRU_C * r * _softplus(-lam)
    a = jnp.exp(log_a)
    mult = jnp.sqrt(_neg_expm1(2.0 * log_a))
    h = _scan(a, mult * (i * xc), h0)
    y = _rms(h * _gelu(gate), gw)
    return [y.astype(BF16)], [_last_row(h)]


def fn_sgu(xs, st, ps):
    (uv,) = xs
    lnw, lnb, ws, bst, gw = ps
    ts = uv.shape[0]
    uvf = _gelu(uv)
    u, v = uvf[:, :D_G], uvf[:, D_G:]
    vc = v - jnp.mean(v, axis=-1, keepdims=True)
    v = vc * lax.rsqrt(jnp.mean(vc * vc, axis=-1, keepdims=True) + EPS) * lnw + lnb
    t_n = SGU_CHUNK
    tril = lax.broadcasted_iota(jnp.int32, (t_n, t_n), 0) >= lax.broadcasted_iota(jnp.int32, (t_n, t_n), 1)
    wg = [jnp.where(tril, ws[t_n * g:t_n * (g + 1)], 0.0) for g in range(SGU_GROUPS)]
    bg = [_col(bst, g) for g in range(SGU_GROUPS)]
    rows = []
    for c in range(ts // t_n):
        vcg = v[c * t_n:(c + 1) * t_n]
        rows.append(jnp.concatenate(
            [_mm(wg[g], vcg[:, 128 * g:128 * (g + 1)]) + bg[g] for g in range(SGU_GROUPS)], axis=1))
    vv = rows[0] if len(rows) == 1 else jnp.concatenate(rows, axis=0)
    return [_rms(u * vv, gw).astype(BF16)], []


def fn_sconv(xs, st, ps):
    bg, cg_ext, hh_ext = xs
    cw, gw = ps
    return [_rms(bg * _conv(cg_ext * hh_ext, cw, 3), gw).astype(BF16)], []


def fn_ffn(xs, st, ps):
    g_ext, v_ext = xs
    cwg, cwv, cbg, cbv = ps
    g = _conv(g_ext, cwg, 3) + cbg
    v = _conv(v_ext, cwv, 3) + cbv
    return [(_gelu(g) * v).astype(BF16)], []


def _my_pos():
    return lax.axis_index("x"), lax.axis_index("y"), lax.axis_index("c")


def _peer(pos, k):
    x_, y_, c_ = pos
    return (1 - x_ if (k >> 2) & 1 else x_, 1 - y_ if (k >> 1) & 1 else y_, 1 - c_ if k & 1 else c_)


def _dev_index(p):
    return 4 * p[0] + 2 * p[1] + p[2]


class _Side:
    def __init__(self, jobs):
        self.jobs = list(jobs)
        n = len(self.jobs)
        self.operands = [a for _, a in self.jobs]
        self.in_specs = [pl.BlockSpec(memory_space=pl.ANY)] * n
        self.out_shape = [jax.ShapeDtypeStruct(((N_DEV,) + a.shape) if kind == "gather" else a.shape, a.dtype)
                          for kind, a in self.jobs]
        self.out_specs = [pl.BlockSpec(memory_space=pl.ANY)] * n
        self.scratch = [pltpu.SemaphoreType.DMA((7 * n,)), pltpu.SemaphoreType.DMA((7 * n,)),
                        pltpu.SemaphoreType.DMA((n,))] if n else []

    def _copies(self, in_refs, out_refs, sems, landings=True):
        send, recv, local = sems
        pos = _my_pos()
        me = _dev_index(pos)
        mine, outgoing, landing = [], [], []
        for j, (kind, _) in enumerate(self.jobs):
            src, dst = in_refs[j], out_refs[j]
            own = src if kind == "gather" else src.at[me]
            mine.append(pltpu.make_async_copy(own, dst.at[me], local.at[j]))
            for k in range(1, N_DEV):
                p = _peer(pos, k)
                sems_k = dict(send_sem=send.at[7 * j + k - 1], recv_sem=recv.at[7 * j + k - 1], device_id=p,
                              device_id_type=MESH)
                outgoing.append(pltpu.make_async_remote_copy(
                    src_ref=src if kind == "gather" else src.at[_dev_index(p)], dst_ref=dst.at[me], **sems_k))
                if landings:
                    landing.append(pltpu.make_async_remote_copy(src_ref=own, dst_ref=dst.at[_dev_index(p)], **sems_k))
        return mine, outgoing, landing

    def start(self, in_refs, out_refs, sems):
        mine, outgoing, _ = self._copies(in_refs, out_refs, sems, landings=False)
        for cp in mine + outgoing:
            cp.start()

    def wait(self, in_refs, out_refs, sems):
        mine, outgoing, landing = self._copies(in_refs, out_refs, sems)
        for cp in landing:
            cp.wait_recv()
        for cp in outgoing:
            cp.wait_send()
        for cp in mine:
            cp.wait()


def _rin(arr, w=None, col=0, halo=False):
    return dict(arr=arr, w=arr.shape[1] if w is None else w, col=col, halo=halo)


def _par(arr, w=None, col=None, row=None):
    return dict(arr=arr, w=w, col=col, row=row)


def _colidx(col, j):
    return col(j) if callable(col) else col


def _block_call(name, body, n_rows, ts, ncol, reverse, row_ins, blk_ins, params, row_outs, blk_outs, acc_outs,
                carries, side=()):
    side = _Side(side)
    ts = min(ts, n_rows)
    nblk = n_rows // ts
    hb = ts // HALO

    def rr(i):
        return (nblk - 1 - i) if reverse else i

    in_specs, operands = [], []
    for s in row_ins:
        in_specs.append(pl.BlockSpec((ts, s["w"]), lambda j, i, s=s: (rr(i), _colidx(s["col"], j))))
        operands.append(s["arr"])
        if s["halo"]:
            in_specs.append(pl.BlockSpec((HALO, s["w"]),
                                         lambda j, i, s=s: (jnp.maximum(rr(i) * hb - 1, 0), _colidx(s["col"], j))))
            operands.append(s["arr"])
    for a in blk_ins:
        nd = a.ndim - 1
        in_specs.append(pl.BlockSpec((None,) + a.shape[1:], lambda j, i, nd=nd: (rr(i),) + (0,) * nd))
        operands.append(a)
    for p in params:
        a = p["arr"]
        if p["row"] is not None:
            in_specs.append(pl.BlockSpec((p["w"], a.shape[1]), lambda j, i, p=p: (_colidx(p["row"], j), 0)))
        elif p["col"] is None:
            in_specs.append(pl.BlockSpec(a.shape, lambda j, i: (0, 0)))
        else:
            in_specs.append(pl.BlockSpec((a.shape[0], p["w"]), lambda j, i, p=p: (0, _colidx(p["col"], j))))
        operands.append(a)

    out_specs, out_shape = [], []
    for o in row_outs:
        out_specs.append(pl.BlockSpec((ts, o["w"]), lambda j, i, o=o: (rr(i), _colidx(o["col"], j))))
        out_shape.append(jax.ShapeDtypeStruct((n_rows, o["total"]), o["dtype"]))
    for o in blk_outs:
        nd = len(o["shape"])
        out_specs.append(pl.BlockSpec((None,) + tuple(o["shape"]), lambda j, i, nd=nd: (rr(i),) + (0,) * nd))
        out_shape.append(jax.ShapeDtypeStruct((nblk,) + tuple(o["shape"]), o["dtype"]))
    for o in acc_outs:
        if o["col"] is None:
            out_specs.append(pl.BlockSpec(o["shape"], lambda j, i: (0, 0)))
            out_shape.append(jax.ShapeDtypeStruct(o["shape"], F32))
        else:
            out_specs.append(pl.BlockSpec(o["shape"], lambda j, i, o=o: (0, _colidx(o["col"], j))))
            out_shape.append(jax.ShapeDtypeStruct((o["shape"][0], o["total"]), F32))

    n_in = len(operands)
    n_row_out, n_blk_out, n_acc = len(row_outs), len(blk_outs), len(acc_outs)
    n_out = n_row_out + n_blk_out + n_acc
    n_side = len(side.jobs)

    def kern(*refs):
        in_refs = refs[:n_in]
        side_in = refs[n_in:n_in + n_side]
        out_refs = refs[n_in + n_side:n_in + n_side + n_out]
        side_out = refs[n_in + n_side + n_out:n_in + 2 * n_side + n_out]
        scratch = refs[n_in + 2 * n_side + n_out:]
        carry_refs, side_sems = scratch[:len(carries)], scratch[len(carries):]
        acc_refs = out_refs[n_row_out + n_blk_out:]
        i = pl.program_id(1)
        r = rr(i)
        if n_side:
            @pl.when((pl.program_id(0) == 0) & (i == 0))
            def _():
                side.start(side_in, side_out, side_sems)

        @pl.when(i == 0)
        def _():
            for c_ref in carry_refs:
                c_ref[...] = jnp.zeros(c_ref.shape, c_ref.dtype)
            for a_ref in acc_refs:
                a_ref[...] = jnp.zeros(a_ref.shape, a_ref.dtype)

        k = 0
        xs = []
        for s in row_ins:
            x = in_refs[k][...]
            k += 1
            if s["halo"]:
                hal = in_refs[k][...]
                k += 1
                hal = jnp.where(r == 0, jnp.zeros_like(hal), hal)
                x = jnp.concatenate([hal, x], axis=0)
            xs.append(x)
        blks = []
        for _ in blk_ins:
            blks.append(in_refs[k][...])
            k += 1
        ps = []
        for _ in params:
            ps.append(in_refs[k][...])
            k += 1
        row_vals, blk_vals, acc_vals, new_carries = body(xs, blks, ps, [c[...] for c in carry_refs], r)
        for ref, val in zip(out_refs[:n_row_out], row_vals):
            ref[...] = val.astype(ref.dtype)
        for ref, val in zip(out_refs[n_row_out:n_row_out + n_blk_out], blk_vals):
            ref[...] = val.astype(ref.dtype)
        for ref, val in zip(acc_refs, acc_vals):
            ref[...] += val
        for ref, val in zip(carry_refs, new_carries):
            ref[...] = val
        if n_side:
            @pl.when((pl.program_id(0) == ncol - 1) & (i == nblk - 1))
            def _():
                side.wait(side_in, side_out, side_sems)

    res = pl.pallas_call(
        kern,
        name=name,
        grid=(ncol, nblk),
        in_specs=in_specs + side.in_specs,
        out_specs=out_specs + side.out_specs,
        out_shape=out_shape + side.out_shape,
        scratch_shapes=[pltpu.VMEM(shape, F32) for shape in carries] + side.scratch,
        compiler_params=pltpu.CompilerParams(dimension_semantics=("arbitrary", "arbitrary"),
                                             vmem_limit_bytes=VMEM_LIMIT),
    )(*operands, *side.operands)
    return list(res)


def _out(w, dtype, total=None, col=0):
    return dict(w=w, dtype=dtype, total=w if total is None else total, col=col)


def seq_fwd(name, fn, n_rows, ts, row_ins, params, outs, state_shapes=(), ncol=1, side=()):
    def body(xs, blks, ps, carries, r):
        o, new_st = fn(xs, list(carries), ps)
        return o, list(carries), [], new_st

    res = _block_call(name, body, n_rows, ts, ncol, False, row_ins, [], params, outs,
                      [dict(shape=s, dtype=F32) for s in state_shapes], [], list(state_shapes), side)
    n_o, n_s = len(outs), len(state_shapes)
    return (res[:n_o], res[n_o:n_o + n_s]) + ((res[n_o + n_s:],) if side else ())


def seq_bwd(name, fn, n_rows, ts, row_ins, params, cots, saved_states=(), din_dtypes=None, ncol=1, din_specs=None,
            side=(), cot_map=None, aux=()):
    n_x, n_p, n_st = len(row_ins), len(params), len(saved_states)
    halo_idx = [k for k, s in enumerate(row_ins) if s["halo"]]
    state_shapes = [a.shape[1:] for a in saved_states]

    def body(xs_all, blks, ps, carries, r):
        xs, cot_vals = xs_all[:n_x], xs_all[n_x:]
        d_state, d_halo = carries[:n_st], carries[n_st:]
        if cot_map is not None:
            cot_vals = cot_map(cot_vals, ps[n_p:])
        (o, _), vjp = jax.vjp(lambda a, b, c: fn(a, b, c), xs, blks, ps[:n_p])
        cot = [c.astype(v.dtype) for c, v in zip(cot_vals, o)]
        dxs, dst, dps = vjp((cot, list(d_state)))
        row_vals, new_halo = [], []
        for k, dx in enumerate(dxs):
            if k in halo_idx:
                hk = halo_idx.index(k)
                rows = dx.shape[0] - HALO
                tail = dx[rows:] + d_halo[hk]
                row_vals.append(jnp.concatenate([dx[HALO:rows], tail], axis=0))
                new_halo.append(dx[:HALO])
            else:
                row_vals.append(dx)
        return row_vals, [], list(dps), list(dst) + new_halo

    din_dtypes = din_dtypes or [F32] * n_x
    douts = []
    for k, s in enumerate(row_ins):
        total, col = (s["w"], 0) if din_specs is None or din_specs[k] is None else din_specs[k]
        douts.append(_out(s["w"], din_dtypes[k], total, col))
    accs = []
    for p in params:
        a = p["arr"]
        if p["col"] is None:
            accs.append(dict(shape=a.shape, total=None, col=None))
        else:
            accs.append(dict(shape=(a.shape[0], p["w"]), total=a.shape[1], col=p["col"]))
    carries = list(state_shapes) + [(HALO, row_ins[k]["w"]) for k in halo_idx]
    res = _block_call(name, body, n_rows, ts, ncol, True, list(row_ins) + list(cots), list(saved_states),
                      list(params) + list(aux), douts, [], accs, carries, side)
    return (res[:n_x], res[n_x:n_x + n_p]) + ((res[n_x + n_p:],) if side else ())


def matmul(name, a, b, mode, out_dtype, tm, tn, tk, side=(), a2=None, b2=None, col_blocks=False):
    side = _Side(side)
    n_side = len(side.jobs)
    if mode == "tn":
        (kk, m), n = a.shape, b.shape[1]
    else:
        (m, kk), n = a.shape, (b.shape[0] if mode == "nt" else b.shape[1])
    k1, n1 = kk, n
    if a2 is not None:
        assert mode != "tn" and b2 is None
        kk += a2.shape[1]
    if b2 is not None:
        assert mode == "tn"
        n += b2.shape[1]
    tm, tn, tk = min(tm, m), min(tn, n), min(tk, kk)
    nk, gm, gn = kk // tk, m // tm, n // tn
    assert m % tm == 0 and n % tn == 0 and kk % tk == 0 and k1 % tk == 0 and n1 % tn == 0, (name, a.shape, b.shape)
    nk1, gn1 = k1 // tk, n1 // tn
    if mode == "tn":
        a_specs = [pl.BlockSpec((tk, tm), lambda i, j, k: (k, i))]
        b_specs = [pl.BlockSpec((tk, tn), lambda i, j, k: (k, jnp.minimum(j, gn1 - 1)))]
        if b2 is not None:
            b_specs.append(pl.BlockSpec((tk, tn), lambda i, j, k: (k, jnp.maximum(j - gn1, 0))))
    else:
        a_specs = [pl.BlockSpec((tm, tk), lambda i, j, k: (i, jnp.minimum(k, nk1 - 1)))]
        if a2 is not None:
            a_specs.append(pl.BlockSpec((tm, tk), lambda i, j, k: (i, jnp.maximum(k - nk1, 0))))
        b_specs = [pl.BlockSpec((tn, tk), lambda i, j, k: (j, k)) if mode == "nt"
                   else pl.BlockSpec((tk, tn), lambda i, j, k: (k, j))]
    ca, cb = {"nn": (1, 0), "nt": (1, 1), "tn": (0, 0)}[mode]
    n_a, n_b = len(a_specs), len(b_specs)

    def kern(*refs):
        a_refs, b_refs = refs[:n_a], refs[n_a:n_a + n_b]
        rest = refs[n_a + n_b:]
        side_in = rest[:n_side]
        o_ref = rest[n_side]
        side_out = rest[1 + n_side:1 + 2 * n_side]
        acc_ref = rest[1 + 2 * n_side]
        side_sems = rest[2 + 2 * n_side:]
        i, j, k = pl.program_id(0), pl.program_id(1), pl.program_id(2)
        if n_side:
            @pl.when((i == 0) & (j == 0) & (k == 0))
            def _():
                side.start(side_in, side_out, side_sems)

        def step(a_ref, b_ref):
            part = lax.dot_general(a_ref[...], b_ref[...], (((ca,), (cb,)), ((), ())), preferred_element_type=F32)
            if nk == 1:
                o_ref[...] = part.astype(o_ref.dtype)
            else:
                @pl.when(k == 0)
                def _():
                    acc_ref[...] = part

                @pl.when(k > 0)
                def _():
                    acc_ref[...] += part

                @pl.when(k == nk - 1)
                def _():
                    o_ref[...] = acc_ref[...].astype(o_ref.dtype)

        if n_a == 2:
            pl.when(k < nk1)(lambda: step(a_refs[0], b_refs[0]))
            pl.when(k >= nk1)(lambda: step(a_refs[1], b_refs[0]))
        elif n_b == 2:
            pl.when(j < gn1)(lambda: step(a_refs[0], b_refs[0]))
            pl.when(j >= gn1)(lambda: step(a_refs[0], b_refs[1]))
        else:
            step(a_refs[0], b_refs[0])

        if n_side:
            @pl.when((i == gm - 1) & (j == gn - 1) & (k == nk - 1))
            def _():
                side.wait(side_in, side_out, side_sems)

    semantics = ("arbitrary",) * 3 if n_side else ("parallel", "parallel", "arbitrary")
    operands = [a] + ([a2] if a2 is not None else []) + [b] + ([b2] if b2 is not None else [])
    res = pl.pallas_call(
        kern,
        name=name,
        grid=(gm, gn, nk),
        in_specs=a_specs + b_specs + side.in_specs,
        out_specs=[pl.BlockSpec((None, tm, tn), lambda i, j, k: (j, i, 0)) if col_blocks
                   else pl.BlockSpec((tm, tn), lambda i, j, k: (i, j))] + side.out_specs,
        out_shape=[jax.ShapeDtypeStruct((gn, m, tn) if col_blocks else (m, n), out_dtype)] + side.out_shape,
        scratch_shapes=[pltpu.VMEM((tm, tn) if nk > 1 else (8, 128), F32)] + side.scratch,
        compiler_params=pltpu.CompilerParams(dimension_semantics=semantics, vmem_limit_bytes=VMEM_LIMIT),
    )(*operands, *side.operands)
    return (res[0], list(res[1:])) if n_side else res[0]


def all_gather(name, x, in_vmem):
    def body(x_ref, out_ref, send_sems, recv_sems, local_sem):
        x_, y_, c_ = _my_pos()
        me, sibling = (x_, y_, c_), (x_, y_, 1 - c_)
        chips = [(1 - x_, y_), (x_, 1 - y_), (1 - x_, 1 - y_)]

        def slot(px, py, pc):
            return out_ref.at[4 * px + 2 * py + pc]

        def copy(k, block, to, src=None):
            return pltpu.make_async_remote_copy(
                src_ref=slot(*block) if src is None else src, dst_ref=slot(*block),
                send_sem=send_sems.at[k], recv_sem=recv_sems.at[k], device_id=to, device_id_type=MESH)

        mine = pltpu.make_async_copy(x_ref, slot(*me), local_sem)
        mine.start()
        first = [copy(0, me, sibling, src=x_ref)]
        first += [copy(1 + j, me, (*chip, c_), src=x_ref) for j, chip in enumerate(chips)]
        for cp in first:
            cp.start()
        passed = [copy(4 + j, (*chip, c_), sibling) for j, chip in enumerate(chips)]
        for j, chip in enumerate(chips):
            copy(1 + j, (*chip, c_), me).wait_recv()
            passed[j].start()
        copy(0, sibling, me).wait_recv()
        for j, chip in enumerate(chips):
            copy(4 + j, (*chip, 1 - c_), me).wait_recv()
        for cp in first + passed:
            cp.wait_send()
        mine.wait()

    space = pltpu.VMEM if in_vmem else pl.ANY
    return pl.pallas_call(
        body,
        name=name,
        out_shape=jax.ShapeDtypeStruct((N_DEV,) + x.shape, x.dtype),
        in_specs=[pl.BlockSpec(memory_space=space)],
        out_specs=pl.BlockSpec(memory_space=space),
        scratch_shapes=[pltpu.SemaphoreType.DMA((7,)), pltpu.SemaphoreType.DMA((7,)), pltpu.SemaphoreType.DMA],
        compiler_params=pltpu.CompilerParams(vmem_limit_bytes=VMEM_LIMIT),
    )(x)


def all_to_all(name, g):
    def body(g_ref, out_ref, send_sems, recv_sems, local_sem):
        x_, y_, c_ = _my_pos()
        me = 4 * x_ + 2 * y_ + c_

        def peer(k):
            fx, fy, fc = (k >> 2) & 1, (k >> 1) & 1, k & 1
            return (1 - x_ if fx else x_, 1 - y_ if fy else y_, 1 - c_ if fc else c_)

        def copy(k):
            px, py, pc = peer(k)
            return pltpu.make_async_remote_copy(
                src_ref=g_ref.at[4 * px + 2 * py + pc], dst_ref=out_ref.at[me],
                send_sem=send_sems.at[k - 1], recv_sem=recv_sems.at[k - 1], device_id=(px, py, pc), device_id_type=MESH)

        def landing(k):
            px, py, pc = peer(k)
            return pltpu.make_async_remote_copy(
                src_ref=g_ref.at[me], dst_ref=out_ref.at[4 * px + 2 * py + pc],
                send_sem=send_sems.at[k - 1], recv_sem=recv_sems.at[k - 1], device_id=(px, py, pc), device_id_type=MESH)

        mine = pltpu.make_async_copy(g_ref.at[me], out_ref.at[me], local_sem)
        mine.start()
        sends = [copy(k) for k in range(1, N_DEV)]
        for cp in sends:
            cp.start()
        for k in range(1, N_DEV):
            landing(k).wait_recv()
        for cp in sends:
            cp.wait_send()
        mine.wait()

    return pl.pallas_call(
        body,
        name=name,
        out_shape=jax.ShapeDtypeStruct(g.shape, g.dtype),
        in_specs=[pl.BlockSpec(memory_space=pl.ANY)],
        out_specs=pl.BlockSpec(memory_space=pl.ANY),
        scratch_shapes=[pltpu.SemaphoreType.DMA((7,)), pltpu.SemaphoreType.DMA((7,)), pltpu.SemaphoreType.DMA],
    )(g)


def sum_blocks(name, g):
    def body(g_ref, o_ref):
        acc = g_ref[0]
        for s in range(1, N_DEV):
            acc = acc + g_ref[s]
        o_ref[...] = acc

    r = g.shape[1]
    tr = r // 4 if r % 32 == 0 else r
    return pl.pallas_call(
        body, name=name, grid=(r // tr,),
        in_specs=[pl.BlockSpec((N_DEV, tr, 128), lambda i: (0, i, 0))],
        out_specs=pl.BlockSpec((tr, 128), lambda i: (i, 0)),
        out_shape=jax.ShapeDtypeStruct((r, 128), F32),
        compiler_params=pltpu.CompilerParams(vmem_limit_bytes=VMEM_LIMIT),
    )(g)


def _adamw_math(w, g, m, v):
    m = ADAM_B1 * m + (1.0 - ADAM_B1) * g
    v = ADAM_B2 * v + (1.0 - ADAM_B2) * (g * g)
    m_hat = m / (1.0 - ADAM_B1 ** ADAM_STEP)
    v_hat = v / (1.0 - ADAM_B2 ** ADAM_STEP)
    delta = -ADAM_LR * (m_hat / (jnp.sqrt(v_hat) + ADAM_EPS) + ADAM_WD * w)
    return delta, m, v


ADAMW_BLOCK_BYTES = 3 << 19


def _row_tile(rows, row_bytes, limit):
    best = 8
    for t in range(8, rows + 1, 8):
        if rows % t == 0 and t * row_bytes <= limit:
            best = t
    return best


def adamw_big(name, w, lands, m, v):
    depth, r, c = w.shape
    outs = None
    for (l, part), land in sorted(lands.items()):
        rows = land.shape[1]
        tr = _row_tile(rows, 4 * (-(-c // 128) * 128), ADAMW_BLOCK_BYTES)
        first = part * rows // tr

        def body(w_ref, l_ref, m_ref, v_ref, *rest):
            g_out, d_out, m_out, v_out = rest[-4:]
            g = l_ref[0].astype(F32)
            for s in range(1, N_DEV):
                g = g + l_ref[s].astype(F32)
            delta, m_new, v_new = _adamw_math(w_ref[...], g, m_ref[...], v_ref[...])
            g_out[...] = g
            d_out[...] = delta
            m_out[...] = m_new
            v_out[...] = v_new

        spec = pl.BlockSpec((None, tr, c), lambda i, l=l, first=first: (l, first + i, 0))
        carried = [] if outs is None else list(outs)
        outs = pl.pallas_call(
            body, name=f"{name}_{l}_{part}", grid=(rows // tr,),
            in_specs=[spec, pl.BlockSpec((N_DEV, tr, c), lambda i: (0, i, 0)), spec, spec]
            + [pl.BlockSpec(memory_space=pl.ANY)] * len(carried),
            out_specs=[spec] * 4,
            out_shape=[jax.ShapeDtypeStruct((depth, r, c), F32)] * 4,
            input_output_aliases={4 + k: k for k in range(len(carried))},
            compiler_params=pltpu.CompilerParams(dimension_semantics=("parallel",), vmem_limit_bytes=VMEM_LIMIT),
        )(w, land, m, v, *carried)
    return outs


def adamw_small(name, ws, gs, ms, vs):
    n = len(ws)

    def body(*refs):
        ins, outs = refs[:4 * n], refs[4 * n:]
        for k in range(n):
            delta, m_new, v_new = _adamw_math(ins[k][...], ins[n + k][...], ins[2 * n + k][...], ins[3 * n + k][...])
            outs[k][...] = delta
            outs[n + k][...] = m_new
            outs[2 * n + k][...] = v_new

    res = pl.pallas_call(
        body, name=name,
        out_shape=[jax.ShapeDtypeStruct(w.shape, F32) for w in ws] * 3,
        compiler_params=pltpu.CompilerParams(vmem_limit_bytes=VMEM_LIMIT),
    )(*ws, *gs, *ms, *vs)
    return res[:n], res[n:2 * n], res[2 * n:]


def cast_bf16(name, w):
    depth, r, c = w.shape
    tr = _row_tile(r, 4 * (-(-c // 128) * 128), ADAMW_BLOCK_BYTES)

    def body(w_ref, o_ref):
        o_ref[...] = w_ref[...].astype(BF16)

    spec = pl.BlockSpec((None, tr, c), lambda l, i: (l, i, 0))
    return pl.pallas_call(body, name=name, grid=(depth, r // tr), in_specs=[spec], out_specs=spec,
                          out_shape=jax.ShapeDtypeStruct((depth, r, c), BF16),
                          compiler_params=pltpu.CompilerParams(dimension_semantics=("parallel", "parallel")))(w)


def _rows_of(shape):
    return -(-math.prod(shape) // 128)


def _pack(arrs):
    pieces = []
    for a in arrs:
        flat = a.reshape(-1).astype(F32)
        pieces.append(jnp.pad(flat, (0, (-flat.shape[0]) % 128)).reshape(-1, 128))
    rows = sum(p.shape[0] for p in pieces)
    if rows % 8:
        pieces.append(jnp.zeros((8 - rows % 8, 128), F32))
    return jnp.concatenate(pieces, axis=0)


def _unpack(packed, shapes, lead=()):
    out, r0 = [], 0
    for s in shapes:
        rows, n = _rows_of(s), math.prod(s)
        piece = packed[..., r0:r0 + rows, :].reshape(lead + (rows * 128,))
        out.append(piece[..., :n].reshape(lead + tuple(s)))
        r0 += rows
    return out


def _block_diag(w):
    h, d, _ = w.shape
    eye = jnp.eye(h, dtype=w.dtype)
    return (eye[:, None, :, None] * w[:, :, None, :]).reshape(h * d, h * d)


def _block_diag_grad(g, h):
    d = g.shape[0] // h
    eye = jnp.eye(h, dtype=g.dtype)
    return jnp.sum(g.reshape(h, d, h, d) * eye[:, None, :, None], axis=2)


def _layer_params(wt, l):
    gp = jnp.pad(jnp.stack([wt["gdn_a_log"][l], wt["gdn_dt_bias"][l]]), ((0, 6), (4, 128 - 4 - GDN_HEADS)))
    d_ffh = wt["ffn_conv_w"].shape[-1] // 2
    return dict(
        pre_mix=wt["pre_mix_norm"][l][None], post_mix=wt["post_mix_norm"][l][None],
        pre_ffn=wt["pre_ffn_norm"][l][None], post_ffn=wt["post_ffn_norm"][l][None],
        gdn_cw=wt["gdn_conv_w"][l], gdn_gp=gp, gdn_nw=wt["gdn_norm_w"][l][None],
        lru_cw=wt["lru_conv_w"][l], lru_cb=wt["lru_conv_b"][l][None],
        lru_wa=_block_diag(wt["lru_wa"][l]), lru_ba=wt["lru_ba"][l].reshape(1, -1),
        lru_wx=_block_diag(wt["lru_wx"][l]), lru_bx=wt["lru_bx"][l].reshape(1, -1),
        lru_lam=wt["lru_lambda"][l][None], gw0=wt["grp_norm_w"][l, 0][None], gw1=wt["grp_norm_w"][l, 1][None],
        gw2=wt["grp_norm_w"][l, 2][None],
        sgu_lnw=wt["sgu_ln_w"][l][None], sgu_lnb=wt["sgu_ln_b"][l][None],
        sgu_ws=wt["sgu_ws"][l].reshape(SGU_GROUPS * SGU_CHUNK, SGU_CHUNK),
        sgu_bt=jnp.pad(wt["sgu_b"][l].T, ((0, 0), (0, 128 - SGU_GROUPS))),
        sc_cw=wt["sconv_w"][l],
        ffn_cw=wt["ffn_conv_w"][l], ffn_cb=wt["ffn_conv_b"][l][None], d_ffh=d_ffh,
    )


TS_ROW = 256
TS_GDN = 256
TS_FFN = 512
TS_FFN_BWD = 256
TC_FFN = 512


def _mixers_fwd(l, p, lp, n, side, side_lru):
    qkv = _rin(p, 3 * D_G, 0, halo=True)
    z = _rin(p, D_G, 3)
    ba = _rin(p, 128, BA_COL // 128)
    gdn_ps = [_par(lp["gdn_cw"]), _par(lp["gdn_gp"]), _par(lp["gdn_nw"])]
    res = seq_fwd(f"gdn_fwd_{l}", fn_gdn, n, TS_GDN, [qkv, z, ba], gdn_ps, [_out(D_G, BF16)],
                  state_shapes=[(GDN_HEADS * GDN_DIM, GDN_DIM)], side=side)
    (y_a,), (gdn_st,), side_res = res if side else res + ([],)
    lru_x = _rin(p, D_G, 4, halo=True)
    lru_gate = _rin(p, D_G, 5)
    lru_ps = [_par(lp[k]) for k in ("lru_cw", "lru_cb", "lru_wa", "lru_ba", "lru_wx", "lru_bx", "lru_lam", "gw0")]
    res = seq_fwd(f"lru_fwd_{l}", fn_lru, n, TS_ROW, [lru_x, lru_gate], lru_ps, [_out(D_G, BF16)],
                  state_shapes=[(1, D_G)], side=side_lru)
    (y_b,), (lru_st,), side_res_lru = res if side_lru else res + ([],)
    uv = _rin(p, 2 * D_G, 3)
    sgu_ps = [_par(lp[k]) for k in ("sgu_lnw", "sgu_lnb", "sgu_ws", "sgu_bt", "gw1")]
    (y_c,), _ = seq_fwd(f"sgu_fwd_{l}", fn_sgu, n, TS_ROW, [uv], sgu_ps, [_out(D_G, BF16)])
    sc = [_rin(p, D_G, 8), _rin(p, D_G, 9, halo=True), _rin(p, D_G, 10, halo=True)]
    sc_ps = [_par(lp["sc_cw"]), _par(lp["gw2"])]
    (y_d,), _ = seq_fwd(f"sconv_fwd_{l}", fn_sconv, n, TS_ROW, sc, sc_ps, [_out(D_G, BF16)])
    ins = dict(gdn=([qkv, z, ba], gdn_ps, [gdn_st]), lru=([lru_x, lru_gate], lru_ps, [lru_st]),
               sgu=([uv], sgu_ps, []), sc=(sc, sc_ps, []))
    return jnp.concatenate([y_a, y_b, y_c, y_d], axis=1), ins, side_res, side_res_lru


def _mixers_bwd(l, dymix, ins, n, side):
    cot = lambda g: [_rin(dymix, D_G, g)]
    xs, ps, st = ins["gdn"]
    res = seq_bwd(f"gdn_bwd_{l}", fn_gdn, n, TS_GDN, xs, ps, cot(0), st, [BF16, BF16, BF16], side=side)
    (dqkv, dz, dba), g_gdn, side_res = res if side else res + ([],)
    xs, ps, st = ins["lru"]
    (dlx, dlg), g_lru = seq_bwd(f"lru_bwd_{l}", fn_lru, n, TS_ROW, xs, ps, cot(1), st, [BF16, BF16])
    xs, ps, st = ins["sgu"]
    (duv,), g_sgu = seq_bwd(f"sgu_bwd_{l}", fn_sgu, n, TS_ROW, xs, ps, cot(2), st, [BF16])
    xs, ps, st = ins["sc"]
    (dsb, dsc, dsh), g_sc = seq_bwd(f"sconv_bwd_{l}", fn_sconv, n, TS_ROW, xs, ps, cot(3), st, [BF16, BF16, BF16])
    dp = jnp.concatenate([dqkv, dz, dlx, dlg, duv, dsb, dsc, dsh, dba], axis=1)
    return dp, dict(gdn=g_gdn, lru=g_lru, sgu=g_sgu, sc=g_sc), side_res


def _ffn_ops(hid, lp):
    d_ffh = lp["d_ffh"]
    off = d_ffh // TC_FFN
    xs = [_rin(hid, TC_FFN, lambda j: j, halo=True), _rin(hid, TC_FFN, lambda j: j + off, halo=True)]
    ps = [_par(lp["ffn_cw"], TC_FFN, lambda j: j), _par(lp["ffn_cw"], TC_FFN, lambda j: j + off),
          _par(lp["ffn_cb"], TC_FFN, lambda j: j), _par(lp["ffn_cb"], TC_FFN, lambda j: j + off)]
    return xs, ps, d_ffh


_FROM_BLOCKS = dict(
    w_in=lambda b: _regroup_w_in(b.transpose(1, 0, 2).reshape(b.shape[1], -1)),
    ffn_up=lambda b: b.transpose(1, 0, 2).reshape(b.shape[1], -1),
    w_out=lambda b: b.reshape(-1, b.shape[2]),
    ffn_down=lambda b: b.reshape(-1, b.shape[2]),
)
_TO_BLOCKS = dict(
    w_in=lambda g: _ungroup_w_in(g).reshape(g.shape[0], N_DEV, -1).transpose(1, 0, 2),
    ffn_up=lambda g: g,
    w_out=lambda g: g.reshape(N_DEV, -1, g.shape[1]),
    ffn_down=lambda g: g.reshape(N_DEV, -1, g.shape[1]),
)


class _Traffic:
    PARTS = dict(w_in=1, w_out=1, ffn_up=2, ffn_down=1)

    def __init__(self, whole=None, shards=None):
        self.whole = dict(whole or {})
        self.shards = shards
        self.gathered = {}
        self.pending = {}
        self.landed = {}

    def _rows(self, key):
        name, l, part = key
        rows = self.shards[name].shape[1] // self.PARTS[name]
        return slice(part * rows, (part + 1) * rows)

    def jobs(self, gather=(), exchange=()):
        if self.shards is None:
            return [], []
        keys = [("gather", k) for k in gather if k not in self.gathered and k[:2] not in self.whole]
        keys += [("exchange", k) for k in exchange if k in self.pending]
        jobs = [(kind, self.shards[k[0]][k[1]][self._rows(k)] if kind == "gather" else self.pending[k])
                for kind, k in keys]
        return jobs, keys

    def done(self, keys, results):
        for (kind, k), r in zip(keys, results):
            if kind == "gather":
                self.gathered[k] = r
            else:
                self.landed[k] = r
                del self.pending[k]

    def weight(self, name, l):
        if (name, l) not in self.whole:
            parts = []
            for part in range(self.PARTS[name]):
                k = (name, l, part)
                if k not in self.gathered:
                    self.gathered[k] = all_gather(f"gather_{name}_{l}_{part}", self.shards[name][l][self._rows(k)], False)
                parts.append(self.gathered[k])
            blocks = parts[0] if len(parts) == 1 else jnp.concatenate(parts, axis=1)
            self.whole[(name, l)] = _FROM_BLOCKS[name](blocks)
        return self.whole[(name, l)]

    def grad(self, name, l, g):
        if self.shards is None:
            self.landed[(name, l)] = g
            return
        blocks = _TO_BLOCKS[name](g)
        for part in range(self.PARTS[name]):
            k = (name, l, part)
            self.pending[k] = blocks[:, self._rows(k)]

    def flush(self):
        for (name, l, part), blocks in list(self.pending.items()):
            self.landed[(name, l, part)] = all_to_all(f"exchange_{name}_{l}_{part}", blocks)
            del self.pending[(name, l, part)]


def local_step(x, target, wt, tr):
    n, d = x.shape
    depth = wt["pre_mix_norm"].shape[0]
    lps = [_layer_params(wt, l) for l in range(depth)]
    saved = []
    xin = x

    def mm(name, a, b, mode, dtype, tm, tn, tk, gather=(), exchange=(), **split):
        jobs, keys = tr.jobs(gather, exchange)
        if not jobs:
            return matmul(name, a, b, mode, dtype, tm, tn, tk, **split)
        out, res = matmul(name, a, b, mode, dtype, tm, tn, tk, side=jobs, **split)
        tr.done(keys, res)
        return out

    (h,), _ = seq_fwd("norm_fwd", fn_norm, n, TS_ROW, [_rin(x)], [_par(lps[0]["pre_mix"])], [_out(d, BF16)])
    dx_last = loss = None
    for l in range(depth):
        lp = lps[l]
        p = mm(f"w_in_fwd_{l}", h, tr.weight("w_in", l), "nn", F32, 2048, N_INP // 5, d,
               gather=[("ffn_up", l, 0)])
        jobs, keys = tr.jobs(gather=[("ffn_up", l, 1)])
        jobs_lru, keys_lru = tr.jobs(gather=[("w_out", l, 0)])
        ymix, mix_ins, res, res_lru = _mixers_fwd(l, p, lp, n, jobs, jobs_lru)
        tr.done(keys, res)
        tr.done(keys_lru, res_lru)
        y = mm(f"w_out_fwd_{l}", ymix, tr.weight("w_out", l), "nn", BF16, 2048, 1024, d)
        res_ps = [_par(lp["post_mix"]), _par(lp["pre_ffn"])]
        (x1, h2), _ = seq_fwd(f"res_mix_fwd_{l}", fn_res, n, TS_ROW, [_rin(xin), _rin(y)], res_ps,
                              [_out(d, F32), _out(d, BF16)])
        nxt = l + 1 < depth
        hid = mm(f"ffn_up_fwd_{l}", h2, tr.weight("ffn_up", l), "nn", F32, 2048, 1024, d,
                 gather=[("ffn_down", l, 0)] + ([("w_out", l + 1, 0)] if nxt else []))
        f_xs, f_ps, d_ffh = _ffn_ops(hid, lp)
        jobs, keys = tr.jobs(gather=[("w_in", l + 1, 0)] if nxt else [])
        res = seq_fwd(f"ffn_act_fwd_{l}", fn_ffn, n, TS_FFN, f_xs, f_ps,
                      [_out(TC_FFN, BF16, d_ffh, lambda j: j)], ncol=d_ffh // TC_FFN, side=jobs)
        (act,) = res[0]
        tr.done(keys, res[2] if jobs else [])
        yf = mm(f"ffn_down_fwd_{l}", act, tr.weight("ffn_down", l), "nn", BF16, 1024, 1024, d_ffh // 2)
        rec = dict(x=xin, h=h, mix_ins=mix_ins, ymix=ymix, y=y, x1=x1, h2=h2, f_xs=f_xs, f_ps=f_ps, act=act, yf=yf)
        if l + 1 < depth:
            ps = [_par(lp["post_ffn"]), _par(lps[l + 1]["pre_mix"])]
            (x2, h), _ = seq_fwd(f"res_ffn_fwd_{l}", fn_res, n, TS_ROW, [_rin(x1), _rin(yf)], ps,
                                 [_out(d, F32), _out(d, BF16)])
            rec["res_ffn_ps"] = ps
            xin = x2
        else:
            def body(xs, blks, ps, carries, r):
                x1_, yf_, t_ = xs
                e = x1_ + _rms(yf_.astype(F32), ps[0]) - t_
                part = 0.5 * jnp.sum(jnp.mean(e * e, axis=-1, keepdims=True), axis=0, keepdims=True)
                return [e * (1.0 / d)], [], [jnp.broadcast_to(part, (8, 128))], []

            ps = [_par(lp["post_ffn"])]
            dx_last, loss = _block_call("loss_fwd", body, n, TS_ROW, 1, False, [_rin(x1), _rin(yf), _rin(target)],
                                        [], ps, [_out(d, F32)], [], [dict(shape=(8, 128), total=None, col=None)], [])
            rec["res_ffn_ps"] = ps
        saved.append(rec)

    grads = {}
    dx2, dh_next = dx_last, None
    for l in reversed(range(depth)):
        rec, lp = saved[l], lps[l]
        d_ffh = lp["d_ffh"]
        g = {}
        if dh_next is None:
            (dx1, dyf), (g["post_ffn"],) = seq_bwd(f"res_ffn_bwd_{l}", fn_res_last, n, TS_ROW,
                                                   [_rin(rec["x1"]), _rin(rec["yf"])], rec["res_ffn_ps"], [_rin(dx2)],
                                                   din_dtypes=[F32, BF16])
        else:
            (dx1, dyf), (g["post_ffn"], g_next_pre) = seq_bwd(
                f"res_ffn_bwd_{l}", fn_res, n, TS_ROW, [_rin(rec["x1"]), _rin(rec["yf"])], rec["res_ffn_ps"],
                [_rin(dx2), _rin(dh_next)], din_dtypes=[F32, BF16])
            grads[l + 1]["pre_mix"] = g_next_pre
        tr.grad("ffn_down", l, mm(f"ffn_down_dw_{l}", rec["act"], dyf, "tn", BF16, d_ffh // 4, 1024, 2048))

        def dact(cot_blocks, aux_blocks):
            return [_dot(cot_blocks[0], aux_blocks[0], 1, 1).astype(BF16)]

        (dhg, dhv), (g_cwg, g_cwv, g_cbg, g_cbv) = seq_bwd(
            f"ffn_act_bwd_{l}", fn_ffn, n, TS_FFN_BWD, rec["f_xs"], rec["f_ps"], [_rin(dyf)],
            din_dtypes=[BF16, BF16], ncol=d_ffh // TC_FFN, din_specs=[(d_ffh, lambda j: j), (d_ffh, lambda j: j)],
            cot_map=dact, aux=[_par(tr.weight("ffn_down", l), TC_FFN, row=lambda j: j)])
        g["ffn_cw"] = jnp.concatenate([g_cwg[:, :d_ffh], g_cwv[:, d_ffh:]], axis=1)
        g["ffn_cb"] = jnp.concatenate([g_cbg[:, :d_ffh], g_cbv[:, d_ffh:]], axis=1)
        dh2 = mm(f"ffn_up_dx_{l}", dhg, tr.weight("ffn_up", l), "nt", BF16, 1024, 1024, d_ffh // 2,
                 exchange=[("ffn_down", l, 0)], a2=dhv)
        tr.grad("ffn_up", l, mm(f"ffn_up_dw_{l}", rec["h2"], dhg, "tn", BF16, 1024, 2 * d_ffh // N_DEV, 2048, b2=dhv,
                                col_blocks=True))
        (dx, dy), (g["post_mix"], g["pre_ffn"]) = seq_bwd(
            f"res_mix_bwd_{l}", fn_res, n, TS_ROW, [_rin(rec["x"]), _rin(rec["y"])],
            [_par(lp["post_mix"]), _par(lp["pre_ffn"])], [_rin(dx1), _rin(dh2)], din_dtypes=[F32, BF16])
        dymix = mm(f"w_out_dx_{l}", dy, tr.weight("w_out", l), "nt", BF16, 2048, 1024, d)
        tr.grad("w_out", l, mm(f"w_out_dw_{l}", rec["ymix"], dy, "tn", BF16, 1024, 1024, 2048))
        jobs, keys = tr.jobs(exchange=[("ffn_up", l, 0), ("w_out", l, 0)])
        dp, g["mix"], res = _mixers_bwd(l, dymix, rec["mix_ins"], n, jobs)
        tr.done(keys, res)
        tr.grad("w_in", l, mm(f"w_in_dw_{l}", rec["h"], dp, "tn", BF16, 1024, N_INP // 5, 2048,
                              exchange=[("ffn_up", l, 1)]))
        dh = mm(f"w_in_dx_{l}", dp, tr.weight("w_in", l), "nt", BF16, 1024, 1024, N_INP // 3,
                exchange=[("w_in", l, 0)])
        grads[l] = g
        dx2, dh_next = dx, dh
    (grad_x,), (g_pre0,) = seq_bwd("norm_bwd", fn_norm_keep, n, TS_ROW, [_rin(x)], [_par(lps[0]["pre_mix"])],
                                   [_rin(dh_next), _rin(dx2)])
    grads[0]["pre_mix"] = g_pre0
    tr.flush()
    return loss[0, 0], grad_x, _name_grads(grads, depth)


def _name_grads(grads, depth):
    per = {k: [] for k in SMALL}
    for l in range(depth):
        g = grads[l]
        m = g["mix"]
        cw, gp, nw = m["gdn"]
        lcw, lcb, lwa, lba, lwx, lbx, llam, gw0 = m["lru"]
        lnw, lnb, ws, bst, gw1 = m["sgu"]
        scw, gw2 = m["sc"]
        per["pre_mix_norm"].append(g["pre_mix"][0])
        per["gdn_conv_w"].append(cw)
        per["gdn_a_log"].append(gp[0, 4:8])
        per["gdn_dt_bias"].append(gp[1, 4:8])
        per["gdn_norm_w"].append(nw[0])
        per["lru_conv_w"].append(lcw)
        per["lru_conv_b"].append(lcb[0])
        per["lru_wa"].append(_block_diag_grad(lwa, LRU_BLOCKS))
        per["lru_ba"].append(lba.reshape(LRU_BLOCKS, -1))
        per["lru_wx"].append(_block_diag_grad(lwx, LRU_BLOCKS))
        per["lru_bx"].append(lbx.reshape(LRU_BLOCKS, -1))
        per["lru_lambda"].append(llam[0])
        per["sgu_ln_w"].append(lnw[0])
        per["sgu_ln_b"].append(lnb[0])
        per["sgu_ws"].append(ws.reshape(SGU_GROUPS, SGU_CHUNK, SGU_CHUNK))
        per["sgu_b"].append(bst[:, :SGU_GROUPS].T)
        per["sconv_w"].append(scw)
        per["grp_norm_w"].append(jnp.concatenate([gw0, gw1, gw2], axis=0))
        per["post_mix_norm"].append(g["post_mix"][0])
        per["pre_ffn_norm"].append(g["pre_ffn"][0])
        per["ffn_conv_w"].append(g["ffn_cw"])
        per["ffn_conv_b"].append(g["ffn_cb"][0])
        per["post_ffn_norm"].append(g["post_ffn"][0])
    return {k: jnp.stack(v) for k, v in per.items()}


def _regroup_w_in(w):
    pad = jnp.zeros(w.shape[:-1] + (N_INP - N_IN,), w.dtype)
    return jnp.concatenate([w[..., :2048], w[..., 2056:], w[..., 2048:2056], pad], axis=-1)


def _ungroup_w_in(g):
    return jnp.concatenate([g[..., :2048], g[..., BA_COL:BA_COL + 8], g[..., 2048:BA_COL]], axis=-1)


def kernel(x, pre_mix_norm, w_in, gdn_conv_w, gdn_a_log, gdn_dt_bias, gdn_norm_w, lru_conv_w, lru_conv_b, lru_wa, lru_ba, lru_wx, lru_bx, lru_lambda, sgu_ln_w, sgu_ln_b, sgu_ws, sgu_b, sconv_w, grp_norm_w, w_out, post_mix_norm, pre_ffn_norm, ffn_up, ffn_conv_w, ffn_conv_b, ffn_down, post_ffn_norm, loss_target, m_pre_mix_norm, m_w_in, m_gdn_conv_w, m_gdn_a_log, m_gdn_dt_bias, m_gdn_norm_w, m_lru_conv_w, m_lru_conv_b, m_lru_wa, m_lru_ba, m_lru_wx, m_lru_bx, m_lru_lambda, m_sgu_ln_w, m_sgu_ln_b, m_sgu_ws, m_sgu_b, m_sconv_w, m_grp_norm_w, m_w_out, m_post_mix_norm, m_pre_ffn_norm, m_ffn_up, m_ffn_conv_w, m_ffn_conv_b, m_ffn_down, m_post_ffn_norm, v_pre_mix_norm, v_w_in, v_gdn_conv_w, v_gdn_a_log, v_gdn_dt_bias, v_gdn_norm_w, v_lru_conv_w, v_lru_conv_b, v_lru_wa, v_lru_ba, v_lru_wx, v_lru_bx, v_lru_lambda, v_sgu_ln_w, v_sgu_ln_b, v_sgu_ws, v_sgu_b, v_sconv_w, v_grp_norm_w, v_w_out, v_post_mix_norm, v_pre_ffn_norm, v_ffn_up, v_ffn_conv_w, v_ffn_conv_b, v_ffn_down, v_post_ffn_norm):
    args = locals()
    w_loc = {k: args[k] for k in WEIGHTS}
    m_loc = {k: args["m_" + k] for k in WEIGHTS}
    v_loc = {k: args["v_" + k] for k in WEIGHTS}
    depth = pre_mix_norm.shape[0]
    x_, y_, c_ = _my_pos()
    me = 4 * x_ + 2 * y_ + c_

    tr = _Traffic(shards={name: cast_bf16(f"cast_{name}", w_loc[name]) for name in BIG})
    wt = {k: w_loc[k] for k in SMALL}
    shard_shapes = [w_loc[k].shape for k in SHARDED_SMALL]
    gathered = all_gather("gather_small", _pack([w_loc[k] for k in SHARDED_SMALL]), True)
    for k, a in zip(SHARDED_SMALL, _unpack(gathered, shard_shapes, lead=(N_DEV,))):
        a = jnp.moveaxis(a, 0, -2)
        wt[k] = a.reshape(a.shape[:-2] + (-1,))

    loss_part, grad_x, g_full = local_step(x[0], loss_target[0], wt, tr)
    loss = lax.psum(loss_part, ("x", "y", "c"))

    outs_g, outs_d, outs_m, outs_v = {}, {}, {}, {}
    for name in BIG:
        lands = {(l, part): a for (n_, l, part), a in tr.landed.items() if n_ == name}
        outs_g[name], outs_d[name], outs_m[name], outs_v[name] = adamw_big(
            f"adamw_{name}", w_loc[name], lands, m_loc[name], v_loc[name])

    full_shapes = [g_full[k].shape for k in SMALL]
    g_all = all_gather("gather_small_grads", _pack([g_full[k] for k in SMALL]), True)
    g_sum = _unpack(sum_blocks("sum_small_grads", g_all), full_shapes)
    g_small = {}
    for k, g in zip(SMALL, g_sum):
        if k in SHARDED_SMALL:
            w = w_loc[k].shape[-1]
            g = lax.dynamic_slice_in_dim(g, me * w, w, axis=g.ndim - 1)
        g_small[k] = g
    d_s, m_s, v_s = adamw_small("adamw_small", [w_loc[k] for k in SMALL], [g_small[k] for k in SMALL],
                                [m_loc[k] for k in SMALL], [v_loc[k] for k in SMALL])
    for k_i, k in enumerate(SMALL):
        outs_g[k], outs_d[k], outs_m[k], outs_v[k] = g_small[k], d_s[k_i], m_s[k_i], v_s[k_i]

    return (loss, grad_x[None], *[outs_g[k] for k in WEIGHTS], *[outs_d[k] for k in WEIGHTS],
            *[outs_m[k] for k in WEIGHTS], *[outs_v[k] for k in WEIGHTS])
```

```python
import functools
import math

import jax
import jax.numpy as jnp
from jax import lax
from jax.experimental import pallas as pl
from jax.experimental.pallas import tpu as pltpu

F32 = jnp.float32
BF16 = jnp.bfloat16
EPS = 1e-6
HALO = 8
VMEM_LIMIT = 56 * 1024 * 1024
MESH = pl.DeviceIdType.MESH
N_DEV = 8

ADAM_LR, ADAM_B1, ADAM_B2, ADAM_EPS, ADAM_WD, ADAM_STEP = 0.001, 0.9, 0.999, 1e-08, 0.01, 10

GDN_HEADS, GDN_DIM, GDN_CHUNK = 4, 128, 64
SGU_GROUPS, SGU_CHUNK = 4, 128
LRU_BLOCKS, LRU_C = 8, 8.0
D_G = 512
N_IN = 5640
N_INP = 5760
BA_COL = 5632

SHARDED_SMALL = ("gdn_conv_w", "lru_conv_w", "sconv_w", "grp_norm_w", "ffn_conv_w")
BIG = ("w_in", "w_out", "ffn_up", "ffn_down")
WEIGHTS = ("pre_mix_norm", "w_in", "gdn_conv_w", "gdn_a_log", "gdn_dt_bias", "gdn_norm_w", "lru_conv_w",
           "lru_conv_b", "lru_wa", "lru_ba", "lru_wx", "lru_bx", "lru_lambda", "sgu_ln_w", "sgu_ln_b", "sgu_ws",
           "sgu_b", "sconv_w", "grp_norm_w", "w_out", "post_mix_norm", "pre_ffn_norm", "ffn_up", "ffn_conv_w",
           "ffn_conv_b", "ffn_down", "post_ffn_norm")
SMALL = tuple(n for n in WEIGHTS if n not in BIG)


def _dot(a, b, ca, cb):
    return lax.dot_general(a.astype(BF16), b.astype(BF16), (((ca,), (cb,)), ((), ())),
                           preferred_element_type=F32)


@jax.custom_vjp
def _mm(a, b):
    return _dot(a, b, 1, 0)


def _mm_f(a, b):
    return _dot(a, b, 1, 0), (a, b)


def _mm_b(res, g):
    a, b = res
    return _dot(g, b, 1, 1), _dot(a, g, 0, 0)


_mm.defvjp(_mm_f, _mm_b)


@jax.custom_vjp
def _mm_nt(a, b):
    return _dot(a, b, 1, 1)


def _mm_nt_f(a, b):
    return _dot(a, b, 1, 1), (a, b)


def _mm_nt_b(res, g):
    a, b = res
    return _dot(g, b, 1, 0), _dot(g, a, 0, 0)


_mm_nt.defvjp(_mm_nt_f, _mm_nt_b)


@jax.custom_vjp
def _mm_tn(a, b):
    return _dot(a, b, 0, 0)


def _mm_tn_f(a, b):
    return _dot(a, b, 0, 0), (a, b)


def _mm_tn_b(res, g):
    a, b = res
    return _dot(b, g, 1, 1), _dot(a, g, 1, 0)


_mm_tn.defvjp(_mm_tn_f, _mm_tn_b)


def _dot_exact(a, b, ca, cb):
    return lax.dot_general(a, b, (((ca,), (cb,)), ((), ())), precision=lax.Precision.HIGHEST,
                           preferred_element_type=F32)


@functools.partial(jax.custom_vjp, nondiff_argnums=(1,))
def _shift_rows(x, s):
    return pltpu.roll(x, s, 0)


def _shift_rows_f(x, s):
    return pltpu.roll(x, s, 0), None


def _shift_rows_b(s, _, g):
    return (pltpu.roll(g, (g.shape[0] - s) % g.shape[0], 0),)


_shift_rows.defvjp(_shift_rows_f, _shift_rows_b)


def _sigmoid(x):
    return 1.0 / (1.0 + jnp.exp(-x))


def _silu(x):
    return x * _sigmoid(x)


GELU_C, GELU_A = 0.7978845608028654, 0.044715


@jax.custom_vjp
def _gelu(x):
    return 0.5 * x * (1.0 + jnp.tanh(GELU_C * (x + GELU_A * (x * x * x))))


def _gelu_f(x):
    t = jnp.tanh(GELU_C * (x + GELU_A * (x * x * x)))
    return 0.5 * x * (1.0 + t), (x, t)


def _gelu_b(res, g):
    x, t = res
    slope = 0.5 * (1.0 + t) + (0.5 * GELU_C) * x * (1.0 - t * t) * (1.0 + (3.0 * GELU_A) * (x * x))
    return (g * slope,)


_gelu.defvjp(_gelu_f, _gelu_b)


@jax.custom_vjp
def _softplus(x):
    e = jnp.exp(-jnp.abs(x))
    u = 1.0 + e
    log1p = jnp.where(u == 1.0, e, jnp.log(u) * (e / jnp.where(u == 1.0, 1.0, u - 1.0)))
    return jnp.maximum(x, 0.0) + log1p


def _softplus_f(x):
    return _softplus(x), x


def _softplus_b(x, g):
    return (g * _sigmoid(x),)


_softplus.defvjp(_softplus_f, _softplus_b)


def _neg_expm1(y):
    return -jnp.tanh(0.5 * y) * (jnp.exp(y) + 1.0)


def _rms(x, w):
    return x * lax.rsqrt(jnp.mean(x * x, axis=-1, keepdims=True) + EPS) * w


def _row(w, k):
    sel = lax.broadcasted_iota(jnp.int32, w.shape, 0) == k
    return jnp.sum(jnp.where(sel, w, 0.0), axis=0, keepdims=True)


def _col(x, j):
    sel = lax.broadcasted_iota(jnp.int32, x.shape, 1) == j
    return jnp.sum(jnp.where(sel, x, 0.0), axis=1, keepdims=True)


def _conv(x_ext, w, taps):
    acc = None
    for k in range(taps):
        s = taps - 1 - k
        t = (x_ext if s == 0 else _shift_rows(x_ext, s)) * _row(w, k)
        acc = t if acc is None else acc + t
    return acc[HALO:]


@jax.custom_vjp
def _scan(a, b, h0):
    n = a.shape[0]
    row = lax.broadcasted_iota(jnp.int32, a.shape, 0)
    s = 1
    while s < n:
        keep = row >= s
        a_sh = jnp.where(keep, pltpu.roll(a, s, 0), 1.0)
        b_sh = jnp.where(keep, pltpu.roll(b, s, 0), 0.0)
        b = a * b_sh + b
        a = a * a_sh
        s *= 2
    return b + a * h0


def _scan_f(a, b, h0):
    h = _scan(a, b, h0)
    return h, (a, h, h0)


def _scan_b(res, dh):
    a, h, h0 = res
    n = a.shape[0]
    row = lax.broadcasted_iota(jnp.int32, a.shape, 0)
    an = jnp.where(row < n - 1, pltpu.roll(a, n - 1, 0), 0.0)
    lam = dh
    s = 1
    while s < n:
        keep = row < n - s
        a_sh = jnp.where(keep, pltpu.roll(an, n - s, 0), 1.0)
        l_sh = jnp.where(keep, pltpu.roll(lam, n - s, 0), 0.0)
        lam = an * l_sh + lam
        an = an * a_sh
        s *= 2
    h_prev = jnp.where(row >= 1, pltpu.roll(h, 1, 0), h0)
    al = a * lam
    dh0 = jnp.sum(jnp.where(row == 0, al, 0.0), axis=0, keepdims=True)
    return lam * h_prev, lam, dh0


_scan.defvjp(_scan_f, _scan_b)


@jax.custom_vjp
def _unit_lower_inverses(ms):
    n = ms[0].shape[0]
    shape = ms[0].shape
    eye = (lax.broadcasted_iota(jnp.int32, shape, 0) == lax.broadcasted_iota(jnp.int32, shape, 1)).astype(F32)
    p = [-m for m in ms]
    t = [eye + a for a in p]
    steps = 1
    while 2 ** steps < n:
        p = [_mm(a, a) for a in p]
        t = [a + _mm(a, c) for a, c in zip(t, p)]
        steps += 1
    return t


def _unit_lower_inverses_f(ms):
    t = _unit_lower_inverses(ms)
    return t, t


def _unit_lower_inverses_b(t, dt):
    x = [_mm_nt(g, a) for g, a in zip(dt, t)]
    return ([-_mm_tn(a, c) for a, c in zip(t, x)],)


_unit_lower_inverses.defvjp(_unit_lower_inverses_f, _unit_lower_inverses_b)


def _last_row(x):
    sel = lax.broadcasted_iota(jnp.int32, x.shape, 0) == x.shape[0] - 1
    return jnp.sum(jnp.where(sel, x, 0.0), axis=0, keepdims=True)


def fn_norm(xs, st, ps):
    (x,), (w,) = xs, ps
    return [_rms(x, w).astype(BF16)], []


def fn_norm_keep(xs, st, ps):
    (x,), (w,) = xs, ps
    return [_rms(x, w).astype(BF16), x], []


def fn_res(xs, st, ps):
    (x, y), (w_post, w_next) = xs, ps
    x1 = x + _rms(y.astype(F32), w_post)
    return [x1, _rms(x1, w_next).astype(BF16)], []


def fn_res_last(xs, st, ps):
    (x, y), (w_post,) = xs, ps
    return [x + _rms(y.astype(F32), w_post)], []


def fn_gdn(xs, st, ps):
    qkv_ext, z, ba = xs
    (state,) = st
    cw, gp, nw = ps
    ts = z.shape[0]
    qkv = _silu(_conv(qkv_ext, cw, 4))
    beta_all = _sigmoid(ba)
    g_all = -jnp.exp(_row(gp, 0)) * _softplus(ba + _row(gp, 1))
    c_n = GDN_CHUNK
    ri = lax.broadcasted_iota(jnp.int32, (c_n, c_n), 0)
    ci = lax.broadcasted_iota(jnp.int32, (c_n, c_n), 1)
    causal, strict = ri >= ci, ri > ci
    tril = causal.astype(F32)
    lane = lax.broadcasted_iota(jnp.int32, (c_n, 128), 1)
    s_h = [state[GDN_DIM * h:GDN_DIM * (h + 1)] for h in range(GDN_HEADS)]
    n_c = ts // c_n
    pairs = [(c, h) for c in range(n_c) for h in range(GDN_HEADS)]
    every = lambda f, *lists: [f(*a) for a in zip(*lists)]

    def piece(c, h, base):
        return qkv[c * c_n:(c + 1) * c_n, base + GDN_DIM * h:base + GDN_DIM * (h + 1)]

    q = [piece(c, h, 0) for c, h in pairs]
    k = [piece(c, h, D_G) for c, h in pairs]
    v = [piece(c, h, 2 * D_G) for c, h in pairs]
    q = every(lambda t: t * lax.rsqrt(jnp.sum(t * t, axis=-1, keepdims=True) + EPS) * (GDN_DIM ** -0.5), q)
    k = every(lambda t: t * lax.rsqrt(jnp.sum(t * t, axis=-1, keepdims=True) + EPS), k)
    gcum_all = [_dot_exact(tril, g_all[c * c_n:(c + 1) * c_n], 1, 0) for c in range(n_c)]
    b = [_col(beta_all[c * c_n:(c + 1) * c_n], h) for c, h in pairs]
    gc = [_col(gcum_all[c], 4 + h) for c, h in pairs]
    gr = [_dot_exact((lane == 4 + h).astype(F32), gcum_all[c], 1, 1) for c, h in pairs]
    decay = every(lambda a, r: jnp.where(causal, jnp.exp(jnp.where(causal, a - r, 0.0)), 0.0), gc, gr)
    kb = every(lambda a, c: a * c, k, b)
    mk = every(lambda a, c, e: _mm_nt(jnp.concatenate([a, c], axis=0), e), kb, q, k)
    m = every(lambda a, dcy: jnp.where(strict, a[:c_n] * dcy, 0.0), mk, decay)
    attn = every(lambda a, dcy: jnp.where(causal, a[c_n:] * dcy, 0.0), mk, decay)
    t_ = _unit_lower_inverses(m)
    eg = every(jnp.exp, gc)
    wu = every(lambda t, a, e, c, d: _mm(t, jnp.concatenate([a * e, c * d], axis=1)), t_, kb, eg, v, b)
    g_last = every(_last_row, gc)
    k_g = every(lambda a, gl, g: a * jnp.exp(gl - g), k, g_last, gc)
    wq = every(lambda a, c, e: jnp.concatenate([a[:, :GDN_DIM], c * e], axis=0), wu, q, eg)
    u = [a[:, GDN_DIM:] for a in wu]
    gl = every(jnp.exp, g_last)

    o = []
    for c in range(n_c):
        idx = range(c * GDN_HEADS, (c + 1) * GDN_HEADS)
        ws = [_mm(wq[i], s_h[h]) for h, i in enumerate(idx)]
        v_new = [u[i] - ws[h][:c_n] for h, i in enumerate(idx)]
        av = [_mm(attn[i], v_new[h]) for h, i in enumerate(idx)]
        kv = [_mm_tn(k_g[i], v_new[h]) for h, i in enumerate(idx)]
        o += [ws[h][c_n:] + av[h] for h in range(GDN_HEADS)]
        s_h = [s_h[h] * gl[i] + kv[h] for h, i in enumerate(idx)]
    zz = [z[c * c_n:(c + 1) * c_n, GDN_DIM * h:GDN_DIM * (h + 1)] for c, h in pairs]
    y = every(lambda a, g: a * lax.rsqrt(jnp.mean(a * a, axis=-1, keepdims=True) + EPS) * nw * _silu(g), o, zz)
    rows = [jnp.concatenate(y[c * GDN_HEADS:(c + 1) * GDN_HEADS], axis=1) for c in range(n_c)]
    y = rows[0] if n_c == 1 else jnp.concatenate(rows, axis=0)
    return [y.astype(BF16)], [jnp.concatenate(s_h, axis=0)]


def fn_lru(xs, st, ps):
    x_ext, gate = xs
    (h0,) = st
    cw, cb, wa, ba, wx, bx, lam, gw = ps
    xc = _conv(x_ext, cw, 4) + cb
    r = _sigmoid(_mm(xc, wa) + ba)
    i = _sigmoid(_mm(xc, wx) + bx)
    log_a = -LRU_C * r * _softplus(-lam)
    a = jnp.exp(log_a)
    mult = jnp.sqrt(_neg_expm1(2.0 * log_a))
    h = _scan(a, mult * (i * xc), h0)
    y = _rms(h * _gelu(gate), gw)
    return [y.astype(BF16)], [_last_row(h)]


def fn_sgu(xs, st, ps):
    (uv,) = xs
    lnw, lnb, ws, bst, gw = ps
    ts = uv.shape[0]
    uvf = _gelu(uv)
    u, v = uvf[:, :D_G], uvf[:, D_G:]
    vc = v - jnp.mean(v, axis=-1, keepdims=True)
    v = vc * lax.rsqrt(jnp.mean(vc * vc, axis=-1, keepdims=True) + EPS) * lnw + lnb
    t_n = SGU_CHUNK
    tril = lax.broadcasted_iota(jnp.int32, (t_n, t_n), 0) >= lax.broadcasted_iota(jnp.int32, (t_n, t_n), 1)
    wg = [jnp.where(tril, ws[t_n * g:t_n * (g + 1)], 0.0) for g in range(SGU_GROUPS)]
    bg = [_col(bst, g) for g in range(SGU_GROUPS)]
    rows = []
    for c in range(ts // t_n):
        vcg = v[c * t_n:(c + 1) * t_n]
        rows.append(jnp.concatenate(
            [_mm(wg[g], vcg[:, 128 * g:128 * (g + 1)]) + bg[g] for g in range(SGU_GROUPS)], axis=1))
    vv = rows[0] if len(rows) == 1 else jnp.concatenate(rows, axis=0)
    return [_rms(u * vv, gw).astype(BF16)], []


def fn_sconv(xs, st, ps):
    bg, cg_ext, hh_ext = xs
    cw, gw = ps
    return [_rms(bg * _conv(cg_ext * hh_ext, cw, 3), gw).astype(BF16)], []


def fn_ffn(xs, st, ps):
    g_ext, v_ext = xs
    cwg, cwv, cbg, cbv = ps
    outs = []
    for c0 in range(0, g_ext.shape[1], FFN_LANES):
        cols = slice(c0, c0 + FFN_LANES)
        g = _conv(g_ext[:, cols], cwg[:, cols], 3) + cbg[:, cols]
        v = _conv(v_ext[:, cols], cwv[:, cols], 3) + cbv[:, cols]
        outs.append((_gelu(g) * v).astype(BF16))
    return [outs[0] if len(outs) == 1 else jnp.concatenate(outs, axis=1)], []


def _my_pos():
    return lax.axis_index("x"), lax.axis_index("y"), lax.axis_index("c")


def _peer(pos, k):
    x_, y_, c_ = pos
    return (1 - x_ if (k >> 2) & 1 else x_, 1 - y_ if (k >> 1) & 1 else y_, 1 - c_ if k & 1 else c_)


def _dev_index(p):
    return 4 * p[0] + 2 * p[1] + p[2]


class _Side:
    def __init__(self, jobs):
        self.jobs = list(jobs)
        n = len(self.jobs)
        self.operands = [a for _, a in self.jobs]
        self.in_specs = [pl.BlockSpec(memory_space=pl.ANY)] * n
        self.out_shape = [jax.ShapeDtypeStruct(((N_DEV,) + a.shape) if kind == "gather" else a.shape, a.dtype)
                          for kind, a in self.jobs]
        self.out_specs = [pl.BlockSpec(memory_space=pl.ANY)] * n
        self.scratch = [pltpu.SemaphoreType.DMA((7 * n,)), pltpu.SemaphoreType.DMA((7 * n,)),
                        pltpu.SemaphoreType.DMA((n,))] if n else []

    def _copies(self, in_refs, out_refs, sems, landings=True):
        send, recv, local = sems
        pos = _my_pos()
        me = _dev_index(pos)
        mine, outgoing, landing = [], [], []
        for j, (kind, _) in enumerate(self.jobs):
            src, dst = in_refs[j], out_refs[j]
            own = src if kind == "gather" else src.at[me]
            mine.append(pltpu.make_async_copy(own, dst.at[me], local.at[j]))
            for k in range(1, N_DEV):
                p = _peer(pos, k)
                sems_k = dict(send_sem=send.at[7 * j + k - 1], recv_sem=recv.at[7 * j + k - 1], device_id=p,
                              device_id_type=MESH)
                outgoing.append(pltpu.make_async_remote_copy(
                    src_ref=src if kind == "gather" else src.at[_dev_index(p)], dst_ref=dst.at[me], **sems_k))
                if landings:
                    landing.append(pltpu.make_async_remote_copy(src_ref=own, dst_ref=dst.at[_dev_index(p)], **sems_k))
        return mine, outgoing, landing

    def start(self, in_refs, out_refs, sems):
        mine, outgoing, _ = self._copies(in_refs, out_refs, sems, landings=False)
        for cp in mine + outgoing:
            cp.start()

    def wait(self, in_refs, out_refs, sems):
        mine, outgoing, landing = self._copies(in_refs, out_refs, sems)
        for cp in landing:
            cp.wait_recv()
        for cp in outgoing:
            cp.wait_send()
        for cp in mine:
            cp.wait()


def _rin(arr, w=None, col=0, halo=False):
    return dict(arr=arr, w=arr.shape[1] if w is None else w, col=col, halo=halo)


def _par(arr, w=None, col=None, row=None):
    return dict(arr=arr, w=w, col=col, row=row)


def _colidx(col, j):
    return col(j) if callable(col) else col


def _block_call(name, body, n_rows, ts, ncol, reverse, row_ins, blk_ins, params, row_outs, blk_outs, acc_outs,
                carries, side=()):
    side = _Side(side)
    ts = min(ts, n_rows)
    nblk = n_rows // ts
    hb = ts // HALO

    def rr(i):
        return (nblk - 1 - i) if reverse else i

    in_specs, operands = [], []
    for s in row_ins:
        in_specs.append(pl.BlockSpec((ts, s["w"]), lambda j, i, s=s: (rr(i), _colidx(s["col"], j))))
        operands.append(s["arr"])
        if s["halo"]:
            in_specs.append(pl.BlockSpec((HALO, s["w"]),
                                         lambda j, i, s=s: (jnp.maximum(rr(i) * hb - 1, 0), _colidx(s["col"], j))))
            operands.append(s["arr"])
    for a in blk_ins:
        nd = a.ndim - 1
        in_specs.append(pl.BlockSpec((None,) + a.shape[1:], lambda j, i, nd=nd: (rr(i),) + (0,) * nd))
        operands.append(a)
    for p in params:
        a = p["arr"]
        if p["row"] is not None:
            in_specs.append(pl.BlockSpec((p["w"], a.shape[1]), lambda j, i, p=p: (_colidx(p["row"], j), 0)))
        elif p["col"] is None:
            in_specs.append(pl.BlockSpec(a.shape, lambda j, i: (0, 0)))
        else:
            in_specs.append(pl.BlockSpec((a.shape[0], p["w"]), lambda j, i, p=p: (0, _colidx(p["col"], j))))
        operands.append(a)

    out_specs, out_shape = [], []
    for o in row_outs:
        out_specs.append(pl.BlockSpec((ts, o["w"]), lambda j, i, o=o: (rr(i), _colidx(o["col"], j))))
        out_shape.append(jax.ShapeDtypeStruct((n_rows, o["total"]), o["dtype"]))
    for o in blk_outs:
        nd = len(o["shape"])
        out_specs.append(pl.BlockSpec((None,) + tuple(o["shape"]), lambda j, i, nd=nd: (rr(i),) + (0,) * nd))
        out_shape.append(jax.ShapeDtypeStruct((nblk,) + tuple(o["shape"]), o["dtype"]))
    for o in acc_outs:
        if o["col"] is None:
            out_specs.append(pl.BlockSpec(o["shape"], lambda j, i: (0, 0)))
            out_shape.append(jax.ShapeDtypeStruct(o["shape"], F32))
        else:
            out_specs.append(pl.BlockSpec(o["shape"], lambda j, i, o=o: (0, _colidx(o["col"], j))))
            out_shape.append(jax.ShapeDtypeStruct((o["shape"][0], o["total"]), F32))

    n_in = len(operands)
    n_row_out, n_blk_out, n_acc = len(row_outs), len(blk_outs), len(acc_outs)
    n_out = n_row_out + n_blk_out + n_acc
    n_side = len(side.jobs)

    def kern(*refs):
        in_refs = refs[:n_in]
        side_in = refs[n_in:n_in + n_side]
        out_refs = refs[n_in + n_side:n_in + n_side + n_out]
        side_out = refs[n_in + n_side + n_out:n_in + 2 * n_side + n_out]
        scratch = refs[n_in + 2 * n_side + n_out:]
        carry_refs, side_sems = scratch[:len(carries)], scratch[len(carries):]
        acc_refs = out_refs[n_row_out + n_blk_out:]
        i = pl.program_id(1)
        r = rr(i)
        if n_side:
            @pl.when((pl.program_id(0) == 0) & (i == 0))
            def _():
                side.start(side_in, side_out, side_sems)

        @pl.when(i == 0)
        def _():
            for c_ref in carry_refs:
                c_ref[...] = jnp.zeros(c_ref.shape, c_ref.dtype)
            for a_ref in acc_refs:
                a_ref[...] = jnp.zeros(a_ref.shape, a_ref.dtype)

        k = 0
        xs = []
        for s in row_ins:
            x = in_refs[k][...]
            k += 1
            if s["halo"]:
                hal = in_refs[k][...]
                k += 1
                hal = jnp.where(r == 0, jnp.zeros_like(hal), hal)
                x = jnp.concatenate([hal, x], axis=0)
            xs.append(x)
        blks = []
        for _ in blk_ins:
            blks.append(in_refs[k][...])
            k += 1
        ps = []
        for _ in params:
            ps.append(in_refs[k][...])
            k += 1
        row_vals, blk_vals, acc_vals, new_carries = body(xs, blks, ps, [c[...] for c in carry_refs], r)
        for ref, val in zip(out_refs[:n_row_out], row_vals):
            ref[...] = val.astype(ref.dtype)
        for ref, val in zip(out_refs[n_row_out:n_row_out + n_blk_out], blk_vals):
            ref[...] = val.astype(ref.dtype)
        for ref, val in zip(acc_refs, acc_vals):
            ref[...] += val
        for ref, val in zip(carry_refs, new_carries):
            ref[...] = val
        if n_side:
            @pl.when((pl.program_id(0) == ncol - 1) & (i == nblk - 1))
            def _():
                side.wait(side_in, side_out, side_sems)

    res = pl.pallas_call(
        kern,
        name=name,
        grid=(ncol, nblk),
        in_specs=in_specs + side.in_specs,
        out_specs=out_specs + side.out_specs,
        out_shape=out_shape + side.out_shape,
        scratch_shapes=[pltpu.VMEM(shape, F32) for shape in carries] + side.scratch,
        compiler_params=pltpu.CompilerParams(dimension_semantics=("arbitrary", "arbitrary"),
                                             vmem_limit_bytes=VMEM_LIMIT),
    )(*operands, *side.operands)
    return list(res)


def _out(w, dtype, total=None, col=0):
    return dict(w=w, dtype=dtype, total=w if total is None else total, col=col)


def seq_fwd(name, fn, n_rows, ts, row_ins, params, outs, state_shapes=(), ncol=1, side=()):
    def body(xs, blks, ps, carries, r):
        o, new_st = fn(xs, list(carries), ps)
        return o, list(carries), [], new_st

    res = _block_call(name, body, n_rows, ts, ncol, False, row_ins, [], params, outs,
                      [dict(shape=s, dtype=F32) for s in state_shapes], [], list(state_shapes), side)
    n_o, n_s = len(outs), len(state_shapes)
    return (res[:n_o], res[n_o:n_o + n_s]) + ((res[n_o + n_s:],) if side else ())


def seq_bwd(name, fn, n_rows, ts, row_ins, params, cots, saved_states=(), din_dtypes=None, ncol=1, din_specs=None,
            side=(), cot_map=None, aux=()):
    n_x, n_p, n_st = len(row_ins), len(params), len(saved_states)
    halo_idx = [k for k, s in enumerate(row_ins) if s["halo"]]
    state_shapes = [a.shape[1:] for a in saved_states]

    def body(xs_all, blks, ps, carries, r):
        xs, cot_vals = xs_all[:n_x], xs_all[n_x:]
        d_state, d_halo = carries[:n_st], carries[n_st:]
        if cot_map is not None:
            cot_vals = cot_map(cot_vals, ps[n_p:])
        (o, _), vjp = jax.vjp(lambda a, b, c: fn(a, b, c), xs, blks, ps[:n_p])
        cot = [c.astype(v.dtype) for c, v in zip(cot_vals, o)]
        dxs, dst, dps = vjp((cot, list(d_state)))
        row_vals, new_halo = [], []
        for k, dx in enumerate(dxs):
            if k in halo_idx:
                hk = halo_idx.index(k)
                rows = dx.shape[0] - HALO
                tail = dx[rows:] + d_halo[hk]
                row_vals.append(jnp.concatenate([dx[HALO:rows], tail], axis=0))
                new_halo.append(dx[:HALO])
            else:
                row_vals.append(dx)
        return row_vals, [], list(dps), list(dst) + new_halo

    din_dtypes = din_dtypes or [F32] * n_x
    douts = []
    for k, s in enumerate(row_ins):
        total, col = (s["w"], 0) if din_specs is None or din_specs[k] is None else din_specs[k]
        douts.append(_out(s["w"], din_dtypes[k], total, col))
    accs = []
    for p in params:
        a = p["arr"]
        if p["col"] is None:
            accs.append(dict(shape=a.shape, total=None, col=None))
        else:
            accs.append(dict(shape=(a.shape[0], p["w"]), total=a.shape[1], col=p["col"]))
    carries = list(state_shapes) + [(HALO, row_ins[k]["w"]) for k in halo_idx]
    res = _block_call(name, body, n_rows, ts, ncol, True, list(row_ins) + list(cots), list(saved_states),
                      list(params) + list(aux), douts, [], accs, carries, side)
    return (res[:n_x], res[n_x:n_x + n_p]) + ((res[n_x + n_p:],) if side else ())


def matmul(name, a, b, mode, out_dtype, tm, tn, tk, side=(), a2=None, b2=None, col_blocks=False):
    side = _Side(side)
    n_side = len(side.jobs)
    if mode == "tn":
        (kk, m), n = a.shape, b.shape[1]
    else:
        (m, kk), n = a.shape, (b.shape[0] if mode == "nt" else b.shape[1])
    k1, n1 = kk, n
    if a2 is not None:
        assert mode != "tn" and b2 is None
        kk += a2.shape[1]
    if b2 is not None:
        assert mode == "tn"
        n += b2.shape[1]
    tm, tn, tk = min(tm, m), min(tn, n), min(tk, kk)
    nk, gm, gn = kk // tk, m // tm, n // tn
    assert m % tm == 0 and n % tn == 0 and kk % tk == 0 and k1 % tk == 0 and n1 % tn == 0, (name, a.shape, b.shape)
    nk1, gn1 = k1 // tk, n1 // tn
    if mode == "tn":
        a_specs = [pl.BlockSpec((tk, tm), lambda i, j, k: (k, i))]
        b_specs = [pl.BlockSpec((tk, tn), lambda i, j, k: (k, jnp.minimum(j, gn1 - 1)))]
        if b2 is not None:
            b_specs.append(pl.BlockSpec((tk, tn), lambda i, j, k: (k, jnp.maximum(j - gn1, 0))))
    else:
        a_specs = [pl.BlockSpec((tm, tk), lambda i, j, k: (i, jnp.minimum(k, nk1 - 1)))]
        if a2 is not None:
            a_specs.append(pl.BlockSpec((tm, tk), lambda i, j, k: (i, jnp.maximum(k - nk1, 0))))
        b_specs = [pl.BlockSpec((tn, tk), lambda i, j, k: (j, k)) if mode == "nt"
                   else pl.BlockSpec((tk, tn), lambda i, j, k: (k, j))]
    ca, cb = {"nn": (1, 0), "nt": (1, 1), "tn": (0, 0)}[mode]
    n_a, n_b = len(a_specs), len(b_specs)

    def kern(*refs):
        a_refs, b_refs = refs[:n_a], refs[n_a:n_a + n_b]
        rest = refs[n_a + n_b:]
        side_in = rest[:n_side]
        o_ref = rest[n_side]
        side_out = rest[1 + n_side:1 + 2 * n_side]
        acc_ref = rest[1 + 2 * n_side]
        side_sems = rest[2 + 2 * n_side:]
        i, j, k = pl.program_id(0), pl.program_id(1), pl.program_id(2)
        if n_side:
            @pl.when((i == 0) & (j == 0) & (k == 0))
            def _():
                side.start(side_in, side_out, side_sems)

        def step(a_ref, b_ref):
            part = lax.dot_general(a_ref[...], b_ref[...], (((ca,), (cb,)), ((), ())), preferred_element_type=F32)
            if nk == 1:
                o_ref[...] = part.astype(o_ref.dtype)
            else:
                @pl.when(k == 0)
                def _():
                    acc_ref[...] = part

                @pl.when(k > 0)
                def _():
                    acc_ref[...] += part

                @pl.when(k == nk - 1)
                def _():
                    o_ref[...] = acc_ref[...].astype(o_ref.dtype)

        if n_a == 2:
            pl.when(k < nk1)(lambda: step(a_refs[0], b_refs[0]))
            pl.when(k >= nk1)(lambda: step(a_refs[1], b_refs[0]))
        elif n_b == 2:
            pl.when(j < gn1)(lambda: step(a_refs[0], b_refs[0]))
            pl.when(j >= gn1)(lambda: step(a_refs[0], b_refs[1]))
        else:
            step(a_refs[0], b_refs[0])

        if n_side:
            @pl.when((i == gm - 1) & (j == gn - 1) & (k == nk - 1))
            def _():
                side.wait(side_in, side_out, side_sems)

    semantics = ("arbitrary",) * 3 if n_side else ("parallel", "parallel", "arbitrary")
    operands = [a] + ([a2] if a2 is not None else []) + [b] + ([b2] if b2 is not None else [])
    res = pl.pallas_call(
        kern,
        name=name,
        grid=(gm, gn, nk),
        in_specs=a_specs + b_specs + side.in_specs,
        out_specs=[pl.BlockSpec((None, tm, tn), lambda i, j, k: (j, i, 0)) if col_blocks
                   else pl.BlockSpec((tm, tn), lambda i, j, k: (i, j))] + side.out_specs,
        out_shape=[jax.ShapeDtypeStruct((gn, m, tn) if col_blocks else (m, n), out_dtype)] + side.out_shape,
        scratch_shapes=[pltpu.VMEM((tm, tn) if nk > 1 else (8, 128), F32)] + side.scratch,
        compiler_params=pltpu.CompilerParams(dimension_semantics=semantics, vmem_limit_bytes=VMEM_LIMIT),
    )(*operands, *side.operands)
    return (res[0], list(res[1:])) if n_side else res[0]


def all_gather(name, x, in_vmem):
    def body(x_ref, out_ref, send_sems, recv_sems, local_sem):
        x_, y_, c_ = _my_pos()
        me, sibling = (x_, y_, c_), (x_, y_, 1 - c_)
        chips = [(1 - x_, y_), (x_, 1 - y_), (1 - x_, 1 - y_)]

        def slot(px, py, pc):
            return out_ref.at[4 * px + 2 * py + pc]

        def copy(k, block, to, src=None):
            return pltpu.make_async_remote_copy(
                src_ref=slot(*block) if src is None else src, dst_ref=slot(*block),
                send_sem=send_sems.at[k], recv_sem=recv_sems.at[k], device_id=to, device_id_type=MESH)

        mine = pltpu.make_async_copy(x_ref, slot(*me), local_sem)
        mine.start()
        first = [copy(0, me, sibling, src=x_ref)]
        first += [copy(1 + j, me, (*chip, c_), src=x_ref) for j, chip in enumerate(chips)]
        for cp in first:
            cp.start()
        passed = [copy(4 + j, (*chip, c_), sibling) for j, chip in enumerate(chips)]
        for j, chip in enumerate(chips):
            copy(1 + j, (*chip, c_), me).wait_recv()
            passed[j].start()
        copy(0, sibling, me).wait_recv()
        for j, chip in enumerate(chips):
            copy(4 + j, (*chip, 1 - c_), me).wait_recv()
        for cp in first + passed:
            cp.wait_send()
        mine.wait()

    space = pltpu.VMEM if in_vmem else pl.ANY
    return pl.pallas_call(
        body,
        name=name,
        out_shape=jax.ShapeDtypeStruct((N_DEV,) + x.shape, x.dtype),
        in_specs=[pl.BlockSpec(memory_space=space)],
        out_specs=pl.BlockSpec(memory_space=space),
        scratch_shapes=[pltpu.SemaphoreType.DMA((7,)), pltpu.SemaphoreType.DMA((7,)), pltpu.SemaphoreType.DMA],
        compiler_params=pltpu.CompilerParams(vmem_limit_bytes=VMEM_LIMIT),
    )(x)


def all_to_all(name, g):
    def body(g_ref, out_ref, send_sems, recv_sems, local_sem):
        x_, y_, c_ = _my_pos()
        me = 4 * x_ + 2 * y_ + c_

        def peer(k):
            fx, fy, fc = (k >> 2) & 1, (k >> 1) & 1, k & 1
            return (1 - x_ if fx else x_, 1 - y_ if fy else y_, 1 - c_ if fc else c_)

        def copy(k):
            px, py, pc = peer(k)
            return pltpu.make_async_remote_copy(
                src_ref=g_ref.at[4 * px + 2 * py + pc], dst_ref=out_ref.at[me],
                send_sem=send_sems.at[k - 1], recv_sem=recv_sems.at[k - 1], device_id=(px, py, pc), device_id_type=MESH)

        def landing(k):
            px, py, pc = peer(k)
            return pltpu.make_async_remote_copy(
                src_ref=g_ref.at[me], dst_ref=out_ref.at[4 * px + 2 * py + pc],
                send_sem=send_sems.at[k - 1], recv_sem=recv_sems.at[k - 1], device_id=(px, py, pc), device_id_type=MESH)

        mine = pltpu.make_async_copy(g_ref.at[me], out_ref.at[me], local_sem)
        mine.start()
        sends = [copy(k) for k in range(1, N_DEV)]
        for cp in sends:
            cp.start()
        for k in range(1, N_DEV):
            landing(k).wait_recv()
        for cp in sends:
            cp.wait_send()
        mine.wait()

    return pl.pallas_call(
        body,
        name=name,
        out_shape=jax.ShapeDtypeStruct(g.shape, g.dtype),
        in_specs=[pl.BlockSpec(memory_space=pl.ANY)],
        out_specs=pl.BlockSpec(memory_space=pl.ANY),
        scratch_shapes=[pltpu.SemaphoreType.DMA((7,)), pltpu.SemaphoreType.DMA((7,)), pltpu.SemaphoreType.DMA],
    )(g)


def sum_blocks(name, g):
    def body(g_ref, o_ref):
        acc = g_ref[0]
        for s in range(1, N_DEV):
            acc = acc + g_ref[s]
        o_ref[...] = acc

    r = g.shape[1]
    tr = r // 4 if r % 32 == 0 else r
    return pl.pallas_call(
        body, name=name, grid=(r // tr,),
        in_specs=[pl.BlockSpec((N_DEV, tr, 128), lambda i: (0, i, 0))],
        out_specs=pl.BlockSpec((tr, 128), lambda i: (i, 0)),
        out_shape=jax.ShapeDtypeStruct((r, 128), F32),
        compiler_params=pltpu.CompilerParams(vmem_limit_bytes=VMEM_LIMIT),
    )(g)


def _adamw_math(w, g, m, v):
    m = ADAM_B1 * m + (1.0 - ADAM_B1) * g
    v = ADAM_B2 * v + (1.0 - ADAM_B2) * (g * g)
    m_hat = m / (1.0 - ADAM_B1 ** ADAM_STEP)
    v_hat = v / (1.0 - ADAM_B2 ** ADAM_STEP)
    delta = -ADAM_LR * (m_hat / (jnp.sqrt(v_hat) + ADAM_EPS) + ADAM_WD * w)
    return delta, m, v


ADAMW_BLOCK_BYTES = 3 << 19


def _row_tile(rows, row_bytes, limit):
    best = 8
    for t in range(8, rows + 1, 8):
        if rows % t == 0 and t * row_bytes <= limit:
            best = t
    return best


def adamw_big(name, w, lands, m, v):
    depth, r, c = w.shape
    outs = None
    for (l, part), land in sorted(lands.items()):
        rows = land.shape[1]
        tr = _row_tile(rows, 4 * (-(-c // 128) * 128), ADAMW_BLOCK_BYTES)
        first = part * rows // tr

        def body(w_ref, l_ref, m_ref, v_ref, *rest):
            g_out, d_out, m_out, v_out = rest[-4:]
            g = l_ref[0].astype(F32)
            for s in range(1, N_DEV):
                g = g + l_ref[s].astype(F32)
            delta, m_new, v_new = _adamw_math(w_ref[...], g, m_ref[...], v_ref[...])
            g_out[...] = g
            d_out[...] = delta
            m_out[...] = m_new
            v_out[...] = v_new

        spec = pl.BlockSpec((None, tr, c), lambda i, l=l, first=first: (l, first + i, 0))
        carried = [] if outs is None else list(outs)
        outs = pl.pallas_call(
            body, name=f"{name}_{l}_{part}", grid=(rows // tr,),
            in_specs=[spec, pl.BlockSpec((N_DEV, tr, c), lambda i: (0, i, 0)), spec, spec]
            + [pl.BlockSpec(memory_space=pl.ANY)] * len(carried),
            out_specs=[spec] * 4,
            out_shape=[jax.ShapeDtypeStruct((depth, r, c), F32)] * 4,
            input_output_aliases={4 + k: k for k in range(len(carried))},
            compiler_params=pltpu.CompilerParams(dimension_semantics=("parallel",), vmem_limit_bytes=VMEM_LIMIT),
        )(w, land, m, v, *carried)
    return outs


def adamw_small(name, ws, gs, ms, vs):
    n = len(ws)

    def body(*refs):
        ins, outs = refs[:4 * n], refs[4 * n:]
        for k in range(n):
            delta, m_new, v_new = _adamw_math(ins[k][...], ins[n + k][...], ins[2 * n + k][...], ins[3 * n + k][...])
            outs[k][...] = delta
            outs[n + k][...] = m_new
            outs[2 * n + k][...] = v_new

    res = pl.pallas_call(
        body, name=name,
        out_shape=[jax.ShapeDtypeStruct(w.shape, F32) for w in ws] * 3,
        compiler_params=pltpu.CompilerParams(vmem_limit_bytes=VMEM_LIMIT),
    )(*ws, *gs, *ms, *vs)
    return res[:n], res[n:2 * n], res[2 * n:]


def cast_bf16(name, w):
    depth, r, c = w.shape
    tr = _row_tile(r, 4 * (-(-c // 128) * 128), ADAMW_BLOCK_BYTES)

    def body(w_ref, o_ref):
        o_ref[...] = w_ref[...].astype(BF16)

    spec = pl.BlockSpec((None, tr, c), lambda l, i: (l, i, 0))
    return pl.pallas_call(body, name=name, grid=(depth, r // tr), in_specs=[spec], out_specs=spec,
                          out_shape=jax.ShapeDtypeStruct((depth, r, c), BF16),
                          compiler_params=pltpu.CompilerParams(dimension_semantics=("parallel", "parallel")))(w)


def _rows_of(shape):
    return -(-math.prod(shape) // 128)


def _pack(arrs):
    pieces = []
    for a in arrs:
        flat = a.reshape(-1).astype(F32)
        pieces.append(jnp.pad(flat, (0, (-flat.shape[0]) % 128)).reshape(-1, 128))
    rows = sum(p.shape[0] for p in pieces)
    if rows % 8:
        pieces.append(jnp.zeros((8 - rows % 8, 128), F32))
    return jnp.concatenate(pieces, axis=0)


def _unpack(packed, shapes, lead=()):
    out, r0 = [], 0
    for s in shapes:
        rows, n = _rows_of(s), math.prod(s)
        piece = packed[..., r0:r0 + rows, :].reshape(lead + (rows * 128,))
        out.append(piece[..., :n].reshape(lead + tuple(s)))
        r0 += rows
    return out


def _block_diag(w):
    h, d, _ = w.shape
    eye = jnp.eye(h, dtype=w.dtype)
    return (eye[:, None, :, None] * w[:, :, None, :]).reshape(h * d, h * d)


def _block_diag_grad(g, h):
    d = g.shape[0] // h
    eye = jnp.eye(h, dtype=g.dtype)
    return jnp.sum(g.reshape(h, d, h, d) * eye[:, None, :, None], axis=2)


def _layer_params(wt, l):
    gp = jnp.pad(jnp.stack([wt["gdn_a_log"][l], wt["gdn_dt_bias"][l]]), ((0, 6), (4, 128 - 4 - GDN_HEADS)))
    d_ffh = wt["ffn_conv_w"].shape[-1] // 2
    return dict(
        pre_mix=wt["pre_mix_norm"][l][None], post_mix=wt["post_mix_norm"][l][None],
        pre_ffn=wt["pre_ffn_norm"][l][None], post_ffn=wt["post_ffn_norm"][l][None],
        gdn_cw=wt["gdn_conv_w"][l], gdn_gp=gp, gdn_nw=wt["gdn_norm_w"][l][None],
        lru_cw=wt["lru_conv_w"][l], lru_cb=wt["lru_conv_b"][l][None],
        lru_wa=_block_diag(wt["lru_wa"][l]), lru_ba=wt["lru_ba"][l].reshape(1, -1),
        lru_wx=_block_diag(wt["lru_wx"][l]), lru_bx=wt["lru_bx"][l].reshape(1, -1),
        lru_lam=wt["lru_lambda"][l][None], gw0=wt["grp_norm_w"][l, 0][None], gw1=wt["grp_norm_w"][l, 1][None],
        gw2=wt["grp_norm_w"][l, 2][None],
        sgu_lnw=wt["sgu_ln_w"][l][None], sgu_lnb=wt["sgu_ln_b"][l][None],
        sgu_ws=wt["sgu_ws"][l].reshape(SGU_GROUPS * SGU_CHUNK, SGU_CHUNK),
        sgu_bt=jnp.pad(wt["sgu_b"][l].T, ((0, 0), (0, 128 - SGU_GROUPS))),
        sc_cw=wt["sconv_w"][l],
        ffn_cw=wt["ffn_conv_w"][l], ffn_cb=wt["ffn_conv_b"][l][None], d_ffh=d_ffh,
    )


TS_ROW = 256
TS_GDN = 256
TS_FFN = 512
TS_FFN_BWD = 256
TC_FFN = 512
FFN_LANES = 128


def _mixers_fwd(l, p, lp, n, side, side_lru):
    qkv = _rin(p, 3 * D_G, 0, halo=True)
    z = _rin(p, D_G, 3)
    ba = _rin(p, 128, BA_COL // 128)
    gdn_ps = [_par(lp["gdn_cw"]), _par(lp["gdn_gp"]), _par(lp["gdn_nw"])]
    res = seq_fwd(f"gdn_fwd_{l}", fn_gdn, n, TS_GDN, [qkv, z, ba], gdn_ps, [_out(D_G, BF16)],
                  state_shapes=[(GDN_HEADS * GDN_DIM, GDN_DIM)], side=side)
    (y_a,), (gdn_st,), side_res = res if side else res + ([],)
    lru_x = _rin(p, D_G, 4, halo=True)
    lru_gate = _rin(p, D_G, 5)
    lru_ps = [_par(lp[k]) for k in ("lru_cw", "lru_cb", "lru_wa", "lru_ba", "lru_wx", "lru_bx", "lru_lam", "gw0")]
    res = seq_fwd(f"lru_fwd_{l}", fn_lru, n, TS_ROW, [lru_x, lru_gate], lru_ps, [_out(D_G, BF16)],
                  state_shapes=[(1, D_G)], side=side_lru)
    (y_b,), (lru_st,), side_res_lru = res if side_lru else res + ([],)
    uv = _rin(p, 2 * D_G, 3)
    sgu_ps = [_par(lp[k]) for k in ("sgu_lnw", "sgu_lnb", "sgu_ws", "sgu_bt", "gw1")]
    (y_c,), _ = seq_fwd(f"sgu_fwd_{l}", fn_sgu, n, TS_ROW, [uv], sgu_ps, [_out(D_G, BF16)])
    sc = [_rin(p, D_G, 8), _rin(p, D_G, 9, halo=True), _rin(p, D_G, 10, halo=True)]
    sc_ps = [_par(lp["sc_cw"]), _par(lp["gw2"])]
    (y_d,), _ = seq_fwd(f"sconv_fwd_{l}", fn_sconv, n, TS_ROW, sc, sc_ps, [_out(D_G, BF16)])
    ins = dict(gdn=([qkv, z, ba], gdn_ps, [gdn_st]), lru=([lru_x, lru_gate], lru_ps, [lru_st]),
               sgu=([uv], sgu_ps, []), sc=(sc, sc_ps, []))
    return jnp.concatenate([y_a, y_b, y_c, y_d], axis=1), ins, side_res, side_res_lru


def _mixers_bwd(l, dymix, ins, n, side):
    cot = lambda g: [_rin(dymix, D_G, g)]
    xs, ps, st = ins["gdn"]
    res = seq_bwd(f"gdn_bwd_{l}", fn_gdn, n, TS_GDN, xs, ps, cot(0), st, [BF16, BF16, BF16], side=side)
    (dqkv, dz, dba), g_gdn, side_res = res if side else res + ([],)
    xs, ps, st = ins["lru"]
    (dlx, dlg), g_lru = seq_bwd(f"lru_bwd_{l}", fn_lru, n, TS_ROW, xs, ps, cot(1), st, [BF16, BF16])
    xs, ps, st = ins["sgu"]
    (duv,), g_sgu = seq_bwd(f"sgu_bwd_{l}", fn_sgu, n, TS_ROW, xs, ps, cot(2), st, [BF16])
    xs, ps, st = ins["sc"]
    (dsb, dsc, dsh), g_sc = seq_bwd(f"sconv_bwd_{l}", fn_sconv, n, TS_ROW, xs, ps, cot(3), st, [BF16, BF16, BF16])
    dp = jnp.concatenate([dqkv, dz, dlx, dlg, duv, dsb, dsc, dsh, dba], axis=1)
    return dp, dict(gdn=g_gdn, lru=g_lru, sgu=g_sgu, sc=g_sc), side_res


def _ffn_ops(hid, lp):
    d_ffh = lp["d_ffh"]
    off = d_ffh // TC_FFN
    xs = [_rin(hid, TC_FFN, lambda j: j, halo=True), _rin(hid, TC_FFN, lambda j: j + off, halo=True)]
    ps = [_par(lp["ffn_cw"], TC_FFN, lambda j: j), _par(lp["ffn_cw"], TC_FFN, lambda j: j + off),
          _par(lp["ffn_cb"], TC_FFN, lambda j: j), _par(lp["ffn_cb"], TC_FFN, lambda j: j + off)]
    return xs, ps, d_ffh


_FROM_BLOCKS = dict(
    w_in=lambda b: _regroup_w_in(b.transpose(1, 0, 2).reshape(b.shape[1], -1)),
    ffn_up=lambda b: b.transpose(1, 0, 2).reshape(b.shape[1], -1),
    w_out=lambda b: b.reshape(-1, b.shape[2]),
    ffn_down=lambda b: b.reshape(-1, b.shape[2]),
)
_TO_BLOCKS = dict(
    w_in=lambda g: _ungroup_w_in(g).reshape(g.shape[0], N_DEV, -1).transpose(1, 0, 2),
    ffn_up=lambda g: g,
    w_out=lambda g: g.reshape(N_DEV, -1, g.shape[1]),
    ffn_down=lambda g: g.reshape(N_DEV, -1, g.shape[1]),
)


class _Traffic:
    PARTS = dict(w_in=1, w_out=1, ffn_up=2, ffn_down=1)

    def __init__(self, whole=None, shards=None):
        self.whole = dict(whole or {})
        self.shards = shards
        self.gathered = {}
        self.pending = {}
        self.landed = {}

    def _rows(self, key):
        name, l, part = key
        rows = self.shards[name].shape[1] // self.PARTS[name]
        return slice(part * rows, (part + 1) * rows)

    def jobs(self, gather=(), exchange=()):
        if self.shards is None:
            return [], []
        keys = [("gather", k) for k in gather if k not in self.gathered and k[:2] not in self.whole]
        keys += [("exchange", k) for k in exchange if k in self.pending]
        jobs = [(kind, self.shards[k[0]][k[1]][self._rows(k)] if kind == "gather" else self.pending[k])
                for kind, k in keys]
        return jobs, keys

    def done(self, keys, results):
        for (kind, k), r in zip(keys, results):
            if kind == "gather":
                self.gathered[k] = r
            else:
                self.landed[k] = r
                del self.pending[k]

    def weight(self, name, l):
        if (name, l) not in self.whole:
            parts = []
            for part in range(self.PARTS[name]):
                k = (name, l, part)
                if k not in self.gathered:
                    self.gathered[k] = all_gather(f"gather_{name}_{l}_{part}", self.shards[name][l][self._rows(k)], False)
                parts.append(self.gathered[k])
            blocks = parts[0] if len(parts) == 1 else jnp.concatenate(parts, axis=1)
            self.whole[(name, l)] = _FROM_BLOCKS[name](blocks)
        return self.whole[(name, l)]

    def grad(self, name, l, g):
        if self.shards is None:
            self.landed[(name, l)] = g
            return
        blocks = _TO_BLOCKS[name](g)
        for part in range(self.PARTS[name]):
            k = (name, l, part)
            self.pending[k] = blocks[:, self._rows(k)]

    def flush(self):
        for (name, l, part), blocks in list(self.pending.items()):
            self.landed[(name, l, part)] = all_to_all(f"exchange_{name}_{l}_{part}", blocks)
            del self.pending[(name, l, part)]


def local_step(x, target, wt, tr):
    n, d = x.shape
    depth = wt["pre_mix_norm"].shape[0]
    lps = [_layer_params(wt, l) for l in range(depth)]
    saved = []
    xin = x

    def mm(name, a, b, mode, dtype, tm, tn, tk, gather=(), exchange=(), **split):
        jobs, keys = tr.jobs(gather, exchange)
        if not jobs:
            return matmul(name, a, b, mode, dtype, tm, tn, tk, **split)
        out, res = matmul(name, a, b, mode, dtype, tm, tn, tk, side=jobs, **split)
        tr.done(keys, res)
        return out

    (h,), _ = seq_fwd("norm_fwd", fn_norm, n, TS_ROW, [_rin(x)], [_par(lps[0]["pre_mix"])], [_out(d, BF16)])
    dx_last = loss = None
    for l in range(depth):
        lp = lps[l]
        p = mm(f"w_in_fwd_{l}", h, tr.weight("w_in", l), "nn", F32, 2048, N_INP // 5, d,
               gather=[("ffn_up", l, 0)])
        jobs, keys = tr.jobs(gather=[("ffn_up", l, 1)])
        jobs_lru, keys_lru = tr.jobs(gather=[("w_out", l, 0)])
        ymix, mix_ins, res, res_lru = _mixers_fwd(l, p, lp, n, jobs, jobs_lru)
        tr.done(keys, res)
        tr.done(keys_lru, res_lru)
        y = mm(f"w_out_fwd_{l}", ymix, tr.weight("w_out", l), "nn", BF16, 2048, 1024, d)
        res_ps = [_par(lp["post_mix"]), _par(lp["pre_ffn"])]
        (x1, h2), _ = seq_fwd(f"res_mix_fwd_{l}", fn_res, n, TS_ROW, [_rin(xin), _rin(y)], res_ps,
                              [_out(d, F32), _out(d, BF16)])
        nxt = l + 1 < depth
        hid = mm(f"ffn_up_fwd_{l}", h2, tr.weight("ffn_up", l), "nn", F32, 2048, 1024, d,
                 gather=[("ffn_down", l, 0)] + ([("w_out", l + 1, 0)] if nxt else []))
        f_xs, f_ps, d_ffh = _ffn_ops(hid, lp)
        jobs, keys = tr.jobs(gather=[("w_in", l + 1, 0)] if nxt else [])
        res = seq_fwd(f"ffn_act_fwd_{l}", fn_ffn, n, TS_FFN, f_xs, f_ps,
                      [_out(TC_FFN, BF16, d_ffh, lambda j: j)], ncol=d_ffh // TC_FFN, side=jobs)
        (act,) = res[0]
        tr.done(keys, res[2] if jobs else [])
        yf = mm(f"ffn_down_fwd_{l}", act, tr.weight("ffn_down", l), "nn", BF16, 1024, 1024, d_ffh // 2)
        rec = dict(x=xin, h=h, mix_ins=mix_ins, ymix=ymix, y=y, x1=x1, h2=h2, f_xs=f_xs, f_ps=f_ps, act=act, yf=yf)
        if l + 1 < depth:
            ps = [_par(lp["post_ffn"]), _par(lps[l + 1]["pre_mix"])]
            (x2, h), _ = seq_fwd(f"res_ffn_fwd_{l}", fn_res, n, TS_ROW, [_rin(x1), _rin(yf)], ps,
                                 [_out(d, F32), _out(d, BF16)])
            rec["res_ffn_ps"] = ps
            xin = x2
        else:
            def body(xs, blks, ps, carries, r):
                x1_, yf_, t_ = xs
                e = x1_ + _rms(yf_.astype(F32), ps[0]) - t_
                part = 0.5 * jnp.sum(jnp.mean(e * e, axis=-1, keepdims=True), axis=0, keepdims=True)
                return [e * (1.0 / d)], [], [jnp.broadcast_to(part, (8, 128))], []

            ps = [_par(lp["post_ffn"])]
            dx_last, loss = _block_call("loss_fwd", body, n, TS_ROW, 1, False, [_rin(x1), _rin(yf), _rin(target)],
                                        [], ps, [_out(d, F32)], [], [dict(shape=(8, 128), total=None, col=None)], [])
            rec["res_ffn_ps"] = ps
        saved.append(rec)

    grads = {}
    dx2, dh_next = dx_last, None
    for l in reversed(range(depth)):
        rec, lp = saved[l], lps[l]
        d_ffh = lp["d_ffh"]
        g = {}
        if dh_next is None:
            (dx1, dyf), (g["post_ffn"],) = seq_bwd(f"res_ffn_bwd_{l}", fn_res_last, n, TS_ROW,
                                                   [_rin(rec["x1"]), _rin(rec["yf"])], rec["res_ffn_ps"], [_rin(dx2)],
                                                   din_dtypes=[F32, BF16])
        else:
            (dx1, dyf), (g["post_ffn"], g_next_pre) = seq_bwd(
                f"res_ffn_bwd_{l}", fn_res, n, TS_ROW, [_rin(rec["x1"]), _rin(rec["yf"])], rec["res_ffn_ps"],
                [_rin(dx2), _rin(dh_next)], din_dtypes=[F32, BF16])
            grads[l + 1]["pre_mix"] = g_next_pre
        tr.grad("ffn_down", l, mm(f"ffn_down_dw_{l}", rec["act"], dyf, "tn", BF16, d_ffh // 4, 1024, 2048))

        def dact(cot_blocks, aux_blocks):
            return [_dot(cot_blocks[0], aux_blocks[0], 1, 1).astype(BF16)]

        (dhg, dhv), (g_cwg, g_cwv, g_cbg, g_cbv) = seq_bwd(
            f"ffn_act_bwd_{l}", fn_ffn, n, TS_FFN_BWD, rec["f_xs"], rec["f_ps"], [_rin(dyf)],
            din_dtypes=[BF16, BF16], ncol=d_ffh // TC_FFN, din_specs=[(d_ffh, lambda j: j), (d_ffh, lambda j: j)],
            cot_map=dact, aux=[_par(tr.weight("ffn_down", l), TC_FFN, row=lambda j: j)])
        g["ffn_cw"] = jnp.concatenate([g_cwg[:, :d_ffh], g_cwv[:, d_ffh:]], axis=1)
        g["ffn_cb"] = jnp.concatenate([g_cbg[:, :d_ffh], g_cbv[:, d_ffh:]], axis=1)
        dh2 = mm(f"ffn_up_dx_{l}", dhg, tr.weight("ffn_up", l), "nt", BF16, 1024, 1024, d_ffh // 2,
                 exchange=[("ffn_down", l, 0)], a2=dhv)
        tr.grad("ffn_up", l, mm(f"ffn_up_dw_{l}", rec["h2"], dhg, "tn", BF16, 1024, 2 * d_ffh // N_DEV, 2048, b2=dhv,
                                col_blocks=True))
        (dx, dy), (g["post_mix"], g["pre_ffn"]) = seq_bwd(
            f"res_mix_bwd_{l}", fn_res, n, TS_ROW, [_rin(rec["x"]), _rin(rec["y"])],
            [_par(lp["post_mix"]), _par(lp["pre_ffn"])], [_rin(dx1), _rin(dh2)], din_dtypes=[F32, BF16])
        dymix = mm(f"w_out_dx_{l}", dy, tr.weight("w_out", l), "nt", BF16, 2048, 1024, d)
        tr.grad("w_out", l, mm(f"w_out_dw_{l}", rec["ymix"], dy, "tn", BF16, 1024, 1024, 2048))
        jobs, keys = tr.jobs(exchange=[("ffn_up", l, 0), ("w_out", l, 0)])
        dp, g["mix"], res = _mixers_bwd(l, dymix, rec["mix_ins"], n, jobs)
        tr.done(keys, res)
        tr.grad("w_in", l, mm(f"w_in_dw_{l}", rec["h"], dp, "tn", BF16, 1024, N_INP // 5, 2048,
                              exchange=[("ffn_up", l, 1)]))
        dh = mm(f"w_in_dx_{l}", dp, tr.weight("w_in", l), "nt", BF16, 1024, 1024, N_INP // 3,
                exchange=[("w_in", l, 0)])
        grads[l] = g
        dx2, dh_next = dx, dh
    (grad_x,), (g_pre0,) = seq_bwd("norm_bwd", fn_norm_keep, n, TS_ROW, [_rin(x)], [_par(lps[0]["pre_mix"])],
                                   [_rin(dh_next), _rin(dx2)])
    grads[0]["pre_mix"] = g_pre0
    tr.flush()
    return loss[0, 0], grad_x, _name_grads(grads, depth)


def _name_grads(grads, depth):
    per = {k: [] for k in SMALL}
    for l in range(depth):
        g = grads[l]
        m = g["mix"]
        cw, gp, nw = m["gdn"]
        lcw, lcb, lwa, lba, lwx, lbx, llam, gw0 = m["lru"]
        lnw, lnb, ws, bst, gw1 = m["sgu"]
        scw, gw2 = m["sc"]
        per["pre_mix_norm"].append(g["pre_mix"][0])
        per["gdn_conv_w"].append(cw)
        per["gdn_a_log"].append(gp[0, 4:8])
        per["gdn_dt_bias"].append(gp[1, 4:8])
        per["gdn_norm_w"].append(nw[0])
        per["lru_conv_w"].append(lcw)
        per["lru_conv_b"].append(lcb[0])
        per["lru_wa"].append(_block_diag_grad(lwa, LRU_BLOCKS))
        per["lru_ba"].append(lba.reshape(LRU_BLOCKS, -1))
        per["lru_wx"].append(_block_diag_grad(lwx, LRU_BLOCKS))
        per["lru_bx"].append(lbx.reshape(LRU_BLOCKS, -1))
        per["lru_lambda"].append(llam[0])
        per["sgu_ln_w"].append(lnw[0])
        per["sgu_ln_b"].append(lnb[0])
        per["sgu_ws"].append(ws.reshape(SGU_GROUPS, SGU_CHUNK, SGU_CHUNK))
        per["sgu_b"].append(bst[:, :SGU_GROUPS].T)
        per["sconv_w"].append(scw)
        per["grp_norm_w"].append(jnp.concatenate([gw0, gw1, gw2], axis=0))
        per["post_mix_norm"].append(g["post_mix"][0])
        per["pre_ffn_norm"].append(g["pre_ffn"][0])
        per["ffn_conv_w"].append(g["ffn_cw"])
        per["ffn_conv_b"].append(g["ffn_cb"][0])
        per["post_ffn_norm"].append(g["post_ffn"][0])
    return {k: jnp.stack(v) for k, v in per.items()}


def _regroup_w_in(w):
    pad = jnp.zeros(w.shape[:-1] + (N_INP - N_IN,), w.dtype)
    return jnp.concatenate([w[..., :2048], w[..., 2056:], w[..., 2048:2056], pad], axis=-1)


def _ungroup_w_in(g):
    return jnp.concatenate([g[..., :2048], g[..., BA_COL:BA_COL + 8], g[..., 2048:BA_COL]], axis=-1)


def kernel(x, pre_mix_norm, w_in, gdn_conv_w, gdn_a_log, gdn_dt_bias, gdn_norm_w, lru_conv_w, lru_conv_b, lru_wa, lru_ba, lru_wx, lru_bx, lru_lambda, sgu_ln_w, sgu_ln_b, sgu_ws, sgu_b, sconv_w, grp_norm_w, w_out, post_mix_norm, pre_ffn_norm, ffn_up, ffn_conv_w, ffn_conv_b, ffn_down, post_ffn_norm, loss_target, m_pre_mix_norm, m_w_in, m_gdn_conv_w, m_gdn_a_log, m_gdn_dt_bias, m_gdn_norm_w, m_lru_conv_w, m_lru_conv_b, m_lru_wa, m_lru_ba, m_lru_wx, m_lru_bx, m_lru_lambda, m_sgu_ln_w, m_sgu_ln_b, m_sgu_ws, m_sgu_b, m_sconv_w, m_grp_norm_w, m_w_out, m_post_mix_norm, m_pre_ffn_norm, m_ffn_up, m_ffn_conv_w, m_ffn_conv_b, m_ffn_down, m_post_ffn_norm, v_pre_mix_norm, v_w_in, v_gdn_conv_w, v_gdn_a_log, v_gdn_dt_bias, v_gdn_norm_w, v_lru_conv_w, v_lru_conv_b, v_lru_wa, v_lru_ba, v_lru_wx, v_lru_bx, v_lru_lambda, v_sgu_ln_w, v_sgu_ln_b, v_sgu_ws, v_sgu_b, v_sconv_w, v_grp_norm_w, v_w_out, v_post_mix_norm, v_pre_ffn_norm, v_ffn_up, v_ffn_conv_w, v_ffn_conv_b, v_ffn_down, v_post_ffn_norm):
    args = locals()
    w_loc = {k: args[k] for k in WEIGHTS}
    m_loc = {k: args["m_" + k] for k in WEIGHTS}
    v_loc = {k: args["v_" + k] for k in WEIGHTS}
    depth = pre_mix_norm.shape[0]
    x_, y_, c_ = _my_pos()
    me = 4 * x_ + 2 * y_ + c_

    tr = _Traffic(shards={name: cast_bf16(f"cast_{name}", w_loc[name]) for name in BIG})
    wt = {k: w_loc[k] for k in SMALL}
    shard_shapes = [w_loc[k].shape for k in SHARDED_SMALL]
    gathered = all_gather("gather_small", _pack([w_loc[k] for k in SHARDED_SMALL]), True)
    for k, a in zip(SHARDED_SMALL, _unpack(gathered, shard_shapes, lead=(N_DEV,))):
        a = jnp.moveaxis(a, 0, -2)
        wt[k] = a.reshape(a.shape[:-2] + (-1,))

    loss_part, grad_x, g_full = local_step(x[0], loss_target[0], wt, tr)
    loss = lax.psum(loss_part, ("x", "y", "c"))

    outs_g, outs_d, outs_m, outs_v = {}, {}, {}, {}
    for name in BIG:
        lands = {(l, part): a for (n_, l, part), a in tr.landed.items() if n_ == name}
        outs_g[name], outs_d[name], outs_m[name], outs_v[name] = adamw_big(
            f"adamw_{name}", w_loc[name], lands, m_loc[name], v_loc[name])

    full_shapes = [g_full[k].shape for k in SMALL]
    g_all = all_gather("gather_small_grads", _pack([g_full[k] for k in SMALL]), True)
    g_sum = _unpack(sum_blocks("sum_small_grads", g_all), full_shapes)
    g_small = {}
    for k, g in zip(SMALL, g_sum):
        if k in SHARDED_SMALL:
            w = w_loc[k].shape[-1]
            g = lax.dynamic_slice_in_dim(g, me * w, w, axis=g.ndim - 1)
        g_small[k] = g
    d_s, m_s, v_s = adamw_small("adamw_small", [w_loc[k] for k in SMALL], [g_small[k] for k in SMALL],
                                [m_loc[k] for k in SMALL], [v_loc[k] for k in SMALL])
    for k_i, k in enumerate(SMALL):
        outs_g[k], outs_d[k], outs_m[k], outs_v[k] = g_small[k], d_s[k_i], m_s[k_i], v_s[k_i]

    return (loss, grad_x[None], *[outs_g[k] for k in WEIGHTS], *[outs_d[k] for k in WEIGHTS],
            *[outs_m[k] for k in WEIGHTS], *[outs_v[k] for k in WEIGHTS])
```

```python
import functools
import math

import jax
import jax.numpy as jnp
from jax import lax
from jax.experimental import pallas as pl
from jax.experimental.pallas import tpu as pltpu

F32 = jnp.float32
BF16 = jnp.bfloat16
EPS = 1e-6
HALO = 8
VMEM_LIMIT = 56 * 1024 * 1024
MESH = pl.DeviceIdType.MESH
N_DEV = 8

ADAM_LR, ADAM_B1, ADAM_B2, ADAM_EPS, ADAM_WD, ADAM_STEP = 0.001, 0.9, 0.999, 1e-08, 0.01, 10

GDN_HEADS, GDN_DIM, GDN_CHUNK = 4, 128, 64
SGU_GROUPS, SGU_CHUNK = 4, 128
LRU_BLOCKS, LRU_C = 8, 8.0
D_G = 512
N_IN = 5640
N_INP = 5760
BA_COL = 5632

SHARDED_SMALL = ("gdn_conv_w", "lru_conv_w", "sconv_w", "grp_norm_w", "ffn_conv_w")
BIG = ("w_in", "w_out", "ffn_up", "ffn_down")
WEIGHTS = ("pre_mix_norm", "w_in", "gdn_conv_w", "gdn_a_log", "gdn_dt_bias", "gdn_norm_w", "lru_conv_w",
           "lru_conv_b", "lru_wa", "lru_ba", "lru_wx", "lru_bx", "lru_lambda", "sgu_ln_w", "sgu_ln_b", "sgu_ws",
           "sgu_b", "sconv_w", "grp_norm_w", "w_out", "post_mix_norm", "pre_ffn_norm", "ffn_up", "ffn_conv_w",
           "ffn_conv_b", "ffn_down", "post_ffn_norm")
SMALL = tuple(n for n in WEIGHTS if n not in BIG)


def _dot(a, b, ca, cb):
    return lax.dot_general(a.astype(BF16), b.astype(BF16), (((ca,), (cb,)), ((), ())),
                           preferred_element_type=F32)


@jax.custom_vjp
def _mm(a, b):
    return _dot(a, b, 1, 0)


def _mm_f(a, b):
    return _dot(a, b, 1, 0), (a, b)


def _mm_b(res, g):
    a, b = res
    return _dot(g, b, 1, 1), _dot(a, g, 0, 0)


_mm.defvjp(_mm_f, _mm_b)


@jax.custom_vjp
def _mm_nt(a, b):
    return _dot(a, b, 1, 1)


def _mm_nt_f(a, b):
    return _dot(a, b, 1, 1), (a, b)


def _mm_nt_b(res, g):
    a, b = res
    return _dot(g, b, 1, 0), _dot(g, a, 0, 0)


_mm_nt.defvjp(_mm_nt_f, _mm_nt_b)


@jax.custom_vjp
def _mm_tn(a, b):
    return _dot(a, b, 0, 0)


def _mm_tn_f(a, b):
    return _dot(a, b, 0, 0), (a, b)


def _mm_tn_b(res, g):
    a, b = res
    return _dot(b, g, 1, 1), _dot(a, g, 1, 0)


_mm_tn.defvjp(_mm_tn_f, _mm_tn_b)


def _dot_exact(a, b, ca, cb):
    return lax.dot_general(a, b, (((ca,), (cb,)), ((), ())), precision=lax.Precision.HIGHEST,
                           preferred_element_type=F32)


@functools.partial(jax.custom_vjp, nondiff_argnums=(1,))
def _shift_rows(x, s):
    return pltpu.roll(x, s, 0)


def _shift_rows_f(x, s):
    return pltpu.roll(x, s, 0), None


def _shift_rows_b(s, _, g):
    return (pltpu.roll(g, (g.shape[0] - s) % g.shape[0], 0),)


_shift_rows.defvjp(_shift_rows_f, _shift_rows_b)


def _sigmoid(x):
    return 1.0 / (1.0 + jnp.exp(-x))


def _silu(x):
    return x * _sigmoid(x)


GELU_C, GELU_A = 0.7978845608028654, 0.044715


@jax.custom_vjp
def _gelu(x):
    return 0.5 * x * (1.0 + jnp.tanh(GELU_C * (x + GELU_A * (x * x * x))))


def _gelu_f(x):
    t = jnp.tanh(GELU_C * (x + GELU_A * (x * x * x)))
    return 0.5 * x * (1.0 + t), (x, t)


def _gelu_b(res, g):
    x, t = res
    slope = 0.5 * (1.0 + t) + (0.5 * GELU_C) * x * (1.0 - t * t) * (1.0 + (3.0 * GELU_A) * (x * x))
    return (g * slope,)


_gelu.defvjp(_gelu_f, _gelu_b)


@jax.custom_vjp
def _softplus(x):
    e = jnp.exp(-jnp.abs(x))
    u = 1.0 + e
    log1p = jnp.where(u == 1.0, e, jnp.log(u) * (e / jnp.where(u == 1.0, 1.0, u - 1.0)))
    return jnp.maximum(x, 0.0) + log1p


def _softplus_f(x):
    return _softplus(x), x


def _softplus_b(x, g):
    return (g * _sigmoid(x),)


_softplus.defvjp(_softplus_f, _softplus_b)


def _neg_expm1(y):
    return -jnp.tanh(0.5 * y) * (jnp.exp(y) + 1.0)


def _rms(x, w):
    return x * lax.rsqrt(jnp.mean(x * x, axis=-1, keepdims=True) + EPS) * w


def _row(w, k):
    sel = lax.broadcasted_iota(jnp.int32, w.shape, 0) == k
    return jnp.sum(jnp.where(sel, w, 0.0), axis=0, keepdims=True)


def _col(x, j):
    sel = lax.broadcasted_iota(jnp.int32, x.shape, 1) == j
    return jnp.sum(jnp.where(sel, x, 0.0), axis=1, keepdims=True)


def _conv(x_ext, w, taps):
    acc = None
    for k in range(taps):
        s = taps - 1 - k
        t = (x_ext if s == 0 else _shift_rows(x_ext, s)) * _row(w, k)
        acc = t if acc is None else acc + t
    return acc[HALO:]


SCAN_LANES = 128


def _by_lane_groups(fn, *arrays):
    c = arrays[0].shape[1]
    if c <= SCAN_LANES:
        return fn(*arrays)
    parts = [fn(*[x[:, c0:c0 + SCAN_LANES] for x in arrays]) for c0 in range(0, c, SCAN_LANES)]
    if isinstance(parts[0], tuple):
        return tuple(jnp.concatenate(p, axis=1) for p in zip(*parts))
    return jnp.concatenate(parts, axis=1)


@jax.custom_vjp
def _scan(a, b, h0):
    return _by_lane_groups(_scan_group, a, b, h0)


def _scan_group(a, b, h0):
    n = a.shape[0]
    row = lax.broadcasted_iota(jnp.int32, a.shape, 0)
    s = 1
    while s < n:
        keep = row >= s
        a_sh = jnp.where(keep, pltpu.roll(a, s, 0), 1.0)
        b_sh = jnp.where(keep, pltpu.roll(b, s, 0), 0.0)
        b = a * b_sh + b
        a = a * a_sh
        s *= 2
    return b + a * h0


def _scan_f(a, b, h0):
    h = _scan(a, b, h0)
    return h, (a, h, h0)


def _scan_b(res, dh):
    a, h, h0 = res
    return _by_lane_groups(_scan_b_group, a, h, h0, dh)


def _scan_b_group(a, h, h0, dh):
    n = a.shape[0]
    row = lax.broadcasted_iota(jnp.int32, a.shape, 0)
    an = jnp.where(row < n - 1, pltpu.roll(a, n - 1, 0), 0.0)
    lam = dh
    s = 1
    while s < n:
        keep = row < n - s
        a_sh = jnp.where(keep, pltpu.roll(an, n - s, 0), 1.0)
        l_sh = jnp.where(keep, pltpu.roll(lam, n - s, 0), 0.0)
        lam = an * l_sh + lam
        an = an * a_sh
        s *= 2
    h_prev = jnp.where(row >= 1, pltpu.roll(h, 1, 0), h0)
    al = a * lam
    dh0 = jnp.sum(jnp.where(row == 0, al, 0.0), axis=0, keepdims=True)
    return lam * h_prev, lam, dh0


_scan.defvjp(_scan_f, _scan_b)


@jax.custom_vjp
def _unit_lower_inverses(ms):
    n = ms[0].shape[0]
    shape = ms[0].shape
    eye = (lax.broadcasted_iota(jnp.int32, shape, 0) == lax.broadcasted_iota(jnp.int32, shape, 1)).astype(F32)
    p = [-m for m in ms]
    t = [eye + a for a in p]
    steps = 1
    while 2 ** steps < n:
        p = [_mm(a, a) for a in p]
        t = [a + _mm(a, c) for a, c in zip(t, p)]
        steps += 1
    return t


def _unit_lower_inverses_f(ms):
    t = _unit_lower_inverses(ms)
    return t, t


def _unit_lower_inverses_b(t, dt):
    x = [_mm_nt(g, a) for g, a in zip(dt, t)]
    return ([-_mm_tn(a, c) for a, c in zip(t, x)],)


_unit_lower_inverses.defvjp(_unit_lower_inverses_f, _unit_lower_inverses_b)


def _last_row(x):
    sel = lax.broadcasted_iota(jnp.int32, x.shape, 0) == x.shape[0] - 1
    return jnp.sum(jnp.where(sel, x, 0.0), axis=0, keepdims=True)


def fn_norm(xs, st, ps):
    (x,), (w,) = xs, ps
    return [_rms(x, w).astype(BF16)], []


def fn_norm_keep(xs, st, ps):
    (x,), (w,) = xs, ps
    return [_rms(x, w).astype(BF16), x], []


def fn_res(xs, st, ps):
    (x, y), (w_post, w_next) = xs, ps
    x1 = x + _rms(y.astype(F32), w_post)
    return [x1, _rms(x1, w_next).astype(BF16)], []


def fn_res_last(xs, st, ps):
    (x, y), (w_post,) = xs, ps
    return [x + _rms(y.astype(F32), w_post)], []


def fn_gdn(xs, st, ps):
    qkv_ext, z, ba = xs
    (state,) = st
    cw, gp, nw = ps
    ts = z.shape[0]
    qkv = _silu(_conv(qkv_ext, cw, 4))
    beta_all = _sigmoid(ba)
    g_all = -jnp.exp(_row(gp, 0)) * _softplus(ba + _row(gp, 1))
    c_n = GDN_CHUNK
    ri = lax.broadcasted_iota(jnp.int32, (c_n, c_n), 0)
    ci = lax.broadcasted_iota(jnp.int32, (c_n, c_n), 1)
    causal, strict = ri >= ci, ri > ci
    tril = causal.astype(F32)
    lane = lax.broadcasted_iota(jnp.int32, (c_n, 128), 1)
    s_h = [state[GDN_DIM * h:GDN_DIM * (h + 1)] for h in range(GDN_HEADS)]
    n_c = ts // c_n
    pairs = [(c, h) for c in range(n_c) for h in range(GDN_HEADS)]
    every = lambda f, *lists: [f(*a) for a in zip(*lists)]

    def piece(c, h, base):
        return qkv[c * c_n:(c + 1) * c_n, base + GDN_DIM * h:base + GDN_DIM * (h + 1)]

    q = [piece(c, h, 0) for c, h in pairs]
    k = [piece(c, h, D_G) for c, h in pairs]
    v = [piece(c, h, 2 * D_G) for c, h in pairs]
    q = every(lambda t: t * lax.rsqrt(jnp.sum(t * t, axis=-1, keepdims=True) + EPS) * (GDN_DIM ** -0.5), q)
    k = every(lambda t: t * lax.rsqrt(jnp.sum(t * t, axis=-1, keepdims=True) + EPS), k)
    gcum_all = [_dot_exact(tril, g_all[c * c_n:(c + 1) * c_n], 1, 0) for c in range(n_c)]
    b = [_col(beta_all[c * c_n:(c + 1) * c_n], h) for c, h in pairs]
    gc = [_col(gcum_all[c], 4 + h) for c, h in pairs]
    gr = [_dot_exact((lane == 4 + h).astype(F32), gcum_all[c], 1, 1) for c, h in pairs]
    decay = every(lambda a, r: jnp.where(causal, jnp.exp(jnp.where(causal, a - r, 0.0)), 0.0), gc, gr)
    kb = every(lambda a, c: a * c, k, b)
    mk = every(lambda a, c, e: _mm_nt(jnp.concatenate([a, c], axis=0), e), kb, q, k)
    m = every(lambda a, dcy: jnp.where(strict, a[:c_n] * dcy, 0.0), mk, decay)
    attn = every(lambda a, dcy: jnp.where(causal, a[c_n:] * dcy, 0.0), mk, decay)
    t_ = _unit_lower_inverses(m)
    eg = every(jnp.exp, gc)
    wu = every(lambda t, a, e, c, d: _mm(t, jnp.concatenate([a * e, c * d], axis=1)), t_, kb, eg, v, b)
    g_last = every(_last_row, gc)
    k_g = every(lambda a, gl, g: a * jnp.exp(gl - g), k, g_last, gc)
    wq = every(lambda a, c, e: jnp.concatenate([a[:, :GDN_DIM], c * e], axis=0), wu, q, eg)
    u = [a[:, GDN_DIM:] for a in wu]
    gl = every(jnp.exp, g_last)

    o = []
    for c in range(n_c):
        idx = range(c * GDN_HEADS, (c + 1) * GDN_HEADS)
        ws = [_mm(wq[i], s_h[h]) for h, i in enumerate(idx)]
        v_new = [u[i] - ws[h][:c_n] for h, i in enumerate(idx)]
        av = [_mm(attn[i], v_new[h]) for h, i in enumerate(idx)]
        kv = [_mm_tn(k_g[i], v_new[h]) for h, i in enumerate(idx)]
        o += [ws[h][c_n:] + av[h] for h in range(GDN_HEADS)]
        s_h = [s_h[h] * gl[i] + kv[h] for h, i in enumerate(idx)]
    zz = [z[c * c_n:(c + 1) * c_n, GDN_DIM * h:GDN_DIM * (h + 1)] for c, h in pairs]
    y = every(lambda a, g: a * lax.rsqrt(jnp.mean(a * a, axis=-1, keepdims=True) + EPS) * nw * _silu(g), o, zz)
    rows = [jnp.concatenate(y[c * GDN_HEADS:(c + 1) * GDN_HEADS], axis=1) for c in range(n_c)]
    y = rows[0] if n_c == 1 else jnp.concatenate(rows, axis=0)
    return [y.astype(BF16)], [jnp.concatenate(s_h, axis=0)]


def fn_lru(xs, st, ps):
    x_ext, gate = xs
    (h0,) = st
    cw, cb, wa, ba, wx, bx, lam, gw = ps
    xc = _conv(x_ext, cw, 4) + cb
    r = _sigmoid(_mm(xc, wa) + ba)
    i = _sigmoid(_mm(xc, wx) + bx)
    log_a = -LRU_C * r * _softplus(-lam)
    a = jnp.exp(log_a)
    mult = jnp.sqrt(_neg_expm1(2.0 * log_a))
    h = _scan(a, mult * (i * xc), h0)
    y = _rms(h * _gelu(gate), gw)
    return [y.astype(BF16)], [_last_row(h)]


def fn_sgu(xs, st, ps):
    (uv,) = xs
    lnw, lnb, ws, bst, gw = ps
    ts = uv.shape[0]
    uvf = _gelu(uv)
    u, v = uvf[:, :D_G], uvf[:, D_G:]
    vc = v - jnp.mean(v, axis=-1, keepdims=True)
    v = vc * lax.rsqrt(jnp.mean(vc * vc, axis=-1, keepdims=True) + EPS) * lnw + lnb
    t_n = SGU_CHUNK
    tril = lax.broadcasted_iota(jnp.int32, (t_n, t_n), 0) >= lax.broadcasted_iota(jnp.int32, (t_n, t_n), 1)
    wg = [jnp.where(tril, ws[t_n * g:t_n * (g + 1)], 0.0) for g in range(SGU_GROUPS)]
    bg = [_col(bst, g) for g in range(SGU_GROUPS)]
    rows = []
    for c in range(ts // t_n):
        vcg = v[c * t_n:(c + 1) * t_n]
        rows.append(jnp.concatenate(
            [_mm(wg[g], vcg[:, 128 * g:128 * (g + 1)]) + bg[g] for g in range(SGU_GROUPS)], axis=1))
    vv = rows[0] if len(rows) == 1 else jnp.concatenate(rows, axis=0)
    return [_rms(u * vv, gw).astype(BF16)], []


def fn_sconv(xs, st, ps):
    bg, cg_ext, hh_ext = xs
    cw, gw = ps
    return [_rms(bg * _conv(cg_ext * hh_ext, cw, 3), gw).astype(BF16)], []


def fn_ffn(xs, st, ps):
    g_ext, v_ext = xs
    cwg, cwv, cbg, cbv = ps
    outs = []
    for c0 in range(0, g_ext.shape[1], FFN_LANES):
        cols = slice(c0, c0 + FFN_LANES)
        g = _conv(g_ext[:, cols], cwg[:, cols], 3) + cbg[:, cols]
        v = _conv(v_ext[:, cols], cwv[:, cols], 3) + cbv[:, cols]
        outs.append((_gelu(g) * v).astype(BF16))
    return [outs[0] if len(outs) == 1 else jnp.concatenate(outs, axis=1)], []


def _my_pos():
    return lax.axis_index("x"), lax.axis_index("y"), lax.axis_index("c")


def _peer(pos, k):
    x_, y_, c_ = pos
    return (1 - x_ if (k >> 2) & 1 else x_, 1 - y_ if (k >> 1) & 1 else y_, 1 - c_ if k & 1 else c_)


def _dev_index(p):
    return 4 * p[0] + 2 * p[1] + p[2]


class _Side:
    def __init__(self, jobs):
        self.jobs = list(jobs)
        n = len(self.jobs)
        self.operands = [a for _, a in self.jobs]
        self.in_specs = [pl.BlockSpec(memory_space=pl.ANY)] * n
        self.out_shape = [jax.ShapeDtypeStruct(((N_DEV,) + a.shape) if kind == "gather" else a.shape, a.dtype)
                          for kind, a in self.jobs]
        self.out_specs = [pl.BlockSpec(memory_space=pl.ANY)] * n
        self.scratch = [pltpu.SemaphoreType.DMA((7 * n,)), pltpu.SemaphoreType.DMA((7 * n,)),
                        pltpu.SemaphoreType.DMA((n,))] if n else []

    def _copies(self, in_refs, out_refs, sems, landings=True):
        send, recv, local = sems
        pos = _my_pos()
        me = _dev_index(pos)
        mine, outgoing, landing = [], [], []
        for j, (kind, _) in enumerate(self.jobs):
            src, dst = in_refs[j], out_refs[j]
            own = src if kind == "gather" else src.at[me]
            mine.append(pltpu.make_async_copy(own, dst.at[me], local.at[j]))
            for k in range(1, N_DEV):
                p = _peer(pos, k)
                sems_k = dict(send_sem=send.at[7 * j + k - 1], recv_sem=recv.at[7 * j + k - 1], device_id=p,
                              device_id_type=MESH)
                outgoing.append(pltpu.make_async_remote_copy(
                    src_ref=src if kind == "gather" else src.at[_dev_index(p)], dst_ref=dst.at[me], **sems_k))
                if landings:
                    landing.append(pltpu.make_async_remote_copy(src_ref=own, dst_ref=dst.at[_dev_index(p)], **sems_k))
        return mine, outgoing, landing

    def start(self, in_refs, out_refs, sems):
        mine, outgoing, _ = self._copies(in_refs, out_refs, sems, landings=False)
        for cp in mine + outgoing:
            cp.start()

    def wait(self, in_refs, out_refs, sems):
        mine, outgoing, landing = self._copies(in_refs, out_refs, sems)
        for cp in landing:
            cp.wait_recv()
        for cp in outgoing:
            cp.wait_send()
        for cp in mine:
            cp.wait()


def _rin(arr, w=None, col=0, halo=False):
    return dict(arr=arr, w=arr.shape[1] if w is None else w, col=col, halo=halo)


def _par(arr, w=None, col=None, row=None):
    return dict(arr=arr, w=w, col=col, row=row)


def _colidx(col, j):
    return col(j) if callable(col) else col


def _block_call(name, body, n_rows, ts, ncol, reverse, row_ins, blk_ins, params, row_outs, blk_outs, acc_outs,
                carries, side=()):
    side = _Side(side)
    ts = min(ts, n_rows)
    nblk = n_rows // ts
    hb = ts // HALO

    def rr(i):
        return (nblk - 1 - i) if reverse else i

    in_specs, operands = [], []
    for s in row_ins:
        in_specs.append(pl.BlockSpec((ts, s["w"]), lambda j, i, s=s: (rr(i), _colidx(s["col"], j))))
        operands.append(s["arr"])
        if s["halo"]:
            in_specs.append(pl.BlockSpec((HALO, s["w"]),
                                         lambda j, i, s=s: (jnp.maximum(rr(i) * hb - 1, 0), _colidx(s["col"], j))))
            operands.append(s["arr"])
    for a in blk_ins:
        nd = a.ndim - 1
        in_specs.append(pl.BlockSpec((None,) + a.shape[1:], lambda j, i, nd=nd: (rr(i),) + (0,) * nd))
        operands.append(a)
    for p in params:
        a = p["arr"]
        if p["row"] is not None:
            in_specs.append(pl.BlockSpec((p["w"], a.shape[1]), lambda j, i, p=p: (_colidx(p["row"], j), 0)))
        elif p["col"] is None:
            in_specs.append(pl.BlockSpec(a.shape, lambda j, i: (0, 0)))
        else:
            in_specs.append(pl.BlockSpec((a.shape[0], p["w"]), lambda j, i, p=p: (0, _colidx(p["col"], j))))
        operands.append(a)

    out_specs, out_shape = [], []
    for o in row_outs:
        out_specs.append(pl.BlockSpec((ts, o["w"]), lambda j, i, o=o: (rr(i), _colidx(o["col"], j))))
        out_shape.append(jax.ShapeDtypeStruct((n_rows, o["total"]), o["dtype"]))
    for o in blk_outs:
        nd = len(o["shape"])
        out_specs.append(pl.BlockSpec((None,) + tuple(o["shape"]), lambda j, i, nd=nd: (rr(i),) + (0,) * nd))
        out_shape.append(jax.ShapeDtypeStruct((nblk,) + tuple(o["shape"]), o["dtype"]))
    for o in acc_outs:
        if o["col"] is None:
            out_specs.append(pl.BlockSpec(o["shape"], lambda j, i: (0, 0)))
            out_shape.append(jax.ShapeDtypeStruct(o["shape"], F32))
        else:
            out_specs.append(pl.BlockSpec(o["shape"], lambda j, i, o=o: (0, _colidx(o["col"], j))))
            out_shape.append(jax.ShapeDtypeStruct((o["shape"][0], o["total"]), F32))

    n_in = len(operands)
    n_row_out, n_blk_out, n_acc = len(row_outs), len(blk_outs), len(acc_outs)
    n_out = n_row_out + n_blk_out + n_acc
    n_side = len(side.jobs)

    def kern(*refs):
        in_refs = refs[:n_in]
        side_in = refs[n_in:n_in + n_side]
        out_refs = refs[n_in + n_side:n_in + n_side + n_out]
        side_out = refs[n_in + n_side + n_out:n_in + 2 * n_side + n_out]
        scratch = refs[n_in + 2 * n_side + n_out:]
        carry_refs, side_sems = scratch[:len(carries)], scratch[len(carries):]
        acc_refs = out_refs[n_row_out + n_blk_out:]
        i = pl.program_id(1)
        r = rr(i)
        if n_side:
            @pl.when((pl.program_id(0) == 0) & (i == 0))
            def _():
                side.start(side_in, side_out, side_sems)

        @pl.when(i == 0)
        def _():
            for c_ref in carry_refs:
                c_ref[...] = jnp.zeros(c_ref.shape, c_ref.dtype)
            for a_ref in acc_refs:
                a_ref[...] = jnp.zeros(a_ref.shape, a_ref.dtype)

        k = 0
        xs = []
        for s in row_ins:
            x = in_refs[k][...]
            k += 1
            if s["halo"]:
                hal = in_refs[k][...]
                k += 1
                hal = jnp.where(r == 0, jnp.zeros_like(hal), hal)
                x = jnp.concatenate([hal, x], axis=0)
            xs.append(x)
        blks = []
        for _ in blk_ins:
            blks.append(in_refs[k][...])
            k += 1
        ps = []
        for _ in params:
            ps.append(in_refs[k][...])
            k += 1
        row_vals, blk_vals, acc_vals, new_carries = body(xs, blks, ps, [c[...] for c in carry_refs], r)
        for ref, val in zip(out_refs[:n_row_out], row_vals):
            ref[...] = val.astype(ref.dtype)
        for ref, val in zip(out_refs[n_row_out:n_row_out + n_blk_out], blk_vals):
            ref[...] = val.astype(ref.dtype)
        for ref, val in zip(acc_refs, acc_vals):
            ref[...] += val
        for ref, val in zip(carry_refs, new_carries):
            ref[...] = val
        if n_side:
            @pl.when((pl.program_id(0) == ncol - 1) & (i == nblk - 1))
            def _():
                side.wait(side_in, side_out, side_sems)

    res = pl.pallas_call(
        kern,
        name=name,
        grid=(ncol, nblk),
        in_specs=in_specs + side.in_specs,
        out_specs=out_specs + side.out_specs,
        out_shape=out_shape + side.out_shape,
        scratch_shapes=[pltpu.VMEM(shape, F32) for shape in carries] + side.scratch,
        compiler_params=pltpu.CompilerParams(dimension_semantics=("arbitrary", "arbitrary"),
                                             vmem_limit_bytes=VMEM_LIMIT),
    )(*operands, *side.operands)
    return list(res)


def _out(w, dtype, total=None, col=0):
    return dict(w=w, dtype=dtype, total=w if total is None else total, col=col)


def seq_fwd(name, fn, n_rows, ts, row_ins, params, outs, state_shapes=(), ncol=1, side=()):
    def body(xs, blks, ps, carries, r):
        o, new_st = fn(xs, list(carries), ps)
        return o, list(carries), [], new_st

    res = _block_call(name, body, n_rows, ts, ncol, False, row_ins, [], params, outs,
                      [dict(shape=s, dtype=F32) for s in state_shapes], [], list(state_shapes), side)
    n_o, n_s = len(outs), len(state_shapes)
    return (res[:n_o], res[n_o:n_o + n_s]) + ((res[n_o + n_s:],) if side else ())


def seq_bwd(name, fn, n_rows, ts, row_ins, params, cots, saved_states=(), din_dtypes=None, ncol=1, din_specs=None,
            side=(), cot_map=None, aux=()):
    n_x, n_p, n_st = len(row_ins), len(params), len(saved_states)
    halo_idx = [k for k, s in enumerate(row_ins) if s["halo"]]
    state_shapes = [a.shape[1:] for a in saved_states]

    def body(xs_all, blks, ps, carries, r):
        xs, cot_vals = xs_all[:n_x], xs_all[n_x:]
        d_state, d_halo = carries[:n_st], carries[n_st:]
        if cot_map is not None:
            cot_vals = cot_map(cot_vals, ps[n_p:])
        (o, _), vjp = jax.vjp(lambda a, b, c: fn(a, b, c), xs, blks, ps[:n_p])
        cot = [c.astype(v.dtype) for c, v in zip(cot_vals, o)]
        dxs, dst, dps = vjp((cot, list(d_state)))
        row_vals, new_halo = [], []
        for k, dx in enumerate(dxs):
            if k in halo_idx:
                hk = halo_idx.index(k)
                rows = dx.shape[0] - HALO
                tail = dx[rows:] + d_halo[hk]
                row_vals.append(jnp.concatenate([dx[HALO:rows], tail], axis=0))
                new_halo.append(dx[:HALO])
            else:
                row_vals.append(dx)
        return row_vals, [], list(dps), list(dst) + new_halo

    din_dtypes = din_dtypes or [F32] * n_x
    douts = []
    for k, s in enumerate(row_ins):
        total, col = (s["w"], 0) if din_specs is None or din_specs[k] is None else din_specs[k]
        douts.append(_out(s["w"], din_dtypes[k], total, col))
    accs = []
    for p in params:
        a = p["arr"]
        if p["col"] is None:
            accs.append(dict(shape=a.shape, total=None, col=None))
        else:
            accs.append(dict(shape=(a.shape[0], p["w"]), total=a.shape[1], col=p["col"]))
    carries = list(state_shapes) + [(HALO, row_ins[k]["w"]) for k in halo_idx]
    res = _block_call(name, body, n_rows, ts, ncol, True, list(row_ins) + list(cots), list(saved_states),
                      list(params) + list(aux), douts, [], accs, carries, side)
    return (res[:n_x], res[n_x:n_x + n_p]) + ((res[n_x + n_p:],) if side else ())


def matmul(name, a, b, mode, out_dtype, tm, tn, tk, side=(), a2=None, b2=None, col_blocks=False):
    side = _Side(side)
    n_side = len(side.jobs)
    if mode == "tn":
        (kk, m), n = a.shape, b.shape[1]
    else:
        (m, kk), n = a.shape, (b.shape[0] if mode == "nt" else b.shape[1])
    k1, n1 = kk, n
    if a2 is not None:
        assert mode != "tn" and b2 is None
        kk += a2.shape[1]
    if b2 is not None:
        assert mode == "tn"
        n += b2.shape[1]
    tm, tn, tk = min(tm, m), min(tn, n), min(tk, kk)
    nk, gm, gn = kk // tk, m // tm, n // tn
    assert m % tm == 0 and n % tn == 0 and kk % tk == 0 and k1 % tk == 0 and n1 % tn == 0, (name, a.shape, b.shape)
    nk1, gn1 = k1 // tk, n1 // tn
    if mode == "tn":
        a_specs = [pl.BlockSpec((tk, tm), lambda i, j, k: (k, i))]
        b_specs = [pl.BlockSpec((tk, tn), lambda i, j, k: (k, jnp.minimum(j, gn1 - 1)))]
        if b2 is not None:
            b_specs.append(pl.BlockSpec((tk, tn), lambda i, j, k: (k, jnp.maximum(j - gn1, 0))))
    else:
        a_specs = [pl.BlockSpec((tm, tk), lambda i, j, k: (i, jnp.minimum(k, nk1 - 1)))]
        if a2 is not None:
            a_specs.append(pl.BlockSpec((tm, tk), lambda i, j, k: (i, jnp.maximum(k - nk1, 0))))
        b_specs = [pl.BlockSpec((tn, tk), lambda i, j, k: (j, k)) if mode == "nt"
                   else pl.BlockSpec((tk, tn), lambda i, j, k: (k, j))]
    ca, cb = {"nn": (1, 0), "nt": (1, 1), "tn": (0, 0)}[mode]
    n_a, n_b = len(a_specs), len(b_specs)

    def kern(*refs):
        a_refs, b_refs = refs[:n_a], refs[n_a:n_a + n_b]
        rest = refs[n_a + n_b:]
        side_in = rest[:n_side]
        o_ref = rest[n_side]
        side_out = rest[1 + n_side:1 + 2 * n_side]
        acc_ref = rest[1 + 2 * n_side]
        side_sems = rest[2 + 2 * n_side:]
        i, j, k = pl.program_id(0), pl.program_id(1), pl.program_id(2)
        if n_side:
            @pl.when((i == 0) & (j == 0) & (k == 0))
            def _():
                side.start(side_in, side_out, side_sems)

        def step(a_ref, b_ref):
            part = lax.dot_general(a_ref[...], b_ref[...], (((ca,), (cb,)), ((), ())), preferred_element_type=F32)
            if nk == 1:
                o_ref[...] = part.astype(o_ref.dtype)
            else:
                @pl.when(k == 0)
                def _():
                    acc_ref[...] = part

                @pl.when(k > 0)
                def _():
                    acc_ref[...] += part

                @pl.when(k == nk - 1)
                def _():
                    o_ref[...] = acc_ref[...].astype(o_ref.dtype)

        if n_a == 2:
            pl.when(k < nk1)(lambda: step(a_refs[0], b_refs[0]))
            pl.when(k >= nk1)(lambda: step(a_refs[1], b_refs[0]))
        elif n_b == 2:
            pl.when(j < gn1)(lambda: step(a_refs[0], b_refs[0]))
            pl.when(j >= gn1)(lambda: step(a_refs[0], b_refs[1]))
        else:
            step(a_refs[0], b_refs[0])

        if n_side:
            @pl.when((i == gm - 1) & (j == gn - 1) & (k == nk - 1))
            def _():
                side.wait(side_in, side_out, side_sems)

    semantics = ("arbitrary",) * 3 if n_side else ("parallel", "parallel", "arbitrary")
    operands = [a] + ([a2] if a2 is not None else []) + [b] + ([b2] if b2 is not None else [])
    res = pl.pallas_call(
        kern,
        name=name,
        grid=(gm, gn, nk),
        in_specs=a_specs + b_specs + side.in_specs,
        out_specs=[pl.BlockSpec((None, tm, tn), lambda i, j, k: (j, i, 0)) if col_blocks
                   else pl.BlockSpec((tm, tn), lambda i, j, k: (i, j))] + side.out_specs,
        out_shape=[jax.ShapeDtypeStruct((gn, m, tn) if col_blocks else (m, n), out_dtype)] + side.out_shape,
        scratch_shapes=[pltpu.VMEM((tm, tn) if nk > 1 else (8, 128), F32)] + side.scratch,
        compiler_params=pltpu.CompilerParams(dimension_semantics=semantics, vmem_limit_bytes=VMEM_LIMIT),
    )(*operands, *side.operands)
    return (res[0], list(res[1:])) if n_side else res[0]


def all_gather(name, x, in_vmem):
    def body(x_ref, out_ref, send_sems, recv_sems, local_sem):
        x_, y_, c_ = _my_pos()
        me, sibling = (x_, y_, c_), (x_, y_, 1 - c_)
        chips = [(1 - x_, y_), (x_, 1 - y_), (1 - x_, 1 - y_)]

        def slot(px, py, pc):
            return out_ref.at[4 * px + 2 * py + pc]

        def copy(k, block, to, src=None):
            return pltpu.make_async_remote_copy(
                src_ref=slot(*block) if src is None else src, dst_ref=slot(*block),
                send_sem=send_sems.at[k], recv_sem=recv_sems.at[k], device_id=to, device_id_type=MESH)

        mine = pltpu.make_async_copy(x_ref, slot(*me), local_sem)
        mine.start()
        first = [copy(0, me, sibling, src=x_ref)]
        first += [copy(1 + j, me, (*chip, c_), src=x_ref) for j, chip in enumerate(chips)]
        for cp in first:
            cp.start()
        passed = [copy(4 + j, (*chip, c_), sibling) for j, chip in enumerate(chips)]
        for j, chip in enumerate(chips):
            copy(1 + j, (*chip, c_), me).wait_recv()
            passed[j].start()
        copy(0, sibling, me).wait_recv()
        for j, chip in enumerate(chips):
            copy(4 + j, (*chip, 1 - c_), me).wait_recv()
        for cp in first + passed:
            cp.wait_send()
        mine.wait()

    space = pltpu.VMEM if in_vmem else pl.ANY
    return pl.pallas_call(
        body,
        name=name,
        out_shape=jax.ShapeDtypeStruct((N_DEV,) + x.shape, x.dtype),
        in_specs=[pl.BlockSpec(memory_space=space)],
        out_specs=pl.BlockSpec(memory_space=space),
        scratch_shapes=[pltpu.SemaphoreType.DMA((7,)), pltpu.SemaphoreType.DMA((7,)), pltpu.SemaphoreType.DMA],
        compiler_params=pltpu.CompilerParams(vmem_limit_bytes=VMEM_LIMIT),
    )(x)


def all_to_all(name, g):
    def body(g_ref, out_ref, send_sems, recv_sems, local_sem):
        x_, y_, c_ = _my_pos()
        me = 4 * x_ + 2 * y_ + c_

        def peer(k):
            fx, fy, fc = (k >> 2) & 1, (k >> 1) & 1, k & 1
            return (1 - x_ if fx else x_, 1 - y_ if fy else y_, 1 - c_ if fc else c_)

        def copy(k):
            px, py, pc = peer(k)
            return pltpu.make_async_remote_copy(
                src_ref=g_ref.at[4 * px + 2 * py + pc], dst_ref=out_ref.at[me],
                send_sem=send_sems.at[k - 1], recv_sem=recv_sems.at[k - 1], device_id=(px, py, pc), device_id_type=MESH)

        def landing(k):
            px, py, pc = peer(k)
            return pltpu.make_async_remote_copy(
                src_ref=g_ref.at[me], dst_ref=out_ref.at[4 * px + 2 * py + pc],
                send_sem=send_sems.at[k - 1], recv_sem=recv_sems.at[k - 1], device_id=(px, py, pc), device_id_type=MESH)

        mine = pltpu.make_async_copy(g_ref.at[me], out_ref.at[me], local_sem)
        mine.start()
        sends = [copy(k) for k in range(1, N_DEV)]
        for cp in sends:
            cp.start()
        for k in range(1, N_DEV):
            landing(k).wait_recv()
        for cp in sends:
            cp.wait_send()
        mine.wait()

    return pl.pallas_call(
        body,
        name=name,
        out_shape=jax.ShapeDtypeStruct(g.shape, g.dtype),
        in_specs=[pl.BlockSpec(memory_space=pl.ANY)],
        out_specs=pl.BlockSpec(memory_space=pl.ANY),
        scratch_shapes=[pltpu.SemaphoreType.DMA((7,)), pltpu.SemaphoreType.DMA((7,)), pltpu.SemaphoreType.DMA],
    )(g)


def sum_blocks(name, g):
    def body(g_ref, o_ref):
        acc = g_ref[0]
        for s in range(1, N_DEV):
            acc = acc + g_ref[s]
        o_ref[...] = acc

    r = g.shape[1]
    tr = r // 4 if r % 32 == 0 else r
    return pl.pallas_call(
        body, name=name, grid=(r // tr,),
        in_specs=[pl.BlockSpec((N_DEV, tr, 128), lambda i: (0, i, 0))],
        out_specs=pl.BlockSpec((tr, 128), lambda i: (i, 0)),
        out_shape=jax.ShapeDtypeStruct((r, 128), F32),
        compiler_params=pltpu.CompilerParams(vmem_limit_bytes=VMEM_LIMIT),
    )(g)


def _adamw_math(w, g, m, v):
    m = ADAM_B1 * m + (1.0 - ADAM_B1) * g
    v = ADAM_B2 * v + (1.0 - ADAM_B2) * (g * g)
    m_hat = m / (1.0 - ADAM_B1 ** ADAM_STEP)
    v_hat = v / (1.0 - ADAM_B2 ** ADAM_STEP)
    delta = -ADAM_LR * (m_hat / (jnp.sqrt(v_hat) + ADAM_EPS) + ADAM_WD * w)
    return delta, m, v


ADAMW_BLOCK_BYTES = 3 << 19


def _row_tile(rows, row_bytes, limit):
    best = 8
    for t in range(8, rows + 1, 8):
        if rows % t == 0 and t * row_bytes <= limit:
            best = t
    return best


def adamw_big(name, w, lands, m, v):
    depth, r, c = w.shape
    outs = None
    for (l, part), land in sorted(lands.items()):
        rows = land.shape[1]
        tr = _row_tile(rows, 4 * (-(-c // 128) * 128), ADAMW_BLOCK_BYTES)
        first = part * rows // tr

        def body(w_ref, l_ref, m_ref, v_ref, *rest):
            g_out, d_out, m_out, v_out = rest[-4:]
            g = l_ref[0].astype(F32)
            for s in range(1, N_DEV):
                g = g + l_ref[s].astype(F32)
            delta, m_new, v_new = _adamw_math(w_ref[...], g, m_ref[...], v_ref[...])
            g_out[...] = g
            d_out[...] = delta
            m_out[...] = m_new
            v_out[...] = v_new

        spec = pl.BlockSpec((None, tr, c), lambda i, l=l, first=first: (l, first + i, 0))
        carried = [] if outs is None else list(outs)
        outs = pl.pallas_call(
            body, name=f"{name}_{l}_{part}", grid=(rows // tr,),
            in_specs=[spec, pl.BlockSpec((N_DEV, tr, c), lambda i: (0, i, 0)), spec, spec]
            + [pl.BlockSpec(memory_space=pl.ANY)] * len(carried),
            out_specs=[spec] * 4,
            out_shape=[jax.ShapeDtypeStruct((depth, r, c), F32)] * 4,
            input_output_aliases={4 + k: k for k in range(len(carried))},
            compiler_params=pltpu.CompilerParams(dimension_semantics=("parallel",), vmem_limit_bytes=VMEM_LIMIT),
        )(w, land, m, v, *carried)
    return outs


def adamw_small(name, ws, gs, ms, vs):
    n = len(ws)

    def body(*refs):
        ins, outs = refs[:4 * n], refs[4 * n:]
        for k in range(n):
            delta, m_new, v_new = _adamw_math(ins[k][...], ins[n + k][...], ins[2 * n + k][...], ins[3 * n + k][...])
            outs[k][...] = delta
            outs[n + k][...] = m_new
            outs[2 * n + k][...] = v_new

    res = pl.pallas_call(
        body, name=name,
        out_shape=[jax.ShapeDtypeStruct(w.shape, F32) for w in ws] * 3,
        compiler_params=pltpu.CompilerParams(vmem_limit_bytes=VMEM_LIMIT),
    )(*ws, *gs, *ms, *vs)
    return res[:n], res[n:2 * n], res[2 * n:]


def cast_bf16(name, w):
    depth, r, c = w.shape
    tr = _row_tile(r, 4 * (-(-c // 128) * 128), ADAMW_BLOCK_BYTES)

    def body(w_ref, o_ref):
        o_ref[...] = w_ref[...].astype(BF16)

    spec = pl.BlockSpec((None, tr, c), lambda l, i: (l, i, 0))
    return pl.pallas_call(body, name=name, grid=(depth, r // tr), in_specs=[spec], out_specs=spec,
                          out_shape=jax.ShapeDtypeStruct((depth, r, c), BF16),
                          compiler_params=pltpu.CompilerParams(dimension_semantics=("parallel", "parallel")))(w)


def _rows_of(shape):
    return -(-math.prod(shape) // 128)


def _pack(arrs):
    pieces = []
    for a in arrs:
        flat = a.reshape(-1).astype(F32)
        pieces.append(jnp.pad(flat, (0, (-flat.shape[0]) % 128)).reshape(-1, 128))
    rows = sum(p.shape[0] for p in pieces)
    if rows % 8:
        pieces.append(jnp.zeros((8 - rows % 8, 128), F32))
    return jnp.concatenate(pieces, axis=0)


def _unpack(packed, shapes, lead=()):
    out, r0 = [], 0
    for s in shapes:
        rows, n = _rows_of(s), math.prod(s)
        piece = packed[..., r0:r0 + rows, :].reshape(lead + (rows * 128,))
        out.append(piece[..., :n].reshape(lead + tuple(s)))
        r0 += rows
    return out


def _block_diag(w):
    h, d, _ = w.shape
    eye = jnp.eye(h, dtype=w.dtype)
    return (eye[:, None, :, None] * w[:, :, None, :]).reshape(h * d, h * d)


def _block_diag_grad(g, h):
    d = g.shape[0] // h
    eye = jnp.eye(h, dtype=g.dtype)
    return jnp.sum(g.reshape(h, d, h, d) * eye[:, None, :, None], axis=2)


def _layer_params(wt, l):
    gp = jnp.pad(jnp.stack([wt["gdn_a_log"][l], wt["gdn_dt_bias"][l]]), ((0, 6), (4, 128 - 4 - GDN_HEADS)))
    d_ffh = wt["ffn_conv_w"].shape[-1] // 2
    return dict(
        pre_mix=wt["pre_mix_norm"][l][None], post_mix=wt["post_mix_norm"][l][None],
        pre_ffn=wt["pre_ffn_norm"][l][None], post_ffn=wt["post_ffn_norm"][l][None],
        gdn_cw=wt["gdn_conv_w"][l], gdn_gp=gp, gdn_nw=wt["gdn_norm_w"][l][None],
        lru_cw=wt["lru_conv_w"][l], lru_cb=wt["lru_conv_b"][l][None],
        lru_wa=_block_diag(wt["lru_wa"][l]), lru_ba=wt["lru_ba"][l].reshape(1, -1),
        lru_wx=_block_diag(wt["lru_wx"][l]), lru_bx=wt["lru_bx"][l].reshape(1, -1),
        lru_lam=wt["lru_lambda"][l][None], gw0=wt["grp_norm_w"][l, 0][None], gw1=wt["grp_norm_w"][l, 1][None],
        gw2=wt["grp_norm_w"][l, 2][None],
        sgu_lnw=wt["sgu_ln_w"][l][None], sgu_lnb=wt["sgu_ln_b"][l][None],
        sgu_ws=wt["sgu_ws"][l].reshape(SGU_GROUPS * SGU_CHUNK, SGU_CHUNK),
        sgu_bt=jnp.pad(wt["sgu_b"][l].T, ((0, 0), (0, 128 - SGU_GROUPS))),
        sc_cw=wt["sconv_w"][l],
        ffn_cw=wt["ffn_conv_w"][l], ffn_cb=wt["ffn_conv_b"][l][None], d_ffh=d_ffh,
    )


TS_ROW = 256
TS_GDN = 256
TS_FFN = 512
TS_FFN_BWD = 256
TC_FFN = 512
FFN_LANES = 128


def _mixers_fwd(l, p, lp, n, side, side_lru):
    qkv = _rin(p, 3 * D_G, 0, halo=True)
    z = _rin(p, D_G, 3)
    ba = _rin(p, 128, BA_COL // 128)
    gdn_ps = [_par(lp["gdn_cw"]), _par(lp["gdn_gp"]), _par(lp["gdn_nw"])]
    res = seq_fwd(f"gdn_fwd_{l}", fn_gdn, n, TS_GDN, [qkv, z, ba], gdn_ps, [_out(D_G, BF16)],
                  state_shapes=[(GDN_HEADS * GDN_DIM, GDN_DIM)], side=side)
    (y_a,), (gdn_st,), side_res = res if side else res + ([],)
    lru_x = _rin(p, D_G, 4, halo=True)
    lru_gate = _rin(p, D_G, 5)
    lru_ps = [_par(lp[k]) for k in ("lru_cw", "lru_cb", "lru_wa", "lru_ba", "lru_wx", "lru_bx", "lru_lam", "gw0")]
    res = seq_fwd(f"lru_fwd_{l}", fn_lru, n, TS_ROW, [lru_x, lru_gate], lru_ps, [_out(D_G, BF16)],
                  state_shapes=[(1, D_G)], side=side_lru)
    (y_b,), (lru_st,), side_res_lru = res if side_lru else res + ([],)
    uv = _rin(p, 2 * D_G, 3)
    sgu_ps = [_par(lp[k]) for k in ("sgu_lnw", "sgu_lnb", "sgu_ws", "sgu_bt", "gw1")]
    (y_c,), _ = seq_fwd(f"sgu_fwd_{l}", fn_sgu, n, TS_ROW, [uv], sgu_ps, [_out(D_G, BF16)])
    sc = [_rin(p, D_G, 8), _rin(p, D_G, 9, halo=True), _rin(p, D_G, 10, halo=True)]
    sc_ps = [_par(lp["sc_cw"]), _par(lp["gw2"])]
    (y_d,), _ = seq_fwd(f"sconv_fwd_{l}", fn_sconv, n, TS_ROW, sc, sc_ps, [_out(D_G, BF16)])
    ins = dict(gdn=([qkv, z, ba], gdn_ps, [gdn_st]), lru=([lru_x, lru_gate], lru_ps, [lru_st]),
               sgu=([uv], sgu_ps, []), sc=(sc, sc_ps, []))
    return jnp.concatenate([y_a, y_b, y_c, y_d], axis=1), ins, side_res, side_res_lru


def _mixers_bwd(l, dymix, ins, n, side):
    cot = lambda g: [_rin(dymix, D_G, g)]
    xs, ps, st = ins["gdn"]
    res = seq_bwd(f"gdn_bwd_{l}", fn_gdn, n, TS_GDN, xs, ps, cot(0), st, [BF16, BF16, BF16], side=side)
    (dqkv, dz, dba), g_gdn, side_res = res if side else res + ([],)
    xs, ps, st = ins["lru"]
    (dlx, dlg), g_lru = seq_bwd(f"lru_bwd_{l}", fn_lru, n, TS_ROW, xs, ps, cot(1), st, [BF16, BF16])
    xs, ps, st = ins["sgu"]
    (duv,), g_sgu = seq_bwd(f"sgu_bwd_{l}", fn_sgu, n, TS_ROW, xs, ps, cot(2), st, [BF16])
    xs, ps, st = ins["sc"]
    (dsb, dsc, dsh), g_sc = seq_bwd(f"sconv_bwd_{l}", fn_sconv, n, TS_ROW, xs, ps, cot(3), st, [BF16, BF16, BF16])
    dp = jnp.concatenate([dqkv, dz, dlx, dlg, duv, dsb, dsc, dsh, dba], axis=1)
    return dp, dict(gdn=g_gdn, lru=g_lru, sgu=g_sgu, sc=g_sc), side_res


def _ffn_ops(hid, lp):
    d_ffh = lp["d_ffh"]
    off = d_ffh // TC_FFN
    xs = [_rin(hid, TC_FFN, lambda j: j, halo=True), _rin(hid, TC_FFN, lambda j: j + off, halo=True)]
    ps = [_par(lp["ffn_cw"], TC_FFN, lambda j: j), _par(lp["ffn_cw"], TC_FFN, lambda j: j + off),
          _par(lp["ffn_cb"], TC_FFN, lambda j: j), _par(lp["ffn_cb"], TC_FFN, lambda j: j + off)]
    return xs, ps, d_ffh


_FROM_BLOCKS = dict(
    w_in=lambda b: _regroup_w_in(b.transpose(1, 0, 2).reshape(b.shape[1], -1)),
    ffn_up=lambda b: b.transpose(1, 0, 2).reshape(b.shape[1], -1),
    w_out=lambda b: b.reshape(-1, b.shape[2]),
    ffn_down=lambda b: b.reshape(-1, b.shape[2]),
)
_TO_BLOCKS = dict(
    w_in=lambda g: _ungroup_w_in(g).reshape(g.shape[0], N_DEV, -1).transpose(1, 0, 2),
    ffn_up=lambda g: g,
    w_out=lambda g: g.reshape(N_DEV, -1, g.shape[1]),
    ffn_down=lambda g: g.reshape(N_DEV, -1, g.shape[1]),
)


class _Traffic:
    PARTS = dict(w_in=1, w_out=1, ffn_up=2, ffn_down=1)

    def __init__(self, whole=None, shards=None):
        self.whole = dict(whole or {})
        self.shards = shards
        self.gathered = {}
        self.pending = {}
        self.landed = {}

    def _rows(self, key):
        name, l, part = key
        rows = self.shards[name].shape[1] // self.PARTS[name]
        return slice(part * rows, (part + 1) * rows)

    def jobs(self, gather=(), exchange=()):
        if self.shards is None:
            return [], []
        keys = [("gather", k) for k in gather if k not in self.gathered and k[:2] not in self.whole]
        keys += [("exchange", k) for k in exchange if k in self.pending]
        jobs = [(kind, self.shards[k[0]][k[1]][self._rows(k)] if kind == "gather" else self.pending[k])
                for kind, k in keys]
        return jobs, keys

    def done(self, keys, results):
        for (kind, k), r in zip(keys, results):
            if kind == "gather":
                self.gathered[k] = r
            else:
                self.landed[k] = r
                del self.pending[k]

    def weight(self, name, l):
        if (name, l) not in self.whole:
            parts = []
            for part in range(self.PARTS[name]):
                k = (name, l, part)
                if k not in self.gathered:
                    self.gathered[k] = all_gather(f"gather_{name}_{l}_{part}", self.shards[name][l][self._rows(k)], False)
                parts.append(self.gathered[k])
            blocks = parts[0] if len(parts) == 1 else jnp.concatenate(parts, axis=1)
            self.whole[(name, l)] = _FROM_BLOCKS[name](blocks)
        return self.whole[(name, l)]

    def grad(self, name, l, g):
        if self.shards is None:
            self.landed[(name, l)] = g
            return
        blocks = _TO_BLOCKS[name](g)
        for part in range(self.PARTS[name]):
            k = (name, l, part)
            self.pending[k] = blocks[:, self._rows(k)]

    def flush(self):
        for (name, l, part), blocks in list(self.pending.items()):
            self.landed[(name, l, part)] = all_to_all(f"exchange_{name}_{l}_{part}", blocks)
            del self.pending[(name, l, part)]


def local_step(x, target, wt, tr):
    n, d = x.shape
    depth = wt["pre_mix_norm"].shape[0]
    lps = [_layer_params(wt, l) for l in range(depth)]
    saved = []
    xin = x

    def mm(name, a, b, mode, dtype, tm, tn, tk, gather=(), exchange=(), **split):
        jobs, keys = tr.jobs(gather, exchange)
        if not jobs:
            return matmul(name, a, b, mode, dtype, tm, tn, tk, **split)
        out, res = matmul(name, a, b, mode, dtype, tm, tn, tk, side=jobs, **split)
        tr.done(keys, res)
        return out

    (h,), _ = seq_fwd("norm_fwd", fn_norm, n, TS_ROW, [_rin(x)], [_par(lps[0]["pre_mix"])], [_out(d, BF16)])
    dx_last = loss = None
    for l in range(depth):
        lp = lps[l]
        p = mm(f"w_in_fwd_{l}", h, tr.weight("w_in", l), "nn", F32, 2048, N_INP // 5, d,
               gather=[("ffn_up", l, 0)])
        jobs, keys = tr.jobs(gather=[("ffn_up", l, 1)])
        jobs_lru, keys_lru = tr.jobs(gather=[("w_out", l, 0)])
        ymix, mix_ins, res, res_lru = _mixers_fwd(l, p, lp, n, jobs, jobs_lru)
        tr.done(keys, res)
        tr.done(keys_lru, res_lru)
        y = mm(f"w_out_fwd_{l}", ymix, tr.weight("w_out", l), "nn", BF16, 2048, 1024, d)
        res_ps = [_par(lp["post_mix"]), _par(lp["pre_ffn"])]
        (x1, h2), _ = seq_fwd(f"res_mix_fwd_{l}", fn_res, n, TS_ROW, [_rin(xin), _rin(y)], res_ps,
                              [_out(d, F32), _out(d, BF16)])
        nxt = l + 1 < depth
        hid = mm(f"ffn_up_fwd_{l}", h2, tr.weight("ffn_up", l), "nn", F32, 2048, 1024, d,
                 gather=[("ffn_down", l, 0)] + ([("w_out", l + 1, 0)] if nxt else []))
        f_xs, f_ps, d_ffh = _ffn_ops(hid, lp)
        jobs, keys = tr.jobs(gather=[("w_in", l + 1, 0)] if nxt else [])
        res = seq_fwd(f"ffn_act_fwd_{l}", fn_ffn, n, TS_FFN, f_xs, f_ps,
                      [_out(TC_FFN, BF16, d_ffh, lambda j: j)], ncol=d_ffh // TC_FFN, side=jobs)
        (act,) = res[0]
        tr.done(keys, res[2] if jobs else [])
        yf = mm(f"ffn_down_fwd_{l}", act, tr.weight("ffn_down", l), "nn", BF16, 1024, 1024, d_ffh // 2)
        rec = dict(x=xin, h=h, mix_ins=mix_ins, ymix=ymix, y=y, x1=x1, h2=h2, f_xs=f_xs, f_ps=f_ps, act=act, yf=yf)
        if l + 1 < depth:
            ps = [_par(lp["post_ffn"]), _par(lps[l + 1]["pre_mix"])]
            (x2, h), _ = seq_fwd(f"res_ffn_fwd_{l}", fn_res, n, TS_ROW, [_rin(x1), _rin(yf)], ps,
                                 [_out(d, F32), _out(d, BF16)])
            rec["res_ffn_ps"] = ps
            xin = x2
        else:
            def body(xs, blks, ps, carries, r):
                x1_, yf_, t_ = xs
                e = x1_ + _rms(yf_.astype(F32), ps[0]) - t_
                part = 0.5 * jnp.sum(jnp.mean(e * e, axis=-1, keepdims=True), axis=0, keepdims=True)
                return [e * (1.0 / d)], [], [jnp.broadcast_to(part, (8, 128))], []

            ps = [_par(lp["post_ffn"])]
            dx_last, loss = _block_call("loss_fwd", body, n, TS_ROW, 1, False, [_rin(x1), _rin(yf), _rin(target)],
                                        [], ps, [_out(d, F32)], [], [dict(shape=(8, 128), total=None, col=None)], [])
            rec["res_ffn_ps"] = ps
        saved.append(rec)

    grads = {}
    dx2, dh_next = dx_last, None
    for l in reversed(range(depth)):
        rec, lp = saved[l], lps[l]
        d_ffh = lp["d_ffh"]
        g = {}
        if dh_next is None:
            (dx1, dyf), (g["post_ffn"],) = seq_bwd(f"res_ffn_bwd_{l}", fn_res_last, n, TS_ROW,
                                                   [_rin(rec["x1"]), _rin(rec["yf"])], rec["res_ffn_ps"], [_rin(dx2)],
                                                   din_dtypes=[F32, BF16])
        else:
            (dx1, dyf), (g["post_ffn"], g_next_pre) = seq_bwd(
                f"res_ffn_bwd_{l}", fn_res, n, TS_ROW, [_rin(rec["x1"]), _rin(rec["yf"])], rec["res_ffn_ps"],
                [_rin(dx2), _rin(dh_next)], din_dtypes=[F32, BF16])
            grads[l + 1]["pre_mix"] = g_next_pre
        tr.grad("ffn_down", l, mm(f"ffn_down_dw_{l}", rec["act"], dyf, "tn", BF16, d_ffh // 4, 1024, 2048))

        def dact(cot_blocks, aux_blocks):
            return [_dot(cot_blocks[0], aux_blocks[0], 1, 1).astype(BF16)]

        (dhg, dhv), (g_cwg, g_cwv, g_cbg, g_cbv) = seq_bwd(
            f"ffn_act_bwd_{l}", fn_ffn, n, TS_FFN_BWD, rec["f_xs"], rec["f_ps"], [_rin(dyf)],
            din_dtypes=[BF16, BF16], ncol=d_ffh // TC_FFN, din_specs=[(d_ffh, lambda j: j), (d_ffh, lambda j: j)],
            cot_map=dact, aux=[_par(tr.weight("ffn_down", l), TC_FFN, row=lambda j: j)])
        g["ffn_cw"] = jnp.concatenate([g_cwg[:, :d_ffh], g_cwv[:, d_ffh:]], axis=1)
        g["ffn_cb"] = jnp.concatenate([g_cbg[:, :d_ffh], g_cbv[:, d_ffh:]], axis=1)
        dh2 = mm(f"ffn_up_dx_{l}", dhg, tr.weight("ffn_up", l), "nt", BF16, 1024, 1024, d_ffh // 2,
                 exchange=[("ffn_down", l, 0)], a2=dhv)
        tr.grad("ffn_up", l, mm(f"ffn_up_dw_{l}", rec["h2"], dhg, "tn", BF16, 1024, 2 * d_ffh // N_DEV, 2048, b2=dhv,
                                col_blocks=True))
        (dx, dy), (g["post_mix"], g["pre_ffn"]) = seq_bwd(
            f"res_mix_bwd_{l}", fn_res, n, TS_ROW, [_rin(rec["x"]), _rin(rec["y"])],
            [_par(lp["post_mix"]), _par(lp["pre_ffn"])], [_rin(dx1), _rin(dh2)], din_dtypes=[F32, BF16])
        dymix = mm(f"w_out_dx_{l}", dy, tr.weight("w_out", l), "nt", BF16, 2048, 1024, d)
        tr.grad("w_out", l, mm(f"w_out_dw_{l}", rec["ymix"], dy, "tn", BF16, 1024, 1024, 2048))
        jobs, keys = tr.jobs(exchange=[("ffn_up", l, 0), ("w_out", l, 0)])
        dp, g["mix"], res = _mixers_bwd(l, dymix, rec["mix_ins"], n, jobs)
        tr.done(keys, res)
        tr.grad("w_in", l, mm(f"w_in_dw_{l}", rec["h"], dp, "tn", BF16, 1024, N_INP // 5, 2048,
                              exchange=[("ffn_up", l, 1)]))
        dh = mm(f"w_in_dx_{l}", dp, tr.weight("w_in", l), "nt", BF16, 1024, 1024, N_INP // 3,
                exchange=[("w_in", l, 0)])
        grads[l] = g
        dx2, dh_next = dx, dh
    (grad_x,), (g_pre0,) = seq_bwd("norm_bwd", fn_norm_keep, n, TS_ROW, [_rin(x)], [_par(lps[0]["pre_mix"])],
                                   [_rin(dh_next), _rin(dx2)])
    grads[0]["pre_mix"] = g_pre0
    tr.flush()
    return loss[0, 0], grad_x, _name_grads(grads, depth)


def _name_grads(grads, depth):
    per = {k: [] for k in SMALL}
    for l in range(depth):
        g = grads[l]
        m = g["mix"]
        cw, gp, nw = m["gdn"]
        lcw, lcb, lwa, lba, lwx, lbx, llam, gw0 = m["lru"]
        lnw, lnb, ws, bst, gw1 = m["sgu"]
        scw, gw2 = m["sc"]
        per["pre_mix_norm"].append(g["pre_mix"][0])
        per["gdn_conv_w"].append(cw)
        per["gdn_a_log"].append(gp[0, 4:8])
        per["gdn_dt_bias"].append(gp[1, 4:8])
        per["gdn_norm_w"].append(nw[0])
        per["lru_conv_w"].append(lcw)
        per["lru_conv_b"].append(lcb[0])
        per["lru_wa"].append(_block_diag_grad(lwa, LRU_BLOCKS))
        per["lru_ba"].append(lba.reshape(LRU_BLOCKS, -1))
        per["lru_wx"].append(_block_diag_grad(lwx, LRU_BLOCKS))
        per["lru_bx"].append(lbx.reshape(LRU_BLOCKS, -1))
        per["lru_lambda"].append(llam[0])
        per["sgu_ln_w"].append(lnw[0])
        per["sgu_ln_b"].append(lnb[0])
        per["sgu_ws"].append(ws.reshape(SGU_GROUPS, SGU_CHUNK, SGU_CHUNK))
        per["sgu_b"].append(bst[:, :SGU_GROUPS].T)
        per["sconv_w"].append(scw)
        per["grp_norm_w"].append(jnp.concatenate([gw0, gw1, gw2], axis=0))
        per["post_mix_norm"].append(g["post_mix"][0])
        per["pre_ffn_norm"].append(g["pre_ffn"][0])
        per["ffn_conv_w"].append(g["ffn_cw"])
        per["ffn_conv_b"].append(g["ffn_cb"][0])
        per["post_ffn_norm"].append(g["post_ffn"][0])
    return {k: jnp.stack(v) for k, v in per.items()}


def _regroup_w_in(w):
    pad = jnp.zeros(w.shape[:-1] + (N_INP - N_IN,), w.dtype)
    return jnp.concatenate([w[..., :2048], w[..., 2056:], w[..., 2048:2056], pad], axis=-1)


def _ungroup_w_in(g):
    return jnp.concatenate([g[..., :2048], g[..., BA_COL:BA_COL + 8], g[..., 2048:BA_COL]], axis=-1)


def kernel(x, pre_mix_norm, w_in, gdn_conv_w, gdn_a_log, gdn_dt_bias, gdn_norm_w, lru_conv_w, lru_conv_b, lru_wa, lru_ba, lru_wx, lru_bx, lru_lambda, sgu_ln_w, sgu_ln_b, sgu_ws, sgu_b, sconv_w, grp_norm_w, w_out, post_mix_norm, pre_ffn_norm, ffn_up, ffn_conv_w, ffn_conv_b, ffn_down, post_ffn_norm, loss_target, m_pre_mix_norm, m_w_in, m_gdn_conv_w, m_gdn_a_log, m_gdn_dt_bias, m_gdn_norm_w, m_lru_conv_w, m_lru_conv_b, m_lru_wa, m_lru_ba, m_lru_wx, m_lru_bx, m_lru_lambda, m_sgu_ln_w, m_sgu_ln_b, m_sgu_ws, m_sgu_b, m_sconv_w, m_grp_norm_w, m_w_out, m_post_mix_norm, m_pre_ffn_norm, m_ffn_up, m_ffn_conv_w, m_ffn_conv_b, m_ffn_down, m_post_ffn_norm, v_pre_mix_norm, v_w_in, v_gdn_conv_w, v_gdn_a_log, v_gdn_dt_bias, v_gdn_norm_w, v_lru_conv_w, v_lru_conv_b, v_lru_wa, v_lru_ba, v_lru_wx, v_lru_bx, v_lru_lambda, v_sgu_ln_w, v_sgu_ln_b, v_sgu_ws, v_sgu_b, v_sconv_w, v_grp_norm_w, v_w_out, v_post_mix_norm, v_pre_ffn_norm, v_ffn_up, v_ffn_conv_w, v_ffn_conv_b, v_ffn_down, v_post_ffn_norm):
    args = locals()
    w_loc = {k: args[k] for k in WEIGHTS}
    m_loc = {k: args["m_" + k] for k in WEIGHTS}
    v_loc = {k: args["v_" + k] for k in WEIGHTS}
    depth = pre_mix_norm.shape[0]
    x_, y_, c_ = _my_pos()
    me = 4 * x_ + 2 * y_ + c_

    tr = _Traffic(shards={name: cast_bf16(f"cast_{name}", w_loc[name]) for name in BIG})
    wt = {k: w_loc[k] for k in SMALL}
    shard_shapes = [w_loc[k].shape for k in SHARDED_SMALL]
    gathered = all_gather("gather_small", _pack([w_loc[k] for k in SHARDED_SMALL]), True)
    for k, a in zip(SHARDED_SMALL, _unpack(gathered, shard_shapes, lead=(N_DEV,))):
        a = jnp.moveaxis(a, 0, -2)
        wt[k] = a.reshape(a.shape[:-2] + (-1,))

    loss_part, grad_x, g_full = local_step(x[0], loss_target[0], wt, tr)
    loss = lax.psum(loss_part, ("x", "y", "c"))

    outs_g, outs_d, outs_m, outs_v = {}, {}, {}, {}
    for name in BIG:
        lands = {(l, part): a for (n_, l, part), a in tr.landed.items() if n_ == name}
        outs_g[name], outs_d[name], outs_m[name], outs_v[name] = adamw_big(
            f"adamw_{name}", w_loc[name], lands, m_loc[name], v_loc[name])

    full_shapes = [g_full[k].shape for k in SMALL]
    g_all = all_gather("gather_small_grads", _pack([g_full[k] for k in SMALL]), True)
    g_sum = _unpack(sum_blocks("sum_small_grads", g_all), full_shapes)
    g_small = {}
    for k, g in zip(SMALL, g_sum):
        if k in SHARDED_SMALL:
            w = w_loc[k].shape[-1]
            g = lax.dynamic_slice_in_dim(g, me * w, w, axis=g.ndim - 1)
        g_small[k] = g
    d_s, m_s, v_s = adamw_small("adamw_small", [w_loc[k] for k in SMALL], [g_small[k] for k in SMALL],
                                [m_loc[k] for k in SMALL], [v_loc[k] for k in SMALL])
    for k_i, k in enumerate(SMALL):
        outs_g[k], outs_d[k], outs_m[k], outs_v[k] = g_small[k], d_s[k_i], m_s[k_i], v_s[k_i]

    return (loss, grad_x[None], *[outs_g[k] for k in WEIGHTS], *[outs_d[k] for k in WEIGHTS],
            *[outs_m[k] for k in WEIGHTS], *[outs_v[k] for k in WEIGHTS])
```
